```python
import jax, jax.numpy as jnp
from jax import lax
import numpy as np

D_MODEL = 1024
BATCH = 16
SEQ = 256
DEPTH = 2
DEC_BATCH = 2
DEC_SEQ = 1024
PAST_LEN = 256

GRID_W = 64
N_MIXERS = 2
BRANCH_WIDTH = D_MODEL
N_FOURIER_GROUPS = 4
HEAD_DIM = 64
N_HEADS = BRANCH_WIDTH // HEAD_DIM
N_KV_HEADS = 4
GQA_GROUP = N_HEADS // N_KV_HEADS
KV_WIDTH = N_KV_HEADS * HEAD_DIM
WINDOW = 128
BLOCK = 128
ROPE_THETA = 10000.0
EPS = 1e-6
NEG_INF = -1e30

kernel_name = "hybrid_fnet_swa_dit_step"


def rms_norm(x, w):
    xf = x.astype(jnp.float32)
    y = xf * lax.rsqrt(jnp.mean(xf * xf, axis=-1, keepdims=True) + EPS)
    return (y * w.astype(jnp.float32)).astype(x.dtype)


def modulation(cond, w_mod, b_mod):
    m = jax.nn.silu(cond) @ w_mod + b_mod
    shift, scale, gate = jnp.split(m[:, None, :], 3, axis=-1)
    return shift, scale, gate


def modulated_norm(x, cond, norm_w, w_mod, b_mod):
    shift, scale, gate = modulation(cond, w_mod, b_mod)
    return rms_norm(x, norm_w) * (1 + scale) + shift, gate


def fourier_mix(u):
    b, s, e = u.shape
    ug = u.astype(jnp.float32).reshape(b, s, N_FOURIER_GROUPS, e // N_FOURIER_GROUPS)
    f = jnp.fft.fft2(ug, axes=(1, 3), norm="ortho")
    return jnp.real(f).reshape(b, s, e).astype(u.dtype)


def fourier_layer(x, cond, norm_w, w_mod, b_mod, w_in, w_out):
    h, gate = modulated_norm(x, cond, norm_w, w_mod, b_mod)
    u, z = jnp.split(h @ w_in, 2, axis=-1)
    y = fourier_mix(u) * jax.nn.silu(z)
    return x + gate * (y @ w_out)


def attn_project(h, w_in, q_norm_w, k_norm_w):
    b, s, _ = h.shape
    q, k, v, z = jnp.split(h @ w_in, [BRANCH_WIDTH, BRANCH_WIDTH + KV_WIDTH,
                                      BRANCH_WIDTH + 2 * KV_WIDTH], axis=-1)
    q = rms_norm(q.reshape(b, s, N_KV_HEADS, GQA_GROUP, HEAD_DIM), q_norm_w)
    k = rms_norm(k.reshape(b, s, N_KV_HEADS, HEAD_DIM), k_norm_w)
    v = v.reshape(b, s, N_KV_HEADS, HEAD_DIM)
    return q, k, v, z


def axial_rope_tables(s):
    rows = s // GRID_W
    row = jnp.repeat(jnp.arange(rows, dtype=jnp.float32), GRID_W)
    col = jnp.tile(jnp.arange(GRID_W, dtype=jnp.float32), rows)
    n_freq = HEAD_DIM // 4
    inv = ROPE_THETA ** (-jnp.arange(n_freq, dtype=jnp.float32) / n_freq)
    ang = jnp.concatenate([row[:, None] * inv, col[:, None] * inv], axis=-1)
    return jnp.cos(ang), jnp.sin(ang)


def apply_rope(x, cos, sin):
    shp = (cos.shape[0],) + (1,) * (x.ndim - 3) + (cos.shape[1],)
    c, s_ = cos.reshape(shp), sin.reshape(shp)
    xf = x.astype(jnp.float32)
    x1, x2 = jnp.split(xf, 2, axis=-1)
    return jnp.concatenate([x1 * c - x2 * s_, x1 * s_ + x2 * c], axis=-1).astype(x.dtype)


def sink_softmax(scores, sink):
    s_col = jnp.broadcast_to(sink.astype(jnp.float32).reshape(N_KV_HEADS, GQA_GROUP, 1, 1),
                             scores.shape[:-1] + (1,))
    p = jax.nn.softmax(jnp.concatenate([scores, s_col], axis=-1), axis=-1)
    return p[..., :-1]


def context_attention(q, k, v, sink):
    b, s = q.shape[:2]
    nq = s // BLOCK
    scale = HEAD_DIM ** -0.5
    qb = jnp.moveaxis(q.reshape(b, nq, BLOCK, N_KV_HEADS, GQA_GROUP, HEAD_DIM), 1, 0)
    kf, vf = k.astype(jnp.float32), v.astype(jnp.float32)

    def one_block(qi):
        sc = jnp.einsum("bqkgd,bjkd->bkgqj", qi.astype(jnp.float32), kf) * scale
        p = sink_softmax(sc, sink)
        return jnp.einsum("bkgqj,bjkd->bqkgd", p, vf)

    o = lax.map(one_block, qb)
    return jnp.moveaxis(o, 0, 1).reshape(b, s, BRANCH_WIDTH).astype(q.dtype)


def latent_attention(q, k, v, k_ctx, v_ctx, sink):
    b, s = q.shape[:2]
    nb = s // BLOCK
    scale = HEAD_DIM ** -0.5
    pad = ((0, 0), (BLOCK, BLOCK), (0, 0), (0, 0))
    kp = jnp.pad(k.astype(jnp.float32), pad).reshape(b, nb + 2, BLOCK, N_KV_HEADS, HEAD_DIM)
    vp = jnp.pad(v.astype(jnp.float32), pad).reshape(b, nb + 2, BLOCK, N_KV_HEADS, HEAD_DIM)
    kb = jnp.concatenate([kp[:, :-2], kp[:, 1:-1], kp[:, 2:]], axis=2)
    vb = jnp.concatenate([vp[:, :-2], vp[:, 1:-1], vp[:, 2:]], axis=2)
    qb = q.astype(jnp.float32).reshape(b, nb, BLOCK, N_KV_HEADS, GQA_GROUP, HEAD_DIM)
    kc, vc = k_ctx.astype(jnp.float32), v_ctx.astype(jnp.float32)

    s_loc = jnp.einsum("bnqkgd,bnjkd->bnkgqj", qb, kb) * scale
    n_i = jnp.arange(nb)[:, None, None]
    q_i = jnp.arange(BLOCK)[None, :, None]
    k_j = jnp.arange(3 * BLOCK)[None, None, :]
    kpos = n_i * BLOCK + k_j - BLOCK
    qpos = n_i * BLOCK + q_i
    valid = (jnp.abs(kpos - qpos) <= WINDOW) & (kpos >= 0) & (kpos < s)
    s_loc = jnp.where(valid[None, :, None, None], s_loc, NEG_INF)
    s_ctx = jnp.einsum("bnqkgd,bjkd->bnkgqj", qb, kc) * scale

    p = sink_softmax(jnp.concatenate([s_loc, s_ctx], axis=-1), sink)
    p_loc, p_ctx = p[..., :3 * BLOCK], p[..., 3 * BLOCK:]
    o = (jnp.einsum("bnkgqj,bnjkd->bnqkgd", p_loc, vb)
         + jnp.einsum("bnkgqj,bjkd->bnqkgd", p_ctx, vc))
    return o.reshape(b, s, BRANCH_WIDTH).astype(q.dtype)


def attn_layer_context(x, cond, norm_w, w_mod, b_mod, w_in, q_norm_w, k_norm_w, sink, w_out):
    h, gate = modulated_norm(x, cond, norm_w, w_mod, b_mod)
    q, k, v, z = attn_project(h, w_in, q_norm_w, k_norm_w)
    o = context_attention(q, k, v, sink)
    return x + gate * ((o * jax.nn.silu(z)) @ w_out), k, v


def attn_layer_latent(x, cond, k_ctx, v_ctx, norm_w, w_mod, b_mod, w_in, q_norm_w, k_norm_w,
                      sink, w_out):
    h, gate = modulated_norm(x, cond, norm_w, w_mod, b_mod)
    q, k, v, z = attn_project(h, w_in, q_norm_w, k_norm_w)
    cos, sin = axial_rope_tables(x.shape[1])
    q, k = apply_rope(q, cos, sin), apply_rope(k, cos, sin)
    o = latent_attention(q, k, v, k_ctx, v_ctx, sink)
    return x + gate * ((o * jax.nn.silu(z)) @ w_out)


def setup_inputs(seed: int = 0) -> dict:
    key = jax.random.key(seed)
    ks = jax.random.split(key, 24)
    f32 = jnp.float32
    D, E = D_MODEL, BRANCH_WIDTH
    nrm = lambda k, shape, s: jax.random.normal(k, shape, f32) * s
    return {
        "x_prompt": nrm(ks[0], (BATCH, SEQ, D), 1.0),
        "x_sample": nrm(ks[1], (DEC_BATCH, DEC_SEQ, D), 1.0),
        "cache_k_l1": nrm(ks[2], (DEC_BATCH, PAST_LEN, N_KV_HEADS, HEAD_DIM), 1.0),
        "cache_v_l1": nrm(ks[3], (DEC_BATCH, PAST_LEN, N_KV_HEADS, HEAD_DIM), 1.0),
        "c": nrm(ks[4], (DEC_BATCH, D), 1.0),
        "c_ctx": nrm(ks[5], (D,), 1.0),
        "norm_w_l0": 1.0 + nrm(ks[6], (D,), 0.02),
        "w_mod_l0": nrm(ks[7], (D, 3 * D), 0.5 * D ** -0.5),
        "b_mod_l0": nrm(ks[8], (3 * D,), 0.02),
        "w_in_l0": nrm(ks[9], (D, 2 * E), D ** -0.5),
        "w_out_l0": nrm(ks[10], (E, D), E ** -0.5),
        "norm_w_l1": 1.0 + nrm(ks[11], (D,), 0.02),
        "w_mod_l1": nrm(ks[12], (D, 3 * D), 0.5 * D ** -0.5),
        "b_mod_l1": nrm(ks[13], (3 * D,), 0.02),
        "w_in_l1": nrm(ks[14], (D, 2 * E + 2 * KV_WIDTH), D ** -0.5),
        "q_norm_w_l1": 1.0 + nrm(ks[15], (HEAD_DIM,), 0.02),
        "k_norm_w_l1": 1.0 + nrm(ks[16], (HEAD_DIM,), 0.02),
        "sink_l1": nrm(ks[17], (N_HEADS,), 0.5),
        "w_out_l1": nrm(ks[18], (E, D), E ** -0.5),
    }


def reference(x_prompt, x_sample, cache_k_l1, cache_v_l1, c, c_ctx,
              norm_w_l0, w_mod_l0, b_mod_l0, w_in_l0, w_out_l0,
              norm_w_l1, w_mod_l1, b_mod_l1, w_in_l1, q_norm_w_l1, k_norm_w_l1, sink_l1,
              w_out_l1):
    fourier_params = (norm_w_l0, w_mod_l0, b_mod_l0, w_in_l0, w_out_l0)
    attn_params = (norm_w_l1, w_mod_l1, b_mod_l1, w_in_l1, q_norm_w_l1, k_norm_w_l1, sink_l1,
                   w_out_l1)
    layer_params = (fourier_params, attn_params)
    cond_ctx = c_ctx[None, :]
    xp, xs = x_prompt, x_sample
    new_k_l1, new_v_l1 = None, None
    for i in range(DEPTH):
        p = layer_params[i]
        if i % N_MIXERS == 0:
            xp = fourier_layer(xp, cond_ctx, *p)
            xs = fourier_layer(xs, c, *p)
        else:
            xp, new_k_l1, new_v_l1 = attn_layer_context(xp, cond_ctx, *p)
            xs = attn_layer_latent(xs, c, cache_k_l1, cache_v_l1, *p)
    return (xp, xs, new_k_l1, new_v_l1)
```

```python
import functools

import numpy as np
import jax
import jax.numpy as jnp
from jax import lax
from jax.experimental import pallas as pl
from jax.experimental.pallas import tpu as pltpu

D_MODEL = 1024
BRANCH = 1024
N_GROUPS = 4
GROUP_W = BRANCH // N_GROUPS
HEAD_DIM = 64
N_HEADS = 16
N_KV = 4
KV_W = N_KV * HEAD_DIM
GRID_W = 64
WINDOW = 128
BLOCK = 128
ROPE_THETA = 10000.0
EPS = 1e-6
NEG_INF = -1e30
LANES = 128
ROW_CHUNK = 256
VMEM_LIMIT = 56 * 1024 * 1024

F32 = jnp.float32
BF16 = jnp.bfloat16


def _dot(a, b):
    return jnp.dot(a, b, preferred_element_type=F32)


def _dot_nt(a, b):
    return lax.dot_general(a, b, (((1,), (1,)), ((), ())), preferred_element_type=F32)


def _mod_norm(x, nw, mod):
    shift = mod[:, :D_MODEL]
    scale = mod[:, D_MODEL:2 * D_MODEL]
    y = x * lax.rsqrt(jnp.mean(x * x, axis=-1, keepdims=True) + EPS)
    return (y * nw) * (1.0 + scale) + shift


def _mod_kernel(cond_ref, w_ref, b_ref, o_ref):
    s = jax.nn.silu(cond_ref[...]).astype(BF16)
    o_ref[...] = _dot(s, w_ref[...].astype(BF16)) + b_ref[...]


def _modulation(cond8, w_mod, b_mod):
    n = w_mod.shape[1]
    bn = 512
    return pl.pallas_call(
        _mod_kernel,
        grid=(n // bn,),
        in_specs=[
            pl.BlockSpec((8, D_MODEL), lambda j: (0, 0)),
            pl.BlockSpec((D_MODEL, bn), lambda j: (0, j)),
            pl.BlockSpec((1, bn), lambda j: (0, j)),
        ],
        out_specs=pl.BlockSpec((8, bn), lambda j: (0, j)),
        out_shape=jax.ShapeDtypeStruct((8, n), F32),
        name="modulation",
    )(cond8, w_mod, b_mod.reshape(1, n))


def _fourier_kernel(x_ref, mod_ref, nw_ref, win_ref, wout_ref, mc_ref, ls_ref, o_ref,
                    ab_scr, z_scr, *, seq, mod_row0):
    b = pl.program_id(0)
    mod = mod_ref[pl.ds(mod_row0 + (b if mod_row0 else 0), 1), :]
    gate = mod[:, 2 * D_MODEL:]
    nw = nw_ref[...]
    n_chunks = seq // ROW_CHUNK
    for c in range(n_chunks):
        rows = slice(c * ROW_CHUNK, (c + 1) * ROW_CHUNK)
        h = _mod_norm(x_ref[0, rows, :], nw, mod).astype(BF16)
        uz = _dot(h, win_ref[...])
        z_scr[rows, :] = uz[:, BRANCH:]
        u = uz[:, :BRANCH].astype(BF16)
        for g in range(N_GROUPS):
            cols = slice(g * GROUP_W, (g + 1) * GROUP_W)
            t = _dot(u[:, cols], mc_ref[...])
            ab_scr[rows, cols] = t[:, :GROUP_W].astype(BF16)
            ab_scr[seq + c * ROW_CHUNK:seq + (c + 1) * ROW_CHUNK, cols] = t[:, GROUP_W:].astype(BF16)
    for c in range(n_chunks):
        rows = slice(c * ROW_CHUNK, (c + 1) * ROW_CHUNK)
        y = _dot(ls_ref[rows, :], ab_scr[...])
        y = (y * jax.nn.silu(z_scr[rows, :])).astype(BF16)
        o_ref[0, rows, :] = x_ref[0, rows, :] + gate * _dot(y, wout_ref[...])


def _dft_tables(seq):
    k = np.arange(GROUP_W)
    ang = 2.0 * np.pi * ((k[:, None] * k[None, :]) % GROUP_W) / GROUP_W
    mc = np.concatenate([np.cos(ang), np.sin(ang)], axis=1) / np.sqrt(GROUP_W)
    n = np.arange(seq)
    ang = 2.0 * np.pi * ((n[:, None] * n[None, :]) % seq) / seq
    ls = np.concatenate([np.cos(ang), -np.sin(ang)], axis=1) / np.sqrt(seq)
    return mc.astype(np.float32), ls.astype(np.float32)


def _const_spec(shape):
    return pl.BlockSpec(shape, lambda b: (0,) * len(shape))


def _fourier_layer(x, mod, mod_row0, nw, win, wout):
    nb, seq, _ = x.shape
    mc, ls = _dft_tables(seq)
    mc = jnp.asarray(mc).astype(BF16)
    ls = jnp.asarray(ls).astype(BF16)
    kern = functools.partial(_fourier_kernel, seq=seq, mod_row0=mod_row0)
    return pl.pallas_call(
        kern,
        grid=(nb,),
        in_specs=[
            pl.BlockSpec((1, seq, D_MODEL), lambda b: (b, 0, 0)),
            _const_spec((8, 3 * D_MODEL)),
            _const_spec((1, D_MODEL)),
            _const_spec((D_MODEL, 2 * BRANCH)),
            _const_spec((BRANCH, D_MODEL)),
            _const_spec((GROUP_W, 2 * GROUP_W)),
            _const_spec((seq, 2 * seq)),
        ],
        out_specs=pl.BlockSpec((1, seq, D_MODEL), lambda b: (b, 0, 0)),
        out_shape=jax.ShapeDtypeStruct(x.shape, F32),
        scratch_shapes=[
            pltpu.VMEM((2 * seq, BRANCH), BF16),
            pltpu.VMEM((seq, BRANCH), F32),
        ],
        compiler_params=pltpu.CompilerParams(
            dimension_semantics=("arbitrary",), vmem_limit_bytes=VMEM_LIMIT),
        name=f"fourier_layer_s{seq}",
    )(x, mod, nw, win, wout, mc, ls)


def _head_mean_sq(t, ind):
    return _dot((t * t).astype(BF16), ind)


def _swap_halves(t):
    return pltpu.roll(t, HEAD_DIM, axis=1)


def _rope(t, cos, sin):
    lane = lax.broadcasted_iota(jnp.int32, t.shape, 1)
    first = (lane % HEAD_DIM) < (HEAD_DIM // 2)
    partner = jnp.where(first, pltpu.roll(t, LANES - HEAD_DIM // 2, axis=1),
                        pltpu.roll(t, HEAD_DIM // 2, axis=1))
    return t * cos + partner * sin


def _stack_pair(blk, g):
    lane = lax.broadcasted_iota(jnp.int32, blk.shape, 1)
    lo = lane < HEAD_DIM
    swapped = _swap_halves(blk)
    if g % 2 == 0:
        return jnp.where(lo, blk, 0.0), jnp.where(lo, 0.0, swapped)
    return jnp.where(lo, swapped, 0.0), jnp.where(lo, 0.0, blk)


def _sink_softmax_unnorm(scores, sink):
    m = sink
    for s in scores:
        m = jnp.maximum(m, jnp.max(s, axis=-1, keepdims=True))
    es = [jnp.exp(s - m) for s in scores]
    den = jnp.exp(sink - m)
    for e in es:
        den = den + jnp.sum(e, axis=-1, keepdims=True)
    return es, den


def _attn_ctx_kernel(sink_ref, x_ref, mod_ref, nw_ref, win_ref, wout_ref, qw_ref, kw_ref, ind_ref,
                     o_ref, ko_ref, vo_ref, q_scr, o_scr, *, seq):
    mod = mod_ref[0:1, :]
    gate = mod[:, 2 * D_MODEL:]
    x = x_ref[0]
    h = _mod_norm(x, nw_ref[...], mod).astype(BF16)
    qkvz = _dot(h, win_ref[...])
    ind = ind_ref[...]
    z = qkvz[:, BRANCH + 2 * KV_W:]
    for p in range(N_HEADS // 2):
        cols = slice(p * LANES, (p + 1) * LANES)
        t = qkvz[:, cols]
        t = t * lax.rsqrt(_head_mean_sq(t, ind) + EPS) * qw_ref[:, cols]
        q_scr[:, cols] = (t * (HEAD_DIM ** -0.5)).astype(BF16)
    kn = []
    vv = []
    for p in range(N_KV // 2):
        cols = slice(BRANCH + p * LANES, BRANCH + (p + 1) * LANES)
        t = qkvz[:, cols]
        t = t * lax.rsqrt(_head_mean_sq(t, ind) + EPS) * kw_ref[:, p * LANES:(p + 1) * LANES]
        kn.append(t)
        ko_ref[0, :, p * LANES:(p + 1) * LANES] = t
        v = qkvz[:, BRANCH + KV_W + p * LANES:BRANCH + KV_W + (p + 1) * LANES]
        vv.append(v)
        vo_ref[0, :, p * LANES:(p + 1) * LANES] = v
    lane = lax.broadcasted_iota(jnp.int32, (seq, LANES), 1)
    lo = lane < HEAD_DIM
    for g in range(N_KV):
        ka, kb = _stack_pair(kn[g // 2], g)
        va, vb = _stack_pair(vv[g // 2], g)
        kst = jnp.concatenate([ka, kb], axis=0).astype(BF16)
        vst = jnp.concatenate([va, vb], axis=0).astype(BF16)
        for pp in range(2):
            p = 2 * g + pp
            cols = slice(p * LANES, (p + 1) * LANES)
            s2 = _dot_nt(q_scr[:, cols], kst)
            (e0,), d0 = _sink_softmax_unnorm([s2[:, :seq]], sink_ref[2 * p])
            (e1,), d1 = _sink_softmax_unnorm([s2[:, seq:]], sink_ref[2 * p + 1])
            pe = jnp.concatenate([e0, e1], axis=1).astype(BF16)
            o = _dot(pe, vst)
            o_scr[:, cols] = o * jnp.where(lo, 1.0 / d0, 1.0 / d1)
    y = (o_scr[...] * jax.nn.silu(z)).astype(BF16)
    o_ref[0] = x + gate * _dot(y, wout_ref[...])


def _head_indicator():
    i = np.arange(LANES)
    same = (i[:, None] // HEAD_DIM) == (i[None, :] // HEAD_DIM)
    return (same / HEAD_DIM).astype(np.float32)


def _attn_ctx_layer(x, mod, nw, win, wout, qw, kw, sink):
    nb, seq, _ = x.shape
    ind = jnp.asarray(_head_indicator()).astype(BF16)
    kern = functools.partial(_attn_ctx_kernel, seq=seq)
    nqkvz = 2 * BRANCH + 2 * KV_W
    return pl.pallas_call(
        kern,
        grid=(nb,),
        in_specs=[
            pl.BlockSpec(memory_space=pltpu.SMEM),
            pl.BlockSpec((1, seq, D_MODEL), lambda b: (b, 0, 0)),
            _const_spec((8, 3 * D_MODEL)),
            _const_spec((1, D_MODEL)),
            _const_spec((D_MODEL, nqkvz)),
            _const_spec((BRANCH, D_MODEL)),
            _const_spec((1, BRANCH)),
            _const_spec((1, KV_W)),
            _const_spec((LANES, LANES)),
        ],
        out_specs=[
            pl.BlockSpec((1, seq, D_MODEL), lambda b: (b, 0, 0)),
            pl.BlockSpec((1, seq, KV_W), lambda b: (b, 0, 0)),
            pl.BlockSpec((1, seq, KV_W), lambda b: (b, 0, 0)),
        ],
        out_shape=[
            jax.ShapeDtypeStruct(x.shape, F32),
            jax.ShapeDtypeStruct((nb, seq, KV_W), F32),
            jax.ShapeDtypeStruct((nb, seq, KV_W), F32),
        ],
        scratch_shapes=[
            pltpu.VMEM((seq, BRANCH), BF16),
            pltpu.VMEM((seq, BRANCH), F32),
        ],
        compiler_params=pltpu.CompilerParams(
            dimension_semantics=("arbitrary",), vmem_limit_bytes=VMEM_LIMIT),
        name="attn_context_layer",
    )(sink, x, mod, nw, win, wout, qw, kw, ind)


def _attn_lat_kernel(sink_ref, x_ref, mod_ref, nw_ref, win_ref, wout_ref, qw_ref, kw_ref, ind_ref,
                     cos_ref, sin_ref, ck_ref, cv_ref, o_ref,
                     q_scr, z_scr, o_scr, ka_scr, kb_scr, va_scr, vb_scr,
                     cka_scr, ckb_scr, cva_scr, cvb_scr, *, seq, past):
    b = pl.program_id(0)
    mod = mod_ref[pl.ds(1 + b, 1), :]
    gate = mod[:, 2 * D_MODEL:]
    nw = nw_ref[...]
    ind = ind_ref[...]
    n_chunks = seq // ROW_CHUNK
    zeros_pad = jnp.zeros((BLOCK, LANES), BF16)
    for g in range(N_KV):
        for scr in (ka_scr, kb_scr, va_scr, vb_scr):
            scr[g, 0:BLOCK, :] = zeros_pad
            scr[g, BLOCK + seq:2 * BLOCK + seq, :] = zeros_pad
        cka, ckb = _stack_pair(ck_ref[0, :, (g // 2) * LANES:(g // 2 + 1) * LANES], g)
        cva, cvb = _stack_pair(cv_ref[0, :, (g // 2) * LANES:(g // 2 + 1) * LANES], g)
        cka_scr[g] = cka.astype(BF16)
        ckb_scr[g] = ckb.astype(BF16)
        cva_scr[g] = cva.astype(BF16)
        cvb_scr[g] = cvb.astype(BF16)
    for c in range(n_chunks):
        rows = slice(c * ROW_CHUNK, (c + 1) * ROW_CHUNK)
        prow = slice(BLOCK + c * ROW_CHUNK, BLOCK + (c + 1) * ROW_CHUNK)
        h = _mod_norm(x_ref[0, rows, :], nw, mod).astype(BF16)
        qkvz = _dot(h, win_ref[...])
        z_scr[rows, :] = qkvz[:, BRANCH + 2 * KV_W:]
        cos = cos_ref[rows, :]
        sin = sin_ref[rows, :]
        for p in range(N_HEADS // 2):
            cols = slice(p * LANES, (p + 1) * LANES)
            t = qkvz[:, cols]
            t = t * lax.rsqrt(_head_mean_sq(t, ind) + EPS) * qw_ref[:, cols]
            t = _rope(t, cos, sin)
            q_scr[rows, cols] = (t * (HEAD_DIM ** -0.5)).astype(BF16)
        for p in range(N_KV // 2):
            cols = slice(BRANCH + p * LANES, BRANCH + (p + 1) * LANES)
            t = qkvz[:, cols]
            t = t * lax.rsqrt(_head_mean_sq(t, ind) + EPS) * kw_ref[:, p * LANES:(p + 1) * LANES]
            t = _rope(t, cos, sin)
            v = qkvz[:, BRANCH + KV_W + p * LANES:BRANCH + KV_W + (p + 1) * LANES]
            for g in (2 * p, 2 * p + 1):
                ka, kb = _stack_pair(t, g)
                va, vb = _stack_pair(v, g)
                ka_scr[g, prow, :] = ka.astype(BF16)
                kb_scr[g, prow, :] = kb.astype(BF16)
                va_scr[g, prow, :] = va.astype(BF16)
                vb_scr[g, prow, :] = vb.astype(BF16)

    win_len = 3 * BLOCK
    qi = lax.broadcasted_iota(jnp.int32, (BLOCK, win_len), 0)
    kj = lax.broadcasted_iota(jnp.int32, (BLOCK, win_len), 1)
    rel = kj - qi
    band = (rel >= BLOCK - WINDOW) & (rel <= BLOCK + WINDOW)
    lane = lax.broadcasted_iota(jnp.int32, (BLOCK, LANES), 1)
    lo = lane < HEAD_DIM

    def q_block(n, carry):
        r0 = pl.multiple_of(n * BLOCK, BLOCK)
        kpos = kj + (r0 - BLOCK)
        valid = band & (kpos >= 0) & (kpos < seq)
        for g in range(N_KV):
            kaw = ka_scr[g, pl.ds(r0, win_len), :]
            kbw = kb_scr[g, pl.ds(r0, win_len), :]
            vaw = va_scr[g, pl.ds(r0, win_len), :]
            vbw = vb_scr[g, pl.ds(r0, win_len), :]
            for pp in range(2):
                p = 2 * g + pp
                cols = slice(p * LANES, (p + 1) * LANES)
                qp = q_scr[pl.ds(r0, BLOCK), cols]
                s_e = jnp.where(valid, _dot_nt(qp, kaw), NEG_INF)
                s_o = jnp.where(valid, _dot_nt(qp, kbw), NEG_INF)
                s_ec = _dot_nt(qp, cka_scr[g])
                s_oc = _dot_nt(qp, ckb_scr[g])
                (e_e, e_ec), d_e = _sink_softmax_unnorm([s_e, s_ec], sink_ref[2 * p])
                (e_o, e_oc), d_o = _sink_softmax_unnorm([s_o, s_oc], sink_ref[2 * p + 1])
                o = (_dot(e_e.astype(BF16), vaw) + _dot(e_o.astype(BF16), vbw)
                     + _dot(e_ec.astype(BF16), cva_scr[g]) + _dot(e_oc.astype(BF16), cvb_scr[g]))
                o_scr[pl.ds(r0, BLOCK), cols] = o * jnp.where(lo, 1.0 / d_e, 1.0 / d_o)
        return carry

    lax.fori_loop(0, seq // BLOCK, q_block, 0)

    for c in range(n_chunks):
        rows = slice(c * ROW_CHUNK, (c + 1) * ROW_CHUNK)
        y = (o_scr[rows, :] * jax.nn.silu(z_scr[rows, :])).astype(BF16)
        o_ref[0, rows, :] = x_ref[0, rows, :] + gate * _dot(y, wout_ref[...])


def _rope_tables(seq):
    rows = seq // GRID_W
    row = jnp.repeat(jnp.arange(rows, dtype=F32), GRID_W)
    col = jnp.tile(jnp.arange(GRID_W, dtype=F32), rows)
    n_freq = HEAD_DIM // 4
    inv = ROPE_THETA ** (-jnp.arange(n_freq, dtype=F32) / n_freq)
    ang = jnp.concatenate([row[:, None] * inv, col[:, None] * inv], axis=-1)
    c, s = jnp.cos(ang), jnp.sin(ang)
    cos = jnp.concatenate([c, c, c, c], axis=-1)
    sin = jnp.concatenate([-s, s, -s, s], axis=-1)
    return cos, sin


def _attn_lat_layer(x, mod, nw, win, wout, qw, kw, sink, ck, cv):
    nb, seq, _ = x.shape
    past = ck.shape[1]
    ind = jnp.asarray(_head_indicator()).astype(BF16)
    cos, sin = _rope_tables(seq)
    kern = functools.partial(_attn_lat_kernel, seq=seq, past=past)
    nqkvz = 2 * BRANCH + 2 * KV_W
    pad_len = seq + 2 * BLOCK
    return pl.pallas_call(
        kern,
        grid=(nb,),
        in_specs=[
            pl.BlockSpec(memory_space=pltpu.SMEM),
            pl.BlockSpec((1, seq, D_MODEL), lambda b: (b, 0, 0)),
            _const_spec((8, 3 * D_MODEL)),
            _const_spec((1, D_MODEL)),
            _const_spec((D_MODEL, nqkvz)),
            _const_spec((BRANCH, D_MODEL)),
            _const_spec((1, BRANCH)),
            _const_spec((1, KV_W)),
            _const_spec((LANES, LANES)),
            _const_spec((seq, LANES)),
            _const_spec((seq, LANES)),
            pl.BlockSpec((1, past, KV_W), lambda b: (b, 0, 0)),
            pl.BlockSpec((1, past, KV_W), lambda b: (b, 0, 0)),
        ],
        out_specs=pl.BlockSpec((1, seq, D_MODEL), lambda b: (b, 0, 0)),
        out_shape=jax.ShapeDtypeStruct(x.shape, F32),
        scratch_shapes=[
            pltpu.VMEM((seq, BRANCH), BF16),
            pltpu.VMEM((seq, BRANCH), F32),
            pltpu.VMEM((seq, BRANCH), F32),
            pltpu.VMEM((N_KV, pad_len, LANES), BF16),
            pltpu.VMEM((N_KV, pad_len, LANES), BF16),
            pltpu.VMEM((N_KV, pad_len, LANES), BF16),
            pltpu.VMEM((N_KV, pad_len, LANES), BF16),
            pltpu.VMEM((N_KV, past, LANES), BF16),
            pltpu.VMEM((N_KV, past, LANES), BF16),
            pltpu.VMEM((N_KV, past, LANES), BF16),
            pltpu.VMEM((N_KV, past, LANES), BF16),
        ],
        compiler_params=pltpu.CompilerParams(
            dimension_semantics=("arbitrary",), vmem_limit_bytes=VMEM_LIMIT),
        name="attn_latent_layer",
    )(sink, x, mod, nw, win, wout, qw, kw, ind, cos, sin, ck, cv)


def kernel(x_prompt, x_sample, cache_k_l1, cache_v_l1, c, c_ctx, norm_w_l0, w_mod_l0, b_mod_l0,
           w_in_l0, w_out_l0, norm_w_l1, w_mod_l1, b_mod_l1, w_in_l1, q_norm_w_l1, k_norm_w_l1,
           sink_l1, w_out_l1):
    nb_ctx, seq_ctx, _ = x_prompt.shape
    nb_lat = x_sample.shape[0]
    past = cache_k_l1.shape[1]
    cond8 = jnp.concatenate(
        [c_ctx[None, :], c, jnp.zeros((8 - 1 - nb_lat, D_MODEL), F32)], axis=0)
    mod0 = _modulation(cond8, w_mod_l0, b_mod_l0)
    mod1 = _modulation(cond8, w_mod_l1, b_mod_l1)

    nw0 = norm_w_l0.reshape(1, D_MODEL)
    nw1 = norm_w_l1.reshape(1, D_MODEL)
    win0 = w_in_l0.astype(BF16)
    wout0 = w_out_l0.astype(BF16)
    win1 = w_in_l1.astype(BF16)
    wout1 = w_out_l1.astype(BF16)
    qw = jnp.tile(q_norm_w_l1, N_HEADS).reshape(1, BRANCH)
    kw = jnp.tile(k_norm_w_l1, N_KV).reshape(1, KV_W)

    xp = _fourier_layer(x_prompt, mod0, 0, nw0, win0, wout0)
    xs = _fourier_layer(x_sample, mod0, 1, nw0, win0, wout0)

    xp, new_k, new_v = _attn_ctx_layer(xp, mod1, nw1, win1, wout1, qw, kw, sink_l1)
    xs = _attn_lat_layer(xs, mod1, nw1, win1, wout1, qw, kw, sink_l1,
                         cache_k_l1.reshape(nb_lat, past, KV_W),
                         cache_v_l1.reshape(nb_lat, past, KV_W))
    new_k = new_k.reshape(nb_ctx, seq_ctx, N_KV, HEAD_DIM)
    new_v = new_v.reshape(nb_ctx, seq_ctx, N_KV, HEAD_DIM)
    return (xp, xs, new_k, new_v)
```

```python
import functools

import numpy as np
import jax
import jax.numpy as jnp
from jax import lax
from jax.experimental import pallas as pl
from jax.experimental.pallas import tpu as pltpu

D_MODEL = 1024
BRANCH = 1024
N_GROUPS = 4
GROUP_W = BRANCH // N_GROUPS
HEAD_DIM = 64
N_HEADS = 16
N_KV = 4
GQA = N_HEADS // N_KV
KV_W = N_KV * HEAD_DIM
GRID_W = 64
WINDOW = 128
BLOCK = 128
ROPE_THETA = 10000.0
EPS = 1e-6
NEG_INF = -1e30
LANES = 128
ROW_CHUNK = 256
Q_BLOCK = 256
VMEM_LIMIT = 56 * 1024 * 1024

F32 = jnp.float32
BF16 = jnp.bfloat16


def _dot(a, b):
    return jnp.dot(a, b, preferred_element_type=F32)


def _dot_nt(a, b):
    return lax.dot_general(a, b, (((1,), (1,)), ((), ())), preferred_element_type=F32)


def _mod_norm(x, nw, mod):
    shift = mod[:, :D_MODEL]
    scale = mod[:, D_MODEL:2 * D_MODEL]
    y = x * lax.rsqrt(jnp.mean(x * x, axis=-1, keepdims=True) + EPS)
    return (y * nw) * (1.0 + scale) + shift


def _mod_kernel(cond_ref, w_ref, b_ref, o_ref):
    s = jax.nn.silu(cond_ref[...]).astype(BF16)
    o_ref[...] = _dot(s, w_ref[...].astype(BF16)) + b_ref[...]


def _modulation(cond8, w_mod, b_mod):
    n = w_mod.shape[1]
    bn = 512
    return pl.pallas_call(
        _mod_kernel,
        grid=(n // bn,),
        in_specs=[
            pl.BlockSpec((8, D_MODEL), lambda j: (0, 0)),
            pl.BlockSpec((D_MODEL, bn), lambda j: (0, j)),
            pl.BlockSpec((1, bn), lambda j: (0, j)),
        ],
        out_specs=pl.BlockSpec((8, bn), lambda j: (0, j)),
        out_shape=jax.ShapeDtypeStruct((8, n), F32),
        name="modulation",
    )(cond8, w_mod, b_mod.reshape(1, n))


def _fourier_kernel(x_ref, mod_ref, nw_ref, win_ref, wout_ref, mc_ref, ls_ref, o_ref,
                    ab_scr, z_scr, *, seq, mod_row0):
    b = pl.program_id(0)
    mod = mod_ref[pl.ds(mod_row0 + (b if mod_row0 else 0), 1), :]
    gate = mod[:, 2 * D_MODEL:]
    nw = nw_ref[...]
    n_chunks = seq // ROW_CHUNK
    for c in range(n_chunks):
        rows = slice(c * ROW_CHUNK, (c + 1) * ROW_CHUNK)
        h = _mod_norm(x_ref[0, rows, :], nw, mod).astype(BF16)
        uz = _dot(h, win_ref[...])
        z_scr[rows, :] = uz[:, BRANCH:]
        u = uz[:, :BRANCH].astype(BF16)
        for g in range(N_GROUPS):
            cols = slice(g * GROUP_W, (g + 1) * GROUP_W)
            t = _dot(u[:, cols], mc_ref[...])
            ab_scr[rows, cols] = t[:, :GROUP_W].astype(BF16)
            ab_scr[seq + c * ROW_CHUNK:seq + (c + 1) * ROW_CHUNK, cols] = t[:, GROUP_W:].astype(BF16)
    for c in range(n_chunks):
        rows = slice(c * ROW_CHUNK, (c + 1) * ROW_CHUNK)
        y = _dot(ls_ref[rows, :], ab_scr[...])
        y = (y * jax.nn.silu(z_scr[rows, :])).astype(BF16)
        o_ref[0, rows, :] = x_ref[0, rows, :] + gate * _dot(y, wout_ref[...])


def _dft_tables(seq):
    k = np.arange(GROUP_W)
    ang = 2.0 * np.pi * ((k[:, None] * k[None, :]) % GROUP_W) / GROUP_W
    mc = np.concatenate([np.cos(ang), np.sin(ang)], axis=1) / np.sqrt(GROUP_W)
    n = np.arange(seq)
    ang = 2.0 * np.pi * ((n[:, None] * n[None, :]) % seq) / seq
    ls = np.concatenate([np.cos(ang), -np.sin(ang)], axis=1) / np.sqrt(seq)
    return mc.astype(np.float32), ls.astype(np.float32)


def _const_spec(shape):
    return pl.BlockSpec(shape, lambda b: (0,) * len(shape))


def _fourier_layer(x, mod, mod_row0, nw, win, wout):
    nb, seq, _ = x.shape
    mc, ls = _dft_tables(seq)
    mc = jnp.asarray(mc).astype(BF16)
    ls = jnp.asarray(ls).astype(BF16)
    kern = functools.partial(_fourier_kernel, seq=seq, mod_row0=mod_row0)
    return pl.pallas_call(
        kern,
        grid=(nb,),
        in_specs=[
            pl.BlockSpec((1, seq, D_MODEL), lambda b: (b, 0, 0)),
            _const_spec((8, 3 * D_MODEL)),
            _const_spec((1, D_MODEL)),
            _const_spec((D_MODEL, 2 * BRANCH)),
            _const_spec((BRANCH, D_MODEL)),
            _const_spec((GROUP_W, 2 * GROUP_W)),
            _const_spec((seq, 2 * seq)),
        ],
        out_specs=pl.BlockSpec((1, seq, D_MODEL), lambda b: (b, 0, 0)),
        out_shape=jax.ShapeDtypeStruct(x.shape, F32),
        scratch_shapes=[
            pltpu.VMEM((2 * seq, BRANCH), BF16),
            pltpu.VMEM((seq, BRANCH), F32),
        ],
        compiler_params=pltpu.CompilerParams(
            dimension_semantics=("arbitrary",), vmem_limit_bytes=VMEM_LIMIT),
        name=f"fourier_layer_s{seq}",
    )(x, mod, nw, win, wout, mc, ls)


def _head_rms(t, w):
    return (t * lax.rsqrt(jnp.mean(t * t, axis=0, keepdims=True) + EPS)) * w


def _rope_t(t, cos, sin):
    half = HEAD_DIM // 2
    x1, x2 = t[:half], t[half:]
    return jnp.concatenate([x1 * cos - x2 * sin, x1 * sin + x2 * cos], axis=0)


def _head_scores(qn, g, keys, masks):
    zeros = jnp.zeros_like(qn)
    qz = jnp.concatenate([qn, zeros] if g % 2 == 0 else [zeros, qn], axis=0)
    blk = slice((g // 2) * LANES, (g // 2 + 1) * LANES)
    scores = []
    for k, msk in zip(keys, masks):
        s = _dot(k[:, blk], qz)
        scores.append(s if msk is None else jnp.where(msk, s, NEG_INF))
    return scores


def _head_softmax_pv(scores, values_t, sink):
    m = sink
    for s in scores:
        m = jnp.maximum(m, jnp.max(s, axis=0, keepdims=True))
    den = jnp.exp(sink - m)
    acc = None
    for s, vt in zip(scores, values_t):
        e = jnp.exp(s - m)
        den = den + jnp.sum(e, axis=0, keepdims=True)
        pv = _dot(vt, e.astype(BF16))
        acc = pv if acc is None else acc + pv
    return acc * (1.0 / den)


def _attend_heads(scores_fn, finish_fn):
    pending = [scores_fn(hd) for hd in range(GQA)]
    for g in range(N_KV):
        nxt = [scores_fn(hd) for hd in range((g + 1) * GQA, (g + 2) * GQA)] if g + 1 < N_KV else None
        for i, sc in enumerate(pending):
            finish_fn(g * GQA + i, sc)
        pending = nxt


def _gate_out(x, o, z, gate, wout):
    y = (o * jax.nn.silu(z)).astype(BF16)
    return x + gate * _dot(y, wout)


def _attn_ctx_kernel(sink_ref, x_ref, mod_ref, nw_ref, wqkvt_ref, wvz_ref, wout_ref, qw_ref, kw_ref,
                     o_ref, ko_ref, vo_ref, ot_scr):
    mod = mod_ref[0:1, :]
    gate = mod[:, 2 * D_MODEL:]
    x = x_ref[0]
    h = _mod_norm(x, nw_ref[...], mod).astype(BF16)
    qkvt = _dot_nt(wqkvt_ref[...], h)
    vz = _dot(h, wvz_ref[...])
    vo_ref[0] = vz[:, :KV_W]
    z = vz[:, KV_W:]
    kw = kw_ref[...]
    knt = jnp.concatenate(
        [_head_rms(qkvt[BRANCH + g * HEAD_DIM:BRANCH + (g + 1) * HEAD_DIM], kw) for g in range(N_KV)],
        axis=0)
    k = knt.T
    ko_ref[0] = k
    kb = k.astype(BF16)
    vt = qkvt[BRANCH + KV_W:].astype(BF16)
    qw = qw_ref[...]

    def scores_fn(hd):
        rows = slice(hd * HEAD_DIM, (hd + 1) * HEAD_DIM)
        qn = (_head_rms(qkvt[rows], qw) * (HEAD_DIM ** -0.5)).astype(BF16)
        return _head_scores(qn, hd // GQA, [kb], [None])

    def finish_fn(hd, sc):
        g = hd // GQA
        ot_scr[hd * HEAD_DIM:(hd + 1) * HEAD_DIM, :] = _head_softmax_pv(
            sc, [vt[g * HEAD_DIM:(g + 1) * HEAD_DIM]], sink_ref[hd])

    _attend_heads(scores_fn, finish_fn)
    o_ref[0] = _gate_out(x, ot_scr[...].T, z, gate, wout_ref[...])


def _attn_ctx_layer(x, mod, nw, wqkvt, wvz, wout, qw, kw, sink):
    nb, seq, _ = x.shape
    return pl.pallas_call(
        _attn_ctx_kernel,
        grid=(nb,),
        in_specs=[
            pl.BlockSpec(memory_space=pltpu.SMEM),
            pl.BlockSpec((1, seq, D_MODEL), lambda b: (b, 0, 0)),
            _const_spec((8, 3 * D_MODEL)),
            _const_spec((1, D_MODEL)),
            _const_spec((BRANCH + 2 * KV_W, D_MODEL)),
            _const_spec((D_MODEL, KV_W + BRANCH)),
            _const_spec((BRANCH, D_MODEL)),
            _const_spec((HEAD_DIM, seq)),
            _const_spec((HEAD_DIM, seq)),
        ],
        out_specs=[
            pl.BlockSpec((1, seq, D_MODEL), lambda b: (b, 0, 0)),
            pl.BlockSpec((1, seq, KV_W), lambda b: (b, 0, 0)),
            pl.BlockSpec((1, seq, KV_W), lambda b: (b, 0, 0)),
        ],
        out_shape=[
            jax.ShapeDtypeStruct(x.shape, F32),
            jax.ShapeDtypeStruct((nb, seq, KV_W), F32),
            jax.ShapeDtypeStruct((nb, seq, KV_W), F32),
        ],
        scratch_shapes=[pltpu.VMEM((BRANCH, seq), F32)],
        compiler_params=pltpu.CompilerParams(
            dimension_semantics=("arbitrary",), vmem_limit_bytes=VMEM_LIMIT),
        name="attn_context_layer",
    )(sink, x, mod, nw, wqkvt, wvz, wout, qw, kw)


def _attn_lat_kernel(sink_ref, x_ref, mod_ref, nw_ref, wqkvt_ref, wz_ref, wout_ref, qw_ref, kw_ref,
                     cos_ref, sin_ref, ck_ref, cv_ref, o_ref,
                     q_scr, z_scr, ot_scr, k_scr, vt_scr, *, seq):
    b = pl.program_id(0)
    mod = mod_ref[pl.ds(1 + b, 1), :]
    gate = mod[:, 2 * D_MODEL:]
    nw = nw_ref[...]
    qw = qw_ref[...]
    kw = kw_ref[...]
    n_blocks = seq // Q_BLOCK
    kv_blocks = seq // BLOCK
    k_scr[0:BLOCK, :] = jnp.zeros((BLOCK, KV_W), BF16)
    k_scr[BLOCK + seq:2 * BLOCK + seq, :] = jnp.zeros((BLOCK, KV_W), BF16)
    vt_scr[0] = jnp.zeros((KV_W, BLOCK), BF16)
    vt_scr[kv_blocks + 1] = jnp.zeros((KV_W, BLOCK), BF16)

    for c in range(n_blocks):
        rows = slice(c * Q_BLOCK, (c + 1) * Q_BLOCK)
        h = _mod_norm(x_ref[0, rows, :], nw, mod).astype(BF16)
        z_scr[rows, :] = _dot(h, wz_ref[...])
        qkvt = _dot_nt(wqkvt_ref[...], h)
        cos = cos_ref[:, rows]
        sin = sin_ref[:, rows]
        for hd in range(N_HEADS):
            hr = slice(hd * HEAD_DIM, (hd + 1) * HEAD_DIM)
            t = _rope_t(_head_rms(qkvt[hr], qw), cos, sin)
            q_scr[c, hr, :] = (t * (HEAD_DIM ** -0.5)).astype(BF16)
        knt = jnp.concatenate(
            [_rope_t(_head_rms(qkvt[BRANCH + g * HEAD_DIM:BRANCH + (g + 1) * HEAD_DIM], kw), cos, sin)
             for g in range(N_KV)], axis=0)
        k_scr[BLOCK + c * Q_BLOCK:BLOCK + (c + 1) * Q_BLOCK, :] = knt.T.astype(BF16)
        vt = qkvt[BRANCH + KV_W:].astype(BF16)
        for j in range(Q_BLOCK // BLOCK):
            vt_scr[1 + c * (Q_BLOCK // BLOCK) + j] = vt[:, j * BLOCK:(j + 1) * BLOCK]

    ckb = ck_ref[0].astype(BF16)
    cvt = cv_ref[0].T.astype(BF16)

    win_len = Q_BLOCK + 2 * BLOCK
    kj = lax.broadcasted_iota(jnp.int32, (win_len, Q_BLOCK), 0)
    qi = lax.broadcasted_iota(jnp.int32, (win_len, Q_BLOCK), 1)
    rel = kj - BLOCK - qi
    band = (rel >= -WINDOW) & (rel <= WINDOW)

    def q_block(n, carry):
        r0 = pl.multiple_of(n * Q_BLOCK, Q_BLOCK)
        kpos = kj + (r0 - BLOCK)
        valid = band & (kpos >= 0) & (kpos < seq)
        kwin = k_scr[pl.ds(r0, win_len), :]
        vwin = jnp.concatenate(
            [vt_scr[n * (Q_BLOCK // BLOCK) + j] for j in range(win_len // BLOCK)], axis=1)

        def scores_fn(hd):
            return _head_scores(q_scr[n, hd * HEAD_DIM:(hd + 1) * HEAD_DIM, :], hd // GQA,
                                [kwin, ckb], [valid, None])

        def finish_fn(hd, sc):
            gr = slice((hd // GQA) * HEAD_DIM, (hd // GQA + 1) * HEAD_DIM)
            ot_scr[n, hd * HEAD_DIM:(hd + 1) * HEAD_DIM, :] = _head_softmax_pv(
                sc, [vwin[gr], cvt[gr]], sink_ref[hd])

        _attend_heads(scores_fn, finish_fn)
        return carry

    lax.fori_loop(0, n_blocks, q_block, 0)

    for c in range(n_blocks):
        rows = slice(c * Q_BLOCK, (c + 1) * Q_BLOCK)
        o_ref[0, rows, :] = _gate_out(x_ref[0, rows, :], ot_scr[c].T, z_scr[rows, :], gate, wout_ref[...])


def _rope_tables_t(seq):
    rows = seq // GRID_W
    row = jnp.repeat(jnp.arange(rows, dtype=F32), GRID_W)
    col = jnp.tile(jnp.arange(GRID_W, dtype=F32), rows)
    n_freq = HEAD_DIM // 4
    inv = ROPE_THETA ** (-jnp.arange(n_freq, dtype=F32) / n_freq)
    ang = jnp.concatenate([row[:, None] * inv, col[:, None] * inv], axis=-1)
    return jnp.cos(ang).T, jnp.sin(ang).T


def _attn_lat_layer(x, mod, nw, wqkvt, wz, wout, qw, kw, sink, ck, cv):
    nb, seq, _ = x.shape
    past = ck.shape[1]
    cos, sin = _rope_tables_t(seq)
    kern = functools.partial(_attn_lat_kernel, seq=seq)
    n_blocks = seq // Q_BLOCK
    return pl.pallas_call(
        kern,
        grid=(nb,),
        in_specs=[
            pl.BlockSpec(memory_space=pltpu.SMEM),
            pl.BlockSpec((1, seq, D_MODEL), lambda b: (b, 0, 0)),
            _const_spec((8, 3 * D_MODEL)),
            _const_spec((1, D_MODEL)),
            _const_spec((BRANCH + 2 * KV_W, D_MODEL)),
            _const_spec((D_MODEL, BRANCH)),
            _const_spec((BRANCH, D_MODEL)),
            _const_spec((HEAD_DIM, Q_BLOCK)),
            _const_spec((HEAD_DIM, Q_BLOCK)),
            _const_spec((HEAD_DIM // 2, seq)),
            _const_spec((HEAD_DIM // 2, seq)),
            pl.BlockSpec((1, past, KV_W), lambda b: (b, 0, 0)),
            pl.BlockSpec((1, past, KV_W), lambda b: (b, 0, 0)),
        ],
        out_specs=pl.BlockSpec((1, seq, D_MODEL), lambda b: (b, 0, 0)),
        out_shape=jax.ShapeDtypeStruct(x.shape, F32),
        scratch_shapes=[
            pltpu.VMEM((n_blocks, BRANCH, Q_BLOCK), BF16),
            pltpu.VMEM((seq, BRANCH), F32),
            pltpu.VMEM((n_blocks, BRANCH, Q_BLOCK), F32),
            pltpu.VMEM((seq + 2 * BLOCK, KV_W), BF16),
            pltpu.VMEM((seq // BLOCK + 2, KV_W, BLOCK), BF16),
        ],
        compiler_params=pltpu.CompilerParams(
            dimension_semantics=("arbitrary",), vmem_limit_bytes=VMEM_LIMIT),
        name="attn_latent_layer",
    )(sink, x, mod, nw, wqkvt, wz, wout, qw, kw, cos, sin, ck, cv)


def kernel(x_prompt, x_sample, cache_k_l1, cache_v_l1, c, c_ctx, norm_w_l0, w_mod_l0, b_mod_l0,
           w_in_l0, w_out_l0, norm_w_l1, w_mod_l1, b_mod_l1, w_in_l1, q_norm_w_l1, k_norm_w_l1,
           sink_l1, w_out_l1):
    nb_ctx, seq_ctx, _ = x_prompt.shape
    nb_lat = x_sample.shape[0]
    past = cache_k_l1.shape[1]
    cond8 = jnp.concatenate(
        [c_ctx[None, :], c, jnp.zeros((8 - 1 - nb_lat, D_MODEL), F32)], axis=0)
    mod0 = _modulation(cond8, w_mod_l0, b_mod_l0)
    mod1 = _modulation(cond8, w_mod_l1, b_mod_l1)

    nw0 = norm_w_l0.reshape(1, D_MODEL)
    nw1 = norm_w_l1.reshape(1, D_MODEL)
    win0 = w_in_l0.astype(BF16)
    wout0 = w_out_l0.astype(BF16)
    wout1 = w_out_l1.astype(BF16)
    wqkvt1 = w_in_l1[:, :BRANCH + 2 * KV_W].T.astype(BF16)
    wvz1 = w_in_l1[:, BRANCH + KV_W:].astype(BF16)
    wz1 = w_in_l1[:, BRANCH + 2 * KV_W:].astype(BF16)
    qw_ctx = jnp.broadcast_to(q_norm_w_l1[:, None], (HEAD_DIM, seq_ctx))
    kw_ctx = jnp.broadcast_to(k_norm_w_l1[:, None], (HEAD_DIM, seq_ctx))
    qw_lat = jnp.broadcast_to(q_norm_w_l1[:, None], (HEAD_DIM, Q_BLOCK))
    kw_lat = jnp.broadcast_to(k_norm_w_l1[:, None], (HEAD_DIM, Q_BLOCK))

    xp = _fourier_layer(x_prompt, mod0, 0, nw0, win0, wout0)
    xs = _fourier_layer(x_sample, mod0, 1, nw0, win0, wout0)

    xp, new_k, new_v = _attn_ctx_layer(xp, mod1, nw1, wqkvt1, wvz1, wout1, qw_ctx, kw_ctx, sink_l1)
    xs = _attn_lat_layer(xs, mod1, nw1, wqkvt1, wz1, wout1, qw_lat, kw_lat, sink_l1,
                         cache_k_l1.reshape(nb_lat, past, KV_W),
                         cache_v_l1.reshape(nb_lat, past, KV_W))
    new_k = new_k.reshape(nb_ctx, seq_ctx, N_KV, HEAD_DIM)
    new_v = new_v.reshape(nb_ctx, seq_ctx, N_KV, HEAD_DIM)
    return (xp, xs, new_k, new_v)
```

```python
import functools

import numpy as np
import jax
import jax.numpy as jnp
from jax import lax
from jax.experimental import pallas as pl
from jax.experimental.pallas import tpu as pltpu

D_MODEL = 1024
BRANCH = 1024
N_GROUPS = 4
GROUP_W = BRANCH // N_GROUPS
HEAD_DIM = 64
N_HEADS = 16
N_KV = 4
GQA = N_HEADS // N_KV
KV_W = N_KV * HEAD_DIM
GRID_W = 64
WINDOW = 128
BLOCK = 128
ROPE_THETA = 10000.0
EPS = 1e-6
NEG_INF = -1e30
LANES = 128
ROW_CHUNK = 256
Q_BLOCK = 256
VMEM_LIMIT = 56 * 1024 * 1024
MOD_ROWS = 8
MOD_K_CHUNK = 256
LOG2E = float(np.log2(np.e))

F32 = jnp.float32
BF16 = jnp.bfloat16


def _dot(a, b):
    return jnp.dot(a, b, preferred_element_type=F32)


def _dot_nt(a, b):
    return lax.dot_general(a, b, (((1,), (1,)), ((), ())), preferred_element_type=F32)


def _mod_row(mod_ref, per_request):
    if per_request:
        return mod_ref[0, pl.ds(1 + pl.program_id(0), 1), :]
    return mod_ref[0, 0:1, :]


def _mod_spec(layer):
    return pl.BlockSpec((1, MOD_ROWS, 3 * D_MODEL), lambda b: (layer, 0, 0))


def _mod_norm(x, nw, mod):
    shift = mod[:, :D_MODEL]
    scale = mod[:, D_MODEL:2 * D_MODEL]
    y = x * lax.rsqrt(jnp.mean(x * x, axis=-1, keepdims=True) + EPS)
    return (y * nw) * (1.0 + scale) + shift


def _mod_kernel(cctx_ref, c_ref, w0_ref, w1_ref, b0_ref, b1_ref, o_ref, cond_scr):
    layer = pl.program_id(0)
    k = pl.program_id(1)
    n_lat = c_ref.shape[0]
    cond_scr[...] = jnp.zeros_like(cond_scr)
    cond_scr[0:1, :] = cctx_ref[...]
    cond_scr[1:1 + n_lat, :] = c_ref[...]
    s = jax.nn.silu(cond_scr[...]).astype(BF16)

    def accumulate(w_ref, b_ref):
        @pl.when(k == 0)
        def _():
            o_ref[0] = jnp.broadcast_to(b_ref[...], o_ref.shape[1:])
        o_ref[0] += _dot(s, w_ref[...].astype(BF16))

    @pl.when(layer == 0)
    def _():
        accumulate(w0_ref, b0_ref)

    @pl.when(layer == 1)
    def _():
        accumulate(w1_ref, b1_ref)


def _modulation(c_ctx, c, w_mod_l0, b_mod_l0, w_mod_l1, b_mod_l1):
    n = w_mod_l0.shape[1]
    n_lat = c.shape[0]
    nk = D_MODEL // MOD_K_CHUNK
    return pl.pallas_call(
        _mod_kernel,
        grid=(2, nk),
        in_specs=[
            pl.BlockSpec((1, MOD_K_CHUNK), lambda l, k: (0, k)),
            pl.BlockSpec((n_lat, MOD_K_CHUNK), lambda l, k: (0, k)),
            pl.BlockSpec((MOD_K_CHUNK, n), lambda l, k: (jnp.where(l == 0, k, nk - 1), 0)),
            pl.BlockSpec((MOD_K_CHUNK, n), lambda l, k: (jnp.where(l == 1, k, 0), 0)),
            pl.BlockSpec((1, n), lambda l, k: (0, 0)),
            pl.BlockSpec((1, n), lambda l, k: (0, 0)),
        ],
        out_specs=pl.BlockSpec((1, MOD_ROWS, n), lambda l, k: (l, 0, 0)),
        out_shape=jax.ShapeDtypeStruct((2, MOD_ROWS, n), F32),
        scratch_shapes=[pltpu.VMEM((MOD_ROWS, MOD_K_CHUNK), F32)],
        compiler_params=pltpu.CompilerParams(dimension_semantics=("arbitrary", "arbitrary")),
        name="modulation",
    )(c_ctx.reshape(1, D_MODEL), c, w_mod_l0, w_mod_l1, b_mod_l0.reshape(1, n), b_mod_l1.reshape(1, n))


def _fourier_kernel(x_ref, mod_ref, nw_ref, win_ref, wout_ref, mc_ref, ls_ref, o_ref,
                    ab_scr, z_scr, *, seq, per_request):
    mod = _mod_row(mod_ref, per_request)
    gate = mod[:, 2 * D_MODEL:]
    nw = nw_ref[...]
    n_chunks = seq // ROW_CHUNK
    for c in range(n_chunks):
        rows = slice(c * ROW_CHUNK, (c + 1) * ROW_CHUNK)
        h = _mod_norm(x_ref[0, rows, :], nw, mod).astype(BF16)
        uz = _dot(h, win_ref[...])
        z_scr[rows, :] = uz[:, BRANCH:]
        u = uz[:, :BRANCH].astype(BF16)
        for g in range(N_GROUPS):
            cols = slice(g * GROUP_W, (g + 1) * GROUP_W)
            t = _dot(u[:, cols], mc_ref[...])
            ab_scr[rows, cols] = t[:, :GROUP_W].astype(BF16)
            ab_scr[seq + c * ROW_CHUNK:seq + (c + 1) * ROW_CHUNK, cols] = t[:, GROUP_W:].astype(BF16)
    for c in range(n_chunks):
        rows = slice(c * ROW_CHUNK, (c + 1) * ROW_CHUNK)
        y = _dot(ls_ref[rows, :], ab_scr[...])
        y = (y * jax.nn.silu(z_scr[rows, :])).astype(BF16)
        o_ref[0, rows, :] = x_ref[0, rows, :] + gate * _dot(y, wout_ref[...])


def _dft_tables(seq):
    k = np.arange(GROUP_W)
    ang = 2.0 * np.pi * ((k[:, None] * k[None, :]) % GROUP_W) / GROUP_W
    mc = np.concatenate([np.cos(ang), np.sin(ang)], axis=1) / np.sqrt(GROUP_W)
    n = np.arange(seq)
    ang = 2.0 * np.pi * ((n[:, None] * n[None, :]) % seq) / seq
    ls = np.concatenate([np.cos(ang), -np.sin(ang)], axis=1) / np.sqrt(seq)
    return mc.astype(np.float32), ls.astype(np.float32)


def _const_spec(shape):
    return pl.BlockSpec(shape, lambda b: (0,) * len(shape))


def _fourier_layer(x, mod, per_request, nw, win, wout):
    nb, seq, _ = x.shape
    mc, ls = _dft_tables(seq)
    mc = jnp.asarray(mc).astype(BF16)
    ls = jnp.asarray(ls).astype(BF16)
    kern = functools.partial(_fourier_kernel, seq=seq, per_request=per_request)
    return pl.pallas_call(
        kern,
        grid=(nb,),
        in_specs=[
            pl.BlockSpec((1, seq, D_MODEL), lambda b: (b, 0, 0)),
            _mod_spec(0),
            _const_spec((1, D_MODEL)),
            _const_spec((D_MODEL, 2 * BRANCH)),
            _const_spec((BRANCH, D_MODEL)),
            _const_spec((GROUP_W, 2 * GROUP_W)),
            _const_spec((seq, 2 * seq)),
        ],
        out_specs=pl.BlockSpec((1, seq, D_MODEL), lambda b: (b, 0, 0)),
        out_shape=jax.ShapeDtypeStruct(x.shape, F32),
        scratch_shapes=[
            pltpu.VMEM((2 * seq, BRANCH), BF16),
            pltpu.VMEM((seq, BRANCH), F32),
        ],
        compiler_params=pltpu.CompilerParams(
            dimension_semantics=("arbitrary",), vmem_limit_bytes=VMEM_LIMIT),
        name=f"fourier_layer_s{seq}",
    )(x, mod, nw, win, wout, mc, ls)


def _head_rms(t, w):
    return (t * lax.rsqrt(jnp.mean(t * t, axis=0, keepdims=True) + EPS)) * w


def _rope_t(t, cos, sin):
    half = HEAD_DIM // 2
    x1, x2 = t[:half], t[half:]
    return jnp.concatenate([x1 * cos - x2 * sin, x1 * sin + x2 * cos], axis=0)


def _head_scores(qn, g, keys, masks):
    zeros = jnp.zeros_like(qn)
    qz = jnp.concatenate([qn, zeros] if g % 2 == 0 else [zeros, qn], axis=0)
    blk = slice((g // 2) * LANES, (g // 2 + 1) * LANES)
    scores = []
    for k, msk in zip(keys, masks):
        s = _dot(k[:, blk], qz)
        scores.append(s if msk is None else jnp.where(msk, s, NEG_INF))
    return scores


ONES_ROWS = 16


def _with_ones_rows(vt):
    return jnp.concatenate([vt, jnp.ones((ONES_ROWS, vt.shape[1]), vt.dtype)], axis=0)


def _head_softmax_pv(scores, values_t, sink2):
    m = sink2
    for s in scores:
        m = jnp.maximum(m, jnp.max(s, axis=0, keepdims=True))
    acc = None
    for s, vt in zip(scores, values_t):
        pv = _dot(vt, jnp.exp2(s - m).astype(BF16))
        acc = pv if acc is None else acc + pv
    den = acc[HEAD_DIM:HEAD_DIM + 1] + jnp.exp2(sink2 - m)
    return acc[:HEAD_DIM] * (1.0 / den)


def _attend_heads(scores_fn, finish_fn):
    pending = [scores_fn(hd) for hd in range(GQA)]
    for g in range(N_KV):
        nxt = [scores_fn(hd) for hd in range((g + 1) * GQA, (g + 2) * GQA)] if g + 1 < N_KV else None
        for i, sc in enumerate(pending):
            finish_fn(g * GQA + i, sc)
        pending = nxt


def _gate_out(x, o, z, gate, wout):
    y = (o * jax.nn.silu(z)).astype(BF16)
    return x + gate * _dot(y, wout)


def _attn_ctx_kernel(sink_ref, x_ref, mod_ref, nw_ref, wqkvt_ref, wz_ref, wout_ref, qw_ref, kw_ref,
                     o_ref, kto_ref, vto_ref, ot_scr):
    seq = x_ref.shape[1]
    mod = _mod_row(mod_ref, False)
    gate = mod[:, 2 * D_MODEL:]
    x = x_ref[0]
    h = _mod_norm(x, nw_ref[...], mod).astype(BF16)
    qkvt = _dot_nt(wqkvt_ref[...], h)
    z = _dot(h, wz_ref[...])
    kw = jnp.broadcast_to(kw_ref[...], (HEAD_DIM, seq))
    knt = jnp.concatenate(
        [_head_rms(qkvt[BRANCH + g * HEAD_DIM:BRANCH + (g + 1) * HEAD_DIM], kw) for g in range(N_KV)],
        axis=0)
    kto_ref[0] = knt
    vto_ref[0] = qkvt[BRANCH + KV_W:]
    kb = knt.T.astype(BF16)
    vt = qkvt[BRANCH + KV_W:].astype(BF16)
    vts = [_with_ones_rows(vt[g * HEAD_DIM:(g + 1) * HEAD_DIM]) for g in range(N_KV)]
    qw = jnp.broadcast_to(qw_ref[...], (HEAD_DIM, seq))

    def scores_fn(hd):
        rows = slice(hd * HEAD_DIM, (hd + 1) * HEAD_DIM)
        qn = (_head_rms(qkvt[rows], qw) * (HEAD_DIM ** -0.5 * LOG2E)).astype(BF16)
        return _head_scores(qn, hd // GQA, [kb], [None])

    def finish_fn(hd, sc):
        ot_scr[hd * HEAD_DIM:(hd + 1) * HEAD_DIM, :] = _head_softmax_pv(
            sc, [vts[hd // GQA]], sink_ref[hd] * LOG2E)

    _attend_heads(scores_fn, finish_fn)
    o_ref[0] = _gate_out(x, ot_scr[...].T, z, gate, wout_ref[...])


def _attn_ctx_layer(x, mod, nw, wqkvt, wz, wout, qw, kw, sink):
    nb, seq, _ = x.shape
    return pl.pallas_call(
        _attn_ctx_kernel,
        grid=(nb,),
        in_specs=[
            pl.BlockSpec(memory_space=pltpu.SMEM),
            pl.BlockSpec((1, seq, D_MODEL), lambda b: (b, 0, 0)),
            _mod_spec(1),
            _const_spec((1, D_MODEL)),
            _const_spec((BRANCH + 2 * KV_W, D_MODEL)),
            _const_spec((D_MODEL, BRANCH)),
            _const_spec((BRANCH, D_MODEL)),
            _const_spec((HEAD_DIM, 1)),
            _const_spec((HEAD_DIM, 1)),
        ],
        out_specs=[
            pl.BlockSpec((1, seq, D_MODEL), lambda b: (b, 0, 0)),
            pl.BlockSpec((1, KV_W, seq), lambda b: (b, 0, 0)),
            pl.BlockSpec((1, KV_W, seq), lambda b: (b, 0, 0)),
        ],
        out_shape=[
            jax.ShapeDtypeStruct(x.shape, F32),
            jax.ShapeDtypeStruct((nb, KV_W, seq), F32),
            jax.ShapeDtypeStruct((nb, KV_W, seq), F32),
        ],
        scratch_shapes=[pltpu.VMEM((BRANCH, seq), F32)],
        compiler_params=pltpu.CompilerParams(
            dimension_semantics=("arbitrary",), vmem_limit_bytes=VMEM_LIMIT),
        name="attn_context_layer",
    )(sink, x, mod, nw, wqkvt, wz, wout, qw, kw)


def _attn_lat_kernel(sink_ref, x_ref, mod_ref, nw_ref, wqkvt_ref, wz_ref, wout_ref, qw_ref, kw_ref,
                     cos_ref, sin_ref, ck_ref, cv_ref, o_ref,
                     q_scr, z_scr, ot_scr, k_scr, vt_scr, *, seq):
    mod = _mod_row(mod_ref, True)
    gate = mod[:, 2 * D_MODEL:]
    nw = nw_ref[...]
    qw = jnp.broadcast_to(qw_ref[...], (HEAD_DIM, Q_BLOCK))
    kw = jnp.broadcast_to(kw_ref[...], (HEAD_DIM, Q_BLOCK))
    n_blocks = seq // Q_BLOCK
    kv_blocks = seq // BLOCK
    k_scr[0:BLOCK, :] = jnp.zeros((BLOCK, KV_W), BF16)
    k_scr[BLOCK + seq:2 * BLOCK + seq, :] = jnp.zeros((BLOCK, KV_W), BF16)
    vt_scr[0] = jnp.zeros((KV_W, BLOCK), BF16)
    vt_scr[kv_blocks + 1] = jnp.zeros((KV_W, BLOCK), BF16)

    for c in range(n_blocks):
        rows = slice(c * Q_BLOCK, (c + 1) * Q_BLOCK)
        h = _mod_norm(x_ref[0, rows, :], nw, mod).astype(BF16)
        z_scr[rows, :] = _dot(h, wz_ref[...])
        qkvt = _dot_nt(wqkvt_ref[...], h)
        cos = cos_ref[:, rows]
        sin = sin_ref[:, rows]
        for hd in range(N_HEADS):
            hr = slice(hd * HEAD_DIM, (hd + 1) * HEAD_DIM)
            t = _rope_t(_head_rms(qkvt[hr], qw), cos, sin)
            q_scr[c, hr, :] = (t * (HEAD_DIM ** -0.5 * LOG2E)).astype(BF16)
        knt = jnp.concatenate(
            [_rope_t(_head_rms(qkvt[BRANCH + g * HEAD_DIM:BRANCH + (g + 1) * HEAD_DIM], kw), cos, sin)
             for g in range(N_KV)], axis=0)
        k_scr[BLOCK + c * Q_BLOCK:BLOCK + (c + 1) * Q_BLOCK, :] = knt.T.astype(BF16)
        vt = qkvt[BRANCH + KV_W:].astype(BF16)
        for j in range(Q_BLOCK // BLOCK):
            vt_scr[1 + c * (Q_BLOCK // BLOCK) + j] = vt[:, j * BLOCK:(j + 1) * BLOCK]

    ckb = ck_ref[0].T.astype(BF16)
    cvt = cv_ref[0].astype(BF16)
    cvts = [_with_ones_rows(cvt[g * HEAD_DIM:(g + 1) * HEAD_DIM]) for g in range(N_KV)]

    win_len = Q_BLOCK + 2 * BLOCK
    kj = lax.broadcasted_iota(jnp.int32, (win_len, Q_BLOCK), 0)
    qi = lax.broadcasted_iota(jnp.int32, (win_len, Q_BLOCK), 1)
    rel = kj - BLOCK - qi
    band = (rel >= -WINDOW) & (rel <= WINDOW)

    def q_block(n, carry):
        r0 = pl.multiple_of(n * Q_BLOCK, Q_BLOCK)
        kpos = kj + (r0 - BLOCK)
        valid = band & (kpos >= 0) & (kpos < seq)
        kwin = k_scr[pl.ds(r0, win_len), :]
        vwin = jnp.concatenate(
            [vt_scr[n * (Q_BLOCK // BLOCK) + j] for j in range(win_len // BLOCK)], axis=1)
        vwins = [_with_ones_rows(vwin[g * HEAD_DIM:(g + 1) * HEAD_DIM]) for g in range(N_KV)]

        def scores_fn(hd):
            return _head_scores(q_scr[n, hd * HEAD_DIM:(hd + 1) * HEAD_DIM, :], hd // GQA,
                                [kwin, ckb], [valid, None])

        def finish_fn(hd, sc):
            g = hd // GQA
            ot_scr[n, hd * HEAD_DIM:(hd + 1) * HEAD_DIM, :] = _head_softmax_pv(
                sc, [vwins[g], cvts[g]], sink_ref[hd] * LOG2E)

        _attend_heads(scores_fn, finish_fn)
        return carry

    lax.fori_loop(0, n_blocks, q_block, 0)

    for c in range(n_blocks):
        rows = slice(c * Q_BLOCK, (c + 1) * Q_BLOCK)
        o_ref[0, rows, :] = _gate_out(x_ref[0, rows, :], ot_scr[c].T, z_scr[rows, :], gate, wout_ref[...])


def _rope_tables_t(seq):
    pos = np.arange(seq)
    n_freq = HEAD_DIM // 4
    inv = ROPE_THETA ** (-np.arange(n_freq, dtype=np.float64) / n_freq)
    ang = np.concatenate([(pos // GRID_W)[:, None] * inv, (pos % GRID_W)[:, None] * inv], axis=-1)
    return np.cos(ang).T.astype(np.float32), np.sin(ang).T.astype(np.float32)


def _attn_lat_layer(x, mod, nw, wqkvt, wz, wout, qw, kw, sink, ckt, cvt):
    nb, seq, _ = x.shape
    past = ckt.shape[2]
    cos, sin = (jnp.asarray(t) for t in _rope_tables_t(seq))
    kern = functools.partial(_attn_lat_kernel, seq=seq)
    n_blocks = seq // Q_BLOCK
    return pl.pallas_call(
        kern,
        grid=(nb,),
        in_specs=[
            pl.BlockSpec(memory_space=pltpu.SMEM),
            pl.BlockSpec((1, seq, D_MODEL), lambda b: (b, 0, 0)),
            _mod_spec(1),
            _const_spec((1, D_MODEL)),
            _const_spec((BRANCH + 2 * KV_W, D_MODEL)),
            _const_spec((D_MODEL, BRANCH)),
            _const_spec((BRANCH, D_MODEL)),
            _const_spec((HEAD_DIM, 1)),
            _const_spec((HEAD_DIM, 1)),
            _const_spec((HEAD_DIM // 2, seq)),
            _const_spec((HEAD_DIM // 2, seq)),
            pl.BlockSpec((1, KV_W, past), lambda b: (b, 0, 0)),
            pl.BlockSpec((1, KV_W, past), lambda b: (b, 0, 0)),
        ],
        out_specs=pl.BlockSpec((1, seq, D_MODEL), lambda b: (b, 0, 0)),
        out_shape=jax.ShapeDtypeStruct(x.shape, F32),
        scratch_shapes=[
            pltpu.VMEM((n_blocks, BRANCH, Q_BLOCK), BF16),
            pltpu.VMEM((seq, BRANCH), F32),
            pltpu.VMEM((n_blocks, BRANCH, Q_BLOCK), F32),
            pltpu.VMEM((seq + 2 * BLOCK, KV_W), BF16),
            pltpu.VMEM((seq // BLOCK + 2, KV_W, BLOCK), BF16),
        ],
        compiler_params=pltpu.CompilerParams(
            dimension_semantics=("arbitrary",), vmem_limit_bytes=VMEM_LIMIT),
        name="attn_latent_layer",
    )(sink, x, mod, nw, wqkvt, wz, wout, qw, kw, cos, sin, ckt, cvt)


def kernel(x_prompt, x_sample, cache_k_l1, cache_v_l1, c, c_ctx, norm_w_l0, w_mod_l0, b_mod_l0,
           w_in_l0, w_out_l0, norm_w_l1, w_mod_l1, b_mod_l1, w_in_l1, q_norm_w_l1, k_norm_w_l1,
           sink_l1, w_out_l1):
    nb_ctx, seq_ctx, _ = x_prompt.shape
    nb_lat = x_sample.shape[0]
    past = cache_k_l1.shape[1]
    assert 1 + nb_lat <= MOD_ROWS
    mod = _modulation(c_ctx, c, w_mod_l0, b_mod_l0, w_mod_l1, b_mod_l1)

    nw0 = norm_w_l0.reshape(1, D_MODEL)
    nw1 = norm_w_l1.reshape(1, D_MODEL)
    win0 = w_in_l0.astype(BF16)
    wout0 = w_out_l0.astype(BF16)
    wout1 = w_out_l1.astype(BF16)
    wqkvt1 = w_in_l1[:, :BRANCH + 2 * KV_W].T.astype(BF16)
    wz1 = w_in_l1[:, BRANCH + 2 * KV_W:].astype(BF16)
    qw = q_norm_w_l1.reshape(HEAD_DIM, 1)
    kw = k_norm_w_l1.reshape(HEAD_DIM, 1)

    xp = _fourier_layer(x_prompt, mod, False, nw0, win0, wout0)
    xs = _fourier_layer(x_sample, mod, True, nw0, win0, wout0)

    def to_feature_major(t):
        return jnp.transpose(t, (0, 2, 3, 1)).reshape(t.shape[0], KV_W, t.shape[1])

    def from_feature_major(t):
        return jnp.transpose(t.reshape(t.shape[0], N_KV, HEAD_DIM, t.shape[2]), (0, 3, 1, 2))

    xp, new_kt, new_vt = _attn_ctx_layer(xp, mod, nw1, wqkvt1, wz1, wout1, qw, kw, sink_l1)
    xs = _attn_lat_layer(xs, mod, nw1, wqkvt1, wz1, wout1, qw, kw, sink_l1,
                         to_feature_major(cache_k_l1), to_feature_major(cache_v_l1))
    return (xp, xs, from_feature_major(new_kt), from_feature_major(new_vt))
```

```python
import functools

import numpy as np
import jax
import jax.numpy as jnp
from jax import lax
from jax.experimental import pallas as pl
from jax.experimental.pallas import tpu as pltpu

D_MODEL = 1024
BRANCH = 1024
N_GROUPS = 4
GROUP_W = BRANCH // N_GROUPS
HEAD_DIM = 64
N_HEADS = 16
N_KV = 4
GQA = N_HEADS // N_KV
KV_W = N_KV * HEAD_DIM
GRID_W = 64
WINDOW = 128
BLOCK = 128
ROPE_THETA = 10000.0
EPS = 1e-6
NEG_INF = -1e30
LANES = 128
ROW_CHUNK = 256
Q_BLOCK = 256
CTX_SEQS_PER_STEP = 2
VMEM_LIMIT = 56 * 1024 * 1024
MOD_ROWS = 8
MOD_K_CHUNK = 256
LOG2E = float(np.log2(np.e))

F32 = jnp.float32
BF16 = jnp.bfloat16


def _dot(a, b):
    return jnp.dot(a, b, preferred_element_type=F32)


def _dot_nt(a, b):
    return lax.dot_general(a, b, (((1,), (1,)), ((), ())), preferred_element_type=F32)


def _mod_row(mod_ref, per_request):
    if per_request:
        return mod_ref[0, pl.ds(1 + pl.program_id(0), 1), :]
    return mod_ref[0, 0:1, :]


def _mod_spec(layer):
    return pl.BlockSpec((1, MOD_ROWS, 3 * D_MODEL), lambda b: (layer, 0, 0))


def _mod_norm(x, nw, mod):
    shift = mod[:, :D_MODEL]
    scale = mod[:, D_MODEL:2 * D_MODEL]
    y = x * lax.rsqrt(jnp.mean(x * x, axis=-1, keepdims=True) + EPS)
    return (y * nw) * (1.0 + scale) + shift


def _mod_kernel(cctx_ref, c_ref, w0_ref, w1_ref, b0_ref, b1_ref, o_ref, cond_scr):
    layer = pl.program_id(0)
    k = pl.program_id(1)
    n_lat = c_ref.shape[0]
    cond_scr[...] = jnp.zeros_like(cond_scr)
    cond_scr[0:1, :] = cctx_ref[...]
    cond_scr[1:1 + n_lat, :] = c_ref[...]
    s = jax.nn.silu(cond_scr[...]).astype(BF16)

    def accumulate(w_ref, b_ref):
        @pl.when(k == 0)
        def _():
            o_ref[0] = jnp.broadcast_to(b_ref[...], o_ref.shape[1:])
        o_ref[0] += _dot(s, w_ref[...].astype(BF16))

    @pl.when(layer == 0)
    def _():
        accumulate(w0_ref, b0_ref)

    @pl.when(layer == 1)
    def _():
        accumulate(w1_ref, b1_ref)


def _modulation(c_ctx, c, w_mod_l0, b_mod_l0, w_mod_l1, b_mod_l1):
    n = w_mod_l0.shape[1]
    n_lat = c.shape[0]
    nk = D_MODEL // MOD_K_CHUNK
    return pl.pallas_call(
        _mod_kernel,
        grid=(2, nk),
        in_specs=[
            pl.BlockSpec((1, MOD_K_CHUNK), lambda l, k: (0, k)),
            pl.BlockSpec((n_lat, MOD_K_CHUNK), lambda l, k: (0, k)),
            pl.BlockSpec((MOD_K_CHUNK, n), lambda l, k: (jnp.where(l == 0, k, nk - 1), 0)),
            pl.BlockSpec((MOD_K_CHUNK, n), lambda l, k: (jnp.where(l == 1, k, 0), 0)),
            pl.BlockSpec((1, n), lambda l, k: (0, 0)),
            pl.BlockSpec((1, n), lambda l, k: (0, 0)),
        ],
        out_specs=pl.BlockSpec((1, MOD_ROWS, n), lambda l, k: (l, 0, 0)),
        out_shape=jax.ShapeDtypeStruct((2, MOD_ROWS, n), F32),
        scratch_shapes=[pltpu.VMEM((MOD_ROWS, MOD_K_CHUNK), F32)],
        compiler_params=pltpu.CompilerParams(dimension_semantics=("arbitrary", "arbitrary")),
        name="modulation",
    )(c_ctx.reshape(1, D_MODEL), c, w_mod_l0, w_mod_l1, b_mod_l0.reshape(1, n), b_mod_l1.reshape(1, n))


def _fourier_kernel(x_ref, mod_ref, nw_ref, win_ref, wout_ref, mc_ref, ls_ref, o_ref,
                    ab_scr, z_scr, *, seq, per_request):
    mod = _mod_row(mod_ref, per_request)
    gate = mod[:, 2 * D_MODEL:]
    nw = nw_ref[...]
    n_seq = x_ref.shape[0]
    n_chunks = seq // ROW_CHUNK
    for i in range(n_seq):
        for c in range(n_chunks):
            rows = slice(c * ROW_CHUNK, (c + 1) * ROW_CHUNK)
            h = _mod_norm(x_ref[i, rows, :], nw, mod).astype(BF16)
            uz = _dot(h, win_ref[...])
            z_scr[i, rows, :] = uz[:, BRANCH:]
            u = uz[:, :BRANCH].astype(BF16)
            for g in range(N_GROUPS):
                cols = slice(g * GROUP_W, (g + 1) * GROUP_W)
                t = _dot(u[:, cols], mc_ref[...])
                ab_scr[i, rows, cols] = t[:, :GROUP_W].astype(BF16)
                ab_scr[i, seq + c * ROW_CHUNK:seq + (c + 1) * ROW_CHUNK, cols] = t[:, GROUP_W:].astype(BF16)
    for i in range(n_seq):
        for c in range(n_chunks):
            rows = slice(c * ROW_CHUNK, (c + 1) * ROW_CHUNK)
            y = _dot(ls_ref[rows, :], ab_scr[i])
            y = (y * jax.nn.silu(z_scr[i, rows, :])).astype(BF16)
            o_ref[i, rows, :] = x_ref[i, rows, :] + gate * _dot(y, wout_ref[...])


def _dft_tables(seq):
    k = np.arange(GROUP_W)
    ang = 2.0 * np.pi * ((k[:, None] * k[None, :]) % GROUP_W) / GROUP_W
    mc = np.concatenate([np.cos(ang), np.sin(ang)], axis=1) / np.sqrt(GROUP_W)
    n = np.arange(seq)
    ang = 2.0 * np.pi * ((n[:, None] * n[None, :]) % seq) / seq
    ls = np.concatenate([np.cos(ang), -np.sin(ang)], axis=1) / np.sqrt(seq)
    return mc.astype(np.float32), ls.astype(np.float32)


def _const_spec(shape):
    return pl.BlockSpec(shape, lambda b: (0,) * len(shape))


def _fourier_layer(x, mod, per_request, nw, win, wout):
    nb, seq, _ = x.shape
    n_seq = 1 if per_request else CTX_SEQS_PER_STEP
    assert nb % n_seq == 0
    mc, ls = _dft_tables(seq)
    mc = jnp.asarray(mc).astype(BF16)
    ls = jnp.asarray(ls).astype(BF16)
    kern = functools.partial(_fourier_kernel, seq=seq, per_request=per_request)
    return pl.pallas_call(
        kern,
        grid=(nb // n_seq,),
        in_specs=[
            pl.BlockSpec((n_seq, seq, D_MODEL), lambda b: (b, 0, 0)),
            _mod_spec(0),
            _const_spec((1, D_MODEL)),
            _const_spec((D_MODEL, 2 * BRANCH)),
            _const_spec((BRANCH, D_MODEL)),
            _const_spec((GROUP_W, 2 * GROUP_W)),
            _const_spec((seq, 2 * seq)),
        ],
        out_specs=pl.BlockSpec((n_seq, seq, D_MODEL), lambda b: (b, 0, 0)),
        out_shape=jax.ShapeDtypeStruct(x.shape, F32),
        scratch_shapes=[
            pltpu.VMEM((n_seq, 2 * seq, BRANCH), BF16),
            pltpu.VMEM((n_seq, seq, BRANCH), F32),
        ],
        compiler_params=pltpu.CompilerParams(
            dimension_semantics=("arbitrary",), vmem_limit_bytes=VMEM_LIMIT),
        name=f"fourier_layer_s{seq}",
    )(x, mod, nw, win, wout, mc, ls)


def _head_rms(t, w):
    return (t * lax.rsqrt(jnp.mean(t * t, axis=0, keepdims=True) + EPS)) * w


def _rope_t(t, cos, sin):
    half = HEAD_DIM // 2
    x1, x2 = t[:half], t[half:]
    return jnp.concatenate([x1 * cos - x2 * sin, x1 * sin + x2 * cos], axis=0)


def _head_scores(qn, g, keys, masks):
    zeros = jnp.zeros_like(qn)
    qz = jnp.concatenate([qn, zeros] if g % 2 == 0 else [zeros, qn], axis=0)
    blk = slice((g // 2) * LANES, (g // 2 + 1) * LANES)
    scores = []
    for k, msk in zip(keys, masks):
        s = _dot(k[:, blk], qz)
        scores.append(s if msk is None else jnp.where(msk, s, NEG_INF))
    return scores


ONES_ROWS = 16


def _with_ones_rows(vt):
    return jnp.concatenate([vt, jnp.ones((ONES_ROWS, vt.shape[1]), vt.dtype)], axis=0)


def _head_softmax_pv(scores, values_t, sink2):
    m = sink2
    for s in scores:
        m = jnp.maximum(m, jnp.max(s, axis=0, keepdims=True))
    acc = None
    for s, vt in zip(scores, values_t):
        pv = _dot(vt, jnp.exp2(s - m).astype(BF16))
        acc = pv if acc is None else acc + pv
    den = acc[HEAD_DIM:HEAD_DIM + 1] + jnp.exp2(sink2 - m)
    return acc[:HEAD_DIM] * (1.0 / den)


def _attend_heads(scores_fn, finish_fn):
    pending = [scores_fn(hd) for hd in range(GQA)]
    for g in range(N_KV):
        nxt = [scores_fn(hd) for hd in range((g + 1) * GQA, (g + 2) * GQA)] if g + 1 < N_KV else None
        for i, sc in enumerate(pending):
            finish_fn(g * GQA + i, sc)
        pending = nxt


def _gate_out(x, o, z, gate, wout):
    y = (o * jax.nn.silu(z)).astype(BF16)
    return x + gate * _dot(y, wout)


def _attn_ctx_kernel(sink_ref, x_ref, mod_ref, nw_ref, wqkvt_ref, wz_ref, wout_ref, qw_ref, kw_ref,
                     o_ref, kto_ref, vto_ref, qkvt_scr, z_scr, ot_scr):
    seq = x_ref.shape[1]
    n_seq = x_ref.shape[0]
    mod = _mod_row(mod_ref, False)
    gate = mod[:, 2 * D_MODEL:]
    kw = jnp.broadcast_to(kw_ref[...], (HEAD_DIM, seq))
    qw = jnp.broadcast_to(qw_ref[...], (HEAD_DIM, seq))

    def project(i):
        h = _mod_norm(x_ref[i], nw_ref[...], mod).astype(BF16)
        qkvt_scr[i] = _dot_nt(wqkvt_ref[...], h)
        z_scr[i] = _dot(h, wz_ref[...])

    def attend(i):
        knt = jnp.concatenate(
            [_head_rms(qkvt_scr[i, BRANCH + g * HEAD_DIM:BRANCH + (g + 1) * HEAD_DIM, :], kw)
             for g in range(N_KV)], axis=0)
        kto_ref[i] = knt
        vtf = qkvt_scr[i, BRANCH + KV_W:, :]
        vto_ref[i] = vtf
        kb = knt.T.astype(BF16)
        vt = vtf.astype(BF16)
        vts = [_with_ones_rows(vt[g * HEAD_DIM:(g + 1) * HEAD_DIM]) for g in range(N_KV)]

        def scores_fn(hd):
            t = qkvt_scr[i, hd * HEAD_DIM:(hd + 1) * HEAD_DIM, :]
            qn = (_head_rms(t, qw) * (HEAD_DIM ** -0.5 * LOG2E)).astype(BF16)
            return _head_scores(qn, hd // GQA, [kb], [None])

        def finish_fn(hd, sc):
            ot_scr[i, hd * HEAD_DIM:(hd + 1) * HEAD_DIM, :] = _head_softmax_pv(
                sc, [vts[hd // GQA]], sink_ref[hd] * LOG2E)

        _attend_heads(scores_fn, finish_fn)

    def output(i):
        o_ref[i] = _gate_out(x_ref[i], ot_scr[i].T, z_scr[i], gate, wout_ref[...])

    for phase in (project, attend, output):
        for i in range(n_seq):
            phase(i)


def _attn_ctx_layer(x, mod, nw, wqkvt, wz, wout, qw, kw, sink):
    nb, seq, _ = x.shape
    n_seq = CTX_SEQS_PER_STEP
    assert nb % n_seq == 0
    return pl.pallas_call(
        _attn_ctx_kernel,
        grid=(nb // n_seq,),
        in_specs=[
            pl.BlockSpec(memory_space=pltpu.SMEM),
            pl.BlockSpec((n_seq, seq, D_MODEL), lambda b: (b, 0, 0)),
            _mod_spec(1),
            _const_spec((1, D_MODEL)),
            _const_spec((BRANCH + 2 * KV_W, D_MODEL)),
            _const_spec((D_MODEL, BRANCH)),
            _const_spec((BRANCH, D_MODEL)),
            _const_spec((HEAD_DIM, 1)),
            _const_spec((HEAD_DIM, 1)),
        ],
        out_specs=[
            pl.BlockSpec((n_seq, seq, D_MODEL), lambda b: (b, 0, 0)),
            pl.BlockSpec((n_seq, KV_W, seq), lambda b: (b, 0, 0)),
            pl.BlockSpec((n_seq, KV_W, seq), lambda b: (b, 0, 0)),
        ],
        out_shape=[
            jax.ShapeDtypeStruct(x.shape, F32),
            jax.ShapeDtypeStruct((nb, KV_W, seq), F32),
            jax.ShapeDtypeStruct((nb, KV_W, seq), F32),
        ],
        scratch_shapes=[
            pltpu.VMEM((n_seq, BRANCH + 2 * KV_W, seq), F32),
            pltpu.VMEM((n_seq, seq, BRANCH), F32),
            pltpu.VMEM((n_seq, BRANCH, seq), F32),
        ],
        compiler_params=pltpu.CompilerParams(
            dimension_semantics=("arbitrary",), vmem_limit_bytes=VMEM_LIMIT),
        name="attn_context_layer",
    )(sink, x, mod, nw, wqkvt, wz, wout, qw, kw)


def _attn_lat_kernel(sink_ref, x_ref, mod_ref, nw_ref, wqkvt_ref, wz_ref, wout_ref, qw_ref, kw_ref,
                     cos_ref, sin_ref, ck_ref, cv_ref, o_ref,
                     q_scr, z_scr, ot_scr, k_scr, vt_scr, *, seq):
    mod = _mod_row(mod_ref, True)
    gate = mod[:, 2 * D_MODEL:]
    nw = nw_ref[...]
    qw = jnp.broadcast_to(qw_ref[...], (HEAD_DIM, Q_BLOCK))
    kw = jnp.broadcast_to(kw_ref[...], (HEAD_DIM, Q_BLOCK))
    n_blocks = seq // Q_BLOCK
    kv_blocks = seq // BLOCK
    k_scr[0:BLOCK, :] = jnp.zeros((BLOCK, KV_W), BF16)
    k_scr[BLOCK + seq:2 * BLOCK + seq, :] = jnp.zeros((BLOCK, KV_W), BF16)
    vt_scr[0] = jnp.zeros((KV_W, BLOCK), BF16)
    vt_scr[kv_blocks + 1] = jnp.zeros((KV_W, BLOCK), BF16)

    for c in range(n_blocks):
        rows = slice(c * Q_BLOCK, (c + 1) * Q_BLOCK)
        h = _mod_norm(x_ref[0, rows, :], nw, mod).astype(BF16)
        z_scr[rows, :] = _dot(h, wz_ref[...])
        qkvt = _dot_nt(wqkvt_ref[...], h)
        cos = cos_ref[:, rows]
        sin = sin_ref[:, rows]
        for hd in range(N_HEADS):
            hr = slice(hd * HEAD_DIM, (hd + 1) * HEAD_DIM)
            t = _rope_t(_head_rms(qkvt[hr], qw), cos, sin)
            q_scr[c, hr, :] = (t * (HEAD_DIM ** -0.5 * LOG2E)).astype(BF16)
        knt = jnp.concatenate(
            [_rope_t(_head_rms(qkvt[BRANCH + g * HEAD_DIM:BRANCH + (g + 1) * HEAD_DIM], kw), cos, sin)
             for g in range(N_KV)], axis=0)
        k_scr[BLOCK + c * Q_BLOCK:BLOCK + (c + 1) * Q_BLOCK, :] = knt.T.astype(BF16)
        vt = qkvt[BRANCH + KV_W:].astype(BF16)
        for j in range(Q_BLOCK // BLOCK):
            vt_scr[1 + c * (Q_BLOCK // BLOCK) + j] = vt[:, j * BLOCK:(j + 1) * BLOCK]

    ckb = ck_ref[0].T.astype(BF16)
    cvt = cv_ref[0].astype(BF16)
    cvts = [_with_ones_rows(cvt[g * HEAD_DIM:(g + 1) * HEAD_DIM]) for g in range(N_KV)]

    win_len = Q_BLOCK + 2 * BLOCK
    kj = lax.broadcasted_iota(jnp.int32, (win_len, Q_BLOCK), 0)
    qi = lax.broadcasted_iota(jnp.int32, (win_len, Q_BLOCK), 1)
    rel = kj - BLOCK - qi
    band = (rel >= -WINDOW) & (rel <= WINDOW)

    def q_block(n, carry):
        r0 = pl.multiple_of(n * Q_BLOCK, Q_BLOCK)
        kpos = kj + (r0 - BLOCK)
        valid = band & (kpos >= 0) & (kpos < seq)
        kwin = k_scr[pl.ds(r0, win_len), :]
        vwin = jnp.concatenate(
            [vt_scr[n * (Q_BLOCK // BLOCK) + j] for j in range(win_len // BLOCK)], axis=1)
        vwins = [_with_ones_rows(vwin[g * HEAD_DIM:(g + 1) * HEAD_DIM]) for g in range(N_KV)]

        def scores_fn(hd):
            return _head_scores(q_scr[n, hd * HEAD_DIM:(hd + 1) * HEAD_DIM, :], hd // GQA,
                                [kwin, ckb], [valid, None])

        def finish_fn(hd, sc):
            g = hd // GQA
            ot_scr[n, hd * HEAD_DIM:(hd + 1) * HEAD_DIM, :] = _head_softmax_pv(
                sc, [vwins[g], cvts[g]], sink_ref[hd] * LOG2E)

        _attend_heads(scores_fn, finish_fn)
        return carry

    lax.fori_loop(0, n_blocks, q_block, 0)

    for c in range(n_blocks):
        rows = slice(c * Q_BLOCK, (c + 1) * Q_BLOCK)
        o_ref[0, rows, :] = _gate_out(x_ref[0, rows, :], ot_scr[c].T, z_scr[rows, :], gate, wout_ref[...])


def _rope_tables_t(seq):
    pos = np.arange(seq)
    n_freq = HEAD_DIM // 4
    inv = ROPE_THETA ** (-np.arange(n_freq, dtype=np.float64) / n_freq)
    ang = np.concatenate([(pos // GRID_W)[:, None] * inv, (pos % GRID_W)[:, None] * inv], axis=-1)
    return np.cos(ang).T.astype(np.float32), np.sin(ang).T.astype(np.float32)


def _attn_lat_layer(x, mod, nw, wqkvt, wz, wout, qw, kw, sink, ckt, cvt):
    nb, seq, _ = x.shape
    past = ckt.shape[2]
    cos, sin = (jnp.asarray(t) for t in _rope_tables_t(seq))
    kern = functools.partial(_attn_lat_kernel, seq=seq)
    n_blocks = seq // Q_BLOCK
    return pl.pallas_call(
        kern,
        grid=(nb,),
        in_specs=[
            pl.BlockSpec(memory_space=pltpu.SMEM),
            pl.BlockSpec((1, seq, D_MODEL), lambda b: (b, 0, 0)),
            _mod_spec(1),
            _const_spec((1, D_MODEL)),
            _const_spec((BRANCH + 2 * KV_W, D_MODEL)),
            _const_spec((D_MODEL, BRANCH)),
            _const_spec((BRANCH, D_MODEL)),
            _const_spec((HEAD_DIM, 1)),
            _const_spec((HEAD_DIM, 1)),
            _const_spec((HEAD_DIM // 2, seq)),
            _const_spec((HEAD_DIM // 2, seq)),
            pl.BlockSpec((1, KV_W, past), lambda b: (b, 0, 0)),
            pl.BlockSpec((1, KV_W, past), lambda b: (b, 0, 0)),
        ],
        out_specs=pl.BlockSpec((1, seq, D_MODEL), lambda b: (b, 0, 0)),
        out_shape=jax.ShapeDtypeStruct(x.shape, F32),
        scratch_shapes=[
            pltpu.VMEM((n_blocks, BRANCH, Q_BLOCK), BF16),
            pltpu.VMEM((seq, BRANCH), F32),
            pltpu.VMEM((n_blocks, BRANCH, Q_BLOCK), F32),
            pltpu.VMEM((seq + 2 * BLOCK, KV_W), BF16),
            pltpu.VMEM((seq // BLOCK + 2, KV_W, BLOCK), BF16),
        ],
        compiler_params=pltpu.CompilerParams(
            dimension_semantics=("arbitrary",), vmem_limit_bytes=VMEM_LIMIT),
        name="attn_latent_layer",
    )(sink, x, mod, nw, wqkvt, wz, wout, qw, kw, cos, sin, ckt, cvt)


def kernel(x_prompt, x_sample, cache_k_l1, cache_v_l1, c, c_ctx, norm_w_l0, w_mod_l0, b_mod_l0,
           w_in_l0, w_out_l0, norm_w_l1, w_mod_l1, b_mod_l1, w_in_l1, q_norm_w_l1, k_norm_w_l1,
           sink_l1, w_out_l1):
    nb_ctx, seq_ctx, _ = x_prompt.shape
    nb_lat = x_sample.shape[0]
    past = cache_k_l1.shape[1]
    assert 1 + nb_lat <= MOD_ROWS
    mod = _modulation(c_ctx, c, w_mod_l0, b_mod_l0, w_mod_l1, b_mod_l1)

    nw0 = norm_w_l0.reshape(1, D_MODEL)
    nw1 = norm_w_l1.reshape(1, D_MODEL)
    win0 = w_in_l0.astype(BF16)
    wout0 = w_out_l0.astype(BF16)
    wout1 = w_out_l1.astype(BF16)
    wqkvt1 = w_in_l1[:, :BRANCH + 2 * KV_W].T.astype(BF16)
    wz1 = w_in_l1[:, BRANCH + 2 * KV_W:].astype(BF16)
    qw = q_norm_w_l1.reshape(HEAD_DIM, 1)
    kw = k_norm_w_l1.reshape(HEAD_DIM, 1)

    xp = _fourier_layer(x_prompt, mod, False, nw0, win0, wout0)
    xs = _fourier_layer(x_sample, mod, True, nw0, win0, wout0)

    def to_feature_major(t):
        return jnp.transpose(t, (0, 2, 3, 1)).reshape(t.shape[0], KV_W, t.shape[1])

    def from_feature_major(t):
        return jnp.transpose(t.reshape(t.shape[0], N_KV, HEAD_DIM, t.shape[2]), (0, 3, 1, 2))

    xp, new_kt, new_vt = _attn_ctx_layer(xp, mod, nw1, wqkvt1, wz1, wout1, qw, kw, sink_l1)
    xs = _attn_lat_layer(xs, mod, nw1, wqkvt1, wz1, wout1, qw, kw, sink_l1,
                         to_feature_major(cache_k_l1), to_feature_major(cache_v_l1))
    return (xp, xs, from_feature_major(new_kt), from_feature_major(new_vt))
```

```python
import functools

import numpy as np
import jax
import jax.numpy as jnp
from jax import lax
from jax.experimental import pallas as pl
from jax.experimental.pallas import tpu as pltpu

D_MODEL = 1024
BRANCH = 1024
N_GROUPS = 4
GROUP_W = BRANCH // N_GROUPS
HEAD_DIM = 64
N_HEADS = 16
N_KV = 4
GQA = N_HEADS // N_KV
KV_W = N_KV * HEAD_DIM
GRID_W = 64
WINDOW = 128
BLOCK = 128
ROPE_THETA = 10000.0
EPS = 1e-6
NEG_INF = -1e30
LANES = 128
ROW_CHUNK = 256
Q_BLOCK = 256
CTX_SEQS_PER_STEP = 4
VMEM_LIMIT = 56 * 1024 * 1024
MOD_ROWS = 8
MOD_K_CHUNK = 256
LOG2E = float(np.log2(np.e))

F32 = jnp.float32
BF16 = jnp.bfloat16


def _dot(a, b):
    return jnp.dot(a, b, preferred_element_type=F32)


def _dot_nt(a, b):
    return lax.dot_general(a, b, (((1,), (1,)), ((), ())), preferred_element_type=F32)


def _mod_row(mod_ref, per_request):
    if per_request:
        return mod_ref[0, pl.ds(1 + pl.program_id(0), 1), :]
    return mod_ref[0, 0:1, :]


def _mod_spec(layer):
    return pl.BlockSpec((1, MOD_ROWS, 3 * D_MODEL), lambda b: (layer, 0, 0))


def _mod_norm(x, nw, mod):
    shift = mod[:, :D_MODEL]
    scale = mod[:, D_MODEL:2 * D_MODEL]
    y = x * lax.rsqrt(jnp.mean(x * x, axis=-1, keepdims=True) + EPS)
    return (y * nw) * (1.0 + scale) + shift


def _mod_kernel(cctx_ref, c_ref, w0_ref, w1_ref, b0_ref, b1_ref, o_ref, cond_scr):
    layer = pl.program_id(0)
    k = pl.program_id(1)
    n_lat = c_ref.shape[0]
    cond_scr[...] = jnp.zeros_like(cond_scr)
    cond_scr[0:1, :] = cctx_ref[...]
    cond_scr[1:1 + n_lat, :] = c_ref[...]
    s = jax.nn.silu(cond_scr[...]).astype(BF16)

    def accumulate(w_ref, b_ref):
        @pl.when(k == 0)
        def _():
            o_ref[0] = jnp.broadcast_to(b_ref[...], o_ref.shape[1:])
        o_ref[0] += _dot(s, w_ref[...].astype(BF16))

    @pl.when(layer == 0)
    def _():
        accumulate(w0_ref, b0_ref)

    @pl.when(layer == 1)
    def _():
        accumulate(w1_ref, b1_ref)


def _modulation(c_ctx, c, w_mod_l0, b_mod_l0, w_mod_l1, b_mod_l1):
    n = w_mod_l0.shape[1]
    n_lat = c.shape[0]
    nk = D_MODEL // MOD_K_CHUNK
    return pl.pallas_call(
        _mod_kernel,
        grid=(2, nk),
        in_specs=[
            pl.BlockSpec((1, MOD_K_CHUNK), lambda l, k: (0, k)),
            pl.BlockSpec((n_lat, MOD_K_CHUNK), lambda l, k: (0, k)),
            pl.BlockSpec((MOD_K_CHUNK, n), lambda l, k: (jnp.where(l == 0, k, nk - 1), 0)),
            pl.BlockSpec((MOD_K_CHUNK, n), lambda l, k: (jnp.where(l == 1, k, 0), 0)),
            pl.BlockSpec((1, n), lambda l, k: (0, 0)),
            pl.BlockSpec((1, n), lambda l, k: (0, 0)),
        ],
        out_specs=pl.BlockSpec((1, MOD_ROWS, n), lambda l, k: (l, 0, 0)),
        out_shape=jax.ShapeDtypeStruct((2, MOD_ROWS, n), F32),
        scratch_shapes=[pltpu.VMEM((MOD_ROWS, MOD_K_CHUNK), F32)],
        compiler_params=pltpu.CompilerParams(dimension_semantics=("arbitrary", "arbitrary")),
        name="modulation",
    )(c_ctx.reshape(1, D_MODEL), c, w_mod_l0, w_mod_l1, b_mod_l0.reshape(1, n), b_mod_l1.reshape(1, n))


def _fourier_prep_kernel(x_ref, mod_ref, nw_ref, win_ref, wout_ref, mc_ref, ls_ref, win1_ref, wout1_ref,
                         o_ref, wqkvt1_ref, wz1_ref, wout1b_ref, ab_scr, z_scr, *, seq):
    w = win1_ref[...]
    wqkvt1_ref[...] = w[:, :BRANCH + 2 * KV_W].T.astype(BF16)
    wz1_ref[...] = w[:, BRANCH + 2 * KV_W:].astype(BF16)
    wout1b_ref[...] = wout1_ref[...].astype(BF16)
    _fourier_kernel(x_ref, mod_ref, nw_ref, win_ref, wout_ref, mc_ref, ls_ref, o_ref, ab_scr, z_scr,
                    seq=seq, per_request=False)


def _fourier_kernel(x_ref, mod_ref, nw_ref, win_ref, wout_ref, mc_ref, ls_ref, o_ref,
                    ab_scr, z_scr, *, seq, per_request):
    mod = _mod_row(mod_ref, per_request)
    gate = mod[:, 2 * D_MODEL:]
    nw = nw_ref[...]
    n_seq = x_ref.shape[0]
    n_chunks = seq // ROW_CHUNK
    for i in range(n_seq):
        for c in range(n_chunks):
            rows = slice(c * ROW_CHUNK, (c + 1) * ROW_CHUNK)
            h = _mod_norm(x_ref[i, rows, :], nw, mod).astype(BF16)
            uz = _dot(h, win_ref[...])
            z_scr[i, rows, :] = uz[:, BRANCH:]
            u = uz[:, :BRANCH].astype(BF16)
            for g in range(N_GROUPS):
                cols = slice(g * GROUP_W, (g + 1) * GROUP_W)
                t = _dot(u[:, cols], mc_ref[...])
                ab_scr[i, rows, cols] = t[:, :GROUP_W].astype(BF16)
                ab_scr[i, seq + c * ROW_CHUNK:seq + (c + 1) * ROW_CHUNK, cols] = t[:, GROUP_W:].astype(BF16)
    for i in range(n_seq):
        for c in range(n_chunks):
            rows = slice(c * ROW_CHUNK, (c + 1) * ROW_CHUNK)
            y = _dot(ls_ref[rows, :], ab_scr[i])
            y = (y * jax.nn.silu(z_scr[i, rows, :])).astype(BF16)
            o_ref[i, rows, :] = x_ref[i, rows, :] + gate * _dot(y, wout_ref[...])


def _dft_tables(seq):
    k = np.arange(GROUP_W)
    ang = 2.0 * np.pi * ((k[:, None] * k[None, :]) % GROUP_W) / GROUP_W
    mc = np.concatenate([np.cos(ang), np.sin(ang)], axis=1) / np.sqrt(GROUP_W)
    n = np.arange(seq)
    ang = 2.0 * np.pi * ((n[:, None] * n[None, :]) % seq) / seq
    ls = np.concatenate([np.cos(ang), -np.sin(ang)], axis=1) / np.sqrt(seq)
    return mc.astype(np.float32), ls.astype(np.float32)


def _const_spec(shape):
    return pl.BlockSpec(shape, lambda b: (0,) * len(shape))


def _fourier_layer(x, mod, per_request, nw, win, wout, next_weights=None):
    nb, seq, _ = x.shape
    n_seq = 1 if per_request else CTX_SEQS_PER_STEP
    assert nb % n_seq == 0
    n_steps = nb // n_seq
    mc, ls = _dft_tables(seq)
    mc = jnp.asarray(mc).astype(BF16)
    ls = jnp.asarray(ls).astype(BF16)
    in_specs = [
        pl.BlockSpec((n_seq, seq, D_MODEL), lambda b: (b, 0, 0)),
        _mod_spec(0),
        _const_spec((1, D_MODEL)),
        _const_spec((D_MODEL, 2 * BRANCH)),
        _const_spec((BRANCH, D_MODEL)),
        _const_spec((GROUP_W, 2 * GROUP_W)),
        _const_spec((seq, 2 * seq)),
    ]
    out_specs = [pl.BlockSpec((n_seq, seq, D_MODEL), lambda b: (b, 0, 0))]
    out_shape = [jax.ShapeDtypeStruct(x.shape, F32)]
    args = [x, mod, nw, win, wout, mc, ls]
    if next_weights is None:
        kern = functools.partial(_fourier_kernel, seq=seq, per_request=per_request)
    else:
        assert not per_request and D_MODEL % (n_steps * LANES) == 0
        rows = D_MODEL // n_steps
        n_qkvz = 2 * BRANCH + 2 * KV_W
        kern = functools.partial(_fourier_prep_kernel, seq=seq)
        in_specs += [pl.BlockSpec((rows, n_qkvz), lambda b: (b, 0)),
                     pl.BlockSpec((rows, D_MODEL), lambda b: (b, 0))]
        out_specs += [pl.BlockSpec((BRANCH + 2 * KV_W, rows), lambda b: (0, b)),
                      pl.BlockSpec((rows, BRANCH), lambda b: (b, 0)),
                      pl.BlockSpec((rows, D_MODEL), lambda b: (b, 0))]
        out_shape += [jax.ShapeDtypeStruct((BRANCH + 2 * KV_W, D_MODEL), BF16),
                      jax.ShapeDtypeStruct((D_MODEL, BRANCH), BF16),
                      jax.ShapeDtypeStruct((BRANCH, D_MODEL), BF16)]
        args += list(next_weights)
    outs = pl.pallas_call(
        kern,
        grid=(n_steps,),
        in_specs=in_specs,
        out_specs=out_specs,
        out_shape=out_shape,
        scratch_shapes=[
            pltpu.VMEM((n_seq, 2 * seq, BRANCH), BF16),
            pltpu.VMEM((n_seq, seq, BRANCH), F32),
        ],
        compiler_params=pltpu.CompilerParams(
            dimension_semantics=("arbitrary",), vmem_limit_bytes=VMEM_LIMIT),
        name=f"fourier_layer_s{seq}",
    )(*args)
    return outs[0] if next_weights is None else outs


def _head_rms(t, w):
    return (t * lax.rsqrt(jnp.mean(t * t, axis=0, keepdims=True) + EPS)) * w


def _rope_t(t, cos, sin):
    half = HEAD_DIM // 2
    x1, x2 = t[:half], t[half:]
    return jnp.concatenate([x1 * cos - x2 * sin, x1 * sin + x2 * cos], axis=0)


def _head_scores(qn, g, keys, masks):
    zeros = jnp.zeros_like(qn)
    qz = jnp.concatenate([qn, zeros] if g % 2 == 0 else [zeros, qn], axis=0)
    blk = slice((g // 2) * LANES, (g // 2 + 1) * LANES)
    scores = []
    for k, msk in zip(keys, masks):
        s = _dot(k[:, blk], qz)
        scores.append(s if msk is None else jnp.where(msk, s, NEG_INF))
    return scores


ONES_ROWS = 16


def _with_ones_rows(vt):
    return jnp.concatenate([vt, jnp.ones((ONES_ROWS, vt.shape[1]), vt.dtype)], axis=0)


def _head_softmax_pv(scores, values_t, sink2):
    m = sink2
    for s in scores:
        m = jnp.maximum(m, jnp.max(s, axis=0, keepdims=True))
    acc = None
    for s, vt in zip(scores, values_t):
        pv = _dot(vt, jnp.exp2(s - m).astype(BF16))
        acc = pv if acc is None else acc + pv
    den = acc[HEAD_DIM:HEAD_DIM + 1] + jnp.exp2(sink2 - m)
    return acc[:HEAD_DIM] * (1.0 / den)


def _attend_heads(scores_fn, finish_fn):
    pending = [scores_fn(hd) for hd in range(GQA)]
    for g in range(N_KV):
        nxt = [scores_fn(hd) for hd in range((g + 1) * GQA, (g + 2) * GQA)] if g + 1 < N_KV else None
        for i, sc in enumerate(pending):
            finish_fn(g * GQA + i, sc)
        pending = nxt


def _gate_out(x, o, z, gate, wout):
    y = (o * jax.nn.silu(z)).astype(BF16)
    return x + gate * _dot(y, wout)


def _attn_ctx_kernel(sink_ref, x_ref, mod_ref, nw_ref, wqkvt_ref, wz_ref, wout_ref, qw_ref, kw_ref,
                     o_ref, kto_ref, vto_ref, qkvt_scr, z_scr, ot_scr):
    seq = x_ref.shape[1]
    n_seq = x_ref.shape[0]
    mod = _mod_row(mod_ref, False)
    gate = mod[:, 2 * D_MODEL:]
    kw = jnp.broadcast_to(kw_ref[...], (HEAD_DIM, seq))
    qw = jnp.broadcast_to(qw_ref[...], (HEAD_DIM, seq))

    def project(i):
        h = _mod_norm(x_ref[i], nw_ref[...], mod).astype(BF16)
        qkvt_scr[i] = _dot_nt(wqkvt_ref[...], h)
        z_scr[i] = _dot(h, wz_ref[...])

    def attend(i):
        knt = jnp.concatenate(
            [_head_rms(qkvt_scr[i, BRANCH + g * HEAD_DIM:BRANCH + (g + 1) * HEAD_DIM, :], kw)
             for g in range(N_KV)], axis=0)
        kto_ref[i] = knt
        vtf = qkvt_scr[i, BRANCH + KV_W:, :]
        vto_ref[i] = vtf
        kb = knt.T.astype(BF16)
        vt = vtf.astype(BF16)
        vts = [_with_ones_rows(vt[g * HEAD_DIM:(g + 1) * HEAD_DIM]) for g in range(N_KV)]

        def scores_fn(hd):
            t = qkvt_scr[i, hd * HEAD_DIM:(hd + 1) * HEAD_DIM, :]
            qn = (_head_rms(t, qw) * (HEAD_DIM ** -0.5 * LOG2E)).astype(BF16)
            return _head_scores(qn, hd // GQA, [kb], [None])

        def finish_fn(hd, sc):
            ot_scr[i, hd * HEAD_DIM:(hd + 1) * HEAD_DIM, :] = _head_softmax_pv(
                sc, [vts[hd // GQA]], sink_ref[hd] * LOG2E)

        _attend_heads(scores_fn, finish_fn)

    def output(i):
        o_ref[i] = _gate_out(x_ref[i], ot_scr[i].T, z_scr[i], gate, wout_ref[...])

    for phase in (project, attend, output):
        for i in range(n_seq):
            phase(i)


def _attn_ctx_layer(x, mod, nw, wqkvt, wz, wout, qw, kw, sink):
    nb, seq, _ = x.shape
    n_seq = CTX_SEQS_PER_STEP
    assert nb % n_seq == 0
    return pl.pallas_call(
        _attn_ctx_kernel,
        grid=(nb // n_seq,),
        in_specs=[
            pl.BlockSpec(memory_space=pltpu.SMEM),
            pl.BlockSpec((n_seq, seq, D_MODEL), lambda b: (b, 0, 0)),
            _mod_spec(1),
            _const_spec((1, D_MODEL)),
            _const_spec((BRANCH + 2 * KV_W, D_MODEL)),
            _const_spec((D_MODEL, BRANCH)),
            _const_spec((BRANCH, D_MODEL)),
            _const_spec((HEAD_DIM, 1)),
            _const_spec((HEAD_DIM, 1)),
        ],
        out_specs=[
            pl.BlockSpec((n_seq, seq, D_MODEL), lambda b: (b, 0, 0)),
            pl.BlockSpec((n_seq, KV_W, seq), lambda b: (b, 0, 0)),
            pl.BlockSpec((n_seq, KV_W, seq), lambda b: (b, 0, 0)),
        ],
        out_shape=[
            jax.ShapeDtypeStruct(x.shape, F32),
            jax.ShapeDtypeStruct((nb, KV_W, seq), F32),
            jax.ShapeDtypeStruct((nb, KV_W, seq), F32),
        ],
        scratch_shapes=[
            pltpu.VMEM((n_seq, BRANCH + 2 * KV_W, seq), F32),
            pltpu.VMEM((n_seq, seq, BRANCH), F32),
            pltpu.VMEM((n_seq, BRANCH, seq), F32),
        ],
        compiler_params=pltpu.CompilerParams(
            dimension_semantics=("arbitrary",), vmem_limit_bytes=VMEM_LIMIT),
        name="attn_context_layer",
    )(sink, x, mod, nw, wqkvt, wz, wout, qw, kw)


def _attn_lat_kernel(sink_ref, x_ref, mod_ref, nw_ref, wqkvt_ref, wz_ref, wout_ref, qw_ref, kw_ref,
                     cos_ref, sin_ref, ck_ref, cv_ref, o_ref,
                     q_scr, z_scr, ot_scr, k_scr, vt_scr, *, seq):
    mod = _mod_row(mod_ref, True)
    gate = mod[:, 2 * D_MODEL:]
    nw = nw_ref[...]
    qw = jnp.broadcast_to(qw_ref[...], (HEAD_DIM, Q_BLOCK))
    kw = jnp.broadcast_to(kw_ref[...], (HEAD_DIM, Q_BLOCK))
    n_blocks = seq // Q_BLOCK
    kv_blocks = seq // BLOCK
    k_scr[0:BLOCK, :] = jnp.zeros((BLOCK, KV_W), BF16)
    k_scr[BLOCK + seq:2 * BLOCK + seq, :] = jnp.zeros((BLOCK, KV_W), BF16)
    vt_scr[0] = jnp.zeros((KV_W, BLOCK), BF16)
    vt_scr[kv_blocks + 1] = jnp.zeros((KV_W, BLOCK), BF16)

    for c in range(n_blocks):
        rows = slice(c * Q_BLOCK, (c + 1) * Q_BLOCK)
        h = _mod_norm(x_ref[0, rows, :], nw, mod).astype(BF16)
        z_scr[rows, :] = _dot(h, wz_ref[...])
        qkvt = _dot_nt(wqkvt_ref[...], h)
        cos = cos_ref[:, rows]
        sin = sin_ref[:, rows]
        for hd in range(N_HEADS):
            hr = slice(hd * HEAD_DIM, (hd + 1) * HEAD_DIM)
            t = _rope_t(_head_rms(qkvt[hr], qw), cos, sin)
            q_scr[c, hr, :] = (t * (HEAD_DIM ** -0.5 * LOG2E)).astype(BF16)
        knt = jnp.concatenate(
            [_rope_t(_head_rms(qkvt[BRANCH + g * HEAD_DIM:BRANCH + (g + 1) * HEAD_DIM], kw), cos, sin)
             for g in range(N_KV)], axis=0)
        k_scr[BLOCK + c * Q_BLOCK:BLOCK + (c + 1) * Q_BLOCK, :] = knt.T.astype(BF16)
        vt = qkvt[BRANCH + KV_W:].astype(BF16)
        for j in range(Q_BLOCK // BLOCK):
            vt_scr[1 + c * (Q_BLOCK // BLOCK) + j] = vt[:, j * BLOCK:(j + 1) * BLOCK]

    ckb = ck_ref[0].T.astype(BF16)
    cvt = cv_ref[0].astype(BF16)
    cvts = [_with_ones_rows(cvt[g * HEAD_DIM:(g + 1) * HEAD_DIM]) for g in range(N_KV)]

    win_len = Q_BLOCK + 2 * BLOCK
    kj = lax.broadcasted_iota(jnp.int32, (win_len, Q_BLOCK), 0)
    qi = lax.broadcasted_iota(jnp.int32, (win_len, Q_BLOCK), 1)
    rel = kj - BLOCK - qi
    band = (rel >= -WINDOW) & (rel <= WINDOW)

    def q_block(n, carry):
        r0 = pl.multiple_of(n * Q_BLOCK, Q_BLOCK)
        kpos = kj + (r0 - BLOCK)
        valid = band & (kpos >= 0) & (kpos < seq)
        kwin = k_scr[pl.ds(r0, win_len), :]
        vwin = jnp.concatenate(
            [vt_scr[n * (Q_BLOCK // BLOCK) + j] for j in range(win_len // BLOCK)], axis=1)
        vwins = [_with_ones_rows(vwin[g * HEAD_DIM:(g + 1) * HEAD_DIM]) for g in range(N_KV)]

        def scores_fn(hd):
            return _head_scores(q_scr[n, hd * HEAD_DIM:(hd + 1) * HEAD_DIM, :], hd // GQA,
                                [kwin, ckb], [valid, None])

        def finish_fn(hd, sc):
            g = hd // GQA
            ot_scr[n, hd * HEAD_DIM:(hd + 1) * HEAD_DIM, :] = _head_softmax_pv(
                sc, [vwins[g], cvts[g]], sink_ref[hd] * LOG2E)

        _attend_heads(scores_fn, finish_fn)
        return carry

    lax.fori_loop(0, n_blocks, q_block, 0)

    for c in range(n_blocks):
        rows = slice(c * Q_BLOCK, (c + 1) * Q_BLOCK)
        o_ref[0, rows, :] = _gate_out(x_ref[0, rows, :], ot_scr[c].T, z_scr[rows, :], gate, wout_ref[...])


def _rope_tables_t(seq):
    pos = np.arange(seq)
    n_freq = HEAD_DIM // 4
    inv = ROPE_THETA ** (-np.arange(n_freq, dtype=np.float64) / n_freq)
    ang = np.concatenate([(pos // GRID_W)[:, None] * inv, (pos % GRID_W)[:, None] * inv], axis=-1)
    return np.cos(ang).T.astype(np.float32), np.sin(ang).T.astype(np.float32)


def _attn_lat_layer(x, mod, nw, wqkvt, wz, wout, qw, kw, sink, ckt, cvt):
    nb, seq, _ = x.shape
    past = ckt.shape[2]
    cos, sin = (jnp.asarray(t) for t in _rope_tables_t(seq))
    kern = functools.partial(_attn_lat_kernel, seq=seq)
    n_blocks = seq // Q_BLOCK
    return pl.pallas_call(
        kern,
        grid=(nb,),
        in_specs=[
            pl.BlockSpec(memory_space=pltpu.SMEM),
            pl.BlockSpec((1, seq, D_MODEL), lambda b: (b, 0, 0)),
            _mod_spec(1),
            _const_spec((1, D_MODEL)),
            _const_spec((BRANCH + 2 * KV_W, D_MODEL)),
            _const_spec((D_MODEL, BRANCH)),
            _const_spec((BRANCH, D_MODEL)),
            _const_spec((HEAD_DIM, 1)),
            _const_spec((HEAD_DIM, 1)),
            _const_spec((HEAD_DIM // 2, seq)),
            _const_spec((HEAD_DIM // 2, seq)),
            pl.BlockSpec((1, KV_W, past), lambda b: (b, 0, 0)),
            pl.BlockSpec((1, KV_W, past), lambda b: (b, 0, 0)),
        ],
        out_specs=pl.BlockSpec((1, seq, D_MODEL), lambda b: (b, 0, 0)),
        out_shape=jax.ShapeDtypeStruct(x.shape, F32),
        scratch_shapes=[
            pltpu.VMEM((n_blocks, BRANCH, Q_BLOCK), BF16),
            pltpu.VMEM((seq, BRANCH), F32),
            pltpu.VMEM((n_blocks, BRANCH, Q_BLOCK), F32),
            pltpu.VMEM((seq + 2 * BLOCK, KV_W), BF16),
            pltpu.VMEM((seq // BLOCK + 2, KV_W, BLOCK), BF16),
        ],
        compiler_params=pltpu.CompilerParams(
            dimension_semantics=("arbitrary",), vmem_limit_bytes=VMEM_LIMIT),
        name="attn_latent_layer",
    )(sink, x, mod, nw, wqkvt, wz, wout, qw, kw, cos, sin, ckt, cvt)


def kernel(x_prompt, x_sample, cache_k_l1, cache_v_l1, c, c_ctx, norm_w_l0, w_mod_l0, b_mod_l0,
           w_in_l0, w_out_l0, norm_w_l1, w_mod_l1, b_mod_l1, w_in_l1, q_norm_w_l1, k_norm_w_l1,
           sink_l1, w_out_l1):
    nb_ctx, seq_ctx, _ = x_prompt.shape
    nb_lat = x_sample.shape[0]
    past = cache_k_l1.shape[1]
    assert 1 + nb_lat <= MOD_ROWS
    mod = _modulation(c_ctx, c, w_mod_l0, b_mod_l0, w_mod_l1, b_mod_l1)

    nw0 = norm_w_l0.reshape(1, D_MODEL)
    nw1 = norm_w_l1.reshape(1, D_MODEL)
    win0 = w_in_l0.astype(BF16)
    wout0 = w_out_l0.astype(BF16)
    qw = q_norm_w_l1.reshape(HEAD_DIM, 1)
    kw = k_norm_w_l1.reshape(HEAD_DIM, 1)

    xp, wqkvt1, wz1, wout1 = _fourier_layer(x_prompt, mod, False, nw0, win0, wout0,
                                            next_weights=(w_in_l1, w_out_l1))
    xs = _fourier_layer(x_sample, mod, True, nw0, win0, wout0)

    def to_feature_major(t):
        return jnp.transpose(t, (0, 2, 3, 1)).reshape(t.shape[0], KV_W, t.shape[1])

    def from_feature_major(t):
        return jnp.transpose(t.reshape(t.shape[0], N_KV, HEAD_DIM, t.shape[2]), (0, 3, 1, 2))

    xp, new_kt, new_vt = _attn_ctx_layer(xp, mod, nw1, wqkvt1, wz1, wout1, qw, kw, sink_l1)
    xs = _attn_lat_layer(xs, mod, nw1, wqkvt1, wz1, wout1, qw, kw, sink_l1,
                         to_feature_major(cache_k_l1), to_feature_major(cache_v_l1))
    return (xp, xs, from_feature_major(new_kt), from_feature_major(new_vt))
```

```python
import functools

import numpy as np
import jax
import jax.numpy as jnp
from jax import lax
from jax.experimental import pallas as pl
from jax.experimental.pallas import tpu as pltpu

D_MODEL = 1024
BRANCH = 1024
N_GROUPS = 4
GROUP_W = BRANCH // N_GROUPS
HEAD_DIM = 64
N_HEADS = 16
N_KV = 4
GQA = N_HEADS // N_KV
KV_W = N_KV * HEAD_DIM
GRID_W = 64
WINDOW = 128
BLOCK = 128
ROPE_THETA = 10000.0
EPS = 1e-6
NEG_INF = -1e30
LANES = 128
ROW_CHUNK = 256
Q_BLOCK = 256
CTX_SEQS_PER_STEP = 2
LAT_BLOCKS_PER_ITER = 1
VMEM_LIMIT = 56 * 1024 * 1024
MOD_ROWS = 8
MOD_K_CHUNK = 256
LOG2E = float(np.log2(np.e))

F32 = jnp.float32
BF16 = jnp.bfloat16


def _dot(a, b):
    return jnp.dot(a, b, preferred_element_type=F32)


def _dot_nt(a, b):
    return lax.dot_general(a, b, (((1,), (1,)), ((), ())), preferred_element_type=F32)


def _mod_row(mod_ref, per_request):
    if per_request:
        return mod_ref[0, pl.ds(1 + pl.program_id(0), 1), :]
    return mod_ref[0, 0:1, :]


def _mod_spec(layer):
    return pl.BlockSpec((1, MOD_ROWS, 3 * D_MODEL), lambda b: (layer, 0, 0))


def _mod_norm(x, nw, mod):
    shift = mod[:, :D_MODEL]
    scale = mod[:, D_MODEL:2 * D_MODEL]
    y = x * lax.rsqrt(jnp.mean(x * x, axis=-1, keepdims=True) + EPS)
    return (y * nw) * (1.0 + scale) + shift


def _mod_kernel(cctx_ref, c_ref, w0_ref, w1_ref, b0_ref, b1_ref, o_ref, cond_scr):
    layer = pl.program_id(0)
    k = pl.program_id(1)
    n_lat = c_ref.shape[0]
    cond_scr[...] = jnp.zeros_like(cond_scr)
    cond_scr[0:1, :] = cctx_ref[...]
    cond_scr[1:1 + n_lat, :] = c_ref[...]
    s = jax.nn.silu(cond_scr[...]).astype(BF16)

    def accumulate(w_ref, b_ref):
        @pl.when(k == 0)
        def _():
            o_ref[0] = jnp.broadcast_to(b_ref[...], o_ref.shape[1:])
        o_ref[0] += _dot(s, w_ref[...].astype(BF16))

    @pl.when(layer == 0)
    def _():
        accumulate(w0_ref, b0_ref)

    @pl.when(layer == 1)
    def _():
        accumulate(w1_ref, b1_ref)


def _modulation(c_ctx, c, w_mod_l0, b_mod_l0, w_mod_l1, b_mod_l1):
    n = w_mod_l0.shape[1]
    n_lat = c.shape[0]
    nk = D_MODEL // MOD_K_CHUNK
    return pl.pallas_call(
        _mod_kernel,
        grid=(2, nk),
        in_specs=[
            pl.BlockSpec((1, MOD_K_CHUNK), lambda l, k: (0, k)),
            pl.BlockSpec((n_lat, MOD_K_CHUNK), lambda l, k: (0, k)),
            pl.BlockSpec((MOD_K_CHUNK, n), lambda l, k: (jnp.where(l == 0, k, nk - 1), 0)),
            pl.BlockSpec((MOD_K_CHUNK, n), lambda l, k: (jnp.where(l == 1, k, 0), 0)),
            pl.BlockSpec((1, n), lambda l, k: (0, 0)),
            pl.BlockSpec((1, n), lambda l, k: (0, 0)),
        ],
        out_specs=pl.BlockSpec((1, MOD_ROWS, n), lambda l, k: (l, 0, 0)),
        out_shape=jax.ShapeDtypeStruct((2, MOD_ROWS, n), F32),
        scratch_shapes=[pltpu.VMEM((MOD_ROWS, MOD_K_CHUNK), F32)],
        compiler_params=pltpu.CompilerParams(dimension_semantics=("arbitrary", "arbitrary")),
        name="modulation",
    )(c_ctx.reshape(1, D_MODEL), c, w_mod_l0, w_mod_l1, b_mod_l0.reshape(1, n), b_mod_l1.reshape(1, n))


def _fourier_prep_kernel(x_ref, mod_ref, nw_ref, win_ref, wout_ref, mc_ref, ls_ref, win1_ref, wout1_ref,
                         o_ref, wqkvt1_ref, wz1_ref, wout1b_ref, ab_scr, z_scr, *, seq):
    w = win1_ref[...]
    wqkvt1_ref[...] = w[:, :BRANCH + 2 * KV_W].T.astype(BF16)
    wz1_ref[...] = w[:, BRANCH + 2 * KV_W:].astype(BF16)
    wout1b_ref[...] = wout1_ref[...].astype(BF16)
    _fourier_kernel(x_ref, mod_ref, nw_ref, win_ref, wout_ref, mc_ref, ls_ref, o_ref, ab_scr, z_scr,
                    seq=seq, per_request=False)


def _fourier_kernel(x_ref, mod_ref, nw_ref, win_ref, wout_ref, mc_ref, ls_ref, o_ref,
                    ab_scr, z_scr, *, seq, per_request):
    mod = _mod_row(mod_ref, per_request)
    gate = mod[:, 2 * D_MODEL:]
    nw = nw_ref[...]
    n_seq = x_ref.shape[0]
    n_chunks = seq // ROW_CHUNK
    for i in range(n_seq):
        for c in range(n_chunks):
            rows = slice(c * ROW_CHUNK, (c + 1) * ROW_CHUNK)
            h = _mod_norm(x_ref[i, rows, :], nw, mod).astype(BF16)
            uz = _dot(h, win_ref[...])
            z_scr[i, rows, :] = uz[:, BRANCH:]
            u = uz[:, :BRANCH].astype(BF16)
            for g in range(N_GROUPS):
                cols = slice(g * GROUP_W, (g + 1) * GROUP_W)
                t = _dot(u[:, cols], mc_ref[...])
                ab_scr[i, rows, cols] = t[:, :GROUP_W].astype(BF16)
                ab_scr[i, seq + c * ROW_CHUNK:seq + (c + 1) * ROW_CHUNK, cols] = t[:, GROUP_W:].astype(BF16)
    for i in range(n_seq):
        for c in range(n_chunks):
            rows = slice(c * ROW_CHUNK, (c + 1) * ROW_CHUNK)
            y = _dot(ls_ref[rows, :], ab_scr[i])
            y = (y * jax.nn.silu(z_scr[i, rows, :])).astype(BF16)
            o_ref[i, rows, :] = x_ref[i, rows, :] + gate * _dot(y, wout_ref[...])


def _dft_tables(seq):
    k = np.arange(GROUP_W)
    ang = 2.0 * np.pi * ((k[:, None] * k[None, :]) % GROUP_W) / GROUP_W
    mc = np.concatenate([np.cos(ang), np.sin(ang)], axis=1) / np.sqrt(GROUP_W)
    n = np.arange(seq)
    ang = 2.0 * np.pi * ((n[:, None] * n[None, :]) % seq) / seq
    ls = np.concatenate([np.cos(ang), -np.sin(ang)], axis=1) / np.sqrt(seq)
    return mc.astype(np.float32), ls.astype(np.float32)


def _const_spec(shape):
    return pl.BlockSpec(shape, lambda b: (0,) * len(shape))


def _fourier_layer(x, mod, per_request, nw, win, wout, next_weights=None):
    nb, seq, _ = x.shape
    n_seq = 1 if per_request else CTX_SEQS_PER_STEP
    assert nb % n_seq == 0
    n_steps = nb // n_seq
    mc, ls = _dft_tables(seq)
    mc = jnp.asarray(mc).astype(BF16)
    ls = jnp.asarray(ls).astype(BF16)
    in_specs = [
        pl.BlockSpec((n_seq, seq, D_MODEL), lambda b: (b, 0, 0)),
        _mod_spec(0),
        _const_spec((1, D_MODEL)),
        _const_spec((D_MODEL, 2 * BRANCH)),
        _const_spec((BRANCH, D_MODEL)),
        _const_spec((GROUP_W, 2 * GROUP_W)),
        _const_spec((seq, 2 * seq)),
    ]
    out_specs = [pl.BlockSpec((n_seq, seq, D_MODEL), lambda b: (b, 0, 0))]
    out_shape = [jax.ShapeDtypeStruct(x.shape, F32)]
    args = [x, mod, nw, win, wout, mc, ls]
    if next_weights is None:
        kern = functools.partial(_fourier_kernel, seq=seq, per_request=per_request)
    else:
        assert not per_request and D_MODEL % (n_steps * LANES) == 0
        rows = D_MODEL // n_steps
        n_qkvz = 2 * BRANCH + 2 * KV_W
        kern = functools.partial(_fourier_prep_kernel, seq=seq)
        in_specs += [pl.BlockSpec((rows, n_qkvz), lambda b: (b, 0)),
                     pl.BlockSpec((rows, D_MODEL), lambda b: (b, 0))]
        out_specs += [pl.BlockSpec((BRANCH + 2 * KV_W, rows), lambda b: (0, b)),
                      pl.BlockSpec((rows, BRANCH), lambda b: (b, 0)),
                      pl.BlockSpec((rows, D_MODEL), lambda b: (b, 0))]
        out_shape += [jax.ShapeDtypeStruct((BRANCH + 2 * KV_W, D_MODEL), BF16),
                      jax.ShapeDtypeStruct((D_MODEL, BRANCH), BF16),
                      jax.ShapeDtypeStruct((BRANCH, D_MODEL), BF16)]
        args += list(next_weights)
    outs = pl.pallas_call(
        kern,
        grid=(n_steps,),
        in_specs=in_specs,
        out_specs=out_specs,
        out_shape=out_shape,
        scratch_shapes=[
            pltpu.VMEM((n_seq, 2 * seq, BRANCH), BF16),
            pltpu.VMEM((n_seq, seq, BRANCH), F32),
        ],
        compiler_params=pltpu.CompilerParams(
            dimension_semantics=("arbitrary",), vmem_limit_bytes=VMEM_LIMIT),
        name=f"fourier_layer_s{seq}",
    )(*args)
    return outs[0] if next_weights is None else outs


def _head_weight_tile(w_ref, n_tokens):
    row = jnp.broadcast_to(w_ref[...], (HEAD_DIM, HEAD_DIM))
    ii = lax.broadcasted_iota(jnp.int32, (HEAD_DIM, HEAD_DIM), 0)
    jj = lax.broadcasted_iota(jnp.int32, (HEAD_DIM, HEAD_DIM), 1)
    col = jnp.sum(jnp.where(ii == jj, row, 0.0), axis=1, keepdims=True)
    return jnp.broadcast_to(col, (HEAD_DIM, n_tokens))


def _head_rms(t, w):
    return (t * lax.rsqrt(jnp.mean(t * t, axis=0, keepdims=True) + EPS)) * w


def _rope_t(t, cos, sin):
    half = HEAD_DIM // 2
    x1, x2 = t[:half], t[half:]
    return jnp.concatenate([x1 * cos - x2 * sin, x1 * sin + x2 * cos], axis=0)


def _head_scores(qn, g, keys, masks):
    zeros = jnp.zeros_like(qn)
    qz = jnp.concatenate([qn, zeros] if g % 2 == 0 else [zeros, qn], axis=0)
    blk = slice((g // 2) * LANES, (g // 2 + 1) * LANES)
    scores = []
    for k, msk in zip(keys, masks):
        s = _dot(k[:, blk], qz)
        scores.append(s if msk is None else jnp.where(msk, s, NEG_INF))
    return scores


ONES_ROWS = 16


def _with_ones_rows(vt):
    return jnp.concatenate([vt, jnp.ones((ONES_ROWS, vt.shape[1]), vt.dtype)], axis=0)


def _head_softmax_pv(scores, values_t, sink2):
    m = sink2
    for s in scores:
        m = jnp.maximum(m, jnp.max(s, axis=0, keepdims=True))
    acc = None
    for s, vt in zip(scores, values_t):
        pv = _dot(vt, jnp.exp2(s - m).astype(BF16))
        acc = pv if acc is None else acc + pv
    den = acc[HEAD_DIM:HEAD_DIM + 1] + jnp.exp2(sink2 - m)
    return acc[:HEAD_DIM] * (1.0 / den)


def _attend_heads(scores_fn, finish_fn, n_heads=N_HEADS):
    n_groups = n_heads // GQA
    pending = [scores_fn(hd) for hd in range(GQA)]
    for g in range(n_groups):
        nxt = [scores_fn(hd) for hd in range((g + 1) * GQA, (g + 2) * GQA)] if g + 1 < n_groups else None
        for i, sc in enumerate(pending):
            finish_fn(g * GQA + i, sc)
        pending = nxt


def _gate_out(x, o, z, gate, wout):
    y = (o * jax.nn.silu(z)).astype(BF16)
    return x + gate * _dot(y, wout)


def _attn_ctx_kernel(sink_ref, x_ref, mod_ref, nw_ref, wqkvt_ref, wz_ref, wout_ref, qw_ref, kw_ref,
                     o_ref, kto_ref, vto_ref, qkvt_scr, z_scr, ot_scr):
    seq = x_ref.shape[1]
    n_seq = x_ref.shape[0]
    mod = _mod_row(mod_ref, False)
    gate = mod[:, 2 * D_MODEL:]
    kw = _head_weight_tile(kw_ref, seq)
    qw = _head_weight_tile(qw_ref, seq)

    def project(i):
        h = _mod_norm(x_ref[i], nw_ref[...], mod).astype(BF16)
        qkvt_scr[i] = _dot_nt(wqkvt_ref[...], h)
        z_scr[i] = _dot(h, wz_ref[...])

    def attend(i):
        knt = jnp.concatenate(
            [_head_rms(qkvt_scr[i, BRANCH + g * HEAD_DIM:BRANCH + (g + 1) * HEAD_DIM, :], kw)
             for g in range(N_KV)], axis=0)
        kto_ref[i] = knt
        vtf = qkvt_scr[i, BRANCH + KV_W:, :]
        vto_ref[i] = vtf
        kb = knt.T.astype(BF16)
        vt = vtf.astype(BF16)
        vts = [_with_ones_rows(vt[g * HEAD_DIM:(g + 1) * HEAD_DIM]) for g in range(N_KV)]

        def scores_fn(hd):
            t = qkvt_scr[i, hd * HEAD_DIM:(hd + 1) * HEAD_DIM, :]
            qn = (_head_rms(t, qw) * (HEAD_DIM ** -0.5 * LOG2E)).astype(BF16)
            return _head_scores(qn, hd // GQA, [kb], [None])

        def finish_fn(hd, sc):
            ot_scr[i, hd * HEAD_DIM:(hd + 1) * HEAD_DIM, :] = _head_softmax_pv(
                sc, [vts[hd // GQA]], sink_ref[hd] * LOG2E)

        _attend_heads(scores_fn, finish_fn)

    def output(i):
        o_ref[i] = _gate_out(x_ref[i], ot_scr[i].T, z_scr[i], gate, wout_ref[...])

    for phase in (project, attend, output):
        for i in range(n_seq):
            phase(i)


def _attn_ctx_layer(x, mod, nw, wqkvt, wz, wout, qw, kw, sink):
    nb, seq, _ = x.shape
    n_seq = CTX_SEQS_PER_STEP
    assert nb % n_seq == 0
    return pl.pallas_call(
        _attn_ctx_kernel,
        grid=(nb // n_seq,),
        in_specs=[
            pl.BlockSpec(memory_space=pltpu.SMEM),
            pl.BlockSpec((n_seq, seq, D_MODEL), lambda b: (b, 0, 0)),
            _mod_spec(1),
            _const_spec((1, D_MODEL)),
            _const_spec((BRANCH + 2 * KV_W, D_MODEL)),
            _const_spec((D_MODEL, BRANCH)),
            _const_spec((BRANCH, D_MODEL)),
            _const_spec((1, HEAD_DIM)),
            _const_spec((1, HEAD_DIM)),
        ],
        out_specs=[
            pl.BlockSpec((n_seq, seq, D_MODEL), lambda b: (b, 0, 0)),
            pl.BlockSpec((n_seq, KV_W, seq), lambda b: (b, 0, 0)),
            pl.BlockSpec((n_seq, KV_W, seq), lambda b: (b, 0, 0)),
        ],
        out_shape=[
            jax.ShapeDtypeStruct(x.shape, F32),
            jax.ShapeDtypeStruct((nb, KV_W, seq), F32),
            jax.ShapeDtypeStruct((nb, KV_W, seq), F32),
        ],
        scratch_shapes=[
            pltpu.VMEM((n_seq, BRANCH + 2 * KV_W, seq), F32),
            pltpu.VMEM((n_seq, seq, BRANCH), F32),
            pltpu.VMEM((n_seq, BRANCH, seq), F32),
        ],
        compiler_params=pltpu.CompilerParams(
            dimension_semantics=("arbitrary",), vmem_limit_bytes=VMEM_LIMIT),
        name="attn_context_layer",
    )(sink, x, mod, nw, wqkvt, wz, wout, qw, kw)


def _attn_lat_kernel(sink_ref, x_ref, mod_ref, nw_ref, wqkvt_ref, wz_ref, wout_ref, qw_ref, kw_ref,
                     cos_ref, sin_ref, ck_ref, cv_ref, o_ref,
                     q_scr, z_scr, ot_scr, k_scr, vt_scr, *, seq):
    mod = _mod_row(mod_ref, True)
    gate = mod[:, 2 * D_MODEL:]
    nw = nw_ref[...]
    qw = _head_weight_tile(qw_ref, Q_BLOCK)
    kw = _head_weight_tile(kw_ref, Q_BLOCK)
    n_blocks = seq // Q_BLOCK
    kv_blocks = seq // BLOCK
    k_scr[0:BLOCK, :] = jnp.zeros((BLOCK, KV_W), BF16)
    k_scr[BLOCK + seq:2 * BLOCK + seq, :] = jnp.zeros((BLOCK, KV_W), BF16)
    vt_scr[0] = jnp.zeros((KV_W, BLOCK), BF16)
    vt_scr[kv_blocks + 1] = jnp.zeros((KV_W, BLOCK), BF16)

    def project(c):
        rows = slice(c * Q_BLOCK, (c + 1) * Q_BLOCK)
        h = _mod_norm(x_ref[0, rows, :], nw, mod).astype(BF16)
        z_scr[rows, :] = _dot(h, wz_ref[...])
        qkvt = _dot_nt(wqkvt_ref[...], h)
        cos = cos_ref[:, rows]
        sin = sin_ref[:, rows]
        for hd in range(N_HEADS):
            hr = slice(hd * HEAD_DIM, (hd + 1) * HEAD_DIM)
            t = _rope_t(_head_rms(qkvt[hr], qw), cos, sin)
            q_scr[c, hr, :] = (t * (HEAD_DIM ** -0.5 * LOG2E)).astype(BF16)
        knt = jnp.concatenate(
            [_rope_t(_head_rms(qkvt[BRANCH + g * HEAD_DIM:BRANCH + (g + 1) * HEAD_DIM], kw), cos, sin)
             for g in range(N_KV)], axis=0)
        k_scr[BLOCK + c * Q_BLOCK:BLOCK + (c + 1) * Q_BLOCK, :] = knt.T.astype(BF16)
        vt = qkvt[BRANCH + KV_W:].astype(BF16)
        for j in range(Q_BLOCK // BLOCK):
            vt_scr[1 + c * (Q_BLOCK // BLOCK) + j] = vt[:, j * BLOCK:(j + 1) * BLOCK]

    ckb = ck_ref[0].T.astype(BF16)
    cvt = cv_ref[0].astype(BF16)
    cvts = [_with_ones_rows(cvt[g * HEAD_DIM:(g + 1) * HEAD_DIM]) for g in range(N_KV)]

    win_len = Q_BLOCK + 2 * BLOCK
    kj = lax.broadcasted_iota(jnp.int32, (win_len, Q_BLOCK), 0)
    qi = lax.broadcasted_iota(jnp.int32, (win_len, Q_BLOCK), 1)
    rel = kj - BLOCK - qi
    band = (rel >= -WINDOW) & (rel <= WINDOW)

    def block_operands(n):
        r0 = pl.multiple_of(n * Q_BLOCK, Q_BLOCK)
        kpos = kj + (r0 - BLOCK)
        valid = band & (kpos >= 0) & (kpos < seq)
        kwin = k_scr[pl.ds(r0, win_len), :]
        vwin = jnp.concatenate(
            [vt_scr[n * (Q_BLOCK // BLOCK) + j] for j in range(win_len // BLOCK)], axis=1)
        vwins = [_with_ones_rows(vwin[g * HEAD_DIM:(g + 1) * HEAD_DIM]) for g in range(N_KV)]
        return n, valid, kwin, vwins

    def attend(it, carry):
        blocks = [block_operands(it * LAT_BLOCKS_PER_ITER + j) for j in range(LAT_BLOCKS_PER_ITER)]

        def scores_fn(idx):
            (n, valid, kwin, _), hd = blocks[idx // N_HEADS], idx % N_HEADS
            return _head_scores(q_scr[n, hd * HEAD_DIM:(hd + 1) * HEAD_DIM, :], hd // GQA,
                                [kwin, ckb], [valid, None])

        def finish_fn(idx, sc):
            (n, _, _, vwins), hd = blocks[idx // N_HEADS], idx % N_HEADS
            g = hd // GQA
            ot_scr[n, hd * HEAD_DIM:(hd + 1) * HEAD_DIM, :] = _head_softmax_pv(
                sc, [vwins[g], cvts[g]], sink_ref[hd] * LOG2E)

        _attend_heads(scores_fn, finish_fn, N_HEADS * LAT_BLOCKS_PER_ITER)
        return carry

    for c in range(n_blocks):
        project(c)
    lax.fori_loop(0, n_blocks // LAT_BLOCKS_PER_ITER, attend, 0)
    for c in range(n_blocks):
        rows = slice(c * Q_BLOCK, (c + 1) * Q_BLOCK)
        o_ref[0, rows, :] = _gate_out(x_ref[0, rows, :], ot_scr[c].T, z_scr[rows, :], gate, wout_ref[...])


def _rope_tables_t(seq):
    pos = np.arange(seq)
    n_freq = HEAD_DIM // 4
    inv = ROPE_THETA ** (-np.arange(n_freq, dtype=np.float64) / n_freq)
    ang = np.concatenate([(pos // GRID_W)[:, None] * inv, (pos % GRID_W)[:, None] * inv], axis=-1)
    return np.cos(ang).T.astype(np.float32), np.sin(ang).T.astype(np.float32)


def _attn_lat_layer(x, mod, nw, wqkvt, wz, wout, qw, kw, sink, ckt, cvt):
    nb, seq, _ = x.shape
    past = ckt.shape[2]
    cos, sin = (jnp.asarray(t) for t in _rope_tables_t(seq))
    kern = functools.partial(_attn_lat_kernel, seq=seq)
    n_blocks = seq // Q_BLOCK
    return pl.pallas_call(
        kern,
        grid=(nb,),
        in_specs=[
            pl.BlockSpec(memory_space=pltpu.SMEM),
            pl.BlockSpec((1, seq, D_MODEL), lambda b: (b, 0, 0)),
            _mod_spec(1),
            _const_spec((1, D_MODEL)),
            _const_spec((BRANCH + 2 * KV_W, D_MODEL)),
            _const_spec((D_MODEL, BRANCH)),
            _const_spec((BRANCH, D_MODEL)),
            _const_spec((1, HEAD_DIM)),
            _const_spec((1, HEAD_DIM)),
            _const_spec((HEAD_DIM // 2, seq)),
            _const_spec((HEAD_DIM // 2, seq)),
            pl.BlockSpec((1, KV_W, past), lambda b: (b, 0, 0)),
            pl.BlockSpec((1, KV_W, past), lambda b: (b, 0, 0)),
        ],
        out_specs=pl.BlockSpec((1, seq, D_MODEL), lambda b: (b, 0, 0)),
        out_shape=jax.ShapeDtypeStruct(x.shape, F32),
        scratch_shapes=[
            pltpu.VMEM((n_blocks, BRANCH, Q_BLOCK), BF16),
            pltpu.VMEM((seq, BRANCH), F32),
            pltpu.VMEM((n_blocks, BRANCH, Q_BLOCK), F32),
            pltpu.VMEM((seq + 2 * BLOCK, KV_W), BF16),
            pltpu.VMEM((seq // BLOCK + 2, KV_W, BLOCK), BF16),
        ],
        compiler_params=pltpu.CompilerParams(
            dimension_semantics=("arbitrary",), vmem_limit_bytes=VMEM_LIMIT),
        name="attn_latent_layer",
    )(sink, x, mod, nw, wqkvt, wz, wout, qw, kw, cos, sin, ckt, cvt)


def kernel(x_prompt, x_sample, cache_k_l1, cache_v_l1, c, c_ctx, norm_w_l0, w_mod_l0, b_mod_l0,
           w_in_l0, w_out_l0, norm_w_l1, w_mod_l1, b_mod_l1, w_in_l1, q_norm_w_l1, k_norm_w_l1,
           sink_l1, w_out_l1):
    nb_ctx, seq_ctx, _ = x_prompt.shape
    nb_lat = x_sample.shape[0]
    past = cache_k_l1.shape[1]
    assert 1 + nb_lat <= MOD_ROWS
    mod = _modulation(c_ctx, c, w_mod_l0, b_mod_l0, w_mod_l1, b_mod_l1)

    nw0 = norm_w_l0.reshape(1, D_MODEL)
    nw1 = norm_w_l1.reshape(1, D_MODEL)
    win0 = w_in_l0.astype(BF16)
    wout0 = w_out_l0.astype(BF16)
    qw = q_norm_w_l1.reshape(1, HEAD_DIM)
    kw = k_norm_w_l1.reshape(1, HEAD_DIM)

    xp, wqkvt1, wz1, wout1 = _fourier_layer(x_prompt, mod, False, nw0, win0, wout0,
                                            next_weights=(w_in_l1, w_out_l1))
    xs = _fourier_layer(x_sample, mod, True, nw0, win0, wout0)

    def to_feature_major(t):
        return jnp.transpose(t, (0, 2, 3, 1)).reshape(t.shape[0], KV_W, t.shape[1])

    def from_feature_major(t):
        return jnp.transpose(t.reshape(t.shape[0], N_KV, HEAD_DIM, t.shape[2]), (0, 3, 1, 2))

    xp, new_kt, new_vt = _attn_ctx_layer(xp, mod, nw1, wqkvt1, wz1, wout1, qw, kw, sink_l1)
    xs = _attn_lat_layer(xs, mod, nw1, wqkvt1, wz1, wout1, qw, kw, sink_l1,
                         to_feature_major(cache_k_l1), to_feature_major(cache_v_l1))
    return (xp, xs, from_feature_major(new_kt), from_feature_major(new_vt))
```

```python
import functools

import numpy as np
import jax
import jax.numpy as jnp
from jax import lax
from jax.experimental import pallas as pl
from jax.experimental.pallas import tpu as pltpu

D_MODEL = 1024
BRANCH = 1024
N_GROUPS = 4
GROUP_W = BRANCH // N_GROUPS
HEAD_DIM = 64
N_HEADS = 16
N_KV = 4
GQA = N_HEADS // N_KV
KV_W = N_KV * HEAD_DIM
GRID_W = 64
WINDOW = 128
BLOCK = 128
ROPE_THETA = 10000.0
EPS = 1e-6
NEG_INF = -1e30
LANES = 128
ROW_CHUNK = 256
Q_BLOCK = 256
HALVES = Q_BLOCK // BLOCK
CTX_SEQS_PER_STEP = 2
VMEM_LIMIT = 56 * 1024 * 1024
MOD_ROWS = 8
MOD_K_CHUNK = 256
LOG2E = float(np.log2(np.e))

F32 = jnp.float32
BF16 = jnp.bfloat16


def _dot(a, b):
    return jnp.dot(a, b, preferred_element_type=F32)


def _dot_nt(a, b):
    return lax.dot_general(a, b, (((1,), (1,)), ((), ())), preferred_element_type=F32)


def _mod_row(mod_ref, per_request):
    if per_request:
        return mod_ref[pl.ds(1 + pl.program_id(0), 1), :]
    return mod_ref[0:1, :]


MOD_SPEC = pl.BlockSpec((MOD_ROWS, 3 * D_MODEL), lambda b: (0, 0))


def _mod_norm(x, nw, mod):
    shift = mod[:, :D_MODEL]
    scale = mod[:, D_MODEL:2 * D_MODEL]
    y = x * lax.rsqrt(jnp.mean(x * x, axis=-1, keepdims=True) + EPS)
    return (y * nw) * (1.0 + scale) + shift


def _mod_accumulate(cctx_ref, c_ref, w_ref, b_ref, o_ref, cond_scr):
    n_lat = c_ref.shape[0]
    cond_scr[...] = jnp.zeros_like(cond_scr)
    cond_scr[0:1, :] = cctx_ref[...]
    cond_scr[1:1 + n_lat, :] = c_ref[...]
    s = jax.nn.silu(cond_scr[...]).astype(BF16)

    @pl.when(pl.program_id(0) == 0)
    def _():
        o_ref[...] = jnp.broadcast_to(b_ref[...], o_ref.shape)

    o_ref[...] += _dot(s, w_ref[...].astype(BF16))


def _mod_in_specs(n_lat, rows):
    return [pl.BlockSpec((1, rows), lambda k: (0, k)),
            pl.BlockSpec((n_lat, rows), lambda k: (0, k)),
            pl.BlockSpec((rows, 3 * D_MODEL), lambda k: (k, 0)),
            pl.BlockSpec((1, 3 * D_MODEL), lambda k: (0, 0))]


def _prep_kernel(cctx_ref, c_ref, wmod_ref, bmod_ref, win_ref, wout_ref,
                 mod_ref, winb_ref, woutb_ref, cond_scr):
    _mod_accumulate(cctx_ref, c_ref, wmod_ref, bmod_ref, mod_ref, cond_scr)
    winb_ref[...] = win_ref[...].astype(BF16)
    woutb_ref[...] = wout_ref[...].astype(BF16)


def _layer0_prep(c_ctx, c, w_mod, b_mod, w_in, w_out):
    n_lat = c.shape[0]
    rows = MOD_K_CHUNK
    return pl.pallas_call(
        _prep_kernel,
        grid=(D_MODEL // rows,),
        in_specs=_mod_in_specs(n_lat, rows) + [
            pl.BlockSpec((rows, w_in.shape[1]), lambda k: (k, 0)),
            pl.BlockSpec((rows, w_out.shape[1]), lambda k: (k, 0)),
        ],
        out_specs=[
            pl.BlockSpec((MOD_ROWS, 3 * D_MODEL), lambda k: (0, 0)),
            pl.BlockSpec((rows, w_in.shape[1]), lambda k: (k, 0)),
            pl.BlockSpec((rows, w_out.shape[1]), lambda k: (k, 0)),
        ],
        out_shape=[
            jax.ShapeDtypeStruct((MOD_ROWS, 3 * D_MODEL), F32),
            jax.ShapeDtypeStruct(w_in.shape, BF16),
            jax.ShapeDtypeStruct(w_out.shape, BF16),
        ],
        scratch_shapes=[pltpu.VMEM((MOD_ROWS, rows), F32)],
        compiler_params=pltpu.CompilerParams(dimension_semantics=("arbitrary",)),
        name="layer0_prep",
    )(c_ctx.reshape(1, D_MODEL), c, w_mod, b_mod.reshape(1, 3 * D_MODEL), w_in, w_out)


def _fourier_prep_kernel(x_ref, mod_ref, nw_ref, win_ref, wout_ref, mc_ref, ls_ref,
                         cctx_ref, c_ref, wmod1_ref, bmod1_ref, win1_ref, wout1_ref,
                         o_ref, mod1_ref, wqkvt1_ref, wz1_ref, wout1b_ref, ab_scr, z_scr, cond_scr, *, seq):
    _mod_accumulate(cctx_ref, c_ref, wmod1_ref, bmod1_ref, mod1_ref, cond_scr)
    w = win1_ref[...]
    wqkvt1_ref[...] = w[:, :BRANCH + 2 * KV_W].T.astype(BF16)
    wz1_ref[...] = w[:, BRANCH + 2 * KV_W:].astype(BF16)
    wout1b_ref[...] = wout1_ref[...].astype(BF16)
    _fourier_kernel(x_ref, mod_ref, nw_ref, win_ref, wout_ref, mc_ref, ls_ref, o_ref, ab_scr, z_scr,
                    seq=seq, per_request=False)


def _fourier_kernel(x_ref, mod_ref, nw_ref, win_ref, wout_ref, mc_ref, ls_ref, o_ref,
                    ab_scr, z_scr, *, seq, per_request):
    mod = _mod_row(mod_ref, per_request)
    gate = mod[:, 2 * D_MODEL:]
    nw = nw_ref[...]
    n_seq = x_ref.shape[0]
    n_chunks = seq // ROW_CHUNK
    for i in range(n_seq):
        for c in range(n_chunks):
            rows = slice(c * ROW_CHUNK, (c + 1) * ROW_CHUNK)
            h = _mod_norm(x_ref[i, rows, :], nw, mod).astype(BF16)
            uz = _dot(h, win_ref[...])
            z_scr[i, rows, :] = uz[:, BRANCH:]
            u = uz[:, :BRANCH].astype(BF16)
            for g in range(N_GROUPS):
                cols = slice(g * GROUP_W, (g + 1) * GROUP_W)
                t = _dot(u[:, cols], mc_ref[...])
                ab_scr[i, rows, cols] = t[:, :GROUP_W].astype(BF16)
                ab_scr[i, seq + c * ROW_CHUNK:seq + (c + 1) * ROW_CHUNK, cols] = t[:, GROUP_W:].astype(BF16)
    for i in range(n_seq):
        for c in range(n_chunks):
            rows = slice(c * ROW_CHUNK, (c + 1) * ROW_CHUNK)
            y = _dot(ls_ref[rows, :], ab_scr[i])
            y = (y * jax.nn.silu(z_scr[i, rows, :])).astype(BF16)
            o_ref[i, rows, :] = x_ref[i, rows, :] + gate * _dot(y, wout_ref[...])


def _dft_tables(seq):
    k = np.arange(GROUP_W)
    ang = 2.0 * np.pi * ((k[:, None] * k[None, :]) % GROUP_W) / GROUP_W
    mc = np.concatenate([np.cos(ang), np.sin(ang)], axis=1) / np.sqrt(GROUP_W)
    n = np.arange(seq)
    ang = 2.0 * np.pi * ((n[:, None] * n[None, :]) % seq) / seq
    ls = np.concatenate([np.cos(ang), -np.sin(ang)], axis=1) / np.sqrt(seq)
    return mc.astype(np.float32), ls.astype(np.float32)


def _const_spec(shape):
    return pl.BlockSpec(shape, lambda b: (0,) * len(shape))


def _fourier_layer(x, mod, per_request, nw, win, wout, next_layer=None):
    nb, seq, _ = x.shape
    n_seq = 1 if per_request else CTX_SEQS_PER_STEP
    assert nb % n_seq == 0
    n_steps = nb // n_seq
    mc, ls = _dft_tables(seq)
    mc = jnp.asarray(mc).astype(BF16)
    ls = jnp.asarray(ls).astype(BF16)
    in_specs = [
        pl.BlockSpec((n_seq, seq, D_MODEL), lambda b: (b, 0, 0)),
        MOD_SPEC,
        _const_spec((1, D_MODEL)),
        _const_spec((D_MODEL, 2 * BRANCH)),
        _const_spec((BRANCH, D_MODEL)),
        _const_spec((GROUP_W, 2 * GROUP_W)),
        _const_spec((seq, 2 * seq)),
    ]
    out_specs = [pl.BlockSpec((n_seq, seq, D_MODEL), lambda b: (b, 0, 0))]
    out_shape = [jax.ShapeDtypeStruct(x.shape, F32)]
    scratch_shapes = [
        pltpu.VMEM((n_seq, 2 * seq, BRANCH), BF16),
        pltpu.VMEM((n_seq, seq, BRANCH), F32),
    ]
    args = [x, mod, nw, win, wout, mc, ls]
    if next_layer is None:
        kern = functools.partial(_fourier_kernel, seq=seq, per_request=per_request)
    else:
        assert not per_request and D_MODEL % (n_steps * LANES) == 0
        c_ctx, c, w_mod1, b_mod1, w_in1, w_out1 = next_layer
        rows = D_MODEL // n_steps
        n_qkvz = 2 * BRANCH + 2 * KV_W
        kern = functools.partial(_fourier_prep_kernel, seq=seq)
        in_specs += _mod_in_specs(c.shape[0], rows) + [
            pl.BlockSpec((rows, n_qkvz), lambda b: (b, 0)),
            pl.BlockSpec((rows, D_MODEL), lambda b: (b, 0))]
        out_specs += [MOD_SPEC,
                      pl.BlockSpec((BRANCH + 2 * KV_W, rows), lambda b: (0, b)),
                      pl.BlockSpec((rows, BRANCH), lambda b: (b, 0)),
                      pl.BlockSpec((rows, D_MODEL), lambda b: (b, 0))]
        out_shape += [jax.ShapeDtypeStruct((MOD_ROWS, 3 * D_MODEL), F32),
                      jax.ShapeDtypeStruct((BRANCH + 2 * KV_W, D_MODEL), BF16),
                      jax.ShapeDtypeStruct((D_MODEL, BRANCH), BF16),
                      jax.ShapeDtypeStruct((BRANCH, D_MODEL), BF16)]
        scratch_shapes += [pltpu.VMEM((MOD_ROWS, rows), F32)]
        args += [c_ctx.reshape(1, D_MODEL), c, w_mod1, b_mod1.reshape(1, 3 * D_MODEL), w_in1, w_out1]
    outs = pl.pallas_call(
        kern,
        grid=(n_steps,),
        in_specs=in_specs,
        out_specs=out_specs,
        out_shape=out_shape,
        scratch_shapes=scratch_shapes,
        compiler_params=pltpu.CompilerParams(
            dimension_semantics=("arbitrary",), vmem_limit_bytes=VMEM_LIMIT),
        name=f"fourier_layer_s{seq}",
    )(*args)
    return outs[0] if next_layer is None else outs


def _head_weight_tile(w_ref, n_tokens):
    row = jnp.broadcast_to(w_ref[...], (HEAD_DIM, HEAD_DIM))
    ii = lax.broadcasted_iota(jnp.int32, (HEAD_DIM, HEAD_DIM), 0)
    jj = lax.broadcasted_iota(jnp.int32, (HEAD_DIM, HEAD_DIM), 1)
    col = jnp.sum(jnp.where(ii == jj, row, 0.0), axis=1, keepdims=True)
    return jnp.broadcast_to(col, (HEAD_DIM, n_tokens))


def _head_rms(t, w):
    return (t * lax.rsqrt(jnp.mean(t * t, axis=0, keepdims=True) + EPS)) * w


def _rope_t(t, cos, sin):
    half = HEAD_DIM // 2
    x1, x2 = t[:half], t[half:]
    return jnp.concatenate([x1 * cos - x2 * sin, x1 * sin + x2 * cos], axis=0)


def _head_scores(qn, g, keys, biases):
    zeros = jnp.zeros_like(qn)
    qz = jnp.concatenate([qn, zeros] if g % 2 == 0 else [zeros, qn], axis=0)
    blk = slice((g // 2) * LANES, (g // 2 + 1) * LANES)
    scores = []
    smax = None
    for k, bias in zip(keys, biases):
        s = _dot(k[:, blk], qz)
        s = s if bias is None else s + bias
        cmax = jnp.max(s, axis=0, keepdims=True)
        smax = cmax if smax is None else jnp.maximum(smax, cmax)
        scores.append(s)
    return scores, smax


ONES_ROWS = 16


def _with_ones_rows(vt):
    return jnp.concatenate([vt, jnp.ones((ONES_ROWS, vt.shape[1]), vt.dtype)], axis=0)


def _head_softmax_pv(scored, values_t, sink2):
    scores, smax = scored
    m = jnp.maximum(smax, sink2)
    acc = None
    for s, vt in zip(scores, values_t):
        pv = _dot(vt, jnp.exp2(s - m).astype(BF16))
        acc = pv if acc is None else acc + pv
    den = acc[HEAD_DIM:HEAD_DIM + 1] + jnp.exp2(sink2 - m)
    return acc[:HEAD_DIM] * (1.0 / den)


def _attend_heads(scores_fn, finish_fn, n_heads=N_HEADS):
    n_groups = n_heads // GQA
    pending = [scores_fn(hd) for hd in range(GQA)]
    for g in range(n_groups):
        nxt = [scores_fn(hd) for hd in range((g + 1) * GQA, (g + 2) * GQA)] if g + 1 < n_groups else None
        for i, sc in enumerate(pending):
            finish_fn(g * GQA + i, sc)
        pending = nxt


def _gate_out(x, o, z, gate, wout):
    y = (o * jax.nn.silu(z)).astype(BF16)
    return x + gate * _dot(y, wout)


def _attn_ctx_kernel(sink_ref, x_ref, mod_ref, nw_ref, wqkvt_ref, wz_ref, wout_ref, qw_ref, kw_ref,
                     o_ref, kto_ref, vto_ref, qkvt_scr, z_scr, ot_scr):
    seq = x_ref.shape[1]
    n_seq = x_ref.shape[0]
    mod = _mod_row(mod_ref, False)
    gate = mod[:, 2 * D_MODEL:]
    kw = _head_weight_tile(kw_ref, seq)
    qw = _head_weight_tile(qw_ref, seq)

    def project(i):
        h = _mod_norm(x_ref[i], nw_ref[...], mod).astype(BF16)
        qkvt_scr[i] = _dot_nt(wqkvt_ref[...], h)
        z_scr[i] = _dot(h, wz_ref[...])

    def attend(i):
        knt = jnp.concatenate(
            [_head_rms(qkvt_scr[i, BRANCH + g * HEAD_DIM:BRANCH + (g + 1) * HEAD_DIM, :], kw)
             for g in range(N_KV)], axis=0)
        kto_ref[i] = knt
        vtf = qkvt_scr[i, BRANCH + KV_W:, :]
        vto_ref[i] = vtf
        kb = knt.T.astype(BF16)
        vt = vtf.astype(BF16)
        vts = [_with_ones_rows(vt[g * HEAD_DIM:(g + 1) * HEAD_DIM]) for g in range(N_KV)]

        def scores_fn(hd):
            t = qkvt_scr[i, hd * HEAD_DIM:(hd + 1) * HEAD_DIM, :]
            qn = (_head_rms(t, qw) * (HEAD_DIM ** -0.5 * LOG2E)).astype(BF16)
            return _head_scores(qn, hd // GQA, [kb], [None])

        def finish_fn(hd, sc):
            ot_scr[i, hd * HEAD_DIM:(hd + 1) * HEAD_DIM, :] = _head_softmax_pv(
                sc, [vts[hd // GQA]], sink_ref[hd] * LOG2E)

        _attend_heads(scores_fn, finish_fn)

    def output(i):
        o_ref[i] = _gate_out(x_ref[i], ot_scr[i].T, z_scr[i], gate, wout_ref[...])

    for phase in (project, attend, output):
        for i in range(n_seq):
            phase(i)


def _attn_ctx_layer(x, mod, nw, wqkvt, wz, wout, qw, kw, sink):
    nb, seq, _ = x.shape
    n_seq = CTX_SEQS_PER_STEP
    assert nb % n_seq == 0
    return pl.pallas_call(
        _attn_ctx_kernel,
        grid=(nb // n_seq,),
        in_specs=[
            pl.BlockSpec(memory_space=pltpu.SMEM),
            pl.BlockSpec((n_seq, seq, D_MODEL), lambda b: (b, 0, 0)),
            MOD_SPEC,
            _const_spec((1, D_MODEL)),
            _const_spec((BRANCH + 2 * KV_W, D_MODEL)),
            _const_spec((D_MODEL, BRANCH)),
            _const_spec((BRANCH, D_MODEL)),
            _const_spec((1, HEAD_DIM)),
            _const_spec((1, HEAD_DIM)),
        ],
        out_specs=[
            pl.BlockSpec((n_seq, seq, D_MODEL), lambda b: (b, 0, 0)),
            pl.BlockSpec((n_seq, KV_W, seq), lambda b: (b, 0, 0)),
            pl.BlockSpec((n_seq, KV_W, seq), lambda b: (b, 0, 0)),
        ],
        out_shape=[
            jax.ShapeDtypeStruct(x.shape, F32),
            jax.ShapeDtypeStruct((nb, KV_W, seq), F32),
            jax.ShapeDtypeStruct((nb, KV_W, seq), F32),
        ],
        scratch_shapes=[
            pltpu.VMEM((n_seq, BRANCH + 2 * KV_W, seq), F32),
            pltpu.VMEM((n_seq, seq, BRANCH), F32),
            pltpu.VMEM((n_seq, BRANCH, seq), F32),
        ],
        compiler_params=pltpu.CompilerParams(
            dimension_semantics=("arbitrary",), vmem_limit_bytes=VMEM_LIMIT),
        name="attn_context_layer",
    )(sink, x, mod, nw, wqkvt, wz, wout, qw, kw)


def _attn_lat_kernel(sink_ref, x_ref, mod_ref, nw_ref, wqkvt_ref, wz_ref, wout_ref, qw_ref, kw_ref,
                     cos_ref, sin_ref, ck_ref, cv_ref, o_ref,
                     q_scr, z_scr, ot_scr, k_scr, vt_scr, *, seq):
    mod = _mod_row(mod_ref, True)
    gate = mod[:, 2 * D_MODEL:]
    nw = nw_ref[...]
    qw = _head_weight_tile(qw_ref, Q_BLOCK)
    kw = _head_weight_tile(kw_ref, Q_BLOCK)
    n_blocks = seq // Q_BLOCK
    kv_blocks = seq // BLOCK
    k_scr[0:BLOCK, :] = jnp.zeros((BLOCK, KV_W), BF16)
    k_scr[BLOCK + seq:2 * BLOCK + seq, :] = jnp.zeros((BLOCK, KV_W), BF16)
    vt_scr[0] = jnp.zeros((KV_W, BLOCK), BF16)
    vt_scr[kv_blocks + 1] = jnp.zeros((KV_W, BLOCK), BF16)

    def project(c):
        rows = slice(c * Q_BLOCK, (c + 1) * Q_BLOCK)
        h = _mod_norm(x_ref[0, rows, :], nw, mod).astype(BF16)
        z_scr[rows, :] = _dot(h, wz_ref[...])
        qkvt = _dot_nt(wqkvt_ref[...], h)
        cos = cos_ref[:, rows]
        sin = sin_ref[:, rows]
        for hd in range(N_HEADS):
            hr = slice(hd * HEAD_DIM, (hd + 1) * HEAD_DIM)
            t = _rope_t(_head_rms(qkvt[hr], qw), cos, sin)
            t = (t * (HEAD_DIM ** -0.5 * LOG2E)).astype(BF16)
            for j in range(HALVES):
                q_scr[c * HALVES + j, hd // 2, :, (hd % 2) * BLOCK:(hd % 2 + 1) * BLOCK] = (
                    t[:, j * BLOCK:(j + 1) * BLOCK])
        knt = jnp.concatenate(
            [_rope_t(_head_rms(qkvt[BRANCH + g * HEAD_DIM:BRANCH + (g + 1) * HEAD_DIM], kw), cos, sin)
             for g in range(N_KV)], axis=0)
        k_scr[BLOCK + c * Q_BLOCK:BLOCK + (c + 1) * Q_BLOCK, :] = knt.T.astype(BF16)
        vt = qkvt[BRANCH + KV_W:].astype(BF16)
        for j in range(Q_BLOCK // BLOCK):
            vt_scr[1 + c * (Q_BLOCK // BLOCK) + j] = vt[:, j * BLOCK:(j + 1) * BLOCK]

    ckb = ck_ref[0].T.astype(BF16)
    cvt = cv_ref[0].astype(BF16)
    cvts = [_with_ones_rows(cvt[g * HEAD_DIM:(g + 1) * HEAD_DIM]) for g in range(N_KV)]

    win_len = 3 * BLOCK
    n_pairs = N_HEADS // 2
    kj = lax.broadcasted_iota(jnp.int32, (win_len, 2 * BLOCK), 0)
    lane = lax.broadcasted_iota(jnp.int32, (win_len, 2 * BLOCK), 1)
    rel = kj - BLOCK - lane % BLOCK
    band = (rel >= -WINDOW) & (rel <= WINDOW)
    first_head = lax.broadcasted_iota(jnp.int32, (1, 2 * BLOCK), 1) < BLOCK

    def half_operands(hb):
        r0 = pl.multiple_of(hb * BLOCK, BLOCK)
        kpos = kj + (r0 - BLOCK)
        valid = jnp.where(band & (kpos >= 0) & (kpos < seq), 0.0, NEG_INF)
        kwin = k_scr[pl.ds(r0, win_len), :]
        vwin = jnp.concatenate([vt_scr[hb + j] for j in range(win_len // BLOCK)], axis=1)
        vwins = [_with_ones_rows(vwin[g * HEAD_DIM:(g + 1) * HEAD_DIM]) for g in range(N_KV)]
        return hb, valid, kwin, vwins

    def attend(n, carry):
        halves = [half_operands(n * HALVES + j) for j in range(HALVES)]

        def scores_fn(idx):
            (hb, valid, kwin, _), pair = halves[idx // n_pairs], idx % n_pairs
            return _head_scores(q_scr[hb, pair], pair // (GQA // 2), [kwin, ckb], [valid, None])

        def finish_fn(idx, sc):
            j, pair = idx // n_pairs, idx % n_pairs
            g = pair // (GQA // 2)
            sink2 = jnp.where(first_head, sink_ref[2 * pair], sink_ref[2 * pair + 1]) * LOG2E
            o = _head_softmax_pv(sc, [halves[j][3][g], cvts[g]], sink2)
            for i in range(2):
                hd = 2 * pair + i
                ot_scr[n, hd * HEAD_DIM:(hd + 1) * HEAD_DIM, j * BLOCK:(j + 1) * BLOCK] = (
                    o[:, i * BLOCK:(i + 1) * BLOCK])

        _attend_heads(scores_fn, finish_fn, HALVES * n_pairs)
        return carry

    for c in range(n_blocks):
        project(c)
    lax.fori_loop(0, n_blocks, attend, 0)
    for c in range(n_blocks):
        rows = slice(c * Q_BLOCK, (c + 1) * Q_BLOCK)
        o_ref[0, rows, :] = _gate_out(x_ref[0, rows, :], ot_scr[c].T, z_scr[rows, :], gate, wout_ref[...])


def _rope_tables_t(seq):
    pos = np.arange(seq)
    n_freq = HEAD_DIM // 4
    inv = ROPE_THETA ** (-np.arange(n_freq, dtype=np.float64) / n_freq)
    ang = np.concatenate([(pos // GRID_W)[:, None] * inv, (pos % GRID_W)[:, None] * inv], axis=-1)
    return np.cos(ang).T.astype(np.float32), np.sin(ang).T.astype(np.float32)


def _attn_lat_layer(x, mod, nw, wqkvt, wz, wout, qw, kw, sink, ckt, cvt):
    nb, seq, _ = x.shape
    past = ckt.shape[2]
    cos, sin = (jnp.asarray(t) for t in _rope_tables_t(seq))
    kern = functools.partial(_attn_lat_kernel, seq=seq)
    n_blocks = seq // Q_BLOCK
    return pl.pallas_call(
        kern,
        grid=(nb,),
        in_specs=[
            pl.BlockSpec(memory_space=pltpu.SMEM),
            pl.BlockSpec((1, seq, D_MODEL), lambda b: (b, 0, 0)),
            MOD_SPEC,
            _const_spec((1, D_MODEL)),
            _const_spec((BRANCH + 2 * KV_W, D_MODEL)),
            _const_spec((D_MODEL, BRANCH)),
            _const_spec((BRANCH, D_MODEL)),
            _const_spec((1, HEAD_DIM)),
            _const_spec((1, HEAD_DIM)),
            _const_spec((HEAD_DIM // 2, seq)),
            _const_spec((HEAD_DIM // 2, seq)),
            pl.BlockSpec((1, KV_W, past), lambda b: (b, 0, 0)),
            pl.BlockSpec((1, KV_W, past), lambda b: (b, 0, 0)),
        ],
        out_specs=pl.BlockSpec((1, seq, D_MODEL), lambda b: (b, 0, 0)),
        out_shape=jax.ShapeDtypeStruct(x.shape, F32),
        scratch_shapes=[
            pltpu.VMEM((seq // BLOCK, N_HEADS // 2, HEAD_DIM, 2 * BLOCK), BF16),
            pltpu.VMEM((seq, BRANCH), F32),
            pltpu.VMEM((n_blocks, BRANCH, Q_BLOCK), F32),
            pltpu.VMEM((seq + 2 * BLOCK, KV_W), BF16),
            pltpu.VMEM((seq // BLOCK + 2, KV_W, BLOCK), BF16),
        ],
        compiler_params=pltpu.CompilerParams(
            dimension_semantics=("arbitrary",), vmem_limit_bytes=VMEM_LIMIT),
        name="attn_latent_layer",
    )(sink, x, mod, nw, wqkvt, wz, wout, qw, kw, cos, sin, ckt, cvt)


def kernel(x_prompt, x_sample, cache_k_l1, cache_v_l1, c, c_ctx, norm_w_l0, w_mod_l0, b_mod_l0,
           w_in_l0, w_out_l0, norm_w_l1, w_mod_l1, b_mod_l1, w_in_l1, q_norm_w_l1, k_norm_w_l1,
           sink_l1, w_out_l1):
    nb_ctx, seq_ctx, _ = x_prompt.shape
    nb_lat = x_sample.shape[0]
    past = cache_k_l1.shape[1]
    assert 1 + nb_lat <= MOD_ROWS
    mod0, win0, wout0 = _layer0_prep(c_ctx, c, w_mod_l0, b_mod_l0, w_in_l0, w_out_l0)

    nw0 = norm_w_l0.reshape(1, D_MODEL)
    nw1 = norm_w_l1.reshape(1, D_MODEL)
    qw = q_norm_w_l1.reshape(1, HEAD_DIM)
    kw = k_norm_w_l1.reshape(1, HEAD_DIM)

    xp, mod1, wqkvt1, wz1, wout1 = _fourier_layer(
        x_prompt, mod0, False, nw0, win0, wout0,
        next_layer=(c_ctx, c, w_mod_l1, b_mod_l1, w_in_l1, w_out_l1))
    xs = _fourier_layer(x_sample, mod0, True, nw0, win0, wout0)

    def to_feature_major(t):
        return jnp.transpose(t, (0, 2, 3, 1)).reshape(t.shape[0], KV_W, t.shape[1])

    def from_feature_major(t):
        return jnp.transpose(t.reshape(t.shape[0], N_KV, HEAD_DIM, t.shape[2]), (0, 3, 1, 2))

    xp, new_kt, new_vt = _attn_ctx_layer(xp, mod1, nw1, wqkvt1, wz1, wout1, qw, kw, sink_l1)
    xs = _attn_lat_layer(xs, mod1, nw1, wqkvt1, wz1, wout1, qw, kw, sink_l1,
                         to_feature_major(cache_k_l1), to_feature_major(cache_v_l1))
    return (xp, xs, from_feature_major(new_kt), from_feature_major(new_vt))
```

```python
import functools

import numpy as np
import jax
import jax.numpy as jnp
from jax import lax
from jax.experimental import pallas as pl
from jax.experimental.pallas import tpu as pltpu

D_MODEL = 1024
BRANCH = 1024
N_GROUPS = 4
GROUP_W = BRANCH // N_GROUPS
HEAD_DIM = 64
N_HEADS = 16
N_KV = 4
GQA = N_HEADS // N_KV
KV_W = N_KV * HEAD_DIM
GRID_W = 64
WINDOW = 128
BLOCK = 128
ROPE_THETA = 10000.0
EPS = 1e-6
NEG_INF = -1e30
LANES = 128
ROW_CHUNK = 256
Q_BLOCK = 256
HALVES = Q_BLOCK // BLOCK
CTX_SEQS_PER_STEP = 2
VMEM_LIMIT = 56 * 1024 * 1024
MOD_ROWS = 8
MOD_K_CHUNK = 256
LOG2E = float(np.log2(np.e))

F32 = jnp.float32
BF16 = jnp.bfloat16


def _dot(a, b):
    return jnp.dot(a, b, preferred_element_type=F32)


def _dot_nt(a, b):
    return lax.dot_general(a, b, (((1,), (1,)), ((), ())), preferred_element_type=F32)


def _mod_row(mod_ref, per_request):
    if per_request:
        return mod_ref[pl.ds(1 + pl.program_id(0), 1), :]
    return mod_ref[0:1, :]


MOD_SPEC = pl.BlockSpec((MOD_ROWS, 3 * D_MODEL), lambda b: (0, 0))


def _mod_norm(x, nw, mod):
    shift = mod[:, :D_MODEL]
    scale = mod[:, D_MODEL:2 * D_MODEL]
    y = x * lax.rsqrt(jnp.mean(x * x, axis=-1, keepdims=True) + EPS)
    return (y * nw) * (1.0 + scale) + shift


def _mod_accumulate(cctx_ref, c_ref, w_ref, b_ref, o_ref, cond_scr):
    n_lat = c_ref.shape[0]
    cond_scr[...] = jnp.zeros_like(cond_scr)
    cond_scr[0:1, :] = cctx_ref[...]
    cond_scr[1:1 + n_lat, :] = c_ref[...]
    s = jax.nn.silu(cond_scr[...]).astype(BF16)

    @pl.when(pl.program_id(0) == 0)
    def _():
        o_ref[...] = jnp.broadcast_to(b_ref[...], o_ref.shape)

    o_ref[...] += _dot(s, w_ref[...].astype(BF16))


def _mod_in_specs(n_lat, rows):
    return [pl.BlockSpec((1, rows), lambda k: (0, k)),
            pl.BlockSpec((n_lat, rows), lambda k: (0, k)),
            pl.BlockSpec((rows, 3 * D_MODEL), lambda k: (k, 0)),
            pl.BlockSpec((1, 3 * D_MODEL), lambda k: (0, 0))]


def _prep_kernel(cctx_ref, c_ref, wmod_ref, bmod_ref, win_ref, wout_ref,
                 mod_ref, winb_ref, woutb_ref, cond_scr):
    _mod_accumulate(cctx_ref, c_ref, wmod_ref, bmod_ref, mod_ref, cond_scr)
    winb_ref[...] = win_ref[...].astype(BF16)
    woutb_ref[...] = wout_ref[...].astype(BF16)


def _layer0_prep(c_ctx, c, w_mod, b_mod, w_in, w_out):
    n_lat = c.shape[0]
    rows = MOD_K_CHUNK
    return pl.pallas_call(
        _prep_kernel,
        grid=(D_MODEL // rows,),
        in_specs=_mod_in_specs(n_lat, rows) + [
            pl.BlockSpec((rows, w_in.shape[1]), lambda k: (k, 0)),
            pl.BlockSpec((rows, w_out.shape[1]), lambda k: (k, 0)),
        ],
        out_specs=[
            pl.BlockSpec((MOD_ROWS, 3 * D_MODEL), lambda k: (0, 0)),
            pl.BlockSpec((rows, w_in.shape[1]), lambda k: (k, 0)),
            pl.BlockSpec((rows, w_out.shape[1]), lambda k: (k, 0)),
        ],
        out_shape=[
            jax.ShapeDtypeStruct((MOD_ROWS, 3 * D_MODEL), F32),
            jax.ShapeDtypeStruct(w_in.shape, BF16),
            jax.ShapeDtypeStruct(w_out.shape, BF16),
        ],
        scratch_shapes=[pltpu.VMEM((MOD_ROWS, rows), F32)],
        compiler_params=pltpu.CompilerParams(dimension_semantics=("arbitrary",)),
        name="layer0_prep",
    )(c_ctx.reshape(1, D_MODEL), c, w_mod, b_mod.reshape(1, 3 * D_MODEL), w_in, w_out)


def _fourier_prep_kernel(x_ref, mod_ref, nw_ref, win_ref, wout_ref, mc_ref, ls_ref,
                         cctx_ref, c_ref, wmod1_ref, bmod1_ref, win1_ref, wout1_ref,
                         o_ref, mod1_ref, wqkvt1_ref, wz1_ref, wout1b_ref, ab_scr, z_scr, cond_scr, *, seq):
    _mod_accumulate(cctx_ref, c_ref, wmod1_ref, bmod1_ref, mod1_ref, cond_scr)
    w = win1_ref[...]
    wqkvt1_ref[...] = w[:, :BRANCH + 2 * KV_W].T.astype(BF16)
    wz1_ref[...] = w[:, BRANCH + 2 * KV_W:].astype(BF16)
    wout1b_ref[...] = wout1_ref[...].astype(BF16)
    _fourier_kernel(x_ref, mod_ref, nw_ref, win_ref, wout_ref, mc_ref, ls_ref, o_ref, ab_scr, z_scr,
                    seq=seq, per_request=False)


def _fourier_kernel(x_ref, mod_ref, nw_ref, win_ref, wout_ref, mc_ref, ls_ref, o_ref,
                    ab_scr, z_scr, *, seq, per_request):
    mod = _mod_row(mod_ref, per_request)
    gate = mod[:, 2 * D_MODEL:]
    nw = nw_ref[...]
    n_seq = x_ref.shape[0]
    n_chunks = seq // ROW_CHUNK
    for i in range(n_seq):
        for c in range(n_chunks):
            rows = slice(c * ROW_CHUNK, (c + 1) * ROW_CHUNK)
            h = _mod_norm(x_ref[i, rows, :], nw, mod).astype(BF16)
            uz = _dot(h, win_ref[...])
            z_scr[i, rows, :] = uz[:, BRANCH:]
            u = uz[:, :BRANCH].astype(BF16)
            for g in range(N_GROUPS):
                cols = slice(g * GROUP_W, (g + 1) * GROUP_W)
                t = _dot(u[:, cols], mc_ref[...])
                ab_scr[i, rows, cols] = t[:, :GROUP_W].astype(BF16)
                ab_scr[i, seq + c * ROW_CHUNK:seq + (c + 1) * ROW_CHUNK, cols] = t[:, GROUP_W:].astype(BF16)
    for i in range(n_seq):
        for c in range(n_chunks):
            rows = slice(c * ROW_CHUNK, (c + 1) * ROW_CHUNK)
            y = _dot(ls_ref[rows, :], ab_scr[i])
            y = (y * jax.nn.silu(z_scr[i, rows, :])).astype(BF16)
            o_ref[i, rows, :] = x_ref[i, rows, :] + gate * _dot(y, wout_ref[...])


def _dft_tables(seq):
    k = np.arange(GROUP_W)
    ang = 2.0 * np.pi * ((k[:, None] * k[None, :]) % GROUP_W) / GROUP_W
    mc = np.concatenate([np.cos(ang), np.sin(ang)], axis=1) / np.sqrt(GROUP_W)
    n = np.arange(seq)
    ang = 2.0 * np.pi * ((n[:, None] * n[None, :]) % seq) / seq
    ls = np.concatenate([np.cos(ang), -np.sin(ang)], axis=1) / np.sqrt(seq)
    return mc.astype(np.float32), ls.astype(np.float32)


def _const_spec(shape):
    return pl.BlockSpec(shape, lambda b: (0,) * len(shape))


def _fourier_layer(x, mod, per_request, nw, win, wout, next_layer=None):
    nb, seq, _ = x.shape
    n_seq = 1 if per_request else CTX_SEQS_PER_STEP
    assert nb % n_seq == 0
    n_steps = nb // n_seq
    mc, ls = _dft_tables(seq)
    mc = jnp.asarray(mc).astype(BF16)
    ls = jnp.asarray(ls).astype(BF16)
    in_specs = [
        pl.BlockSpec((n_seq, seq, D_MODEL), lambda b: (b, 0, 0)),
        MOD_SPEC,
        _const_spec((1, D_MODEL)),
        _const_spec((D_MODEL, 2 * BRANCH)),
        _const_spec((BRANCH, D_MODEL)),
        _const_spec((GROUP_W, 2 * GROUP_W)),
        _const_spec((seq, 2 * seq)),
    ]
    out_specs = [pl.BlockSpec((n_seq, seq, D_MODEL), lambda b: (b, 0, 0))]
    out_shape = [jax.ShapeDtypeStruct(x.shape, F32)]
    scratch_shapes = [
        pltpu.VMEM((n_seq, 2 * seq, BRANCH), BF16),
        pltpu.VMEM((n_seq, seq, BRANCH), F32),
    ]
    args = [x, mod, nw, win, wout, mc, ls]
    if next_layer is None:
        kern = functools.partial(_fourier_kernel, seq=seq, per_request=per_request)
    else:
        assert not per_request and D_MODEL % (n_steps * LANES) == 0
        c_ctx, c, w_mod1, b_mod1, w_in1, w_out1 = next_layer
        rows = D_MODEL // n_steps
        n_qkvz = 2 * BRANCH + 2 * KV_W
        kern = functools.partial(_fourier_prep_kernel, seq=seq)
        in_specs += _mod_in_specs(c.shape[0], rows) + [
            pl.BlockSpec((rows, n_qkvz), lambda b: (b, 0)),
            pl.BlockSpec((rows, D_MODEL), lambda b: (b, 0))]
        out_specs += [MOD_SPEC,
                      pl.BlockSpec((BRANCH + 2 * KV_W, rows), lambda b: (0, b)),
                      pl.BlockSpec((rows, BRANCH), lambda b: (b, 0)),
                      pl.BlockSpec((rows, D_MODEL), lambda b: (b, 0))]
        out_shape += [jax.ShapeDtypeStruct((MOD_ROWS, 3 * D_MODEL), F32),
                      jax.ShapeDtypeStruct((BRANCH + 2 * KV_W, D_MODEL), BF16),
                      jax.ShapeDtypeStruct((D_MODEL, BRANCH), BF16),
                      jax.ShapeDtypeStruct((BRANCH, D_MODEL), BF16)]
        scratch_shapes += [pltpu.VMEM((MOD_ROWS, rows), F32)]
        args += [c_ctx.reshape(1, D_MODEL), c, w_mod1, b_mod1.reshape(1, 3 * D_MODEL), w_in1, w_out1]
    outs = pl.pallas_call(
        kern,
        grid=(n_steps,),
        in_specs=in_specs,
        out_specs=out_specs,
        out_shape=out_shape,
        scratch_shapes=scratch_shapes,
        compiler_params=pltpu.CompilerParams(
            dimension_semantics=("arbitrary",), vmem_limit_bytes=VMEM_LIMIT),
        name=f"fourier_layer_s{seq}",
    )(*args)
    return outs[0] if next_layer is None else outs


def _head_weight_tile(w_ref, n_tokens):
    row = jnp.broadcast_to(w_ref[...], (HEAD_DIM, HEAD_DIM))
    ii = lax.broadcasted_iota(jnp.int32, (HEAD_DIM, HEAD_DIM), 0)
    jj = lax.broadcasted_iota(jnp.int32, (HEAD_DIM, HEAD_DIM), 1)
    col = jnp.sum(jnp.where(ii == jj, row, 0.0), axis=1, keepdims=True)
    return jnp.broadcast_to(col, (HEAD_DIM, n_tokens))


def _head_rms(t, w):
    return (t * lax.rsqrt(jnp.mean(t * t, axis=0, keepdims=True) + EPS)) * w


def _rope_t(t, cos, sin):
    half = HEAD_DIM // 2
    x1, x2 = t[:half], t[half:]
    return jnp.concatenate([x1 * cos - x2 * sin, x1 * sin + x2 * cos], axis=0)


def _head_scores(qn, g, keys, biases):
    zeros = jnp.zeros_like(qn)
    qz = jnp.concatenate([qn, zeros] if g % 2 == 0 else [zeros, qn], axis=0)
    blk = slice((g // 2) * LANES, (g // 2 + 1) * LANES)
    chunks = []
    for k, bias in zip(keys, biases):
        for r in range(0, k.shape[0], KEY_CHUNK):
            s = _dot(k[r:r + KEY_CHUNK, blk], qz)
            s = s if bias is None else s + bias[r:r + KEY_CHUNK]
            cmax = jnp.max(s, axis=0, keepdims=True)
            chunks.append((jnp.exp2(s - cmax).astype(BF16), cmax))
    return chunks


ONES_ROWS = 16
KEY_CHUNK = 128


def _with_ones_rows(vt):
    return jnp.concatenate([vt, jnp.ones((ONES_ROWS, vt.shape[1]), vt.dtype)], axis=0)


def _head_softmax_pv(chunks, values_t, sink2):
    m = sink2
    for _, cmax in chunks:
        m = jnp.maximum(m, cmax)
    p = jnp.concatenate([e * jnp.exp2(cmax - m).astype(BF16) for e, cmax in chunks], axis=0)
    acc = _dot(values_t, p)
    den = acc[HEAD_DIM:HEAD_DIM + 1] + jnp.exp2(sink2 - m)
    return acc[:HEAD_DIM] * (1.0 / den)


def _attend_heads(scores_fn, finish_fn, n_heads=N_HEADS):
    n_groups = n_heads // GQA
    pending = [scores_fn(hd) for hd in range(GQA)]
    for g in range(n_groups):
        nxt = [scores_fn(hd) for hd in range((g + 1) * GQA, (g + 2) * GQA)] if g + 1 < n_groups else None
        for i, sc in enumerate(pending):
            finish_fn(g * GQA + i, sc)
        pending = nxt


def _gate_out(x, o, z, gate, wout):
    y = (o * jax.nn.silu(z)).astype(BF16)
    return x + gate * _dot(y, wout)


def _attn_ctx_kernel(sink_ref, x_ref, mod_ref, nw_ref, wqkvt_ref, wz_ref, wout_ref, qw_ref, kw_ref,
                     o_ref, kto_ref, vto_ref, qkvt_scr, z_scr, ot_scr):
    seq = x_ref.shape[1]
    n_seq = x_ref.shape[0]
    mod = _mod_row(mod_ref, False)
    gate = mod[:, 2 * D_MODEL:]
    kw = _head_weight_tile(kw_ref, seq)
    qw = _head_weight_tile(qw_ref, seq)

    def project(i):
        h = _mod_norm(x_ref[i], nw_ref[...], mod).astype(BF16)
        qkvt_scr[i] = _dot_nt(wqkvt_ref[...], h)
        z_scr[i] = _dot(h, wz_ref[...])

    def attend(i):
        knt = jnp.concatenate(
            [_head_rms(qkvt_scr[i, BRANCH + g * HEAD_DIM:BRANCH + (g + 1) * HEAD_DIM, :], kw)
             for g in range(N_KV)], axis=0)
        kto_ref[i] = knt
        vtf = qkvt_scr[i, BRANCH + KV_W:, :]
        vto_ref[i] = vtf
        kb = knt.T.astype(BF16)
        vt = vtf.astype(BF16)
        vts = [_with_ones_rows(vt[g * HEAD_DIM:(g + 1) * HEAD_DIM]) for g in range(N_KV)]

        def scores_fn(hd):
            t = qkvt_scr[i, hd * HEAD_DIM:(hd + 1) * HEAD_DIM, :]
            qn = (_head_rms(t, qw) * (HEAD_DIM ** -0.5 * LOG2E)).astype(BF16)
            return _head_scores(qn, hd // GQA, [kb], [None])

        def finish_fn(hd, sc):
            ot_scr[i, hd * HEAD_DIM:(hd + 1) * HEAD_DIM, :] = _head_softmax_pv(
                sc, vts[hd // GQA], sink_ref[hd] * LOG2E)

        _attend_heads(scores_fn, finish_fn)

    def output(i):
        o_ref[i] = _gate_out(x_ref[i], ot_scr[i].T, z_scr[i], gate, wout_ref[...])

    for phase in (project, attend, output):
        for i in range(n_seq):
            phase(i)


def _attn_ctx_layer(x, mod, nw, wqkvt, wz, wout, qw, kw, sink):
    nb, seq, _ = x.shape
    n_seq = CTX_SEQS_PER_STEP
    assert nb % n_seq == 0
    return pl.pallas_call(
        _attn_ctx_kernel,
        grid=(nb // n_seq,),
        in_specs=[
            pl.BlockSpec(memory_space=pltpu.SMEM),
            pl.BlockSpec((n_seq, seq, D_MODEL), lambda b: (b, 0, 0)),
            MOD_SPEC,
            _const_spec((1, D_MODEL)),
            _const_spec((BRANCH + 2 * KV_W, D_MODEL)),
            _const_spec((D_MODEL, BRANCH)),
            _const_spec((BRANCH, D_MODEL)),
            _const_spec((1, HEAD_DIM)),
            _const_spec((1, HEAD_DIM)),
        ],
        out_specs=[
            pl.BlockSpec((n_seq, seq, D_MODEL), lambda b: (b, 0, 0)),
            pl.BlockSpec((n_seq, KV_W, seq), lambda b: (b, 0, 0)),
            pl.BlockSpec((n_seq, KV_W, seq), lambda b: (b, 0, 0)),
        ],
        out_shape=[
            jax.ShapeDtypeStruct(x.shape, F32),
            jax.ShapeDtypeStruct((nb, KV_W, seq), F32),
            jax.ShapeDtypeStruct((nb, KV_W, seq), F32),
        ],
        scratch_shapes=[
            pltpu.VMEM((n_seq, BRANCH + 2 * KV_W, seq), F32),
            pltpu.VMEM((n_seq, seq, BRANCH), F32),
            pltpu.VMEM((n_seq, BRANCH, seq), F32),
        ],
        compiler_params=pltpu.CompilerParams(
            dimension_semantics=("arbitrary",), vmem_limit_bytes=VMEM_LIMIT),
        name="attn_context_layer",
    )(sink, x, mod, nw, wqkvt, wz, wout, qw, kw)


def _attn_lat_kernel(sink_ref, x_ref, mod_ref, nw_ref, wqkvt_ref, wz_ref, wout_ref, qw_ref, kw_ref,
                     cos_ref, sin_ref, ck_ref, cv_ref, o_ref,
                     q_scr, z_scr, ot_scr, k_scr, vt_scr, *, seq):
    mod = _mod_row(mod_ref, True)
    gate = mod[:, 2 * D_MODEL:]
    nw = nw_ref[...]
    qw = _head_weight_tile(qw_ref, Q_BLOCK)
    kw = _head_weight_tile(kw_ref, Q_BLOCK)
    n_blocks = seq // Q_BLOCK
    kv_blocks = seq // BLOCK
    k_scr[0:BLOCK, :] = jnp.zeros((BLOCK, KV_W), BF16)
    k_scr[BLOCK + seq:2 * BLOCK + seq, :] = jnp.zeros((BLOCK, KV_W), BF16)
    vt_scr[0] = jnp.zeros((KV_W, BLOCK), BF16)
    vt_scr[kv_blocks + 1] = jnp.zeros((KV_W, BLOCK), BF16)

    def project(c):
        rows = slice(c * Q_BLOCK, (c + 1) * Q_BLOCK)
        h = _mod_norm(x_ref[0, rows, :], nw, mod).astype(BF16)
        z_scr[rows, :] = _dot(h, wz_ref[...])
        qkvt = _dot_nt(wqkvt_ref[...], h)
        cos = cos_ref[:, rows]
        sin = sin_ref[:, rows]
        for hd in range(N_HEADS):
            hr = slice(hd * HEAD_DIM, (hd + 1) * HEAD_DIM)
            t = _rope_t(_head_rms(qkvt[hr], qw), cos, sin)
            t = (t * (HEAD_DIM ** -0.5 * LOG2E)).astype(BF16)
            for j in range(HALVES):
                q_scr[c * HALVES + j, hd // 2, :, (hd % 2) * BLOCK:(hd % 2 + 1) * BLOCK] = (
                    t[:, j * BLOCK:(j + 1) * BLOCK])
        knt = jnp.concatenate(
            [_rope_t(_head_rms(qkvt[BRANCH + g * HEAD_DIM:BRANCH + (g + 1) * HEAD_DIM], kw), cos, sin)
             for g in range(N_KV)], axis=0)
        k_scr[BLOCK + c * Q_BLOCK:BLOCK + (c + 1) * Q_BLOCK, :] = knt.T.astype(BF16)
        vt = qkvt[BRANCH + KV_W:].astype(BF16)
        for j in range(Q_BLOCK // BLOCK):
            vt_scr[1 + c * (Q_BLOCK // BLOCK) + j] = vt[:, j * BLOCK:(j + 1) * BLOCK]

    ckb = ck_ref[0].T.astype(BF16)
    cvt = cv_ref[0].astype(BF16)

    win_len = 3 * BLOCK
    n_pairs = N_HEADS // 2
    kj = lax.broadcasted_iota(jnp.int32, (win_len, 2 * BLOCK), 0)
    lane = lax.broadcasted_iota(jnp.int32, (win_len, 2 * BLOCK), 1)
    rel = kj - BLOCK - lane % BLOCK
    band = (rel >= -WINDOW) & (rel <= WINDOW)
    first_head = lax.broadcasted_iota(jnp.int32, (1, 2 * BLOCK), 1) < BLOCK

    def half_operands(hb):
        r0 = pl.multiple_of(hb * BLOCK, BLOCK)
        kpos = kj + (r0 - BLOCK)
        valid = jnp.where(band & (kpos >= 0) & (kpos < seq), 0.0, NEG_INF)
        kwin = k_scr[pl.ds(r0, win_len), :]
        vall = jnp.concatenate([vt_scr[hb + j] for j in range(win_len // BLOCK)] + [cvt], axis=1)
        valls = [_with_ones_rows(vall[g * HEAD_DIM:(g + 1) * HEAD_DIM]) for g in range(N_KV)]
        return hb, valid, kwin, valls

    def attend(n, carry):
        halves = [half_operands(n * HALVES + j) for j in range(HALVES)]

        def scores_fn(idx):
            (hb, valid, kwin, _), pair = halves[idx // n_pairs], idx % n_pairs
            return _head_scores(q_scr[hb, pair], pair // (GQA // 2), [kwin, ckb], [valid, None])

        def finish_fn(idx, sc):
            j, pair = idx // n_pairs, idx % n_pairs
            g = pair // (GQA // 2)
            sink2 = jnp.where(first_head, sink_ref[2 * pair], sink_ref[2 * pair + 1]) * LOG2E
            o = _head_softmax_pv(sc, halves[j][3][g], sink2)
            for i in range(2):
                hd = 2 * pair + i
                ot_scr[n, hd * HEAD_DIM:(hd + 1) * HEAD_DIM, j * BLOCK:(j + 1) * BLOCK] = (
                    o[:, i * BLOCK:(i + 1) * BLOCK])

        _attend_heads(scores_fn, finish_fn, HALVES * n_pairs)
        return carry

    for c in range(n_blocks):
        project(c)
    lax.fori_loop(0, n_blocks, attend, 0)
    for c in range(n_blocks):
        rows = slice(c * Q_BLOCK, (c + 1) * Q_BLOCK)
        o_ref[0, rows, :] = _gate_out(x_ref[0, rows, :], ot_scr[c].T, z_scr[rows, :], gate, wout_ref[...])


def _rope_tables_t(seq):
    pos = np.arange(seq)
    n_freq = HEAD_DIM // 4
    inv = ROPE_THETA ** (-np.arange(n_freq, dtype=np.float64) / n_freq)
    ang = np.concatenate([(pos // GRID_W)[:, None] * inv, (pos % GRID_W)[:, None] * inv], axis=-1)
    return np.cos(ang).T.astype(np.float32), np.sin(ang).T.astype(np.float32)


def _attn_lat_layer(x, mod, nw, wqkvt, wz, wout, qw, kw, sink, ckt, cvt):
    nb, seq, _ = x.shape
    past = ckt.shape[2]
    cos, sin = (jnp.asarray(t) for t in _rope_tables_t(seq))
    kern = functools.partial(_attn_lat_kernel, seq=seq)
    n_blocks = seq // Q_BLOCK
    return pl.pallas_call(
        kern,
        grid=(nb,),
        in_specs=[
            pl.BlockSpec(memory_space=pltpu.SMEM),
            pl.BlockSpec((1, seq, D_MODEL), lambda b: (b, 0, 0)),
            MOD_SPEC,
            _const_spec((1, D_MODEL)),
            _const_spec((BRANCH + 2 * KV_W, D_MODEL)),
            _const_spec((D_MODEL, BRANCH)),
            _const_spec((BRANCH, D_MODEL)),
            _const_spec((1, HEAD_DIM)),
            _const_spec((1, HEAD_DIM)),
            _const_spec((HEAD_DIM // 2, seq)),
            _const_spec((HEAD_DIM // 2, seq)),
            pl.BlockSpec((1, KV_W, past), lambda b: (b, 0, 0)),
            pl.BlockSpec((1, KV_W, past), lambda b: (b, 0, 0)),
        ],
        out_specs=pl.BlockSpec((1, seq, D_MODEL), lambda b: (b, 0, 0)),
        out_shape=jax.ShapeDtypeStruct(x.shape, F32),
        scratch_shapes=[
            pltpu.VMEM((seq // BLOCK, N_HEADS // 2, HEAD_DIM, 2 * BLOCK), BF16),
            pltpu.VMEM((seq, BRANCH), F32),
            pltpu.VMEM((n_blocks, BRANCH, Q_BLOCK), F32),
            pltpu.VMEM((seq + 2 * BLOCK, KV_W), BF16),
            pltpu.VMEM((seq // BLOCK + 2, KV_W, BLOCK), BF16),
        ],
        compiler_params=pltpu.CompilerParams(
            dimension_semantics=("arbitrary",), vmem_limit_bytes=VMEM_LIMIT),
        name="attn_latent_layer",
    )(sink, x, mod, nw, wqkvt, wz, wout, qw, kw, cos, sin, ckt, cvt)


def kernel(x_prompt, x_sample, cache_k_l1, cache_v_l1, c, c_ctx, norm_w_l0, w_mod_l0, b_mod_l0,
           w_in_l0, w_out_l0, norm_w_l1, w_mod_l1, b_mod_l1, w_in_l1, q_norm_w_l1, k_norm_w_l1,
           sink_l1, w_out_l1):
    nb_ctx, seq_ctx, _ = x_prompt.shape
    nb_lat = x_sample.shape[0]
    past = cache_k_l1.shape[1]
    assert 1 + nb_lat <= MOD_ROWS
    mod0, win0, wout0 = _layer0_prep(c_ctx, c, w_mod_l0, b_mod_l0, w_in_l0, w_out_l0)

    nw0 = norm_w_l0.reshape(1, D_MODEL)
    nw1 = norm_w_l1.reshape(1, D_MODEL)
    qw = q_norm_w_l1.reshape(1, HEAD_DIM)
    kw = k_norm_w_l1.reshape(1, HEAD_DIM)

    xp, mod1, wqkvt1, wz1, wout1 = _fourier_layer(
        x_prompt, mod0, False, nw0, win0, wout0,
        next_layer=(c_ctx, c, w_mod_l1, b_mod_l1, w_in_l1, w_out_l1))
    xs = _fourier_layer(x_sample, mod0, True, nw0, win0, wout0)

    def to_feature_major(t):
        return jnp.transpose(t, (0, 2, 3, 1)).reshape(t.shape[0], KV_W, t.shape[1])

    def from_feature_major(t):
        return jnp.transpose(t.reshape(t.shape[0], N_KV, HEAD_DIM, t.shape[2]), (0, 3, 1, 2))

    xp, new_kt, new_vt = _attn_ctx_layer(xp, mod1, nw1, wqkvt1, wz1, wout1, qw, kw, sink_l1)
    xs = _attn_lat_layer(xs, mod1, nw1, wqkvt1, wz1, wout1, qw, kw, sink_l1,
                         to_feature_major(cache_k_l1), to_feature_major(cache_v_l1))
    return (xp, xs, from_feature_major(new_kt), from_feature_major(new_vt))
```

```python
import functools

import numpy as np
import jax
import jax.numpy as jnp
from jax import lax
from jax.experimental import pallas as pl
from jax.experimental.pallas import tpu as pltpu

D_MODEL = 1024
BRANCH = 1024
N_GROUPS = 4
GROUP_W = BRANCH // N_GROUPS
HALF_W = GROUP_W // 2
HEAD_DIM = 64
N_HEADS = 16
N_KV = 4
GQA = N_HEADS // N_KV
KV_W = N_KV * HEAD_DIM
GRID_W = 64
WINDOW = 128
BLOCK = 128
ROPE_THETA = 10000.0
EPS = 1e-6
NEG_INF = -1e30
LANES = 128
ROW_CHUNK = 256
Q_BLOCK = 256
HALVES = Q_BLOCK // BLOCK
CTX_SEQS_PER_STEP = 2
VMEM_LIMIT = 56 * 1024 * 1024
MOD_ROWS = 8
MOD_K_CHUNK = 256
LOG2E = float(np.log2(np.e))

F32 = jnp.float32
BF16 = jnp.bfloat16


def _dot(a, b):
    return jnp.dot(a, b, preferred_element_type=F32)


def _dot_nt(a, b):
    return lax.dot_general(a, b, (((1,), (1,)), ((), ())), preferred_element_type=F32)


def _mod_row(mod_ref, per_request):
    if per_request:
        return mod_ref[pl.ds(1 + pl.program_id(0), 1), :]
    return mod_ref[0:1, :]


MOD_SPEC = pl.BlockSpec((MOD_ROWS, 3 * D_MODEL), lambda b: (0, 0))


def _mod_norm(x, nw, mod):
    shift = mod[:, :D_MODEL]
    scale = mod[:, D_MODEL:2 * D_MODEL]
    y = x * lax.rsqrt(jnp.mean(x * x, axis=-1, keepdims=True) + EPS)
    return (y * nw) * (1.0 + scale) + shift


def _mod_accumulate(cctx_ref, c_ref, w_ref, b_ref, o_ref, cond_scr):
    n_lat = c_ref.shape[0]
    cond_scr[...] = jnp.zeros_like(cond_scr)
    cond_scr[0:1, :] = cctx_ref[...]
    cond_scr[1:1 + n_lat, :] = c_ref[...]
    s = jax.nn.silu(cond_scr[...]).astype(BF16)

    @pl.when(pl.program_id(0) == 0)
    def _():
        o_ref[...] = jnp.broadcast_to(b_ref[...], o_ref.shape)

    o_ref[...] += _dot(s, w_ref[...].astype(BF16))


def _mod_in_specs(n_lat, rows):
    return [pl.BlockSpec((1, rows), lambda k: (0, k)),
            pl.BlockSpec((n_lat, rows), lambda k: (0, k)),
            pl.BlockSpec((rows, 3 * D_MODEL), lambda k: (k, 0)),
            pl.BlockSpec((1, 3 * D_MODEL), lambda k: (0, 0))]


def _mirror_perm():
    j = np.arange(GROUP_W)
    return np.where(j <= HALF_W, j, GROUP_W + HALF_W - j)


def _prep_kernel(cctx_ref, c_ref, wmod_ref, bmod_ref, win_ref, wout_ref, pm_ref,
                 mod_ref, winb_ref, woutb_ref, cond_scr):
    _mod_accumulate(cctx_ref, c_ref, wmod_ref, bmod_ref, mod_ref, cond_scr)
    w = win_ref[...].astype(BF16)
    winb_ref[:, :BRANCH] = w[:, :BRANCH]
    pm = pm_ref[...]
    for g in range(N_GROUPS):
        cols = slice(BRANCH + g * GROUP_W, BRANCH + (g + 1) * GROUP_W)
        winb_ref[:, cols] = _dot(w[:, cols], pm).astype(BF16)
    woutb_ref[...] = _dot(pm, wout_ref[...].astype(BF16)).astype(BF16)


def _layer0_prep(c_ctx, c, w_mod, b_mod, w_in, w_out):
    n_lat = c.shape[0]
    rows = MOD_K_CHUNK
    assert rows == GROUP_W
    perm = _mirror_perm()
    pm = jnp.asarray((np.arange(GROUP_W)[:, None] == perm[None, :]).astype(np.float32)).astype(BF16)
    return pl.pallas_call(
        _prep_kernel,
        grid=(D_MODEL // rows,),
        in_specs=_mod_in_specs(n_lat, rows) + [
            pl.BlockSpec((rows, w_in.shape[1]), lambda k: (k, 0)),
            pl.BlockSpec((rows, w_out.shape[1]), lambda k: (k, 0)),
            pl.BlockSpec((GROUP_W, GROUP_W), lambda k: (0, 0)),
        ],
        out_specs=[
            pl.BlockSpec((MOD_ROWS, 3 * D_MODEL), lambda k: (0, 0)),
            pl.BlockSpec((rows, w_in.shape[1]), lambda k: (k, 0)),
            pl.BlockSpec((rows, w_out.shape[1]), lambda k: (k, 0)),
        ],
        out_shape=[
            jax.ShapeDtypeStruct((MOD_ROWS, 3 * D_MODEL), F32),
            jax.ShapeDtypeStruct(w_in.shape, BF16),
            jax.ShapeDtypeStruct(w_out.shape, BF16),
        ],
        scratch_shapes=[pltpu.VMEM((MOD_ROWS, rows), F32)],
        compiler_params=pltpu.CompilerParams(dimension_semantics=("arbitrary",)),
        name="layer0_prep",
    )(c_ctx.reshape(1, D_MODEL), c, w_mod, b_mod.reshape(1, 3 * D_MODEL), w_in, w_out, pm)


def _fourier_prep_kernel(x_ref, mod_ref, nw_ref, win_ref, wout_ref, m1_ref, cs_ref, ss_ref,
                         cctx_ref, c_ref, wmod1_ref, bmod1_ref, win1_ref, wout1_ref,
                         o_ref, mod1_ref, wqkvt1_ref, wz1_ref, wout1b_ref,
                         ta_scr, tb_scr, tr_scr, z_scr, cond_scr, *, seq):
    _mod_accumulate(cctx_ref, c_ref, wmod1_ref, bmod1_ref, mod1_ref, cond_scr)
    w = win1_ref[...]
    wqkvt1_ref[...] = w[:, :BRANCH + 2 * KV_W].T.astype(BF16)
    wz1_ref[...] = w[:, BRANCH + 2 * KV_W:].astype(BF16)
    wout1b_ref[...] = wout1_ref[...].astype(BF16)
    _fourier_kernel(x_ref, mod_ref, nw_ref, win_ref, wout_ref, m1_ref, cs_ref, ss_ref, o_ref,
                    ta_scr, tb_scr, tr_scr, z_scr, seq=seq, per_request=False)


def _fourier_kernel(x_ref, mod_ref, nw_ref, win_ref, wout_ref, m1_ref, cs_ref, ss_ref, o_ref,
                    ta_scr, tb_scr, tr_scr, z_scr, *, seq, per_request):
    mod = _mod_row(mod_ref, per_request)
    gate = mod[:, 2 * D_MODEL:]
    nw = nw_ref[...]
    n_seq = x_ref.shape[0]
    n_chunks = seq // ROW_CHUNK
    lane = lax.broadcasted_iota(jnp.int32, (ROW_CHUNK, HALF_W), 1)
    for i in range(n_seq):
        for c in range(n_chunks):
            rows = slice(c * ROW_CHUNK, (c + 1) * ROW_CHUNK)
            h = _mod_norm(x_ref[i, rows, :], nw, mod).astype(BF16)
            uz = _dot(h, win_ref[...])
            z_scr[i, rows, :] = uz[:, BRANCH:]
            u = uz[:, :BRANCH].astype(BF16)
            tr = jnp.zeros((ROW_CHUNK, HALF_W), F32)
            for g in range(N_GROUPS):
                t = _dot(u[:, g * GROUP_W:(g + 1) * GROUP_W], m1_ref[...])
                half = slice(g * HALF_W, (g + 1) * HALF_W)
                ta_scr[i, rows, half] = t[:, :HALF_W].astype(BF16)
                tb = t[:, HALF_W:]
                tb_scr[i, rows, half] = tb.astype(BF16)
                tr = jnp.where(lane == g, tb if g == 0 else pltpu.roll(tb, g, axis=1), tr)
            tr_scr[i, rows, :] = tr.astype(BF16)
    for i in range(n_seq):
        for c in range(n_chunks):
            rows = slice(c * ROW_CHUNK, (c + 1) * ROW_CHUNK)
            cs = cs_ref[rows, :]
            p = _dot(cs, ta_scr[i])
            q = _dot(ss_ref[rows, :], tb_scr[i])
            r = _dot(cs, tr_scr[i])
            parts = []
            for g in range(N_GROUPS):
                half = slice(g * HALF_W, (g + 1) * HALF_W)
                pg, qg = p[:, half], q[:, half]
                rg = r if g == 0 else pltpu.roll(r, HALF_W - g, axis=1)
                parts.append(jnp.where(lane == 0, pg, pg - qg))
                parts.append(jnp.where(lane == 0, rg, pg + qg))
            y = jnp.concatenate(parts, axis=1)
            y = (y * jax.nn.silu(z_scr[i, rows, :])).astype(BF16)
            o_ref[i, rows, :] = x_ref[i, rows, :] + gate * _dot(y, wout_ref[...])


def _dft_tables(seq):
    c = np.arange(GROUP_W)[:, None]
    k = np.arange(HALF_W)[None, :]
    cos_lo = np.cos(2.0 * np.pi * ((c * k) % GROUP_W) / GROUP_W)
    sin_lo = np.sin(2.0 * np.pi * ((c * k) % GROUP_W) / GROUP_W)
    sin_lo[:, 0] = np.cos(np.pi * c[:, 0])
    m1 = np.concatenate([cos_lo, sin_lo], axis=1) / np.sqrt(GROUP_W)
    n = np.arange(seq)
    ang = 2.0 * np.pi * ((n[:, None] * n[None, :]) % seq) / seq
    cs = np.cos(ang) / np.sqrt(seq)
    ss = np.sin(ang) / np.sqrt(seq)
    return m1.astype(np.float32), cs.astype(np.float32), ss.astype(np.float32)


def _const_spec(shape):
    return pl.BlockSpec(shape, lambda b: (0,) * len(shape))


def _fourier_layer(x, mod, per_request, nw, win, wout, next_layer=None):
    nb, seq, _ = x.shape
    n_seq = 1 if per_request else CTX_SEQS_PER_STEP
    assert nb % n_seq == 0
    n_steps = nb // n_seq
    m1, cs, ss = (jnp.asarray(t).astype(BF16) for t in _dft_tables(seq))
    in_specs = [
        pl.BlockSpec((n_seq, seq, D_MODEL), lambda b: (b, 0, 0)),
        MOD_SPEC,
        _const_spec((1, D_MODEL)),
        _const_spec((D_MODEL, 2 * BRANCH)),
        _const_spec((BRANCH, D_MODEL)),
        _const_spec((GROUP_W, GROUP_W)),
        _const_spec((seq, seq)),
        _const_spec((seq, seq)),
    ]
    out_specs = [pl.BlockSpec((n_seq, seq, D_MODEL), lambda b: (b, 0, 0))]
    out_shape = [jax.ShapeDtypeStruct(x.shape, F32)]
    scratch_shapes = [
        pltpu.VMEM((n_seq, seq, N_GROUPS * HALF_W), BF16),
        pltpu.VMEM((n_seq, seq, N_GROUPS * HALF_W), BF16),
        pltpu.VMEM((n_seq, seq, HALF_W), BF16),
        pltpu.VMEM((n_seq, seq, BRANCH), F32),
    ]
    args = [x, mod, nw, win, wout, m1, cs, ss]
    if next_layer is None:
        kern = functools.partial(_fourier_kernel, seq=seq, per_request=per_request)
    else:
        assert not per_request and D_MODEL % (n_steps * LANES) == 0
        c_ctx, c, w_mod1, b_mod1, w_in1, w_out1 = next_layer
        rows = D_MODEL // n_steps
        n_qkvz = 2 * BRANCH + 2 * KV_W
        kern = functools.partial(_fourier_prep_kernel, seq=seq)
        in_specs += _mod_in_specs(c.shape[0], rows) + [
            pl.BlockSpec((rows, n_qkvz), lambda b: (b, 0)),
            pl.BlockSpec((rows, D_MODEL), lambda b: (b, 0))]
        out_specs += [MOD_SPEC,
                      pl.BlockSpec((BRANCH + 2 * KV_W, rows), lambda b: (0, b)),
                      pl.BlockSpec((rows, BRANCH), lambda b: (b, 0)),
                      pl.BlockSpec((rows, D_MODEL), lambda b: (b, 0))]
        out_shape += [jax.ShapeDtypeStruct((MOD_ROWS, 3 * D_MODEL), F32),
                      jax.ShapeDtypeStruct((BRANCH + 2 * KV_W, D_MODEL), BF16),
                      jax.ShapeDtypeStruct((D_MODEL, BRANCH), BF16),
                      jax.ShapeDtypeStruct((BRANCH, D_MODEL), BF16)]
        scratch_shapes += [pltpu.VMEM((MOD_ROWS, rows), F32)]
        args += [c_ctx.reshape(1, D_MODEL), c, w_mod1, b_mod1.reshape(1, 3 * D_MODEL), w_in1, w_out1]
    outs = pl.pallas_call(
        kern,
        grid=(n_steps,),
        in_specs=in_specs,
        out_specs=out_specs,
        out_shape=out_shape,
        scratch_shapes=scratch_shapes,
        compiler_params=pltpu.CompilerParams(
            dimension_semantics=("arbitrary",), vmem_limit_bytes=VMEM_LIMIT),
        name=f"fourier_layer_s{seq}",
    )(*args)
    return outs[0] if next_layer is None else outs


def _head_weight_tile(w_ref, n_tokens):
    row = jnp.broadcast_to(w_ref[...], (HEAD_DIM, HEAD_DIM))
    ii = lax.broadcasted_iota(jnp.int32, (HEAD_DIM, HEAD_DIM), 0)
    jj = lax.broadcasted_iota(jnp.int32, (HEAD_DIM, HEAD_DIM), 1)
    col = jnp.sum(jnp.where(ii == jj, row, 0.0), axis=1, keepdims=True)
    return jnp.broadcast_to(col, (HEAD_DIM, n_tokens))


def _head_rms(t, w):
    return (t * lax.rsqrt(jnp.mean(t * t, axis=0, keepdims=True) + EPS)) * w


def _rope_t(t, cos, sin):
    half = HEAD_DIM // 2
    x1, x2 = t[:half], t[half:]
    return jnp.concatenate([x1 * cos - x2 * sin, x1 * sin + x2 * cos], axis=0)


def _head_scores(qn, g, keys, biases):
    zeros = jnp.zeros_like(qn)
    qz = jnp.concatenate([qn, zeros] if g % 2 == 0 else [zeros, qn], axis=0)
    blk = slice((g // 2) * LANES, (g // 2 + 1) * LANES)
    scores = []
    smax = None
    for k, bias in zip(keys, biases):
        s = _dot(k[:, blk], qz)
        s = s if bias is None else s + bias
        cmax = jnp.max(s, axis=0, keepdims=True)
        smax = cmax if smax is None else jnp.maximum(smax, cmax)
        scores.append(s)
    return scores, smax


ONES_ROWS = 16


def _with_ones_rows(vt):
    return jnp.concatenate([vt, jnp.ones((ONES_ROWS, vt.shape[1]), vt.dtype)], axis=0)


def _head_softmax_pv(scored, values_t, sink2):
    scores, smax = scored
    m = jnp.maximum(smax, sink2)
    p = jnp.concatenate([jnp.exp2(s - m).astype(BF16) for s in scores], axis=0)
    acc = _dot(values_t, p)
    den = acc[HEAD_DIM:HEAD_DIM + 1] + jnp.exp2(sink2 - m)
    return acc[:HEAD_DIM] * (1.0 / den)


def _attend_heads(scores_fn, finish_fn, n_heads=N_HEADS):
    n_groups = n_heads // GQA
    pending = [scores_fn(hd) for hd in range(GQA)]
    for g in range(n_groups):
        nxt = [scores_fn(hd) for hd in range((g + 1) * GQA, (g + 2) * GQA)] if g + 1 < n_groups else None
        for i, sc in enumerate(pending):
            finish_fn(g * GQA + i, sc)
        pending = nxt


def _gate_out(x, o, z, gate, wout):
    y = (o * jax.nn.silu(z)).astype(BF16)
    return x + gate * _dot(y, wout)


def _attn_ctx_kernel(sink_ref, x_ref, mod_ref, nw_ref, wqkvt_ref, wz_ref, wout_ref, qw_ref, kw_ref,
                     o_ref, kto_ref, vto_ref, qkvt_scr, z_scr, ot_scr):
    seq = x_ref.shape[1]
    n_seq = x_ref.shape[0]
    mod = _mod_row(mod_ref, False)
    gate = mod[:, 2 * D_MODEL:]
    kw = _head_weight_tile(kw_ref, seq)
    qw = _head_weight_tile(qw_ref, seq)

    def project(i):
        h = _mod_norm(x_ref[i], nw_ref[...], mod).astype(BF16)
        qkvt_scr[i] = _dot_nt(wqkvt_ref[...], h)
        z_scr[i] = _dot(h, wz_ref[...])

    def attend(i):
        knt = jnp.concatenate(
            [_head_rms(qkvt_scr[i, BRANCH + g * HEAD_DIM:BRANCH + (g + 1) * HEAD_DIM, :], kw)
             for g in range(N_KV)], axis=0)
        kto_ref[i] = knt
        vtf = qkvt_scr[i, BRANCH + KV_W:, :]
        vto_ref[i] = vtf
        kb = knt.T.astype(BF16)
        vt = vtf.astype(BF16)
        vts = [_with_ones_rows(vt[g * HEAD_DIM:(g + 1) * HEAD_DIM]) for g in range(N_KV)]

        def scores_fn(hd):
            t = qkvt_scr[i, hd * HEAD_DIM:(hd + 1) * HEAD_DIM, :]
            qn = (_head_rms(t, qw) * (HEAD_DIM ** -0.5 * LOG2E)).astype(BF16)
            return _head_scores(qn, hd // GQA, [kb], [None])

        def finish_fn(hd, sc):
            ot_scr[i, hd * HEAD_DIM:(hd + 1) * HEAD_DIM, :] = _head_softmax_pv(
                sc, vts[hd // GQA], sink_ref[hd] * LOG2E)

        _attend_heads(scores_fn, finish_fn)

    def output(i):
        o_ref[i] = _gate_out(x_ref[i], ot_scr[i].T, z_scr[i], gate, wout_ref[...])

    for phase in (project, attend, output):
        for i in range(n_seq):
            phase(i)


def _attn_ctx_layer(x, mod, nw, wqkvt, wz, wout, qw, kw, sink):
    nb, seq, _ = x.shape
    n_seq = CTX_SEQS_PER_STEP
    assert nb % n_seq == 0
    return pl.pallas_call(
        _attn_ctx_kernel,
        grid=(nb // n_seq,),
        in_specs=[
            pl.BlockSpec(memory_space=pltpu.SMEM),
            pl.BlockSpec((n_seq, seq, D_MODEL), lambda b: (b, 0, 0)),
            MOD_SPEC,
            _const_spec((1, D_MODEL)),
            _const_spec((BRANCH + 2 * KV_W, D_MODEL)),
            _const_spec((D_MODEL, BRANCH)),
            _const_spec((BRANCH, D_MODEL)),
            _const_spec((1, HEAD_DIM)),
            _const_spec((1, HEAD_DIM)),
        ],
        out_specs=[
            pl.BlockSpec((n_seq, seq, D_MODEL), lambda b: (b, 0, 0)),
            pl.BlockSpec((n_seq, KV_W, seq), lambda b: (b, 0, 0)),
            pl.BlockSpec((n_seq, KV_W, seq), lambda b: (b, 0, 0)),
        ],
        out_shape=[
            jax.ShapeDtypeStruct(x.shape, F32),
            jax.ShapeDtypeStruct((nb, KV_W, seq), F32),
            jax.ShapeDtypeStruct((nb, KV_W, seq), F32),
        ],
        scratch_shapes=[
            pltpu.VMEM((n_seq, BRANCH + 2 * KV_W, seq), F32),
            pltpu.VMEM((n_seq, seq, BRANCH), F32),
            pltpu.VMEM((n_seq, BRANCH, seq), F32),
        ],
        compiler_params=pltpu.CompilerParams(
            dimension_semantics=("arbitrary",), vmem_limit_bytes=VMEM_LIMIT),
        name="attn_context_layer",
    )(sink, x, mod, nw, wqkvt, wz, wout, qw, kw)


def _attn_lat_kernel(sink_ref, x_ref, mod_ref, nw_ref, wqkvt_ref, wz_ref, wout_ref, qw_ref, kw_ref,
                     cos_ref, sin_ref, ck_ref, cv_ref, o_ref,
                     q_scr, z_scr, ot_scr, k_scr, vt_scr, *, seq):
    mod = _mod_row(mod_ref, True)
    gate = mod[:, 2 * D_MODEL:]
    nw = nw_ref[...]
    qw = _head_weight_tile(qw_ref, Q_BLOCK)
    kw = _head_weight_tile(kw_ref, Q_BLOCK)
    n_blocks = seq // Q_BLOCK
    kv_blocks = seq // BLOCK
    k_scr[0:BLOCK, :] = jnp.zeros((BLOCK, KV_W), BF16)
    k_scr[BLOCK + seq:2 * BLOCK + seq, :] = jnp.zeros((BLOCK, KV_W), BF16)
    vt_scr[0] = jnp.zeros((KV_W, BLOCK), BF16)
    vt_scr[kv_blocks + 1] = jnp.zeros((KV_W, BLOCK), BF16)

    def project(c):
        rows = slice(c * Q_BLOCK, (c + 1) * Q_BLOCK)
        h = _mod_norm(x_ref[0, rows, :], nw, mod).astype(BF16)
        z_scr[rows, :] = _dot(h, wz_ref[...])
        qkvt = _dot_nt(wqkvt_ref[...], h)
        cos = cos_ref[:, rows]
        sin = sin_ref[:, rows]
        for hd in range(N_HEADS):
            hr = slice(hd * HEAD_DIM, (hd + 1) * HEAD_DIM)
            t = _rope_t(_head_rms(qkvt[hr], qw), cos, sin)
            t = (t * (HEAD_DIM ** -0.5 * LOG2E)).astype(BF16)
            for j in range(HALVES):
                q_scr[c * HALVES + j, hd // 2, :, (hd % 2) * BLOCK:(hd % 2 + 1) * BLOCK] = (
                    t[:, j * BLOCK:(j + 1) * BLOCK])
        knt = jnp.concatenate(
            [_rope_t(_head_rms(qkvt[BRANCH + g * HEAD_DIM:BRANCH + (g + 1) * HEAD_DIM], kw), cos, sin)
             for g in range(N_KV)], axis=0)
        k_scr[BLOCK + c * Q_BLOCK:BLOCK + (c + 1) * Q_BLOCK, :] = knt.T.astype(BF16)
        vt = qkvt[BRANCH + KV_W:].astype(BF16)
        for j in range(Q_BLOCK // BLOCK):
            vt_scr[1 + c * (Q_BLOCK // BLOCK) + j] = vt[:, j * BLOCK:(j + 1) * BLOCK]

    ckb = ck_ref[0].T.astype(BF16)
    cvt = cv_ref[0].astype(BF16)

    win_len = 3 * BLOCK
    n_pairs = N_HEADS // 2
    kj = lax.broadcasted_iota(jnp.int32, (win_len, 2 * BLOCK), 0)
    lane = lax.broadcasted_iota(jnp.int32, (win_len, 2 * BLOCK), 1)
    rel = kj - BLOCK - lane % BLOCK
    band = (rel >= -WINDOW) & (rel <= WINDOW)
    first_head = lax.broadcasted_iota(jnp.int32, (1, 2 * BLOCK), 1) < BLOCK

    def half_operands(hb):
        r0 = pl.multiple_of(hb * BLOCK, BLOCK)
        kpos = kj + (r0 - BLOCK)
        valid = jnp.where(band & (kpos >= 0) & (kpos < seq), 0.0, NEG_INF)
        kwin = k_scr[pl.ds(r0, win_len), :]
        vall = jnp.concatenate([vt_scr[hb + j] for j in range(win_len // BLOCK)] + [cvt], axis=1)
        valls = [_with_ones_rows(vall[g * HEAD_DIM:(g + 1) * HEAD_DIM]) for g in range(N_KV)]
        return hb, valid, kwin, valls

    def attend(n, carry):
        halves = [half_operands(n * HALVES + j) for j in range(HALVES)]

        def scores_fn(idx):
            (hb, valid, kwin, _), pair = halves[idx // n_pairs], idx % n_pairs
            return _head_scores(q_scr[hb, pair], pair // (GQA // 2), [kwin, ckb], [valid, None])

        def finish_fn(idx, sc):
            j, pair = idx // n_pairs, idx % n_pairs
            g = pair // (GQA // 2)
            sink2 = jnp.where(first_head, sink_ref[2 * pair], sink_ref[2 * pair + 1]) * LOG2E
            o = _head_softmax_pv(sc, halves[j][3][g], sink2)
            for i in range(2):
                hd = 2 * pair + i
                ot_scr[n, hd * HEAD_DIM:(hd + 1) * HEAD_DIM, j * BLOCK:(j + 1) * BLOCK] = (
                    o[:, i * BLOCK:(i + 1) * BLOCK])

        _attend_heads(scores_fn, finish_fn, HALVES * n_pairs)
        return carry

    for c in range(n_blocks):
        project(c)
    lax.fori_loop(0, n_blocks, attend, 0)
    for c in range(n_blocks):
        rows = slice(c * Q_BLOCK, (c + 1) * Q_BLOCK)
        o_ref[0, rows, :] = _gate_out(x_ref[0, rows, :], ot_scr[c].T, z_scr[rows, :], gate, wout_ref[...])


def _rope_tables_t(seq):
    pos = np.arange(seq)
    n_freq = HEAD_DIM // 4
    inv = ROPE_THETA ** (-np.arange(n_freq, dtype=np.float64) / n_freq)
    ang = np.concatenate([(pos // GRID_W)[:, None] * inv, (pos % GRID_W)[:, None] * inv], axis=-1)
    return np.cos(ang).T.astype(np.float32), np.sin(ang).T.astype(np.float32)


def _attn_lat_layer(x, mod, nw, wqkvt, wz, wout, qw, kw, sink, ckt, cvt):
    nb, seq, _ = x.shape
    past = ckt.shape[2]
    cos, sin = (jnp.asarray(t) for t in _rope_tables_t(seq))
    kern = functools.partial(_attn_lat_kernel, seq=seq)
    n_blocks = seq // Q_BLOCK
    return pl.pallas_call(
        kern,
        grid=(nb,),
        in_specs=[
            pl.BlockSpec(memory_space=pltpu.SMEM),
            pl.BlockSpec((1, seq, D_MODEL), lambda b: (b, 0, 0)),
            MOD_SPEC,
            _const_spec((1, D_MODEL)),
            _const_spec((BRANCH + 2 * KV_W, D_MODEL)),
            _const_spec((D_MODEL, BRANCH)),
            _const_spec((BRANCH, D_MODEL)),
            _const_spec((1, HEAD_DIM)),
            _const_spec((1, HEAD_DIM)),
            _const_spec((HEAD_DIM // 2, seq)),
            _const_spec((HEAD_DIM // 2, seq)),
            pl.BlockSpec((1, KV_W, past), lambda b: (b, 0, 0)),
            pl.BlockSpec((1, KV_W, past), lambda b: (b, 0, 0)),
        ],
        out_specs=pl.BlockSpec((1, seq, D_MODEL), lambda b: (b, 0, 0)),
        out_shape=jax.ShapeDtypeStruct(x.shape, F32),
        scratch_shapes=[
            pltpu.VMEM((seq // BLOCK, N_HEADS // 2, HEAD_DIM, 2 * BLOCK), BF16),
            pltpu.VMEM((seq, BRANCH), F32),
            pltpu.VMEM((n_blocks, BRANCH, Q_BLOCK), F32),
            pltpu.VMEM((seq + 2 * BLOCK, KV_W), BF16),
            pltpu.VMEM((seq // BLOCK + 2, KV_W, BLOCK), BF16),
        ],
        compiler_params=pltpu.CompilerParams(
            dimension_semantics=("arbitrary",), vmem_limit_bytes=VMEM_LIMIT),
        name="attn_latent_layer",
    )(sink, x, mod, nw, wqkvt, wz, wout, qw, kw, cos, sin, ckt, cvt)


def kernel(x_prompt, x_sample, cache_k_l1, cache_v_l1, c, c_ctx, norm_w_l0, w_mod_l0, b_mod_l0,
           w_in_l0, w_out_l0, norm_w_l1, w_mod_l1, b_mod_l1, w_in_l1, q_norm_w_l1, k_norm_w_l1,
           sink_l1, w_out_l1):
    nb_ctx, seq_ctx, _ = x_prompt.shape
    nb_lat = x_sample.shape[0]
    past = cache_k_l1.shape[1]
    assert 1 + nb_lat <= MOD_ROWS
    mod0, win0, wout0 = _layer0_prep(c_ctx, c, w_mod_l0, b_mod_l0, w_in_l0, w_out_l0)

    nw0 = norm_w_l0.reshape(1, D_MODEL)
    nw1 = norm_w_l1.reshape(1, D_MODEL)
    qw = q_norm_w_l1.reshape(1, HEAD_DIM)
    kw = k_norm_w_l1.reshape(1, HEAD_DIM)

    xp, mod1, wqkvt1, wz1, wout1 = _fourier_layer(
        x_prompt, mod0, False, nw0, win0, wout0,
        next_layer=(c_ctx, c, w_mod_l1, b_mod_l1, w_in_l1, w_out_l1))
    xs = _fourier_layer(x_sample, mod0, True, nw0, win0, wout0)

    def to_feature_major(t):
        return jnp.transpose(t, (0, 2, 3, 1)).reshape(t.shape[0], KV_W, t.shape[1])

    def from_feature_major(t):
        return jnp.transpose(t.reshape(t.shape[0], N_KV, HEAD_DIM, t.shape[2]), (0, 3, 1, 2))

    xp, new_kt, new_vt = _attn_ctx_layer(xp, mod1, nw1, wqkvt1, wz1, wout1, qw, kw, sink_l1)
    xs = _attn_lat_layer(xs, mod1, nw1, wqkvt1, wz1, wout1, qw, kw, sink_l1,
                         to_feature_major(cache_k_l1), to_feature_major(cache_v_l1))
    return (xp, xs, from_feature_major(new_kt), from_feature_major(new_vt))
```

```python
import functools

import numpy as np
import jax
import jax.numpy as jnp
from jax import lax
from jax.experimental import pallas as pl
from jax.experimental.pallas import tpu as pltpu

D_MODEL = 1024
BRANCH = 1024
N_GROUPS = 4
GROUP_W = BRANCH // N_GROUPS
HALF_W = GROUP_W // 2
HEAD_DIM = 64
N_HEADS = 16
N_KV = 4
GQA = N_HEADS // N_KV
KV_W = N_KV * HEAD_DIM
GRID_W = 64
WINDOW = 128
BLOCK = 128
ROPE_THETA = 10000.0
EPS = 1e-6
NEG_INF = -1e30
LANES = 128
ROW_CHUNK = 256
Q_BLOCK = 256
HALVES = Q_BLOCK // BLOCK
CTX_SEQS_PER_STEP = 4
VMEM_LIMIT = 56 * 1024 * 1024
MOD_ROWS = 8
MOD_K_CHUNK = 256
LOG2E = float(np.log2(np.e))

F32 = jnp.float32
BF16 = jnp.bfloat16


def _dot(a, b):
    return jnp.dot(a, b, preferred_element_type=F32)


def _dot_nt(a, b):
    return lax.dot_general(a, b, (((1,), (1,)), ((), ())), preferred_element_type=F32)


def _mod_row(mod_ref, per_request):
    if per_request:
        return mod_ref[pl.ds(1 + pl.program_id(0), 1), :]
    return mod_ref[0:1, :]


MOD_SPEC = pl.BlockSpec((MOD_ROWS, 3 * D_MODEL), lambda b: (0, 0))


def _mod_norm(x, nw, mod):
    shift = mod[:, :D_MODEL]
    scale = mod[:, D_MODEL:2 * D_MODEL]
    y = x * lax.rsqrt(jnp.mean(x * x, axis=-1, keepdims=True) + EPS)
    return (y * nw) * (1.0 + scale) + shift


def _mod_accumulate(cctx_ref, c_ref, w_ref, b_ref, o_ref, cond_scr):
    n_lat = c_ref.shape[0]
    cond_scr[...] = jnp.zeros_like(cond_scr)
    cond_scr[0:1, :] = cctx_ref[...]
    cond_scr[1:1 + n_lat, :] = c_ref[...]
    s = jax.nn.silu(cond_scr[...]).astype(BF16)

    @pl.when(pl.program_id(0) == 0)
    def _():
        o_ref[...] = jnp.broadcast_to(b_ref[...], o_ref.shape)

    o_ref[...] += _dot(s, w_ref[...].astype(BF16))


def _mod_in_specs(n_lat, rows):
    return [pl.BlockSpec((1, rows), lambda k: (0, k)),
            pl.BlockSpec((n_lat, rows), lambda k: (0, k)),
            pl.BlockSpec((rows, 3 * D_MODEL), lambda k: (k, 0)),
            pl.BlockSpec((1, 3 * D_MODEL), lambda k: (0, 0))]


def _mirror_perm():
    j = np.arange(GROUP_W)
    return np.where(j <= HALF_W, j, GROUP_W + HALF_W - j)


def _prep_kernel(cctx_ref, c_ref, wmod_ref, bmod_ref, win_ref, wout_ref, pm_ref,
                 mod_ref, winb_ref, woutb_ref, cond_scr):
    _mod_accumulate(cctx_ref, c_ref, wmod_ref, bmod_ref, mod_ref, cond_scr)
    w = win_ref[...].astype(BF16)
    winb_ref[:, :BRANCH] = w[:, :BRANCH]
    pm = pm_ref[...]
    for g in range(N_GROUPS):
        cols = slice(BRANCH + g * GROUP_W, BRANCH + (g + 1) * GROUP_W)
        winb_ref[:, cols] = _dot(w[:, cols], pm).astype(BF16)
    woutb_ref[...] = _dot(pm, wout_ref[...].astype(BF16)).astype(BF16)


def _layer0_prep(c_ctx, c, w_mod, b_mod, w_in, w_out):
    n_lat = c.shape[0]
    rows = MOD_K_CHUNK
    assert rows == GROUP_W
    perm = _mirror_perm()
    pm = jnp.asarray((np.arange(GROUP_W)[:, None] == perm[None, :]).astype(np.float32)).astype(BF16)
    return pl.pallas_call(
        _prep_kernel,
        grid=(D_MODEL // rows,),
        in_specs=_mod_in_specs(n_lat, rows) + [
            pl.BlockSpec((rows, w_in.shape[1]), lambda k: (k, 0)),
            pl.BlockSpec((rows, w_out.shape[1]), lambda k: (k, 0)),
            pl.BlockSpec((GROUP_W, GROUP_W), lambda k: (0, 0)),
        ],
        out_specs=[
            pl.BlockSpec((MOD_ROWS, 3 * D_MODEL), lambda k: (0, 0)),
            pl.BlockSpec((rows, w_in.shape[1]), lambda k: (k, 0)),
            pl.BlockSpec((rows, w_out.shape[1]), lambda k: (k, 0)),
        ],
        out_shape=[
            jax.ShapeDtypeStruct((MOD_ROWS, 3 * D_MODEL), F32),
            jax.ShapeDtypeStruct(w_in.shape, BF16),
            jax.ShapeDtypeStruct(w_out.shape, BF16),
        ],
        scratch_shapes=[pltpu.VMEM((MOD_ROWS, rows), F32)],
        compiler_params=pltpu.CompilerParams(dimension_semantics=("arbitrary",)),
        name="layer0_prep",
    )(c_ctx.reshape(1, D_MODEL), c, w_mod, b_mod.reshape(1, 3 * D_MODEL), w_in, w_out, pm)


def _fourier_prep_kernel(x_ref, mod_ref, nw_ref, win_ref, wout_ref, m1_ref, cs_ref, ss_ref,
                         cctx_ref, c_ref, wmod1_ref, bmod1_ref, win1_ref, wout1_ref,
                         o_ref, mod1_ref, wqkvt1_ref, wz1_ref, wout1b_ref,
                         ta_scr, tb_scr, tr_scr, z_scr, cond_scr, *, seq):
    _mod_accumulate(cctx_ref, c_ref, wmod1_ref, bmod1_ref, mod1_ref, cond_scr)
    w = win1_ref[...]
    wqkvt1_ref[...] = w[:, :BRANCH + 2 * KV_W].T.astype(BF16)
    wz1_ref[...] = w[:, BRANCH + 2 * KV_W:].astype(BF16)
    wout1b_ref[...] = wout1_ref[...].astype(BF16)
    _fourier_kernel(x_ref, mod_ref, nw_ref, win_ref, wout_ref, m1_ref, cs_ref, ss_ref, o_ref,
                    ta_scr, tb_scr, tr_scr, z_scr, seq=seq, per_request=False)


def _fourier_kernel(x_ref, mod_ref, nw_ref, win_ref, wout_ref, m1_ref, cs_ref, ss_ref, o_ref,
                    ta_scr, tb_scr, tr_scr, z_scr, *, seq, per_request):
    mod = _mod_row(mod_ref, per_request)
    gate = mod[:, 2 * D_MODEL:]
    nw = nw_ref[...]
    n_seq = x_ref.shape[0]
    n_chunks = seq // ROW_CHUNK
    lane = lax.broadcasted_iota(jnp.int32, (ROW_CHUNK, HALF_W), 1)
    for i in range(n_seq):
        for c in range(n_chunks):
            rows = slice(c * ROW_CHUNK, (c + 1) * ROW_CHUNK)
            h = _mod_norm(x_ref[i, rows, :], nw, mod).astype(BF16)
            uz = _dot(h, win_ref[...])
            z_scr[i, rows, :] = uz[:, BRANCH:]
            u = uz[:, :BRANCH].astype(BF16)
            tr = jnp.zeros((ROW_CHUNK, HALF_W), F32)
            for g in range(N_GROUPS):
                t = _dot(u[:, g * GROUP_W:(g + 1) * GROUP_W], m1_ref[...])
                half = slice(g * HALF_W, (g + 1) * HALF_W)
                ta_scr[i, rows, half] = t[:, :HALF_W].astype(BF16)
                tb = t[:, HALF_W:]
                tb_scr[i, rows, half] = tb.astype(BF16)
                tr = jnp.where(lane == g, tb if g == 0 else pltpu.roll(tb, g, axis=1), tr)
            tr_scr[i, rows, :] = tr.astype(BF16)
    for i in range(n_seq):
        for c in range(n_chunks):
            rows = slice(c * ROW_CHUNK, (c + 1) * ROW_CHUNK)
            cs = cs_ref[rows, :]
            p = _dot(cs, ta_scr[i])
            q = _dot(ss_ref[rows, :], tb_scr[i])
            r = _dot(cs, tr_scr[i])
            parts = []
            for g in range(N_GROUPS):
                half = slice(g * HALF_W, (g + 1) * HALF_W)
                pg, qg = p[:, half], q[:, half]
                rg = r if g == 0 else pltpu.roll(r, HALF_W - g, axis=1)
                parts.append(jnp.where(lane == 0, pg, pg - qg))
                parts.append(jnp.where(lane == 0, rg, pg + qg))
            y = jnp.concatenate(parts, axis=1)
            y = (y * jax.nn.silu(z_scr[i, rows, :])).astype(BF16)
            o_ref[i, rows, :] = x_ref[i, rows, :] + gate * _dot(y, wout_ref[...])


def _dft_tables(seq):
    c = np.arange(GROUP_W)[:, None]
    k = np.arange(HALF_W)[None, :]
    cos_lo = np.cos(2.0 * np.pi * ((c * k) % GROUP_W) / GROUP_W)
    sin_lo = np.sin(2.0 * np.pi * ((c * k) % GROUP_W) / GROUP_W)
    sin_lo[:, 0] = np.cos(np.pi * c[:, 0])
    m1 = np.concatenate([cos_lo, sin_lo], axis=1) / np.sqrt(GROUP_W)
    n = np.arange(seq)
    ang = 2.0 * np.pi * ((n[:, None] * n[None, :]) % seq) / seq
    cs = np.cos(ang) / np.sqrt(seq)
    ss = np.sin(ang) / np.sqrt(seq)
    return m1.astype(np.float32), cs.astype(np.float32), ss.astype(np.float32)


def _const_spec(shape):
    return pl.BlockSpec(shape, lambda b: (0,) * len(shape))


def _fourier_layer(x, mod, per_request, nw, win, wout, next_layer=None):
    nb, seq, _ = x.shape
    n_seq = 1 if per_request else CTX_SEQS_PER_STEP
    assert nb % n_seq == 0
    n_steps = nb // n_seq
    m1, cs, ss = (jnp.asarray(t).astype(BF16) for t in _dft_tables(seq))
    in_specs = [
        pl.BlockSpec((n_seq, seq, D_MODEL), lambda b: (b, 0, 0)),
        MOD_SPEC,
        _const_spec((1, D_MODEL)),
        _const_spec((D_MODEL, 2 * BRANCH)),
        _const_spec((BRANCH, D_MODEL)),
        _const_spec((GROUP_W, GROUP_W)),
        _const_spec((seq, seq)),
        _const_spec((seq, seq)),
    ]
    out_specs = [pl.BlockSpec((n_seq, seq, D_MODEL), lambda b: (b, 0, 0))]
    out_shape = [jax.ShapeDtypeStruct(x.shape, F32)]
    scratch_shapes = [
        pltpu.VMEM((n_seq, seq, N_GROUPS * HALF_W), BF16),
        pltpu.VMEM((n_seq, seq, N_GROUPS * HALF_W), BF16),
        pltpu.VMEM((n_seq, seq, HALF_W), BF16),
        pltpu.VMEM((n_seq, seq, BRANCH), F32),
    ]
    args = [x, mod, nw, win, wout, m1, cs, ss]
    if next_layer is None:
        kern = functools.partial(_fourier_kernel, seq=seq, per_request=per_request)
    else:
        assert not per_request and D_MODEL % (n_steps * LANES) == 0
        c_ctx, c, w_mod1, b_mod1, w_in1, w_out1 = next_layer
        rows = D_MODEL // n_steps
        n_qkvz = 2 * BRANCH + 2 * KV_W
        kern = functools.partial(_fourier_prep_kernel, seq=seq)
        in_specs += _mod_in_specs(c.shape[0], rows) + [
            pl.BlockSpec((rows, n_qkvz), lambda b: (b, 0)),
            pl.BlockSpec((rows, D_MODEL), lambda b: (b, 0))]
        out_specs += [MOD_SPEC,
                      pl.BlockSpec((BRANCH + 2 * KV_W, rows), lambda b: (0, b)),
                      pl.BlockSpec((rows, BRANCH), lambda b: (b, 0)),
                      pl.BlockSpec((rows, D_MODEL), lambda b: (b, 0))]
        out_shape += [jax.ShapeDtypeStruct((MOD_ROWS, 3 * D_MODEL), F32),
                      jax.ShapeDtypeStruct((BRANCH + 2 * KV_W, D_MODEL), BF16),
                      jax.ShapeDtypeStruct((D_MODEL, BRANCH), BF16),
                      jax.ShapeDtypeStruct((BRANCH, D_MODEL), BF16)]
        scratch_shapes += [pltpu.VMEM((MOD_ROWS, rows), F32)]
        args += [c_ctx.reshape(1, D_MODEL), c, w_mod1, b_mod1.reshape(1, 3 * D_MODEL), w_in1, w_out1]
    outs = pl.pallas_call(
        kern,
        grid=(n_steps,),
        in_specs=in_specs,
        out_specs=out_specs,
        out_shape=out_shape,
        scratch_shapes=scratch_shapes,
        compiler_params=pltpu.CompilerParams(
            dimension_semantics=("arbitrary",), vmem_limit_bytes=VMEM_LIMIT),
        name=f"fourier_layer_s{seq}",
    )(*args)
    return outs[0] if next_layer is None else outs


def _head_weight_tile(w_ref, n_tokens):
    row = jnp.broadcast_to(w_ref[...], (HEAD_DIM, HEAD_DIM))
    ii = lax.broadcasted_iota(jnp.int32, (HEAD_DIM, HEAD_DIM), 0)
    jj = lax.broadcasted_iota(jnp.int32, (HEAD_DIM, HEAD_DIM), 1)
    col = jnp.sum(jnp.where(ii == jj, row, 0.0), axis=1, keepdims=True)
    return jnp.broadcast_to(col, (HEAD_DIM, n_tokens))


def _head_rms(t, w):
    return (t * lax.rsqrt(jnp.mean(t * t, axis=0, keepdims=True) + EPS)) * w


def _rope_t(t, cos, sin):
    half = HEAD_DIM // 2
    x1, x2 = t[:half], t[half:]
    return jnp.concatenate([x1 * cos - x2 * sin, x1 * sin + x2 * cos], axis=0)


def _head_scores(qn, g, keys, biases):
    zeros = jnp.zeros_like(qn)
    qz = jnp.concatenate([qn, zeros] if g % 2 == 0 else [zeros, qn], axis=0)
    blk = slice((g // 2) * LANES, (g // 2 + 1) * LANES)
    scores = []
    smax = None
    for k, bias in zip(keys, biases):
        s = _dot(k[:, blk], qz)
        s = s if bias is None else s + bias
        cmax = jnp.max(s, axis=0, keepdims=True)
        smax = cmax if smax is None else jnp.maximum(smax, cmax)
        scores.append(s)
    return scores, smax


ONES_ROWS = 16


def _with_ones_rows(vt):
    return jnp.concatenate([vt, jnp.ones((ONES_ROWS, vt.shape[1]), vt.dtype)], axis=0)


def _head_softmax_pv(scored, values_t, sink2):
    scores, smax = scored
    m = jnp.maximum(smax, sink2)
    p = jnp.concatenate([jnp.exp2(s - m).astype(BF16) for s in scores], axis=0)
    acc = _dot(values_t, p)
    den = acc[HEAD_DIM:HEAD_DIM + 1] + jnp.exp2(sink2 - m)
    return acc[:HEAD_DIM] * (1.0 / den)


def _attend_heads(scores_fn, finish_fn, n_heads=N_HEADS):
    n_groups = n_heads // GQA
    pending = [scores_fn(hd) for hd in range(GQA)]
    for g in range(n_groups):
        nxt = [scores_fn(hd) for hd in range((g + 1) * GQA, (g + 2) * GQA)] if g + 1 < n_groups else None
        for i, sc in enumerate(pending):
            finish_fn(g * GQA + i, sc)
        pending = nxt


def _gate_out(x, o, z, gate, wout):
    y = (o * jax.nn.silu(z)).astype(BF16)
    return x + gate * _dot(y, wout)


def _attn_ctx_kernel(sink_ref, x_ref, mod_ref, nw_ref, wqkvt_ref, wz_ref, wout_ref, qw_ref, kw_ref,
                     o_ref, kto_ref, vto_ref, qkvt_scr, z_scr, ot_scr):
    seq = x_ref.shape[1]
    n_seq = x_ref.shape[0]
    mod = _mod_row(mod_ref, False)
    gate = mod[:, 2 * D_MODEL:]
    kw = _head_weight_tile(kw_ref, seq)
    qw = _head_weight_tile(qw_ref, seq)

    def project(i):
        h = _mod_norm(x_ref[i], nw_ref[...], mod).astype(BF16)
        qkvt_scr[i] = _dot_nt(wqkvt_ref[...], h)
        z_scr[i] = _dot(h, wz_ref[...])

    def attend(i):
        knt = jnp.concatenate(
            [_head_rms(qkvt_scr[i, BRANCH + g * HEAD_DIM:BRANCH + (g + 1) * HEAD_DIM, :], kw)
             for g in range(N_KV)], axis=0)
        kto_ref[i] = knt
        vtf = qkvt_scr[i, BRANCH + KV_W:, :]
        vto_ref[i] = vtf
        kb = knt.T.astype(BF16)
        vt = vtf.astype(BF16)
        vts = [_with_ones_rows(vt[g * HEAD_DIM:(g + 1) * HEAD_DIM]) for g in range(N_KV)]

        def scores_fn(hd):
            t = qkvt_scr[i, hd * HEAD_DIM:(hd + 1) * HEAD_DIM, :]
            qn = (_head_rms(t, qw) * (HEAD_DIM ** -0.5 * LOG2E)).astype(BF16)
            return _head_scores(qn, hd // GQA, [kb], [None])

        def finish_fn(hd, sc):
            ot_scr[i, hd * HEAD_DIM:(hd + 1) * HEAD_DIM, :] = _head_softmax_pv(
                sc, vts[hd // GQA], sink_ref[hd] * LOG2E)

        _attend_heads(scores_fn, finish_fn)

    def output(i):
        o_ref[i] = _gate_out(x_ref[i], ot_scr[i].T, z_scr[i], gate, wout_ref[...])

    for phase in (project, attend, output):
        for i in range(n_seq):
            phase(i)


def _attn_ctx_layer(x, mod, nw, wqkvt, wz, wout, qw, kw, sink):
    nb, seq, _ = x.shape
    n_seq = CTX_SEQS_PER_STEP
    assert nb % n_seq == 0
    return pl.pallas_call(
        _attn_ctx_kernel,
        grid=(nb // n_seq,),
        in_specs=[
            pl.BlockSpec(memory_space=pltpu.SMEM),
            pl.BlockSpec((n_seq, seq, D_MODEL), lambda b: (b, 0, 0)),
            MOD_SPEC,
            _const_spec((1, D_MODEL)),
            _const_spec((BRANCH + 2 * KV_W, D_MODEL)),
            _const_spec((D_MODEL, BRANCH)),
            _const_spec((BRANCH, D_MODEL)),
            _const_spec((1, HEAD_DIM)),
            _const_spec((1, HEAD_DIM)),
        ],
        out_specs=[
            pl.BlockSpec((n_seq, seq, D_MODEL), lambda b: (b, 0, 0)),
            pl.BlockSpec((n_seq, KV_W, seq), lambda b: (b, 0, 0)),
            pl.BlockSpec((n_seq, KV_W, seq), lambda b: (b, 0, 0)),
        ],
        out_shape=[
            jax.ShapeDtypeStruct(x.shape, F32),
            jax.ShapeDtypeStruct((nb, KV_W, seq), F32),
            jax.ShapeDtypeStruct((nb, KV_W, seq), F32),
        ],
        scratch_shapes=[
            pltpu.VMEM((n_seq, BRANCH + 2 * KV_W, seq), F32),
            pltpu.VMEM((n_seq, seq, BRANCH), F32),
            pltpu.VMEM((n_seq, BRANCH, seq), F32),
        ],
        compiler_params=pltpu.CompilerParams(
            dimension_semantics=("arbitrary",), vmem_limit_bytes=VMEM_LIMIT),
        name="attn_context_layer",
    )(sink, x, mod, nw, wqkvt, wz, wout, qw, kw)


def _attn_lat_kernel(sink_ref, x_ref, mod_ref, nw_ref, wqkvt_ref, wz_ref, wout_ref, qw_ref, kw_ref,
                     cos_ref, sin_ref, ck_ref, cv_ref, o_ref,
                     q_scr, z_scr, ot_scr, k_scr, vt_scr, *, seq):
    mod = _mod_row(mod_ref, True)
    gate = mod[:, 2 * D_MODEL:]
    nw = nw_ref[...]
    qw = _head_weight_tile(qw_ref, Q_BLOCK)
    kw = _head_weight_tile(kw_ref, Q_BLOCK)
    n_blocks = seq // Q_BLOCK
    kv_blocks = seq // BLOCK
    k_scr[0:BLOCK, :] = jnp.zeros((BLOCK, KV_W), BF16)
    k_scr[BLOCK + seq:2 * BLOCK + seq, :] = jnp.zeros((BLOCK, KV_W), BF16)
    vt_scr[0] = jnp.zeros((KV_W, BLOCK), BF16)
    vt_scr[kv_blocks + 1] = jnp.zeros((KV_W, BLOCK), BF16)

    def project(c):
        rows = slice(c * Q_BLOCK, (c + 1) * Q_BLOCK)
        h = _mod_norm(x_ref[0, rows, :], nw, mod).astype(BF16)
        z_scr[rows, :] = _dot(h, wz_ref[...])
        qkvt = _dot_nt(wqkvt_ref[...], h)
        cos = cos_ref[:, rows]
        sin = sin_ref[:, rows]
        for hd in range(N_HEADS):
            hr = slice(hd * HEAD_DIM, (hd + 1) * HEAD_DIM)
            t = _rope_t(_head_rms(qkvt[hr], qw), cos, sin)
            t = (t * (HEAD_DIM ** -0.5 * LOG2E)).astype(BF16)
            for j in range(HALVES):
                q_scr[c * HALVES + j, hd // 2, :, (hd % 2) * BLOCK:(hd % 2 + 1) * BLOCK] = (
                    t[:, j * BLOCK:(j + 1) * BLOCK])
        knt = jnp.concatenate(
            [_rope_t(_head_rms(qkvt[BRANCH + g * HEAD_DIM:BRANCH + (g + 1) * HEAD_DIM], kw), cos, sin)
             for g in range(N_KV)], axis=0)
        k_scr[BLOCK + c * Q_BLOCK:BLOCK + (c + 1) * Q_BLOCK, :] = knt.T.astype(BF16)
        vt = qkvt[BRANCH + KV_W:].astype(BF16)
        for j in range(Q_BLOCK // BLOCK):
            vt_scr[1 + c * (Q_BLOCK // BLOCK) + j] = vt[:, j * BLOCK:(j + 1) * BLOCK]

    ckb = ck_ref[0].T.astype(BF16)
    cvt = cv_ref[0].astype(BF16)

    win_len = 3 * BLOCK
    n_pairs = N_HEADS // 2
    kj = lax.broadcasted_iota(jnp.int32, (win_len, 2 * BLOCK), 0)
    lane = lax.broadcasted_iota(jnp.int32, (win_len, 2 * BLOCK), 1)
    rel = kj - BLOCK - lane % BLOCK
    band = (rel >= -WINDOW) & (rel <= WINDOW)
    first_head = lax.broadcasted_iota(jnp.int32, (1, 2 * BLOCK), 1) < BLOCK

    def half_operands(hb):
        r0 = pl.multiple_of(hb * BLOCK, BLOCK)
        kpos = kj + (r0 - BLOCK)
        valid = jnp.where(band & (kpos >= 0) & (kpos < seq), 0.0, NEG_INF)
        kwin = k_scr[pl.ds(r0, win_len), :]
        vall = jnp.concatenate([vt_scr[hb + j] for j in range(win_len // BLOCK)] + [cvt], axis=1)
        valls = [_with_ones_rows(vall[g * HEAD_DIM:(g + 1) * HEAD_DIM]) for g in range(N_KV)]
        return hb, valid, kwin, valls

    def attend(n, carry):
        halves = [half_operands(n * HALVES + j) for j in range(HALVES)]

        def scores_fn(idx):
            (hb, valid, kwin, _), pair = halves[idx // n_pairs], idx % n_pairs
            return _head_scores(q_scr[hb, pair], pair // (GQA // 2), [kwin, ckb], [valid, None])

        def finish_fn(idx, sc):
            j, pair = idx // n_pairs, idx % n_pairs
            g = pair // (GQA // 2)
            sink2 = jnp.where(first_head, sink_ref[2 * pair], sink_ref[2 * pair + 1]) * LOG2E
            o = _head_softmax_pv(sc, halves[j][3][g], sink2)
            for i in range(2):
                hd = 2 * pair + i
                ot_scr[n, hd * HEAD_DIM:(hd + 1) * HEAD_DIM, j * BLOCK:(j + 1) * BLOCK] = (
                    o[:, i * BLOCK:(i + 1) * BLOCK])

        _attend_heads(scores_fn, finish_fn, HALVES * n_pairs)
        return carry

    for c in range(n_blocks):
        project(c)
    lax.fori_loop(0, n_blocks, attend, 0)
    for c in range(n_blocks):
        rows = slice(c * Q_BLOCK, (c + 1) * Q_BLOCK)
        o_ref[0, rows, :] = _gate_out(x_ref[0, rows, :], ot_scr[c].T, z_scr[rows, :], gate, wout_ref[...])


def _rope_tables_t(seq):
    pos = np.arange(seq)
    n_freq = HEAD_DIM // 4
    inv = ROPE_THETA ** (-np.arange(n_freq, dtype=np.float64) / n_freq)
    ang = np.concatenate([(pos // GRID_W)[:, None] * inv, (pos % GRID_W)[:, None] * inv], axis=-1)
    return np.cos(ang).T.astype(np.float32), np.sin(ang).T.astype(np.float32)


def _attn_lat_layer(x, mod, nw, wqkvt, wz, wout, qw, kw, sink, ckt, cvt):
    nb, seq, _ = x.shape
    past = ckt.shape[2]
    cos, sin = (jnp.asarray(t) for t in _rope_tables_t(seq))
    kern = functools.partial(_attn_lat_kernel, seq=seq)
    n_blocks = seq // Q_BLOCK
    return pl.pallas_call(
        kern,
        grid=(nb,),
        in_specs=[
            pl.BlockSpec(memory_space=pltpu.SMEM),
            pl.BlockSpec((1, seq, D_MODEL), lambda b: (b, 0, 0)),
            MOD_SPEC,
            _const_spec((1, D_MODEL)),
            _const_spec((BRANCH + 2 * KV_W, D_MODEL)),
            _const_spec((D_MODEL, BRANCH)),
            _const_spec((BRANCH, D_MODEL)),
            _const_spec((1, HEAD_DIM)),
            _const_spec((1, HEAD_DIM)),
            _const_spec((HEAD_DIM // 2, seq)),
            _const_spec((HEAD_DIM // 2, seq)),
            pl.BlockSpec((1, KV_W, past), lambda b: (b, 0, 0)),
            pl.BlockSpec((1, KV_W, past), lambda b: (b, 0, 0)),
        ],
        out_specs=pl.BlockSpec((1, seq, D_MODEL), lambda b: (b, 0, 0)),
        out_shape=jax.ShapeDtypeStruct(x.shape, F32),
        scratch_shapes=[
            pltpu.VMEM((seq // BLOCK, N_HEADS // 2, HEAD_DIM, 2 * BLOCK), BF16),
            pltpu.VMEM((seq, BRANCH), F32),
            pltpu.VMEM((n_blocks, BRANCH, Q_BLOCK), F32),
            pltpu.VMEM((seq + 2 * BLOCK, KV_W), BF16),
            pltpu.VMEM((seq // BLOCK + 2, KV_W, BLOCK), BF16),
        ],
        compiler_params=pltpu.CompilerParams(
            dimension_semantics=("arbitrary",), vmem_limit_bytes=VMEM_LIMIT),
        name="attn_latent_layer",
    )(sink, x, mod, nw, wqkvt, wz, wout, qw, kw, cos, sin, ckt, cvt)


def kernel(x_prompt, x_sample, cache_k_l1, cache_v_l1, c, c_ctx, norm_w_l0, w_mod_l0, b_mod_l0,
           w_in_l0, w_out_l0, norm_w_l1, w_mod_l1, b_mod_l1, w_in_l1, q_norm_w_l1, k_norm_w_l1,
           sink_l1, w_out_l1):
    nb_ctx, seq_ctx, _ = x_prompt.shape
    nb_lat = x_sample.shape[0]
    past = cache_k_l1.shape[1]
    assert 1 + nb_lat <= MOD_ROWS
    mod0, win0, wout0 = _layer0_prep(c_ctx, c, w_mod_l0, b_mod_l0, w_in_l0, w_out_l0)

    nw0 = norm_w_l0.reshape(1, D_MODEL)
    nw1 = norm_w_l1.reshape(1, D_MODEL)
    qw = q_norm_w_l1.reshape(1, HEAD_DIM)
    kw = k_norm_w_l1.reshape(1, HEAD_DIM)

    xp, mod1, wqkvt1, wz1, wout1 = _fourier_layer(
        x_prompt, mod0, False, nw0, win0, wout0,
        next_layer=(c_ctx, c, w_mod_l1, b_mod_l1, w_in_l1, w_out_l1))
    xs = _fourier_layer(x_sample, mod0, True, nw0, win0, wout0)

    def to_feature_major(t):
        return jnp.transpose(t, (0, 2, 3, 1)).reshape(t.shape[0], KV_W, t.shape[1])

    def from_feature_major(t):
        return jnp.transpose(t.reshape(t.shape[0], N_KV, HEAD_DIM, t.shape[2]), (0, 3, 1, 2))

    xp, new_kt, new_vt = _attn_ctx_layer(xp, mod1, nw1, wqkvt1, wz1, wout1, qw, kw, sink_l1)
    xs = _attn_lat_layer(xs, mod1, nw1, wqkvt1, wz1, wout1, qw, kw, sink_l1,
                         to_feature_major(cache_k_l1), to_feature_major(cache_v_l1))
    return (xp, xs, from_feature_major(new_kt), from_feature_major(new_vt))
```

```python
import functools

import numpy as np
import jax
import jax.numpy as jnp
from jax import lax
from jax.experimental import pallas as pl
from jax.experimental.pallas import tpu as pltpu

D_MODEL = 1024
BRANCH = 1024
N_GROUPS = 4
GROUP_W = BRANCH // N_GROUPS
HALF_W = GROUP_W // 2
HEAD_DIM = 64
N_HEADS = 16
N_KV = 4
GQA = N_HEADS // N_KV
KV_W = N_KV * HEAD_DIM
GRID_W = 64
WINDOW = 128
BLOCK = 128
ROPE_THETA = 10000.0
EPS = 1e-6
NEG_INF = -1e30
LANES = 128
ROW_CHUNK = 256
Q_BLOCK = 256
HALVES = Q_BLOCK // BLOCK
CTX_SEQS_PER_STEP = 2
VMEM_LIMIT = 56 * 1024 * 1024
MOD_ROWS = 8
MOD_K_CHUNK = 256
LOG2E = float(np.log2(np.e))

F32 = jnp.float32
BF16 = jnp.bfloat16


def _dot(a, b):
    return jnp.dot(a, b, preferred_element_type=F32)


def _dot_nt(a, b):
    return lax.dot_general(a, b, (((1,), (1,)), ((), ())), preferred_element_type=F32)


def _mod_row(mod_ref, per_request):
    if per_request:
        return mod_ref[pl.ds(1 + pl.program_id(0), 1), :]
    return mod_ref[0:1, :]


MOD_SPEC = pl.BlockSpec((MOD_ROWS, 3 * D_MODEL), lambda b: (0, 0))


def _mod_norm(x, nw, mod):
    shift = mod[:, :D_MODEL]
    scale = mod[:, D_MODEL:2 * D_MODEL]
    y = x * lax.rsqrt(jnp.mean(x * x, axis=-1, keepdims=True) + EPS)
    return (y * nw) * (1.0 + scale) + shift


def _mod_accumulate(cctx_ref, c_ref, w_ref, b_ref, o_ref, cond_scr):
    n_lat = c_ref.shape[0]
    cond_scr[...] = jnp.zeros_like(cond_scr)
    cond_scr[0:1, :] = cctx_ref[...]
    cond_scr[1:1 + n_lat, :] = c_ref[...]
    s = jax.nn.silu(cond_scr[...]).astype(BF16)

    @pl.when(pl.program_id(0) == 0)
    def _():
        o_ref[...] = jnp.broadcast_to(b_ref[...], o_ref.shape)

    o_ref[...] += _dot(s, w_ref[...].astype(BF16))


def _mod_in_specs(n_lat, rows):
    return [pl.BlockSpec((1, rows), lambda k: (0, k)),
            pl.BlockSpec((n_lat, rows), lambda k: (0, k)),
            pl.BlockSpec((rows, 3 * D_MODEL), lambda k: (k, 0)),
            pl.BlockSpec((1, 3 * D_MODEL), lambda k: (0, 0))]


def _mirror_perm():
    j = np.arange(GROUP_W)
    return np.where(j <= HALF_W, j, GROUP_W + HALF_W - j)


def _prep_kernel(cctx_ref, c_ref, wmod_ref, bmod_ref, win_ref, wout_ref, pm_ref,
                 mod_ref, winb_ref, woutb_ref, cond_scr):
    _mod_accumulate(cctx_ref, c_ref, wmod_ref, bmod_ref, mod_ref, cond_scr)
    w = win_ref[...].astype(BF16)
    winb_ref[:, :BRANCH] = w[:, :BRANCH]
    pm = pm_ref[...]
    for g in range(N_GROUPS):
        cols = slice(BRANCH + g * GROUP_W, BRANCH + (g + 1) * GROUP_W)
        winb_ref[:, cols] = _dot(w[:, cols], pm).astype(BF16)
    woutb_ref[...] = _dot(pm, wout_ref[...].astype(BF16)).astype(BF16)


def _layer0_prep(c_ctx, c, w_mod, b_mod, w_in, w_out):
    n_lat = c.shape[0]
    rows = MOD_K_CHUNK
    assert rows == GROUP_W
    perm = _mirror_perm()
    pm = jnp.asarray((np.arange(GROUP_W)[:, None] == perm[None, :]).astype(np.float32)).astype(BF16)
    return pl.pallas_call(
        _prep_kernel,
        grid=(D_MODEL // rows,),
        in_specs=_mod_in_specs(n_lat, rows) + [
            pl.BlockSpec((rows, w_in.shape[1]), lambda k: (k, 0)),
            pl.BlockSpec((rows, w_out.shape[1]), lambda k: (k, 0)),
            pl.BlockSpec((GROUP_W, GROUP_W), lambda k: (0, 0)),
        ],
        out_specs=[
            pl.BlockSpec((MOD_ROWS, 3 * D_MODEL), lambda k: (0, 0)),
            pl.BlockSpec((rows, w_in.shape[1]), lambda k: (k, 0)),
            pl.BlockSpec((rows, w_out.shape[1]), lambda k: (k, 0)),
        ],
        out_shape=[
            jax.ShapeDtypeStruct((MOD_ROWS, 3 * D_MODEL), F32),
            jax.ShapeDtypeStruct(w_in.shape, BF16),
            jax.ShapeDtypeStruct(w_out.shape, BF16),
        ],
        scratch_shapes=[pltpu.VMEM((MOD_ROWS, rows), F32)],
        compiler_params=pltpu.CompilerParams(dimension_semantics=("arbitrary",)),
        name="layer0_prep",
    )(c_ctx.reshape(1, D_MODEL), c, w_mod, b_mod.reshape(1, 3 * D_MODEL), w_in, w_out, pm)


def _fourier_prep_kernel(x_ref, mod_ref, nw_ref, win_ref, wout_ref, m1_ref, cs_ref, ss_ref,
                         cctx_ref, c_ref, wmod1_ref, bmod1_ref, win1_ref, wout1_ref,
                         o_ref, mod1_ref, wqkvt1_ref, wz1_ref, wout1b_ref,
                         ta_scr, tb_scr, tr_scr, z_scr, cond_scr, *, seq):
    _mod_accumulate(cctx_ref, c_ref, wmod1_ref, bmod1_ref, mod1_ref, cond_scr)
    w = win1_ref[...]
    wqkvt1_ref[...] = w[:, :BRANCH + 2 * KV_W].T.astype(BF16)
    wz1_ref[...] = w[:, BRANCH + 2 * KV_W:].astype(BF16)
    wout1b_ref[...] = wout1_ref[...].astype(BF16)
    _fourier_kernel(x_ref, mod_ref, nw_ref, win_ref, wout_ref, m1_ref, cs_ref, ss_ref, o_ref,
                    ta_scr, tb_scr, tr_scr, z_scr, seq=seq, per_request=False)


def _fourier_kernel(x_ref, mod_ref, nw_ref, win_ref, wout_ref, m1_ref, cs_ref, ss_ref, o_ref,
                    ta_scr, tb_scr, tr_scr, z_scr, *, seq, per_request):
    mod = _mod_row(mod_ref, per_request)
    gate = mod[:, 2 * D_MODEL:]
    nw = nw_ref[...]
    n_seq = x_ref.shape[0]
    n_chunks = seq // ROW_CHUNK
    lane = lax.broadcasted_iota(jnp.int32, (ROW_CHUNK, HALF_W), 1)
    for i in range(n_seq):
        for c in range(n_chunks):
            rows = slice(c * ROW_CHUNK, (c + 1) * ROW_CHUNK)
            h = _mod_norm(x_ref[i, rows, :], nw, mod).astype(BF16)
            uz = _dot(h, win_ref[...])
            z_scr[i, rows, :] = uz[:, BRANCH:]
            u = uz[:, :BRANCH].astype(BF16)
            tr = jnp.zeros((ROW_CHUNK, HALF_W), F32)
            for g in range(N_GROUPS):
                t = _dot(u[:, g * GROUP_W:(g + 1) * GROUP_W], m1_ref[...])
                half = slice(g * HALF_W, (g + 1) * HALF_W)
                ta_scr[i, rows, half] = t[:, :HALF_W].astype(BF16)
                tb = t[:, HALF_W:]
                tb_scr[i, rows, half] = tb.astype(BF16)
                tr = jnp.where(lane == g, tb if g == 0 else pltpu.roll(tb, g, axis=1), tr)
            tr_scr[i, rows, :] = tr.astype(BF16)
    for i in range(n_seq):
        for c in range(n_chunks):
            rows = slice(c * ROW_CHUNK, (c + 1) * ROW_CHUNK)
            cs = cs_ref[rows, :]
            p = _dot(cs, ta_scr[i])
            q = _dot(ss_ref[rows, :], tb_scr[i])
            r = _dot(cs, tr_scr[i])
            parts = []
            for g in range(N_GROUPS):
                half = slice(g * HALF_W, (g + 1) * HALF_W)
                pg, qg = p[:, half], q[:, half]
                rg = r if g == 0 else pltpu.roll(r, HALF_W - g, axis=1)
                parts.append(jnp.where(lane == 0, pg, pg - qg))
                parts.append(jnp.where(lane == 0, rg, pg + qg))
            y = jnp.concatenate(parts, axis=1)
            y = (y * jax.nn.silu(z_scr[i, rows, :])).astype(BF16)
            o_ref[i, rows, :] = x_ref[i, rows, :] + gate * _dot(y, wout_ref[...])


def _dft_tables(seq):
    c = np.arange(GROUP_W)[:, None]
    k = np.arange(HALF_W)[None, :]
    cos_lo = np.cos(2.0 * np.pi * ((c * k) % GROUP_W) / GROUP_W)
    sin_lo = np.sin(2.0 * np.pi * ((c * k) % GROUP_W) / GROUP_W)
    sin_lo[:, 0] = np.cos(np.pi * c[:, 0])
    m1 = np.concatenate([cos_lo, sin_lo], axis=1) / np.sqrt(GROUP_W)
    n = np.arange(seq)
    ang = 2.0 * np.pi * ((n[:, None] * n[None, :]) % seq) / seq
    cs = np.cos(ang) / np.sqrt(seq)
    ss = np.sin(ang) / np.sqrt(seq)
    return m1.astype(np.float32), cs.astype(np.float32), ss.astype(np.float32)


def _const_spec(shape):
    return pl.BlockSpec(shape, lambda b: (0,) * len(shape))


def _fourier_layer(x, mod, per_request, nw, win, wout, next_layer=None):
    nb, seq, _ = x.shape
    n_seq = 1 if per_request else CTX_SEQS_PER_STEP
    assert nb % n_seq == 0
    n_steps = nb // n_seq
    m1, cs, ss = (jnp.asarray(t).astype(BF16) for t in _dft_tables(seq))
    in_specs = [
        pl.BlockSpec((n_seq, seq, D_MODEL), lambda b: (b, 0, 0)),
        MOD_SPEC,
        _const_spec((1, D_MODEL)),
        _const_spec((D_MODEL, 2 * BRANCH)),
        _const_spec((BRANCH, D_MODEL)),
        _const_spec((GROUP_W, GROUP_W)),
        _const_spec((seq, seq)),
        _const_spec((seq, seq)),
    ]
    out_specs = [pl.BlockSpec((n_seq, seq, D_MODEL), lambda b: (b, 0, 0))]
    out_shape = [jax.ShapeDtypeStruct(x.shape, F32)]
    scratch_shapes = [
        pltpu.VMEM((n_seq, seq, N_GROUPS * HALF_W), BF16),
        pltpu.VMEM((n_seq, seq, N_GROUPS * HALF_W), BF16),
        pltpu.VMEM((n_seq, seq, HALF_W), BF16),
        pltpu.VMEM((n_seq, seq, BRANCH), F32),
    ]
    args = [x, mod, nw, win, wout, m1, cs, ss]
    if next_layer is None:
        kern = functools.partial(_fourier_kernel, seq=seq, per_request=per_request)
    else:
        assert not per_request and D_MODEL % (n_steps * LANES) == 0
        c_ctx, c, w_mod1, b_mod1, w_in1, w_out1 = next_layer
        rows = D_MODEL // n_steps
        n_qkvz = 2 * BRANCH + 2 * KV_W
        kern = functools.partial(_fourier_prep_kernel, seq=seq)
        in_specs += _mod_in_specs(c.shape[0], rows) + [
            pl.BlockSpec((rows, n_qkvz), lambda b: (b, 0)),
            pl.BlockSpec((rows, D_MODEL), lambda b: (b, 0))]
        out_specs += [MOD_SPEC,
                      pl.BlockSpec((BRANCH + 2 * KV_W, rows), lambda b: (0, b)),
                      pl.BlockSpec((rows, BRANCH), lambda b: (b, 0)),
                      pl.BlockSpec((rows, D_MODEL), lambda b: (b, 0))]
        out_shape += [jax.ShapeDtypeStruct((MOD_ROWS, 3 * D_MODEL), F32),
                      jax.ShapeDtypeStruct((BRANCH + 2 * KV_W, D_MODEL), BF16),
                      jax.ShapeDtypeStruct((D_MODEL, BRANCH), BF16),
                      jax.ShapeDtypeStruct((BRANCH, D_MODEL), BF16)]
        scratch_shapes += [pltpu.VMEM((MOD_ROWS, rows), F32)]
        args += [c_ctx.reshape(1, D_MODEL), c, w_mod1, b_mod1.reshape(1, 3 * D_MODEL), w_in1, w_out1]
    outs = pl.pallas_call(
        kern,
        grid=(n_steps,),
        in_specs=in_specs,
        out_specs=out_specs,
        out_shape=out_shape,
        scratch_shapes=scratch_shapes,
        compiler_params=pltpu.CompilerParams(
            dimension_semantics=("arbitrary",), vmem_limit_bytes=VMEM_LIMIT),
        name=f"fourier_layer_s{seq}",
    )(*args)
    return outs[0] if next_layer is None else outs


def _head_weight_tile(w_ref, n_tokens):
    row = jnp.broadcast_to(w_ref[...], (HEAD_DIM, HEAD_DIM))
    ii = lax.broadcasted_iota(jnp.int32, (HEAD_DIM, HEAD_DIM), 0)
    jj = lax.broadcasted_iota(jnp.int32, (HEAD_DIM, HEAD_DIM), 1)
    col = jnp.sum(jnp.where(ii == jj, row, 0.0), axis=1, keepdims=True)
    return jnp.broadcast_to(col, (HEAD_DIM, n_tokens))


def _head_rms(t, w):
    return (t * lax.rsqrt(jnp.mean(t * t, axis=0, keepdims=True) + EPS)) * w


def _rope_t(t, cos, sin):
    half = HEAD_DIM // 2
    x1, x2 = t[:half], t[half:]
    return jnp.concatenate([x1 * cos - x2 * sin, x1 * sin + x2 * cos], axis=0)


def _head_scores(qn, g, keys, biases):
    zeros = jnp.zeros_like(qn)
    qz = jnp.concatenate([qn, zeros] if g % 2 == 0 else [zeros, qn], axis=0)
    blk = slice((g // 2) * LANES, (g // 2 + 1) * LANES)
    scores = []
    smax = None
    for k, bias in zip(keys, biases):
        s = _dot(k[:, blk], qz)
        s = s if bias is None else s + bias
        cmax = jnp.max(s, axis=0, keepdims=True)
        smax = cmax if smax is None else jnp.maximum(smax, cmax)
        scores.append(s)
    return scores, smax


ONES_ROWS = 16


def _with_ones_rows(vt):
    return jnp.concatenate([vt, jnp.ones((ONES_ROWS, vt.shape[1]), vt.dtype)], axis=0)


def _head_softmax_pv(scored, values_t, sink2):
    return _head_pv(_head_probs(scored, sink2), values_t, sink2)


def _head_probs(scored, sink2):
    scores, smax = scored
    m = jnp.maximum(smax, sink2)
    return jnp.concatenate([jnp.exp2((s - m).astype(BF16)) for s in scores], axis=0), m


def _head_pv(probs, values_t, sink2):
    p, m = probs
    acc = _dot(values_t, p)
    den = acc[HEAD_DIM:HEAD_DIM + 1] + jnp.exp2(sink2 - m)
    return acc[:HEAD_DIM] * (1.0 / den)


def _run_all(fns):
    for fn in fns:
        fn()


def _attend_heads(n_units, stage, scores_fn, probs_fn, pv_fn, fillers=()):
    n_stages = n_units // stage
    pending = [scores_fn(u) for u in range(stage)]
    for g in range(n_stages):
        units = range(g * stage, (g + 1) * stage)
        if g < len(fillers):
            fillers[g]()
        nxt = [scores_fn(u) for u in range((g + 1) * stage, (g + 2) * stage)] if g + 1 < n_stages else None
        probs = [probs_fn(u, sc) for u, sc in zip(units, pending)]
        for u, pr in zip(units, probs):
            pv_fn(u, pr)
        pending = nxt


def _gate_out(x, o, z, gate, wout):
    y = (o * jax.nn.silu(z)).astype(BF16)
    return x + gate * _dot(y, wout)


def _attn_ctx_kernel(sink_ref, x_ref, mod_ref, nw_ref, wqkvt_ref, wz_ref, wout_ref, qw_ref, kw_ref,
                     o_ref, kto_ref, vto_ref, qkvt_scr, z_scr, ot_scr):
    seq = x_ref.shape[1]
    n_seq = x_ref.shape[0]
    mod = _mod_row(mod_ref, False)
    gate = mod[:, 2 * D_MODEL:]
    kw = _head_weight_tile(kw_ref, seq)
    qw = _head_weight_tile(qw_ref, seq)

    def project(i):
        h = _mod_norm(x_ref[i], nw_ref[...], mod).astype(BF16)
        qkvt_scr[i] = _dot_nt(wqkvt_ref[...], h)
        z_scr[i] = _dot(h, wz_ref[...])

    def keys_values(i):
        knt = jnp.concatenate(
            [_head_rms(qkvt_scr[i, BRANCH + g * HEAD_DIM:BRANCH + (g + 1) * HEAD_DIM, :], kw)
             for g in range(N_KV)], axis=0)
        kto_ref[i] = knt
        vtf = qkvt_scr[i, BRANCH + KV_W:, :]
        vto_ref[i] = vtf
        vt = vtf.astype(BF16)
        return (knt.T.astype(BF16),
                [_with_ones_rows(vt[g * HEAD_DIM:(g + 1) * HEAD_DIM]) for g in range(N_KV)])

    def output(i):
        o_ref[i] = _gate_out(x_ref[i], ot_scr[i].T, z_scr[i], gate, wout_ref[...])

    for i in range(n_seq):
        project(i)
    kv = [keys_values(i) for i in range(n_seq)]

    def scores_fn(u):
        i, hd = divmod(u, N_HEADS)
        t = qkvt_scr[i, hd * HEAD_DIM:(hd + 1) * HEAD_DIM, :]
        qn = (_head_rms(t, qw) * (HEAD_DIM ** -0.5 * LOG2E)).astype(BF16)
        return _head_scores(qn, hd // GQA, [kv[i][0]], [None])

    def probs_fn(u, sc):
        return _head_probs(sc, sink_ref[u % N_HEADS] * LOG2E)

    def pv_fn(u, pr):
        i, hd = divmod(u, N_HEADS)
        ot_scr[i, hd * HEAD_DIM:(hd + 1) * HEAD_DIM, :] = _head_pv(
            pr, kv[i][1][hd // GQA], sink_ref[hd] * LOG2E)

    fillers = [lambda: None] + [functools.partial(output, i) for i in range(n_seq - 1)]
    _attend_heads(n_seq * N_HEADS, N_HEADS, scores_fn, probs_fn, pv_fn, fillers)
    output(n_seq - 1)


def _attn_ctx_layer(x, mod, nw, wqkvt, wz, wout, qw, kw, sink):
    nb, seq, _ = x.shape
    n_seq = CTX_SEQS_PER_STEP
    assert nb % n_seq == 0
    return pl.pallas_call(
        _attn_ctx_kernel,
        grid=(nb // n_seq,),
        in_specs=[
            pl.BlockSpec(memory_space=pltpu.SMEM),
            pl.BlockSpec((n_seq, seq, D_MODEL), lambda b: (b, 0, 0)),
            MOD_SPEC,
            _const_spec((1, D_MODEL)),
            _const_spec((BRANCH + 2 * KV_W, D_MODEL)),
            _const_spec((D_MODEL, BRANCH)),
            _const_spec((BRANCH, D_MODEL)),
            _const_spec((1, HEAD_DIM)),
            _const_spec((1, HEAD_DIM)),
        ],
        out_specs=[
            pl.BlockSpec((n_seq, seq, D_MODEL), lambda b: (b, 0, 0)),
            pl.BlockSpec((n_seq, KV_W, seq), lambda b: (b, 0, 0)),
            pl.BlockSpec((n_seq, KV_W, seq), lambda b: (b, 0, 0)),
        ],
        out_shape=[
            jax.ShapeDtypeStruct(x.shape, F32),
            jax.ShapeDtypeStruct((nb, KV_W, seq), F32),
            jax.ShapeDtypeStruct((nb, KV_W, seq), F32),
        ],
        scratch_shapes=[
            pltpu.VMEM((n_seq, BRANCH + 2 * KV_W, seq), F32),
            pltpu.VMEM((n_seq, seq, BRANCH), F32),
            pltpu.VMEM((n_seq, BRANCH, seq), F32),
        ],
        compiler_params=pltpu.CompilerParams(
            dimension_semantics=("arbitrary",), vmem_limit_bytes=VMEM_LIMIT),
        name="attn_context_layer",
    )(sink, x, mod, nw, wqkvt, wz, wout, qw, kw)


def _attn_lat_kernel(sink_ref, x_ref, mod_ref, nw_ref, wqkvt_ref, wz_ref, wout_ref, qw_ref, kw_ref,
                     cos_ref, sin_ref, ck_ref, cv_ref, o_ref,
                     q_scr, z_scr, ot_scr, k_scr, vt_scr, *, seq):
    mod = _mod_row(mod_ref, True)
    gate = mod[:, 2 * D_MODEL:]
    nw = nw_ref[...]
    qw = _head_weight_tile(qw_ref, Q_BLOCK)
    kw = _head_weight_tile(kw_ref, Q_BLOCK)
    n_blocks = seq // Q_BLOCK
    kv_blocks = seq // BLOCK
    k_scr[0:BLOCK, :] = jnp.zeros((BLOCK, KV_W), BF16)
    k_scr[BLOCK + seq:2 * BLOCK + seq, :] = jnp.zeros((BLOCK, KV_W), BF16)
    vt_scr[0] = jnp.zeros((KV_W, BLOCK), BF16)
    vt_scr[kv_blocks + 1] = jnp.zeros((KV_W, BLOCK), BF16)

    def project_pieces(c):
        rows = slice(c * Q_BLOCK, (c + 1) * Q_BLOCK)
        cos = cos_ref[:, rows]
        sin = sin_ref[:, rows]
        state = {}

        def piece(j):
            if j == 0:
                state["h"] = _mod_norm(x_ref[0, rows, :], nw, mod).astype(BF16)
            h = state["h"]
            zc = slice(j * (BRANCH // N_KV), (j + 1) * (BRANCH // N_KV))
            z_scr[rows, zc] = _dot(h, wz_ref[:, zc])
            qt = _dot_nt(wqkvt_ref[j * GQA * HEAD_DIM:(j + 1) * GQA * HEAD_DIM, :], h)
            for i in range(GQA):
                hd = j * GQA + i
                t = _rope_t(_head_rms(qt[i * HEAD_DIM:(i + 1) * HEAD_DIM], qw), cos, sin)
                t = (t * (HEAD_DIM ** -0.5 * LOG2E)).astype(BF16)
                for half in range(HALVES):
                    q_scr[c * HALVES + half, hd // 2, :, (hd % 2) * BLOCK:(hd % 2 + 1) * BLOCK] = (
                        t[:, half * BLOCK:(half + 1) * BLOCK])
            if j == N_KV - 1:
                kvt = _dot_nt(wqkvt_ref[BRANCH:, :], h)
                knt = jnp.concatenate(
                    [_rope_t(_head_rms(kvt[g * HEAD_DIM:(g + 1) * HEAD_DIM], kw), cos, sin)
                     for g in range(N_KV)], axis=0)
                k_scr[BLOCK + c * Q_BLOCK:BLOCK + (c + 1) * Q_BLOCK, :] = knt.T.astype(BF16)
                vt = kvt[KV_W:].astype(BF16)
                for half in range(HALVES):
                    vt_scr[1 + c * HALVES + half] = vt[:, half * BLOCK:(half + 1) * BLOCK]

        return [functools.partial(piece, j) for j in range(N_KV)]

    def output_pieces(c):
        rows = slice(c * Q_BLOCK, (c + 1) * Q_BLOCK)
        state = {}

        def piece(j):
            if j == 0:
                state["y"] = (ot_scr[c].T * jax.nn.silu(z_scr[rows, :])).astype(BF16)
            cols = slice(j * (D_MODEL // N_KV), (j + 1) * (D_MODEL // N_KV))
            o_ref[0, rows, cols] = x_ref[0, rows, cols] + gate[:, cols] * _dot(state["y"], wout_ref[:, cols])

        return [functools.partial(piece, j) for j in range(N_KV)]

    ckb = ck_ref[0].T.astype(BF16)
    cvt = cv_ref[0].astype(BF16)

    win_len = 3 * BLOCK
    n_pairs = N_HEADS // 2
    kj = lax.broadcasted_iota(jnp.int32, (win_len, 2 * BLOCK), 0)
    lane = lax.broadcasted_iota(jnp.int32, (win_len, 2 * BLOCK), 1)
    rel = kj - BLOCK - lane % BLOCK
    band = (rel >= -WINDOW) & (rel <= WINDOW)
    first_head = lax.broadcasted_iota(jnp.int32, (1, 2 * BLOCK), 1) < BLOCK

    def half_operands(hb):
        r0 = hb * BLOCK
        kpos = kj + (r0 - BLOCK)
        valid = jnp.where(band & (kpos >= 0) & (kpos < seq), 0.0, NEG_INF)
        kwin = k_scr[r0:r0 + win_len, :]
        vall = jnp.concatenate([vt_scr[hb + j] for j in range(win_len // BLOCK)] + [cvt], axis=1)
        valls = [_with_ones_rows(vall[g * HEAD_DIM:(g + 1) * HEAD_DIM]) for g in range(N_KV)]
        return hb, valid, kwin, valls

    operands = {}

    def half(hb):
        if hb not in operands:
            operands[hb] = half_operands(hb)
        return operands[hb]

    def pair_sink2(pair):
        return jnp.where(first_head, sink_ref[2 * pair], sink_ref[2 * pair + 1]) * LOG2E

    def scores_fn(u):
        hb, pair = divmod(u, n_pairs)
        _, valid, kwin, _ = half(hb)
        return _head_scores(q_scr[hb, pair], pair // (GQA // 2), [kwin, ckb], [valid, None])

    def probs_fn(u, sc):
        return _head_probs(sc, pair_sink2(u % n_pairs))

    def pv_fn(u, pr):
        hb, pair = divmod(u, n_pairs)
        n, j = divmod(hb, HALVES)
        o = _head_pv(pr, half(hb)[3][pair // (GQA // 2)], pair_sink2(pair))
        for i in range(2):
            hd = 2 * pair + i
            ot_scr[n, hd * HEAD_DIM:(hd + 1) * HEAD_DIM, j * BLOCK:(j + 1) * BLOCK] = (
                o[:, i * BLOCK:(i + 1) * BLOCK])

    for c in range(min(2, n_blocks)):
        for piece in project_pieces(c):
            piece()
    fillers = []
    for n in range(n_blocks):
        groups = ([project_pieces(n + 2)] if n + 2 < n_blocks else []) + ([output_pieces(n - 1)] if n else [])
        per_stage = N_KV // HALVES
        for j in range(HALVES):
            fillers.append(functools.partial(
                _run_all, [grp[k] for grp in groups for k in range(j * per_stage, (j + 1) * per_stage)]))
    _attend_heads(n_blocks * HALVES * n_pairs, n_pairs, scores_fn, probs_fn, pv_fn, fillers)
    for piece in output_pieces(n_blocks - 1):
        piece()


def _rope_tables_t(seq):
    pos = np.arange(seq)
    n_freq = HEAD_DIM // 4
    inv = ROPE_THETA ** (-np.arange(n_freq, dtype=np.float64) / n_freq)
    ang = np.concatenate([(pos // GRID_W)[:, None] * inv, (pos % GRID_W)[:, None] * inv], axis=-1)
    return np.cos(ang).T.astype(np.float32), np.sin(ang).T.astype(np.float32)


def _attn_lat_layer(x, mod, nw, wqkvt, wz, wout, qw, kw, sink, ckt, cvt):
    nb, seq, _ = x.shape
    past = ckt.shape[2]
    cos, sin = (jnp.asarray(t) for t in _rope_tables_t(seq))
    kern = functools.partial(_attn_lat_kernel, seq=seq)
    n_blocks = seq // Q_BLOCK
    return pl.pallas_call(
        kern,
        grid=(nb,),
        in_specs=[
            pl.BlockSpec(memory_space=pltpu.SMEM),
            pl.BlockSpec((1, seq, D_MODEL), lambda b: (b, 0, 0)),
            MOD_SPEC,
            _const_spec((1, D_MODEL)),
            _const_spec((BRANCH + 2 * KV_W, D_MODEL)),
            _const_spec((D_MODEL, BRANCH)),
            _const_spec((BRANCH, D_MODEL)),
            _const_spec((1, HEAD_DIM)),
            _const_spec((1, HEAD_DIM)),
            _const_spec((HEAD_DIM // 2, seq)),
            _const_spec((HEAD_DIM // 2, seq)),
            pl.BlockSpec((1, KV_W, past), lambda b: (b, 0, 0)),
            pl.BlockSpec((1, KV_W, past), lambda b: (b, 0, 0)),
        ],
        out_specs=pl.BlockSpec((1, seq, D_MODEL), lambda b: (b, 0, 0)),
        out_shape=jax.ShapeDtypeStruct(x.shape, F32),
        scratch_shapes=[
            pltpu.VMEM((seq // BLOCK, N_HEADS // 2, HEAD_DIM, 2 * BLOCK), BF16),
            pltpu.VMEM((seq, BRANCH), F32),
            pltpu.VMEM((n_blocks, BRANCH, Q_BLOCK), F32),
            pltpu.VMEM((seq + 2 * BLOCK, KV_W), BF16),
            pltpu.VMEM((seq // BLOCK + 2, KV_W, BLOCK), BF16),
        ],
        compiler_params=pltpu.CompilerParams(
            dimension_semantics=("arbitrary",), vmem_limit_bytes=VMEM_LIMIT),
        name="attn_latent_layer",
    )(sink, x, mod, nw, wqkvt, wz, wout, qw, kw, cos, sin, ckt, cvt)


def kernel(x_prompt, x_sample, cache_k_l1, cache_v_l1, c, c_ctx, norm_w_l0, w_mod_l0, b_mod_l0,
           w_in_l0, w_out_l0, norm_w_l1, w_mod_l1, b_mod_l1, w_in_l1, q_norm_w_l1, k_norm_w_l1,
           sink_l1, w_out_l1):
    nb_ctx, seq_ctx, _ = x_prompt.shape
    nb_lat = x_sample.shape[0]
    past = cache_k_l1.shape[1]
    assert 1 + nb_lat <= MOD_ROWS
    mod0, win0, wout0 = _layer0_prep(c_ctx, c, w_mod_l0, b_mod_l0, w_in_l0, w_out_l0)

    nw0 = norm_w_l0.reshape(1, D_MODEL)
    nw1 = norm_w_l1.reshape(1, D_MODEL)
    qw = q_norm_w_l1.reshape(1, HEAD_DIM)
    kw = k_norm_w_l1.reshape(1, HEAD_DIM)

    xp, mod1, wqkvt1, wz1, wout1 = _fourier_layer(
        x_prompt, mod0, False, nw0, win0, wout0,
        next_layer=(c_ctx, c, w_mod_l1, b_mod_l1, w_in_l1, w_out_l1))
    xs = _fourier_layer(x_sample, mod0, True, nw0, win0, wout0)

    def to_feature_major(t):
        return jnp.transpose(t, (0, 2, 3, 1)).reshape(t.shape[0], KV_W, t.shape[1])

    def from_feature_major(t):
        return jnp.transpose(t.reshape(t.shape[0], N_KV, HEAD_DIM, t.shape[2]), (0, 3, 1, 2))

    xp, new_kt, new_vt = _attn_ctx_layer(xp, mod1, nw1, wqkvt1, wz1, wout1, qw, kw, sink_l1)
    xs = _attn_lat_layer(xs, mod1, nw1, wqkvt1, wz1, wout1, qw, kw, sink_l1,
                         to_feature_major(cache_k_l1), to_feature_major(cache_v_l1))
    return (xp, xs, from_feature_major(new_kt), from_feature_major(new_vt))
```

```python
import functools

import numpy as np
import jax
import jax.numpy as jnp
from jax import lax
from jax.experimental import pallas as pl
from jax.experimental.pallas import tpu as pltpu

D_MODEL = 1024
BRANCH = 1024
N_GROUPS = 4
GROUP_W = BRANCH // N_GROUPS
HALF_W = GROUP_W // 2
HEAD_DIM = 64
N_HEADS = 16
N_KV = 4
GQA = N_HEADS // N_KV
KV_W = N_KV * HEAD_DIM
GRID_W = 64
WINDOW = 128
BLOCK = 128
ROPE_THETA = 10000.0
EPS = 1e-6
NEG_INF = -1e30
LANES = 128
ROW_CHUNK = 256
Q_BLOCK = 256
HALVES = Q_BLOCK // BLOCK
CTX_SEQS_PER_STEP = 2
VMEM_LIMIT = 56 * 1024 * 1024
MOD_ROWS = 8
MOD_K_CHUNK = 256
LOG2E = float(np.log2(np.e))

F32 = jnp.float32
BF16 = jnp.bfloat16


def _dot(a, b):
    return jnp.dot(a, b, preferred_element_type=F32)


def _dot_nt(a, b):
    return lax.dot_general(a, b, (((1,), (1,)), ((), ())), preferred_element_type=F32)


def _mod_row(mod_ref, per_request):
    if per_request:
        return mod_ref[pl.ds(1 + pl.program_id(0), 1), :]
    return mod_ref[0:1, :]


MOD_SPEC = pl.BlockSpec((MOD_ROWS, 3 * D_MODEL), lambda b: (0, 0))


def _mod_norm(x, nw, mod):
    shift = mod[:, :D_MODEL]
    scale = mod[:, D_MODEL:2 * D_MODEL]
    y = x * lax.rsqrt(jnp.mean(x * x, axis=-1, keepdims=True) + EPS)
    return (y * nw) * (1.0 + scale) + shift


def _mod_accumulate(cctx_ref, c_ref, w_ref, b_ref, o_ref, cond_scr):
    n_lat = c_ref.shape[0]
    cond_scr[...] = jnp.zeros_like(cond_scr)
    cond_scr[0:1, :] = cctx_ref[...]
    cond_scr[1:1 + n_lat, :] = c_ref[...]
    s = jax.nn.silu(cond_scr[...]).astype(BF16)

    @pl.when(pl.program_id(0) == 0)
    def _():
        o_ref[...] = jnp.broadcast_to(b_ref[...], o_ref.shape)

    o_ref[...] += _dot(s, w_ref[...].astype(BF16))


def _mod_in_specs(n_lat, rows):
    return [pl.BlockSpec((1, rows), lambda k: (0, k)),
            pl.BlockSpec((n_lat, rows), lambda k: (0, k)),
            pl.BlockSpec((rows, 3 * D_MODEL), lambda k: (k, 0)),
            pl.BlockSpec((1, 3 * D_MODEL), lambda k: (0, 0))]


def _mirror_perm():
    j = np.arange(GROUP_W)
    return np.where(j <= HALF_W, j, GROUP_W + HALF_W - j)


def _prep_kernel(cctx_ref, c_ref, wmod_ref, bmod_ref, win_ref, wout_ref, pm_ref,
                 mod_ref, winb_ref, woutb_ref, cond_scr):
    _mod_accumulate(cctx_ref, c_ref, wmod_ref, bmod_ref, mod_ref, cond_scr)
    w = win_ref[...].astype(BF16)
    winb_ref[:, :BRANCH] = w[:, :BRANCH]
    pm = pm_ref[...]
    for g in range(N_GROUPS):
        cols = slice(BRANCH + g * GROUP_W, BRANCH + (g + 1) * GROUP_W)
        winb_ref[:, cols] = _dot(w[:, cols], pm).astype(BF16)
    woutb_ref[...] = _dot(pm, wout_ref[...].astype(BF16)).astype(BF16)


def _layer0_prep(c_ctx, c, w_mod, b_mod, w_in, w_out):
    n_lat = c.shape[0]
    rows = MOD_K_CHUNK
    assert rows == GROUP_W
    perm = _mirror_perm()
    pm = jnp.asarray((np.arange(GROUP_W)[:, None] == perm[None, :]).astype(np.float32)).astype(BF16)
    return pl.pallas_call(
        _prep_kernel,
        grid=(D_MODEL // rows,),
        in_specs=_mod_in_specs(n_lat, rows) + [
            pl.BlockSpec((rows, w_in.shape[1]), lambda k: (k, 0)),
            pl.BlockSpec((rows, w_out.shape[1]), lambda k: (k, 0)),
            pl.BlockSpec((GROUP_W, GROUP_W), lambda k: (0, 0)),
        ],
        out_specs=[
            pl.BlockSpec((MOD_ROWS, 3 * D_MODEL), lambda k: (0, 0)),
            pl.BlockSpec((rows, w_in.shape[1]), lambda k: (k, 0)),
            pl.BlockSpec((rows, w_out.shape[1]), lambda k: (k, 0)),
        ],
        out_shape=[
            jax.ShapeDtypeStruct((MOD_ROWS, 3 * D_MODEL), F32),
            jax.ShapeDtypeStruct(w_in.shape, BF16),
            jax.ShapeDtypeStruct(w_out.shape, BF16),
        ],
        scratch_shapes=[pltpu.VMEM((MOD_ROWS, rows), F32)],
        compiler_params=pltpu.CompilerParams(dimension_semantics=("arbitrary",)),
        name="layer0_prep",
    )(c_ctx.reshape(1, D_MODEL), c, w_mod, b_mod.reshape(1, 3 * D_MODEL), w_in, w_out, pm)


def _fourier_prep_kernel(x_ref, mod_ref, nw_ref, win_ref, wout_ref, m1_ref, cs_ref, ss_ref,
                         cctx_ref, c_ref, wmod1_ref, bmod1_ref, win1_ref, wout1_ref,
                         o_ref, mod1_ref, wqkvt1_ref, wz1_ref, wout1b_ref,
                         ta_scr, tb_scr, tr_scr, z_scr, cond_scr, *, seq):
    _mod_accumulate(cctx_ref, c_ref, wmod1_ref, bmod1_ref, mod1_ref, cond_scr)
    w = win1_ref[...]
    wqkvt1_ref[...] = w[:, :BRANCH + 2 * KV_W].T.astype(BF16)
    wz1_ref[...] = w[:, BRANCH + 2 * KV_W:].astype(BF16)
    wout1b_ref[...] = wout1_ref[...].astype(BF16)
    _fourier_kernel(x_ref, mod_ref, nw_ref, win_ref, wout_ref, m1_ref, cs_ref, ss_ref, o_ref,
                    ta_scr, tb_scr, tr_scr, z_scr, seq=seq, per_request=False)


def _fourier_kernel(x_ref, mod_ref, nw_ref, win_ref, wout_ref, m1_ref, cs_ref, ss_ref, o_ref,
                    ta_scr, tb_scr, tr_scr, z_scr, *, seq, per_request):
    mod = _mod_row(mod_ref, per_request)
    gate = mod[:, 2 * D_MODEL:]
    nw = nw_ref[...]
    n_seq = x_ref.shape[0]
    n_chunks = seq // ROW_CHUNK
    lane = lax.broadcasted_iota(jnp.int32, (ROW_CHUNK, HALF_W), 1)
    for i in range(n_seq):
        for c in range(n_chunks):
            rows = slice(c * ROW_CHUNK, (c + 1) * ROW_CHUNK)
            h = _mod_norm(x_ref[i, rows, :], nw, mod).astype(BF16)
            uz = _dot(h, win_ref[...])
            z_scr[i, rows, :] = uz[:, BRANCH:]
            u = uz[:, :BRANCH].astype(BF16)
            tr = jnp.zeros((ROW_CHUNK, HALF_W), F32)
            for g in range(N_GROUPS):
                t = _dot(u[:, g * GROUP_W:(g + 1) * GROUP_W], m1_ref[...])
                half = slice(g * HALF_W, (g + 1) * HALF_W)
                ta_scr[i, rows, half] = t[:, :HALF_W].astype(BF16)
                tb = t[:, HALF_W:]
                tb_scr[i, rows, half] = tb.astype(BF16)
                tr = jnp.where(lane == g, tb if g == 0 else pltpu.roll(tb, g, axis=1), tr)
            tr_scr[i, rows, :] = tr.astype(BF16)
    for i in range(n_seq):
        for c in range(n_chunks):
            rows = slice(c * ROW_CHUNK, (c + 1) * ROW_CHUNK)
            cs = cs_ref[rows, :]
            p = _dot(cs, ta_scr[i])
            q = _dot(ss_ref[rows, :], tb_scr[i])
            r = _dot(cs, tr_scr[i])
            parts = []
            for g in range(N_GROUPS):
                half = slice(g * HALF_W, (g + 1) * HALF_W)
                pg, qg = p[:, half], q[:, half]
                rg = r if g == 0 else pltpu.roll(r, HALF_W - g, axis=1)
                parts.append(jnp.where(lane == 0, pg, pg - qg))
                parts.append(jnp.where(lane == 0, rg, pg + qg))
            y = jnp.concatenate(parts, axis=1)
            y = (y * jax.nn.silu(z_scr[i, rows, :])).astype(BF16)
            o_ref[i, rows, :] = x_ref[i, rows, :] + gate * _dot(y, wout_ref[...])


def _dft_tables(seq):
    c = np.arange(GROUP_W)[:, None]
    k = np.arange(HALF_W)[None, :]
    cos_lo = np.cos(2.0 * np.pi * ((c * k) % GROUP_W) / GROUP_W)
    sin_lo = np.sin(2.0 * np.pi * ((c * k) % GROUP_W) / GROUP_W)
    sin_lo[:, 0] = np.cos(np.pi * c[:, 0])
    m1 = np.concatenate([cos_lo, sin_lo], axis=1) / np.sqrt(GROUP_W)
    n = np.arange(seq)
    ang = 2.0 * np.pi * ((n[:, None] * n[None, :]) % seq) / seq
    cs = np.cos(ang) / np.sqrt(seq)
    ss = np.sin(ang) / np.sqrt(seq)
    return m1.astype(np.float32), cs.astype(np.float32), ss.astype(np.float32)


def _const_spec(shape):
    return pl.BlockSpec(shape, lambda b: (0,) * len(shape))


def _fourier_layer(x, mod, per_request, nw, win, wout, next_layer=None):
    nb, seq, _ = x.shape
    n_seq = 1 if per_request else CTX_SEQS_PER_STEP
    assert nb % n_seq == 0
    n_steps = nb // n_seq
    m1, cs, ss = (jnp.asarray(t).astype(BF16) for t in _dft_tables(seq))
    in_specs = [
        pl.BlockSpec((n_seq, seq, D_MODEL), lambda b: (b, 0, 0)),
        MOD_SPEC,
        _const_spec((1, D_MODEL)),
        _const_spec((D_MODEL, 2 * BRANCH)),
        _const_spec((BRANCH, D_MODEL)),
        _const_spec((GROUP_W, GROUP_W)),
        _const_spec((seq, seq)),
        _const_spec((seq, seq)),
    ]
    out_specs = [pl.BlockSpec((n_seq, seq, D_MODEL), lambda b: (b, 0, 0))]
    out_shape = [jax.ShapeDtypeStruct(x.shape, F32)]
    scratch_shapes = [
        pltpu.VMEM((n_seq, seq, N_GROUPS * HALF_W), BF16),
        pltpu.VMEM((n_seq, seq, N_GROUPS * HALF_W), BF16),
        pltpu.VMEM((n_seq, seq, HALF_W), BF16),
        pltpu.VMEM((n_seq, seq, BRANCH), F32),
    ]
    args = [x, mod, nw, win, wout, m1, cs, ss]
    if next_layer is None:
        kern = functools.partial(_fourier_kernel, seq=seq, per_request=per_request)
    else:
        assert not per_request and D_MODEL % (n_steps * LANES) == 0
        c_ctx, c, w_mod1, b_mod1, w_in1, w_out1 = next_layer
        rows = D_MODEL // n_steps
        n_qkvz = 2 * BRANCH + 2 * KV_W
        kern = functools.partial(_fourier_prep_kernel, seq=seq)
        in_specs += _mod_in_specs(c.shape[0], rows) + [
            pl.BlockSpec((rows, n_qkvz), lambda b: (b, 0)),
            pl.BlockSpec((rows, D_MODEL), lambda b: (b, 0))]
        out_specs += [MOD_SPEC,
                      pl.BlockSpec((BRANCH + 2 * KV_W, rows), lambda b: (0, b)),
                      pl.BlockSpec((rows, BRANCH), lambda b: (b, 0)),
                      pl.BlockSpec((rows, D_MODEL), lambda b: (b, 0))]
        out_shape += [jax.ShapeDtypeStruct((MOD_ROWS, 3 * D_MODEL), F32),
                      jax.ShapeDtypeStruct((BRANCH + 2 * KV_W, D_MODEL), BF16),
                      jax.ShapeDtypeStruct((D_MODEL, BRANCH), BF16),
                      jax.ShapeDtypeStruct((BRANCH, D_MODEL), BF16)]
        scratch_shapes += [pltpu.VMEM((MOD_ROWS, rows), F32)]
        args += [c_ctx.reshape(1, D_MODEL), c, w_mod1, b_mod1.reshape(1, 3 * D_MODEL), w_in1, w_out1]
    outs = pl.pallas_call(
        kern,
        grid=(n_steps,),
        in_specs=in_specs,
        out_specs=out_specs,
        out_shape=out_shape,
        scratch_shapes=scratch_shapes,
        compiler_params=pltpu.CompilerParams(
            dimension_semantics=("arbitrary",), vmem_limit_bytes=VMEM_LIMIT),
        name=f"fourier_layer_s{seq}",
    )(*args)
    return outs[0] if next_layer is None else outs


def _head_weight_tile(w_ref, n_tokens):
    row = jnp.broadcast_to(w_ref[...], (HEAD_DIM, HEAD_DIM))
    ii = lax.broadcasted_iota(jnp.int32, (HEAD_DIM, HEAD_DIM), 0)
    jj = lax.broadcasted_iota(jnp.int32, (HEAD_DIM, HEAD_DIM), 1)
    col = jnp.sum(jnp.where(ii == jj, row, 0.0), axis=1, keepdims=True)
    return jnp.broadcast_to(col, (HEAD_DIM, n_tokens))


def _head_rms(t, w):
    return (t * lax.rsqrt(jnp.mean(t * t, axis=0, keepdims=True) + EPS)) * w


def _rope_t(t, cos, sin):
    half = HEAD_DIM // 2
    x1, x2 = t[:half], t[half:]
    return jnp.concatenate([x1 * cos - x2 * sin, x1 * sin + x2 * cos], axis=0)


def _head_scores(qn, g, keys, biases):
    zeros = jnp.zeros_like(qn)
    qz = jnp.concatenate([qn, zeros] if g % 2 == 0 else [zeros, qn], axis=0)
    blk = slice((g // 2) * LANES, (g // 2 + 1) * LANES)
    scores = []
    smax = None
    for k, bias in zip(keys, biases):
        s = _dot(k[:, blk], qz)
        s = s if bias is None else s + bias
        cmax = jnp.max(s, axis=0, keepdims=True)
        smax = cmax if smax is None else jnp.maximum(smax, cmax)
        scores.append(s)
    return scores, smax


ONES_ROWS = 16


def _with_ones_rows(vt):
    return jnp.concatenate([vt, jnp.ones((ONES_ROWS, vt.shape[1]), vt.dtype)], axis=0)


def _head_probs(scored, sink2):
    scores, smax = scored
    m = jnp.maximum(smax, sink2)
    return jnp.concatenate([jnp.exp2(s - m).astype(BF16) for s in scores], axis=0), m


def _head_pv(probs, values_t, sink2):
    p, m = probs
    acc = _dot(values_t, p)
    den = acc[HEAD_DIM:HEAD_DIM + 1] + jnp.exp2(sink2 - m)
    return acc[:HEAD_DIM] * (1.0 / den)


def _attend_heads(n_units, stage, scores_fn, probs_fn, pv_fn, fillers=()):
    n_stages = n_units // stage
    pending = [scores_fn(u) for u in range(stage)]
    for g in range(n_stages):
        units = range(g * stage, (g + 1) * stage)
        if g < len(fillers):
            fillers[g]()
        nxt = [scores_fn(u) for u in range((g + 1) * stage, (g + 2) * stage)] if g + 1 < n_stages else None
        probs = [probs_fn(u, sc) for u, sc in zip(units, pending)]
        for u, pr in zip(units, probs):
            pv_fn(u, pr)
        pending = nxt


def _gate_out(x, o, z, gate, wout):
    y = (o * jax.nn.silu(z)).astype(BF16)
    return x + gate * _dot(y, wout)


def _attn_ctx_kernel(sink_ref, x_ref, mod_ref, nw_ref, wqkvt_ref, wz_ref, wout_ref, qw_ref, kw_ref,
                     o_ref, kto_ref, vto_ref, qkvt_scr, z_scr, ot_scr):
    seq = x_ref.shape[1]
    n_seq = x_ref.shape[0]
    mod = _mod_row(mod_ref, False)
    gate = mod[:, 2 * D_MODEL:]
    kw = _head_weight_tile(kw_ref, seq)
    qw = _head_weight_tile(qw_ref, seq)

    def project(i):
        h = _mod_norm(x_ref[i], nw_ref[...], mod).astype(BF16)
        qkvt_scr[i] = _dot_nt(wqkvt_ref[...], h)
        z_scr[i] = _dot(h, wz_ref[...])

    def keys_values(i):
        knt = jnp.concatenate(
            [_head_rms(qkvt_scr[i, BRANCH + g * HEAD_DIM:BRANCH + (g + 1) * HEAD_DIM, :], kw)
             for g in range(N_KV)], axis=0)
        kto_ref[i] = knt
        vtf = qkvt_scr[i, BRANCH + KV_W:, :]
        vto_ref[i] = vtf
        vt = vtf.astype(BF16)
        return (knt.T.astype(BF16),
                [_with_ones_rows(vt[g * HEAD_DIM:(g + 1) * HEAD_DIM]) for g in range(N_KV)])

    def output(i):
        o_ref[i] = _gate_out(x_ref[i], ot_scr[i].T, z_scr[i], gate, wout_ref[...])

    for i in range(n_seq):
        project(i)
    kv = [keys_values(i) for i in range(n_seq)]

    def scores_fn(u):
        i, hd = divmod(u, N_HEADS)
        t = qkvt_scr[i, hd * HEAD_DIM:(hd + 1) * HEAD_DIM, :]
        qn = (_head_rms(t, qw) * (HEAD_DIM ** -0.5 * LOG2E)).astype(BF16)
        return _head_scores(qn, hd // GQA, [kv[i][0]], [None])

    def probs_fn(u, sc):
        return _head_probs(sc, sink_ref[u % N_HEADS] * LOG2E)

    def pv_fn(u, pr):
        i, hd = divmod(u, N_HEADS)
        ot_scr[i, hd * HEAD_DIM:(hd + 1) * HEAD_DIM, :] = _head_pv(
            pr, kv[i][1][hd // GQA], sink_ref[hd] * LOG2E)

    fillers = [lambda: None] + [functools.partial(output, i) for i in range(n_seq - 1)]
    _attend_heads(n_seq * N_HEADS, N_HEADS, scores_fn, probs_fn, pv_fn, fillers)
    output(n_seq - 1)


def _attn_ctx_layer(x, mod, nw, wqkvt, wz, wout, qw, kw, sink):
    nb, seq, _ = x.shape
    n_seq = CTX_SEQS_PER_STEP
    assert nb % n_seq == 0
    return pl.pallas_call(
        _attn_ctx_kernel,
        grid=(nb // n_seq,),
        in_specs=[
            pl.BlockSpec(memory_space=pltpu.SMEM),
            pl.BlockSpec((n_seq, seq, D_MODEL), lambda b: (b, 0, 0)),
            MOD_SPEC,
            _const_spec((1, D_MODEL)),
            _const_spec((BRANCH + 2 * KV_W, D_MODEL)),
            _const_spec((D_MODEL, BRANCH)),
            _const_spec((BRANCH, D_MODEL)),
            _const_spec((1, HEAD_DIM)),
            _const_spec((1, HEAD_DIM)),
        ],
        out_specs=[
            pl.BlockSpec((n_seq, seq, D_MODEL), lambda b: (b, 0, 0)),
            pl.BlockSpec((n_seq, KV_W, seq), lambda b: (b, 0, 0)),
            pl.BlockSpec((n_seq, KV_W, seq), lambda b: (b, 0, 0)),
        ],
        out_shape=[
            jax.ShapeDtypeStruct(x.shape, F32),
            jax.ShapeDtypeStruct((nb, KV_W, seq), F32),
            jax.ShapeDtypeStruct((nb, KV_W, seq), F32),
        ],
        scratch_shapes=[
            pltpu.VMEM((n_seq, BRANCH + 2 * KV_W, seq), F32),
            pltpu.VMEM((n_seq, seq, BRANCH), F32),
            pltpu.VMEM((n_seq, BRANCH, seq), F32),
        ],
        compiler_params=pltpu.CompilerParams(
            dimension_semantics=("arbitrary",), vmem_limit_bytes=VMEM_LIMIT),
        name="attn_context_layer",
    )(sink, x, mod, nw, wqkvt, wz, wout, qw, kw)


def _attn_lat_kernel(sink_ref, x_ref, mod_ref, nw_ref, wqkvt_ref, wz_ref, wout_ref, qw_ref, kw_ref,
                     cos_ref, sin_ref, ck_ref, cv_ref, o_ref,
                     q_scr, z_scr, ot_scr, k_scr, vt_scr, *, seq):
    mod = _mod_row(mod_ref, True)
    gate = mod[:, 2 * D_MODEL:]
    nw = nw_ref[...]
    qw = _head_weight_tile(qw_ref, Q_BLOCK)
    kw = _head_weight_tile(kw_ref, Q_BLOCK)
    n_blocks = seq // Q_BLOCK
    kv_blocks = seq // BLOCK
    k_scr[0:BLOCK, :] = jnp.zeros((BLOCK, KV_W), BF16)
    k_scr[BLOCK + seq:2 * BLOCK + seq, :] = jnp.zeros((BLOCK, KV_W), BF16)
    vt_scr[0] = jnp.zeros((KV_W, BLOCK), BF16)
    vt_scr[kv_blocks + 1] = jnp.zeros((KV_W, BLOCK), BF16)

    def project(c):
        rows = slice(c * Q_BLOCK, (c + 1) * Q_BLOCK)
        h = _mod_norm(x_ref[0, rows, :], nw, mod).astype(BF16)
        z_scr[rows, :] = _dot(h, wz_ref[...])
        qkvt = _dot_nt(wqkvt_ref[...], h)
        cos = cos_ref[:, rows]
        sin = sin_ref[:, rows]
        for hd in range(N_HEADS):
            hr = slice(hd * HEAD_DIM, (hd + 1) * HEAD_DIM)
            t = _rope_t(_head_rms(qkvt[hr], qw), cos, sin)
            t = (t * (HEAD_DIM ** -0.5 * LOG2E)).astype(BF16)
            for j in range(HALVES):
                q_scr[c * HALVES + j, hd // 2, :, (hd % 2) * BLOCK:(hd % 2 + 1) * BLOCK] = (
                    t[:, j * BLOCK:(j + 1) * BLOCK])
        knt = jnp.concatenate(
            [_rope_t(_head_rms(qkvt[BRANCH + g * HEAD_DIM:BRANCH + (g + 1) * HEAD_DIM], kw), cos, sin)
             for g in range(N_KV)], axis=0)
        k_scr[BLOCK + c * Q_BLOCK:BLOCK + (c + 1) * Q_BLOCK, :] = knt.T.astype(BF16)
        vt = qkvt[BRANCH + KV_W:].astype(BF16)
        for j in range(HALVES):
            vt_scr[1 + c * HALVES + j] = vt[:, j * BLOCK:(j + 1) * BLOCK]

    ckb = ck_ref[0].T.astype(BF16)
    cvt = cv_ref[0].astype(BF16)

    win_len = 3 * BLOCK
    n_pairs = N_HEADS // 2
    kj = lax.broadcasted_iota(jnp.int32, (win_len, 2 * BLOCK), 0)
    lane = lax.broadcasted_iota(jnp.int32, (win_len, 2 * BLOCK), 1)
    rel = kj - BLOCK - lane % BLOCK
    band = (rel >= -WINDOW) & (rel <= WINDOW)
    first_head = lax.broadcasted_iota(jnp.int32, (1, 2 * BLOCK), 1) < BLOCK

    def half_operands(hb):
        r0 = pl.multiple_of(hb * BLOCK, BLOCK)
        kpos = kj + (r0 - BLOCK)
        valid = jnp.where(band & (kpos >= 0) & (kpos < seq), 0.0, NEG_INF)
        kwin = k_scr[pl.ds(r0, win_len), :]
        vall = jnp.concatenate([vt_scr[hb + j] for j in range(win_len // BLOCK)] + [cvt], axis=1)
        valls = [_with_ones_rows(vall[g * HEAD_DIM:(g + 1) * HEAD_DIM]) for g in range(N_KV)]
        return valid, kwin, valls

    def pair_sink2(pair):
        return jnp.where(first_head, sink_ref[2 * pair], sink_ref[2 * pair + 1]) * LOG2E

    def attend(n, carry):
        halves = [half_operands(n * HALVES + j) for j in range(HALVES)]

        def scores_fn(u):
            j, pair = divmod(u, n_pairs)
            valid, kwin, _ = halves[j]
            return _head_scores(q_scr[n * HALVES + j, pair], pair // (GQA // 2), [kwin, ckb], [valid, None])

        def probs_fn(u, sc):
            return _head_probs(sc, pair_sink2(u % n_pairs))

        def pv_fn(u, pr):
            j, pair = divmod(u, n_pairs)
            o = _head_pv(pr, halves[j][2][pair // (GQA // 2)], pair_sink2(pair))
            for i in range(2):
                hd = 2 * pair + i
                ot_scr[n, hd * HEAD_DIM:(hd + 1) * HEAD_DIM, j * BLOCK:(j + 1) * BLOCK] = (
                    o[:, i * BLOCK:(i + 1) * BLOCK])

        _attend_heads(HALVES * n_pairs, GQA, scores_fn, probs_fn, pv_fn)
        return carry

    for c in range(n_blocks):
        project(c)
    lax.fori_loop(0, n_blocks, attend, 0)
    for c in range(n_blocks):
        rows = slice(c * Q_BLOCK, (c + 1) * Q_BLOCK)
        o_ref[0, rows, :] = _gate_out(x_ref[0, rows, :], ot_scr[c].T, z_scr[rows, :], gate, wout_ref[...])


def _rope_tables_t(seq):
    pos = np.arange(seq)
    n_freq = HEAD_DIM // 4
    inv = ROPE_THETA ** (-np.arange(n_freq, dtype=np.float64) / n_freq)
    ang = np.concatenate([(pos // GRID_W)[:, None] * inv, (pos % GRID_W)[:, None] * inv], axis=-1)
    return np.cos(ang).T.astype(np.float32), np.sin(ang).T.astype(np.float32)


def _attn_lat_layer(x, mod, nw, wqkvt, wz, wout, qw, kw, sink, ckt, cvt):
    nb, seq, _ = x.shape
    past = ckt.shape[2]
    cos, sin = (jnp.asarray(t) for t in _rope_tables_t(seq))
    kern = functools.partial(_attn_lat_kernel, seq=seq)
    n_blocks = seq // Q_BLOCK
    return pl.pallas_call(
        kern,
        grid=(nb,),
        in_specs=[
            pl.BlockSpec(memory_space=pltpu.SMEM),
            pl.BlockSpec((1, seq, D_MODEL), lambda b: (b, 0, 0)),
            MOD_SPEC,
            _const_spec((1, D_MODEL)),
            _const_spec((BRANCH + 2 * KV_W, D_MODEL)),
            _const_spec((D_MODEL, BRANCH)),
            _const_spec((BRANCH, D_MODEL)),
            _const_spec((1, HEAD_DIM)),
            _const_spec((1, HEAD_DIM)),
            _const_spec((HEAD_DIM // 2, seq)),
            _const_spec((HEAD_DIM // 2, seq)),
            pl.BlockSpec((1, KV_W, past), lambda b: (b, 0, 0)),
            pl.BlockSpec((1, KV_W, past), lambda b: (b, 0, 0)),
        ],
        out_specs=pl.BlockSpec((1, seq, D_MODEL), lambda b: (b, 0, 0)),
        out_shape=jax.ShapeDtypeStruct(x.shape, F32),
        scratch_shapes=[
            pltpu.VMEM((seq // BLOCK, N_HEADS // 2, HEAD_DIM, 2 * BLOCK), BF16),
            pltpu.VMEM((seq, BRANCH), F32),
            pltpu.VMEM((n_blocks, BRANCH, Q_BLOCK), F32),
            pltpu.VMEM((seq + 2 * BLOCK, KV_W), BF16),
            pltpu.VMEM((seq // BLOCK + 2, KV_W, BLOCK), BF16),
        ],
        compiler_params=pltpu.CompilerParams(
            dimension_semantics=("arbitrary",), vmem_limit_bytes=VMEM_LIMIT),
        name="attn_latent_layer",
    )(sink, x, mod, nw, wqkvt, wz, wout, qw, kw, cos, sin, ckt, cvt)


def kernel(x_prompt, x_sample, cache_k_l1, cache_v_l1, c, c_ctx, norm_w_l0, w_mod_l0, b_mod_l0,
           w_in_l0, w_out_l0, norm_w_l1, w_mod_l1, b_mod_l1, w_in_l1, q_norm_w_l1, k_norm_w_l1,
           sink_l1, w_out_l1):
    nb_ctx, seq_ctx, _ = x_prompt.shape
    nb_lat = x_sample.shape[0]
    past = cache_k_l1.shape[1]
    assert 1 + nb_lat <= MOD_ROWS
    mod0, win0, wout0 = _layer0_prep(c_ctx, c, w_mod_l0, b_mod_l0, w_in_l0, w_out_l0)

    nw0 = norm_w_l0.reshape(1, D_MODEL)
    nw1 = norm_w_l1.reshape(1, D_MODEL)
    qw = q_norm_w_l1.reshape(1, HEAD_DIM)
    kw = k_norm_w_l1.reshape(1, HEAD_DIM)

    xp, mod1, wqkvt1, wz1, wout1 = _fourier_layer(
        x_prompt, mod0, False, nw0, win0, wout0,
        next_layer=(c_ctx, c, w_mod_l1, b_mod_l1, w_in_l1, w_out_l1))
    xs = _fourier_layer(x_sample, mod0, True, nw0, win0, wout0)

    def to_feature_major(t):
        return jnp.transpose(t, (0, 2, 3, 1)).reshape(t.shape[0], KV_W, t.shape[1])

    def from_feature_major(t):
        return jnp.transpose(t.reshape(t.shape[0], N_KV, HEAD_DIM, t.shape[2]), (0, 3, 1, 2))

    xp, new_kt, new_vt = _attn_ctx_layer(xp, mod1, nw1, wqkvt1, wz1, wout1, qw, kw, sink_l1)
    xs = _attn_lat_layer(xs, mod1, nw1, wqkvt1, wz1, wout1, qw, kw, sink_l1,
                         to_feature_major(cache_k_l1), to_feature_major(cache_v_l1))
    return (xp, xs, from_feature_major(new_kt), from_feature_major(new_vt))
```

```python
import functools

import numpy as np
import jax
import jax.numpy as jnp
from jax import lax
from jax.experimental import pallas as pl
from jax.experimental.pallas import tpu as pltpu

D_MODEL = 1024
BRANCH = 1024
N_GROUPS = 4
GROUP_W = BRANCH // N_GROUPS
HALF_W = GROUP_W // 2
HEAD_DIM = 64
N_HEADS = 16
N_KV = 4
GQA = N_HEADS // N_KV
KV_W = N_KV * HEAD_DIM
GRID_W = 64
WINDOW = 128
BLOCK = 128
ROPE_THETA = 10000.0
EPS = 1e-6
NEG_INF = -1e30
LANES = 128
ROW_CHUNK = 256
Q_BLOCK = 256
HALVES = Q_BLOCK // BLOCK
CTX_SEQS_PER_STEP = 2
VMEM_LIMIT = 56 * 1024 * 1024
MOD_ROWS = 8
MOD_K_CHUNK = 256
ONES_ROWS = 16
LOG2E = float(np.log2(np.e))

F32 = jnp.float32
BF16 = jnp.bfloat16


def _dot(a, b):
    return jnp.dot(a, b, preferred_element_type=F32)


def _dot_nt(a, b):
    return lax.dot_general(a, b, (((1,), (1,)), ((), ())), preferred_element_type=F32)


def _mod_row(mod_ref, per_request):
    if per_request:
        return mod_ref[pl.ds(1 + pl.program_id(0), 1), :]
    return mod_ref[0:1, :]


MOD_SPEC = pl.BlockSpec((MOD_ROWS, 3 * D_MODEL), lambda b: (0, 0))


def _mod_norm(x, nw, mod):
    shift = mod[:, :D_MODEL]
    scale = mod[:, D_MODEL:2 * D_MODEL]
    y = x * lax.rsqrt(jnp.mean(x * x, axis=-1, keepdims=True) + EPS)
    return (y * nw) * (1.0 + scale) + shift


def _mod_accumulate(cctx_ref, c_ref, w_ref, b_ref, o_ref, cond_scr):
    n_lat = c_ref.shape[0]
    cond_scr[...] = jnp.zeros_like(cond_scr)
    cond_scr[0:1, :] = cctx_ref[...]
    cond_scr[1:1 + n_lat, :] = c_ref[...]
    s = jax.nn.silu(cond_scr[...]).astype(BF16)

    @pl.when(pl.program_id(0) == 0)
    def _():
        o_ref[...] = jnp.broadcast_to(b_ref[...], o_ref.shape)

    o_ref[...] += _dot(s, w_ref[...].astype(BF16))


def _mod_in_specs(n_lat, rows):
    return [pl.BlockSpec((1, rows), lambda k: (0, k)),
            pl.BlockSpec((n_lat, rows), lambda k: (0, k)),
            pl.BlockSpec((rows, 3 * D_MODEL), lambda k: (k, 0)),
            pl.BlockSpec((1, 3 * D_MODEL), lambda k: (0, 0))]


def _mirror_perm():
    j = np.arange(GROUP_W)
    return np.where(j <= HALF_W, j, GROUP_W + HALF_W - j)


def _prep_kernel(cctx_ref, c_ref, wmod_ref, bmod_ref, win_ref, wout_ref, pm_ref,
                 mod_ref, winb_ref, woutb_ref, cond_scr):
    _mod_accumulate(cctx_ref, c_ref, wmod_ref, bmod_ref, mod_ref, cond_scr)
    w = win_ref[...].astype(BF16)
    winb_ref[:, :BRANCH] = w[:, :BRANCH]
    pm = pm_ref[...]
    for g in range(N_GROUPS):
        cols = slice(BRANCH + g * GROUP_W, BRANCH + (g + 1) * GROUP_W)
        winb_ref[:, cols] = _dot(w[:, cols], pm).astype(BF16)
    woutb_ref[...] = _dot(pm, wout_ref[...].astype(BF16)).astype(BF16)


def _layer0_prep(c_ctx, c, w_mod, b_mod, w_in, w_out):
    n_lat = c.shape[0]
    rows = MOD_K_CHUNK
    assert rows == GROUP_W
    perm = _mirror_perm()
    pm = jnp.asarray((np.arange(GROUP_W)[:, None] == perm[None, :]).astype(np.float32)).astype(BF16)
    return pl.pallas_call(
        _prep_kernel,
        grid=(D_MODEL // rows,),
        in_specs=_mod_in_specs(n_lat, rows) + [
            pl.BlockSpec((rows, w_in.shape[1]), lambda k: (k, 0)),
            pl.BlockSpec((rows, w_out.shape[1]), lambda k: (k, 0)),
            pl.BlockSpec((GROUP_W, GROUP_W), lambda k: (0, 0)),
        ],
        out_specs=[
            pl.BlockSpec((MOD_ROWS, 3 * D_MODEL), lambda k: (0, 0)),
            pl.BlockSpec((rows, w_in.shape[1]), lambda k: (k, 0)),
            pl.BlockSpec((rows, w_out.shape[1]), lambda k: (k, 0)),
        ],
        out_shape=[
            jax.ShapeDtypeStruct((MOD_ROWS, 3 * D_MODEL), F32),
            jax.ShapeDtypeStruct(w_in.shape, BF16),
            jax.ShapeDtypeStruct(w_out.shape, BF16),
        ],
        scratch_shapes=[pltpu.VMEM((MOD_ROWS, rows), F32)],
        compiler_params=pltpu.CompilerParams(dimension_semantics=("arbitrary",)),
        name="layer0_prep",
    )(c_ctx.reshape(1, D_MODEL), c, w_mod, b_mod.reshape(1, 3 * D_MODEL), w_in, w_out, pm)


def _fourier_prep_kernel(x_ref, mod_ref, nw_ref, win_ref, wout_ref, m1_ref, cs_ref, ss_ref,
                         cctx_ref, c_ref, wmod1_ref, bmod1_ref, win1_ref, wout1_ref,
                         o_ref, mod1_ref, wqkvt1_ref, wz1_ref, wout1b_ref,
                         ta_scr, tb_scr, tr_scr, z_scr, cond_scr, *, seq):
    _mod_accumulate(cctx_ref, c_ref, wmod1_ref, bmod1_ref, mod1_ref, cond_scr)
    w = win1_ref[...]
    wqkvt1_ref[...] = w[:, :BRANCH + 2 * KV_W].T.astype(BF16)
    wz1_ref[...] = w[:, BRANCH + 2 * KV_W:].astype(BF16)
    wout1b_ref[...] = wout1_ref[...].astype(BF16)
    _fourier_kernel(x_ref, mod_ref, nw_ref, win_ref, wout_ref, m1_ref, cs_ref, ss_ref, o_ref,
                    ta_scr, tb_scr, tr_scr, z_scr, seq=seq, per_request=False)


def _fourier_kernel(x_ref, mod_ref, nw_ref, win_ref, wout_ref, m1_ref, cs_ref, ss_ref, o_ref,
                    ta_scr, tb_scr, tr_scr, z_scr, *, seq, per_request):
    mod = _mod_row(mod_ref, per_request)
    gate = mod[:, 2 * D_MODEL:]
    nw = nw_ref[...]
    n_seq = x_ref.shape[0]
    n_chunks = seq // ROW_CHUNK
    lane = lax.broadcasted_iota(jnp.int32, (ROW_CHUNK, HALF_W), 1)
    for i in range(n_seq):
        for c in range(n_chunks):
            rows = slice(c * ROW_CHUNK, (c + 1) * ROW_CHUNK)
            h = _mod_norm(x_ref[i, rows, :], nw, mod).astype(BF16)
            uz = _dot(h, win_ref[...])
            z_scr[i, rows, :] = uz[:, BRANCH:]
            u = uz[:, :BRANCH].astype(BF16)
            tr = jnp.zeros((ROW_CHUNK, HALF_W), F32)
            for g in range(N_GROUPS):
                t = _dot(u[:, g * GROUP_W:(g + 1) * GROUP_W], m1_ref[...])
                half = slice(g * HALF_W, (g + 1) * HALF_W)
                ta_scr[i, rows, half] = t[:, :HALF_W].astype(BF16)
                tb = t[:, HALF_W:]
                tb_scr[i, rows, half] = tb.astype(BF16)
                tr = jnp.where(lane == g, tb if g == 0 else pltpu.roll(tb, g, axis=1), tr)
            tr_scr[i, rows, :] = tr.astype(BF16)
    for i in range(n_seq):
        for c in range(n_chunks):
            rows = slice(c * ROW_CHUNK, (c + 1) * ROW_CHUNK)
            cs = cs_ref[rows, :]
            p = _dot(cs, ta_scr[i])
            q = _dot(ss_ref[rows, :], tb_scr[i])
            r = _dot(cs, tr_scr[i])
            parts = []
            for g in range(N_GROUPS):
                half = slice(g * HALF_W, (g + 1) * HALF_W)
                pg, qg = p[:, half], q[:, half]
                rg = r if g == 0 else pltpu.roll(r, HALF_W - g, axis=1)
                parts.append(jnp.where(lane == 0, pg, pg - qg))
                parts.append(jnp.where(lane == 0, rg, pg + qg))
            y = jnp.concatenate(parts, axis=1)
            y = (y * jax.nn.silu(z_scr[i, rows, :])).astype(BF16)
            o_ref[i, rows, :] = x_ref[i, rows, :] + gate * _dot(y, wout_ref[...])


def _dft_tables(seq):
    c = np.arange(GROUP_W)[:, None]
    k = np.arange(HALF_W)[None, :]
    cos_lo = np.cos(2.0 * np.pi * ((c * k) % GROUP_W) / GROUP_W)
    sin_lo = np.sin(2.0 * np.pi * ((c * k) % GROUP_W) / GROUP_W)
    sin_lo[:, 0] = np.cos(np.pi * c[:, 0])
    m1 = np.concatenate([cos_lo, sin_lo], axis=1) / np.sqrt(GROUP_W)
    n = np.arange(seq)
    ang = 2.0 * np.pi * ((n[:, None] * n[None, :]) % seq) / seq
    cs = np.cos(ang) / np.sqrt(seq)
    ss = np.sin(ang) / np.sqrt(seq)
    return m1.astype(np.float32), cs.astype(np.float32), ss.astype(np.float32)


def _const_spec(shape):
    return pl.BlockSpec(shape, lambda b: (0,) * len(shape))


def _fourier_layer(x, mod, per_request, nw, win, wout, next_layer=None):
    nb, seq, _ = x.shape
    n_seq = 1 if per_request else CTX_SEQS_PER_STEP
    assert nb % n_seq == 0
    n_steps = nb // n_seq
    m1, cs, ss = (jnp.asarray(t).astype(BF16) for t in _dft_tables(seq))
    in_specs = [
        pl.BlockSpec((n_seq, seq, D_MODEL), lambda b: (b, 0, 0)),
        MOD_SPEC,
        _const_spec((1, D_MODEL)),
        _const_spec((D_MODEL, 2 * BRANCH)),
        _const_spec((BRANCH, D_MODEL)),
        _const_spec((GROUP_W, GROUP_W)),
        _const_spec((seq, seq)),
        _const_spec((seq, seq)),
    ]
    out_specs = [pl.BlockSpec((n_seq, seq, D_MODEL), lambda b: (b, 0, 0))]
    out_shape = [jax.ShapeDtypeStruct(x.shape, F32)]
    scratch_shapes = [
        pltpu.VMEM((n_seq, seq, N_GROUPS * HALF_W), BF16),
        pltpu.VMEM((n_seq, seq, N_GROUPS * HALF_W), BF16),
        pltpu.VMEM((n_seq, seq, HALF_W), BF16),
        pltpu.VMEM((n_seq, seq, BRANCH), F32),
    ]
    args = [x, mod, nw, win, wout, m1, cs, ss]
    if next_layer is None:
        kern = functools.partial(_fourier_kernel, seq=seq, per_request=per_request)
    else:
        assert not per_request and D_MODEL % (n_steps * LANES) == 0
        c_ctx, c, w_mod1, b_mod1, w_in1, w_out1 = next_layer
        rows = D_MODEL // n_steps
        n_qkvz = 2 * BRANCH + 2 * KV_W
        kern = functools.partial(_fourier_prep_kernel, seq=seq)
        in_specs += _mod_in_specs(c.shape[0], rows) + [
            pl.BlockSpec((rows, n_qkvz), lambda b: (b, 0)),
            pl.BlockSpec((rows, D_MODEL), lambda b: (b, 0))]
        out_specs += [MOD_SPEC,
                      pl.BlockSpec((BRANCH + 2 * KV_W, rows), lambda b: (0, b)),
                      pl.BlockSpec((rows, BRANCH), lambda b: (b, 0)),
                      pl.BlockSpec((rows, D_MODEL), lambda b: (b, 0))]
        out_shape += [jax.ShapeDtypeStruct((MOD_ROWS, 3 * D_MODEL), F32),
                      jax.ShapeDtypeStruct((BRANCH + 2 * KV_W, D_MODEL), BF16),
                      jax.ShapeDtypeStruct((D_MODEL, BRANCH), BF16),
                      jax.ShapeDtypeStruct((BRANCH, D_MODEL), BF16)]
        scratch_shapes += [pltpu.VMEM((MOD_ROWS, rows), F32)]
        args += [c_ctx.reshape(1, D_MODEL), c, w_mod1, b_mod1.reshape(1, 3 * D_MODEL), w_in1, w_out1]
    outs = pl.pallas_call(
        kern,
        grid=(n_steps,),
        in_specs=in_specs,
        out_specs=out_specs,
        out_shape=out_shape,
        scratch_shapes=scratch_shapes,
        compiler_params=pltpu.CompilerParams(
            dimension_semantics=("arbitrary",), vmem_limit_bytes=VMEM_LIMIT),
        name=f"fourier_layer_s{seq}",
    )(*args)
    return outs[0] if next_layer is None else outs


def _head_weight_tile(w_ref, n_tokens):
    row = jnp.broadcast_to(w_ref[...], (HEAD_DIM, HEAD_DIM))
    ii = lax.broadcasted_iota(jnp.int32, (HEAD_DIM, HEAD_DIM), 0)
    jj = lax.broadcasted_iota(jnp.int32, (HEAD_DIM, HEAD_DIM), 1)
    col = jnp.sum(jnp.where(ii == jj, row, 0.0), axis=1, keepdims=True)
    return jnp.broadcast_to(col, (HEAD_DIM, n_tokens))


def _head_rms(t, w):
    return (t * lax.rsqrt(jnp.mean(t * t, axis=0, keepdims=True) + EPS)) * w


def _rope_t(t, cos, sin):
    half = HEAD_DIM // 2
    x1, x2 = t[:half], t[half:]
    return jnp.concatenate([x1 * cos - x2 * sin, x1 * sin + x2 * cos], axis=0)


def _head_scores(qn, g, keys, biases):
    zeros = jnp.zeros_like(qn)
    qz = jnp.concatenate([qn, zeros] if g % 2 == 0 else [zeros, qn], axis=0)
    blk = slice((g // 2) * LANES, (g // 2 + 1) * LANES)
    scores = []
    smax = None
    for k, bias in zip(keys, biases):
        s = _dot(k[:, blk], qz)
        s = s if bias is None else s + bias
        cmax = jnp.max(s, axis=0, keepdims=True)
        smax = cmax if smax is None else jnp.maximum(smax, cmax)
        scores.append(s)
    return scores, smax


def _with_ones_rows(vt):
    return jnp.concatenate([vt, jnp.ones((ONES_ROWS, vt.shape[1]), vt.dtype)], axis=0)


def _head_probs(scored, sink2):
    scores, smax = scored
    m = jnp.maximum(smax, sink2)
    return jnp.concatenate([jnp.exp2(s - m).astype(BF16) for s in scores], axis=0), m


def _head_pv(probs, values_t, sink2):
    p, m = probs
    acc = _dot(values_t, p)
    den = acc[HEAD_DIM:HEAD_DIM + 1] + jnp.exp2(sink2 - m)
    return acc[:HEAD_DIM] * (1.0 / den)


def _attend_heads(n_units, stage, scores_fn, probs_fn, pv_fn, fillers=()):
    n_stages = n_units // stage
    pending = [scores_fn(u) for u in range(stage)]
    for g in range(n_stages):
        units = range(g * stage, (g + 1) * stage)
        if g < len(fillers):
            fillers[g]()
        nxt = [scores_fn(u) for u in range((g + 1) * stage, (g + 2) * stage)] if g + 1 < n_stages else None
        probs = [probs_fn(u, sc) for u, sc in zip(units, pending)]
        for u, pr in zip(units, probs):
            pv_fn(u, pr)
        pending = nxt


def _gate_out(x, o, z, gate, wout):
    y = (o * jax.nn.silu(z)).astype(BF16)
    return x + gate * _dot(y, wout)


def _attn_ctx_kernel(sink_ref, x_ref, mod_ref, nw_ref, wqkvt_ref, wz_ref, wout_ref, qw_ref, kw_ref,
                     o_ref, kto_ref, vto_ref, qkvt_scr, z_scr, ot_scr):
    seq = x_ref.shape[1]
    n_seq = x_ref.shape[0]
    mod = _mod_row(mod_ref, False)
    gate = mod[:, 2 * D_MODEL:]
    kw = _head_weight_tile(kw_ref, seq)
    qw = _head_weight_tile(qw_ref, seq)

    def project(i):
        h = _mod_norm(x_ref[i], nw_ref[...], mod).astype(BF16)
        qkvt_scr[i] = _dot_nt(wqkvt_ref[...], h)
        z_scr[i] = _dot(h, wz_ref[...])

    def keys_values(i):
        knt = jnp.concatenate(
            [_head_rms(qkvt_scr[i, BRANCH + g * HEAD_DIM:BRANCH + (g + 1) * HEAD_DIM, :], kw)
             for g in range(N_KV)], axis=0)
        kto_ref[i] = knt
        vtf = qkvt_scr[i, BRANCH + KV_W:, :]
        vto_ref[i] = vtf
        vt = vtf.astype(BF16)
        return (knt.T.astype(BF16),
                [_with_ones_rows(vt[g * HEAD_DIM:(g + 1) * HEAD_DIM]) for g in range(N_KV)])

    def output(i):
        o_ref[i] = _gate_out(x_ref[i], ot_scr[i].T, z_scr[i], gate, wout_ref[...])

    for i in range(n_seq):
        project(i)
    kv = [keys_values(i) for i in range(n_seq)]

    def scores_fn(u):
        i, hd = divmod(u, N_HEADS)
        t = qkvt_scr[i, hd * HEAD_DIM:(hd + 1) * HEAD_DIM, :]
        qn = (_head_rms(t, qw) * (HEAD_DIM ** -0.5 * LOG2E)).astype(BF16)
        return _head_scores(qn, hd // GQA, [kv[i][0]], [None])

    def probs_fn(u, sc):
        return _head_probs(sc, sink_ref[u % N_HEADS] * LOG2E)

    def pv_fn(u, pr):
        i, hd = divmod(u, N_HEADS)
        ot_scr[i, hd * HEAD_DIM:(hd + 1) * HEAD_DIM, :] = _head_pv(
            pr, kv[i][1][hd // GQA], sink_ref[hd] * LOG2E)

    fillers = [lambda: None] + [functools.partial(output, i) for i in range(n_seq - 1)]
    _attend_heads(n_seq * N_HEADS, N_HEADS, scores_fn, probs_fn, pv_fn, fillers)
    output(n_seq - 1)


def _attn_ctx_layer(x, mod, nw, wqkvt, wz, wout, qw, kw, sink):
    nb, seq, _ = x.shape
    n_seq = CTX_SEQS_PER_STEP
    assert nb % n_seq == 0
    return pl.pallas_call(
        _attn_ctx_kernel,
        grid=(nb // n_seq,),
        in_specs=[
            pl.BlockSpec(memory_space=pltpu.SMEM),
            pl.BlockSpec((n_seq, seq, D_MODEL), lambda b: (b, 0, 0)),
            MOD_SPEC,
            _const_spec((1, D_MODEL)),
            _const_spec((BRANCH + 2 * KV_W, D_MODEL)),
            _const_spec((D_MODEL, BRANCH)),
            _const_spec((BRANCH, D_MODEL)),
            _const_spec((1, HEAD_DIM)),
            _const_spec((1, HEAD_DIM)),
        ],
        out_specs=[
            pl.BlockSpec((n_seq, seq, D_MODEL), lambda b: (b, 0, 0)),
            pl.BlockSpec((n_seq, KV_W, seq), lambda b: (b, 0, 0)),
            pl.BlockSpec((n_seq, KV_W, seq), lambda b: (b, 0, 0)),
        ],
        out_shape=[
            jax.ShapeDtypeStruct(x.shape, F32),
            jax.ShapeDtypeStruct((nb, KV_W, seq), F32),
            jax.ShapeDtypeStruct((nb, KV_W, seq), F32),
        ],
        scratch_shapes=[
            pltpu.VMEM((n_seq, BRANCH + 2 * KV_W, seq), F32),
            pltpu.VMEM((n_seq, seq, BRANCH), F32),
            pltpu.VMEM((n_seq, BRANCH, seq), F32),
        ],
        compiler_params=pltpu.CompilerParams(
            dimension_semantics=("arbitrary",), vmem_limit_bytes=VMEM_LIMIT),
        name="attn_context_layer",
    )(sink, x, mod, nw, wqkvt, wz, wout, qw, kw)


def _attn_lat_kernel(sink_ref, x_ref, mod_ref, nw_ref, wqkvt_ref, wz_ref, wout_ref, qw_ref, kw_ref,
                     cos_ref, sin_ref, ck_ref, cv_ref, o_ref,
                     q_scr, z_scr, ot_scr, k_scr, vt_scr, *, seq):
    mod = _mod_row(mod_ref, True)
    gate = mod[:, 2 * D_MODEL:]
    nw = nw_ref[...]
    qw = _head_weight_tile(qw_ref, Q_BLOCK)
    kw = _head_weight_tile(kw_ref, Q_BLOCK)
    n_blocks = seq // Q_BLOCK
    kv_blocks = seq // BLOCK
    k_scr[0:BLOCK, :] = jnp.zeros((BLOCK, KV_W), BF16)
    k_scr[BLOCK + seq:2 * BLOCK + seq, :] = jnp.zeros((BLOCK, KV_W), BF16)
    vt_scr[0] = jnp.zeros((KV_W, BLOCK), BF16)
    vt_scr[kv_blocks + 1] = jnp.zeros((KV_W, BLOCK), BF16)

    def project(c):
        rows = slice(c * Q_BLOCK, (c + 1) * Q_BLOCK)
        h = _mod_norm(x_ref[0, rows, :], nw, mod).astype(BF16)
        z_scr[rows, :] = _dot(h, wz_ref[...])
        qkvt = _dot_nt(wqkvt_ref[...], h)
        cos = cos_ref[:, rows]
        sin = sin_ref[:, rows]
        for hd in range(N_HEADS):
            hr = slice(hd * HEAD_DIM, (hd + 1) * HEAD_DIM)
            t = _rope_t(_head_rms(qkvt[hr], qw), cos, sin)
            t = (t * (HEAD_DIM ** -0.5 * LOG2E)).astype(BF16)
            for j in range(HALVES):
                q_scr[c * HALVES + j, hd // 2, :, (hd % 2) * BLOCK:(hd % 2 + 1) * BLOCK] = (
                    t[:, j * BLOCK:(j + 1) * BLOCK])
        knt = jnp.concatenate(
            [_rope_t(_head_rms(qkvt[BRANCH + g * HEAD_DIM:BRANCH + (g + 1) * HEAD_DIM], kw), cos, sin)
             for g in range(N_KV)], axis=0)
        k_scr[BLOCK + c * Q_BLOCK:BLOCK + (c + 1) * Q_BLOCK, :] = knt.T.astype(BF16)
        vt = qkvt[BRANCH + KV_W:].astype(BF16)
        for j in range(HALVES):
            vt_scr[1 + c * HALVES + j] = vt[:, j * BLOCK:(j + 1) * BLOCK]

    ckb = ck_ref[0].T.astype(BF16)
    cvt = cv_ref[0].astype(BF16)

    win_len = 3 * BLOCK
    n_pairs = N_HEADS // 2
    kj = lax.broadcasted_iota(jnp.int32, (win_len, 2 * BLOCK), 0)
    lane = lax.broadcasted_iota(jnp.int32, (win_len, 2 * BLOCK), 1)
    rel = kj - BLOCK - lane % BLOCK
    band = (rel >= -WINDOW) & (rel <= WINDOW)
    first_head = lax.broadcasted_iota(jnp.int32, (1, 2 * BLOCK), 1) < BLOCK

    def half_operands(hb):
        r0 = pl.multiple_of(hb * BLOCK, BLOCK)
        kpos = kj + (r0 - BLOCK)
        valid = jnp.where(band & (kpos >= 0) & (kpos < seq), 0.0, NEG_INF)
        kwin = k_scr[pl.ds(r0, win_len), :]
        vall = jnp.concatenate([vt_scr[hb + j] for j in range(win_len // BLOCK)] + [cvt], axis=1)
        valls = [_with_ones_rows(vall[g * HEAD_DIM:(g + 1) * HEAD_DIM]) for g in range(N_KV)]
        return valid, kwin, valls

    def pair_sink2(pair):
        return jnp.where(first_head, sink_ref[2 * pair], sink_ref[2 * pair + 1]) * LOG2E

    def attend(n, carry):
        halves = [half_operands(n * HALVES + j) for j in range(HALVES)]

        def scores_fn(u):
            j, pair = divmod(u, n_pairs)
            valid, kwin, _ = halves[j]
            return _head_scores(q_scr[n * HALVES + j, pair], pair // (GQA // 2), [kwin, ckb], [valid, None])

        def probs_fn(u, sc):
            return _head_probs(sc, pair_sink2(u % n_pairs))

        def pv_fn(u, pr):
            j, pair = divmod(u, n_pairs)
            o = _head_pv(pr, halves[j][2][pair // (GQA // 2)], pair_sink2(pair))
            for i in range(2):
                hd = 2 * pair + i
                ot_scr[n, hd * HEAD_DIM:(hd + 1) * HEAD_DIM, j * BLOCK:(j + 1) * BLOCK] = (
                    o[:, i * BLOCK:(i + 1) * BLOCK])

        _attend_heads(HALVES * n_pairs, GQA, scores_fn, probs_fn, pv_fn)
        return carry

    for c in range(n_blocks):
        project(c)
    lax.fori_loop(0, n_blocks, attend, 0)
    for c in range(n_blocks):
        rows = slice(c * Q_BLOCK, (c + 1) * Q_BLOCK)
        o_ref[0, rows, :] = _gate_out(x_ref[0, rows, :], ot_scr[c].T, z_scr[rows, :], gate, wout_ref[...])


def _rope_tables_t(seq):
    pos = np.arange(seq)
    n_freq = HEAD_DIM // 4
    inv = ROPE_THETA ** (-np.arange(n_freq, dtype=np.float64) / n_freq)
    ang = np.concatenate([(pos // GRID_W)[:, None] * inv, (pos % GRID_W)[:, None] * inv], axis=-1)
    return np.cos(ang).T.astype(np.float32), np.sin(ang).T.astype(np.float32)


def _attn_lat_layer(x, mod, nw, wqkvt, wz, wout, qw, kw, sink, ckt, cvt):
    nb, seq, _ = x.shape
    past = ckt.shape[2]
    cos, sin = (jnp.asarray(t) for t in _rope_tables_t(seq))
    kern = functools.partial(_attn_lat_kernel, seq=seq)
    n_blocks = seq // Q_BLOCK
    return pl.pallas_call(
        kern,
        grid=(nb,),
        in_specs=[
            pl.BlockSpec(memory_space=pltpu.SMEM),
            pl.BlockSpec((1, seq, D_MODEL), lambda b: (b, 0, 0)),
            MOD_SPEC,
            _const_spec((1, D_MODEL)),
            _const_spec((BRANCH + 2 * KV_W, D_MODEL)),
            _const_spec((D_MODEL, BRANCH)),
            _const_spec((BRANCH, D_MODEL)),
            _const_spec((1, HEAD_DIM)),
            _const_spec((1, HEAD_DIM)),
            _const_spec((HEAD_DIM // 2, seq)),
            _const_spec((HEAD_DIM // 2, seq)),
            pl.BlockSpec((1, KV_W, past), lambda b: (b, 0, 0)),
            pl.BlockSpec((1, KV_W, past), lambda b: (b, 0, 0)),
        ],
        out_specs=pl.BlockSpec((1, seq, D_MODEL), lambda b: (b, 0, 0)),
        out_shape=jax.ShapeDtypeStruct(x.shape, F32),
        scratch_shapes=[
            pltpu.VMEM((seq // BLOCK, N_HEADS // 2, HEAD_DIM, 2 * BLOCK), BF16),
            pltpu.VMEM((seq, BRANCH), F32),
            pltpu.VMEM((n_blocks, BRANCH, Q_BLOCK), F32),
            pltpu.VMEM((seq + 2 * BLOCK, KV_W), BF16),
            pltpu.VMEM((seq // BLOCK + 2, KV_W, BLOCK), BF16),
        ],
        compiler_params=pltpu.CompilerParams(
            dimension_semantics=("arbitrary",), vmem_limit_bytes=VMEM_LIMIT),
        name="attn_latent_layer",
    )(sink, x, mod, nw, wqkvt, wz, wout, qw, kw, cos, sin, ckt, cvt)


def kernel(x_prompt, x_sample, cache_k_l1, cache_v_l1, c, c_ctx, norm_w_l0, w_mod_l0, b_mod_l0,
           w_in_l0, w_out_l0, norm_w_l1, w_mod_l1, b_mod_l1, w_in_l1, q_norm_w_l1, k_norm_w_l1,
           sink_l1, w_out_l1):
    nb_ctx, seq_ctx, _ = x_prompt.shape
    nb_lat = x_sample.shape[0]
    past = cache_k_l1.shape[1]
    assert 1 + nb_lat <= MOD_ROWS
    mod0, win0, wout0 = _layer0_prep(c_ctx, c, w_mod_l0, b_mod_l0, w_in_l0, w_out_l0)

    nw0 = norm_w_l0.reshape(1, D_MODEL)
    nw1 = norm_w_l1.reshape(1, D_MODEL)
    qw = q_norm_w_l1.reshape(1, HEAD_DIM)
    kw = k_norm_w_l1.reshape(1, HEAD_DIM)

    xp, mod1, wqkvt1, wz1, wout1 = _fourier_layer(
        x_prompt, mod0, False, nw0, win0, wout0,
        next_layer=(c_ctx, c, w_mod_l1, b_mod_l1, w_in_l1, w_out_l1))
    xs = _fourier_layer(x_sample, mod0, True, nw0, win0, wout0)

    def to_feature_major(t):
        return jnp.transpose(t, (0, 2, 3, 1)).reshape(t.shape[0], KV_W, t.shape[1])

    def from_feature_major(t):
        return jnp.transpose(t.reshape(t.shape[0], N_KV, HEAD_DIM, t.shape[2]), (0, 3, 1, 2))

    xp, new_kt, new_vt = _attn_ctx_layer(xp, mod1, nw1, wqkvt1, wz1, wout1, qw, kw, sink_l1)
    xs = _attn_lat_layer(xs, mod1, nw1, wqkvt1, wz1, wout1, qw, kw, sink_l1,
                         to_feature_major(cache_k_l1), to_feature_major(cache_v_l1))
    return (xp, xs, from_feature_major(new_kt), from_feature_major(new_vt))
```

```python
import functools

import numpy as np
import jax
import jax.numpy as jnp
from jax import lax
from jax.experimental import pallas as pl
from jax.experimental.pallas import tpu as pltpu

D_MODEL = 1024
BRANCH = 1024
N_GROUPS = 4
GROUP_W = BRANCH // N_GROUPS
HALF_W = GROUP_W // 2
HEAD_DIM = 64
N_HEADS = 16
N_KV = 4
GQA = N_HEADS // N_KV
KV_W = N_KV * HEAD_DIM
GRID_W = 64
WINDOW = 128
BLOCK = 128
ROPE_THETA = 10000.0
EPS = 1e-6
NEG_INF = -1e30
LANES = 128
ROW_CHUNK = 256
Q_BLOCK = 256
HALVES = Q_BLOCK // BLOCK
CTX_SEQS_PER_STEP = 2
VMEM_LIMIT = 56 * 1024 * 1024
MOD_ROWS = 8
MOD_K_CHUNK = 256
ONES_ROWS = 16
LOG2E = float(np.log2(np.e))

F32 = jnp.float32
BF16 = jnp.bfloat16


def _dot(a, b):
    return jnp.dot(a, b, preferred_element_type=F32)


def _dot_nt(a, b):
    return lax.dot_general(a, b, (((1,), (1,)), ((), ())), preferred_element_type=F32)


def _mod_row(mod_ref, per_request):
    if per_request:
        return mod_ref[pl.ds(1 + pl.program_id(0), 1), :]
    return mod_ref[0:1, :]


MOD_SPEC = pl.BlockSpec((MOD_ROWS, 3 * D_MODEL), lambda b: (0, 0))


def _mod_norm(x, nw, mod):
    shift = mod[:, :D_MODEL]
    scale = mod[:, D_MODEL:2 * D_MODEL]
    y = x * lax.rsqrt(jnp.mean(x * x, axis=-1, keepdims=True) + EPS)
    return (y * nw) * (1.0 + scale) + shift


def _mod_accumulate(cctx_ref, c_ref, w_ref, b_ref, o_ref, cond_scr):
    n_lat = c_ref.shape[0]
    cond_scr[...] = jnp.zeros_like(cond_scr)
    cond_scr[0:1, :] = cctx_ref[...]
    cond_scr[1:1 + n_lat, :] = c_ref[...]
    s = jax.nn.silu(cond_scr[...]).astype(BF16)

    @pl.when(pl.program_id(0) == 0)
    def _():
        o_ref[...] = jnp.broadcast_to(b_ref[...], o_ref.shape)

    o_ref[...] += _dot(s, w_ref[...].astype(BF16))


def _mod_in_specs(n_lat, rows):
    return [pl.BlockSpec((1, rows), lambda k: (0, k)),
            pl.BlockSpec((n_lat, rows), lambda k: (0, k)),
            pl.BlockSpec((rows, 3 * D_MODEL), lambda k: (k, 0)),
            pl.BlockSpec((1, 3 * D_MODEL), lambda k: (0, 0))]


def _mirror_perm():
    j = np.arange(GROUP_W)
    return np.where(j <= HALF_W, j, GROUP_W + HALF_W - j)


def _prep_kernel(cctx_ref, c_ref, wmod_ref, bmod_ref, win_ref, wout_ref, pm_ref,
                 mod_ref, winb_ref, woutb_ref, cond_scr):
    _mod_accumulate(cctx_ref, c_ref, wmod_ref, bmod_ref, mod_ref, cond_scr)
    w = win_ref[...].astype(BF16)
    winb_ref[:, :BRANCH] = w[:, :BRANCH]
    pm = pm_ref[...]
    for g in range(N_GROUPS):
        cols = slice(BRANCH + g * GROUP_W, BRANCH + (g + 1) * GROUP_W)
        winb_ref[:, cols] = _dot(w[:, cols], pm).astype(BF16)
    woutb_ref[...] = _dot(pm, wout_ref[...].astype(BF16)).astype(BF16)


def _layer0_prep(c_ctx, c, w_mod, b_mod, w_in, w_out):
    n_lat = c.shape[0]
    rows = MOD_K_CHUNK
    assert rows == GROUP_W
    perm = _mirror_perm()
    pm = jnp.asarray((np.arange(GROUP_W)[:, None] == perm[None, :]).astype(np.float32)).astype(BF16)
    return pl.pallas_call(
        _prep_kernel,
        grid=(D_MODEL // rows,),
        in_specs=_mod_in_specs(n_lat, rows) + [
            pl.BlockSpec((rows, w_in.shape[1]), lambda k: (k, 0)),
            pl.BlockSpec((rows, w_out.shape[1]), lambda k: (k, 0)),
            pl.BlockSpec((GROUP_W, GROUP_W), lambda k: (0, 0)),
        ],
        out_specs=[
            pl.BlockSpec((MOD_ROWS, 3 * D_MODEL), lambda k: (0, 0)),
            pl.BlockSpec((rows, w_in.shape[1]), lambda k: (k, 0)),
            pl.BlockSpec((rows, w_out.shape[1]), lambda k: (k, 0)),
        ],
        out_shape=[
            jax.ShapeDtypeStruct((MOD_ROWS, 3 * D_MODEL), F32),
            jax.ShapeDtypeStruct(w_in.shape, BF16),
            jax.ShapeDtypeStruct(w_out.shape, BF16),
        ],
        scratch_shapes=[pltpu.VMEM((MOD_ROWS, rows), F32)],
        compiler_params=pltpu.CompilerParams(dimension_semantics=("arbitrary",)),
        name="layer0_prep",
    )(c_ctx.reshape(1, D_MODEL), c, w_mod, b_mod.reshape(1, 3 * D_MODEL), w_in, w_out, pm)


def _fourier_layer_kernel(xc_ref, xl_ref, mod_ref, nw_ref, win_ref, wout_ref, m1_ref,
                          csc_ref, ssc_ref, csl_ref, ssl_ref,
                          cctx_ref, c_ref, wmod1_ref, bmod1_ref, win1_ref, wout1_ref,
                          oc_ref, ol_ref, mod1_ref, wqkvt1_ref, wz1_ref, wout1b_ref,
                          ta_scr, tb_scr, tr_scr, z_scr, cond_scr, *, n_ctx_steps):
    step = pl.program_id(0)

    @pl.when(step < n_ctx_steps)
    def _():
        _mod_accumulate(cctx_ref, c_ref, wmod1_ref, bmod1_ref, mod1_ref, cond_scr)
        w = win1_ref[...]
        wqkvt1_ref[...] = w[:, :BRANCH + 2 * KV_W].T.astype(BF16)
        wz1_ref[...] = w[:, BRANCH + 2 * KV_W:].astype(BF16)
        wout1b_ref[...] = wout1_ref[...].astype(BF16)
        _fourier_body(xc_ref, oc_ref, mod_ref[0:1, :], nw_ref, win_ref, wout_ref, m1_ref, csc_ref, ssc_ref,
                      ta_scr, tb_scr, tr_scr, z_scr)

    @pl.when(step >= n_ctx_steps)
    def _():
        mod = mod_ref[pl.ds(1 + step - n_ctx_steps, 1), :]
        _fourier_body(xl_ref, ol_ref, mod, nw_ref, win_ref, wout_ref, m1_ref, csl_ref, ssl_ref,
                      ta_scr, tb_scr, tr_scr, z_scr)


def _fourier_body(x_ref, o_ref, mod, nw_ref, win_ref, wout_ref, m1_ref, cs_ref, ss_ref,
                  ta_scr, tb_scr, tr_scr, z_scr):
    gate = mod[:, 2 * D_MODEL:]
    nw = nw_ref[...]
    n_seq, seq, _ = x_ref.shape
    n_chunks = seq // ROW_CHUNK
    lane = lax.broadcasted_iota(jnp.int32, (ROW_CHUNK, HALF_W), 1)
    for i in range(n_seq):
        for c in range(n_chunks):
            rows = slice(c * ROW_CHUNK, (c + 1) * ROW_CHUNK)
            srows = slice(i * seq + c * ROW_CHUNK, i * seq + (c + 1) * ROW_CHUNK)
            h = _mod_norm(x_ref[i, rows, :], nw, mod).astype(BF16)
            uz = _dot(h, win_ref[...])
            z_scr[srows, :] = uz[:, BRANCH:]
            u = uz[:, :BRANCH].astype(BF16)
            tr = jnp.zeros((ROW_CHUNK, HALF_W), F32)
            for g in range(N_GROUPS):
                t = _dot(u[:, g * GROUP_W:(g + 1) * GROUP_W], m1_ref[...])
                half = slice(g * HALF_W, (g + 1) * HALF_W)
                ta_scr[srows, half] = t[:, :HALF_W].astype(BF16)
                tb = t[:, HALF_W:]
                tb_scr[srows, half] = tb.astype(BF16)
                tr = jnp.where(lane == g, tb if g == 0 else pltpu.roll(tb, g, axis=1), tr)
            tr_scr[srows, :] = tr.astype(BF16)
    for i in range(n_seq):
        seq_rows = slice(i * seq, (i + 1) * seq)
        for c in range(n_chunks):
            rows = slice(c * ROW_CHUNK, (c + 1) * ROW_CHUNK)
            srows = slice(i * seq + c * ROW_CHUNK, i * seq + (c + 1) * ROW_CHUNK)
            cs = cs_ref[rows, :]
            p = _dot(cs, ta_scr[seq_rows, :])
            q = _dot(ss_ref[rows, :], tb_scr[seq_rows, :])
            r = _dot(cs, tr_scr[seq_rows, :])
            parts = []
            for g in range(N_GROUPS):
                half = slice(g * HALF_W, (g + 1) * HALF_W)
                pg, qg = p[:, half], q[:, half]
                rg = r if g == 0 else pltpu.roll(r, HALF_W - g, axis=1)
                parts.append(jnp.where(lane == 0, pg, pg - qg))
                parts.append(jnp.where(lane == 0, rg, pg + qg))
            y = jnp.concatenate(parts, axis=1)
            y = (y * jax.nn.silu(z_scr[srows, :])).astype(BF16)
            o_ref[i, rows, :] = x_ref[i, rows, :] + gate * _dot(y, wout_ref[...])


def _dft_tables(seq):
    c = np.arange(GROUP_W)[:, None]
    k = np.arange(HALF_W)[None, :]
    cos_lo = np.cos(2.0 * np.pi * ((c * k) % GROUP_W) / GROUP_W)
    sin_lo = np.sin(2.0 * np.pi * ((c * k) % GROUP_W) / GROUP_W)
    sin_lo[:, 0] = np.cos(np.pi * c[:, 0])
    m1 = np.concatenate([cos_lo, sin_lo], axis=1) / np.sqrt(GROUP_W)
    n = np.arange(seq)
    ang = 2.0 * np.pi * ((n[:, None] * n[None, :]) % seq) / seq
    cs = np.cos(ang) / np.sqrt(seq)
    ss = np.sin(ang) / np.sqrt(seq)
    return m1.astype(np.float32), cs.astype(np.float32), ss.astype(np.float32)


def _const_spec(shape):
    return pl.BlockSpec(shape, lambda b: (0,) * len(shape))


def _resident_spec(shape):
    return pl.BlockSpec(shape, lambda b: (0,) * len(shape), pipeline_mode=pl.Buffered(1))


def _fourier_layer(x_ctx, x_lat, mod, nw, win, wout, next_layer):
    nb_ctx, seq_ctx, _ = x_ctx.shape
    nb_lat, seq_lat, _ = x_lat.shape
    n_seq = CTX_SEQS_PER_STEP
    assert nb_ctx % n_seq == 0
    n_ctx = nb_ctx // n_seq
    assert D_MODEL % (n_ctx * LANES) == 0 and n_seq * seq_ctx <= seq_lat
    rows = D_MODEL // n_ctx
    c_ctx, c, w_mod1, b_mod1, w_in1, w_out1 = next_layer
    n_qkvz = 2 * BRANCH + 2 * KV_W
    m1, csc, ssc = (jnp.asarray(t).astype(BF16) for t in _dft_tables(seq_ctx))
    _, csl, ssl = (jnp.asarray(t).astype(BF16) for t in _dft_tables(seq_lat))

    def ctx_step(i):
        return jnp.minimum(i, n_ctx - 1)

    def lat_step(i):
        return jnp.maximum(i - n_ctx, 0)

    in_specs = [
        pl.BlockSpec((n_seq, seq_ctx, D_MODEL), lambda i: (ctx_step(i), 0, 0)),
        pl.BlockSpec((1, seq_lat, D_MODEL), lambda i: (lat_step(i), 0, 0)),
        MOD_SPEC,
        _const_spec((1, D_MODEL)),
        _resident_spec((D_MODEL, 2 * BRANCH)),
        _resident_spec((BRANCH, D_MODEL)),
        _const_spec((GROUP_W, GROUP_W)),
        _const_spec((seq_ctx, seq_ctx)),
        _const_spec((seq_ctx, seq_ctx)),
        _resident_spec((seq_lat, seq_lat)),
        _resident_spec((seq_lat, seq_lat)),
        pl.BlockSpec((1, rows), lambda i: (0, ctx_step(i))),
        pl.BlockSpec((c.shape[0], rows), lambda i: (0, ctx_step(i))),
        pl.BlockSpec((rows, 3 * D_MODEL), lambda i: (ctx_step(i), 0)),
        _const_spec((1, 3 * D_MODEL)),
        pl.BlockSpec((rows, n_qkvz), lambda i: (ctx_step(i), 0)),
        pl.BlockSpec((rows, D_MODEL), lambda i: (ctx_step(i), 0)),
    ]
    out_specs = [
        pl.BlockSpec((n_seq, seq_ctx, D_MODEL), lambda i: (ctx_step(i), 0, 0)),
        pl.BlockSpec((1, seq_lat, D_MODEL), lambda i: (lat_step(i), 0, 0)),
        MOD_SPEC,
        pl.BlockSpec((BRANCH + 2 * KV_W, rows), lambda i: (0, ctx_step(i))),
        pl.BlockSpec((rows, BRANCH), lambda i: (ctx_step(i), 0)),
        pl.BlockSpec((rows, D_MODEL), lambda i: (ctx_step(i), 0)),
    ]
    out_shape = [
        jax.ShapeDtypeStruct(x_ctx.shape, F32),
        jax.ShapeDtypeStruct(x_lat.shape, F32),
        jax.ShapeDtypeStruct((MOD_ROWS, 3 * D_MODEL), F32),
        jax.ShapeDtypeStruct((BRANCH + 2 * KV_W, D_MODEL), BF16),
        jax.ShapeDtypeStruct((D_MODEL, BRANCH), BF16),
        jax.ShapeDtypeStruct((BRANCH, D_MODEL), BF16),
    ]
    return pl.pallas_call(
        functools.partial(_fourier_layer_kernel, n_ctx_steps=n_ctx),
        grid=(n_ctx + nb_lat,),
        in_specs=in_specs,
        out_specs=out_specs,
        out_shape=out_shape,
        scratch_shapes=[
            pltpu.VMEM((seq_lat, N_GROUPS * HALF_W), BF16),
            pltpu.VMEM((seq_lat, N_GROUPS * HALF_W), BF16),
            pltpu.VMEM((seq_lat, HALF_W), BF16),
            pltpu.VMEM((seq_lat, BRANCH), F32),
            pltpu.VMEM((MOD_ROWS, rows), F32),
        ],
        compiler_params=pltpu.CompilerParams(
            dimension_semantics=("arbitrary",), vmem_limit_bytes=VMEM_LIMIT),
        name="fourier_layer",
    )(x_ctx, x_lat, mod, nw, win, wout, m1, csc, ssc, csl, ssl,
      c_ctx.reshape(1, D_MODEL), c, w_mod1, b_mod1.reshape(1, 3 * D_MODEL), w_in1, w_out1)


def _head_weight_tile(w_ref, n_tokens):
    row = jnp.broadcast_to(w_ref[...], (HEAD_DIM, HEAD_DIM))
    ii = lax.broadcasted_iota(jnp.int32, (HEAD_DIM, HEAD_DIM), 0)
    jj = lax.broadcasted_iota(jnp.int32, (HEAD_DIM, HEAD_DIM), 1)
    col = jnp.sum(jnp.where(ii == jj, row, 0.0), axis=1, keepdims=True)
    return jnp.broadcast_to(col, (HEAD_DIM, n_tokens))


def _head_rms(t, w):
    return (t * lax.rsqrt(jnp.mean(t * t, axis=0, keepdims=True) + EPS)) * w


def _rope_t(t, cos, sin):
    half = HEAD_DIM // 2
    x1, x2 = t[:half], t[half:]
    return jnp.concatenate([x1 * cos - x2 * sin, x1 * sin + x2 * cos], axis=0)


def _head_scores(qn, g, keys, biases):
    zeros = jnp.zeros_like(qn)
    qz = jnp.concatenate([qn, zeros] if g % 2 == 0 else [zeros, qn], axis=0)
    blk = slice((g // 2) * LANES, (g // 2 + 1) * LANES)
    scores = []
    smax = None
    for k, bias in zip(keys, biases):
        s = _dot(k[:, blk], qz)
        s = s if bias is None else s + bias
        cmax = jnp.max(s, axis=0, keepdims=True)
        smax = cmax if smax is None else jnp.maximum(smax, cmax)
        scores.append(s)
    return scores, smax


def _with_ones_rows(vt):
    return jnp.concatenate([vt, jnp.ones((ONES_ROWS, vt.shape[1]), vt.dtype)], axis=0)


def _head_probs(scored, sink2):
    scores, smax = scored
    m = jnp.maximum(smax, sink2)
    return jnp.concatenate([jnp.exp2(s - m).astype(BF16) for s in scores], axis=0), m


def _head_pv(probs, values_t, sink2):
    p, m = probs
    acc = _dot(values_t, p)
    den = acc[HEAD_DIM:HEAD_DIM + 1] + jnp.exp2(sink2 - m)
    return acc[:HEAD_DIM] * (1.0 / den)


def _attend_heads(n_units, stage, scores_fn, probs_fn, pv_fn, fillers=()):
    n_stages = n_units // stage
    pending = [scores_fn(u) for u in range(stage)]
    for g in range(n_stages):
        units = range(g * stage, (g + 1) * stage)
        if g < len(fillers):
            fillers[g]()
        nxt = [scores_fn(u) for u in range((g + 1) * stage, (g + 2) * stage)] if g + 1 < n_stages else None
        probs = [probs_fn(u, sc) for u, sc in zip(units, pending)]
        for u, pr in zip(units, probs):
            pv_fn(u, pr)
        pending = nxt


def _gate_out(x, o, z, gate, wout):
    y = (o * jax.nn.silu(z)).astype(BF16)
    return x + gate * _dot(y, wout)


def _attn_ctx_kernel(sink_ref, x_ref, mod_ref, nw_ref, wqkvt_ref, wz_ref, wout_ref, qw_ref, kw_ref,
                     o_ref, kto_ref, vto_ref, qkvt_scr, z_scr, ot_scr):
    seq = x_ref.shape[1]
    n_seq = x_ref.shape[0]
    mod = _mod_row(mod_ref, False)
    gate = mod[:, 2 * D_MODEL:]
    kw = _head_weight_tile(kw_ref, seq)
    qw = _head_weight_tile(qw_ref, seq)

    def project(i):
        h = _mod_norm(x_ref[i], nw_ref[...], mod).astype(BF16)
        qkvt_scr[i] = _dot_nt(wqkvt_ref[...], h)
        z_scr[i] = _dot(h, wz_ref[...])

    def keys_values(i):
        knt = jnp.concatenate(
            [_head_rms(qkvt_scr[i, BRANCH + g * HEAD_DIM:BRANCH + (g + 1) * HEAD_DIM, :], kw)
             for g in range(N_KV)], axis=0)
        kto_ref[i] = knt
        vtf = qkvt_scr[i, BRANCH + KV_W:, :]
        vto_ref[i] = vtf
        vt = vtf.astype(BF16)
        return (knt.T.astype(BF16),
                [_with_ones_rows(vt[g * HEAD_DIM:(g + 1) * HEAD_DIM]) for g in range(N_KV)])

    def output(i):
        o_ref[i] = _gate_out(x_ref[i], ot_scr[i].T, z_scr[i], gate, wout_ref[...])

    for i in range(n_seq):
        project(i)
    kv = [keys_values(i) for i in range(n_seq)]

    def scores_fn(u):
        i, hd = divmod(u, N_HEADS)
        t = qkvt_scr[i, hd * HEAD_DIM:(hd + 1) * HEAD_DIM, :]
        qn = (_head_rms(t, qw) * (HEAD_DIM ** -0.5 * LOG2E)).astype(BF16)
        return _head_scores(qn, hd // GQA, [kv[i][0]], [None])

    def probs_fn(u, sc):
        return _head_probs(sc, sink_ref[u % N_HEADS] * LOG2E)

    def pv_fn(u, pr):
        i, hd = divmod(u, N_HEADS)
        ot_scr[i, hd * HEAD_DIM:(hd + 1) * HEAD_DIM, :] = _head_pv(
            pr, kv[i][1][hd // GQA], sink_ref[hd] * LOG2E)

    fillers = [lambda: None] + [functools.partial(output, i) for i in range(n_seq - 1)]
    _attend_heads(n_seq * N_HEADS, N_HEADS, scores_fn, probs_fn, pv_fn, fillers)
    output(n_seq - 1)


def _attn_ctx_layer(x, mod, nw, wqkvt, wz, wout, qw, kw, sink):
    nb, seq, _ = x.shape
    n_seq = CTX_SEQS_PER_STEP
    assert nb % n_seq == 0
    return pl.pallas_call(
        _attn_ctx_kernel,
        grid=(nb // n_seq,),
        in_specs=[
            pl.BlockSpec(memory_space=pltpu.SMEM),
            pl.BlockSpec((n_seq, seq, D_MODEL), lambda b: (b, 0, 0)),
            MOD_SPEC,
            _const_spec((1, D_MODEL)),
            _const_spec((BRANCH + 2 * KV_W, D_MODEL)),
            _const_spec((D_MODEL, BRANCH)),
            _const_spec((BRANCH, D_MODEL)),
            _const_spec((1, HEAD_DIM)),
            _const_spec((1, HEAD_DIM)),
        ],
        out_specs=[
            pl.BlockSpec((n_seq, seq, D_MODEL), lambda b: (b, 0, 0)),
            pl.BlockSpec((n_seq, KV_W, seq), lambda b: (b, 0, 0)),
            pl.BlockSpec((n_seq, KV_W, seq), lambda b: (b, 0, 0)),
        ],
        out_shape=[
            jax.ShapeDtypeStruct(x.shape, F32),
            jax.ShapeDtypeStruct((nb, KV_W, seq), F32),
            jax.ShapeDtypeStruct((nb, KV_W, seq), F32),
        ],
        scratch_shapes=[
            pltpu.VMEM((n_seq, BRANCH + 2 * KV_W, seq), F32),
            pltpu.VMEM((n_seq, seq, BRANCH), F32),
            pltpu.VMEM((n_seq, BRANCH, seq), F32),
        ],
        compiler_params=pltpu.CompilerParams(
            dimension_semantics=("arbitrary",), vmem_limit_bytes=VMEM_LIMIT),
        name="attn_context_layer",
    )(sink, x, mod, nw, wqkvt, wz, wout, qw, kw)


def _attn_lat_kernel(sink_ref, x_ref, mod_ref, nw_ref, wqkvt_ref, wz_ref, wout_ref, qw_ref, kw_ref,
                     cos_ref, sin_ref, ck_ref, cv_ref, o_ref,
                     q_scr, z_scr, ot_scr, k_scr, vt_scr, *, seq):
    mod = _mod_row(mod_ref, True)
    gate = mod[:, 2 * D_MODEL:]
    nw = nw_ref[...]
    qw = _head_weight_tile(qw_ref, Q_BLOCK)
    kw = _head_weight_tile(kw_ref, Q_BLOCK)
    n_blocks = seq // Q_BLOCK
    kv_blocks = seq // BLOCK
    k_scr[0:BLOCK, :] = jnp.zeros((BLOCK, KV_W), BF16)
    k_scr[BLOCK + seq:2 * BLOCK + seq, :] = jnp.zeros((BLOCK, KV_W), BF16)
    vt_scr[0] = jnp.zeros((KV_W, BLOCK), BF16)
    vt_scr[kv_blocks + 1] = jnp.zeros((KV_W, BLOCK), BF16)

    def project(c):
        rows = slice(c * Q_BLOCK, (c + 1) * Q_BLOCK)
        h = _mod_norm(x_ref[0, rows, :], nw, mod).astype(BF16)
        z_scr[rows, :] = _dot(h, wz_ref[...])
        qkvt = _dot_nt(wqkvt_ref[...], h)
        cos = cos_ref[:, rows]
        sin = sin_ref[:, rows]
        for hd in range(N_HEADS):
            hr = slice(hd * HEAD_DIM, (hd + 1) * HEAD_DIM)
            t = _rope_t(_head_rms(qkvt[hr], qw), cos, sin)
            t = (t * (HEAD_DIM ** -0.5 * LOG2E)).astype(BF16)
            for j in range(HALVES):
                q_scr[c * HALVES + j, hd // 2, :, (hd % 2) * BLOCK:(hd % 2 + 1) * BLOCK] = (
                    t[:, j * BLOCK:(j + 1) * BLOCK])
        knt = jnp.concatenate(
            [_rope_t(_head_rms(qkvt[BRANCH + g * HEAD_DIM:BRANCH + (g + 1) * HEAD_DIM], kw), cos, sin)
             for g in range(N_KV)], axis=0)
        k_scr[BLOCK + c * Q_BLOCK:BLOCK + (c + 1) * Q_BLOCK, :] = knt.T.astype(BF16)
        vt = qkvt[BRANCH + KV_W:].astype(BF16)
        for j in range(HALVES):
            vt_scr[1 + c * HALVES + j] = vt[:, j * BLOCK:(j + 1) * BLOCK]

    ckb = ck_ref[0].T.astype(BF16)
    cvt = cv_ref[0].astype(BF16)

    win_len = 3 * BLOCK
    n_pairs = N_HEADS // 2
    kj = lax.broadcasted_iota(jnp.int32, (win_len, 2 * BLOCK), 0)
    lane = lax.broadcasted_iota(jnp.int32, (win_len, 2 * BLOCK), 1)
    rel = kj - BLOCK - lane % BLOCK
    band = (rel >= -WINDOW) & (rel <= WINDOW)
    first_head = lax.broadcasted_iota(jnp.int32, (1, 2 * BLOCK), 1) < BLOCK

    def half_operands(hb):
        r0 = pl.multiple_of(hb * BLOCK, BLOCK)
        kpos = kj + (r0 - BLOCK)
        valid = jnp.where(band & (kpos >= 0) & (kpos < seq), 0.0, NEG_INF)
        kwin = k_scr[pl.ds(r0, win_len), :]
        vall = jnp.concatenate([vt_scr[hb + j] for j in range(win_len // BLOCK)] + [cvt], axis=1)
        valls = [_with_ones_rows(vall[g * HEAD_DIM:(g + 1) * HEAD_DIM]) for g in range(N_KV)]
        return valid, kwin, valls

    def pair_sink2(pair):
        return jnp.where(first_head, sink_ref[2 * pair], sink_ref[2 * pair + 1]) * LOG2E

    def attend(n, carry):
        halves = [half_operands(n * HALVES + j) for j in range(HALVES)]

        def scores_fn(u):
            j, pair = divmod(u, n_pairs)
            valid, kwin, _ = halves[j]
            return _head_scores(q_scr[n * HALVES + j, pair], pair // (GQA // 2), [kwin, ckb], [valid, None])

        def probs_fn(u, sc):
            return _head_probs(sc, pair_sink2(u % n_pairs))

        def pv_fn(u, pr):
            j, pair = divmod(u, n_pairs)
            o = _head_pv(pr, halves[j][2][pair // (GQA // 2)], pair_sink2(pair))
            for i in range(2):
                hd = 2 * pair + i
                ot_scr[n, hd * HEAD_DIM:(hd + 1) * HEAD_DIM, j * BLOCK:(j + 1) * BLOCK] = (
                    o[:, i * BLOCK:(i + 1) * BLOCK])

        _attend_heads(HALVES * n_pairs, GQA, scores_fn, probs_fn, pv_fn)
        return carry

    for c in range(n_blocks):
        project(c)
    lax.fori_loop(0, n_blocks, attend, 0)
    for c in range(n_blocks):
        rows = slice(c * Q_BLOCK, (c + 1) * Q_BLOCK)
        o_ref[0, rows, :] = _gate_out(x_ref[0, rows, :], ot_scr[c].T, z_scr[rows, :], gate, wout_ref[...])


def _rope_tables_t(seq):
    pos = np.arange(seq)
    n_freq = HEAD_DIM // 4
    inv = ROPE_THETA ** (-np.arange(n_freq, dtype=np.float64) / n_freq)
    ang = np.concatenate([(pos // GRID_W)[:, None] * inv, (pos % GRID_W)[:, None] * inv], axis=-1)
    return np.cos(ang).T.astype(np.float32), np.sin(ang).T.astype(np.float32)


def _attn_lat_layer(x, mod, nw, wqkvt, wz, wout, qw, kw, sink, ckt, cvt):
    nb, seq, _ = x.shape
    past = ckt.shape[2]
    cos, sin = (jnp.asarray(t) for t in _rope_tables_t(seq))
    kern = functools.partial(_attn_lat_kernel, seq=seq)
    n_blocks = seq // Q_BLOCK
    return pl.pallas_call(
        kern,
        grid=(nb,),
        in_specs=[
            pl.BlockSpec(memory_space=pltpu.SMEM),
            pl.BlockSpec((1, seq, D_MODEL), lambda b: (b, 0, 0)),
            MOD_SPEC,
            _const_spec((1, D_MODEL)),
            _const_spec((BRANCH + 2 * KV_W, D_MODEL)),
            _const_spec((D_MODEL, BRANCH)),
            _const_spec((BRANCH, D_MODEL)),
            _const_spec((1, HEAD_DIM)),
            _const_spec((1, HEAD_DIM)),
            _const_spec((HEAD_DIM // 2, seq)),
            _const_spec((HEAD_DIM // 2, seq)),
            pl.BlockSpec((1, KV_W, past), lambda b: (b, 0, 0)),
            pl.BlockSpec((1, KV_W, past), lambda b: (b, 0, 0)),
        ],
        out_specs=pl.BlockSpec((1, seq, D_MODEL), lambda b: (b, 0, 0)),
        out_shape=jax.ShapeDtypeStruct(x.shape, F32),
        scratch_shapes=[
            pltpu.VMEM((seq // BLOCK, N_HEADS // 2, HEAD_DIM, 2 * BLOCK), BF16),
            pltpu.VMEM((seq, BRANCH), F32),
            pltpu.VMEM((n_blocks, BRANCH, Q_BLOCK), F32),
            pltpu.VMEM((seq + 2 * BLOCK, KV_W), BF16),
            pltpu.VMEM((seq // BLOCK + 2, KV_W, BLOCK), BF16),
        ],
        compiler_params=pltpu.CompilerParams(
            dimension_semantics=("arbitrary",), vmem_limit_bytes=VMEM_LIMIT),
        name="attn_latent_layer",
    )(sink, x, mod, nw, wqkvt, wz, wout, qw, kw, cos, sin, ckt, cvt)


def kernel(x_prompt, x_sample, cache_k_l1, cache_v_l1, c, c_ctx, norm_w_l0, w_mod_l0, b_mod_l0,
           w_in_l0, w_out_l0, norm_w_l1, w_mod_l1, b_mod_l1, w_in_l1, q_norm_w_l1, k_norm_w_l1,
           sink_l1, w_out_l1):
    nb_ctx, seq_ctx, _ = x_prompt.shape
    nb_lat = x_sample.shape[0]
    past = cache_k_l1.shape[1]
    assert 1 + nb_lat <= MOD_ROWS
    mod0, win0, wout0 = _layer0_prep(c_ctx, c, w_mod_l0, b_mod_l0, w_in_l0, w_out_l0)

    nw0 = norm_w_l0.reshape(1, D_MODEL)
    nw1 = norm_w_l1.reshape(1, D_MODEL)
    qw = q_norm_w_l1.reshape(1, HEAD_DIM)
    kw = k_norm_w_l1.reshape(1, HEAD_DIM)

    xp, xs, mod1, wqkvt1, wz1, wout1 = _fourier_layer(
        x_prompt, x_sample, mod0, nw0, win0, wout0,
        next_layer=(c_ctx, c, w_mod_l1, b_mod_l1, w_in_l1, w_out_l1))

    def to_feature_major(t):
        return jnp.transpose(t, (0, 2, 3, 1)).reshape(t.shape[0], KV_W, t.shape[1])

    def from_feature_major(t):
        return jnp.transpose(t.reshape(t.shape[0], N_KV, HEAD_DIM, t.shape[2]), (0, 3, 1, 2))

    xp, new_kt, new_vt = _attn_ctx_layer(xp, mod1, nw1, wqkvt1, wz1, wout1, qw, kw, sink_l1)
    xs = _attn_lat_layer(xs, mod1, nw1, wqkvt1, wz1, wout1, qw, kw, sink_l1,
                         to_feature_major(cache_k_l1), to_feature_major(cache_v_l1))
    return (xp, xs, from_feature_major(new_kt), from_feature_major(new_vt))
```

```python
import functools

import numpy as np
import jax
import jax.numpy as jnp
from jax import lax
from jax.experimental import pallas as pl
from jax.experimental.pallas import tpu as pltpu

D_MODEL = 1024
BRANCH = 1024
N_GROUPS = 4
GROUP_W = BRANCH // N_GROUPS
HALF_W = GROUP_W // 2
HEAD_DIM = 64
N_HEADS = 16
N_KV = 4
GQA = N_HEADS // N_KV
KV_W = N_KV * HEAD_DIM
GRID_W = 64
WINDOW = 128
BLOCK = 128
ROPE_THETA = 10000.0
EPS = 1e-6
NEG_INF = -1e30
LANES = 128
ROW_CHUNK = 256
Q_BLOCK = 256
HALVES = Q_BLOCK // BLOCK
CTX_SEQS_PER_STEP = 2
VMEM_LIMIT = 56 * 1024 * 1024
MOD_ROWS = 8
MOD_K_CHUNK = 256
ONES_ROWS = 16
LOG2E = float(np.log2(np.e))

F32 = jnp.float32
BF16 = jnp.bfloat16


def _dot(a, b):
    return jnp.dot(a, b, preferred_element_type=F32)


def _dot_nt(a, b):
    return lax.dot_general(a, b, (((1,), (1,)), ((), ())), preferred_element_type=F32)


MOD_SPEC = pl.BlockSpec((MOD_ROWS, 3 * D_MODEL), lambda b: (0, 0))


def _mod_norm(x, nw, mod):
    shift = mod[:, :D_MODEL]
    scale = mod[:, D_MODEL:2 * D_MODEL]
    y = x * lax.rsqrt(jnp.mean(x * x, axis=-1, keepdims=True) + EPS)
    return (y * nw) * (1.0 + scale) + shift


def _mod_accumulate(cctx_ref, c_ref, w_ref, b_ref, o_ref, cond_scr):
    n_lat = c_ref.shape[0]
    cond_scr[...] = jnp.zeros_like(cond_scr)
    cond_scr[0:1, :] = cctx_ref[...]
    cond_scr[1:1 + n_lat, :] = c_ref[...]
    s = jax.nn.silu(cond_scr[...]).astype(BF16)

    @pl.when(pl.program_id(0) == 0)
    def _():
        o_ref[...] = jnp.broadcast_to(b_ref[...], o_ref.shape)

    o_ref[...] += _dot(s, w_ref[...].astype(BF16))


def _mod_in_specs(n_lat, rows):
    return [pl.BlockSpec((1, rows), lambda k: (0, k)),
            pl.BlockSpec((n_lat, rows), lambda k: (0, k)),
            pl.BlockSpec((rows, 3 * D_MODEL), lambda k: (k, 0)),
            pl.BlockSpec((1, 3 * D_MODEL), lambda k: (0, 0))]


def _mirror_perm():
    j = np.arange(GROUP_W)
    return np.where(j <= HALF_W, j, GROUP_W + HALF_W - j)


def _prep_kernel(cctx_ref, c_ref, wmod_ref, bmod_ref, win_ref, wout_ref, pm_ref,
                 mod_ref, winb_ref, woutb_ref, cond_scr):
    _mod_accumulate(cctx_ref, c_ref, wmod_ref, bmod_ref, mod_ref, cond_scr)
    w = win_ref[...].astype(BF16)
    winb_ref[:, :BRANCH] = w[:, :BRANCH]
    pm = pm_ref[...]
    for g in range(N_GROUPS):
        cols = slice(BRANCH + g * GROUP_W, BRANCH + (g + 1) * GROUP_W)
        winb_ref[:, cols] = _dot(w[:, cols], pm).astype(BF16)
    woutb_ref[...] = _dot(pm, wout_ref[...].astype(BF16)).astype(BF16)


def _layer0_prep(c_ctx, c, w_mod, b_mod, w_in, w_out):
    n_lat = c.shape[0]
    rows = MOD_K_CHUNK
    assert rows == GROUP_W
    perm = _mirror_perm()
    pm = jnp.asarray((np.arange(GROUP_W)[:, None] == perm[None, :]).astype(np.float32)).astype(BF16)
    return pl.pallas_call(
        _prep_kernel,
        grid=(D_MODEL // rows,),
        in_specs=_mod_in_specs(n_lat, rows) + [
            pl.BlockSpec((rows, w_in.shape[1]), lambda k: (k, 0)),
            pl.BlockSpec((rows, w_out.shape[1]), lambda k: (k, 0)),
            pl.BlockSpec((GROUP_W, GROUP_W), lambda k: (0, 0)),
        ],
        out_specs=[
            pl.BlockSpec((MOD_ROWS, 3 * D_MODEL), lambda k: (0, 0)),
            pl.BlockSpec((rows, w_in.shape[1]), lambda k: (k, 0)),
            pl.BlockSpec((rows, w_out.shape[1]), lambda k: (k, 0)),
        ],
        out_shape=[
            jax.ShapeDtypeStruct((MOD_ROWS, 3 * D_MODEL), F32),
            jax.ShapeDtypeStruct(w_in.shape, BF16),
            jax.ShapeDtypeStruct(w_out.shape, BF16),
        ],
        scratch_shapes=[pltpu.VMEM((MOD_ROWS, rows), F32)],
        compiler_params=pltpu.CompilerParams(dimension_semantics=("arbitrary",)),
        name="layer0_prep",
    )(c_ctx.reshape(1, D_MODEL), c, w_mod, b_mod.reshape(1, 3 * D_MODEL), w_in, w_out, pm)


def _fourier_layer_kernel(xc_ref, xl_ref, mod_ref, nw_ref, win_ref, wout_ref, m1_ref,
                          csc_ref, ssc_ref, csl_ref, ssl_ref,
                          cctx_ref, c_ref, wmod1_ref, bmod1_ref, win1_ref, wout1_ref,
                          oc_ref, ol_ref, mod1_ref, wqkvt1_ref, wz1_ref, wout1b_ref,
                          ta_scr, tb_scr, tr_scr, z_scr, cond_scr, *, n_ctx_steps):
    step = pl.program_id(0)

    @pl.when(step < n_ctx_steps)
    def _():
        _mod_accumulate(cctx_ref, c_ref, wmod1_ref, bmod1_ref, mod1_ref, cond_scr)
        w = win1_ref[...]
        wqkvt1_ref[...] = w[:, :BRANCH + 2 * KV_W].T.astype(BF16)
        wz1_ref[...] = w[:, BRANCH + 2 * KV_W:].astype(BF16)
        wout1b_ref[...] = wout1_ref[...].astype(BF16)
        _fourier_body(xc_ref, oc_ref, mod_ref[0:1, :], nw_ref, win_ref, wout_ref, m1_ref, csc_ref, ssc_ref,
                      ta_scr, tb_scr, tr_scr, z_scr)

    @pl.when(step >= n_ctx_steps)
    def _():
        mod = mod_ref[pl.ds(1 + step - n_ctx_steps, 1), :]
        _fourier_body(xl_ref, ol_ref, mod, nw_ref, win_ref, wout_ref, m1_ref, csl_ref, ssl_ref,
                      ta_scr, tb_scr, tr_scr, z_scr)


def _fourier_body(x_ref, o_ref, mod, nw_ref, win_ref, wout_ref, m1_ref, cs_ref, ss_ref,
                  ta_scr, tb_scr, tr_scr, z_scr):
    gate = mod[:, 2 * D_MODEL:]
    nw = nw_ref[...]
    n_seq, seq, _ = x_ref.shape
    n_chunks = seq // ROW_CHUNK
    lane = lax.broadcasted_iota(jnp.int32, (ROW_CHUNK, HALF_W), 1)
    for i in range(n_seq):
        for c in range(n_chunks):
            rows = slice(c * ROW_CHUNK, (c + 1) * ROW_CHUNK)
            srows = slice(i * seq + c * ROW_CHUNK, i * seq + (c + 1) * ROW_CHUNK)
            h = _mod_norm(x_ref[i, rows, :], nw, mod).astype(BF16)
            uz = _dot(h, win_ref[...])
            z_scr[srows, :] = uz[:, BRANCH:]
            u = uz[:, :BRANCH].astype(BF16)
            tr = jnp.zeros((ROW_CHUNK, HALF_W), F32)
            for g in range(N_GROUPS):
                t = _dot(u[:, g * GROUP_W:(g + 1) * GROUP_W], m1_ref[...])
                half = slice(g * HALF_W, (g + 1) * HALF_W)
                ta_scr[srows, half] = t[:, :HALF_W].astype(BF16)
                tb = t[:, HALF_W:]
                tb_scr[srows, half] = tb.astype(BF16)
                tr = jnp.where(lane == g, tb if g == 0 else pltpu.roll(tb, g, axis=1), tr)
            tr_scr[srows, :] = tr.astype(BF16)
    for i in range(n_seq):
        seq_rows = slice(i * seq, (i + 1) * seq)
        for c in range(n_chunks):
            rows = slice(c * ROW_CHUNK, (c + 1) * ROW_CHUNK)
            srows = slice(i * seq + c * ROW_CHUNK, i * seq + (c + 1) * ROW_CHUNK)
            cs = cs_ref[rows, :]
            p = _dot(cs, ta_scr[seq_rows, :])
            q = _dot(ss_ref[rows, :], tb_scr[seq_rows, :])
            r = _dot(cs, tr_scr[seq_rows, :])
            parts = []
            for g in range(N_GROUPS):
                half = slice(g * HALF_W, (g + 1) * HALF_W)
                pg, qg = p[:, half], q[:, half]
                rg = r if g == 0 else pltpu.roll(r, HALF_W - g, axis=1)
                parts.append(jnp.where(lane == 0, pg, pg - qg))
                parts.append(jnp.where(lane == 0, rg, pg + qg))
            y = jnp.concatenate(parts, axis=1)
            y = (y * jax.nn.silu(z_scr[srows, :])).astype(BF16)
            o_ref[i, rows, :] = x_ref[i, rows, :] + gate * _dot(y, wout_ref[...])


def _dft_tables(seq):
    c = np.arange(GROUP_W)[:, None]
    k = np.arange(HALF_W)[None, :]
    cos_lo = np.cos(2.0 * np.pi * ((c * k) % GROUP_W) / GROUP_W)
    sin_lo = np.sin(2.0 * np.pi * ((c * k) % GROUP_W) / GROUP_W)
    sin_lo[:, 0] = np.cos(np.pi * c[:, 0])
    m1 = np.concatenate([cos_lo, sin_lo], axis=1) / np.sqrt(GROUP_W)
    n = np.arange(seq)
    ang = 2.0 * np.pi * ((n[:, None] * n[None, :]) % seq) / seq
    cs = np.cos(ang) / np.sqrt(seq)
    ss = np.sin(ang) / np.sqrt(seq)
    return m1.astype(np.float32), cs.astype(np.float32), ss.astype(np.float32)


def _const_spec(shape):
    return pl.BlockSpec(shape, lambda b: (0,) * len(shape))


def _resident_spec(shape):
    return pl.BlockSpec(shape, lambda b: (0,) * len(shape), pipeline_mode=pl.Buffered(1))


def _fourier_layer(x_ctx, x_lat, mod, nw, win, wout, next_layer):
    nb_ctx, seq_ctx, _ = x_ctx.shape
    nb_lat, seq_lat, _ = x_lat.shape
    n_seq = CTX_SEQS_PER_STEP
    assert nb_ctx % n_seq == 0
    n_ctx = nb_ctx // n_seq
    assert D_MODEL % (n_ctx * LANES) == 0 and n_seq * seq_ctx <= seq_lat
    rows = D_MODEL // n_ctx
    c_ctx, c, w_mod1, b_mod1, w_in1, w_out1 = next_layer
    n_qkvz = 2 * BRANCH + 2 * KV_W
    m1, csc, ssc = (jnp.asarray(t).astype(BF16) for t in _dft_tables(seq_ctx))
    _, csl, ssl = (jnp.asarray(t).astype(BF16) for t in _dft_tables(seq_lat))

    def ctx_step(i):
        return jnp.minimum(i, n_ctx - 1)

    def lat_step(i):
        return jnp.maximum(i - n_ctx, 0)

    in_specs = [
        pl.BlockSpec((n_seq, seq_ctx, D_MODEL), lambda i: (ctx_step(i), 0, 0)),
        pl.BlockSpec((1, seq_lat, D_MODEL), lambda i: (lat_step(i), 0, 0)),
        MOD_SPEC,
        _const_spec((1, D_MODEL)),
        _resident_spec((D_MODEL, 2 * BRANCH)),
        _resident_spec((BRANCH, D_MODEL)),
        _const_spec((GROUP_W, GROUP_W)),
        _const_spec((seq_ctx, seq_ctx)),
        _const_spec((seq_ctx, seq_ctx)),
        _resident_spec((seq_lat, seq_lat)),
        _resident_spec((seq_lat, seq_lat)),
        pl.BlockSpec((1, rows), lambda i: (0, ctx_step(i))),
        pl.BlockSpec((c.shape[0], rows), lambda i: (0, ctx_step(i))),
        pl.BlockSpec((rows, 3 * D_MODEL), lambda i: (ctx_step(i), 0)),
        _const_spec((1, 3 * D_MODEL)),
        pl.BlockSpec((rows, n_qkvz), lambda i: (ctx_step(i), 0)),
        pl.BlockSpec((rows, D_MODEL), lambda i: (ctx_step(i), 0)),
    ]
    out_specs = [
        pl.BlockSpec((n_seq, seq_ctx, D_MODEL), lambda i: (ctx_step(i), 0, 0)),
        pl.BlockSpec((1, seq_lat, D_MODEL), lambda i: (lat_step(i), 0, 0)),
        MOD_SPEC,
        pl.BlockSpec((BRANCH + 2 * KV_W, rows), lambda i: (0, ctx_step(i))),
        pl.BlockSpec((rows, BRANCH), lambda i: (ctx_step(i), 0)),
        pl.BlockSpec((rows, D_MODEL), lambda i: (ctx_step(i), 0)),
    ]
    out_shape = [
        jax.ShapeDtypeStruct(x_ctx.shape, F32),
        jax.ShapeDtypeStruct(x_lat.shape, F32),
        jax.ShapeDtypeStruct((MOD_ROWS, 3 * D_MODEL), F32),
        jax.ShapeDtypeStruct((BRANCH + 2 * KV_W, D_MODEL), BF16),
        jax.ShapeDtypeStruct((D_MODEL, BRANCH), BF16),
        jax.ShapeDtypeStruct((BRANCH, D_MODEL), BF16),
    ]
    return pl.pallas_call(
        functools.partial(_fourier_layer_kernel, n_ctx_steps=n_ctx),
        grid=(n_ctx + nb_lat,),
        in_specs=in_specs,
        out_specs=out_specs,
        out_shape=out_shape,
        scratch_shapes=[
            pltpu.VMEM((seq_lat, N_GROUPS * HALF_W), BF16),
            pltpu.VMEM((seq_lat, N_GROUPS * HALF_W), BF16),
            pltpu.VMEM((seq_lat, HALF_W), BF16),
            pltpu.VMEM((seq_lat, BRANCH), F32),
            pltpu.VMEM((MOD_ROWS, rows), F32),
        ],
        compiler_params=pltpu.CompilerParams(
            dimension_semantics=("arbitrary",), vmem_limit_bytes=VMEM_LIMIT),
        name="fourier_layer",
    )(x_ctx, x_lat, mod, nw, win, wout, m1, csc, ssc, csl, ssl,
      c_ctx.reshape(1, D_MODEL), c, w_mod1, b_mod1.reshape(1, 3 * D_MODEL), w_in1, w_out1)


def _head_weight_tile(w_ref, n_tokens):
    row = jnp.broadcast_to(w_ref[...], (HEAD_DIM, HEAD_DIM))
    ii = lax.broadcasted_iota(jnp.int32, (HEAD_DIM, HEAD_DIM), 0)
    jj = lax.broadcasted_iota(jnp.int32, (HEAD_DIM, HEAD_DIM), 1)
    col = jnp.sum(jnp.where(ii == jj, row, 0.0), axis=1, keepdims=True)
    return jnp.broadcast_to(col, (HEAD_DIM, n_tokens))


def _head_rms(t, w):
    return (t * lax.rsqrt(jnp.mean(t * t, axis=0, keepdims=True) + EPS)) * w


def _rope_t(t, cos, sin):
    half = HEAD_DIM // 2
    x1, x2 = t[:half], t[half:]
    return jnp.concatenate([x1 * cos - x2 * sin, x1 * sin + x2 * cos], axis=0)


def _head_scores(qn, g, keys, biases):
    zeros = jnp.zeros_like(qn)
    qz = jnp.concatenate([qn, zeros] if g % 2 == 0 else [zeros, qn], axis=0)
    blk = slice((g // 2) * LANES, (g // 2 + 1) * LANES)
    scores = []
    smax = None
    for k, bias in zip(keys, biases):
        s = _dot(k[:, blk], qz)
        s = s if bias is None else s + bias
        cmax = jnp.max(s, axis=0, keepdims=True)
        smax = cmax if smax is None else jnp.maximum(smax, cmax)
        scores.append(s)
    return scores, smax


def _with_ones_rows(vt):
    return jnp.concatenate([vt, jnp.ones((ONES_ROWS, vt.shape[1]), vt.dtype)], axis=0)


def _head_probs(scored, sink2):
    scores, smax = scored
    m = jnp.maximum(smax, sink2)
    return jnp.concatenate([jnp.exp2(s - m).astype(BF16) for s in scores], axis=0), m


def _head_pv(probs, values_t, sink2):
    p, m = probs
    acc = _dot(values_t, p)
    den = acc[HEAD_DIM:HEAD_DIM + 1] + jnp.exp2(sink2 - m)
    return acc[:HEAD_DIM] * (1.0 / den)


def _attend_heads(n_units, stage, scores_fn, probs_fn, pv_fn, fillers=()):
    n_stages = n_units // stage
    pending = [scores_fn(u) for u in range(stage)]
    for g in range(n_stages):
        units = range(g * stage, (g + 1) * stage)
        if g < len(fillers):
            fillers[g]()
        nxt = [scores_fn(u) for u in range((g + 1) * stage, (g + 2) * stage)] if g + 1 < n_stages else None
        probs = [probs_fn(u, sc) for u, sc in zip(units, pending)]
        for u, pr in zip(units, probs):
            pv_fn(u, pr)
        pending = nxt


def _gate_out(x, o, z, gate, wout):
    y = (o * jax.nn.silu(z)).astype(BF16)
    return x + gate * _dot(y, wout)


def _attn_ctx_body(sink_ref, x_ref, mod, nw_ref, wqkvt_ref, wz_ref, wout_ref, qw_ref, kw_ref,
                   o_ref, kto_ref, vto_ref, qkvt_scr, z_scr, ot_scr):
    seq = x_ref.shape[1]
    n_seq = x_ref.shape[0]
    gate = mod[:, 2 * D_MODEL:]
    kw = _head_weight_tile(kw_ref, seq)
    qw = _head_weight_tile(qw_ref, seq)

    def project(i):
        h = _mod_norm(x_ref[i], nw_ref[...], mod).astype(BF16)
        qkvt_scr[i] = _dot_nt(wqkvt_ref[...], h)
        z_scr[i * seq:(i + 1) * seq, :] = _dot(h, wz_ref[...])

    def keys_values(i):
        knt = jnp.concatenate(
            [_head_rms(qkvt_scr[i, BRANCH + g * HEAD_DIM:BRANCH + (g + 1) * HEAD_DIM, :], kw)
             for g in range(N_KV)], axis=0)
        kto_ref[i] = knt
        vtf = qkvt_scr[i, BRANCH + KV_W:, :]
        vto_ref[i] = vtf
        vt = vtf.astype(BF16)
        return (knt.T.astype(BF16),
                [_with_ones_rows(vt[g * HEAD_DIM:(g + 1) * HEAD_DIM]) for g in range(N_KV)])

    def output(i):
        o_ref[i] = _gate_out(x_ref[i], ot_scr[i].T, z_scr[i * seq:(i + 1) * seq, :], gate, wout_ref[...])

    for i in range(n_seq):
        project(i)
    kv = [keys_values(i) for i in range(n_seq)]

    def scores_fn(u):
        i, hd = divmod(u, N_HEADS)
        t = qkvt_scr[i, hd * HEAD_DIM:(hd + 1) * HEAD_DIM, :]
        qn = (_head_rms(t, qw) * (HEAD_DIM ** -0.5 * LOG2E)).astype(BF16)
        return _head_scores(qn, hd // GQA, [kv[i][0]], [None])

    def probs_fn(u, sc):
        return _head_probs(sc, sink_ref[u % N_HEADS] * LOG2E)

    def pv_fn(u, pr):
        i, hd = divmod(u, N_HEADS)
        ot_scr[i, hd * HEAD_DIM:(hd + 1) * HEAD_DIM, :] = _head_pv(
            pr, kv[i][1][hd // GQA], sink_ref[hd] * LOG2E)

    fillers = [lambda: None] + [functools.partial(output, i) for i in range(n_seq - 1)]
    _attend_heads(n_seq * N_HEADS, N_HEADS, scores_fn, probs_fn, pv_fn, fillers)
    output(n_seq - 1)


def _attn_lat_body(sink_ref, x_ref, mod, nw_ref, wqkvt_ref, wz_ref, wout_ref, qw_ref, kw_ref,
                   cos_ref, sin_ref, ck_ref, cv_ref, o_ref,
                   q_scr, z_scr, ot_scr, k_scr, vt_scr):
    seq = x_ref.shape[1]
    gate = mod[:, 2 * D_MODEL:]
    nw = nw_ref[...]
    qw = _head_weight_tile(qw_ref, Q_BLOCK)
    kw = _head_weight_tile(kw_ref, Q_BLOCK)
    n_blocks = seq // Q_BLOCK
    kv_blocks = seq // BLOCK
    k_scr[0:BLOCK, :] = jnp.zeros((BLOCK, KV_W), BF16)
    k_scr[BLOCK + seq:2 * BLOCK + seq, :] = jnp.zeros((BLOCK, KV_W), BF16)
    vt_scr[0] = jnp.zeros((KV_W, BLOCK), BF16)
    vt_scr[kv_blocks + 1] = jnp.zeros((KV_W, BLOCK), BF16)

    def project(c):
        rows = slice(c * Q_BLOCK, (c + 1) * Q_BLOCK)
        h = _mod_norm(x_ref[0, rows, :], nw, mod).astype(BF16)
        z_scr[rows, :] = _dot(h, wz_ref[...])
        qkvt = _dot_nt(wqkvt_ref[...], h)
        cos = cos_ref[:, rows]
        sin = sin_ref[:, rows]
        for hd in range(N_HEADS):
            hr = slice(hd * HEAD_DIM, (hd + 1) * HEAD_DIM)
            t = _rope_t(_head_rms(qkvt[hr], qw), cos, sin)
            t = (t * (HEAD_DIM ** -0.5 * LOG2E)).astype(BF16)
            for j in range(HALVES):
                q_scr[c * HALVES + j, hd // 2, :, (hd % 2) * BLOCK:(hd % 2 + 1) * BLOCK] = (
                    t[:, j * BLOCK:(j + 1) * BLOCK])
        knt = jnp.concatenate(
            [_rope_t(_head_rms(qkvt[BRANCH + g * HEAD_DIM:BRANCH + (g + 1) * HEAD_DIM], kw), cos, sin)
             for g in range(N_KV)], axis=0)
        k_scr[BLOCK + c * Q_BLOCK:BLOCK + (c + 1) * Q_BLOCK, :] = knt.T.astype(BF16)
        vt = qkvt[BRANCH + KV_W:].astype(BF16)
        for j in range(HALVES):
            vt_scr[1 + c * HALVES + j] = vt[:, j * BLOCK:(j + 1) * BLOCK]

    ckb = ck_ref[0].T.astype(BF16)
    cvt = cv_ref[0].astype(BF16)

    win_len = 3 * BLOCK
    n_pairs = N_HEADS // 2
    kj = lax.broadcasted_iota(jnp.int32, (win_len, 2 * BLOCK), 0)
    lane = lax.broadcasted_iota(jnp.int32, (win_len, 2 * BLOCK), 1)
    rel = kj - BLOCK - lane % BLOCK
    band = (rel >= -WINDOW) & (rel <= WINDOW)
    first_head = lax.broadcasted_iota(jnp.int32, (1, 2 * BLOCK), 1) < BLOCK

    def half_operands(hb):
        r0 = pl.multiple_of(hb * BLOCK, BLOCK)
        kpos = kj + (r0 - BLOCK)
        valid = jnp.where(band & (kpos >= 0) & (kpos < seq), 0.0, NEG_INF)
        kwin = k_scr[pl.ds(r0, win_len), :]
        vall = jnp.concatenate([vt_scr[hb + j] for j in range(win_len // BLOCK)] + [cvt], axis=1)
        valls = [_with_ones_rows(vall[g * HEAD_DIM:(g + 1) * HEAD_DIM]) for g in range(N_KV)]
        return valid, kwin, valls

    def pair_sink2(pair):
        return jnp.where(first_head, sink_ref[2 * pair], sink_ref[2 * pair + 1]) * LOG2E

    def attend(n, carry):
        halves = [half_operands(n * HALVES + j) for j in range(HALVES)]

        def scores_fn(u):
            j, pair = divmod(u, n_pairs)
            valid, kwin, _ = halves[j]
            return _head_scores(q_scr[n * HALVES + j, pair], pair // (GQA // 2), [kwin, ckb], [valid, None])

        def probs_fn(u, sc):
            return _head_probs(sc, pair_sink2(u % n_pairs))

        def pv_fn(u, pr):
            j, pair = divmod(u, n_pairs)
            o = _head_pv(pr, halves[j][2][pair // (GQA // 2)], pair_sink2(pair))
            for i in range(2):
                hd = 2 * pair + i
                ot_scr[n, hd * HEAD_DIM:(hd + 1) * HEAD_DIM, j * BLOCK:(j + 1) * BLOCK] = (
                    o[:, i * BLOCK:(i + 1) * BLOCK])

        _attend_heads(HALVES * n_pairs, GQA, scores_fn, probs_fn, pv_fn)
        return carry

    for c in range(n_blocks):
        project(c)
    lax.fori_loop(0, n_blocks, attend, 0)
    for c in range(n_blocks):
        rows = slice(c * Q_BLOCK, (c + 1) * Q_BLOCK)
        o_ref[0, rows, :] = _gate_out(x_ref[0, rows, :], ot_scr[c].T, z_scr[rows, :], gate, wout_ref[...])


def _rope_tables_t(seq):
    pos = np.arange(seq)
    n_freq = HEAD_DIM // 4
    inv = ROPE_THETA ** (-np.arange(n_freq, dtype=np.float64) / n_freq)
    ang = np.concatenate([(pos // GRID_W)[:, None] * inv, (pos % GRID_W)[:, None] * inv], axis=-1)
    return np.cos(ang).T.astype(np.float32), np.sin(ang).T.astype(np.float32)


def _attn_layer_kernel(sink_ref, xc_ref, xl_ref, mod_ref, nw_ref, wqkvt_ref, wz_ref, wout_ref, qw_ref, kw_ref,
                       cos_ref, sin_ref, ck_ref, cv_ref, oc_ref, kto_ref, vto_ref, ol_ref,
                       qkvt_scr, q_scr, z_scr, ot_scr, k_scr, vt_scr, *, n_ctx_steps):
    step = pl.program_id(0)

    @pl.when(step < n_ctx_steps)
    def _():
        _attn_ctx_body(sink_ref, xc_ref, mod_ref[0:1, :], nw_ref, wqkvt_ref, wz_ref, wout_ref, qw_ref, kw_ref,
                       oc_ref, kto_ref, vto_ref, qkvt_scr, z_scr, ot_scr)

    @pl.when(step >= n_ctx_steps)
    def _():
        mod = mod_ref[pl.ds(1 + step - n_ctx_steps, 1), :]
        _attn_lat_body(sink_ref, xl_ref, mod, nw_ref, wqkvt_ref, wz_ref, wout_ref, qw_ref, kw_ref,
                       cos_ref, sin_ref, ck_ref, cv_ref, ol_ref, q_scr, z_scr, ot_scr, k_scr, vt_scr)


def _attn_layer(x_ctx, x_lat, mod, nw, wqkvt, wz, wout, qw, kw, sink, ckt, cvt):
    nb_ctx, seq_ctx, _ = x_ctx.shape
    nb_lat, seq_lat, _ = x_lat.shape
    past = ckt.shape[2]
    n_seq = CTX_SEQS_PER_STEP
    assert nb_ctx % n_seq == 0 and seq_ctx == Q_BLOCK and n_seq <= seq_lat // Q_BLOCK
    n_ctx = nb_ctx // n_seq
    cos, sin = (jnp.asarray(t) for t in _rope_tables_t(seq_lat))

    def ctx_step(i):
        return jnp.minimum(i, n_ctx - 1)

    def lat_step(i):
        return jnp.maximum(i - n_ctx, 0)

    return pl.pallas_call(
        functools.partial(_attn_layer_kernel, n_ctx_steps=n_ctx),
        grid=(n_ctx + nb_lat,),
        in_specs=[
            pl.BlockSpec(memory_space=pltpu.SMEM),
            pl.BlockSpec((n_seq, seq_ctx, D_MODEL), lambda i: (ctx_step(i), 0, 0)),
            pl.BlockSpec((1, seq_lat, D_MODEL), lambda i: (lat_step(i), 0, 0)),
            MOD_SPEC,
            _const_spec((1, D_MODEL)),
            _const_spec((BRANCH + 2 * KV_W, D_MODEL)),
            _const_spec((D_MODEL, BRANCH)),
            _const_spec((BRANCH, D_MODEL)),
            _const_spec((1, HEAD_DIM)),
            _const_spec((1, HEAD_DIM)),
            _const_spec((HEAD_DIM // 2, seq_lat)),
            _const_spec((HEAD_DIM // 2, seq_lat)),
            pl.BlockSpec((1, KV_W, past), lambda i: (lat_step(i), 0, 0)),
            pl.BlockSpec((1, KV_W, past), lambda i: (lat_step(i), 0, 0)),
        ],
        out_specs=[
            pl.BlockSpec((n_seq, seq_ctx, D_MODEL), lambda i: (ctx_step(i), 0, 0)),
            pl.BlockSpec((n_seq, KV_W, seq_ctx), lambda i: (ctx_step(i), 0, 0)),
            pl.BlockSpec((n_seq, KV_W, seq_ctx), lambda i: (ctx_step(i), 0, 0)),
            pl.BlockSpec((1, seq_lat, D_MODEL), lambda i: (lat_step(i), 0, 0)),
        ],
        out_shape=[
            jax.ShapeDtypeStruct(x_ctx.shape, F32),
            jax.ShapeDtypeStruct((nb_ctx, KV_W, seq_ctx), F32),
            jax.ShapeDtypeStruct((nb_ctx, KV_W, seq_ctx), F32),
            jax.ShapeDtypeStruct(x_lat.shape, F32),
        ],
        scratch_shapes=[
            pltpu.VMEM((n_seq, BRANCH + 2 * KV_W, seq_ctx), F32),
            pltpu.VMEM((seq_lat // BLOCK, N_HEADS // 2, HEAD_DIM, 2 * BLOCK), BF16),
            pltpu.VMEM((seq_lat, BRANCH), F32),
            pltpu.VMEM((seq_lat // Q_BLOCK, BRANCH, Q_BLOCK), F32),
            pltpu.VMEM((seq_lat + 2 * BLOCK, KV_W), BF16),
            pltpu.VMEM((seq_lat // BLOCK + 2, KV_W, BLOCK), BF16),
        ],
        compiler_params=pltpu.CompilerParams(
            dimension_semantics=("arbitrary",), vmem_limit_bytes=VMEM_LIMIT),
        name="attn_layer",
    )(sink, x_ctx, x_lat, mod, nw, wqkvt, wz, wout, qw, kw, cos, sin, ckt, cvt)


def kernel(x_prompt, x_sample, cache_k_l1, cache_v_l1, c, c_ctx, norm_w_l0, w_mod_l0, b_mod_l0,
           w_in_l0, w_out_l0, norm_w_l1, w_mod_l1, b_mod_l1, w_in_l1, q_norm_w_l1, k_norm_w_l1,
           sink_l1, w_out_l1):
    nb_ctx, seq_ctx, _ = x_prompt.shape
    nb_lat = x_sample.shape[0]
    past = cache_k_l1.shape[1]
    assert 1 + nb_lat <= MOD_ROWS
    mod0, win0, wout0 = _layer0_prep(c_ctx, c, w_mod_l0, b_mod_l0, w_in_l0, w_out_l0)

    nw0 = norm_w_l0.reshape(1, D_MODEL)
    nw1 = norm_w_l1.reshape(1, D_MODEL)
    qw = q_norm_w_l1.reshape(1, HEAD_DIM)
    kw = k_norm_w_l1.reshape(1, HEAD_DIM)

    xp, xs, mod1, wqkvt1, wz1, wout1 = _fourier_layer(
        x_prompt, x_sample, mod0, nw0, win0, wout0,
        next_layer=(c_ctx, c, w_mod_l1, b_mod_l1, w_in_l1, w_out_l1))

    def to_feature_major(t):
        return jnp.transpose(t, (0, 2, 3, 1)).reshape(t.shape[0], KV_W, t.shape[1])

    def from_feature_major(t):
        return jnp.transpose(t.reshape(t.shape[0], N_KV, HEAD_DIM, t.shape[2]), (0, 3, 1, 2))

    xp, new_kt, new_vt, xs = _attn_layer(xp, xs, mod1, nw1, wqkvt1, wz1, wout1, qw, kw, sink_l1,
                                         to_feature_major(cache_k_l1), to_feature_major(cache_v_l1))
    return (xp, xs, from_feature_major(new_kt), from_feature_major(new_vt))
```

```python
import functools

import numpy as np
import jax
import jax.numpy as jnp
from jax import lax
from jax.experimental import pallas as pl
from jax.experimental.pallas import tpu as pltpu

D_MODEL = 1024
BRANCH = 1024
N_GROUPS = 4
GROUP_W = BRANCH // N_GROUPS
HALF_W = GROUP_W // 2
HEAD_DIM = 64
N_HEADS = 16
N_KV = 4
GQA = N_HEADS // N_KV
KV_W = N_KV * HEAD_DIM
GRID_W = 64
WINDOW = 128
BLOCK = 128
ROPE_THETA = 10000.0
EPS = 1e-6
NEG_INF = -1e30
LANES = 128
ROW_CHUNK = 256
Q_BLOCK = 256
HALVES = Q_BLOCK // BLOCK
CTX_SEQS_PER_STEP = 2
VMEM_LIMIT = 56 * 1024 * 1024
MOD_ROWS = 8
MOD_K_CHUNK = 256
ONES_ROWS = 16
LOG2E = float(np.log2(np.e))

F32 = jnp.float32
BF16 = jnp.bfloat16


def _dot(a, b):
    return jnp.dot(a, b, preferred_element_type=F32)


def _dot_nt(a, b):
    return lax.dot_general(a, b, (((1,), (1,)), ((), ())), preferred_element_type=F32)


MOD_SPEC = pl.BlockSpec((MOD_ROWS, 3 * D_MODEL), lambda b: (0, 0))


def _mod_norm(x, nw, mod):
    shift = mod[:, :D_MODEL]
    scale = mod[:, D_MODEL:2 * D_MODEL]
    y = x * lax.rsqrt(jnp.mean(x * x, axis=-1, keepdims=True) + EPS)
    return (y * nw) * (1.0 + scale) + shift


def _mod_accumulate(cctx_ref, c_ref, w_ref, b_ref, o_ref, cond_scr):
    n_lat = c_ref.shape[0]
    cond_scr[...] = jnp.zeros_like(cond_scr)
    cond_scr[0:1, :] = cctx_ref[...]
    cond_scr[1:1 + n_lat, :] = c_ref[...]
    s = jax.nn.silu(cond_scr[...]).astype(BF16)

    @pl.when(pl.program_id(0) == 0)
    def _():
        o_ref[...] = jnp.broadcast_to(b_ref[...], o_ref.shape)

    o_ref[...] += _dot(s, w_ref[...].astype(BF16))


def _mod_in_specs(n_lat, rows):
    return [pl.BlockSpec((1, rows), lambda k: (0, k)),
            pl.BlockSpec((n_lat, rows), lambda k: (0, k)),
            pl.BlockSpec((rows, 3 * D_MODEL), lambda k: (k, 0)),
            pl.BlockSpec((1, 3 * D_MODEL), lambda k: (0, 0))]


def _mirror_perm():
    j = np.arange(GROUP_W)
    return np.where(j <= HALF_W, j, GROUP_W + HALF_W - j)


def _prep_kernel(cctx_ref, c_ref, wmod_ref, bmod_ref, win_ref, wout_ref, pm_ref,
                 mod_ref, winb_ref, woutb_ref, cond_scr):
    _mod_accumulate(cctx_ref, c_ref, wmod_ref, bmod_ref, mod_ref, cond_scr)
    w = win_ref[...].astype(BF16)
    winb_ref[:, :BRANCH] = w[:, :BRANCH]
    pm = pm_ref[...]
    for g in range(N_GROUPS):
        cols = slice(BRANCH + g * GROUP_W, BRANCH + (g + 1) * GROUP_W)
        winb_ref[:, cols] = _dot(w[:, cols], pm).astype(BF16)
    woutb_ref[...] = _dot(pm, wout_ref[...].astype(BF16)).astype(BF16)


def _layer0_prep(c_ctx, c, w_mod, b_mod, w_in, w_out):
    n_lat = c.shape[0]
    rows = MOD_K_CHUNK
    assert rows == GROUP_W
    perm = _mirror_perm()
    pm = jnp.asarray((np.arange(GROUP_W)[:, None] == perm[None, :]).astype(np.float32)).astype(BF16)
    return pl.pallas_call(
        _prep_kernel,
        grid=(D_MODEL // rows,),
        in_specs=_mod_in_specs(n_lat, rows) + [
            pl.BlockSpec((rows, w_in.shape[1]), lambda k: (k, 0)),
            pl.BlockSpec((rows, w_out.shape[1]), lambda k: (k, 0)),
            pl.BlockSpec((GROUP_W, GROUP_W), lambda k: (0, 0)),
        ],
        out_specs=[
            pl.BlockSpec((MOD_ROWS, 3 * D_MODEL), lambda k: (0, 0)),
            pl.BlockSpec((rows, w_in.shape[1]), lambda k: (k, 0)),
            pl.BlockSpec((rows, w_out.shape[1]), lambda k: (k, 0)),
        ],
        out_shape=[
            jax.ShapeDtypeStruct((MOD_ROWS, 3 * D_MODEL), F32),
            jax.ShapeDtypeStruct(w_in.shape, BF16),
            jax.ShapeDtypeStruct(w_out.shape, BF16),
        ],
        scratch_shapes=[pltpu.VMEM((MOD_ROWS, rows), F32)],
        compiler_params=pltpu.CompilerParams(dimension_semantics=("arbitrary",)),
        name="layer0_prep",
    )(c_ctx.reshape(1, D_MODEL), c, w_mod, b_mod.reshape(1, 3 * D_MODEL), w_in, w_out, pm)


def _fourier_layer_kernel(xc_ref, xl_ref, mod_ref, nw_ref, win_ref, wout_ref, m1_ref,
                          csc_ref, ssc_ref, csl_ref, ssl_ref,
                          cctx_ref, c_ref, wmod1_ref, bmod1_ref, win1_ref, wout1_ref,
                          oc_ref, ol_ref, mod1_ref, wqkvt1_ref, wz1_ref, wout1b_ref,
                          ta_scr, tb_scr, tr_scr, z_scr, cond_scr, *, n_ctx_steps):
    step = pl.program_id(0)

    @pl.when(step < n_ctx_steps)
    def _():
        _mod_accumulate(cctx_ref, c_ref, wmod1_ref, bmod1_ref, mod1_ref, cond_scr)
        w = win1_ref[...]
        wqkvt1_ref[...] = w[:, :BRANCH + 2 * KV_W].T.astype(BF16)
        wz1_ref[...] = w[:, BRANCH + 2 * KV_W:].astype(BF16)
        wout1b_ref[...] = wout1_ref[...].astype(BF16)
        _fourier_body(xc_ref, oc_ref, mod_ref[0:1, :], nw_ref, win_ref, wout_ref, m1_ref, csc_ref, ssc_ref,
                      ta_scr, tb_scr, tr_scr, z_scr)

    @pl.when(step >= n_ctx_steps)
    def _():
        mod = mod_ref[pl.ds(1 + step - n_ctx_steps, 1), :]
        _fourier_body(xl_ref, ol_ref, mod, nw_ref, win_ref, wout_ref, m1_ref, csl_ref, ssl_ref,
                      ta_scr, tb_scr, tr_scr, z_scr)


def _fourier_body(x_ref, o_ref, mod, nw_ref, win_ref, wout_ref, m1_ref, cs_ref, ss_ref,
                  ta_scr, tb_scr, tr_scr, z_scr):
    gate = mod[:, 2 * D_MODEL:]
    nw = nw_ref[...]
    n_seq, seq, _ = x_ref.shape
    n_chunks = seq // ROW_CHUNK
    lane = lax.broadcasted_iota(jnp.int32, (ROW_CHUNK, HALF_W), 1)
    for i in range(n_seq):
        for c in range(n_chunks):
            rows = slice(c * ROW_CHUNK, (c + 1) * ROW_CHUNK)
            srows = slice(i * seq + c * ROW_CHUNK, i * seq + (c + 1) * ROW_CHUNK)
            h = _mod_norm(x_ref[i, rows, :], nw, mod).astype(BF16)
            uz = _dot(h, win_ref[...])
            z_scr[srows, :] = uz[:, BRANCH:]
            u = uz[:, :BRANCH].astype(BF16)
            tr = jnp.zeros((ROW_CHUNK, HALF_W), F32)
            for g in range(N_GROUPS):
                t = _dot(u[:, g * GROUP_W:(g + 1) * GROUP_W], m1_ref[...])
                half = slice(g * HALF_W, (g + 1) * HALF_W)
                ta_scr[srows, half] = t[:, :HALF_W].astype(BF16)
                tb = t[:, HALF_W:]
                tb_scr[srows, half] = tb.astype(BF16)
                tr = jnp.where(lane == g, tb if g == 0 else pltpu.roll(tb, g, axis=1), tr)
            tr_scr[srows, :] = tr.astype(BF16)
    for i in range(n_seq):
        seq_rows = slice(i * seq, (i + 1) * seq)
        for c in range(n_chunks):
            rows = slice(c * ROW_CHUNK, (c + 1) * ROW_CHUNK)
            srows = slice(i * seq + c * ROW_CHUNK, i * seq + (c + 1) * ROW_CHUNK)
            cs = cs_ref[rows, :]
            p = _dot(cs, ta_scr[seq_rows, :])
            q = _dot(ss_ref[rows, :], tb_scr[seq_rows, :])
            r = _dot(cs, tr_scr[seq_rows, :])
            parts = []
            for g in range(N_GROUPS):
                half = slice(g * HALF_W, (g + 1) * HALF_W)
                pg, qg = p[:, half], q[:, half]
                rg = r if g == 0 else pltpu.roll(r, HALF_W - g, axis=1)
                parts.append(jnp.where(lane == 0, pg, pg - qg))
                parts.append(jnp.where(lane == 0, rg, pg + qg))
            y = jnp.concatenate(parts, axis=1)
            y = (y * jax.nn.silu(z_scr[srows, :])).astype(BF16)
            o_ref[i, rows, :] = x_ref[i, rows, :] + gate * _dot(y, wout_ref[...])


def _dft_tables(seq):
    c = np.arange(GROUP_W)[:, None]
    k = np.arange(HALF_W)[None, :]
    cos_lo = np.cos(2.0 * np.pi * ((c * k) % GROUP_W) / GROUP_W)
    sin_lo = np.sin(2.0 * np.pi * ((c * k) % GROUP_W) / GROUP_W)
    sin_lo[:, 0] = np.cos(np.pi * c[:, 0])
    m1 = np.concatenate([cos_lo, sin_lo], axis=1) / np.sqrt(GROUP_W)
    n = np.arange(seq)
    ang = 2.0 * np.pi * ((n[:, None] * n[None, :]) % seq) / seq
    cs = np.cos(ang) / np.sqrt(seq)
    ss = np.sin(ang) / np.sqrt(seq)
    return m1.astype(np.float32), cs.astype(np.float32), ss.astype(np.float32)


def _const_spec(shape):
    return pl.BlockSpec(shape, lambda b: (0,) * len(shape))


def _fourier_layer(x_ctx, x_lat, mod, nw, win, wout, next_layer):
    nb_ctx, seq_ctx, _ = x_ctx.shape
    nb_lat, seq_lat, _ = x_lat.shape
    n_seq = CTX_SEQS_PER_STEP
    assert nb_ctx % n_seq == 0
    n_ctx = nb_ctx // n_seq
    assert D_MODEL % (n_ctx * LANES) == 0 and n_seq * seq_ctx <= seq_lat
    rows = D_MODEL // n_ctx
    c_ctx, c, w_mod1, b_mod1, w_in1, w_out1 = next_layer
    n_qkvz = 2 * BRANCH + 2 * KV_W
    m1, csc, ssc = (jnp.asarray(t).astype(BF16) for t in _dft_tables(seq_ctx))
    _, csl, ssl = (jnp.asarray(t).astype(BF16) for t in _dft_tables(seq_lat))

    def ctx_step(i):
        return jnp.minimum(i, n_ctx - 1)

    def lat_step(i):
        return jnp.maximum(i - n_ctx, 0)

    in_specs = [
        pl.BlockSpec((n_seq, seq_ctx, D_MODEL), lambda i: (ctx_step(i), 0, 0)),
        pl.BlockSpec((1, seq_lat, D_MODEL), lambda i: (lat_step(i), 0, 0)),
        MOD_SPEC,
        _const_spec((1, D_MODEL)),
        _const_spec((D_MODEL, 2 * BRANCH)),
        _const_spec((BRANCH, D_MODEL)),
        _const_spec((GROUP_W, GROUP_W)),
        _const_spec((seq_ctx, seq_ctx)),
        _const_spec((seq_ctx, seq_ctx)),
        _const_spec((seq_lat, seq_lat)),
        _const_spec((seq_lat, seq_lat)),
        pl.BlockSpec((1, rows), lambda i: (0, ctx_step(i))),
        pl.BlockSpec((c.shape[0], rows), lambda i: (0, ctx_step(i))),
        pl.BlockSpec((rows, 3 * D_MODEL), lambda i: (ctx_step(i), 0)),
        _const_spec((1, 3 * D_MODEL)),
        pl.BlockSpec((rows, n_qkvz), lambda i: (ctx_step(i), 0)),
        pl.BlockSpec((rows, D_MODEL), lambda i: (ctx_step(i), 0)),
    ]
    out_specs = [
        pl.BlockSpec((n_seq, seq_ctx, D_MODEL), lambda i: (ctx_step(i), 0, 0)),
        pl.BlockSpec((1, seq_lat, D_MODEL), lambda i: (lat_step(i), 0, 0)),
        MOD_SPEC,
        pl.BlockSpec((BRANCH + 2 * KV_W, rows), lambda i: (0, ctx_step(i))),
        pl.BlockSpec((rows, BRANCH), lambda i: (ctx_step(i), 0)),
        pl.BlockSpec((rows, D_MODEL), lambda i: (ctx_step(i), 0)),
    ]
    out_shape = [
        jax.ShapeDtypeStruct(x_ctx.shape, F32),
        jax.ShapeDtypeStruct(x_lat.shape, F32),
        jax.ShapeDtypeStruct((MOD_ROWS, 3 * D_MODEL), F32),
        jax.ShapeDtypeStruct((BRANCH + 2 * KV_W, D_MODEL), BF16),
        jax.ShapeDtypeStruct((D_MODEL, BRANCH), BF16),
        jax.ShapeDtypeStruct((BRANCH, D_MODEL), BF16),
    ]
    return pl.pallas_call(
        functools.partial(_fourier_layer_kernel, n_ctx_steps=n_ctx),
        grid=(n_ctx + nb_lat,),
        in_specs=in_specs,
        out_specs=out_specs,
        out_shape=out_shape,
        scratch_shapes=[
            pltpu.VMEM((seq_lat, N_GROUPS * HALF_W), BF16),
            pltpu.VMEM((seq_lat, N_GROUPS * HALF_W), BF16),
            pltpu.VMEM((seq_lat, HALF_W), BF16),
            pltpu.VMEM((seq_lat, BRANCH), F32),
            pltpu.VMEM((MOD_ROWS, rows), F32),
        ],
        compiler_params=pltpu.CompilerParams(
            dimension_semantics=("arbitrary",), vmem_limit_bytes=VMEM_LIMIT),
        name="fourier_layer",
    )(x_ctx, x_lat, mod, nw, win, wout, m1, csc, ssc, csl, ssl,
      c_ctx.reshape(1, D_MODEL), c, w_mod1, b_mod1.reshape(1, 3 * D_MODEL), w_in1, w_out1)


def _head_weight_tile(w_ref, n_tokens):
    row = jnp.broadcast_to(w_ref[...], (HEAD_DIM, HEAD_DIM))
    ii = lax.broadcasted_iota(jnp.int32, (HEAD_DIM, HEAD_DIM), 0)
    jj = lax.broadcasted_iota(jnp.int32, (HEAD_DIM, HEAD_DIM), 1)
    col = jnp.sum(jnp.where(ii == jj, row, 0.0), axis=1, keepdims=True)
    return jnp.broadcast_to(col, (HEAD_DIM, n_tokens))


def _head_rms(t, w):
    return (t * lax.rsqrt(jnp.mean(t * t, axis=0, keepdims=True) + EPS)) * w


def _rope_t(t, cos, sin):
    half = HEAD_DIM // 2
    x1, x2 = t[:half], t[half:]
    return jnp.concatenate([x1 * cos - x2 * sin, x1 * sin + x2 * cos], axis=0)


def _head_scores(qn, g, keys, biases):
    zeros = jnp.zeros_like(qn)
    qz = jnp.concatenate([qn, zeros] if g % 2 == 0 else [zeros, qn], axis=0)
    blk = slice((g // 2) * LANES, (g // 2 + 1) * LANES)
    scores = []
    smax = None
    for k, bias in zip(keys, biases):
        s = _dot(k[:, blk], qz)
        s = s if bias is None else s + bias
        cmax = jnp.max(s, axis=0, keepdims=True)
        smax = cmax if smax is None else jnp.maximum(smax, cmax)
        scores.append(s)
    return scores, smax


def _with_ones_rows(vt):
    return jnp.concatenate([vt, jnp.ones((ONES_ROWS, vt.shape[1]), vt.dtype)], axis=0)


def _head_probs(scored, sink2):
    scores, smax = scored
    m = jnp.maximum(smax, sink2)
    return jnp.concatenate([jnp.exp2(s - m).astype(BF16) for s in scores], axis=0), m


def _head_pv(probs, values_t, sink2):
    p, m = probs
    acc = _dot(values_t, p)
    den = acc[HEAD_DIM:HEAD_DIM + 1] + jnp.exp2(sink2 - m)
    return acc[:HEAD_DIM] * (1.0 / den)


def _attend_heads(n_units, stage, scores_fn, probs_fn, pv_fn, fillers=()):
    n_stages = n_units // stage
    pending = [scores_fn(u) for u in range(stage)]
    for g in range(n_stages):
        units = range(g * stage, (g + 1) * stage)
        if g < len(fillers):
            fillers[g]()
        nxt = [scores_fn(u) for u in range((g + 1) * stage, (g + 2) * stage)] if g + 1 < n_stages else None
        probs = [probs_fn(u, sc) for u, sc in zip(units, pending)]
        for u, pr in zip(units, probs):
            pv_fn(u, pr)
        pending = nxt


def _gate_out(x, o, z, gate, wout):
    y = (o * jax.nn.silu(z)).astype(BF16)
    return x + gate * _dot(y, wout)


def _attn_ctx_body(sink_ref, x_ref, mod, nw_ref, wqkvt_ref, wz_ref, wout_ref, qw_ref, kw_ref,
                   o_ref, kto_ref, vto_ref, qkvt_scr, z_scr, ot_scr):
    seq = x_ref.shape[1]
    n_seq = x_ref.shape[0]
    gate = mod[:, 2 * D_MODEL:]
    kw = _head_weight_tile(kw_ref, seq)
    qw = _head_weight_tile(qw_ref, seq)

    def project(i):
        h = _mod_norm(x_ref[i], nw_ref[...], mod).astype(BF16)
        qkvt_scr[i] = _dot_nt(wqkvt_ref[...], h)
        z_scr[i * seq:(i + 1) * seq, :] = _dot(h, wz_ref[...])

    def keys_values(i):
        knt = jnp.concatenate(
            [_head_rms(qkvt_scr[i, BRANCH + g * HEAD_DIM:BRANCH + (g + 1) * HEAD_DIM, :], kw)
             for g in range(N_KV)], axis=0)
        kto_ref[i] = knt
        vtf = qkvt_scr[i, BRANCH + KV_W:, :]
        vto_ref[i] = vtf
        vt = vtf.astype(BF16)
        return (knt.T.astype(BF16),
                [_with_ones_rows(vt[g * HEAD_DIM:(g + 1) * HEAD_DIM]) for g in range(N_KV)])

    def output(i):
        o_ref[i] = _gate_out(x_ref[i], ot_scr[i].T, z_scr[i * seq:(i + 1) * seq, :], gate, wout_ref[...])

    for i in range(n_seq):
        project(i)
    kv = [keys_values(i) for i in range(n_seq)]

    def scores_fn(u):
        i, hd = divmod(u, N_HEADS)
        t = qkvt_scr[i, hd * HEAD_DIM:(hd + 1) * HEAD_DIM, :]
        qn = (_head_rms(t, qw) * (HEAD_DIM ** -0.5 * LOG2E)).astype(BF16)
        return _head_scores(qn, hd // GQA, [kv[i][0]], [None])

    def probs_fn(u, sc):
        return _head_probs(sc, sink_ref[u % N_HEADS] * LOG2E)

    def pv_fn(u, pr):
        i, hd = divmod(u, N_HEADS)
        ot_scr[i, hd * HEAD_DIM:(hd + 1) * HEAD_DIM, :] = _head_pv(
            pr, kv[i][1][hd // GQA], sink_ref[hd] * LOG2E)

    fillers = [lambda: None] + [functools.partial(output, i) for i in range(n_seq - 1)]
    _attend_heads(n_seq * N_HEADS, N_HEADS, scores_fn, probs_fn, pv_fn, fillers)
    output(n_seq - 1)


def _attn_lat_body(sink_ref, x_ref, mod, nw_ref, wqkvt_ref, wz_ref, wout_ref, qw_ref, kw_ref,
                   cos_ref, sin_ref, ck_ref, cv_ref, o_ref,
                   q_scr, z_scr, ot_scr, k_scr, vt_scr):
    seq = x_ref.shape[1]
    gate = mod[:, 2 * D_MODEL:]
    nw = nw_ref[...]
    qw = _head_weight_tile(qw_ref, Q_BLOCK)
    kw = _head_weight_tile(kw_ref, Q_BLOCK)
    n_blocks = seq // Q_BLOCK
    kv_blocks = seq // BLOCK
    k_scr[0:BLOCK, :] = jnp.zeros((BLOCK, KV_W), BF16)
    k_scr[BLOCK + seq:2 * BLOCK + seq, :] = jnp.zeros((BLOCK, KV_W), BF16)
    vt_scr[0] = jnp.zeros((KV_W, BLOCK), BF16)
    vt_scr[kv_blocks + 1] = jnp.zeros((KV_W, BLOCK), BF16)

    def project(c):
        rows = slice(c * Q_BLOCK, (c + 1) * Q_BLOCK)
        h = _mod_norm(x_ref[0, rows, :], nw, mod).astype(BF16)
        z_scr[rows, :] = _dot(h, wz_ref[...])
        qkvt = _dot_nt(wqkvt_ref[...], h)
        cos = cos_ref[:, rows]
        sin = sin_ref[:, rows]
        for hd in range(N_HEADS):
            hr = slice(hd * HEAD_DIM, (hd + 1) * HEAD_DIM)
            t = _rope_t(_head_rms(qkvt[hr], qw), cos, sin)
            t = (t * (HEAD_DIM ** -0.5 * LOG2E)).astype(BF16)
            for j in range(HALVES):
                q_scr[c * HALVES + j, hd // 2, :, (hd % 2) * BLOCK:(hd % 2 + 1) * BLOCK] = (
                    t[:, j * BLOCK:(j + 1) * BLOCK])
        knt = jnp.concatenate(
            [_rope_t(_head_rms(qkvt[BRANCH + g * HEAD_DIM:BRANCH + (g + 1) * HEAD_DIM], kw), cos, sin)
             for g in range(N_KV)], axis=0)
        k_scr[BLOCK + c * Q_BLOCK:BLOCK + (c + 1) * Q_BLOCK, :] = knt.T.astype(BF16)
        vt = qkvt[BRANCH + KV_W:].astype(BF16)
        for j in range(HALVES):
            vt_scr[1 + c * HALVES + j] = vt[:, j * BLOCK:(j + 1) * BLOCK]

    ckb = ck_ref[0].T.astype(BF16)
    cvt = cv_ref[0].astype(BF16)

    win_len = 3 * BLOCK
    n_pairs = N_HEADS // 2
    kj = lax.broadcasted_iota(jnp.int32, (win_len, 2 * BLOCK), 0)
    lane = lax.broadcasted_iota(jnp.int32, (win_len, 2 * BLOCK), 1)
    rel = kj - BLOCK - lane % BLOCK
    band = (rel >= -WINDOW) & (rel <= WINDOW)
    first_head = lax.broadcasted_iota(jnp.int32, (1, 2 * BLOCK), 1) < BLOCK

    def half_operands(hb):
        r0 = pl.multiple_of(hb * BLOCK, BLOCK)
        kpos = kj + (r0 - BLOCK)
        valid = jnp.where(band & (kpos >= 0) & (kpos < seq), 0.0, NEG_INF)
        kwin = k_scr[pl.ds(r0, win_len), :]
        vall = jnp.concatenate([vt_scr[hb + j] for j in range(win_len // BLOCK)] + [cvt], axis=1)
        valls = [_with_ones_rows(vall[g * HEAD_DIM:(g + 1) * HEAD_DIM]) for g in range(N_KV)]
        return valid, kwin, valls

    def pair_sink2(pair):
        return jnp.where(first_head, sink_ref[2 * pair], sink_ref[2 * pair + 1]) * LOG2E

    def attend(n, carry):
        halves = [half_operands(n * HALVES + j) for j in range(HALVES)]

        def scores_fn(u):
            j, pair = divmod(u, n_pairs)
            valid, kwin, _ = halves[j]
            return _head_scores(q_scr[n * HALVES + j, pair], pair // (GQA // 2), [kwin, ckb], [valid, None])

        def probs_fn(u, sc):
            return _head_probs(sc, pair_sink2(u % n_pairs))

        def pv_fn(u, pr):
            j, pair = divmod(u, n_pairs)
            o = _head_pv(pr, halves[j][2][pair // (GQA // 2)], pair_sink2(pair))
            for i in range(2):
                hd = 2 * pair + i
                ot_scr[n, hd * HEAD_DIM:(hd + 1) * HEAD_DIM, j * BLOCK:(j + 1) * BLOCK] = (
                    o[:, i * BLOCK:(i + 1) * BLOCK])

        _attend_heads(HALVES * n_pairs, GQA, scores_fn, probs_fn, pv_fn)
        return carry

    for c in range(n_blocks):
        project(c)
    lax.fori_loop(0, n_blocks, attend, 0)
    for c in range(n_blocks):
        rows = slice(c * Q_BLOCK, (c + 1) * Q_BLOCK)
        o_ref[0, rows, :] = _gate_out(x_ref[0, rows, :], ot_scr[c].T, z_scr[rows, :], gate, wout_ref[...])


def _rope_tables_t(seq):
    pos = np.arange(seq)
    n_freq = HEAD_DIM // 4
    inv = ROPE_THETA ** (-np.arange(n_freq, dtype=np.float64) / n_freq)
    ang = np.concatenate([(pos // GRID_W)[:, None] * inv, (pos % GRID_W)[:, None] * inv], axis=-1)
    return np.cos(ang).T.astype(np.float32), np.sin(ang).T.astype(np.float32)


def _attn_layer_kernel(sink_ref, xc_ref, xl_ref, mod_ref, nw_ref, wqkvt_ref, wz_ref, wout_ref, qw_ref, kw_ref,
                       cos_ref, sin_ref, ck_ref, cv_ref, oc_ref, kto_ref, vto_ref, ol_ref,
                       qkvt_scr, q_scr, z_scr, ot_scr, k_scr, vt_scr, *, n_ctx_steps):
    step = pl.program_id(0)

    @pl.when(step < n_ctx_steps)
    def _():
        _attn_ctx_body(sink_ref, xc_ref, mod_ref[0:1, :], nw_ref, wqkvt_ref, wz_ref, wout_ref, qw_ref, kw_ref,
                       oc_ref, kto_ref, vto_ref, qkvt_scr, z_scr, ot_scr)

    @pl.when(step >= n_ctx_steps)
    def _():
        mod = mod_ref[pl.ds(1 + step - n_ctx_steps, 1), :]
        _attn_lat_body(sink_ref, xl_ref, mod, nw_ref, wqkvt_ref, wz_ref, wout_ref, qw_ref, kw_ref,
                       cos_ref, sin_ref, ck_ref, cv_ref, ol_ref, q_scr, z_scr, ot_scr, k_scr, vt_scr)


def _attn_layer(x_ctx, x_lat, mod, nw, wqkvt, wz, wout, qw, kw, sink, ckt, cvt):
    nb_ctx, seq_ctx, _ = x_ctx.shape
    nb_lat, seq_lat, _ = x_lat.shape
    past = ckt.shape[2]
    n_seq = CTX_SEQS_PER_STEP
    assert nb_ctx % n_seq == 0 and seq_ctx == Q_BLOCK and n_seq <= seq_lat // Q_BLOCK
    n_ctx = nb_ctx // n_seq
    cos, sin = (jnp.asarray(t) for t in _rope_tables_t(seq_lat))

    def ctx_step(i):
        return jnp.minimum(i, n_ctx - 1)

    def lat_step(i):
        return jnp.maximum(i - n_ctx, 0)

    return pl.pallas_call(
        functools.partial(_attn_layer_kernel, n_ctx_steps=n_ctx),
        grid=(n_ctx + nb_lat,),
        in_specs=[
            pl.BlockSpec(memory_space=pltpu.SMEM),
            pl.BlockSpec((n_seq, seq_ctx, D_MODEL), lambda i: (ctx_step(i), 0, 0)),
            pl.BlockSpec((1, seq_lat, D_MODEL), lambda i: (lat_step(i), 0, 0)),
            MOD_SPEC,
            _const_spec((1, D_MODEL)),
            _const_spec((BRANCH + 2 * KV_W, D_MODEL)),
            _const_spec((D_MODEL, BRANCH)),
            _const_spec((BRANCH, D_MODEL)),
            _const_spec((1, HEAD_DIM)),
            _const_spec((1, HEAD_DIM)),
            _const_spec((HEAD_DIM // 2, seq_lat)),
            _const_spec((HEAD_DIM // 2, seq_lat)),
            pl.BlockSpec((1, KV_W, past), lambda i: (lat_step(i), 0, 0)),
            pl.BlockSpec((1, KV_W, past), lambda i: (lat_step(i), 0, 0)),
        ],
        out_specs=[
            pl.BlockSpec((n_seq, seq_ctx, D_MODEL), lambda i: (ctx_step(i), 0, 0)),
            pl.BlockSpec((n_seq, KV_W, seq_ctx), lambda i: (ctx_step(i), 0, 0)),
            pl.BlockSpec((n_seq, KV_W, seq_ctx), lambda i: (ctx_step(i), 0, 0)),
            pl.BlockSpec((1, seq_lat, D_MODEL), lambda i: (lat_step(i), 0, 0)),
        ],
        out_shape=[
            jax.ShapeDtypeStruct(x_ctx.shape, F32),
            jax.ShapeDtypeStruct((nb_ctx, KV_W, seq_ctx), F32),
            jax.ShapeDtypeStruct((nb_ctx, KV_W, seq_ctx), F32),
            jax.ShapeDtypeStruct(x_lat.shape, F32),
        ],
        scratch_shapes=[
            pltpu.VMEM((n_seq, BRANCH + 2 * KV_W, seq_ctx), F32),
            pltpu.VMEM((seq_lat // BLOCK, N_HEADS // 2, HEAD_DIM, 2 * BLOCK), BF16),
            pltpu.VMEM((seq_lat, BRANCH), F32),
            pltpu.VMEM((seq_lat // Q_BLOCK, BRANCH, Q_BLOCK), F32),
            pltpu.VMEM((seq_lat + 2 * BLOCK, KV_W), BF16),
            pltpu.VMEM((seq_lat // BLOCK + 2, KV_W, BLOCK), BF16),
        ],
        compiler_params=pltpu.CompilerParams(
            dimension_semantics=("arbitrary",), vmem_limit_bytes=VMEM_LIMIT),
        name="attn_layer",
    )(sink, x_ctx, x_lat, mod, nw, wqkvt, wz, wout, qw, kw, cos, sin, ckt, cvt)


def kernel(x_prompt, x_sample, cache_k_l1, cache_v_l1, c, c_ctx, norm_w_l0, w_mod_l0, b_mod_l0,
           w_in_l0, w_out_l0, norm_w_l1, w_mod_l1, b_mod_l1, w_in_l1, q_norm_w_l1, k_norm_w_l1,
           sink_l1, w_out_l1):
    nb_ctx, seq_ctx, _ = x_prompt.shape
    nb_lat = x_sample.shape[0]
    past = cache_k_l1.shape[1]
    assert 1 + nb_lat <= MOD_ROWS
    mod0, win0, wout0 = _layer0_prep(c_ctx, c, w_mod_l0, b_mod_l0, w_in_l0, w_out_l0)

    nw0 = norm_w_l0.reshape(1, D_MODEL)
    nw1 = norm_w_l1.reshape(1, D_MODEL)
    qw = q_norm_w_l1.reshape(1, HEAD_DIM)
    kw = k_norm_w_l1.reshape(1, HEAD_DIM)

    xp, xs, mod1, wqkvt1, wz1, wout1 = _fourier_layer(
        x_prompt, x_sample, mod0, nw0, win0, wout0,
        next_layer=(c_ctx, c, w_mod_l1, b_mod_l1, w_in_l1, w_out_l1))

    def to_feature_major(t):
        return jnp.transpose(t, (0, 2, 3, 1)).reshape(t.shape[0], KV_W, t.shape[1])

    def from_feature_major(t):
        return jnp.transpose(t.reshape(t.shape[0], N_KV, HEAD_DIM, t.shape[2]), (0, 3, 1, 2))

    xp, new_kt, new_vt, xs = _attn_layer(xp, xs, mod1, nw1, wqkvt1, wz1, wout1, qw, kw, sink_l1,
                                         to_feature_major(cache_k_l1), to_feature_major(cache_v_l1))
    return (xp, xs, from_feature_major(new_kt), from_feature_major(new_vt))
```

```python
import functools

import numpy as np
import jax
import jax.numpy as jnp
from jax import lax
from jax.experimental import pallas as pl
from jax.experimental.pallas import tpu as pltpu

D_MODEL = 1024
BRANCH = 1024
N_GROUPS = 4
GROUP_W = BRANCH // N_GROUPS
HALF_W = GROUP_W // 2
HEAD_DIM = 64
N_HEADS = 16
N_KV = 4
GQA = N_HEADS // N_KV
KV_W = N_KV * HEAD_DIM
GRID_W = 64
WINDOW = 128
BLOCK = 128
ROPE_THETA = 10000.0
EPS = 1e-6
NEG_INF = -1e30
LANES = 128
ROW_CHUNK = 256
Q_BLOCK = 256
HALVES = Q_BLOCK // BLOCK
CTX_SEQS_PER_STEP = 2
VMEM_LIMIT = 56 * 1024 * 1024
FOURIER_VMEM_LIMIT = 60 * 1024 * 1024
MOD_ROWS = 8
ONES_ROWS = 16
LOG2E = float(np.log2(np.e))

F32 = jnp.float32
BF16 = jnp.bfloat16


def _dot(a, b):
    return jnp.dot(a, b, preferred_element_type=F32)


def _dot_nt(a, b):
    return lax.dot_general(a, b, (((1,), (1,)), ((), ())), preferred_element_type=F32)


MOD_SPEC = pl.BlockSpec((MOD_ROWS, 3 * D_MODEL), lambda b: (0, 0))


def _mod_norm(x, nw, mod):
    shift = mod[:, :D_MODEL]
    scale = mod[:, D_MODEL:2 * D_MODEL]
    y = x * lax.rsqrt(jnp.mean(x * x, axis=-1, keepdims=True) + EPS)
    return (y * nw) * (1.0 + scale) + shift


def _mod_accumulate(cctx_ref, c_ref, w_ref, b_ref, o_ref, cond_scr, first_step):
    n_lat = c_ref.shape[0]
    cond_scr[...] = jnp.zeros_like(cond_scr)
    cond_scr[0:1, :] = cctx_ref[...]
    cond_scr[1:1 + n_lat, :] = c_ref[...]
    s = jax.nn.silu(cond_scr[...]).astype(BF16)

    @pl.when(pl.program_id(0) == first_step)
    def _():
        o_ref[...] = jnp.broadcast_to(b_ref[...], o_ref.shape)

    o_ref[...] += _dot(s, w_ref[...].astype(BF16))


def _mirror_perm():
    j = np.arange(GROUP_W)
    return np.where(j <= HALF_W, j, GROUP_W + HALF_W - j)


def _perm_matrices():
    perm = _mirror_perm()
    pm = (np.arange(GROUP_W)[:, None] == perm[None, :]).astype(np.float32)
    assert (pm[:HALF_W, HALF_W:] == 0).all() and (pm[HALF_W:, :HALF_W] == 0).all()
    return pm, np.stack([pm[:HALF_W, :HALF_W], pm[HALF_W:, HALF_W:]])


def _fourier_layer_kernel(xc_ref, xl_ref, nw_ref, m1_ref, csc_ref, ssc_ref, csl_ref, ssl_ref,
                          cctx0_ref, c0_ref, wmod0_ref, bmod0_ref, win0_ref, wout0_ref, pm_ref, pmh_ref,
                          cctx_ref, c_ref, wmod1_ref, bmod1_ref, win1_ref, wout1_ref,
                          oc_ref, ol_ref, mod1_ref, wqkvt1_ref, wz1_ref, wout1b_ref,
                          mod0_scr, win_scr, wout_scr, ta_scr, tb_scr, tr_scr, z_scr, cond_scr,
                          *, n_prep_steps, n_ctx_steps):
    step = pl.program_id(0)

    @pl.when(step < n_prep_steps)
    def _():
        _mod_accumulate(cctx0_ref, c0_ref, wmod0_ref, bmod0_ref, mod0_scr, cond_scr, 0)
        rows = pl.ds(pl.multiple_of(step * HALF_W, HALF_W), HALF_W)
        w = win0_ref[...].astype(BF16)
        win_scr[rows, :BRANCH] = w[:, :BRANCH]
        pm = pm_ref[...]
        for g in range(N_GROUPS):
            cols = slice(BRANCH + g * GROUP_W, BRANCH + (g + 1) * GROUP_W)
            win_scr[rows, cols] = _dot(w[:, cols], pm).astype(BF16)
        wout_scr[rows, :] = _dot(pmh_ref[step % 2], wout0_ref[...].astype(BF16)).astype(BF16)

    @pl.when(jnp.logical_and(step >= n_prep_steps, step < n_prep_steps + n_ctx_steps))
    def _():
        _mod_accumulate(cctx_ref, c_ref, wmod1_ref, bmod1_ref, mod1_ref, cond_scr, n_prep_steps)
        w = win1_ref[...]
        wqkvt1_ref[...] = w[:, :BRANCH + 2 * KV_W].T.astype(BF16)
        wz1_ref[...] = w[:, BRANCH + 2 * KV_W:].astype(BF16)
        wout1b_ref[...] = wout1_ref[...].astype(BF16)
        _fourier_body(xc_ref, oc_ref, mod0_scr[0:1, :], nw_ref, win_scr, wout_scr, m1_ref, csc_ref, ssc_ref,
                      ta_scr, tb_scr, tr_scr, z_scr)

    @pl.when(step >= n_prep_steps + n_ctx_steps)
    def _():
        mod = mod0_scr[pl.ds(1 + step - n_prep_steps - n_ctx_steps, 1), :]
        _fourier_body(xl_ref, ol_ref, mod, nw_ref, win_scr, wout_scr, m1_ref, csl_ref, ssl_ref,
                      ta_scr, tb_scr, tr_scr, z_scr)


def _fourier_body(x_ref, o_ref, mod, nw_ref, win_ref, wout_ref, m1_ref, cs_ref, ss_ref,
                  ta_scr, tb_scr, tr_scr, z_scr):
    gate = mod[:, 2 * D_MODEL:]
    nw = nw_ref[...]
    n_seq, seq, _ = x_ref.shape
    n_chunks = seq // ROW_CHUNK
    lane = lax.broadcasted_iota(jnp.int32, (ROW_CHUNK, HALF_W), 1)
    for i in range(n_seq):
        for c in range(n_chunks):
            rows = slice(c * ROW_CHUNK, (c + 1) * ROW_CHUNK)
            srows = slice(i * seq + c * ROW_CHUNK, i * seq + (c + 1) * ROW_CHUNK)
            h = _mod_norm(x_ref[i, rows, :], nw, mod).astype(BF16)
            uz = _dot(h, win_ref[...])
            z_scr[srows, :] = uz[:, BRANCH:]
            u = uz[:, :BRANCH].astype(BF16)
            tr = jnp.zeros((ROW_CHUNK, HALF_W), F32)
            for g in range(N_GROUPS):
                t = _dot(u[:, g * GROUP_W:(g + 1) * GROUP_W], m1_ref[...])
                half = slice(g * HALF_W, (g + 1) * HALF_W)
                ta_scr[srows, half] = t[:, :HALF_W].astype(BF16)
                tb = t[:, HALF_W:]
                tb_scr[srows, half] = tb.astype(BF16)
                tr = jnp.where(lane == g, tb if g == 0 else pltpu.roll(tb, g, axis=1), tr)
            tr_scr[srows, :] = tr.astype(BF16)
    for i in range(n_seq):
        seq_rows = slice(i * seq, (i + 1) * seq)
        for c in range(n_chunks):
            rows = slice(c * ROW_CHUNK, (c + 1) * ROW_CHUNK)
            srows = slice(i * seq + c * ROW_CHUNK, i * seq + (c + 1) * ROW_CHUNK)
            cs = cs_ref[rows, :]
            p = _dot(cs, ta_scr[seq_rows, :])
            q = _dot(ss_ref[rows, :], tb_scr[seq_rows, :])
            r = _dot(cs, tr_scr[seq_rows, :])
            parts = []
            for g in range(N_GROUPS):
                half = slice(g * HALF_W, (g + 1) * HALF_W)
                pg, qg = p[:, half], q[:, half]
                rg = r if g == 0 else pltpu.roll(r, HALF_W - g, axis=1)
                parts.append(jnp.where(lane == 0, pg, pg - qg))
                parts.append(jnp.where(lane == 0, rg, pg + qg))
            y = jnp.concatenate(parts, axis=1)
            y = (y * jax.nn.silu(z_scr[srows, :])).astype(BF16)
            o_ref[i, rows, :] = x_ref[i, rows, :] + gate * _dot(y, wout_ref[...])


def _dft_tables(seq):
    c = np.arange(GROUP_W)[:, None]
    k = np.arange(HALF_W)[None, :]
    cos_lo = np.cos(2.0 * np.pi * ((c * k) % GROUP_W) / GROUP_W)
    sin_lo = np.sin(2.0 * np.pi * ((c * k) % GROUP_W) / GROUP_W)
    sin_lo[:, 0] = np.cos(np.pi * c[:, 0])
    m1 = np.concatenate([cos_lo, sin_lo], axis=1) / np.sqrt(GROUP_W)
    n = np.arange(seq)
    ang = 2.0 * np.pi * ((n[:, None] * n[None, :]) % seq) / seq
    cs = np.cos(ang) / np.sqrt(seq)
    ss = np.sin(ang) / np.sqrt(seq)
    return m1.astype(np.float32), cs.astype(np.float32), ss.astype(np.float32)


def _const_spec(shape):
    return pl.BlockSpec(shape, lambda b: (0,) * len(shape))


def _fourier_layer(x_ctx, x_lat, nw, this_layer, next_layer):
    nb_ctx, seq_ctx, _ = x_ctx.shape
    nb_lat, seq_lat, _ = x_lat.shape
    n_seq = CTX_SEQS_PER_STEP
    assert nb_ctx % n_seq == 0
    n_prep = D_MODEL // HALF_W
    n_ctx = nb_ctx // n_seq
    assert D_MODEL % (n_ctx * LANES) == 0 and n_seq * seq_ctx <= seq_lat
    rows = D_MODEL // n_ctx
    assert rows == HALF_W
    c_ctx, c, w_mod0, b_mod0, w_in0, w_out0 = this_layer
    _, _, w_mod1, b_mod1, w_in1, w_out1 = next_layer
    n_lat = c.shape[0]
    n_qkvz = 2 * BRANCH + 2 * KV_W
    m1, csc, ssc = (jnp.asarray(t).astype(BF16) for t in _dft_tables(seq_ctx))
    _, csl, ssl = (jnp.asarray(t).astype(BF16) for t in _dft_tables(seq_lat))
    pm, pmh = (jnp.asarray(t).astype(BF16) for t in _perm_matrices())
    cc = c_ctx.reshape(1, D_MODEL)

    def prep_step(i):
        return jnp.minimum(i, n_prep - 1)

    def ctx_step(i):
        return jnp.clip(i - n_prep, 0, n_ctx - 1)

    def lat_step(i):
        return jnp.maximum(i - n_prep - n_ctx, 0)

    def chunk_specs(step_fn, w_in_cols):
        return [pl.BlockSpec((1, HALF_W), lambda i: (0, step_fn(i))),
                pl.BlockSpec((n_lat, HALF_W), lambda i: (0, step_fn(i))),
                pl.BlockSpec((HALF_W, 3 * D_MODEL), lambda i: (step_fn(i), 0)),
                _const_spec((1, 3 * D_MODEL)),
                pl.BlockSpec((HALF_W, w_in_cols), lambda i: (step_fn(i), 0)),
                pl.BlockSpec((HALF_W, D_MODEL), lambda i: (step_fn(i), 0))]

    in_specs = [
        pl.BlockSpec((n_seq, seq_ctx, D_MODEL), lambda i: (ctx_step(i), 0, 0)),
        pl.BlockSpec((1, seq_lat, D_MODEL), lambda i: (lat_step(i), 0, 0)),
        _const_spec((1, D_MODEL)),
        _const_spec((GROUP_W, GROUP_W)),
        _const_spec((seq_ctx, seq_ctx)),
        _const_spec((seq_ctx, seq_ctx)),
        _const_spec((seq_lat, seq_lat)),
        _const_spec((seq_lat, seq_lat)),
    ] + chunk_specs(prep_step, 2 * BRANCH) + [
        _const_spec((GROUP_W, GROUP_W)),
        _const_spec((2, HALF_W, HALF_W)),
    ] + chunk_specs(ctx_step, n_qkvz)
    out_specs = [
        pl.BlockSpec((n_seq, seq_ctx, D_MODEL), lambda i: (ctx_step(i), 0, 0)),
        pl.BlockSpec((1, seq_lat, D_MODEL), lambda i: (lat_step(i), 0, 0)),
        MOD_SPEC,
        pl.BlockSpec((BRANCH + 2 * KV_W, rows), lambda i: (0, ctx_step(i))),
        pl.BlockSpec((rows, BRANCH), lambda i: (ctx_step(i), 0)),
        pl.BlockSpec((rows, D_MODEL), lambda i: (ctx_step(i), 0)),
    ]
    out_shape = [
        jax.ShapeDtypeStruct(x_ctx.shape, F32),
        jax.ShapeDtypeStruct(x_lat.shape, F32),
        jax.ShapeDtypeStruct((MOD_ROWS, 3 * D_MODEL), F32),
        jax.ShapeDtypeStruct((BRANCH + 2 * KV_W, D_MODEL), BF16),
        jax.ShapeDtypeStruct((D_MODEL, BRANCH), BF16),
        jax.ShapeDtypeStruct((BRANCH, D_MODEL), BF16),
    ]
    return pl.pallas_call(
        functools.partial(_fourier_layer_kernel, n_prep_steps=n_prep, n_ctx_steps=n_ctx),
        grid=(n_prep + n_ctx + nb_lat,),
        in_specs=in_specs,
        out_specs=out_specs,
        out_shape=out_shape,
        scratch_shapes=[
            pltpu.VMEM((MOD_ROWS, 3 * D_MODEL), F32),
            pltpu.VMEM((D_MODEL, 2 * BRANCH), BF16),
            pltpu.VMEM((BRANCH, D_MODEL), BF16),
            pltpu.VMEM((seq_lat, N_GROUPS * HALF_W), BF16),
            pltpu.VMEM((seq_lat, N_GROUPS * HALF_W), BF16),
            pltpu.VMEM((seq_lat, HALF_W), BF16),
            pltpu.VMEM((seq_lat, BRANCH), F32),
            pltpu.VMEM((MOD_ROWS, HALF_W), F32),
        ],
        compiler_params=pltpu.CompilerParams(
            dimension_semantics=("arbitrary",), vmem_limit_bytes=FOURIER_VMEM_LIMIT),
        name="fourier_layer",
    )(x_ctx, x_lat, nw, m1, csc, ssc, csl, ssl,
      cc, c, w_mod0, b_mod0.reshape(1, 3 * D_MODEL), w_in0, w_out0, pm, pmh,
      cc, c, w_mod1, b_mod1.reshape(1, 3 * D_MODEL), w_in1, w_out1)


def _head_weight_tile(w_ref, n_tokens):
    row = jnp.broadcast_to(w_ref[...], (HEAD_DIM, HEAD_DIM))
    ii = lax.broadcasted_iota(jnp.int32, (HEAD_DIM, HEAD_DIM), 0)
    jj = lax.broadcasted_iota(jnp.int32, (HEAD_DIM, HEAD_DIM), 1)
    col = jnp.sum(jnp.where(ii == jj, row, 0.0), axis=1, keepdims=True)
    return jnp.broadcast_to(col, (HEAD_DIM, n_tokens))


def _head_rms(t, w):
    return (t * lax.rsqrt(jnp.mean(t * t, axis=0, keepdims=True) + EPS)) * w


def _rope_t(t, cos, sin):
    half = HEAD_DIM // 2
    x1, x2 = t[:half], t[half:]
    return jnp.concatenate([x1 * cos - x2 * sin, x1 * sin + x2 * cos], axis=0)


def _head_scores(qn, g, keys, biases):
    zeros = jnp.zeros_like(qn)
    qz = jnp.concatenate([qn, zeros] if g % 2 == 0 else [zeros, qn], axis=0)
    blk = slice((g // 2) * LANES, (g // 2 + 1) * LANES)
    scores = []
    smax = None
    for k, bias in zip(keys, biases):
        s = _dot(k[:, blk], qz)
        s = s if bias is None else s + bias
        cmax = jnp.max(s, axis=0, keepdims=True)
        smax = cmax if smax is None else jnp.maximum(smax, cmax)
        scores.append(s)
    return scores, smax


def _with_ones_rows(vt):
    return jnp.concatenate([vt, jnp.ones((ONES_ROWS, vt.shape[1]), vt.dtype)], axis=0)


def _head_probs(scored, sink2):
    scores, smax = scored
    m = jnp.maximum(smax, sink2)
    return jnp.concatenate([jnp.exp2(s - m).astype(BF16) for s in scores], axis=0), m


def _head_pv(probs, values_t, sink2):
    p, m = probs
    acc = _dot(values_t, p)
    den = acc[HEAD_DIM:HEAD_DIM + 1] + jnp.exp2(sink2 - m)
    return acc[:HEAD_DIM] * (1.0 / den)


def _attend_heads(n_units, stage, scores_fn, probs_fn, pv_fn, fillers=()):
    n_stages = n_units // stage
    pending = [scores_fn(u) for u in range(stage)]
    for g in range(n_stages):
        units = range(g * stage, (g + 1) * stage)
        if g < len(fillers):
            fillers[g]()
        nxt = [scores_fn(u) for u in range((g + 1) * stage, (g + 2) * stage)] if g + 1 < n_stages else None
        probs = [probs_fn(u, sc) for u, sc in zip(units, pending)]
        for u, pr in zip(units, probs):
            pv_fn(u, pr)
        pending = nxt


def _gate_out(x, o, z, gate, wout):
    y = (o * jax.nn.silu(z)).astype(BF16)
    return x + gate * _dot(y, wout)


def _attn_ctx_body(sink_ref, x_ref, mod, nw_ref, wqkvt_ref, wz_ref, wout_ref, qw_ref, kw_ref,
                   o_ref, kto_ref, vto_ref, qkvt_scr, z_scr, ot_scr):
    seq = x_ref.shape[1]
    n_seq = x_ref.shape[0]
    gate = mod[:, 2 * D_MODEL:]
    kw = _head_weight_tile(kw_ref, seq)
    qw = _head_weight_tile(qw_ref, seq)

    def project(i):
        h = _mod_norm(x_ref[i], nw_ref[...], mod).astype(BF16)
        qkvt_scr[i] = _dot_nt(wqkvt_ref[...], h)
        z_scr[i * seq:(i + 1) * seq, :] = _dot(h, wz_ref[...])

    def keys_values(i):
        knt = jnp.concatenate(
            [_head_rms(qkvt_scr[i, BRANCH + g * HEAD_DIM:BRANCH + (g + 1) * HEAD_DIM, :], kw)
             for g in range(N_KV)], axis=0)
        kto_ref[i] = knt
        vtf = qkvt_scr[i, BRANCH + KV_W:, :]
        vto_ref[i] = vtf
        vt = vtf.astype(BF16)
        return (knt.T.astype(BF16),
                [_with_ones_rows(vt[g * HEAD_DIM:(g + 1) * HEAD_DIM]) for g in range(N_KV)])

    def output(i):
        o_ref[i] = _gate_out(x_ref[i], ot_scr[i].T, z_scr[i * seq:(i + 1) * seq, :], gate, wout_ref[...])

    for i in range(n_seq):
        project(i)
    kv = [keys_values(i) for i in range(n_seq)]

    def scores_fn(u):
        i, hd = divmod(u, N_HEADS)
        t = qkvt_scr[i, hd * HEAD_DIM:(hd + 1) * HEAD_DIM, :]
        qn = (_head_rms(t, qw) * (HEAD_DIM ** -0.5 * LOG2E)).astype(BF16)
        return _head_scores(qn, hd // GQA, [kv[i][0]], [None])

    def probs_fn(u, sc):
        return _head_probs(sc, sink_ref[u % N_HEADS] * LOG2E)

    def pv_fn(u, pr):
        i, hd = divmod(u, N_HEADS)
        ot_scr[i, hd * HEAD_DIM:(hd + 1) * HEAD_DIM, :] = _head_pv(
            pr, kv[i][1][hd // GQA], sink_ref[hd] * LOG2E)

    fillers = [lambda: None] + [functools.partial(output, i) for i in range(n_seq - 1)]
    _attend_heads(n_seq * N_HEADS, N_HEADS, scores_fn, probs_fn, pv_fn, fillers)
    output(n_seq - 1)


def _attn_lat_body(sink_ref, x_ref, mod, nw_ref, wqkvt_ref, wz_ref, wout_ref, qw_ref, kw_ref,
                   cos_ref, sin_ref, ck_ref, cv_ref, o_ref,
                   q_scr, z_scr, ot_scr, k_scr, vt_scr):
    seq = x_ref.shape[1]
    gate = mod[:, 2 * D_MODEL:]
    nw = nw_ref[...]
    qw = _head_weight_tile(qw_ref, Q_BLOCK)
    kw = _head_weight_tile(kw_ref, Q_BLOCK)
    n_blocks = seq // Q_BLOCK
    kv_blocks = seq // BLOCK
    k_scr[0:BLOCK, :] = jnp.zeros((BLOCK, KV_W), BF16)
    k_scr[BLOCK + seq:2 * BLOCK + seq, :] = jnp.zeros((BLOCK, KV_W), BF16)
    vt_scr[0] = jnp.zeros((KV_W, BLOCK), BF16)
    vt_scr[kv_blocks + 1] = jnp.zeros((KV_W, BLOCK), BF16)

    def project(c):
        rows = slice(c * Q_BLOCK, (c + 1) * Q_BLOCK)
        h = _mod_norm(x_ref[0, rows, :], nw, mod).astype(BF16)
        z_scr[rows, :] = _dot(h, wz_ref[...])
        qkvt = _dot_nt(wqkvt_ref[...], h)
        cos = cos_ref[:, rows]
        sin = sin_ref[:, rows]
        for hd in range(N_HEADS):
            hr = slice(hd * HEAD_DIM, (hd + 1) * HEAD_DIM)
            t = _rope_t(_head_rms(qkvt[hr], qw), cos, sin)
            t = (t * (HEAD_DIM ** -0.5 * LOG2E)).astype(BF16)
            for j in range(HALVES):
                q_scr[c * HALVES + j, hd // 2, :, (hd % 2) * BLOCK:(hd % 2 + 1) * BLOCK] = (
                    t[:, j * BLOCK:(j + 1) * BLOCK])
        knt = jnp.concatenate(
            [_rope_t(_head_rms(qkvt[BRANCH + g * HEAD_DIM:BRANCH + (g + 1) * HEAD_DIM], kw), cos, sin)
             for g in range(N_KV)], axis=0)
        k_scr[BLOCK + c * Q_BLOCK:BLOCK + (c + 1) * Q_BLOCK, :] = knt.T.astype(BF16)
        vt = qkvt[BRANCH + KV_W:].astype(BF16)
        for j in range(HALVES):
            vt_scr[1 + c * HALVES + j] = vt[:, j * BLOCK:(j + 1) * BLOCK]

    ckb = ck_ref[0].T.astype(BF16)
    cvt = cv_ref[0].astype(BF16)

    win_len = 3 * BLOCK
    n_pairs = N_HEADS // 2
    kj = lax.broadcasted_iota(jnp.int32, (win_len, 2 * BLOCK), 0)
    lane = lax.broadcasted_iota(jnp.int32, (win_len, 2 * BLOCK), 1)
    rel = kj - BLOCK - lane % BLOCK
    band = (rel >= -WINDOW) & (rel <= WINDOW)
    first_head = lax.broadcasted_iota(jnp.int32, (1, 2 * BLOCK), 1) < BLOCK

    def half_operands(hb):
        r0 = pl.multiple_of(hb * BLOCK, BLOCK)
        kpos = kj + (r0 - BLOCK)
        valid = jnp.where(band & (kpos >= 0) & (kpos < seq), 0.0, NEG_INF)
        kwin = k_scr[pl.ds(r0, win_len), :]
        vall = jnp.concatenate([vt_scr[hb + j] for j in range(win_len // BLOCK)] + [cvt], axis=1)
        valls = [_with_ones_rows(vall[g * HEAD_DIM:(g + 1) * HEAD_DIM]) for g in range(N_KV)]
        return valid, kwin, valls

    def pair_sink2(pair):
        return jnp.where(first_head, sink_ref[2 * pair], sink_ref[2 * pair + 1]) * LOG2E

    def attend(n, carry):
        halves = [half_operands(n * HALVES + j) for j in range(HALVES)]

        def scores_fn(u):
            j, pair = divmod(u, n_pairs)
            valid, kwin, _ = halves[j]
            return _head_scores(q_scr[n * HALVES + j, pair], pair // (GQA // 2), [kwin, ckb], [valid, None])

        def probs_fn(u, sc):
            return _head_probs(sc, pair_sink2(u % n_pairs))

        def pv_fn(u, pr):
            j, pair = divmod(u, n_pairs)
            o = _head_pv(pr, halves[j][2][pair // (GQA // 2)], pair_sink2(pair))
            for i in range(2):
                hd = 2 * pair + i
                ot_scr[n, hd * HEAD_DIM:(hd + 1) * HEAD_DIM, j * BLOCK:(j + 1) * BLOCK] = (
                    o[:, i * BLOCK:(i + 1) * BLOCK])

        _attend_heads(HALVES * n_pairs, GQA, scores_fn, probs_fn, pv_fn)
        return carry

    for c in range(n_blocks):
        project(c)
    lax.fori_loop(0, n_blocks, attend, 0)
    for c in range(n_blocks):
        rows = slice(c * Q_BLOCK, (c + 1) * Q_BLOCK)
        o_ref[0, rows, :] = _gate_out(x_ref[0, rows, :], ot_scr[c].T, z_scr[rows, :], gate, wout_ref[...])


def _rope_tables_t(seq):
    pos = np.arange(seq)
    n_freq = HEAD_DIM // 4
    inv = ROPE_THETA ** (-np.arange(n_freq, dtype=np.float64) / n_freq)
    ang = np.concatenate([(pos // GRID_W)[:, None] * inv, (pos % GRID_W)[:, None] * inv], axis=-1)
    return np.cos(ang).T.astype(np.float32), np.sin(ang).T.astype(np.float32)


def _attn_layer_kernel(sink_ref, xc_ref, xl_ref, mod_ref, nw_ref, wqkvt_ref, wz_ref, wout_ref, qw_ref, kw_ref,
                       cos_ref, sin_ref, ck_ref, cv_ref, oc_ref, kto_ref, vto_ref, ol_ref,
                       qkvt_scr, q_scr, z_scr, ot_scr, k_scr, vt_scr, *, n_ctx_steps):
    step = pl.program_id(0)

    @pl.when(step < n_ctx_steps)
    def _():
        _attn_ctx_body(sink_ref, xc_ref, mod_ref[0:1, :], nw_ref, wqkvt_ref, wz_ref, wout_ref, qw_ref, kw_ref,
                       oc_ref, kto_ref, vto_ref, qkvt_scr, z_scr, ot_scr)

    @pl.when(step >= n_ctx_steps)
    def _():
        mod = mod_ref[pl.ds(1 + step - n_ctx_steps, 1), :]
        _attn_lat_body(sink_ref, xl_ref, mod, nw_ref, wqkvt_ref, wz_ref, wout_ref, qw_ref, kw_ref,
                       cos_ref, sin_ref, ck_ref, cv_ref, ol_ref, q_scr, z_scr, ot_scr, k_scr, vt_scr)


def _attn_layer(x_ctx, x_lat, mod, nw, wqkvt, wz, wout, qw, kw, sink, ckt, cvt):
    nb_ctx, seq_ctx, _ = x_ctx.shape
    nb_lat, seq_lat, _ = x_lat.shape
    past = ckt.shape[2]
    n_seq = CTX_SEQS_PER_STEP
    assert nb_ctx % n_seq == 0 and seq_ctx == Q_BLOCK and n_seq <= seq_lat // Q_BLOCK
    n_ctx = nb_ctx // n_seq
    cos, sin = (jnp.asarray(t) for t in _rope_tables_t(seq_lat))

    def ctx_step(i):
        return jnp.minimum(i, n_ctx - 1)

    def lat_step(i):
        return jnp.maximum(i - n_ctx, 0)

    return pl.pallas_call(
        functools.partial(_attn_layer_kernel, n_ctx_steps=n_ctx),
        grid=(n_ctx + nb_lat,),
        in_specs=[
            pl.BlockSpec(memory_space=pltpu.SMEM),
            pl.BlockSpec((n_seq, seq_ctx, D_MODEL), lambda i: (ctx_step(i), 0, 0)),
            pl.BlockSpec((1, seq_lat, D_MODEL), lambda i: (lat_step(i), 0, 0)),
            MOD_SPEC,
            _const_spec((1, D_MODEL)),
            _const_spec((BRANCH + 2 * KV_W, D_MODEL)),
            _const_spec((D_MODEL, BRANCH)),
            _const_spec((BRANCH, D_MODEL)),
            _const_spec((1, HEAD_DIM)),
            _const_spec((1, HEAD_DIM)),
            _const_spec((HEAD_DIM // 2, seq_lat)),
            _const_spec((HEAD_DIM // 2, seq_lat)),
            pl.BlockSpec((1, KV_W, past), lambda i: (lat_step(i), 0, 0)),
            pl.BlockSpec((1, KV_W, past), lambda i: (lat_step(i), 0, 0)),
        ],
        out_specs=[
            pl.BlockSpec((n_seq, seq_ctx, D_MODEL), lambda i: (ctx_step(i), 0, 0)),
            pl.BlockSpec((n_seq, KV_W, seq_ctx), lambda i: (ctx_step(i), 0, 0)),
            pl.BlockSpec((n_seq, KV_W, seq_ctx), lambda i: (ctx_step(i), 0, 0)),
            pl.BlockSpec((1, seq_lat, D_MODEL), lambda i: (lat_step(i), 0, 0)),
        ],
        out_shape=[
            jax.ShapeDtypeStruct(x_ctx.shape, F32),
            jax.ShapeDtypeStruct((nb_ctx, KV_W, seq_ctx), F32),
            jax.ShapeDtypeStruct((nb_ctx, KV_W, seq_ctx), F32),
            jax.ShapeDtypeStruct(x_lat.shape, F32),
        ],
        scratch_shapes=[
            pltpu.VMEM((n_seq, BRANCH + 2 * KV_W, seq_ctx), F32),
            pltpu.VMEM((seq_lat // BLOCK, N_HEADS // 2, HEAD_DIM, 2 * BLOCK), BF16),
            pltpu.VMEM((seq_lat, BRANCH), F32),
            pltpu.VMEM((seq_lat // Q_BLOCK, BRANCH, Q_BLOCK), F32),
            pltpu.VMEM((seq_lat + 2 * BLOCK, KV_W), BF16),
            pltpu.VMEM((seq_lat // BLOCK + 2, KV_W, BLOCK), BF16),
        ],
        compiler_params=pltpu.CompilerParams(
            dimension_semantics=("arbitrary",), vmem_limit_bytes=VMEM_LIMIT),
        name="attn_layer",
    )(sink, x_ctx, x_lat, mod, nw, wqkvt, wz, wout, qw, kw, cos, sin, ckt, cvt)


def kernel(x_prompt, x_sample, cache_k_l1, cache_v_l1, c, c_ctx, norm_w_l0, w_mod_l0, b_mod_l0,
           w_in_l0, w_out_l0, norm_w_l1, w_mod_l1, b_mod_l1, w_in_l1, q_norm_w_l1, k_norm_w_l1,
           sink_l1, w_out_l1):
    nb_ctx, seq_ctx, _ = x_prompt.shape
    nb_lat = x_sample.shape[0]
    past = cache_k_l1.shape[1]
    assert 1 + nb_lat <= MOD_ROWS
    nw0 = norm_w_l0.reshape(1, D_MODEL)
    nw1 = norm_w_l1.reshape(1, D_MODEL)
    qw = q_norm_w_l1.reshape(1, HEAD_DIM)
    kw = k_norm_w_l1.reshape(1, HEAD_DIM)

    xp, xs, mod1, wqkvt1, wz1, wout1 = _fourier_layer(
        x_prompt, x_sample, nw0,
        this_layer=(c_ctx, c, w_mod_l0, b_mod_l0, w_in_l0, w_out_l0),
        next_layer=(c_ctx, c, w_mod_l1, b_mod_l1, w_in_l1, w_out_l1))

    def to_feature_major(t):
        return jnp.transpose(t, (0, 2, 3, 1)).reshape(t.shape[0], KV_W, t.shape[1])

    def from_feature_major(t):
        return jnp.transpose(t.reshape(t.shape[0], N_KV, HEAD_DIM, t.shape[2]), (0, 3, 1, 2))

    xp, new_kt, new_vt, xs = _attn_layer(xp, xs, mod1, nw1, wqkvt1, wz1, wout1, qw, kw, sink_l1,
                                         to_feature_major(cache_k_l1), to_feature_major(cache_v_l1))
    return (xp, xs, from_feature_major(new_kt), from_feature_major(new_vt))
```

```python
import functools

import numpy as np
import jax
import jax.numpy as jnp
from jax import lax
from jax.experimental import pallas as pl
from jax.experimental.pallas import tpu as pltpu

D_MODEL = 1024
BRANCH = 1024
N_GROUPS = 4
GROUP_W = BRANCH // N_GROUPS
HALF_W = GROUP_W // 2
HEAD_DIM = 64
N_HEADS = 16
N_KV = 4
GQA = N_HEADS // N_KV
KV_W = N_KV * HEAD_DIM
GRID_W = 64
WINDOW = 128
BLOCK = 128
ROPE_THETA = 10000.0
EPS = 1e-6
NEG_INF = -1e30
LANES = 128
ROW_CHUNK = 256
Q_BLOCK = 256
HALVES = Q_BLOCK // BLOCK
CTX_SEQS_PER_STEP = 2
VMEM_LIMIT = 56 * 1024 * 1024
FOURIER_VMEM_LIMIT = 60 * 1024 * 1024
MOD_ROWS = 8
ONES_ROWS = 16
LOG2E = float(np.log2(np.e))

F32 = jnp.float32
BF16 = jnp.bfloat16


def _dot(a, b):
    return jnp.dot(a, b, preferred_element_type=F32)


def _dot_nt(a, b):
    return lax.dot_general(a, b, (((1,), (1,)), ((), ())), preferred_element_type=F32)


MOD_SPEC = pl.BlockSpec((MOD_ROWS, 3 * D_MODEL), lambda b: (0, 0))


def _mod_norm(x, nw, mod):
    shift = mod[:, :D_MODEL]
    scale = mod[:, D_MODEL:2 * D_MODEL]
    y = x * lax.rsqrt(jnp.mean(x * x, axis=-1, keepdims=True) + EPS)
    return (y * nw) * (1.0 + scale) + shift


def _mod_accumulate(cctx_ref, c_ref, w_ref, b_ref, o_ref, cond_scr, first_step):
    n_lat = c_ref.shape[0]
    cond_scr[...] = jnp.zeros_like(cond_scr)
    cond_scr[0:1, :] = cctx_ref[...]
    cond_scr[1:1 + n_lat, :] = c_ref[...]
    s = jax.nn.silu(cond_scr[...]).astype(BF16)

    @pl.when(pl.program_id(0) == first_step)
    def _():
        o_ref[...] = jnp.broadcast_to(b_ref[...], o_ref.shape)

    o_ref[...] += _dot(s, w_ref[...].astype(BF16))


def _mirror_perm():
    j = np.arange(GROUP_W)
    return np.where(j <= HALF_W, j, GROUP_W + HALF_W - j)


def _perm_matrices():
    perm = _mirror_perm()
    pm = (np.arange(GROUP_W)[:, None] == perm[None, :]).astype(np.float32)
    assert (pm[:HALF_W, HALF_W:] == 0).all() and (pm[HALF_W:, :HALF_W] == 0).all()
    return pm, np.stack([pm[:HALF_W, :HALF_W], pm[HALF_W:, HALF_W:]])


def _fourier_layer_kernel(xc_ref, xl_ref, nw_ref, m1_ref, csc_ref, ssc_ref, csl_ref, ssl_ref,
                          cctx0_ref, c0_ref, wmod0_ref, bmod0_ref, win0_ref, wout0_ref, pm_ref, pmh_ref,
                          cctx_ref, c_ref, wmod1_ref, bmod1_ref, win1_ref, wout1_ref,
                          oc_ref, ol_ref, mod1_ref, wqkvt1_ref, wz1_ref, wout1b_ref,
                          mod0_scr, win_scr, wout_scr, ta_scr, tb_scr, tr_scr, z_scr, cond_scr,
                          *, n_prep_steps, n_ctx_steps):
    step = pl.program_id(0)

    @pl.when(step < n_prep_steps)
    def _():
        _mod_accumulate(cctx0_ref, c0_ref, wmod0_ref, bmod0_ref, mod0_scr, cond_scr, 0)
        rows = pl.ds(pl.multiple_of(step * HALF_W, HALF_W), HALF_W)
        w = win0_ref[...].astype(BF16)
        win_scr[rows, :BRANCH] = w[:, :BRANCH]
        pm = pm_ref[...]
        for g in range(N_GROUPS):
            cols = slice(BRANCH + g * GROUP_W, BRANCH + (g + 1) * GROUP_W)
            win_scr[rows, cols] = _dot(w[:, cols], pm).astype(BF16)
        wout_scr[rows, :] = _dot(pmh_ref[step % 2], wout0_ref[...].astype(BF16)).astype(BF16)

    @pl.when(jnp.logical_and(step >= n_prep_steps, step < n_prep_steps + n_ctx_steps))
    def _():
        _mod_accumulate(cctx_ref, c_ref, wmod1_ref, bmod1_ref, mod1_ref, cond_scr, n_prep_steps)
        w = win1_ref[...]
        wqkvt1_ref[...] = w[:, :BRANCH + 2 * KV_W].T.astype(BF16)
        wz1_ref[...] = w[:, BRANCH + 2 * KV_W:].astype(BF16)
        wout1b_ref[...] = wout1_ref[...].astype(BF16)
        _fourier_body(xc_ref, oc_ref, mod0_scr[0:1, :], nw_ref, win_scr, wout_scr, m1_ref, csc_ref, ssc_ref,
                      ta_scr, tb_scr, tr_scr, z_scr)

    @pl.when(step >= n_prep_steps + n_ctx_steps)
    def _():
        mod = mod0_scr[pl.ds(1 + step - n_prep_steps - n_ctx_steps, 1), :]
        _fourier_body(xl_ref, ol_ref, mod, nw_ref, win_scr, wout_scr, m1_ref, csl_ref, ssl_ref,
                      ta_scr, tb_scr, tr_scr, z_scr)


def _fourier_body(x_ref, o_ref, mod, nw_ref, win_ref, wout_ref, m1_ref, cs_ref, ss_ref,
                  ta_scr, tb_scr, tr_scr, z_scr):
    gate = mod[:, 2 * D_MODEL:]
    nw = nw_ref[...]
    n_seq, seq, _ = x_ref.shape
    n_chunks = seq // ROW_CHUNK
    lane = lax.broadcasted_iota(jnp.int32, (ROW_CHUNK, HALF_W), 1)
    for i in range(n_seq):
        for c in range(n_chunks):
            rows = slice(c * ROW_CHUNK, (c + 1) * ROW_CHUNK)
            srows = slice(i * seq + c * ROW_CHUNK, i * seq + (c + 1) * ROW_CHUNK)
            h = _mod_norm(x_ref[i, rows, :], nw, mod).astype(BF16)
            uz = _dot(h, win_ref[...])
            z_scr[srows, :] = uz[:, BRANCH:]
            u = uz[:, :BRANCH].astype(BF16)
            tr = jnp.zeros((ROW_CHUNK, HALF_W), F32)
            for g in range(N_GROUPS):
                t = _dot(u[:, g * GROUP_W:(g + 1) * GROUP_W], m1_ref[...])
                half = slice(g * HALF_W, (g + 1) * HALF_W)
                ta_scr[srows, half] = t[:, :HALF_W].astype(BF16)
                tb = t[:, HALF_W:]
                tb_scr[srows, half] = tb.astype(BF16)
                tr = jnp.where(lane == g, tb if g == 0 else pltpu.roll(tb, g, axis=1), tr)
            tr_scr[srows, :] = tr.astype(BF16)
    for i in range(n_seq):
        seq_rows = slice(i * seq, (i + 1) * seq)
        for c in range(n_chunks):
            rows = slice(c * ROW_CHUNK, (c + 1) * ROW_CHUNK)
            srows = slice(i * seq + c * ROW_CHUNK, i * seq + (c + 1) * ROW_CHUNK)
            cs = cs_ref[rows, :]
            p = _dot(cs, ta_scr[seq_rows, :])
            q = _dot(ss_ref[rows, :], tb_scr[seq_rows, :])
            r = _dot(cs, tr_scr[seq_rows, :])
            parts = []
            for g in range(N_GROUPS):
                half = slice(g * HALF_W, (g + 1) * HALF_W)
                pg, qg = p[:, half], q[:, half]
                rg = r if g == 0 else pltpu.roll(r, HALF_W - g, axis=1)
                parts.append(jnp.where(lane == 0, pg, pg - qg))
                parts.append(jnp.where(lane == 0, rg, pg + qg))
            y = jnp.concatenate(parts, axis=1)
            y = (y * jax.nn.silu(z_scr[srows, :])).astype(BF16)
            o_ref[i, rows, :] = x_ref[i, rows, :] + gate * _dot(y, wout_ref[...])


def _dft_tables(seq):
    c = np.arange(GROUP_W)[:, None]
    k = np.arange(HALF_W)[None, :]
    cos_lo = np.cos(2.0 * np.pi * ((c * k) % GROUP_W) / GROUP_W)
    sin_lo = np.sin(2.0 * np.pi * ((c * k) % GROUP_W) / GROUP_W)
    sin_lo[:, 0] = np.cos(np.pi * c[:, 0])
    m1 = np.concatenate([cos_lo, sin_lo], axis=1) / np.sqrt(GROUP_W)
    n = np.arange(seq)
    ang = 2.0 * np.pi * ((n[:, None] * n[None, :]) % seq) / seq
    cs = np.cos(ang) / np.sqrt(seq)
    ss = np.sin(ang) / np.sqrt(seq)
    return m1.astype(np.float32), cs.astype(np.float32), ss.astype(np.float32)


def _const_spec(shape):
    return pl.BlockSpec(shape, lambda b: (0,) * len(shape))


def _fourier_layer(x_ctx, x_lat, nw, this_layer, next_layer):
    nb_ctx, seq_ctx, _ = x_ctx.shape
    nb_lat, seq_lat, _ = x_lat.shape
    n_seq = CTX_SEQS_PER_STEP
    assert nb_ctx % n_seq == 0
    n_prep = D_MODEL // HALF_W
    n_ctx = nb_ctx // n_seq
    assert D_MODEL % (n_ctx * LANES) == 0 and n_seq * seq_ctx <= seq_lat
    rows = D_MODEL // n_ctx
    assert rows == HALF_W
    c_ctx, c, w_mod0, b_mod0, w_in0, w_out0 = this_layer
    _, _, w_mod1, b_mod1, w_in1, w_out1 = next_layer
    n_lat = c.shape[0]
    n_qkvz = 2 * BRANCH + 2 * KV_W
    m1, csc, ssc = (jnp.asarray(t).astype(BF16) for t in _dft_tables(seq_ctx))
    _, csl, ssl = (jnp.asarray(t).astype(BF16) for t in _dft_tables(seq_lat))
    pm, pmh = (jnp.asarray(t).astype(BF16) for t in _perm_matrices())
    cc = c_ctx.reshape(1, D_MODEL)

    def prep_step(i):
        return jnp.minimum(i, n_prep - 1)

    def ctx_step(i):
        return jnp.clip(i - n_prep, 0, n_ctx - 1)

    def lat_step(i):
        return jnp.maximum(i - n_prep - n_ctx, 0)

    def chunk_specs(step_fn, w_in_cols):
        return [pl.BlockSpec((1, HALF_W), lambda i: (0, step_fn(i))),
                pl.BlockSpec((n_lat, HALF_W), lambda i: (0, step_fn(i))),
                pl.BlockSpec((HALF_W, 3 * D_MODEL), lambda i: (step_fn(i), 0)),
                _const_spec((1, 3 * D_MODEL)),
                pl.BlockSpec((HALF_W, w_in_cols), lambda i: (step_fn(i), 0)),
                pl.BlockSpec((HALF_W, D_MODEL), lambda i: (step_fn(i), 0))]

    in_specs = [
        pl.BlockSpec((n_seq, seq_ctx, D_MODEL), lambda i: (ctx_step(i), 0, 0)),
        pl.BlockSpec((1, seq_lat, D_MODEL), lambda i: (lat_step(i), 0, 0)),
        _const_spec((1, D_MODEL)),
        _const_spec((GROUP_W, GROUP_W)),
        _const_spec((seq_ctx, seq_ctx)),
        _const_spec((seq_ctx, seq_ctx)),
        _const_spec((seq_lat, seq_lat)),
        _const_spec((seq_lat, seq_lat)),
    ] + chunk_specs(prep_step, 2 * BRANCH) + [
        _const_spec((GROUP_W, GROUP_W)),
        _const_spec((2, HALF_W, HALF_W)),
    ] + chunk_specs(ctx_step, n_qkvz)
    out_specs = [
        pl.BlockSpec((n_seq, seq_ctx, D_MODEL), lambda i: (ctx_step(i), 0, 0)),
        pl.BlockSpec((1, seq_lat, D_MODEL), lambda i: (lat_step(i), 0, 0)),
        MOD_SPEC,
        pl.BlockSpec((BRANCH + 2 * KV_W, rows), lambda i: (0, ctx_step(i))),
        pl.BlockSpec((rows, BRANCH), lambda i: (ctx_step(i), 0)),
        pl.BlockSpec((rows, D_MODEL), lambda i: (ctx_step(i), 0)),
    ]
    out_shape = [
        jax.ShapeDtypeStruct(x_ctx.shape, F32),
        jax.ShapeDtypeStruct(x_lat.shape, F32),
        jax.ShapeDtypeStruct((MOD_ROWS, 3 * D_MODEL), F32),
        jax.ShapeDtypeStruct((BRANCH + 2 * KV_W, D_MODEL), BF16),
        jax.ShapeDtypeStruct((D_MODEL, BRANCH), BF16),
        jax.ShapeDtypeStruct((BRANCH, D_MODEL), BF16),
    ]
    return pl.pallas_call(
        functools.partial(_fourier_layer_kernel, n_prep_steps=n_prep, n_ctx_steps=n_ctx),
        grid=(n_prep + n_ctx + nb_lat,),
        in_specs=in_specs,
        out_specs=out_specs,
        out_shape=out_shape,
        scratch_shapes=[
            pltpu.VMEM((MOD_ROWS, 3 * D_MODEL), F32),
            pltpu.VMEM((D_MODEL, 2 * BRANCH), BF16),
            pltpu.VMEM((BRANCH, D_MODEL), BF16),
            pltpu.VMEM((seq_lat, N_GROUPS * HALF_W), BF16),
            pltpu.VMEM((seq_lat, N_GROUPS * HALF_W), BF16),
            pltpu.VMEM((seq_lat, HALF_W), BF16),
            pltpu.VMEM((seq_lat, BRANCH), F32),
            pltpu.VMEM((MOD_ROWS, HALF_W), F32),
        ],
        compiler_params=pltpu.CompilerParams(
            dimension_semantics=("arbitrary",), vmem_limit_bytes=FOURIER_VMEM_LIMIT),
        name="fourier_layer",
    )(x_ctx, x_lat, nw, m1, csc, ssc, csl, ssl,
      cc, c, w_mod0, b_mod0.reshape(1, 3 * D_MODEL), w_in0, w_out0, pm, pmh,
      cc, c, w_mod1, b_mod1.reshape(1, 3 * D_MODEL), w_in1, w_out1)


def _head_weight_tile(w_ref, n_tokens):
    row = jnp.broadcast_to(w_ref[...], (HEAD_DIM, HEAD_DIM))
    ii = lax.broadcasted_iota(jnp.int32, (HEAD_DIM, HEAD_DIM), 0)
    jj = lax.broadcasted_iota(jnp.int32, (HEAD_DIM, HEAD_DIM), 1)
    col = jnp.sum(jnp.where(ii == jj, row, 0.0), axis=1, keepdims=True)
    return jnp.broadcast_to(col, (HEAD_DIM, n_tokens))


def _head_rms(t, w):
    return (t * lax.rsqrt(jnp.mean(t * t, axis=0, keepdims=True) + EPS)) * w


def _rope_t(t, cos, sin):
    half = HEAD_DIM // 2
    x1, x2 = t[:half], t[half:]
    return jnp.concatenate([x1 * cos - x2 * sin, x1 * sin + x2 * cos], axis=0)


def _head_scores(qn, g, keys, biases):
    zeros = jnp.zeros_like(qn)
    qz = jnp.concatenate([qn, zeros] if g % 2 == 0 else [zeros, qn], axis=0)
    blk = slice((g // 2) * LANES, (g // 2 + 1) * LANES)
    scores = []
    smax = None
    for k, bias in zip(keys, biases):
        s = _dot(k[:, blk], qz)
        if bias is not None:
            s = jnp.concatenate([s[c * BLOCK:(c + 1) * BLOCK] if b is None else s[c * BLOCK:(c + 1) * BLOCK] + b
                                 for c, b in enumerate(bias)], axis=0)
        cmax = jnp.max(s, axis=0, keepdims=True)
        smax = cmax if smax is None else jnp.maximum(smax, cmax)
        scores.append(s)
    return scores, smax


def _with_ones_rows(vt):
    return jnp.concatenate([vt, jnp.ones((ONES_ROWS, vt.shape[1]), vt.dtype)], axis=0)


def _head_probs(scored, sink2):
    scores, smax = scored
    m = jnp.maximum(smax, sink2)
    return jnp.concatenate([jnp.exp2(s - m).astype(BF16) for s in scores], axis=0), m


def _head_pv(probs, values_t, sink2):
    p, m = probs
    acc = _dot(values_t, p)
    den = acc[HEAD_DIM:HEAD_DIM + 1] + jnp.exp2(sink2 - m)
    return acc[:HEAD_DIM] * (1.0 / den)


def _attend_heads(n_units, stage, scores_fn, probs_fn, pv_fn, fillers=()):
    n_stages = n_units // stage
    pending = [scores_fn(u) for u in range(stage)]
    for g in range(n_stages):
        units = range(g * stage, (g + 1) * stage)
        if g < len(fillers):
            fillers[g]()
        nxt = [scores_fn(u) for u in range((g + 1) * stage, (g + 2) * stage)] if g + 1 < n_stages else None
        probs = [probs_fn(u, sc) for u, sc in zip(units, pending)]
        for u, pr in zip(units, probs):
            pv_fn(u, pr)
        pending = nxt


def _gate_out(x, o, z, gate, wout):
    y = (o * jax.nn.silu(z)).astype(BF16)
    return x + gate * _dot(y, wout)


def _attn_ctx_body(sink_ref, x_ref, mod, nw_ref, wqkvt_ref, wz_ref, wout_ref, qw_ref, kw_ref,
                   o_ref, kto_ref, vto_ref, qkvt_scr, z_scr, ot_scr):
    seq = x_ref.shape[1]
    n_seq = x_ref.shape[0]
    gate = mod[:, 2 * D_MODEL:]
    kw = _head_weight_tile(kw_ref, seq)
    qw = _head_weight_tile(qw_ref, seq)

    def project(i):
        h = _mod_norm(x_ref[i], nw_ref[...], mod).astype(BF16)
        qkvt_scr[i] = _dot_nt(wqkvt_ref[...], h)
        z_scr[i * seq:(i + 1) * seq, :] = _dot(h, wz_ref[...])

    def keys_values(i):
        knt = jnp.concatenate(
            [_head_rms(qkvt_scr[i, BRANCH + g * HEAD_DIM:BRANCH + (g + 1) * HEAD_DIM, :], kw)
             for g in range(N_KV)], axis=0)
        kto_ref[i] = knt
        vtf = qkvt_scr[i, BRANCH + KV_W:, :]
        vto_ref[i] = vtf
        vt = vtf.astype(BF16)
        return (knt.T.astype(BF16),
                [_with_ones_rows(vt[g * HEAD_DIM:(g + 1) * HEAD_DIM]) for g in range(N_KV)])

    def output(i):
        o_ref[i] = _gate_out(x_ref[i], ot_scr[i].T, z_scr[i * seq:(i + 1) * seq, :], gate, wout_ref[...])

    for i in range(n_seq):
        project(i)
    kv = [keys_values(i) for i in range(n_seq)]

    def scores_fn(u):
        i, hd = divmod(u, N_HEADS)
        t = qkvt_scr[i, hd * HEAD_DIM:(hd + 1) * HEAD_DIM, :]
        qn = (_head_rms(t, qw) * (HEAD_DIM ** -0.5 * LOG2E)).astype(BF16)
        return _head_scores(qn, hd // GQA, [kv[i][0]], [None])

    def probs_fn(u, sc):
        return _head_probs(sc, sink_ref[u % N_HEADS] * LOG2E)

    def pv_fn(u, pr):
        i, hd = divmod(u, N_HEADS)
        ot_scr[i, hd * HEAD_DIM:(hd + 1) * HEAD_DIM, :] = _head_pv(
            pr, kv[i][1][hd // GQA], sink_ref[hd] * LOG2E)

    fillers = [lambda: None] + [functools.partial(output, i) for i in range(n_seq - 1)]
    _attend_heads(n_seq * N_HEADS, N_HEADS, scores_fn, probs_fn, pv_fn, fillers)
    output(n_seq - 1)


def _attn_lat_body(sink_ref, x_ref, mod, nw_ref, wqkvt_ref, wz_ref, wout_ref, qw_ref, kw_ref,
                   cos_ref, sin_ref, ck_ref, cv_ref, o_ref,
                   q_scr, z_scr, ot_scr, k_scr, vt_scr):
    seq = x_ref.shape[1]
    gate = mod[:, 2 * D_MODEL:]
    nw = nw_ref[...]
    qw = _head_weight_tile(qw_ref, Q_BLOCK)
    kw = _head_weight_tile(kw_ref, Q_BLOCK)
    n_blocks = seq // Q_BLOCK
    kv_blocks = seq // BLOCK
    k_scr[0:BLOCK, :] = jnp.zeros((BLOCK, KV_W), BF16)
    k_scr[BLOCK + seq:2 * BLOCK + seq, :] = jnp.zeros((BLOCK, KV_W), BF16)
    vt_scr[0] = jnp.zeros((KV_W, BLOCK), BF16)
    vt_scr[kv_blocks + 1] = jnp.zeros((KV_W, BLOCK), BF16)

    def project(c):
        rows = slice(c * Q_BLOCK, (c + 1) * Q_BLOCK)
        h = _mod_norm(x_ref[0, rows, :], nw, mod).astype(BF16)
        z_scr[rows, :] = _dot(h, wz_ref[...])
        qkvt = _dot_nt(wqkvt_ref[...], h)
        cos = cos_ref[:, rows]
        sin = sin_ref[:, rows]
        for hd in range(N_HEADS):
            hr = slice(hd * HEAD_DIM, (hd + 1) * HEAD_DIM)
            t = _rope_t(_head_rms(qkvt[hr], qw), cos, sin)
            t = (t * (HEAD_DIM ** -0.5 * LOG2E)).astype(BF16)
            for j in range(HALVES):
                q_scr[c * HALVES + j, hd // 2, :, (hd % 2) * BLOCK:(hd % 2 + 1) * BLOCK] = (
                    t[:, j * BLOCK:(j + 1) * BLOCK])
        knt = jnp.concatenate(
            [_rope_t(_head_rms(qkvt[BRANCH + g * HEAD_DIM:BRANCH + (g + 1) * HEAD_DIM], kw), cos, sin)
             for g in range(N_KV)], axis=0)
        k_scr[BLOCK + c * Q_BLOCK:BLOCK + (c + 1) * Q_BLOCK, :] = knt.T.astype(BF16)
        vt = qkvt[BRANCH + KV_W:].astype(BF16)
        for j in range(HALVES):
            vt_scr[1 + c * HALVES + j] = vt[:, j * BLOCK:(j + 1) * BLOCK]

    ckb = ck_ref[0].T.astype(BF16)
    cvt = cv_ref[0].astype(BF16)

    win_len = 3 * BLOCK
    n_pairs = N_HEADS // 2
    kj = lax.broadcasted_iota(jnp.int32, (BLOCK, 2 * BLOCK), 0)
    lane = lax.broadcasted_iota(jnp.int32, (BLOCK, 2 * BLOCK), 1)
    first_head = lax.broadcasted_iota(jnp.int32, (1, 2 * BLOCK), 1) < BLOCK
    assert WINDOW >= BLOCK - 1

    def band_bias(c):
        rel = kj + (c - 1) * BLOCK - lane % BLOCK
        return jnp.where((rel >= -WINDOW) & (rel <= WINDOW), 0.0, NEG_INF)

    band_before, band_after = band_bias(0), band_bias(2)

    def half_operands(hb):
        r0 = pl.multiple_of(hb * BLOCK, BLOCK)
        valid = (jnp.where(hb > 0, band_before, NEG_INF), None,
                 jnp.where(hb < kv_blocks - 1, band_after, NEG_INF))
        kwin = k_scr[pl.ds(r0, win_len), :]
        vall = jnp.concatenate([vt_scr[hb + j] for j in range(win_len // BLOCK)] + [cvt], axis=1)
        valls = [_with_ones_rows(vall[g * HEAD_DIM:(g + 1) * HEAD_DIM]) for g in range(N_KV)]
        return valid, kwin, valls

    def pair_sink2(pair):
        return jnp.where(first_head, sink_ref[2 * pair], sink_ref[2 * pair + 1]) * LOG2E

    def attend(n, carry):
        halves = [half_operands(n * HALVES + j) for j in range(HALVES)]

        def scores_fn(u):
            j, pair = divmod(u, n_pairs)
            valid, kwin, _ = halves[j]
            return _head_scores(q_scr[n * HALVES + j, pair], pair // (GQA // 2), [kwin, ckb], [valid, None])

        def probs_fn(u, sc):
            return _head_probs(sc, pair_sink2(u % n_pairs))

        def pv_fn(u, pr):
            j, pair = divmod(u, n_pairs)
            o = _head_pv(pr, halves[j][2][pair // (GQA // 2)], pair_sink2(pair))
            for i in range(2):
                hd = 2 * pair + i
                ot_scr[n, hd * HEAD_DIM:(hd + 1) * HEAD_DIM, j * BLOCK:(j + 1) * BLOCK] = (
                    o[:, i * BLOCK:(i + 1) * BLOCK])

        _attend_heads(HALVES * n_pairs, GQA, scores_fn, probs_fn, pv_fn)
        return carry

    for c in range(n_blocks):
        project(c)
    lax.fori_loop(0, n_blocks, attend, 0)
    for c in range(n_blocks):
        rows = slice(c * Q_BLOCK, (c + 1) * Q_BLOCK)
        o_ref[0, rows, :] = _gate_out(x_ref[0, rows, :], ot_scr[c].T, z_scr[rows, :], gate, wout_ref[...])


def _rope_tables_t(seq):
    pos = np.arange(seq)
    n_freq = HEAD_DIM // 4
    inv = ROPE_THETA ** (-np.arange(n_freq, dtype=np.float64) / n_freq)
    ang = np.concatenate([(pos // GRID_W)[:, None] * inv, (pos % GRID_W)[:, None] * inv], axis=-1)
    return np.cos(ang).T.astype(np.float32), np.sin(ang).T.astype(np.float32)


def _attn_layer_kernel(sink_ref, xc_ref, xl_ref, mod_ref, nw_ref, wqkvt_ref, wz_ref, wout_ref, qw_ref, kw_ref,
                       cos_ref, sin_ref, ck_ref, cv_ref, oc_ref, kto_ref, vto_ref, ol_ref,
                       qkvt_scr, q_scr, z_scr, ot_scr, k_scr, vt_scr, *, n_ctx_steps):
    step = pl.program_id(0)

    @pl.when(step < n_ctx_steps)
    def _():
        _attn_ctx_body(sink_ref, xc_ref, mod_ref[0:1, :], nw_ref, wqkvt_ref, wz_ref, wout_ref, qw_ref, kw_ref,
                       oc_ref, kto_ref, vto_ref, qkvt_scr, z_scr, ot_scr)

    @pl.when(step >= n_ctx_steps)
    def _():
        mod = mod_ref[pl.ds(1 + step - n_ctx_steps, 1), :]
        _attn_lat_body(sink_ref, xl_ref, mod, nw_ref, wqkvt_ref, wz_ref, wout_ref, qw_ref, kw_ref,
                       cos_ref, sin_ref, ck_ref, cv_ref, ol_ref, q_scr, z_scr, ot_scr, k_scr, vt_scr)


def _attn_layer(x_ctx, x_lat, mod, nw, wqkvt, wz, wout, qw, kw, sink, ckt, cvt):
    nb_ctx, seq_ctx, _ = x_ctx.shape
    nb_lat, seq_lat, _ = x_lat.shape
    past = ckt.shape[2]
    n_seq = CTX_SEQS_PER_STEP
    assert nb_ctx % n_seq == 0 and seq_ctx == Q_BLOCK and n_seq <= seq_lat // Q_BLOCK
    n_ctx = nb_ctx // n_seq
    cos, sin = (jnp.asarray(t) for t in _rope_tables_t(seq_lat))

    def ctx_step(i):
        return jnp.minimum(i, n_ctx - 1)

    def lat_step(i):
        return jnp.maximum(i - n_ctx, 0)

    return pl.pallas_call(
        functools.partial(_attn_layer_kernel, n_ctx_steps=n_ctx),
        grid=(n_ctx + nb_lat,),
        in_specs=[
            pl.BlockSpec(memory_space=pltpu.SMEM),
            pl.BlockSpec((n_seq, seq_ctx, D_MODEL), lambda i: (ctx_step(i), 0, 0)),
            pl.BlockSpec((1, seq_lat, D_MODEL), lambda i: (lat_step(i), 0, 0)),
            MOD_SPEC,
            _const_spec((1, D_MODEL)),
            _const_spec((BRANCH + 2 * KV_W, D_MODEL)),
            _const_spec((D_MODEL, BRANCH)),
            _const_spec((BRANCH, D_MODEL)),
            _const_spec((1, HEAD_DIM)),
            _const_spec((1, HEAD_DIM)),
            _const_spec((HEAD_DIM // 2, seq_lat)),
            _const_spec((HEAD_DIM // 2, seq_lat)),
            pl.BlockSpec((1, KV_W, past), lambda i: (lat_step(i), 0, 0)),
            pl.BlockSpec((1, KV_W, past), lambda i: (lat_step(i), 0, 0)),
        ],
        out_specs=[
            pl.BlockSpec((n_seq, seq_ctx, D_MODEL), lambda i: (ctx_step(i), 0, 0)),
            pl.BlockSpec((n_seq, KV_W, seq_ctx), lambda i: (ctx_step(i), 0, 0)),
            pl.BlockSpec((n_seq, KV_W, seq_ctx), lambda i: (ctx_step(i), 0, 0)),
            pl.BlockSpec((1, seq_lat, D_MODEL), lambda i: (lat_step(i), 0, 0)),
        ],
        out_shape=[
            jax.ShapeDtypeStruct(x_ctx.shape, F32),
            jax.ShapeDtypeStruct((nb_ctx, KV_W, seq_ctx), F32),
            jax.ShapeDtypeStruct((nb_ctx, KV_W, seq_ctx), F32),
            jax.ShapeDtypeStruct(x_lat.shape, F32),
        ],
        scratch_shapes=[
            pltpu.VMEM((n_seq, BRANCH + 2 * KV_W, seq_ctx), F32),
            pltpu.VMEM((seq_lat // BLOCK, N_HEADS // 2, HEAD_DIM, 2 * BLOCK), BF16),
            pltpu.VMEM((seq_lat, BRANCH), F32),
            pltpu.VMEM((seq_lat // Q_BLOCK, BRANCH, Q_BLOCK), F32),
            pltpu.VMEM((seq_lat + 2 * BLOCK, KV_W), BF16),
            pltpu.VMEM((seq_lat // BLOCK + 2, KV_W, BLOCK), BF16),
        ],
        compiler_params=pltpu.CompilerParams(
            dimension_semantics=("arbitrary",), vmem_limit_bytes=VMEM_LIMIT),
        name="attn_layer",
    )(sink, x_ctx, x_lat, mod, nw, wqkvt, wz, wout, qw, kw, cos, sin, ckt, cvt)


def kernel(x_prompt, x_sample, cache_k_l1, cache_v_l1, c, c_ctx, norm_w_l0, w_mod_l0, b_mod_l0,
           w_in_l0, w_out_l0, norm_w_l1, w_mod_l1, b_mod_l1, w_in_l1, q_norm_w_l1, k_norm_w_l1,
           sink_l1, w_out_l1):
    nb_ctx, seq_ctx, _ = x_prompt.shape
    nb_lat = x_sample.shape[0]
    past = cache_k_l1.shape[1]
    assert 1 + nb_lat <= MOD_ROWS
    nw0 = norm_w_l0.reshape(1, D_MODEL)
    nw1 = norm_w_l1.reshape(1, D_MODEL)
    qw = q_norm_w_l1.reshape(1, HEAD_DIM)
    kw = k_norm_w_l1.reshape(1, HEAD_DIM)

    xp, xs, mod1, wqkvt1, wz1, wout1 = _fourier_layer(
        x_prompt, x_sample, nw0,
        this_layer=(c_ctx, c, w_mod_l0, b_mod_l0, w_in_l0, w_out_l0),
        next_layer=(c_ctx, c, w_mod_l1, b_mod_l1, w_in_l1, w_out_l1))

    def to_feature_major(t):
        return jnp.transpose(t, (0, 2, 3, 1)).reshape(t.shape[0], KV_W, t.shape[1])

    def from_feature_major(t):
        return jnp.transpose(t.reshape(t.shape[0], N_KV, HEAD_DIM, t.shape[2]), (0, 3, 1, 2))

    xp, new_kt, new_vt, xs = _attn_layer(xp, xs, mod1, nw1, wqkvt1, wz1, wout1, qw, kw, sink_l1,
                                         to_feature_major(cache_k_l1), to_feature_major(cache_v_l1))
    return (xp, xs, from_feature_major(new_kt), from_feature_major(new_vt))
```

```python
import functools

import numpy as np
import jax
import jax.numpy as jnp
from jax import lax
from jax.experimental import pallas as pl
from jax.experimental.pallas import tpu as pltpu

D_MODEL = 1024
BRANCH = 1024
N_GROUPS = 4
GROUP_W = BRANCH // N_GROUPS
HALF_W = GROUP_W // 2
HEAD_DIM = 64
N_HEADS = 16
N_KV = 4
GQA = N_HEADS // N_KV
KV_W = N_KV * HEAD_DIM
GRID_W = 64
WINDOW = 128
BLOCK = 128
ROPE_THETA = 10000.0
EPS = 1e-6
NEG_INF = -1e30
LANES = 128
ROW_CHUNK = 256
Q_BLOCK = 256
HALVES = Q_BLOCK // BLOCK
CTX_SEQS_PER_STEP = 2
VMEM_LIMIT = 56 * 1024 * 1024
FOURIER_VMEM_LIMIT = 60 * 1024 * 1024
MOD_ROWS = 8
ONES_ROWS = 16
LOG2E = float(np.log2(np.e))

F32 = jnp.float32
BF16 = jnp.bfloat16


def _dot(a, b):
    return jnp.dot(a, b, preferred_element_type=F32)


def _dot_nt(a, b):
    return lax.dot_general(a, b, (((1,), (1,)), ((), ())), preferred_element_type=F32)


MOD_SPEC = pl.BlockSpec((MOD_ROWS, 3 * D_MODEL), lambda b: (0, 0))


def _mod_norm(x, nw, mod):
    shift = mod[:, :D_MODEL]
    scale = mod[:, D_MODEL:2 * D_MODEL]
    y = x * lax.rsqrt(jnp.mean(x * x, axis=-1, keepdims=True) + EPS)
    return (y * nw) * (1.0 + scale) + shift


def _mod_accumulate(cctx_ref, c_ref, w_ref, b_ref, o_ref, cond_scr, first_step):
    n_lat = c_ref.shape[0]
    cond_scr[...] = jnp.zeros_like(cond_scr)
    cond_scr[0:1, :] = cctx_ref[...]
    cond_scr[1:1 + n_lat, :] = c_ref[...]
    s = jax.nn.silu(cond_scr[...]).astype(BF16)

    @pl.when(pl.program_id(0) == first_step)
    def _():
        o_ref[...] = jnp.broadcast_to(b_ref[...], o_ref.shape)

    o_ref[...] += _dot(s, w_ref[...].astype(BF16))


def _mirror_perm():
    j = np.arange(GROUP_W)
    return np.where(j <= HALF_W, j, GROUP_W + HALF_W - j)


def _perm_matrices():
    perm = _mirror_perm()
    pm = (np.arange(GROUP_W)[:, None] == perm[None, :]).astype(np.float32)
    assert (pm[:HALF_W, HALF_W:] == 0).all() and (pm[HALF_W:, :HALF_W] == 0).all()
    return pm, np.stack([pm[:HALF_W, :HALF_W], pm[HALF_W:, HALF_W:]])


def _fourier_layer_kernel(xc_ref, xl_ref, nw_ref, m1_ref, csc_ref, ssc_ref, csl_ref, ssl_ref,
                          cctx0_ref, c0_ref, wmod0_ref, bmod0_ref, win0_ref, wout0_ref, pm_ref, pmh_ref,
                          cctx_ref, c_ref, wmod1_ref, bmod1_ref, win1_ref, wout1_ref,
                          oc_ref, ol_ref, mod1_ref, wqkvt1_ref, wz1_ref, wout1b_ref,
                          mod0_scr, win_scr, wout_scr, ta_scr, tb_scr, tr_scr, z_scr, cond_scr,
                          *, n_prep_steps, n_ctx_steps):
    step = pl.program_id(0)

    @pl.when(step < n_prep_steps)
    def _():
        _mod_accumulate(cctx0_ref, c0_ref, wmod0_ref, bmod0_ref, mod0_scr, cond_scr, 0)
        rows = pl.ds(pl.multiple_of(step * HALF_W, HALF_W), HALF_W)
        w = win0_ref[...].astype(BF16)
        win_scr[rows, :BRANCH] = w[:, :BRANCH]
        pm = pm_ref[...]
        for g in range(N_GROUPS):
            cols = slice(BRANCH + g * GROUP_W, BRANCH + (g + 1) * GROUP_W)
            win_scr[rows, cols] = _dot(w[:, cols], pm).astype(BF16)
        wout_scr[rows, :] = _dot(pmh_ref[step % 2], wout0_ref[...].astype(BF16)).astype(BF16)

    @pl.when(jnp.logical_and(step >= n_prep_steps, step < n_prep_steps + n_ctx_steps))
    def _():
        _mod_accumulate(cctx_ref, c_ref, wmod1_ref, bmod1_ref, mod1_ref, cond_scr, n_prep_steps)
        w = win1_ref[...]
        wqkvt1_ref[...] = w[:, :BRANCH + 2 * KV_W].T.astype(BF16)
        wz1_ref[...] = w[:, BRANCH + 2 * KV_W:].astype(BF16)
        wout1b_ref[...] = wout1_ref[...].astype(BF16)
        _fourier_body(xc_ref, oc_ref, mod0_scr[0:1, :], nw_ref, win_scr, wout_scr, m1_ref, csc_ref, ssc_ref,
                      ta_scr, tb_scr, tr_scr, z_scr)

    @pl.when(step >= n_prep_steps + n_ctx_steps)
    def _():
        mod = mod0_scr[pl.ds(1 + step - n_prep_steps - n_ctx_steps, 1), :]
        _fourier_body(xl_ref, ol_ref, mod, nw_ref, win_scr, wout_scr, m1_ref, csl_ref, ssl_ref,
                      ta_scr, tb_scr, tr_scr, z_scr)


def _fourier_body(x_ref, o_ref, mod, nw_ref, win_ref, wout_ref, m1_ref, cs_ref, ss_ref,
                  ta_scr, tb_scr, tr_scr, z_scr):
    gate = mod[:, 2 * D_MODEL:]
    nw = nw_ref[...]
    n_seq, seq, _ = x_ref.shape
    n_chunks = seq // ROW_CHUNK
    lane = lax.broadcasted_iota(jnp.int32, (ROW_CHUNK, HALF_W), 1)
    for i in range(n_seq):
        for c in range(n_chunks):
            rows = slice(c * ROW_CHUNK, (c + 1) * ROW_CHUNK)
            srows = slice(i * seq + c * ROW_CHUNK, i * seq + (c + 1) * ROW_CHUNK)
            h = _mod_norm(x_ref[i, rows, :], nw, mod).astype(BF16)
            uz = _dot(h, win_ref[...])
            z_scr[srows, :] = uz[:, BRANCH:]
            u = uz[:, :BRANCH].astype(BF16)
            tr = jnp.zeros((ROW_CHUNK, HALF_W), F32)
            for g in range(N_GROUPS):
                t = _dot(u[:, g * GROUP_W:(g + 1) * GROUP_W], m1_ref[...])
                half = slice(g * HALF_W, (g + 1) * HALF_W)
                ta_scr[srows, half] = t[:, :HALF_W].astype(BF16)
                tb = t[:, HALF_W:]
                tb_scr[srows, half] = tb.astype(BF16)
                tr = jnp.where(lane == g, tb if g == 0 else pltpu.roll(tb, g, axis=1), tr)
            tr_scr[srows, :] = tr.astype(BF16)
    for i in range(n_seq):
        seq_rows = slice(i * seq, (i + 1) * seq)
        for c in range(n_chunks):
            rows = slice(c * ROW_CHUNK, (c + 1) * ROW_CHUNK)
            srows = slice(i * seq + c * ROW_CHUNK, i * seq + (c + 1) * ROW_CHUNK)
            cs = cs_ref[rows, :]
            p = _dot(cs, ta_scr[seq_rows, :])
            q = _dot(ss_ref[rows, :], tb_scr[seq_rows, :])
            r = _dot(cs, tr_scr[seq_rows, :])
            parts = []
            for g in range(N_GROUPS):
                half = slice(g * HALF_W, (g + 1) * HALF_W)
                pg, qg = p[:, half], q[:, half]
                rg = r if g == 0 else pltpu.roll(r, HALF_W - g, axis=1)
                parts.append(jnp.where(lane == 0, pg, pg - qg))
                parts.append(jnp.where(lane == 0, rg, pg + qg))
            y = jnp.concatenate(parts, axis=1)
            y = (y * jax.nn.silu(z_scr[srows, :])).astype(BF16)
            o_ref[i, rows, :] = x_ref[i, rows, :] + gate * _dot(y, wout_ref[...])


def _dft_tables(seq):
    c = np.arange(GROUP_W)[:, None]
    k = np.arange(HALF_W)[None, :]
    cos_lo = np.cos(2.0 * np.pi * ((c * k) % GROUP_W) / GROUP_W)
    sin_lo = np.sin(2.0 * np.pi * ((c * k) % GROUP_W) / GROUP_W)
    sin_lo[:, 0] = np.cos(np.pi * c[:, 0])
    m1 = np.concatenate([cos_lo, sin_lo], axis=1) / np.sqrt(GROUP_W)
    n = np.arange(seq)
    ang = 2.0 * np.pi * ((n[:, None] * n[None, :]) % seq) / seq
    cs = np.cos(ang) / np.sqrt(seq)
    ss = np.sin(ang) / np.sqrt(seq)
    return m1.astype(np.float32), cs.astype(np.float32), ss.astype(np.float32)


def _const_spec(shape):
    return pl.BlockSpec(shape, lambda b: (0,) * len(shape))


def _fourier_layer(x_ctx, x_lat, nw, this_layer, next_layer):
    nb_ctx, seq_ctx, _ = x_ctx.shape
    nb_lat, seq_lat, _ = x_lat.shape
    n_seq = CTX_SEQS_PER_STEP
    assert nb_ctx % n_seq == 0
    n_prep = D_MODEL // HALF_W
    n_ctx = nb_ctx // n_seq
    assert D_MODEL % (n_ctx * LANES) == 0 and n_seq * seq_ctx <= seq_lat
    rows = D_MODEL // n_ctx
    assert rows == HALF_W
    c_ctx, c, w_mod0, b_mod0, w_in0, w_out0 = this_layer
    _, _, w_mod1, b_mod1, w_in1, w_out1 = next_layer
    n_lat = c.shape[0]
    n_qkvz = 2 * BRANCH + 2 * KV_W
    m1, csc, ssc = (jnp.asarray(t).astype(BF16) for t in _dft_tables(seq_ctx))
    _, csl, ssl = (jnp.asarray(t).astype(BF16) for t in _dft_tables(seq_lat))
    pm, pmh = (jnp.asarray(t).astype(BF16) for t in _perm_matrices())
    cc = c_ctx.reshape(1, D_MODEL)

    def prep_step(i):
        return jnp.minimum(i, n_prep - 1)

    def ctx_step(i):
        return jnp.clip(i - n_prep, 0, n_ctx - 1)

    def lat_step(i):
        return jnp.maximum(i - n_prep - n_ctx, 0)

    def chunk_specs(step_fn, w_in_cols):
        return [pl.BlockSpec((1, HALF_W), lambda i: (0, step_fn(i))),
                pl.BlockSpec((n_lat, HALF_W), lambda i: (0, step_fn(i))),
                pl.BlockSpec((HALF_W, 3 * D_MODEL), lambda i: (step_fn(i), 0)),
                _const_spec((1, 3 * D_MODEL)),
                pl.BlockSpec((HALF_W, w_in_cols), lambda i: (step_fn(i), 0)),
                pl.BlockSpec((HALF_W, D_MODEL), lambda i: (step_fn(i), 0))]

    in_specs = [
        pl.BlockSpec((n_seq, seq_ctx, D_MODEL), lambda i: (ctx_step(i), 0, 0)),
        pl.BlockSpec((1, seq_lat, D_MODEL), lambda i: (lat_step(i), 0, 0)),
        _const_spec((1, D_MODEL)),
        _const_spec((GROUP_W, GROUP_W)),
        _const_spec((seq_ctx, seq_ctx)),
        _const_spec((seq_ctx, seq_ctx)),
        _const_spec((seq_lat, seq_lat)),
        _const_spec((seq_lat, seq_lat)),
    ] + chunk_specs(prep_step, 2 * BRANCH) + [
        _const_spec((GROUP_W, GROUP_W)),
        _const_spec((2, HALF_W, HALF_W)),
    ] + chunk_specs(ctx_step, n_qkvz)
    out_specs = [
        pl.BlockSpec((n_seq, seq_ctx, D_MODEL), lambda i: (ctx_step(i), 0, 0)),
        pl.BlockSpec((1, seq_lat, D_MODEL), lambda i: (lat_step(i), 0, 0)),
        MOD_SPEC,
        pl.BlockSpec((BRANCH + 2 * KV_W, rows), lambda i: (0, ctx_step(i))),
        pl.BlockSpec((rows, BRANCH), lambda i: (ctx_step(i), 0)),
        pl.BlockSpec((rows, D_MODEL), lambda i: (ctx_step(i), 0)),
    ]
    out_shape = [
        jax.ShapeDtypeStruct(x_ctx.shape, F32),
        jax.ShapeDtypeStruct(x_lat.shape, F32),
        jax.ShapeDtypeStruct((MOD_ROWS, 3 * D_MODEL), F32),
        jax.ShapeDtypeStruct((BRANCH + 2 * KV_W, D_MODEL), BF16),
        jax.ShapeDtypeStruct((D_MODEL, BRANCH), BF16),
        jax.ShapeDtypeStruct((BRANCH, D_MODEL), BF16),
    ]
    return pl.pallas_call(
        functools.partial(_fourier_layer_kernel, n_prep_steps=n_prep, n_ctx_steps=n_ctx),
        grid=(n_prep + n_ctx + nb_lat,),
        in_specs=in_specs,
        out_specs=out_specs,
        out_shape=out_shape,
        scratch_shapes=[
            pltpu.VMEM((MOD_ROWS, 3 * D_MODEL), F32),
            pltpu.VMEM((D_MODEL, 2 * BRANCH), BF16),
            pltpu.VMEM((BRANCH, D_MODEL), BF16),
            pltpu.VMEM((seq_lat, N_GROUPS * HALF_W), BF16),
            pltpu.VMEM((seq_lat, N_GROUPS * HALF_W), BF16),
            pltpu.VMEM((seq_lat, HALF_W), BF16),
            pltpu.VMEM((seq_lat, BRANCH), F32),
            pltpu.VMEM((MOD_ROWS, HALF_W), F32),
        ],
        compiler_params=pltpu.CompilerParams(
            dimension_semantics=("arbitrary",), vmem_limit_bytes=FOURIER_VMEM_LIMIT),
        name="fourier_layer",
    )(x_ctx, x_lat, nw, m1, csc, ssc, csl, ssl,
      cc, c, w_mod0, b_mod0.reshape(1, 3 * D_MODEL), w_in0, w_out0, pm, pmh,
      cc, c, w_mod1, b_mod1.reshape(1, 3 * D_MODEL), w_in1, w_out1)


def _head_weight_tile(w_ref, n_tokens):
    row = jnp.broadcast_to(w_ref[...], (HEAD_DIM, HEAD_DIM))
    ii = lax.broadcasted_iota(jnp.int32, (HEAD_DIM, HEAD_DIM), 0)
    jj = lax.broadcasted_iota(jnp.int32, (HEAD_DIM, HEAD_DIM), 1)
    col = jnp.sum(jnp.where(ii == jj, row, 0.0), axis=1, keepdims=True)
    return jnp.broadcast_to(col, (HEAD_DIM, n_tokens))


def _head_rms(t, w):
    return (t * lax.rsqrt(jnp.mean(t * t, axis=0, keepdims=True) + EPS)) * w


def _rope_t(t, cos, sin):
    half = HEAD_DIM // 2
    x1, x2 = t[:half], t[half:]
    return jnp.concatenate([x1 * cos - x2 * sin, x1 * sin + x2 * cos], axis=0)


def _head_scores(qn, g, keys, biases):
    zeros = jnp.zeros_like(qn)
    qz = jnp.concatenate([qn, zeros] if g % 2 == 0 else [zeros, qn], axis=0)
    blk = slice((g // 2) * LANES, (g // 2 + 1) * LANES)
    scores = []
    smax = None
    for k, bias in zip(keys, biases):
        s = _dot(k[:, blk], qz)
        if bias is not None:
            s = jnp.concatenate(
                [s[c * BLOCK:(c + 1) * BLOCK] if b is None else
                 s[c * BLOCK:(c + 1) * BLOCK] + jnp.concatenate([b] * (s.shape[1] // b.shape[1]), axis=1)
                 for c, b in enumerate(bias)], axis=0)
        cmax = jnp.max(s, axis=0, keepdims=True)
        smax = cmax if smax is None else jnp.maximum(smax, cmax)
        scores.append(s)
    return scores, smax


def _with_ones_rows(vt):
    return jnp.concatenate([vt, jnp.ones((ONES_ROWS, vt.shape[1]), vt.dtype)], axis=0)


def _head_probs(scored, sink2):
    scores, smax = scored
    m = jnp.maximum(smax, sink2)
    return jnp.concatenate([jnp.exp2(s - m).astype(BF16) for s in scores], axis=0), m


def _head_pv(probs, values_t, sink2):
    p, m = probs
    acc = _dot(values_t, p)
    den = acc[HEAD_DIM:HEAD_DIM + 1] + jnp.exp2(sink2 - m)
    return acc[:HEAD_DIM] * (1.0 / den)


def _attend_heads(n_units, stage, scores_fn, probs_fn, pv_fn, fillers=()):
    n_stages = n_units // stage
    pending = [scores_fn(u) for u in range(stage)]
    for g in range(n_stages):
        units = range(g * stage, (g + 1) * stage)
        if g < len(fillers):
            fillers[g]()
        nxt = [scores_fn(u) for u in range((g + 1) * stage, (g + 2) * stage)] if g + 1 < n_stages else None
        probs = [probs_fn(u, sc) for u, sc in zip(units, pending)]
        for u, pr in zip(units, probs):
            pv_fn(u, pr)
        pending = nxt


def _gate_out(x, o, z, gate, wout):
    y = (o * jax.nn.silu(z)).astype(BF16)
    return x + gate * _dot(y, wout)


def _attn_ctx_body(sink_ref, x_ref, mod, nw_ref, wqkvt_ref, wz_ref, wout_ref, qw_ref, kw_ref,
                   o_ref, kto_ref, vto_ref, qkvt_scr, z_scr, ot_scr):
    seq = x_ref.shape[1]
    n_seq = x_ref.shape[0]
    gate = mod[:, 2 * D_MODEL:]
    kw = _head_weight_tile(kw_ref, seq)
    qw = _head_weight_tile(qw_ref, seq)

    def project(i):
        h = _mod_norm(x_ref[i], nw_ref[...], mod).astype(BF16)
        qkvt_scr[i] = _dot_nt(wqkvt_ref[...], h)
        z_scr[i * seq:(i + 1) * seq, :] = _dot(h, wz_ref[...])

    def keys_values(i):
        knt = jnp.concatenate(
            [_head_rms(qkvt_scr[i, BRANCH + g * HEAD_DIM:BRANCH + (g + 1) * HEAD_DIM, :], kw)
             for g in range(N_KV)], axis=0)
        kto_ref[i] = knt
        vtf = qkvt_scr[i, BRANCH + KV_W:, :]
        vto_ref[i] = vtf
        vt = vtf.astype(BF16)
        return (knt.T.astype(BF16),
                [_with_ones_rows(vt[g * HEAD_DIM:(g + 1) * HEAD_DIM]) for g in range(N_KV)])

    def output(i):
        o_ref[i] = _gate_out(x_ref[i], ot_scr[i].T, z_scr[i * seq:(i + 1) * seq, :], gate, wout_ref[...])

    for i in range(n_seq):
        project(i)
    kv = [keys_values(i) for i in range(n_seq)]

    def scores_fn(u):
        i, hd = divmod(u, N_HEADS)
        t = qkvt_scr[i, hd * HEAD_DIM:(hd + 1) * HEAD_DIM, :]
        qn = (_head_rms(t, qw) * (HEAD_DIM ** -0.5 * LOG2E)).astype(BF16)
        return _head_scores(qn, hd // GQA, [kv[i][0]], [None])

    def probs_fn(u, sc):
        return _head_probs(sc, sink_ref[u % N_HEADS] * LOG2E)

    def pv_fn(u, pr):
        i, hd = divmod(u, N_HEADS)
        ot_scr[i, hd * HEAD_DIM:(hd + 1) * HEAD_DIM, :] = _head_pv(
            pr, kv[i][1][hd // GQA], sink_ref[hd] * LOG2E)

    fillers = [lambda: None] + [functools.partial(output, i) for i in range(n_seq - 1)]
    _attend_heads(n_seq * N_HEADS, N_HEADS, scores_fn, probs_fn, pv_fn, fillers)
    output(n_seq - 1)


def _attn_lat_body(sink_ref, x_ref, mod, nw_ref, wqkvt_ref, wz_ref, wout_ref, qw_ref, kw_ref,
                   cos_ref, sin_ref, ck_ref, cv_ref, o_ref,
                   q_scr, z_scr, ot_scr, k_scr, vt_scr):
    seq = x_ref.shape[1]
    gate = mod[:, 2 * D_MODEL:]
    nw = nw_ref[...]
    qw = _head_weight_tile(qw_ref, Q_BLOCK)
    kw = _head_weight_tile(kw_ref, Q_BLOCK)
    n_blocks = seq // Q_BLOCK
    kv_blocks = seq // BLOCK
    k_scr[0:BLOCK, :] = jnp.zeros((BLOCK, KV_W), BF16)
    k_scr[BLOCK + seq:2 * BLOCK + seq, :] = jnp.zeros((BLOCK, KV_W), BF16)
    vt_scr[0] = jnp.zeros((KV_W, BLOCK), BF16)
    vt_scr[kv_blocks + 1] = jnp.zeros((KV_W, BLOCK), BF16)

    def project(c):
        rows = slice(c * Q_BLOCK, (c + 1) * Q_BLOCK)
        h = _mod_norm(x_ref[0, rows, :], nw, mod).astype(BF16)
        z_scr[rows, :] = _dot(h, wz_ref[...])
        qkvt = _dot_nt(wqkvt_ref[...], h)
        cos = cos_ref[:, rows]
        sin = sin_ref[:, rows]
        for hd in range(N_HEADS):
            hr = slice(hd * HEAD_DIM, (hd + 1) * HEAD_DIM)
            t = _rope_t(_head_rms(qkvt[hr], qw), cos, sin)
            t = (t * (HEAD_DIM ** -0.5 * LOG2E)).astype(BF16)
            for j in range(HALVES):
                q_scr[c * HALVES + j, hd // 2, :, (hd % 2) * BLOCK:(hd % 2 + 1) * BLOCK] = (
                    t[:, j * BLOCK:(j + 1) * BLOCK])
        knt = jnp.concatenate(
            [_rope_t(_head_rms(qkvt[BRANCH + g * HEAD_DIM:BRANCH + (g + 1) * HEAD_DIM], kw), cos, sin)
             for g in range(N_KV)], axis=0)
        k_scr[BLOCK + c * Q_BLOCK:BLOCK + (c + 1) * Q_BLOCK, :] = knt.T.astype(BF16)
        vt = qkvt[BRANCH + KV_W:].astype(BF16)
        for j in range(HALVES):
            vt_scr[1 + c * HALVES + j] = vt[:, j * BLOCK:(j + 1) * BLOCK]

    ckb = ck_ref[0].T.astype(BF16)
    cvt = cv_ref[0].astype(BF16)

    win_len = 3 * BLOCK
    n_pairs = N_HEADS // 2
    kj = lax.broadcasted_iota(jnp.int32, (BLOCK, BLOCK), 0)
    qi = lax.broadcasted_iota(jnp.int32, (BLOCK, BLOCK), 1)
    first_head = lax.broadcasted_iota(jnp.int32, (1, 2 * BLOCK), 1) < BLOCK
    assert WINDOW >= BLOCK - 1

    def band_bias(c):
        rel = kj + (c - 1) * BLOCK - qi
        return jnp.where((rel >= -WINDOW) & (rel <= WINDOW), 0.0, NEG_INF)

    band_before, band_after = band_bias(0), band_bias(2)

    def half_operands(hb):
        r0 = pl.multiple_of(hb * BLOCK, BLOCK)
        valid = (jnp.where(hb > 0, band_before, NEG_INF), None,
                 jnp.where(hb < kv_blocks - 1, band_after, NEG_INF))
        kwin = k_scr[pl.ds(r0, win_len), :]
        vall = jnp.concatenate([vt_scr[hb + j] for j in range(win_len // BLOCK)] + [cvt], axis=1)
        valls = [_with_ones_rows(vall[g * HEAD_DIM:(g + 1) * HEAD_DIM]) for g in range(N_KV)]
        return valid, kwin, valls

    def pair_sink2(pair):
        return jnp.where(first_head, sink_ref[2 * pair], sink_ref[2 * pair + 1]) * LOG2E

    def attend(n, carry):
        halves = [half_operands(n * HALVES + j) for j in range(HALVES)]

        def scores_fn(u):
            j, pair = divmod(u, n_pairs)
            valid, kwin, _ = halves[j]
            return _head_scores(q_scr[n * HALVES + j, pair], pair // (GQA // 2), [kwin, ckb], [valid, None])

        def probs_fn(u, sc):
            return _head_probs(sc, pair_sink2(u % n_pairs))

        def pv_fn(u, pr):
            j, pair = divmod(u, n_pairs)
            o = _head_pv(pr, halves[j][2][pair // (GQA // 2)], pair_sink2(pair))
            for i in range(2):
                hd = 2 * pair + i
                ot_scr[n, hd * HEAD_DIM:(hd + 1) * HEAD_DIM, j * BLOCK:(j + 1) * BLOCK] = (
                    o[:, i * BLOCK:(i + 1) * BLOCK])

        _attend_heads(HALVES * n_pairs, GQA, scores_fn, probs_fn, pv_fn)
        return carry

    for c in range(n_blocks):
        project(c)
    lax.fori_loop(0, n_blocks, attend, 0)
    for c in range(n_blocks):
        rows = slice(c * Q_BLOCK, (c + 1) * Q_BLOCK)
        o_ref[0, rows, :] = _gate_out(x_ref[0, rows, :], ot_scr[c].T, z_scr[rows, :], gate, wout_ref[...])


def _rope_tables_t(seq):
    pos = np.arange(seq)
    n_freq = HEAD_DIM // 4
    inv = ROPE_THETA ** (-np.arange(n_freq, dtype=np.float64) / n_freq)
    ang = np.concatenate([(pos // GRID_W)[:, None] * inv, (pos % GRID_W)[:, None] * inv], axis=-1)
    return np.cos(ang).T.astype(np.float32), np.sin(ang).T.astype(np.float32)


def _attn_layer_kernel(sink_ref, xc_ref, xl_ref, mod_ref, nw_ref, wqkvt_ref, wz_ref, wout_ref, qw_ref, kw_ref,
                       cos_ref, sin_ref, ck_ref, cv_ref, oc_ref, kto_ref, vto_ref, ol_ref,
                       qkvt_scr, q_scr, z_scr, ot_scr, k_scr, vt_scr, *, n_ctx_steps):
    step = pl.program_id(0)

    @pl.when(step < n_ctx_steps)
    def _():
        _attn_ctx_body(sink_ref, xc_ref, mod_ref[0:1, :], nw_ref, wqkvt_ref, wz_ref, wout_ref, qw_ref, kw_ref,
                       oc_ref, kto_ref, vto_ref, qkvt_scr, z_scr, ot_scr)

    @pl.when(step >= n_ctx_steps)
    def _():
        mod = mod_ref[pl.ds(1 + step - n_ctx_steps, 1), :]
        _attn_lat_body(sink_ref, xl_ref, mod, nw_ref, wqkvt_ref, wz_ref, wout_ref, qw_ref, kw_ref,
                       cos_ref, sin_ref, ck_ref, cv_ref, ol_ref, q_scr, z_scr, ot_scr, k_scr, vt_scr)


def _attn_layer(x_ctx, x_lat, mod, nw, wqkvt, wz, wout, qw, kw, sink, ckt, cvt):
    nb_ctx, seq_ctx, _ = x_ctx.shape
    nb_lat, seq_lat, _ = x_lat.shape
    past = ckt.shape[2]
    n_seq = CTX_SEQS_PER_STEP
    assert nb_ctx % n_seq == 0 and seq_ctx == Q_BLOCK and n_seq <= seq_lat // Q_BLOCK
    n_ctx = nb_ctx // n_seq
    cos, sin = (jnp.asarray(t) for t in _rope_tables_t(seq_lat))

    def ctx_step(i):
        return jnp.minimum(i, n_ctx - 1)

    def lat_step(i):
        return jnp.maximum(i - n_ctx, 0)

    return pl.pallas_call(
        functools.partial(_attn_layer_kernel, n_ctx_steps=n_ctx),
        grid=(n_ctx + nb_lat,),
        in_specs=[
            pl.BlockSpec(memory_space=pltpu.SMEM),
            pl.BlockSpec((n_seq, seq_ctx, D_MODEL), lambda i: (ctx_step(i), 0, 0)),
            pl.BlockSpec((1, seq_lat, D_MODEL), lambda i: (lat_step(i), 0, 0)),
            MOD_SPEC,
            _const_spec((1, D_MODEL)),
            _const_spec((BRANCH + 2 * KV_W, D_MODEL)),
            _const_spec((D_MODEL, BRANCH)),
            _const_spec((BRANCH, D_MODEL)),
            _const_spec((1, HEAD_DIM)),
            _const_spec((1, HEAD_DIM)),
            _const_spec((HEAD_DIM // 2, seq_lat)),
            _const_spec((HEAD_DIM // 2, seq_lat)),
            pl.BlockSpec((1, KV_W, past), lambda i: (lat_step(i), 0, 0)),
            pl.BlockSpec((1, KV_W, past), lambda i: (lat_step(i), 0, 0)),
        ],
        out_specs=[
            pl.BlockSpec((n_seq, seq_ctx, D_MODEL), lambda i: (ctx_step(i), 0, 0)),
            pl.BlockSpec((n_seq, KV_W, seq_ctx), lambda i: (ctx_step(i), 0, 0)),
            pl.BlockSpec((n_seq, KV_W, seq_ctx), lambda i: (ctx_step(i), 0, 0)),
            pl.BlockSpec((1, seq_lat, D_MODEL), lambda i: (lat_step(i), 0, 0)),
        ],
        out_shape=[
            jax.ShapeDtypeStruct(x_ctx.shape, F32),
            jax.ShapeDtypeStruct((nb_ctx, KV_W, seq_ctx), F32),
            jax.ShapeDtypeStruct((nb_ctx, KV_W, seq_ctx), F32),
            jax.ShapeDtypeStruct(x_lat.shape, F32),
        ],
        scratch_shapes=[
            pltpu.VMEM((n_seq, BRANCH + 2 * KV_W, seq_ctx), F32),
            pltpu.VMEM((seq_lat // BLOCK, N_HEADS // 2, HEAD_DIM, 2 * BLOCK), BF16),
            pltpu.VMEM((seq_lat, BRANCH), F32),
            pltpu.VMEM((seq_lat // Q_BLOCK, BRANCH, Q_BLOCK), F32),
            pltpu.VMEM((seq_lat + 2 * BLOCK, KV_W), BF16),
            pltpu.VMEM((seq_lat // BLOCK + 2, KV_W, BLOCK), BF16),
        ],
        compiler_params=pltpu.CompilerParams(
            dimension_semantics=("arbitrary",), vmem_limit_bytes=VMEM_LIMIT),
        name="attn_layer",
    )(sink, x_ctx, x_lat, mod, nw, wqkvt, wz, wout, qw, kw, cos, sin, ckt, cvt)


def kernel(x_prompt, x_sample, cache_k_l1, cache_v_l1, c, c_ctx, norm_w_l0, w_mod_l0, b_mod_l0,
           w_in_l0, w_out_l0, norm_w_l1, w_mod_l1, b_mod_l1, w_in_l1, q_norm_w_l1, k_norm_w_l1,
           sink_l1, w_out_l1):
    nb_ctx, seq_ctx, _ = x_prompt.shape
    nb_lat = x_sample.shape[0]
    past = cache_k_l1.shape[1]
    assert 1 + nb_lat <= MOD_ROWS
    nw0 = norm_w_l0.reshape(1, D_MODEL)
    nw1 = norm_w_l1.reshape(1, D_MODEL)
    qw = q_norm_w_l1.reshape(1, HEAD_DIM)
    kw = k_norm_w_l1.reshape(1, HEAD_DIM)

    xp, xs, mod1, wqkvt1, wz1, wout1 = _fourier_layer(
        x_prompt, x_sample, nw0,
        this_layer=(c_ctx, c, w_mod_l0, b_mod_l0, w_in_l0, w_out_l0),
        next_layer=(c_ctx, c, w_mod_l1, b_mod_l1, w_in_l1, w_out_l1))

    def to_feature_major(t):
        return jnp.transpose(t, (0, 2, 3, 1)).reshape(t.shape[0], KV_W, t.shape[1])

    def from_feature_major(t):
        return jnp.transpose(t.reshape(t.shape[0], N_KV, HEAD_DIM, t.shape[2]), (0, 3, 1, 2))

    xp, new_kt, new_vt, xs = _attn_layer(xp, xs, mod1, nw1, wqkvt1, wz1, wout1, qw, kw, sink_l1,
                                         to_feature_major(cache_k_l1), to_feature_major(cache_v_l1))
    return (xp, xs, from_feature_major(new_kt), from_feature_major(new_vt))
```

```python
import functools

import numpy as np
import jax
import jax.numpy as jnp
from jax import lax
from jax.experimental import pallas as pl
from jax.experimental.pallas import tpu as pltpu

D_MODEL = 1024
BRANCH = 1024
N_GROUPS = 4
GROUP_W = BRANCH // N_GROUPS
HALF_W = GROUP_W // 2
HEAD_DIM = 64
N_HEADS = 16
N_KV = 4
GQA = N_HEADS // N_KV
KV_W = N_KV * HEAD_DIM
GRID_W = 64
WINDOW = 128
BLOCK = 128
ROPE_THETA = 10000.0
EPS = 1e-6
NEG_INF = -1e30
LANES = 128
ROW_CHUNK = 256
Q_BLOCK = 256
HALVES = Q_BLOCK // BLOCK
CTX_SEQS_PER_STEP = 2
VMEM_LIMIT = 56 * 1024 * 1024
FOURIER_VMEM_LIMIT = 60 * 1024 * 1024
MOD_ROWS = 8
ONES_ROWS = 16
LOG2E = float(np.log2(np.e))

F32 = jnp.float32
BF16 = jnp.bfloat16


def _dot(a, b):
    return jnp.dot(a, b, preferred_element_type=F32)


def _dot_nt(a, b):
    return lax.dot_general(a, b, (((1,), (1,)), ((), ())), preferred_element_type=F32)


MOD_SPEC = pl.BlockSpec((MOD_ROWS, 3 * D_MODEL), lambda b: (0, 0))


def _mod_norm(x, nw, mod):
    shift = mod[:, :D_MODEL]
    scale = mod[:, D_MODEL:2 * D_MODEL]
    y = x * lax.rsqrt(jnp.mean(x * x, axis=-1, keepdims=True) + EPS)
    return (y * nw) * (1.0 + scale) + shift


def _mod_accumulate(cctx_ref, c_ref, w_ref, b_ref, o_ref, cond_scr, first_step):
    n_lat = c_ref.shape[0]
    cond_scr[...] = jnp.zeros_like(cond_scr)
    cond_scr[0:1, :] = cctx_ref[...]
    cond_scr[1:1 + n_lat, :] = c_ref[...]
    s = jax.nn.silu(cond_scr[...]).astype(BF16)

    @pl.when(pl.program_id(0) == first_step)
    def _():
        o_ref[...] = jnp.broadcast_to(b_ref[...], o_ref.shape)

    o_ref[...] += _dot(s, w_ref[...].astype(BF16))


def _mirror_perm():
    j = np.arange(GROUP_W)
    return np.where(j <= HALF_W, j, GROUP_W + HALF_W - j)


def _perm_matrices():
    perm = _mirror_perm()
    pm = (np.arange(GROUP_W)[:, None] == perm[None, :]).astype(np.float32)
    assert (pm[:HALF_W, HALF_W:] == 0).all() and (pm[HALF_W:, :HALF_W] == 0).all()
    return pm, np.stack([pm[:HALF_W, :HALF_W], pm[HALF_W:, HALF_W:]])


def _fourier_layer_kernel(xc_ref, xl_hbm, nw_ref, m1_ref, csc_ref, ssc_ref, csl_hbm, ssl_hbm,
                          cctx0_ref, c0_ref, wmod0_ref, bmod0_ref, win0_ref, wout0_ref, pm_ref, pmh_ref,
                          cctx_ref, c_ref, wmod1_ref, bmod1_ref, win1_ref, wout1_ref,
                          oc_ref, ol_ref, mod1_ref, wqkvt1_ref, wz1_ref, wout1b_ref,
                          mod0_scr, win_scr, wout_scr, ta_scr, tb_scr, tr_scr, z_scr, cond_scr,
                          xl_scr, csl_scr, ssl_scr, copy_sem,
                          *, n_prep_steps, n_ctx_steps):
    step = pl.program_id(0)
    n_lat = xl_scr.shape[0]
    lat = step - (n_prep_steps + n_ctx_steps)

    def request_copy(r):
        return pltpu.make_async_copy(xl_hbm.at[r], xl_scr.at[r], copy_sem.at[r])

    table_copies = [pltpu.make_async_copy(csl_hbm, csl_scr, copy_sem.at[n_lat]),
                    pltpu.make_async_copy(ssl_hbm, ssl_scr, copy_sem.at[n_lat + 1])]

    @pl.when(step == n_prep_steps)
    def _():
        for copy in table_copies + [request_copy(r) for r in range(n_lat)]:
            copy.start()

    @pl.when(step < n_prep_steps)
    def _():
        _mod_accumulate(cctx0_ref, c0_ref, wmod0_ref, bmod0_ref, mod0_scr, cond_scr, 0)
        rows = pl.ds(pl.multiple_of(step * HALF_W, HALF_W), HALF_W)
        w = win0_ref[...].astype(BF16)
        win_scr[rows, :BRANCH] = w[:, :BRANCH]
        pm = pm_ref[...]
        for g in range(N_GROUPS):
            cols = slice(BRANCH + g * GROUP_W, BRANCH + (g + 1) * GROUP_W)
            win_scr[rows, cols] = _dot(w[:, cols], pm).astype(BF16)
        wout_scr[rows, :] = _dot(pmh_ref[step % 2], wout0_ref[...].astype(BF16)).astype(BF16)

    @pl.when(jnp.logical_and(step >= n_prep_steps, step < n_prep_steps + n_ctx_steps))
    def _():
        _mod_accumulate(cctx_ref, c_ref, wmod1_ref, bmod1_ref, mod1_ref, cond_scr, n_prep_steps)
        w = win1_ref[...]
        wqkvt1_ref[...] = w[:, :BRANCH + 2 * KV_W].T.astype(BF16)
        wz1_ref[...] = w[:, BRANCH + 2 * KV_W:].astype(BF16)
        wout1b_ref[...] = wout1_ref[...].astype(BF16)
        _fourier_body(xc_ref, oc_ref, mod0_scr[0:1, :], nw_ref, win_scr, wout_scr, m1_ref, csc_ref, ssc_ref,
                      ta_scr, tb_scr, tr_scr, z_scr)

    @pl.when(step >= n_prep_steps + n_ctx_steps)
    def _():
        @pl.when(lat == 0)
        def _():
            for copy in table_copies:
                copy.wait()

        request_copy(lat).wait()
        mod = mod0_scr[pl.ds(1 + lat, 1), :]
        _fourier_body(xl_scr.at[pl.ds(lat, 1)], ol_ref, mod, nw_ref, win_scr, wout_scr, m1_ref,
                      csl_scr, ssl_scr, ta_scr, tb_scr, tr_scr, z_scr)


def _fourier_body(x_ref, o_ref, mod, nw_ref, win_ref, wout_ref, m1_ref, cs_ref, ss_ref,
                  ta_scr, tb_scr, tr_scr, z_scr):
    gate = mod[:, 2 * D_MODEL:]
    nw = nw_ref[...]
    n_seq, seq, _ = x_ref.shape
    n_chunks = seq // ROW_CHUNK
    lane = lax.broadcasted_iota(jnp.int32, (ROW_CHUNK, HALF_W), 1)
    for i in range(n_seq):
        for c in range(n_chunks):
            rows = slice(c * ROW_CHUNK, (c + 1) * ROW_CHUNK)
            srows = slice(i * seq + c * ROW_CHUNK, i * seq + (c + 1) * ROW_CHUNK)
            h = _mod_norm(x_ref[i, rows, :], nw, mod).astype(BF16)
            uz = _dot(h, win_ref[...])
            z_scr[srows, :] = uz[:, BRANCH:]
            u = uz[:, :BRANCH].astype(BF16)
            tr = jnp.zeros((ROW_CHUNK, HALF_W), F32)
            for g in range(N_GROUPS):
                t = _dot(u[:, g * GROUP_W:(g + 1) * GROUP_W], m1_ref[...])
                half = slice(g * HALF_W, (g + 1) * HALF_W)
                ta_scr[srows, half] = t[:, :HALF_W].astype(BF16)
                tb = t[:, HALF_W:]
                tb_scr[srows, half] = tb.astype(BF16)
                tr = jnp.where(lane == g, tb if g == 0 else pltpu.roll(tb, g, axis=1), tr)
            tr_scr[srows, :] = tr.astype(BF16)
    for i in range(n_seq):
        seq_rows = slice(i * seq, (i + 1) * seq)
        for c in range(n_chunks):
            rows = slice(c * ROW_CHUNK, (c + 1) * ROW_CHUNK)
            srows = slice(i * seq + c * ROW_CHUNK, i * seq + (c + 1) * ROW_CHUNK)
            cs = cs_ref[rows, :]
            p = _dot(cs, ta_scr[seq_rows, :])
            q = _dot(ss_ref[rows, :], tb_scr[seq_rows, :])
            r = _dot(cs, tr_scr[seq_rows, :])
            parts = []
            for g in range(N_GROUPS):
                half = slice(g * HALF_W, (g + 1) * HALF_W)
                pg, qg = p[:, half], q[:, half]
                rg = r if g == 0 else pltpu.roll(r, HALF_W - g, axis=1)
                parts.append(jnp.where(lane == 0, pg, pg - qg))
                parts.append(jnp.where(lane == 0, rg, pg + qg))
            y = jnp.concatenate(parts, axis=1)
            y = (y * jax.nn.silu(z_scr[srows, :])).astype(BF16)
            o_ref[i, rows, :] = x_ref[i, rows, :] + gate * _dot(y, wout_ref[...])


def _dft_tables(seq):
    c = np.arange(GROUP_W)[:, None]
    k = np.arange(HALF_W)[None, :]
    cos_lo = np.cos(2.0 * np.pi * ((c * k) % GROUP_W) / GROUP_W)
    sin_lo = np.sin(2.0 * np.pi * ((c * k) % GROUP_W) / GROUP_W)
    sin_lo[:, 0] = np.cos(np.pi * c[:, 0])
    m1 = np.concatenate([cos_lo, sin_lo], axis=1) / np.sqrt(GROUP_W)
    n = np.arange(seq)
    ang = 2.0 * np.pi * ((n[:, None] * n[None, :]) % seq) / seq
    cs = np.cos(ang) / np.sqrt(seq)
    ss = np.sin(ang) / np.sqrt(seq)
    return m1.astype(np.float32), cs.astype(np.float32), ss.astype(np.float32)


def _const_spec(shape):
    return pl.BlockSpec(shape, lambda b: (0,) * len(shape))


def _fourier_layer(x_ctx, x_lat, nw, this_layer, next_layer):
    nb_ctx, seq_ctx, _ = x_ctx.shape
    nb_lat, seq_lat, _ = x_lat.shape
    n_seq = CTX_SEQS_PER_STEP
    assert nb_ctx % n_seq == 0
    n_prep = D_MODEL // HALF_W
    n_ctx = nb_ctx // n_seq
    assert D_MODEL % (n_ctx * LANES) == 0 and n_seq * seq_ctx <= seq_lat
    rows = D_MODEL // n_ctx
    assert rows == HALF_W
    c_ctx, c, w_mod0, b_mod0, w_in0, w_out0 = this_layer
    _, _, w_mod1, b_mod1, w_in1, w_out1 = next_layer
    n_lat = c.shape[0]
    n_qkvz = 2 * BRANCH + 2 * KV_W
    m1, csc, ssc = (jnp.asarray(t).astype(BF16) for t in _dft_tables(seq_ctx))
    _, csl, ssl = (jnp.asarray(t).astype(BF16) for t in _dft_tables(seq_lat))
    pm, pmh = (jnp.asarray(t).astype(BF16) for t in _perm_matrices())
    cc = c_ctx.reshape(1, D_MODEL)

    def prep_step(i):
        return jnp.minimum(i, n_prep - 1)

    def ctx_step(i):
        return jnp.clip(i - n_prep, 0, n_ctx - 1)

    def lat_step(i):
        return jnp.maximum(i - n_prep - n_ctx, 0)

    def chunk_specs(step_fn, w_in_cols):
        return [pl.BlockSpec((1, HALF_W), lambda i: (0, step_fn(i))),
                pl.BlockSpec((n_lat, HALF_W), lambda i: (0, step_fn(i))),
                pl.BlockSpec((HALF_W, 3 * D_MODEL), lambda i: (step_fn(i), 0)),
                _const_spec((1, 3 * D_MODEL)),
                pl.BlockSpec((HALF_W, w_in_cols), lambda i: (step_fn(i), 0)),
                pl.BlockSpec((HALF_W, D_MODEL), lambda i: (step_fn(i), 0))]

    in_specs = [
        pl.BlockSpec((n_seq, seq_ctx, D_MODEL), lambda i: (ctx_step(i), 0, 0)),
        pl.BlockSpec(memory_space=pl.ANY),
        _const_spec((1, D_MODEL)),
        _const_spec((GROUP_W, GROUP_W)),
        _const_spec((seq_ctx, seq_ctx)),
        _const_spec((seq_ctx, seq_ctx)),
        pl.BlockSpec(memory_space=pl.ANY),
        pl.BlockSpec(memory_space=pl.ANY),
    ] + chunk_specs(prep_step, 2 * BRANCH) + [
        _const_spec((GROUP_W, GROUP_W)),
        _const_spec((2, HALF_W, HALF_W)),
    ] + chunk_specs(ctx_step, n_qkvz)
    out_specs = [
        pl.BlockSpec((n_seq, seq_ctx, D_MODEL), lambda i: (ctx_step(i), 0, 0)),
        pl.BlockSpec((1, seq_lat, D_MODEL), lambda i: (lat_step(i), 0, 0)),
        MOD_SPEC,
        pl.BlockSpec((BRANCH + 2 * KV_W, rows), lambda i: (0, ctx_step(i))),
        pl.BlockSpec((rows, BRANCH), lambda i: (ctx_step(i), 0)),
        pl.BlockSpec((rows, D_MODEL), lambda i: (ctx_step(i), 0)),
    ]
    out_shape = [
        jax.ShapeDtypeStruct(x_ctx.shape, F32),
        jax.ShapeDtypeStruct(x_lat.shape, F32),
        jax.ShapeDtypeStruct((MOD_ROWS, 3 * D_MODEL), F32),
        jax.ShapeDtypeStruct((BRANCH + 2 * KV_W, D_MODEL), BF16),
        jax.ShapeDtypeStruct((D_MODEL, BRANCH), BF16),
        jax.ShapeDtypeStruct((BRANCH, D_MODEL), BF16),
    ]
    return pl.pallas_call(
        functools.partial(_fourier_layer_kernel, n_prep_steps=n_prep, n_ctx_steps=n_ctx),
        grid=(n_prep + n_ctx + nb_lat,),
        in_specs=in_specs,
        out_specs=out_specs,
        out_shape=out_shape,
        scratch_shapes=[
            pltpu.VMEM((MOD_ROWS, 3 * D_MODEL), F32),
            pltpu.VMEM((D_MODEL, 2 * BRANCH), BF16),
            pltpu.VMEM((BRANCH, D_MODEL), BF16),
            pltpu.VMEM((seq_lat, N_GROUPS * HALF_W), BF16),
            pltpu.VMEM((seq_lat, N_GROUPS * HALF_W), BF16),
            pltpu.VMEM((seq_lat, HALF_W), BF16),
            pltpu.VMEM((seq_lat, BRANCH), F32),
            pltpu.VMEM((MOD_ROWS, HALF_W), F32),
            pltpu.VMEM(x_lat.shape, F32),
            pltpu.VMEM((seq_lat, seq_lat), BF16),
            pltpu.VMEM((seq_lat, seq_lat), BF16),
            pltpu.SemaphoreType.DMA((nb_lat + 2,)),
        ],
        compiler_params=pltpu.CompilerParams(
            dimension_semantics=("arbitrary",), vmem_limit_bytes=FOURIER_VMEM_LIMIT),
        name="fourier_layer",
    )(x_ctx, x_lat, nw, m1, csc, ssc, csl, ssl,
      cc, c, w_mod0, b_mod0.reshape(1, 3 * D_MODEL), w_in0, w_out0, pm, pmh,
      cc, c, w_mod1, b_mod1.reshape(1, 3 * D_MODEL), w_in1, w_out1)


def _head_weight_tile(w_ref, n_tokens):
    row = jnp.broadcast_to(w_ref[...], (HEAD_DIM, HEAD_DIM))
    ii = lax.broadcasted_iota(jnp.int32, (HEAD_DIM, HEAD_DIM), 0)
    jj = lax.broadcasted_iota(jnp.int32, (HEAD_DIM, HEAD_DIM), 1)
    col = jnp.sum(jnp.where(ii == jj, row, 0.0), axis=1, keepdims=True)
    return jnp.broadcast_to(col, (HEAD_DIM, n_tokens))


def _head_rms(t, w):
    return (t * lax.rsqrt(jnp.mean(t * t, axis=0, keepdims=True) + EPS)) * w


def _rope_t(t, cos, sin):
    half = HEAD_DIM // 2
    x1, x2 = t[:half], t[half:]
    return jnp.concatenate([x1 * cos - x2 * sin, x1 * sin + x2 * cos], axis=0)


def _head_scores(qn, g, keys, biases):
    zeros = jnp.zeros_like(qn)
    qz = jnp.concatenate([qn, zeros] if g % 2 == 0 else [zeros, qn], axis=0)
    blk = slice((g // 2) * LANES, (g // 2 + 1) * LANES)
    scores = []
    smax = None
    for k, bias in zip(keys, biases):
        s = _dot(k[:, blk], qz)
        if bias is not None:
            s = jnp.concatenate(
                [s[c * BLOCK:(c + 1) * BLOCK] if b is None else
                 s[c * BLOCK:(c + 1) * BLOCK] + jnp.concatenate([b] * (s.shape[1] // b.shape[1]), axis=1)
                 for c, b in enumerate(bias)], axis=0)
        cmax = jnp.max(s, axis=0, keepdims=True)
        smax = cmax if smax is None else jnp.maximum(smax, cmax)
        scores.append(s)
    return scores, smax


def _with_ones_rows(vt):
    return jnp.concatenate([vt, jnp.ones((ONES_ROWS, vt.shape[1]), vt.dtype)], axis=0)


def _head_probs(scored, sink2):
    scores, smax = scored
    m = jnp.maximum(smax, sink2)
    return jnp.concatenate([jnp.exp2(s - m).astype(BF16) for s in scores], axis=0), m


def _head_pv(probs, values_t, sink2):
    p, m = probs
    acc = _dot(values_t, p)
    den = acc[HEAD_DIM:HEAD_DIM + 1] + jnp.exp2(sink2 - m)
    return acc[:HEAD_DIM] * (1.0 / den)


def _attend_heads(n_units, stage, scores_fn, probs_fn, pv_fn, fillers=()):
    n_stages = n_units // stage
    pending = [scores_fn(u) for u in range(stage)]
    for g in range(n_stages):
        units = range(g * stage, (g + 1) * stage)
        if g < len(fillers):
            fillers[g]()
        nxt = [scores_fn(u) for u in range((g + 1) * stage, (g + 2) * stage)] if g + 1 < n_stages else None
        probs = [probs_fn(u, sc) for u, sc in zip(units, pending)]
        for u, pr in zip(units, probs):
            pv_fn(u, pr)
        pending = nxt


def _gate_out(x, o, z, gate, wout):
    y = (o * jax.nn.silu(z)).astype(BF16)
    return x + gate * _dot(y, wout)


def _attn_ctx_body(sink_ref, x_ref, mod, nw_ref, wqkvt_ref, wz_ref, wout_ref, qw_ref, kw_ref,
                   o_ref, kto_ref, vto_ref, qkvt_scr, z_scr, ot_scr):
    seq = x_ref.shape[1]
    n_seq = x_ref.shape[0]
    gate = mod[:, 2 * D_MODEL:]
    kw = _head_weight_tile(kw_ref, seq)
    qw = _head_weight_tile(qw_ref, seq)

    def project(i):
        h = _mod_norm(x_ref[i], nw_ref[...], mod).astype(BF16)
        qkvt_scr[i] = _dot_nt(wqkvt_ref[...], h)
        z_scr[i * seq:(i + 1) * seq, :] = _dot(h, wz_ref[...])

    def keys_values(i):
        knt = jnp.concatenate(
            [_head_rms(qkvt_scr[i, BRANCH + g * HEAD_DIM:BRANCH + (g + 1) * HEAD_DIM, :], kw)
             for g in range(N_KV)], axis=0)
        kto_ref[i] = knt
        vtf = qkvt_scr[i, BRANCH + KV_W:, :]
        vto_ref[i] = vtf
        vt = vtf.astype(BF16)
        return (knt.T.astype(BF16),
                [_with_ones_rows(vt[g * HEAD_DIM:(g + 1) * HEAD_DIM]) for g in range(N_KV)])

    def output(i):
        o_ref[i] = _gate_out(x_ref[i], ot_scr[i].T, z_scr[i * seq:(i + 1) * seq, :], gate, wout_ref[...])

    for i in range(n_seq):
        project(i)
    kv = [keys_values(i) for i in range(n_seq)]

    def scores_fn(u):
        i, hd = divmod(u, N_HEADS)
        t = qkvt_scr[i, hd * HEAD_DIM:(hd + 1) * HEAD_DIM, :]
        qn = (_head_rms(t, qw) * (HEAD_DIM ** -0.5 * LOG2E)).astype(BF16)
        return _head_scores(qn, hd // GQA, [kv[i][0]], [None])

    def probs_fn(u, sc):
        return _head_probs(sc, sink_ref[u % N_HEADS] * LOG2E)

    def pv_fn(u, pr):
        i, hd = divmod(u, N_HEADS)
        ot_scr[i, hd * HEAD_DIM:(hd + 1) * HEAD_DIM, :] = _head_pv(
            pr, kv[i][1][hd // GQA], sink_ref[hd] * LOG2E)

    fillers = [lambda: None] + [functools.partial(output, i) for i in range(n_seq - 1)]
    _attend_heads(n_seq * N_HEADS, N_HEADS, scores_fn, probs_fn, pv_fn, fillers)
    output(n_seq - 1)


def _attn_lat_body(sink_ref, x_ref, mod, nw_ref, wqkvt_ref, wz_ref, wout_ref, qw_ref, kw_ref,
                   cos_ref, sin_ref, ck_ref, cv_ref, o_ref,
                   q_scr, z_scr, ot_scr, k_scr, vt_scr):
    seq = x_ref.shape[1]
    gate = mod[:, 2 * D_MODEL:]
    nw = nw_ref[...]
    qw = _head_weight_tile(qw_ref, Q_BLOCK)
    kw = _head_weight_tile(kw_ref, Q_BLOCK)
    n_blocks = seq // Q_BLOCK
    kv_blocks = seq // BLOCK
    k_scr[0:BLOCK, :] = jnp.zeros((BLOCK, KV_W), BF16)
    k_scr[BLOCK + seq:2 * BLOCK + seq, :] = jnp.zeros((BLOCK, KV_W), BF16)
    vt_scr[0] = jnp.zeros((KV_W, BLOCK), BF16)
    vt_scr[kv_blocks + 1] = jnp.zeros((KV_W, BLOCK), BF16)

    def project(c):
        rows = slice(c * Q_BLOCK, (c + 1) * Q_BLOCK)
        h = _mod_norm(x_ref[0, rows, :], nw, mod).astype(BF16)
        z_scr[rows, :] = _dot(h, wz_ref[...])
        qkvt = _dot_nt(wqkvt_ref[...], h)
        cos = cos_ref[:, rows]
        sin = sin_ref[:, rows]
        for hd in range(N_HEADS):
            hr = slice(hd * HEAD_DIM, (hd + 1) * HEAD_DIM)
            t = _rope_t(_head_rms(qkvt[hr], qw), cos, sin)
            t = (t * (HEAD_DIM ** -0.5 * LOG2E)).astype(BF16)
            for j in range(HALVES):
                q_scr[c * HALVES + j, hd // 2, :, (hd % 2) * BLOCK:(hd % 2 + 1) * BLOCK] = (
                    t[:, j * BLOCK:(j + 1) * BLOCK])
        knt = jnp.concatenate(
            [_rope_t(_head_rms(qkvt[BRANCH + g * HEAD_DIM:BRANCH + (g + 1) * HEAD_DIM], kw), cos, sin)
             for g in range(N_KV)], axis=0)
        k_scr[BLOCK + c * Q_BLOCK:BLOCK + (c + 1) * Q_BLOCK, :] = knt.T.astype(BF16)
        vt = qkvt[BRANCH + KV_W:].astype(BF16)
        for j in range(HALVES):
            vt_scr[1 + c * HALVES + j] = vt[:, j * BLOCK:(j + 1) * BLOCK]

    ckb = ck_ref[0].T.astype(BF16)
    cvt = cv_ref[0].astype(BF16)

    win_len = 3 * BLOCK
    n_pairs = N_HEADS // 2
    kj = lax.broadcasted_iota(jnp.int32, (BLOCK, BLOCK), 0)
    qi = lax.broadcasted_iota(jnp.int32, (BLOCK, BLOCK), 1)
    first_head = lax.broadcasted_iota(jnp.int32, (1, 2 * BLOCK), 1) < BLOCK
    assert WINDOW >= BLOCK - 1

    def band_bias(c):
        rel = kj + (c - 1) * BLOCK - qi
        return jnp.where((rel >= -WINDOW) & (rel <= WINDOW), 0.0, NEG_INF)

    band_before, band_after = band_bias(0), band_bias(2)

    def half_operands(hb):
        r0 = pl.multiple_of(hb * BLOCK, BLOCK)
        valid = (jnp.where(hb > 0, band_before, NEG_INF), None,
                 jnp.where(hb < kv_blocks - 1, band_after, NEG_INF))
        kwin = k_scr[pl.ds(r0, win_len), :]
        vall = jnp.concatenate([vt_scr[hb + j] for j in range(win_len // BLOCK)] + [cvt], axis=1)
        valls = [_with_ones_rows(vall[g * HEAD_DIM:(g + 1) * HEAD_DIM]) for g in range(N_KV)]
        return valid, kwin, valls

    def pair_sink2(pair):
        return jnp.where(first_head, sink_ref[2 * pair], sink_ref[2 * pair + 1]) * LOG2E

    def attend(n, carry):
        halves = [half_operands(n * HALVES + j) for j in range(HALVES)]

        def scores_fn(u):
            j, pair = divmod(u, n_pairs)
            valid, kwin, _ = halves[j]
            return _head_scores(q_scr[n * HALVES + j, pair], pair // (GQA // 2), [kwin, ckb], [valid, None])

        def probs_fn(u, sc):
            return _head_probs(sc, pair_sink2(u % n_pairs))

        def pv_fn(u, pr):
            j, pair = divmod(u, n_pairs)
            o = _head_pv(pr, halves[j][2][pair // (GQA // 2)], pair_sink2(pair))
            for i in range(2):
                hd = 2 * pair + i
                ot_scr[n, hd * HEAD_DIM:(hd + 1) * HEAD_DIM, j * BLOCK:(j + 1) * BLOCK] = (
                    o[:, i * BLOCK:(i + 1) * BLOCK])

        _attend_heads(HALVES * n_pairs, GQA, scores_fn, probs_fn, pv_fn)
        return carry

    for c in range(n_blocks):
        project(c)
    lax.fori_loop(0, n_blocks, attend, 0)
    for c in range(n_blocks):
        rows = slice(c * Q_BLOCK, (c + 1) * Q_BLOCK)
        o_ref[0, rows, :] = _gate_out(x_ref[0, rows, :], ot_scr[c].T, z_scr[rows, :], gate, wout_ref[...])


def _rope_tables_t(seq):
    pos = np.arange(seq)
    n_freq = HEAD_DIM // 4
    inv = ROPE_THETA ** (-np.arange(n_freq, dtype=np.float64) / n_freq)
    ang = np.concatenate([(pos // GRID_W)[:, None] * inv, (pos % GRID_W)[:, None] * inv], axis=-1)
    return np.cos(ang).T.astype(np.float32), np.sin(ang).T.astype(np.float32)


def _attn_layer_kernel(sink_ref, xc_ref, xl_hbm, mod_ref, nw_ref, wqkvt_ref, wz_ref, wout_ref, qw_ref, kw_ref,
                       cos_ref, sin_ref, ck_ref, cv_ref, oc_ref, kto_ref, vto_ref, ol_ref,
                       qkvt_scr, q_scr, z_scr, ot_scr, k_scr, vt_scr, xl_scr, copy_sem, *, n_ctx_steps):
    step = pl.program_id(0)
    lat = step - n_ctx_steps

    def request_copy(r):
        return pltpu.make_async_copy(xl_hbm.at[r], xl_scr.at[r], copy_sem.at[r])

    @pl.when(step == 0)
    def _():
        for r in range(xl_scr.shape[0]):
            request_copy(r).start()

    @pl.when(step < n_ctx_steps)
    def _():
        _attn_ctx_body(sink_ref, xc_ref, mod_ref[0:1, :], nw_ref, wqkvt_ref, wz_ref, wout_ref, qw_ref, kw_ref,
                       oc_ref, kto_ref, vto_ref, qkvt_scr, z_scr, ot_scr)

    @pl.when(step >= n_ctx_steps)
    def _():
        request_copy(lat).wait()
        mod = mod_ref[pl.ds(1 + lat, 1), :]
        _attn_lat_body(sink_ref, xl_scr.at[pl.ds(lat, 1)], mod, nw_ref, wqkvt_ref, wz_ref, wout_ref, qw_ref,
                       kw_ref, cos_ref, sin_ref, ck_ref, cv_ref, ol_ref, q_scr, z_scr, ot_scr, k_scr, vt_scr)


def _attn_layer(x_ctx, x_lat, mod, nw, wqkvt, wz, wout, qw, kw, sink, ckt, cvt):
    nb_ctx, seq_ctx, _ = x_ctx.shape
    nb_lat, seq_lat, _ = x_lat.shape
    past = ckt.shape[2]
    n_seq = CTX_SEQS_PER_STEP
    assert nb_ctx % n_seq == 0 and seq_ctx == Q_BLOCK and n_seq <= seq_lat // Q_BLOCK
    n_ctx = nb_ctx // n_seq
    cos, sin = (jnp.asarray(t) for t in _rope_tables_t(seq_lat))

    def ctx_step(i):
        return jnp.minimum(i, n_ctx - 1)

    def lat_step(i):
        return jnp.maximum(i - n_ctx, 0)

    return pl.pallas_call(
        functools.partial(_attn_layer_kernel, n_ctx_steps=n_ctx),
        grid=(n_ctx + nb_lat,),
        in_specs=[
            pl.BlockSpec(memory_space=pltpu.SMEM),
            pl.BlockSpec((n_seq, seq_ctx, D_MODEL), lambda i: (ctx_step(i), 0, 0)),
            pl.BlockSpec(memory_space=pl.ANY),
            MOD_SPEC,
            _const_spec((1, D_MODEL)),
            _const_spec((BRANCH + 2 * KV_W, D_MODEL)),
            _const_spec((D_MODEL, BRANCH)),
            _const_spec((BRANCH, D_MODEL)),
            _const_spec((1, HEAD_DIM)),
            _const_spec((1, HEAD_DIM)),
            _const_spec((HEAD_DIM // 2, seq_lat)),
            _const_spec((HEAD_DIM // 2, seq_lat)),
            pl.BlockSpec((1, KV_W, past), lambda i: (lat_step(i), 0, 0)),
            pl.BlockSpec((1, KV_W, past), lambda i: (lat_step(i), 0, 0)),
        ],
        out_specs=[
            pl.BlockSpec((n_seq, seq_ctx, D_MODEL), lambda i: (ctx_step(i), 0, 0)),
            pl.BlockSpec((n_seq, KV_W, seq_ctx), lambda i: (ctx_step(i), 0, 0)),
            pl.BlockSpec((n_seq, KV_W, seq_ctx), lambda i: (ctx_step(i), 0, 0)),
            pl.BlockSpec((1, seq_lat, D_MODEL), lambda i: (lat_step(i), 0, 0)),
        ],
        out_shape=[
            jax.ShapeDtypeStruct(x_ctx.shape, F32),
            jax.ShapeDtypeStruct((nb_ctx, KV_W, seq_ctx), F32),
            jax.ShapeDtypeStruct((nb_ctx, KV_W, seq_ctx), F32),
            jax.ShapeDtypeStruct(x_lat.shape, F32),
        ],
        scratch_shapes=[
            pltpu.VMEM((n_seq, BRANCH + 2 * KV_W, seq_ctx), F32),
            pltpu.VMEM((seq_lat // BLOCK, N_HEADS // 2, HEAD_DIM, 2 * BLOCK), BF16),
            pltpu.VMEM((seq_lat, BRANCH), F32),
            pltpu.VMEM((seq_lat // Q_BLOCK, BRANCH, Q_BLOCK), F32),
            pltpu.VMEM((seq_lat + 2 * BLOCK, KV_W), BF16),
            pltpu.VMEM((seq_lat // BLOCK + 2, KV_W, BLOCK), BF16),
            pltpu.VMEM(x_lat.shape, F32),
            pltpu.SemaphoreType.DMA((nb_lat,)),
        ],
        compiler_params=pltpu.CompilerParams(
            dimension_semantics=("arbitrary",), vmem_limit_bytes=VMEM_LIMIT),
        name="attn_layer",
    )(sink, x_ctx, x_lat, mod, nw, wqkvt, wz, wout, qw, kw, cos, sin, ckt, cvt)


def kernel(x_prompt, x_sample, cache_k_l1, cache_v_l1, c, c_ctx, norm_w_l0, w_mod_l0, b_mod_l0,
           w_in_l0, w_out_l0, norm_w_l1, w_mod_l1, b_mod_l1, w_in_l1, q_norm_w_l1, k_norm_w_l1,
           sink_l1, w_out_l1):
    nb_ctx, seq_ctx, _ = x_prompt.shape
    nb_lat = x_sample.shape[0]
    past = cache_k_l1.shape[1]
    assert 1 + nb_lat <= MOD_ROWS
    nw0 = norm_w_l0.reshape(1, D_MODEL)
    nw1 = norm_w_l1.reshape(1, D_MODEL)
    qw = q_norm_w_l1.reshape(1, HEAD_DIM)
    kw = k_norm_w_l1.reshape(1, HEAD_DIM)

    xp, xs, mod1, wqkvt1, wz1, wout1 = _fourier_layer(
        x_prompt, x_sample, nw0,
        this_layer=(c_ctx, c, w_mod_l0, b_mod_l0, w_in_l0, w_out_l0),
        next_layer=(c_ctx, c, w_mod_l1, b_mod_l1, w_in_l1, w_out_l1))

    def to_feature_major(t):
        return jnp.transpose(t, (0, 2, 3, 1)).reshape(t.shape[0], KV_W, t.shape[1])

    def from_feature_major(t):
        return jnp.transpose(t.reshape(t.shape[0], N_KV, HEAD_DIM, t.shape[2]), (0, 3, 1, 2))

    xp, new_kt, new_vt, xs = _attn_layer(xp, xs, mod1, nw1, wqkvt1, wz1, wout1, qw, kw, sink_l1,
                                         to_feature_major(cache_k_l1), to_feature_major(cache_v_l1))
    return (xp, xs, from_feature_major(new_kt), from_feature_major(new_vt))
```

```python
import functools

import numpy as np
import jax
import jax.numpy as jnp
from jax import lax
from jax.experimental import pallas as pl
from jax.experimental.pallas import tpu as pltpu

D_MODEL = 1024
BRANCH = 1024
N_GROUPS = 4
GROUP_W = BRANCH // N_GROUPS
HALF_W = GROUP_W // 2
HEAD_DIM = 64
N_HEADS = 16
N_KV = 4
GQA = N_HEADS // N_KV
KV_W = N_KV * HEAD_DIM
GRID_W = 64
WINDOW = 128
BLOCK = 128
ROPE_THETA = 10000.0
EPS = 1e-6
NEG_INF = -1e30
LANES = 128
ROW_CHUNK = 256
Q_BLOCK = 256
HALVES = Q_BLOCK // BLOCK
CTX_SEQS_PER_STEP = 2
VMEM_LIMIT = 56 * 1024 * 1024
FOURIER_VMEM_LIMIT = 60 * 1024 * 1024
MOD_ROWS = 8
ONES_ROWS = 16
LOG2E = float(np.log2(np.e))

F32 = jnp.float32
BF16 = jnp.bfloat16


def _dot(a, b):
    return jnp.dot(a, b, preferred_element_type=F32)


def _dot_nt(a, b):
    return lax.dot_general(a, b, (((1,), (1,)), ((), ())), preferred_element_type=F32)


MOD_SPEC = pl.BlockSpec((MOD_ROWS, 3 * D_MODEL), lambda b: (0, 0))


def _mod_norm(x, nw, mod):
    shift = mod[:, :D_MODEL]
    scale = mod[:, D_MODEL:2 * D_MODEL]
    y = x * lax.rsqrt(jnp.mean(x * x, axis=-1, keepdims=True) + EPS)
    return (y * nw) * (1.0 + scale) + shift


def _mod_accumulate(cctx_ref, c_ref, w_ref, b_ref, o_ref, cond_scr, first_step):
    n_lat = c_ref.shape[0]
    cond_scr[...] = jnp.zeros_like(cond_scr)
    cond_scr[0:1, :] = cctx_ref[...]
    cond_scr[1:1 + n_lat, :] = c_ref[...]
    s = jax.nn.silu(cond_scr[...]).astype(BF16)

    @pl.when(pl.program_id(0) == first_step)
    def _():
        o_ref[...] = jnp.broadcast_to(b_ref[...], o_ref.shape)

    o_ref[...] += _dot(s, w_ref[...].astype(BF16))


def _mirror_perm():
    j = np.arange(GROUP_W)
    return np.where(j <= HALF_W, j, GROUP_W + HALF_W - j)


def _perm_matrices():
    perm = _mirror_perm()
    pm = (np.arange(GROUP_W)[:, None] == perm[None, :]).astype(np.float32)
    assert (pm[:HALF_W, HALF_W:] == 0).all() and (pm[HALF_W:, :HALF_W] == 0).all()
    return pm, np.stack([pm[:HALF_W, :HALF_W], pm[HALF_W:, HALF_W:]])


def _fourier_layer_kernel(xc_ref, xl_hbm, nw_ref, m1_ref, csc_ref, ssc_ref, csl_hbm, ssl_hbm,
                          cctx0_ref, c0_ref, wmod0_ref, bmod0_ref, win0_ref, wout0_ref, pm_ref, pmh_ref,
                          cctx_ref, c_ref, wmod1_ref, bmod1_ref, win1_ref, wout1_ref,
                          oc_ref, ol_hbm, mod1_ref, wqkvt1_ref, wz1_ref, wout1b_ref,
                          mod0_scr, win_scr, wout_scr, ta_scr, tb_scr, tr_scr, z_scr, cond_scr,
                          xl_scr, csl_scr, ssl_scr, copy_sem, ol_scr, out_sem,
                          *, n_prep_steps, n_ctx_steps):
    step = pl.program_id(0)
    n_lat = xl_scr.shape[0]
    lat = step - (n_prep_steps + n_ctx_steps)

    def request_copy(r):
        return pltpu.make_async_copy(xl_hbm.at[r], xl_scr.at[r], copy_sem.at[r])

    table_copies = [pltpu.make_async_copy(csl_hbm, csl_scr, copy_sem.at[n_lat]),
                    pltpu.make_async_copy(ssl_hbm, ssl_scr, copy_sem.at[n_lat + 1])]

    def result_copy(r, c):
        rows = pl.ds(c * ROW_CHUNK, ROW_CHUNK)
        return pltpu.make_async_copy(ol_scr.at[r, rows], ol_hbm.at[r, rows], out_sem.at[r, c])

    @pl.when(step == n_prep_steps)
    def _():
        for copy in table_copies + [request_copy(r) for r in range(n_lat)]:
            copy.start()

    @pl.when(step < n_prep_steps)
    def _():
        _mod_accumulate(cctx0_ref, c0_ref, wmod0_ref, bmod0_ref, mod0_scr, cond_scr, 0)
        rows = pl.ds(pl.multiple_of(step * HALF_W, HALF_W), HALF_W)
        w = win0_ref[...].astype(BF16)
        win_scr[rows, :BRANCH] = w[:, :BRANCH]
        pm = pm_ref[...]
        for g in range(N_GROUPS):
            cols = slice(BRANCH + g * GROUP_W, BRANCH + (g + 1) * GROUP_W)
            win_scr[rows, cols] = _dot(w[:, cols], pm).astype(BF16)
        wout_scr[rows, :] = _dot(pmh_ref[step % 2], wout0_ref[...].astype(BF16)).astype(BF16)

    @pl.when(jnp.logical_and(step >= n_prep_steps, step < n_prep_steps + n_ctx_steps))
    def _():
        _mod_accumulate(cctx_ref, c_ref, wmod1_ref, bmod1_ref, mod1_ref, cond_scr, n_prep_steps)
        w = win1_ref[...]
        wqkvt1_ref[...] = w[:, :BRANCH + 2 * KV_W].T.astype(BF16)
        wz1_ref[...] = w[:, BRANCH + 2 * KV_W:].astype(BF16)
        wout1b_ref[...] = wout1_ref[...].astype(BF16)
        _fourier_body(xc_ref, oc_ref, mod0_scr[0:1, :], nw_ref, win_scr, wout_scr, m1_ref, csc_ref, ssc_ref,
                      ta_scr, tb_scr, tr_scr, z_scr)

    @pl.when(step >= n_prep_steps + n_ctx_steps)
    def _():
        @pl.when(lat == 0)
        def _():
            for copy in table_copies:
                copy.wait()

        request_copy(lat).wait()
        mod = mod0_scr[pl.ds(1 + lat, 1), :]
        _fourier_body(xl_scr.at[pl.ds(lat, 1)], ol_scr.at[pl.ds(lat, 1)], mod, nw_ref, win_scr, wout_scr,
                      m1_ref, csl_scr, ssl_scr, ta_scr, tb_scr, tr_scr, z_scr,
                      rows_done=lambda _, rows: result_copy(lat, rows.start // ROW_CHUNK).start())

        @pl.when(lat == n_lat - 1)
        def _():
            for r in range(n_lat):
                for c in range(out_sem.shape[1]):
                    result_copy(r, c).wait()


def _fourier_body(x_ref, o_ref, mod, nw_ref, win_ref, wout_ref, m1_ref, cs_ref, ss_ref,
                  ta_scr, tb_scr, tr_scr, z_scr, rows_done=None):
    gate = mod[:, 2 * D_MODEL:]
    nw = nw_ref[...]
    n_seq, seq, _ = x_ref.shape
    n_chunks = seq // ROW_CHUNK
    lane = lax.broadcasted_iota(jnp.int32, (ROW_CHUNK, HALF_W), 1)
    for i in range(n_seq):
        for c in range(n_chunks):
            rows = slice(c * ROW_CHUNK, (c + 1) * ROW_CHUNK)
            srows = slice(i * seq + c * ROW_CHUNK, i * seq + (c + 1) * ROW_CHUNK)
            h = _mod_norm(x_ref[i, rows, :], nw, mod).astype(BF16)
            uz = _dot(h, win_ref[...])
            z_scr[srows, :] = uz[:, BRANCH:]
            u = uz[:, :BRANCH].astype(BF16)
            tr = jnp.zeros((ROW_CHUNK, HALF_W), F32)
            for g in range(N_GROUPS):
                t = _dot(u[:, g * GROUP_W:(g + 1) * GROUP_W], m1_ref[...])
                half = slice(g * HALF_W, (g + 1) * HALF_W)
                ta_scr[srows, half] = t[:, :HALF_W].astype(BF16)
                tb = t[:, HALF_W:]
                tb_scr[srows, half] = tb.astype(BF16)
                tr = jnp.where(lane == g, tb if g == 0 else pltpu.roll(tb, g, axis=1), tr)
            tr_scr[srows, :] = tr.astype(BF16)
    for i in range(n_seq):
        seq_rows = slice(i * seq, (i + 1) * seq)
        for c in range(n_chunks):
            rows = slice(c * ROW_CHUNK, (c + 1) * ROW_CHUNK)
            srows = slice(i * seq + c * ROW_CHUNK, i * seq + (c + 1) * ROW_CHUNK)
            cs = cs_ref[rows, :]
            p = _dot(cs, ta_scr[seq_rows, :])
            q = _dot(ss_ref[rows, :], tb_scr[seq_rows, :])
            r = _dot(cs, tr_scr[seq_rows, :])
            parts = []
            for g in range(N_GROUPS):
                half = slice(g * HALF_W, (g + 1) * HALF_W)
                pg, qg = p[:, half], q[:, half]
                rg = r if g == 0 else pltpu.roll(r, HALF_W - g, axis=1)
                parts.append(jnp.where(lane == 0, pg, pg - qg))
                parts.append(jnp.where(lane == 0, rg, pg + qg))
            y = jnp.concatenate(parts, axis=1)
            y = (y * jax.nn.silu(z_scr[srows, :])).astype(BF16)
            o_ref[i, rows, :] = x_ref[i, rows, :] + gate * _dot(y, wout_ref[...])
            if rows_done is not None:
                rows_done(i, rows)


def _dft_tables(seq):
    c = np.arange(GROUP_W)[:, None]
    k = np.arange(HALF_W)[None, :]
    cos_lo = np.cos(2.0 * np.pi * ((c * k) % GROUP_W) / GROUP_W)
    sin_lo = np.sin(2.0 * np.pi * ((c * k) % GROUP_W) / GROUP_W)
    sin_lo[:, 0] = np.cos(np.pi * c[:, 0])
    m1 = np.concatenate([cos_lo, sin_lo], axis=1) / np.sqrt(GROUP_W)
    n = np.arange(seq)
    ang = 2.0 * np.pi * ((n[:, None] * n[None, :]) % seq) / seq
    cs = np.cos(ang) / np.sqrt(seq)
    ss = np.sin(ang) / np.sqrt(seq)
    return m1.astype(np.float32), cs.astype(np.float32), ss.astype(np.float32)


def _const_spec(shape):
    return pl.BlockSpec(shape, lambda b: (0,) * len(shape))


def _fourier_layer(x_ctx, x_lat, nw, this_layer, next_layer):
    nb_ctx, seq_ctx, _ = x_ctx.shape
    nb_lat, seq_lat, _ = x_lat.shape
    n_seq = CTX_SEQS_PER_STEP
    assert nb_ctx % n_seq == 0
    n_prep = D_MODEL // HALF_W
    n_ctx = nb_ctx // n_seq
    assert D_MODEL % (n_ctx * LANES) == 0 and n_seq * seq_ctx <= seq_lat
    rows = D_MODEL // n_ctx
    assert rows == HALF_W
    c_ctx, c, w_mod0, b_mod0, w_in0, w_out0 = this_layer
    _, _, w_mod1, b_mod1, w_in1, w_out1 = next_layer
    n_lat = c.shape[0]
    n_qkvz = 2 * BRANCH + 2 * KV_W
    m1, csc, ssc = (jnp.asarray(t).astype(BF16) for t in _dft_tables(seq_ctx))
    _, csl, ssl = (jnp.asarray(t).astype(BF16) for t in _dft_tables(seq_lat))
    pm, pmh = (jnp.asarray(t).astype(BF16) for t in _perm_matrices())
    cc = c_ctx.reshape(1, D_MODEL)

    def prep_step(i):
        return jnp.minimum(i, n_prep - 1)

    def ctx_step(i):
        return jnp.clip(i - n_prep, 0, n_ctx - 1)

    def chunk_specs(step_fn, w_in_cols):
        return [pl.BlockSpec((1, HALF_W), lambda i: (0, step_fn(i))),
                pl.BlockSpec((n_lat, HALF_W), lambda i: (0, step_fn(i))),
                pl.BlockSpec((HALF_W, 3 * D_MODEL), lambda i: (step_fn(i), 0)),
                _const_spec((1, 3 * D_MODEL)),
                pl.BlockSpec((HALF_W, w_in_cols), lambda i: (step_fn(i), 0)),
                pl.BlockSpec((HALF_W, D_MODEL), lambda i: (step_fn(i), 0))]

    in_specs = [
        pl.BlockSpec((n_seq, seq_ctx, D_MODEL), lambda i: (ctx_step(i), 0, 0)),
        pl.BlockSpec(memory_space=pl.ANY),
        _const_spec((1, D_MODEL)),
        _const_spec((GROUP_W, GROUP_W)),
        _const_spec((seq_ctx, seq_ctx)),
        _const_spec((seq_ctx, seq_ctx)),
        pl.BlockSpec(memory_space=pl.ANY),
        pl.BlockSpec(memory_space=pl.ANY),
    ] + chunk_specs(prep_step, 2 * BRANCH) + [
        _const_spec((GROUP_W, GROUP_W)),
        _const_spec((2, HALF_W, HALF_W)),
    ] + chunk_specs(ctx_step, n_qkvz)
    out_specs = [
        pl.BlockSpec((n_seq, seq_ctx, D_MODEL), lambda i: (ctx_step(i), 0, 0)),
        pl.BlockSpec(memory_space=pl.ANY),
        MOD_SPEC,
        pl.BlockSpec((BRANCH + 2 * KV_W, rows), lambda i: (0, ctx_step(i))),
        pl.BlockSpec((rows, BRANCH), lambda i: (ctx_step(i), 0)),
        pl.BlockSpec((rows, D_MODEL), lambda i: (ctx_step(i), 0)),
    ]
    out_shape = [
        jax.ShapeDtypeStruct(x_ctx.shape, F32),
        jax.ShapeDtypeStruct(x_lat.shape, F32),
        jax.ShapeDtypeStruct((MOD_ROWS, 3 * D_MODEL), F32),
        jax.ShapeDtypeStruct((BRANCH + 2 * KV_W, D_MODEL), BF16),
        jax.ShapeDtypeStruct((D_MODEL, BRANCH), BF16),
        jax.ShapeDtypeStruct((BRANCH, D_MODEL), BF16),
    ]
    return pl.pallas_call(
        functools.partial(_fourier_layer_kernel, n_prep_steps=n_prep, n_ctx_steps=n_ctx),
        grid=(n_prep + n_ctx + nb_lat,),
        in_specs=in_specs,
        out_specs=out_specs,
        out_shape=out_shape,
        scratch_shapes=[
            pltpu.VMEM((MOD_ROWS, 3 * D_MODEL), F32),
            pltpu.VMEM((D_MODEL, 2 * BRANCH), BF16),
            pltpu.VMEM((BRANCH, D_MODEL), BF16),
            pltpu.VMEM((seq_lat, N_GROUPS * HALF_W), BF16),
            pltpu.VMEM((seq_lat, N_GROUPS * HALF_W), BF16),
            pltpu.VMEM((seq_lat, HALF_W), BF16),
            pltpu.VMEM((seq_lat, BRANCH), F32),
            pltpu.VMEM((MOD_ROWS, HALF_W), F32),
            pltpu.VMEM(x_lat.shape, F32),
            pltpu.VMEM((seq_lat, seq_lat), BF16),
            pltpu.VMEM((seq_lat, seq_lat), BF16),
            pltpu.SemaphoreType.DMA((nb_lat + 2,)),
            pltpu.VMEM(x_lat.shape, F32),
            pltpu.SemaphoreType.DMA((nb_lat, seq_lat // ROW_CHUNK)),
        ],
        compiler_params=pltpu.CompilerParams(
            dimension_semantics=("arbitrary",), vmem_limit_bytes=FOURIER_VMEM_LIMIT),
        name="fourier_layer",
    )(x_ctx, x_lat, nw, m1, csc, ssc, csl, ssl,
      cc, c, w_mod0, b_mod0.reshape(1, 3 * D_MODEL), w_in0, w_out0, pm, pmh,
      cc, c, w_mod1, b_mod1.reshape(1, 3 * D_MODEL), w_in1, w_out1)


def _head_weight_tile(w_ref, n_tokens):
    row = jnp.broadcast_to(w_ref[...], (HEAD_DIM, HEAD_DIM))
    ii = lax.broadcasted_iota(jnp.int32, (HEAD_DIM, HEAD_DIM), 0)
    jj = lax.broadcasted_iota(jnp.int32, (HEAD_DIM, HEAD_DIM), 1)
    col = jnp.sum(jnp.where(ii == jj, row, 0.0), axis=1, keepdims=True)
    return jnp.broadcast_to(col, (HEAD_DIM, n_tokens))


def _head_rms(t, w):
    return (t * lax.rsqrt(jnp.mean(t * t, axis=0, keepdims=True) + EPS)) * w


def _rope_t(t, cos, sin):
    half = HEAD_DIM // 2
    x1, x2 = t[:half], t[half:]
    return jnp.concatenate([x1 * cos - x2 * sin, x1 * sin + x2 * cos], axis=0)


def _head_scores(qn, g, keys, biases):
    zeros = jnp.zeros_like(qn)
    qz = jnp.concatenate([qn, zeros] if g % 2 == 0 else [zeros, qn], axis=0)
    blk = slice((g // 2) * LANES, (g // 2 + 1) * LANES)
    scores = []
    smax = None
    for k, bias in zip(keys, biases):
        s = _dot(k[:, blk], qz)
        if bias is not None:
            s = jnp.concatenate(
                [s[c * BLOCK:(c + 1) * BLOCK] if b is None else
                 s[c * BLOCK:(c + 1) * BLOCK] + jnp.concatenate([b] * (s.shape[1] // b.shape[1]), axis=1)
                 for c, b in enumerate(bias)], axis=0)
        cmax = jnp.max(s, axis=0, keepdims=True)
        smax = cmax if smax is None else jnp.maximum(smax, cmax)
        scores.append(s)
    return scores, smax


def _with_ones_rows(vt):
    return jnp.concatenate([vt, jnp.ones((ONES_ROWS, vt.shape[1]), vt.dtype)], axis=0)


def _head_probs(scored, sink2):
    scores, smax = scored
    m = jnp.maximum(smax, sink2)
    return jnp.concatenate([jnp.exp2(s - m).astype(BF16) for s in scores], axis=0), m


def _head_pv(probs, values_t, sink2):
    p, m = probs
    acc = _dot(values_t, p)
    den = acc[HEAD_DIM:HEAD_DIM + 1] + jnp.exp2(sink2 - m)
    return acc[:HEAD_DIM] * (1.0 / den)


def _attend_heads(n_units, stage, scores_fn, probs_fn, pv_fn, fillers=()):
    n_stages = n_units // stage
    pending = [scores_fn(u) for u in range(stage)]
    for g in range(n_stages):
        units = range(g * stage, (g + 1) * stage)
        if g < len(fillers):
            fillers[g]()
        nxt = [scores_fn(u) for u in range((g + 1) * stage, (g + 2) * stage)] if g + 1 < n_stages else None
        probs = [probs_fn(u, sc) for u, sc in zip(units, pending)]
        for u, pr in zip(units, probs):
            pv_fn(u, pr)
        pending = nxt


def _gate_out(x, o, z, gate, wout):
    y = (o * jax.nn.silu(z)).astype(BF16)
    return x + gate * _dot(y, wout)


def _attn_ctx_body(sink_ref, x_ref, mod, nw_ref, wqkvt_ref, wz_ref, wout_ref, qw_ref, kw_ref,
                   o_ref, kto_ref, vto_ref, qkvt_scr, z_scr, ot_scr):
    seq = x_ref.shape[1]
    n_seq = x_ref.shape[0]
    gate = mod[:, 2 * D_MODEL:]
    kw = _head_weight_tile(kw_ref, seq)
    qw = _head_weight_tile(qw_ref, seq)

    def project(i):
        h = _mod_norm(x_ref[i], nw_ref[...], mod).astype(BF16)
        qkvt_scr[i] = _dot_nt(wqkvt_ref[...], h)
        z_scr[i * seq:(i + 1) * seq, :] = _dot(h, wz_ref[...])

    def keys_values(i):
        knt = jnp.concatenate(
            [_head_rms(qkvt_scr[i, BRANCH + g * HEAD_DIM:BRANCH + (g + 1) * HEAD_DIM, :], kw)
             for g in range(N_KV)], axis=0)
        kto_ref[i] = knt
        vtf = qkvt_scr[i, BRANCH + KV_W:, :]
        vto_ref[i] = vtf
        vt = vtf.astype(BF16)
        return (knt.T.astype(BF16),
                [_with_ones_rows(vt[g * HEAD_DIM:(g + 1) * HEAD_DIM]) for g in range(N_KV)])

    def output(i):
        o_ref[i] = _gate_out(x_ref[i], ot_scr[i].T, z_scr[i * seq:(i + 1) * seq, :], gate, wout_ref[...])

    for i in range(n_seq):
        project(i)
    kv = [keys_values(i) for i in range(n_seq)]

    def scores_fn(u):
        i, hd = divmod(u, N_HEADS)
        t = qkvt_scr[i, hd * HEAD_DIM:(hd + 1) * HEAD_DIM, :]
        qn = (_head_rms(t, qw) * (HEAD_DIM ** -0.5 * LOG2E)).astype(BF16)
        return _head_scores(qn, hd // GQA, [kv[i][0]], [None])

    def probs_fn(u, sc):
        return _head_probs(sc, sink_ref[u % N_HEADS] * LOG2E)

    def pv_fn(u, pr):
        i, hd = divmod(u, N_HEADS)
        ot_scr[i, hd * HEAD_DIM:(hd + 1) * HEAD_DIM, :] = _head_pv(
            pr, kv[i][1][hd // GQA], sink_ref[hd] * LOG2E)

    fillers = [lambda: None] + [functools.partial(output, i) for i in range(n_seq - 1)]
    _attend_heads(n_seq * N_HEADS, N_HEADS, scores_fn, probs_fn, pv_fn, fillers)
    output(n_seq - 1)


def _attn_lat_body(sink_ref, x_ref, mod, nw_ref, wqkvt_ref, wz_ref, wout_ref, qw_ref, kw_ref,
                   cos_ref, sin_ref, ck_ref, cv_ref, o_ref,
                   q_scr, z_scr, ot_scr, k_scr, vt_scr):
    seq = x_ref.shape[1]
    gate = mod[:, 2 * D_MODEL:]
    nw = nw_ref[...]
    qw = _head_weight_tile(qw_ref, Q_BLOCK)
    kw = _head_weight_tile(kw_ref, Q_BLOCK)
    n_blocks = seq // Q_BLOCK
    kv_blocks = seq // BLOCK
    k_scr[0:BLOCK, :] = jnp.zeros((BLOCK, KV_W), BF16)
    k_scr[BLOCK + seq:2 * BLOCK + seq, :] = jnp.zeros((BLOCK, KV_W), BF16)
    vt_scr[0] = jnp.zeros((KV_W, BLOCK), BF16)
    vt_scr[kv_blocks + 1] = jnp.zeros((KV_W, BLOCK), BF16)

    def project(c):
        rows = slice(c * Q_BLOCK, (c + 1) * Q_BLOCK)
        h = _mod_norm(x_ref[0, rows, :], nw, mod).astype(BF16)
        z_scr[rows, :] = _dot(h, wz_ref[...])
        qkvt = _dot_nt(wqkvt_ref[...], h)
        cos = cos_ref[:, rows]
        sin = sin_ref[:, rows]
        for hd in range(N_HEADS):
            hr = slice(hd * HEAD_DIM, (hd + 1) * HEAD_DIM)
            t = _rope_t(_head_rms(qkvt[hr], qw), cos, sin)
            t = (t * (HEAD_DIM ** -0.5 * LOG2E)).astype(BF16)
            for j in range(HALVES):
                q_scr[c * HALVES + j, hd // 2, :, (hd % 2) * BLOCK:(hd % 2 + 1) * BLOCK] = (
                    t[:, j * BLOCK:(j + 1) * BLOCK])
        knt = jnp.concatenate(
            [_rope_t(_head_rms(qkvt[BRANCH + g * HEAD_DIM:BRANCH + (g + 1) * HEAD_DIM], kw), cos, sin)
             for g in range(N_KV)], axis=0)
        k_scr[BLOCK + c * Q_BLOCK:BLOCK + (c + 1) * Q_BLOCK, :] = knt.T.astype(BF16)
        vt = qkvt[BRANCH + KV_W:].astype(BF16)
        for j in range(HALVES):
            vt_scr[1 + c * HALVES + j] = vt[:, j * BLOCK:(j + 1) * BLOCK]

    ckb = ck_ref[0].T.astype(BF16)
    cvt = cv_ref[0].astype(BF16)

    win_len = 3 * BLOCK
    n_pairs = N_HEADS // 2
    kj = lax.broadcasted_iota(jnp.int32, (BLOCK, BLOCK), 0)
    qi = lax.broadcasted_iota(jnp.int32, (BLOCK, BLOCK), 1)
    first_head = lax.broadcasted_iota(jnp.int32, (1, 2 * BLOCK), 1) < BLOCK
    assert WINDOW >= BLOCK - 1

    def band_bias(c):
        rel = kj + (c - 1) * BLOCK - qi
        return jnp.where((rel >= -WINDOW) & (rel <= WINDOW), 0.0, NEG_INF)

    band_before, band_after = band_bias(0), band_bias(2)

    def half_operands(hb):
        r0 = pl.multiple_of(hb * BLOCK, BLOCK)
        valid = (jnp.where(hb > 0, band_before, NEG_INF), None,
                 jnp.where(hb < kv_blocks - 1, band_after, NEG_INF))
        kwin = k_scr[pl.ds(r0, win_len), :]
        vall = jnp.concatenate([vt_scr[hb + j] for j in range(win_len // BLOCK)] + [cvt], axis=1)
        valls = [_with_ones_rows(vall[g * HEAD_DIM:(g + 1) * HEAD_DIM]) for g in range(N_KV)]
        return valid, kwin, valls

    def pair_sink2(pair):
        return jnp.where(first_head, sink_ref[2 * pair], sink_ref[2 * pair + 1]) * LOG2E

    def attend(n, carry):
        halves = [half_operands(n * HALVES + j) for j in range(HALVES)]

        def scores_fn(u):
            j, pair = divmod(u, n_pairs)
            valid, kwin, _ = halves[j]
            return _head_scores(q_scr[n * HALVES + j, pair], pair // (GQA // 2), [kwin, ckb], [valid, None])

        def probs_fn(u, sc):
            return _head_probs(sc, pair_sink2(u % n_pairs))

        def pv_fn(u, pr):
            j, pair = divmod(u, n_pairs)
            o = _head_pv(pr, halves[j][2][pair // (GQA // 2)], pair_sink2(pair))
            for i in range(2):
                hd = 2 * pair + i
                ot_scr[n, hd * HEAD_DIM:(hd + 1) * HEAD_DIM, j * BLOCK:(j + 1) * BLOCK] = (
                    o[:, i * BLOCK:(i + 1) * BLOCK])

        _attend_heads(HALVES * n_pairs, GQA, scores_fn, probs_fn, pv_fn)
        return carry

    for c in range(n_blocks):
        project(c)
    lax.fori_loop(0, n_blocks, attend, 0)
    for c in range(n_blocks):
        rows = slice(c * Q_BLOCK, (c + 1) * Q_BLOCK)
        o_ref[0, rows, :] = _gate_out(x_ref[0, rows, :], ot_scr[c].T, z_scr[rows, :], gate, wout_ref[...])


def _rope_tables_t(seq):
    pos = np.arange(seq)
    n_freq = HEAD_DIM // 4
    inv = ROPE_THETA ** (-np.arange(n_freq, dtype=np.float64) / n_freq)
    ang = np.concatenate([(pos // GRID_W)[:, None] * inv, (pos % GRID_W)[:, None] * inv], axis=-1)
    return np.cos(ang).T.astype(np.float32), np.sin(ang).T.astype(np.float32)


def _attn_layer_kernel(sink_ref, xc_ref, xl_ref, mod_ref, nw_ref, wqkvt_ref, wz_ref, wout_ref, qw_ref, kw_ref,
                       cos_ref, sin_ref, ck_ref, cv_ref, oc_ref, kto_ref, vto_ref, ol_ref,
                       qkvt_scr, q_scr, z_scr, ot_scr, k_scr, vt_scr, *, n_ctx_steps):
    step = pl.program_id(0)

    @pl.when(step < n_ctx_steps)
    def _():
        _attn_ctx_body(sink_ref, xc_ref, mod_ref[0:1, :], nw_ref, wqkvt_ref, wz_ref, wout_ref, qw_ref, kw_ref,
                       oc_ref, kto_ref, vto_ref, qkvt_scr, z_scr, ot_scr)

    @pl.when(step >= n_ctx_steps)
    def _():
        mod = mod_ref[pl.ds(1 + step - n_ctx_steps, 1), :]
        _attn_lat_body(sink_ref, xl_ref, mod, nw_ref, wqkvt_ref, wz_ref, wout_ref, qw_ref, kw_ref,
                       cos_ref, sin_ref, ck_ref, cv_ref, ol_ref, q_scr, z_scr, ot_scr, k_scr, vt_scr)


def _attn_layer(x_ctx, x_lat, mod, nw, wqkvt, wz, wout, qw, kw, sink, ckt, cvt):
    nb_ctx, seq_ctx, _ = x_ctx.shape
    nb_lat, seq_lat, _ = x_lat.shape
    past = ckt.shape[2]
    n_seq = CTX_SEQS_PER_STEP
    assert nb_ctx % n_seq == 0 and seq_ctx == Q_BLOCK and n_seq <= seq_lat // Q_BLOCK
    n_ctx = nb_ctx // n_seq
    cos, sin = (jnp.asarray(t) for t in _rope_tables_t(seq_lat))

    def ctx_step(i):
        return jnp.minimum(i, n_ctx - 1)

    def lat_step(i):
        return jnp.maximum(i - n_ctx, 0)

    return pl.pallas_call(
        functools.partial(_attn_layer_kernel, n_ctx_steps=n_ctx),
        grid=(n_ctx + nb_lat,),
        in_specs=[
            pl.BlockSpec(memory_space=pltpu.SMEM),
            pl.BlockSpec((n_seq, seq_ctx, D_MODEL), lambda i: (ctx_step(i), 0, 0)),
            pl.BlockSpec((1, seq_lat, D_MODEL), lambda i: (lat_step(i), 0, 0)),
            MOD_SPEC,
            _const_spec((1, D_MODEL)),
            _const_spec((BRANCH + 2 * KV_W, D_MODEL)),
            _const_spec((D_MODEL, BRANCH)),
            _const_spec((BRANCH, D_MODEL)),
            _const_spec((1, HEAD_DIM)),
            _const_spec((1, HEAD_DIM)),
            _const_spec((HEAD_DIM // 2, seq_lat)),
            _const_spec((HEAD_DIM // 2, seq_lat)),
            pl.BlockSpec((1, KV_W, past), lambda i: (lat_step(i), 0, 0)),
            pl.BlockSpec((1, KV_W, past), lambda i: (lat_step(i), 0, 0)),
        ],
        out_specs=[
            pl.BlockSpec((n_seq, seq_ctx, D_MODEL), lambda i: (ctx_step(i), 0, 0)),
            pl.BlockSpec((n_seq, KV_W, seq_ctx), lambda i: (ctx_step(i), 0, 0)),
            pl.BlockSpec((n_seq, KV_W, seq_ctx), lambda i: (ctx_step(i), 0, 0)),
            pl.BlockSpec((1, seq_lat, D_MODEL), lambda i: (lat_step(i), 0, 0)),
        ],
        out_shape=[
            jax.ShapeDtypeStruct(x_ctx.shape, F32),
            jax.ShapeDtypeStruct((nb_ctx, KV_W, seq_ctx), F32),
            jax.ShapeDtypeStruct((nb_ctx, KV_W, seq_ctx), F32),
            jax.ShapeDtypeStruct(x_lat.shape, F32),
        ],
        scratch_shapes=[
            pltpu.VMEM((n_seq, BRANCH + 2 * KV_W, seq_ctx), F32),
            pltpu.VMEM((seq_lat // BLOCK, N_HEADS // 2, HEAD_DIM, 2 * BLOCK), BF16),
            pltpu.VMEM((seq_lat, BRANCH), F32),
            pltpu.VMEM((seq_lat // Q_BLOCK, BRANCH, Q_BLOCK), F32),
            pltpu.VMEM((seq_lat + 2 * BLOCK, KV_W), BF16),
            pltpu.VMEM((seq_lat // BLOCK + 2, KV_W, BLOCK), BF16),
        ],
        compiler_params=pltpu.CompilerParams(
            dimension_semantics=("arbitrary",), vmem_limit_bytes=VMEM_LIMIT),
        name="attn_layer",
    )(sink, x_ctx, x_lat, mod, nw, wqkvt, wz, wout, qw, kw, cos, sin, ckt, cvt)


def kernel(x_prompt, x_sample, cache_k_l1, cache_v_l1, c, c_ctx, norm_w_l0, w_mod_l0, b_mod_l0,
           w_in_l0, w_out_l0, norm_w_l1, w_mod_l1, b_mod_l1, w_in_l1, q_norm_w_l1, k_norm_w_l1,
           sink_l1, w_out_l1):
    nb_ctx, seq_ctx, _ = x_prompt.shape
    nb_lat = x_sample.shape[0]
    past = cache_k_l1.shape[1]
    assert 1 + nb_lat <= MOD_ROWS
    nw0 = norm_w_l0.reshape(1, D_MODEL)
    nw1 = norm_w_l1.reshape(1, D_MODEL)
    qw = q_norm_w_l1.reshape(1, HEAD_DIM)
    kw = k_norm_w_l1.reshape(1, HEAD_DIM)

    xp, xs, mod1, wqkvt1, wz1, wout1 = _fourier_layer(
        x_prompt, x_sample, nw0,
        this_layer=(c_ctx, c, w_mod_l0, b_mod_l0, w_in_l0, w_out_l0),
        next_layer=(c_ctx, c, w_mod_l1, b_mod_l1, w_in_l1, w_out_l1))

    def to_feature_major(t):
        return jnp.transpose(t, (0, 2, 3, 1)).reshape(t.shape[0], KV_W, t.shape[1])

    def from_feature_major(t):
        return jnp.transpose(t.reshape(t.shape[0], N_KV, HEAD_DIM, t.shape[2]), (0, 3, 1, 2))

    xp, new_kt, new_vt, xs = _attn_layer(xp, xs, mod1, nw1, wqkvt1, wz1, wout1, qw, kw, sink_l1,
                                         to_feature_major(cache_k_l1), to_feature_major(cache_v_l1))
    return (xp, xs, from_feature_major(new_kt), from_feature_major(new_vt))
```

```python
import functools

import numpy as np
import jax
import jax.numpy as jnp
from jax import lax
from jax.experimental import pallas as pl
from jax.experimental.pallas import tpu as pltpu

D_MODEL = 1024
BRANCH = 1024
N_GROUPS = 4
GROUP_W = BRANCH // N_GROUPS
HALF_W = GROUP_W // 2
HEAD_DIM = 64
N_HEADS = 16
N_KV = 4
GQA = N_HEADS // N_KV
KV_W = N_KV * HEAD_DIM
GRID_W = 64
WINDOW = 128
BLOCK = 128
ROPE_THETA = 10000.0
EPS = 1e-6
NEG_INF = -1e30
LANES = 128
ROW_CHUNK = 256
Q_BLOCK = 256
HALVES = Q_BLOCK // BLOCK
CTX_SEQS_PER_STEP = 2
VMEM_LIMIT = 56 * 1024 * 1024
FOURIER_VMEM_LIMIT = 60 * 1024 * 1024
MOD_ROWS = 8
ONES_ROWS = 16
LOG2E = float(np.log2(np.e))

F32 = jnp.float32
BF16 = jnp.bfloat16


def _dot(a, b):
    return jnp.dot(a, b, preferred_element_type=F32)


def _dot_nt(a, b):
    return lax.dot_general(a, b, (((1,), (1,)), ((), ())), preferred_element_type=F32)


MOD_SPEC = pl.BlockSpec((MOD_ROWS, 3 * D_MODEL), lambda b: (0, 0))


def _mod_norm(x, nw, mod):
    shift = mod[:, :D_MODEL]
    scale = mod[:, D_MODEL:2 * D_MODEL]
    y = x * lax.rsqrt(jnp.mean(x * x, axis=-1, keepdims=True) + EPS)
    return (y * nw) * (1.0 + scale) + shift


def _mod_accumulate(cctx_ref, c_ref, w_ref, b_ref, o_ref, cond_scr, first_step):
    n_lat = c_ref.shape[0]
    cond_scr[...] = jnp.zeros_like(cond_scr)
    cond_scr[0:1, :] = cctx_ref[...]
    cond_scr[1:1 + n_lat, :] = c_ref[...]
    s = jax.nn.silu(cond_scr[...]).astype(BF16)

    @pl.when(pl.program_id(0) == first_step)
    def _():
        o_ref[...] = jnp.broadcast_to(b_ref[...], o_ref.shape)

    o_ref[...] += _dot(s, w_ref[...].astype(BF16))


def _mirror_perm():
    j = np.arange(GROUP_W)
    return np.where(j <= HALF_W, j, GROUP_W + HALF_W - j)


def _perm_matrices():
    perm = _mirror_perm()
    pm = (np.arange(GROUP_W)[:, None] == perm[None, :]).astype(np.float32)
    assert (pm[:HALF_W, HALF_W:] == 0).all() and (pm[HALF_W:, :HALF_W] == 0).all()
    return pm, np.stack([pm[:HALF_W, :HALF_W], pm[HALF_W:, HALF_W:]])


def _fourier_layer_kernel(xc_ref, xl_hbm, nw_ref, m1_ref, csc_ref, ssc_ref, csl_hbm, ssl_hbm,
                          cctx0_ref, c0_ref, wmod0_ref, bmod0_ref, win0_ref, wout0_ref, pm_ref, pmh_ref,
                          cctx_ref, c_ref, wmod1_ref, bmod1_ref, win1_ref, wout1_ref,
                          oc_ref, ol_hbm, mod1_ref, wqkvt1_ref, wz1_ref, wout1b_ref,
                          mod0_scr, win_scr, wout_scr, ta_scr, tb_scr, tr_scr, z_scr, cond_scr,
                          xl_scr, csl_scr, ssl_scr, copy_sem, ol_scr, out_sem,
                          *, n_prep_steps, n_ctx_steps):
    step = pl.program_id(0)
    n_lat = xl_scr.shape[0]
    lat = step - (n_prep_steps + n_ctx_steps)

    def request_copy(r):
        return pltpu.make_async_copy(xl_hbm.at[r], xl_scr.at[r], copy_sem.at[r])

    table_copies = [pltpu.make_async_copy(csl_hbm, csl_scr, copy_sem.at[n_lat]),
                    pltpu.make_async_copy(ssl_hbm, ssl_scr, copy_sem.at[n_lat + 1])]

    def result_copy(r, c):
        rows = pl.ds(c * ROW_CHUNK, ROW_CHUNK)
        return pltpu.make_async_copy(ol_scr.at[r, rows], ol_hbm.at[r, rows], out_sem.at[r, c])

    @pl.when(step == n_prep_steps)
    def _():
        for copy in table_copies + [request_copy(r) for r in range(n_lat)]:
            copy.start()

    @pl.when(step < n_prep_steps)
    def _():
        _mod_accumulate(cctx0_ref, c0_ref, wmod0_ref, bmod0_ref, mod0_scr, cond_scr, 0)
        rows = pl.ds(pl.multiple_of(step * HALF_W, HALF_W), HALF_W)
        w = win0_ref[...].astype(BF16)
        win_scr[rows, :BRANCH] = w[:, :BRANCH]
        pm = pm_ref[...]
        for g in range(N_GROUPS):
            cols = slice(BRANCH + g * GROUP_W, BRANCH + (g + 1) * GROUP_W)
            win_scr[rows, cols] = _dot(w[:, cols], pm).astype(BF16)
        wout_scr[rows, :] = _dot(pmh_ref[step % 2], wout0_ref[...].astype(BF16)).astype(BF16)

    @pl.when(jnp.logical_and(step >= n_prep_steps, step < n_prep_steps + n_ctx_steps))
    def _():
        _mod_accumulate(cctx_ref, c_ref, wmod1_ref, bmod1_ref, mod1_ref, cond_scr, n_prep_steps)
        w = win1_ref[...]
        wqkvt1_ref[...] = w[:, :BRANCH + 2 * KV_W].T.astype(BF16)
        wz1_ref[...] = w[:, BRANCH + 2 * KV_W:].astype(BF16)
        wout1b_ref[...] = wout1_ref[...].astype(BF16)
        _fourier_body(xc_ref, oc_ref, mod0_scr[0:1, :], nw_ref, win_scr, wout_scr, m1_ref, csc_ref, ssc_ref,
                      ta_scr, tb_scr, tr_scr, z_scr)

    @pl.when(step >= n_prep_steps + n_ctx_steps)
    def _():
        @pl.when(lat == 0)
        def _():
            for copy in table_copies:
                copy.wait()

        request_copy(lat).wait()
        mod = mod0_scr[pl.ds(1 + lat, 1), :]
        _fourier_body(xl_scr.at[pl.ds(lat, 1)], ol_scr.at[pl.ds(lat, 1)], mod, nw_ref, win_scr, wout_scr,
                      m1_ref, csl_scr, ssl_scr, ta_scr, tb_scr, tr_scr, z_scr,
                      rows_done=lambda _, rows: result_copy(lat, rows.start // ROW_CHUNK).start())

        @pl.when(lat == n_lat - 1)
        def _():
            for r in range(n_lat):
                for c in range(out_sem.shape[1]):
                    result_copy(r, c).wait()


def _fourier_body(x_ref, o_ref, mod, nw_ref, win_ref, wout_ref, m1_ref, cs_ref, ss_ref,
                  ta_scr, tb_scr, tr_scr, z_scr, rows_done=None):
    gate = mod[:, 2 * D_MODEL:]
    nw = nw_ref[...]
    n_seq, seq, _ = x_ref.shape
    n_chunks = seq // ROW_CHUNK
    lane = lax.broadcasted_iota(jnp.int32, (ROW_CHUNK, HALF_W), 1)
    for i in range(n_seq):
        for c in range(n_chunks):
            rows = slice(c * ROW_CHUNK, (c + 1) * ROW_CHUNK)
            srows = slice(i * seq + c * ROW_CHUNK, i * seq + (c + 1) * ROW_CHUNK)
            h = _mod_norm(x_ref[i, rows, :], nw, mod).astype(BF16)
            uz = _dot(h, win_ref[...])
            z_scr[srows, :] = uz[:, BRANCH:]
            u = uz[:, :BRANCH].astype(BF16)
            tr = jnp.zeros((ROW_CHUNK, HALF_W), F32)
            for g in range(N_GROUPS):
                t = _dot(u[:, g * GROUP_W:(g + 1) * GROUP_W], m1_ref[...])
                half = slice(g * HALF_W, (g + 1) * HALF_W)
                ta_scr[srows, half] = t[:, :HALF_W].astype(BF16)
                tb = t[:, HALF_W:]
                tb_scr[srows, half] = tb.astype(BF16)
                tr = jnp.where(lane == g, tb if g == 0 else pltpu.roll(tb, g, axis=1), tr)
            tr_scr[srows, :] = tr.astype(BF16)
    for i in range(n_seq):
        seq_rows = slice(i * seq, (i + 1) * seq)
        for c in range(n_chunks):
            rows = slice(c * ROW_CHUNK, (c + 1) * ROW_CHUNK)
            srows = slice(i * seq + c * ROW_CHUNK, i * seq + (c + 1) * ROW_CHUNK)
            cs = cs_ref[rows, :]
            p = _dot(cs, ta_scr[seq_rows, :])
            q = _dot(ss_ref[rows, :], tb_scr[seq_rows, :])
            r = _dot(cs, tr_scr[seq_rows, :])
            parts = []
            for g in range(N_GROUPS):
                half = slice(g * HALF_W, (g + 1) * HALF_W)
                pg, qg = p[:, half], q[:, half]
                rg = r if g == 0 else pltpu.roll(r, HALF_W - g, axis=1)
                parts.append(jnp.where(lane == 0, pg, pg - qg))
                parts.append(jnp.where(lane == 0, rg, pg + qg))
            y = jnp.concatenate(parts, axis=1)
            y = (y * jax.nn.silu(z_scr[srows, :])).astype(BF16)
            o_ref[i, rows, :] = x_ref[i, rows, :] + gate * _dot(y, wout_ref[...])
            if rows_done is not None:
                rows_done(i, rows)


def _dft_tables(seq):
    c = np.arange(GROUP_W)[:, None]
    k = np.arange(HALF_W)[None, :]
    cos_lo = np.cos(2.0 * np.pi * ((c * k) % GROUP_W) / GROUP_W)
    sin_lo = np.sin(2.0 * np.pi * ((c * k) % GROUP_W) / GROUP_W)
    sin_lo[:, 0] = np.cos(np.pi * c[:, 0])
    m1 = np.concatenate([cos_lo, sin_lo], axis=1) / np.sqrt(GROUP_W)
    n = np.arange(seq)
    ang = 2.0 * np.pi * ((n[:, None] * n[None, :]) % seq) / seq
    cs = np.cos(ang) / np.sqrt(seq)
    ss = np.sin(ang) / np.sqrt(seq)
    return m1.astype(np.float32), cs.astype(np.float32), ss.astype(np.float32)


def _const_spec(shape):
    return pl.BlockSpec(shape, lambda b: (0,) * len(shape))


def _fourier_layer(x_ctx, x_lat, nw, this_layer, next_layer):
    nb_ctx, seq_ctx, _ = x_ctx.shape
    nb_lat, seq_lat, _ = x_lat.shape
    n_seq = CTX_SEQS_PER_STEP
    assert nb_ctx % n_seq == 0
    n_prep = D_MODEL // HALF_W
    n_ctx = nb_ctx // n_seq
    assert D_MODEL % (n_ctx * LANES) == 0 and n_seq * seq_ctx <= seq_lat
    rows = D_MODEL // n_ctx
    assert rows == HALF_W
    c_ctx, c, w_mod0, b_mod0, w_in0, w_out0 = this_layer
    _, _, w_mod1, b_mod1, w_in1, w_out1 = next_layer
    n_lat = c.shape[0]
    n_qkvz = 2 * BRANCH + 2 * KV_W
    m1, csc, ssc = (jnp.asarray(t).astype(BF16) for t in _dft_tables(seq_ctx))
    _, csl, ssl = (jnp.asarray(t).astype(BF16) for t in _dft_tables(seq_lat))
    pm, pmh = (jnp.asarray(t).astype(BF16) for t in _perm_matrices())
    cc = c_ctx.reshape(1, D_MODEL)

    def prep_step(i):
        return jnp.minimum(i, n_prep - 1)

    def ctx_step(i):
        return jnp.clip(i - n_prep, 0, n_ctx - 1)

    def chunk_specs(step_fn, w_in_cols):
        return [pl.BlockSpec((1, HALF_W), lambda i: (0, step_fn(i))),
                pl.BlockSpec((n_lat, HALF_W), lambda i: (0, step_fn(i))),
                pl.BlockSpec((HALF_W, 3 * D_MODEL), lambda i: (step_fn(i), 0)),
                _const_spec((1, 3 * D_MODEL)),
                pl.BlockSpec((HALF_W, w_in_cols), lambda i: (step_fn(i), 0)),
                pl.BlockSpec((HALF_W, D_MODEL), lambda i: (step_fn(i), 0))]

    in_specs = [
        pl.BlockSpec((n_seq, seq_ctx, D_MODEL), lambda i: (ctx_step(i), 0, 0)),
        pl.BlockSpec(memory_space=pl.ANY),
        _const_spec((1, D_MODEL)),
        _const_spec((GROUP_W, GROUP_W)),
        _const_spec((seq_ctx, seq_ctx)),
        _const_spec((seq_ctx, seq_ctx)),
        pl.BlockSpec(memory_space=pl.ANY),
        pl.BlockSpec(memory_space=pl.ANY),
    ] + chunk_specs(prep_step, 2 * BRANCH) + [
        _const_spec((GROUP_W, GROUP_W)),
        _const_spec((2, HALF_W, HALF_W)),
    ] + chunk_specs(ctx_step, n_qkvz)
    out_specs = [
        pl.BlockSpec((n_seq, seq_ctx, D_MODEL), lambda i: (ctx_step(i), 0, 0)),
        pl.BlockSpec(memory_space=pl.ANY),
        MOD_SPEC,
        pl.BlockSpec((BRANCH + 2 * KV_W, rows), lambda i: (0, ctx_step(i))),
        pl.BlockSpec((rows, BRANCH), lambda i: (ctx_step(i), 0)),
        pl.BlockSpec((rows, D_MODEL), lambda i: (ctx_step(i), 0)),
    ]
    out_shape = [
        jax.ShapeDtypeStruct(x_ctx.shape, F32),
        jax.ShapeDtypeStruct(x_lat.shape, F32),
        jax.ShapeDtypeStruct((MOD_ROWS, 3 * D_MODEL), F32),
        jax.ShapeDtypeStruct((BRANCH + 2 * KV_W, D_MODEL), BF16),
        jax.ShapeDtypeStruct((D_MODEL, BRANCH), BF16),
        jax.ShapeDtypeStruct((BRANCH, D_MODEL), BF16),
    ]
    return pl.pallas_call(
        functools.partial(_fourier_layer_kernel, n_prep_steps=n_prep, n_ctx_steps=n_ctx),
        grid=(n_prep + n_ctx + nb_lat,),
        in_specs=in_specs,
        out_specs=out_specs,
        out_shape=out_shape,
        scratch_shapes=[
            pltpu.VMEM((MOD_ROWS, 3 * D_MODEL), F32),
            pltpu.VMEM((D_MODEL, 2 * BRANCH), BF16),
            pltpu.VMEM((BRANCH, D_MODEL), BF16),
            pltpu.VMEM((seq_lat, N_GROUPS * HALF_W), BF16),
            pltpu.VMEM((seq_lat, N_GROUPS * HALF_W), BF16),
            pltpu.VMEM((seq_lat, HALF_W), BF16),
            pltpu.VMEM((seq_lat, BRANCH), F32),
            pltpu.VMEM((MOD_ROWS, HALF_W), F32),
            pltpu.VMEM(x_lat.shape, F32),
            pltpu.VMEM((seq_lat, seq_lat), BF16),
            pltpu.VMEM((seq_lat, seq_lat), BF16),
            pltpu.SemaphoreType.DMA((nb_lat + 2,)),
            pltpu.VMEM(x_lat.shape, F32),
            pltpu.SemaphoreType.DMA((nb_lat, seq_lat // ROW_CHUNK)),
        ],
        compiler_params=pltpu.CompilerParams(
            dimension_semantics=("arbitrary",), vmem_limit_bytes=FOURIER_VMEM_LIMIT),
        name="fourier_layer",
    )(x_ctx, x_lat, nw, m1, csc, ssc, csl, ssl,
      cc, c, w_mod0, b_mod0.reshape(1, 3 * D_MODEL), w_in0, w_out0, pm, pmh,
      cc, c, w_mod1, b_mod1.reshape(1, 3 * D_MODEL), w_in1, w_out1)


def _head_weight_tile(w_ref, n_tokens):
    row = jnp.broadcast_to(w_ref[...], (HEAD_DIM, HEAD_DIM))
    ii = lax.broadcasted_iota(jnp.int32, (HEAD_DIM, HEAD_DIM), 0)
    jj = lax.broadcasted_iota(jnp.int32, (HEAD_DIM, HEAD_DIM), 1)
    col = jnp.sum(jnp.where(ii == jj, row, 0.0), axis=1, keepdims=True)
    return jnp.broadcast_to(col, (HEAD_DIM, n_tokens))


def _head_rms(t, w):
    return (t * lax.rsqrt(jnp.mean(t * t, axis=0, keepdims=True) + EPS)) * w


def _rope_t(t, cos, sin):
    half = HEAD_DIM // 2
    x1, x2 = t[:half], t[half:]
    return jnp.concatenate([x1 * cos - x2 * sin, x1 * sin + x2 * cos], axis=0)


def _head_scores(qn, g, keys, biases):
    zeros = jnp.zeros_like(qn)
    qz = jnp.concatenate([qn, zeros] if g % 2 == 0 else [zeros, qn], axis=0)
    blk = slice((g // 2) * LANES, (g // 2 + 1) * LANES)
    scores = []
    smax = None
    for k, bias in zip(keys, biases):
        s = _dot(k[:, blk], qz)
        if bias is not None:
            s = jnp.concatenate(
                [s[c * BLOCK:(c + 1) * BLOCK] if b is None else
                 s[c * BLOCK:(c + 1) * BLOCK] + jnp.concatenate([b] * (s.shape[1] // b.shape[1]), axis=1)
                 for c, b in enumerate(bias)], axis=0)
        cmax = jnp.max(s, axis=0, keepdims=True)
        smax = cmax if smax is None else jnp.maximum(smax, cmax)
        scores.append(s)
    return scores, smax


def _with_ones_rows(vt):
    return jnp.concatenate([vt, jnp.ones((ONES_ROWS, vt.shape[1]), vt.dtype)], axis=0)


def _head_probs(scored, sink2):
    scores, smax = scored
    m = jnp.maximum(smax, sink2)
    return jnp.concatenate([jnp.exp2(s - m).astype(BF16) for s in scores], axis=0), m


def _head_pv(probs, values_t, sink2):
    p, m = probs
    acc = _dot(values_t, p)
    den = acc[HEAD_DIM:HEAD_DIM + 1] + jnp.exp2(sink2 - m)
    return acc[:HEAD_DIM] * (1.0 / den)


def _attend_heads(n_units, stage, scores_fn, probs_fn, pv_fn, fillers=()):
    n_stages = n_units // stage
    pending = [scores_fn(u) for u in range(stage)]
    for g in range(n_stages):
        units = range(g * stage, (g + 1) * stage)
        if g < len(fillers):
            fillers[g]()
        nxt = [scores_fn(u) for u in range((g + 1) * stage, (g + 2) * stage)] if g + 1 < n_stages else None
        probs = [probs_fn(u, sc) for u, sc in zip(units, pending)]
        for u, pr in zip(units, probs):
            pv_fn(u, pr)
        pending = nxt


def _gate_out(x, o, z, gate, wout):
    y = (o * jax.nn.silu(z)).astype(BF16)
    return x + gate * _dot(y, wout)


def _attn_ctx_body(sink_ref, x_ref, mod, nw_ref, wqkvt_ref, wz_ref, wout_ref, qw_ref, kw_ref,
                   o_ref, kto_ref, vto_ref, qkvt_scr, z_scr, ot_scr):
    seq = x_ref.shape[1]
    n_seq = x_ref.shape[0]
    gate = mod[:, 2 * D_MODEL:]
    kw = _head_weight_tile(kw_ref, seq)
    qw = _head_weight_tile(qw_ref, seq)

    def project(i):
        h = _mod_norm(x_ref[i], nw_ref[...], mod).astype(BF16)
        qkvt_scr[i] = _dot_nt(wqkvt_ref[...], h)
        z_scr[i * seq:(i + 1) * seq, :] = _dot(h, wz_ref[...])

    def keys_values(i):
        knt = jnp.concatenate(
            [_head_rms(qkvt_scr[i, BRANCH + g * HEAD_DIM:BRANCH + (g + 1) * HEAD_DIM, :], kw)
             for g in range(N_KV)], axis=0)
        kto_ref[i] = knt
        vtf = qkvt_scr[i, BRANCH + KV_W:, :]
        vto_ref[i] = vtf
        vt = vtf.astype(BF16)
        return (knt.T.astype(BF16),
                [_with_ones_rows(vt[g * HEAD_DIM:(g + 1) * HEAD_DIM]) for g in range(N_KV)])

    def output(i):
        o_ref[i] = _gate_out(x_ref[i], ot_scr[i].T, z_scr[i * seq:(i + 1) * seq, :], gate, wout_ref[...])

    for i in range(n_seq):
        project(i)
    kv = [keys_values(i) for i in range(n_seq)]

    def scores_fn(u):
        i, hd = divmod(u, N_HEADS)
        t = qkvt_scr[i, hd * HEAD_DIM:(hd + 1) * HEAD_DIM, :]
        qn = (_head_rms(t, qw) * (HEAD_DIM ** -0.5 * LOG2E)).astype(BF16)
        return _head_scores(qn, hd // GQA, [kv[i][0]], [None])

    def probs_fn(u, sc):
        return _head_probs(sc, sink_ref[u % N_HEADS] * LOG2E)

    def pv_fn(u, pr):
        i, hd = divmod(u, N_HEADS)
        ot_scr[i, hd * HEAD_DIM:(hd + 1) * HEAD_DIM, :] = _head_pv(
            pr, kv[i][1][hd // GQA], sink_ref[hd] * LOG2E)

    fillers = [lambda: None] + [functools.partial(output, i) for i in range(n_seq - 1)]
    _attend_heads(n_seq * N_HEADS, N_HEADS, scores_fn, probs_fn, pv_fn, fillers)
    output(n_seq - 1)


def _attn_lat_body(sink_ref, x_ref, mod, nw_ref, wqkvt_ref, wz_ref, wout_ref, qw_ref, kw_ref,
                   cos_ref, sin_ref, ck_ref, cv_ref, o_ref,
                   q_scr, z_scr, ot_scr, k_scr, vt_scr, block_done=None):
    seq = x_ref.shape[1]
    gate = mod[:, 2 * D_MODEL:]
    nw = nw_ref[...]
    qw = _head_weight_tile(qw_ref, Q_BLOCK)
    kw = _head_weight_tile(kw_ref, Q_BLOCK)
    n_blocks = seq // Q_BLOCK
    kv_blocks = seq // BLOCK
    k_scr[0:BLOCK, :] = jnp.zeros((BLOCK, KV_W), BF16)
    k_scr[BLOCK + seq:2 * BLOCK + seq, :] = jnp.zeros((BLOCK, KV_W), BF16)
    vt_scr[0] = jnp.zeros((KV_W, BLOCK), BF16)
    vt_scr[kv_blocks + 1] = jnp.zeros((KV_W, BLOCK), BF16)

    def project(c):
        rows = slice(c * Q_BLOCK, (c + 1) * Q_BLOCK)
        h = _mod_norm(x_ref[0, rows, :], nw, mod).astype(BF16)
        z_scr[rows, :] = _dot(h, wz_ref[...])
        qkvt = _dot_nt(wqkvt_ref[...], h)
        cos = cos_ref[:, rows]
        sin = sin_ref[:, rows]
        for hd in range(N_HEADS):
            hr = slice(hd * HEAD_DIM, (hd + 1) * HEAD_DIM)
            t = _rope_t(_head_rms(qkvt[hr], qw), cos, sin)
            t = (t * (HEAD_DIM ** -0.5 * LOG2E)).astype(BF16)
            for j in range(HALVES):
                q_scr[c * HALVES + j, hd // 2, :, (hd % 2) * BLOCK:(hd % 2 + 1) * BLOCK] = (
                    t[:, j * BLOCK:(j + 1) * BLOCK])
        knt = jnp.concatenate(
            [_rope_t(_head_rms(qkvt[BRANCH + g * HEAD_DIM:BRANCH + (g + 1) * HEAD_DIM], kw), cos, sin)
             for g in range(N_KV)], axis=0)
        k_scr[BLOCK + c * Q_BLOCK:BLOCK + (c + 1) * Q_BLOCK, :] = knt.T.astype(BF16)
        vt = qkvt[BRANCH + KV_W:].astype(BF16)
        for j in range(HALVES):
            vt_scr[1 + c * HALVES + j] = vt[:, j * BLOCK:(j + 1) * BLOCK]

    ckb = ck_ref[0].T.astype(BF16)
    cvt = cv_ref[0].astype(BF16)

    win_len = 3 * BLOCK
    n_pairs = N_HEADS // 2
    kj = lax.broadcasted_iota(jnp.int32, (BLOCK, BLOCK), 0)
    qi = lax.broadcasted_iota(jnp.int32, (BLOCK, BLOCK), 1)
    first_head = lax.broadcasted_iota(jnp.int32, (1, 2 * BLOCK), 1) < BLOCK
    assert WINDOW >= BLOCK - 1

    def band_bias(c):
        rel = kj + (c - 1) * BLOCK - qi
        return jnp.where((rel >= -WINDOW) & (rel <= WINDOW), 0.0, NEG_INF)

    band_before, band_after = band_bias(0), band_bias(2)

    def half_operands(hb):
        r0 = pl.multiple_of(hb * BLOCK, BLOCK)
        valid = (jnp.where(hb > 0, band_before, NEG_INF), None,
                 jnp.where(hb < kv_blocks - 1, band_after, NEG_INF))
        kwin = k_scr[pl.ds(r0, win_len), :]
        vall = jnp.concatenate([vt_scr[hb + j] for j in range(win_len // BLOCK)] + [cvt], axis=1)
        valls = [_with_ones_rows(vall[g * HEAD_DIM:(g + 1) * HEAD_DIM]) for g in range(N_KV)]
        return valid, kwin, valls

    def pair_sink2(pair):
        return jnp.where(first_head, sink_ref[2 * pair], sink_ref[2 * pair + 1]) * LOG2E

    def attend(n, carry):
        halves = [half_operands(n * HALVES + j) for j in range(HALVES)]

        def scores_fn(u):
            j, pair = divmod(u, n_pairs)
            valid, kwin, _ = halves[j]
            return _head_scores(q_scr[n * HALVES + j, pair], pair // (GQA // 2), [kwin, ckb], [valid, None])

        def probs_fn(u, sc):
            return _head_probs(sc, pair_sink2(u % n_pairs))

        def pv_fn(u, pr):
            j, pair = divmod(u, n_pairs)
            o = _head_pv(pr, halves[j][2][pair // (GQA // 2)], pair_sink2(pair))
            for i in range(2):
                hd = 2 * pair + i
                ot_scr[n, hd * HEAD_DIM:(hd + 1) * HEAD_DIM, j * BLOCK:(j + 1) * BLOCK] = (
                    o[:, i * BLOCK:(i + 1) * BLOCK])

        _attend_heads(HALVES * n_pairs, GQA, scores_fn, probs_fn, pv_fn)
        return carry

    for c in range(n_blocks):
        project(c)
    lax.fori_loop(0, n_blocks, attend, 0)
    for c in range(n_blocks):
        rows = slice(c * Q_BLOCK, (c + 1) * Q_BLOCK)
        o_ref[0, rows, :] = _gate_out(x_ref[0, rows, :], ot_scr[c].T, z_scr[rows, :], gate, wout_ref[...])
        if block_done is not None:
            block_done(c)


def _rope_tables_t(seq):
    pos = np.arange(seq)
    n_freq = HEAD_DIM // 4
    inv = ROPE_THETA ** (-np.arange(n_freq, dtype=np.float64) / n_freq)
    ang = np.concatenate([(pos // GRID_W)[:, None] * inv, (pos % GRID_W)[:, None] * inv], axis=-1)
    return np.cos(ang).T.astype(np.float32), np.sin(ang).T.astype(np.float32)


def _attn_layer_kernel(sink_ref, xc_ref, xl_ref, mod_ref, nw_ref, wqkvt_ref, wz_ref, wout_ref, qw_ref, kw_ref,
                       cos_ref, sin_ref, ck_ref, cv_ref, oc_ref, kto_ref, vto_ref, ol_hbm,
                       qkvt_scr, q_scr, z_scr, ot_scr, k_scr, vt_scr, ol_scr, out_sem, *, n_ctx_steps):
    step = pl.program_id(0)
    n_lat = ol_scr.shape[0]
    lat = step - n_ctx_steps

    def result_copy(r, c):
        rows = pl.ds(c * Q_BLOCK, Q_BLOCK)
        return pltpu.make_async_copy(ol_scr.at[r, rows], ol_hbm.at[r, rows], out_sem.at[r, c])

    @pl.when(step < n_ctx_steps)
    def _():
        _attn_ctx_body(sink_ref, xc_ref, mod_ref[0:1, :], nw_ref, wqkvt_ref, wz_ref, wout_ref, qw_ref, kw_ref,
                       oc_ref, kto_ref, vto_ref, qkvt_scr, z_scr, ot_scr)

    @pl.when(step >= n_ctx_steps)
    def _():
        mod = mod_ref[pl.ds(1 + lat, 1), :]
        _attn_lat_body(sink_ref, xl_ref, mod, nw_ref, wqkvt_ref, wz_ref, wout_ref, qw_ref, kw_ref,
                       cos_ref, sin_ref, ck_ref, cv_ref, ol_scr.at[pl.ds(lat, 1)],
                       q_scr, z_scr, ot_scr, k_scr, vt_scr,
                       block_done=lambda c: result_copy(lat, c).start())

        @pl.when(lat == n_lat - 1)
        def _():
            for r in range(n_lat):
                for c in range(out_sem.shape[1]):
                    result_copy(r, c).wait()


def _attn_layer(x_ctx, x_lat, mod, nw, wqkvt, wz, wout, qw, kw, sink, ckt, cvt):
    nb_ctx, seq_ctx, _ = x_ctx.shape
    nb_lat, seq_lat, _ = x_lat.shape
    past = ckt.shape[2]
    n_seq = CTX_SEQS_PER_STEP
    assert nb_ctx % n_seq == 0 and seq_ctx == Q_BLOCK and n_seq <= seq_lat // Q_BLOCK
    n_ctx = nb_ctx // n_seq
    cos, sin = (jnp.asarray(t) for t in _rope_tables_t(seq_lat))

    def ctx_step(i):
        return jnp.minimum(i, n_ctx - 1)

    def lat_step(i):
        return jnp.maximum(i - n_ctx, 0)

    return pl.pallas_call(
        functools.partial(_attn_layer_kernel, n_ctx_steps=n_ctx),
        grid=(n_ctx + nb_lat,),
        in_specs=[
            pl.BlockSpec(memory_space=pltpu.SMEM),
            pl.BlockSpec((n_seq, seq_ctx, D_MODEL), lambda i: (ctx_step(i), 0, 0)),
            pl.BlockSpec((1, seq_lat, D_MODEL), lambda i: (lat_step(i), 0, 0)),
            MOD_SPEC,
            _const_spec((1, D_MODEL)),
            _const_spec((BRANCH + 2 * KV_W, D_MODEL)),
            _const_spec((D_MODEL, BRANCH)),
            _const_spec((BRANCH, D_MODEL)),
            _const_spec((1, HEAD_DIM)),
            _const_spec((1, HEAD_DIM)),
            _const_spec((HEAD_DIM // 2, seq_lat)),
            _const_spec((HEAD_DIM // 2, seq_lat)),
            pl.BlockSpec((1, KV_W, past), lambda i: (lat_step(i), 0, 0)),
            pl.BlockSpec((1, KV_W, past), lambda i: (lat_step(i), 0, 0)),
        ],
        out_specs=[
            pl.BlockSpec((n_seq, seq_ctx, D_MODEL), lambda i: (ctx_step(i), 0, 0)),
            pl.BlockSpec((n_seq, KV_W, seq_ctx), lambda i: (ctx_step(i), 0, 0)),
            pl.BlockSpec((n_seq, KV_W, seq_ctx), lambda i: (ctx_step(i), 0, 0)),
            pl.BlockSpec(memory_space=pl.ANY),
        ],
        out_shape=[
            jax.ShapeDtypeStruct(x_ctx.shape, F32),
            jax.ShapeDtypeStruct((nb_ctx, KV_W, seq_ctx), F32),
            jax.ShapeDtypeStruct((nb_ctx, KV_W, seq_ctx), F32),
            jax.ShapeDtypeStruct(x_lat.shape, F32),
        ],
        scratch_shapes=[
            pltpu.VMEM((n_seq, BRANCH + 2 * KV_W, seq_ctx), F32),
            pltpu.VMEM((seq_lat // BLOCK, N_HEADS // 2, HEAD_DIM, 2 * BLOCK), BF16),
            pltpu.VMEM((seq_lat, BRANCH), F32),
            pltpu.VMEM((seq_lat // Q_BLOCK, BRANCH, Q_BLOCK), F32),
            pltpu.VMEM((seq_lat + 2 * BLOCK, KV_W), BF16),
            pltpu.VMEM((seq_lat // BLOCK + 2, KV_W, BLOCK), BF16),
            pltpu.VMEM(x_lat.shape, F32),
            pltpu.SemaphoreType.DMA((nb_lat, seq_lat // Q_BLOCK)),
        ],
        compiler_params=pltpu.CompilerParams(
            dimension_semantics=("arbitrary",), vmem_limit_bytes=VMEM_LIMIT),
        name="attn_layer",
    )(sink, x_ctx, x_lat, mod, nw, wqkvt, wz, wout, qw, kw, cos, sin, ckt, cvt)


def kernel(x_prompt, x_sample, cache_k_l1, cache_v_l1, c, c_ctx, norm_w_l0, w_mod_l0, b_mod_l0,
           w_in_l0, w_out_l0, norm_w_l1, w_mod_l1, b_mod_l1, w_in_l1, q_norm_w_l1, k_norm_w_l1,
           sink_l1, w_out_l1):
    nb_ctx, seq_ctx, _ = x_prompt.shape
    nb_lat = x_sample.shape[0]
    past = cache_k_l1.shape[1]
    assert 1 + nb_lat <= MOD_ROWS
    nw0 = norm_w_l0.reshape(1, D_MODEL)
    nw1 = norm_w_l1.reshape(1, D_MODEL)
    qw = q_norm_w_l1.reshape(1, HEAD_DIM)
    kw = k_norm_w_l1.reshape(1, HEAD_DIM)

    xp, xs, mod1, wqkvt1, wz1, wout1 = _fourier_layer(
        x_prompt, x_sample, nw0,
        this_layer=(c_ctx, c, w_mod_l0, b_mod_l0, w_in_l0, w_out_l0),
        next_layer=(c_ctx, c, w_mod_l1, b_mod_l1, w_in_l1, w_out_l1))

    def to_feature_major(t):
        return jnp.transpose(t, (0, 2, 3, 1)).reshape(t.shape[0], KV_W, t.shape[1])

    def from_feature_major(t):
        return jnp.transpose(t.reshape(t.shape[0], N_KV, HEAD_DIM, t.shape[2]), (0, 3, 1, 2))

    xp, new_kt, new_vt, xs = _attn_layer(xp, xs, mod1, nw1, wqkvt1, wz1, wout1, qw, kw, sink_l1,
                                         to_feature_major(cache_k_l1), to_feature_major(cache_v_l1))
    return (xp, xs, from_feature_major(new_kt), from_feature_major(new_vt))
```

```python
import functools

import numpy as np
import jax
import jax.numpy as jnp
from jax import lax
from jax.experimental import pallas as pl
from jax.experimental.pallas import tpu as pltpu

D_MODEL = 1024
BRANCH = 1024
N_GROUPS = 4
GROUP_W = BRANCH // N_GROUPS
HALF_W = GROUP_W // 2
HEAD_DIM = 64
N_HEADS = 16
N_KV = 4
GQA = N_HEADS // N_KV
KV_W = N_KV * HEAD_DIM
GRID_W = 64
WINDOW = 128
BLOCK = 128
ROPE_THETA = 10000.0
EPS = 1e-6
NEG_INF = -1e30
LANES = 128
ROW_CHUNK = 256
Q_BLOCK = 256
HALVES = Q_BLOCK // BLOCK
CTX_SEQS_PER_STEP = 2
VMEM_LIMIT = 56 * 1024 * 1024
FOURIER_VMEM_LIMIT = 60 * 1024 * 1024
MOD_ROWS = 8
ONES_ROWS = 16
LOG2E = float(np.log2(np.e))

F32 = jnp.float32
BF16 = jnp.bfloat16


def _dot(a, b):
    return jnp.dot(a, b, preferred_element_type=F32)


def _dot_nt(a, b):
    return lax.dot_general(a, b, (((1,), (1,)), ((), ())), preferred_element_type=F32)


MOD_SPEC = pl.BlockSpec((MOD_ROWS, 3 * D_MODEL), lambda b: (0, 0))


def _mod_norm(x, nw, mod):
    shift = mod[:, :D_MODEL]
    scale = mod[:, D_MODEL:2 * D_MODEL]
    y = x * lax.rsqrt(jnp.mean(x * x, axis=-1, keepdims=True) + EPS)
    return (y * nw) * (1.0 + scale) + shift


def _mod_accumulate(cctx_ref, c_ref, w_ref, b_ref, o_ref, cond_scr, first_step):
    n_lat = c_ref.shape[0]
    cond_scr[...] = jnp.zeros_like(cond_scr)
    cond_scr[0:1, :] = cctx_ref[...]
    cond_scr[1:1 + n_lat, :] = c_ref[...]
    s = jax.nn.silu(cond_scr[...]).astype(BF16)

    @pl.when(pl.program_id(0) == first_step)
    def _():
        o_ref[...] = jnp.broadcast_to(b_ref[...], o_ref.shape)

    o_ref[...] += _dot(s, w_ref[...].astype(BF16))


def _mirror_perm():
    j = np.arange(GROUP_W)
    return np.where(j <= HALF_W, j, GROUP_W + HALF_W - j)


def _perm_matrices():
    perm = _mirror_perm()
    pm = (np.arange(GROUP_W)[:, None] == perm[None, :]).astype(np.float32)
    assert (pm[:HALF_W, HALF_W:] == 0).all() and (pm[HALF_W:, :HALF_W] == 0).all()
    return pm, np.stack([pm[:HALF_W, :HALF_W], pm[HALF_W:, HALF_W:]])


def _fourier_layer_kernel(xc_ref, xl_hbm, nw_ref, m1_ref, csc_ref, ssc_ref, csl_hbm, ssl_hbm,
                          cctx0_ref, c0_ref, wmod0_ref, bmod0_ref, win0_ref, wout0_ref, pm_ref, pmh_ref,
                          cctx_ref, c_ref, wmod1_ref, bmod1_ref, win1_ref, wout1_ref,
                          oc_hbm, ol_hbm, mod1_ref, wqkvt1_ref, wz1_ref, wout1b_ref,
                          mod0_scr, win_scr, wout_scr, ta_scr, tb_scr, tr_scr, z_scr, cond_scr,
                          xl_scr, csl_scr, ssl_scr, copy_sem, ol_scr, out_sem, oc_scr, oc_sem,
                          *, n_prep_steps, n_ctx_steps):
    step = pl.program_id(0)
    n_lat = xl_scr.shape[0]
    lat = step - (n_prep_steps + n_ctx_steps)

    def request_copy(r):
        return pltpu.make_async_copy(xl_hbm.at[r], xl_scr.at[r], copy_sem.at[r])

    table_copies = [pltpu.make_async_copy(csl_hbm, csl_scr, copy_sem.at[n_lat]),
                    pltpu.make_async_copy(ssl_hbm, ssl_scr, copy_sem.at[n_lat + 1])]

    def result_copy(r, c):
        rows = pl.ds(c * ROW_CHUNK, ROW_CHUNK)
        return pltpu.make_async_copy(ol_scr.at[r, rows], ol_hbm.at[r, rows], out_sem.at[r, c])

    n_seq, seq_ctx, _ = xc_ref.shape
    ctx = step - n_prep_steps

    def ctx_result_copy(k, i):
        return pltpu.make_async_copy(oc_scr.at[k % 2, i], oc_hbm.at[k * n_seq + i], oc_sem.at[k % 2, i])

    @pl.when(step == n_prep_steps)
    def _():
        for copy in table_copies + [request_copy(r) for r in range(n_lat)]:
            copy.start()

    @pl.when(step < n_prep_steps)
    def _():
        _mod_accumulate(cctx0_ref, c0_ref, wmod0_ref, bmod0_ref, mod0_scr, cond_scr, 0)
        rows = pl.ds(pl.multiple_of(step * HALF_W, HALF_W), HALF_W)
        w = win0_ref[...].astype(BF16)
        win_scr[rows, :BRANCH] = w[:, :BRANCH]
        pm = pm_ref[...]
        for g in range(N_GROUPS):
            cols = slice(BRANCH + g * GROUP_W, BRANCH + (g + 1) * GROUP_W)
            win_scr[rows, cols] = _dot(w[:, cols], pm).astype(BF16)
        wout_scr[rows, :] = _dot(pmh_ref[step % 2], wout0_ref[...].astype(BF16)).astype(BF16)

    @pl.when(jnp.logical_and(step >= n_prep_steps, step < n_prep_steps + n_ctx_steps))
    def _():
        @pl.when(ctx >= 2)
        def _():
            for i in range(n_seq):
                ctx_result_copy(ctx - 2, i).wait()

        _mod_accumulate(cctx_ref, c_ref, wmod1_ref, bmod1_ref, mod1_ref, cond_scr, n_prep_steps)
        w = win1_ref[...]
        wqkvt1_ref[...] = w[:, :BRANCH + 2 * KV_W].T.astype(BF16)
        wz1_ref[...] = w[:, BRANCH + 2 * KV_W:].astype(BF16)
        wout1b_ref[...] = wout1_ref[...].astype(BF16)

        def ctx_rows_done(i, rows):
            if rows.stop == seq_ctx:
                ctx_result_copy(ctx, i).start()

        _fourier_body(xc_ref, oc_scr.at[ctx % 2], mod0_scr[0:1, :], nw_ref, win_scr, wout_scr, m1_ref,
                      csc_ref, ssc_ref, ta_scr, tb_scr, tr_scr, z_scr, rows_done=ctx_rows_done)

    @pl.when(step >= n_prep_steps + n_ctx_steps)
    def _():
        @pl.when(lat == 0)
        def _():
            for copy in table_copies:
                copy.wait()
            for k in (n_ctx_steps - 2, n_ctx_steps - 1):
                for i in range(n_seq):
                    ctx_result_copy(k, i).wait()

        request_copy(lat).wait()
        mod = mod0_scr[pl.ds(1 + lat, 1), :]
        _fourier_body(xl_scr.at[pl.ds(lat, 1)], ol_scr.at[pl.ds(lat, 1)], mod, nw_ref, win_scr, wout_scr,
                      m1_ref, csl_scr, ssl_scr, ta_scr, tb_scr, tr_scr, z_scr,
                      rows_done=lambda _, rows: result_copy(lat, rows.start // ROW_CHUNK).start())

        @pl.when(lat == n_lat - 1)
        def _():
            for r in range(n_lat):
                for c in range(out_sem.shape[1]):
                    result_copy(r, c).wait()


def _fourier_body(x_ref, o_ref, mod, nw_ref, win_ref, wout_ref, m1_ref, cs_ref, ss_ref,
                  ta_scr, tb_scr, tr_scr, z_scr, rows_done=None):
    gate = mod[:, 2 * D_MODEL:]
    nw = nw_ref[...]
    n_seq, seq, _ = x_ref.shape
    n_chunks = seq // ROW_CHUNK
    lane = lax.broadcasted_iota(jnp.int32, (ROW_CHUNK, HALF_W), 1)
    for i in range(n_seq):
        for c in range(n_chunks):
            rows = slice(c * ROW_CHUNK, (c + 1) * ROW_CHUNK)
            srows = slice(i * seq + c * ROW_CHUNK, i * seq + (c + 1) * ROW_CHUNK)
            h = _mod_norm(x_ref[i, rows, :], nw, mod).astype(BF16)
            uz = _dot(h, win_ref[...])
            z_scr[srows, :] = uz[:, BRANCH:]
            u = uz[:, :BRANCH].astype(BF16)
            tr = jnp.zeros((ROW_CHUNK, HALF_W), F32)
            for g in range(N_GROUPS):
                t = _dot(u[:, g * GROUP_W:(g + 1) * GROUP_W], m1_ref[...])
                half = slice(g * HALF_W, (g + 1) * HALF_W)
                ta_scr[srows, half] = t[:, :HALF_W].astype(BF16)
                tb = t[:, HALF_W:]
                tb_scr[srows, half] = tb.astype(BF16)
                tr = jnp.where(lane == g, tb if g == 0 else pltpu.roll(tb, g, axis=1), tr)
            tr_scr[srows, :] = tr.astype(BF16)
    for i in range(n_seq):
        seq_rows = slice(i * seq, (i + 1) * seq)
        for c in range(n_chunks):
            rows = slice(c * ROW_CHUNK, (c + 1) * ROW_CHUNK)
            srows = slice(i * seq + c * ROW_CHUNK, i * seq + (c + 1) * ROW_CHUNK)
            cs = cs_ref[rows, :]
            p = _dot(cs, ta_scr[seq_rows, :])
            q = _dot(ss_ref[rows, :], tb_scr[seq_rows, :])
            r = _dot(cs, tr_scr[seq_rows, :])
            parts = []
            for g in range(N_GROUPS):
                half = slice(g * HALF_W, (g + 1) * HALF_W)
                pg, qg = p[:, half], q[:, half]
                rg = r if g == 0 else pltpu.roll(r, HALF_W - g, axis=1)
                parts.append(jnp.where(lane == 0, pg, pg - qg))
                parts.append(jnp.where(lane == 0, rg, pg + qg))
            y = jnp.concatenate(parts, axis=1)
            y = (y * jax.nn.silu(z_scr[srows, :])).astype(BF16)
            o_ref[i, rows, :] = x_ref[i, rows, :] + gate * _dot(y, wout_ref[...])
            if rows_done is not None:
                rows_done(i, rows)


def _dft_tables(seq):
    c = np.arange(GROUP_W)[:, None]
    k = np.arange(HALF_W)[None, :]
    cos_lo = np.cos(2.0 * np.pi * ((c * k) % GROUP_W) / GROUP_W)
    sin_lo = np.sin(2.0 * np.pi * ((c * k) % GROUP_W) / GROUP_W)
    sin_lo[:, 0] = np.cos(np.pi * c[:, 0])
    m1 = np.concatenate([cos_lo, sin_lo], axis=1) / np.sqrt(GROUP_W)
    n = np.arange(seq)
    ang = 2.0 * np.pi * ((n[:, None] * n[None, :]) % seq) / seq
    cs = np.cos(ang) / np.sqrt(seq)
    ss = np.sin(ang) / np.sqrt(seq)
    return m1.astype(np.float32), cs.astype(np.float32), ss.astype(np.float32)


def _const_spec(shape):
    return pl.BlockSpec(shape, lambda b: (0,) * len(shape))


def _fourier_layer(x_ctx, x_lat, nw, this_layer, next_layer):
    nb_ctx, seq_ctx, _ = x_ctx.shape
    nb_lat, seq_lat, _ = x_lat.shape
    n_seq = CTX_SEQS_PER_STEP
    assert nb_ctx % n_seq == 0
    n_prep = D_MODEL // HALF_W
    n_ctx = nb_ctx // n_seq
    assert D_MODEL % (n_ctx * LANES) == 0 and n_seq * seq_ctx <= seq_lat and n_ctx >= 2
    rows = D_MODEL // n_ctx
    assert rows == HALF_W
    c_ctx, c, w_mod0, b_mod0, w_in0, w_out0 = this_layer
    _, _, w_mod1, b_mod1, w_in1, w_out1 = next_layer
    n_lat = c.shape[0]
    n_qkvz = 2 * BRANCH + 2 * KV_W
    m1, csc, ssc = (jnp.asarray(t).astype(BF16) for t in _dft_tables(seq_ctx))
    _, csl, ssl = (jnp.asarray(t).astype(BF16) for t in _dft_tables(seq_lat))
    pm, pmh = (jnp.asarray(t).astype(BF16) for t in _perm_matrices())
    cc = c_ctx.reshape(1, D_MODEL)

    def prep_step(i):
        return jnp.minimum(i, n_prep - 1)

    def ctx_step(i):
        return jnp.clip(i - n_prep, 0, n_ctx - 1)

    def chunk_specs(step_fn, w_in_cols):
        return [pl.BlockSpec((1, HALF_W), lambda i: (0, step_fn(i))),
                pl.BlockSpec((n_lat, HALF_W), lambda i: (0, step_fn(i))),
                pl.BlockSpec((HALF_W, 3 * D_MODEL), lambda i: (step_fn(i), 0)),
                _const_spec((1, 3 * D_MODEL)),
                pl.BlockSpec((HALF_W, w_in_cols), lambda i: (step_fn(i), 0)),
                pl.BlockSpec((HALF_W, D_MODEL), lambda i: (step_fn(i), 0))]

    in_specs = [
        pl.BlockSpec((n_seq, seq_ctx, D_MODEL), lambda i: (ctx_step(i), 0, 0)),
        pl.BlockSpec(memory_space=pl.ANY),
        _const_spec((1, D_MODEL)),
        _const_spec((GROUP_W, GROUP_W)),
        _const_spec((seq_ctx, seq_ctx)),
        _const_spec((seq_ctx, seq_ctx)),
        pl.BlockSpec(memory_space=pl.ANY),
        pl.BlockSpec(memory_space=pl.ANY),
    ] + chunk_specs(prep_step, 2 * BRANCH) + [
        _const_spec((GROUP_W, GROUP_W)),
        _const_spec((2, HALF_W, HALF_W)),
    ] + chunk_specs(ctx_step, n_qkvz)
    out_specs = [
        pl.BlockSpec(memory_space=pl.ANY),
        pl.BlockSpec(memory_space=pl.ANY),
        MOD_SPEC,
        pl.BlockSpec((BRANCH + 2 * KV_W, rows), lambda i: (0, ctx_step(i))),
        pl.BlockSpec((rows, BRANCH), lambda i: (ctx_step(i), 0)),
        pl.BlockSpec((rows, D_MODEL), lambda i: (ctx_step(i), 0)),
    ]
    out_shape = [
        jax.ShapeDtypeStruct(x_ctx.shape, F32),
        jax.ShapeDtypeStruct(x_lat.shape, F32),
        jax.ShapeDtypeStruct((MOD_ROWS, 3 * D_MODEL), F32),
        jax.ShapeDtypeStruct((BRANCH + 2 * KV_W, D_MODEL), BF16),
        jax.ShapeDtypeStruct((D_MODEL, BRANCH), BF16),
        jax.ShapeDtypeStruct((BRANCH, D_MODEL), BF16),
    ]
    return pl.pallas_call(
        functools.partial(_fourier_layer_kernel, n_prep_steps=n_prep, n_ctx_steps=n_ctx),
        grid=(n_prep + n_ctx + nb_lat,),
        in_specs=in_specs,
        out_specs=out_specs,
        out_shape=out_shape,
        scratch_shapes=[
            pltpu.VMEM((MOD_ROWS, 3 * D_MODEL), F32),
            pltpu.VMEM((D_MODEL, 2 * BRANCH), BF16),
            pltpu.VMEM((BRANCH, D_MODEL), BF16),
            pltpu.VMEM((seq_lat, N_GROUPS * HALF_W), BF16),
            pltpu.VMEM((seq_lat, N_GROUPS * HALF_W), BF16),
            pltpu.VMEM((seq_lat, HALF_W), BF16),
            pltpu.VMEM((seq_lat, BRANCH), F32),
            pltpu.VMEM((MOD_ROWS, HALF_W), F32),
            pltpu.VMEM(x_lat.shape, F32),
            pltpu.VMEM((seq_lat, seq_lat), BF16),
            pltpu.VMEM((seq_lat, seq_lat), BF16),
            pltpu.SemaphoreType.DMA((nb_lat + 2,)),
            pltpu.VMEM(x_lat.shape, F32),
            pltpu.SemaphoreType.DMA((nb_lat, seq_lat // ROW_CHUNK)),
            pltpu.VMEM((2, n_seq, seq_ctx, D_MODEL), F32),
            pltpu.SemaphoreType.DMA((2, n_seq)),
        ],
        compiler_params=pltpu.CompilerParams(
            dimension_semantics=("arbitrary",), vmem_limit_bytes=FOURIER_VMEM_LIMIT),
        name="fourier_layer",
    )(x_ctx, x_lat, nw, m1, csc, ssc, csl, ssl,
      cc, c, w_mod0, b_mod0.reshape(1, 3 * D_MODEL), w_in0, w_out0, pm, pmh,
      cc, c, w_mod1, b_mod1.reshape(1, 3 * D_MODEL), w_in1, w_out1)


def _head_weight_tile(w_ref, n_tokens):
    row = jnp.broadcast_to(w_ref[...], (HEAD_DIM, HEAD_DIM))
    ii = lax.broadcasted_iota(jnp.int32, (HEAD_DIM, HEAD_DIM), 0)
    jj = lax.broadcasted_iota(jnp.int32, (HEAD_DIM, HEAD_DIM), 1)
    col = jnp.sum(jnp.where(ii == jj, row, 0.0), axis=1, keepdims=True)
    return jnp.broadcast_to(col, (HEAD_DIM, n_tokens))


def _head_rms(t, w):
    return (t * lax.rsqrt(jnp.mean(t * t, axis=0, keepdims=True) + EPS)) * w


def _rope_t(t, cos, sin):
    half = HEAD_DIM // 2
    x1, x2 = t[:half], t[half:]
    return jnp.concatenate([x1 * cos - x2 * sin, x1 * sin + x2 * cos], axis=0)


def _head_scores(qn, g, keys, biases):
    zeros = jnp.zeros_like(qn)
    qz = jnp.concatenate([qn, zeros] if g % 2 == 0 else [zeros, qn], axis=0)
    blk = slice((g // 2) * LANES, (g // 2 + 1) * LANES)
    scores = []
    smax = None
    for k, bias in zip(keys, biases):
        s = _dot(k[:, blk], qz)
        if bias is not None:
            s = jnp.concatenate(
                [s[c * BLOCK:(c + 1) * BLOCK] if b is None else
                 s[c * BLOCK:(c + 1) * BLOCK] + jnp.concatenate([b] * (s.shape[1] // b.shape[1]), axis=1)
                 for c, b in enumerate(bias)], axis=0)
        cmax = jnp.max(s, axis=0, keepdims=True)
        smax = cmax if smax is None else jnp.maximum(smax, cmax)
        scores.append(s)
    return scores, smax


def _with_ones_rows(vt):
    return jnp.concatenate([vt, jnp.ones((ONES_ROWS, vt.shape[1]), vt.dtype)], axis=0)


def _head_probs(scored, sink2):
    scores, smax = scored
    m = jnp.maximum(smax, sink2)
    return jnp.concatenate([jnp.exp2(s - m).astype(BF16) for s in scores], axis=0), m


def _head_pv(probs, values_t, sink2):
    p, m = probs
    acc = _dot(values_t, p)
    den = acc[HEAD_DIM:HEAD_DIM + 1] + jnp.exp2(sink2 - m)
    return acc[:HEAD_DIM] * (1.0 / den)


def _attend_heads(n_units, stage, scores_fn, probs_fn, pv_fn, fillers=()):
    n_stages = n_units // stage
    pending = [scores_fn(u) for u in range(stage)]
    for g in range(n_stages):
        units = range(g * stage, (g + 1) * stage)
        if g < len(fillers):
            fillers[g]()
        nxt = [scores_fn(u) for u in range((g + 1) * stage, (g + 2) * stage)] if g + 1 < n_stages else None
        probs = [probs_fn(u, sc) for u, sc in zip(units, pending)]
        for u, pr in zip(units, probs):
            pv_fn(u, pr)
        pending = nxt


def _gate_out(x, o, z, gate, wout):
    y = (o * jax.nn.silu(z)).astype(BF16)
    return x + gate * _dot(y, wout)


def _attn_ctx_body(sink_ref, x_ref, mod, nw_ref, wqkvt_ref, wz_ref, wout_ref, qw_ref, kw_ref,
                   o_ref, kto_ref, vto_ref, qkvt_scr, z_scr, ot_scr, seq_done=None):
    seq = x_ref.shape[1]
    n_seq = x_ref.shape[0]
    gate = mod[:, 2 * D_MODEL:]
    kw = _head_weight_tile(kw_ref, seq)
    qw = _head_weight_tile(qw_ref, seq)

    def project(i):
        h = _mod_norm(x_ref[i], nw_ref[...], mod).astype(BF16)
        qkvt_scr[i] = _dot_nt(wqkvt_ref[...], h)
        z_scr[i * seq:(i + 1) * seq, :] = _dot(h, wz_ref[...])

    def keys_values(i):
        knt = jnp.concatenate(
            [_head_rms(qkvt_scr[i, BRANCH + g * HEAD_DIM:BRANCH + (g + 1) * HEAD_DIM, :], kw)
             for g in range(N_KV)], axis=0)
        kto_ref[i] = knt
        vtf = qkvt_scr[i, BRANCH + KV_W:, :]
        vto_ref[i] = vtf
        vt = vtf.astype(BF16)
        return (knt.T.astype(BF16),
                [_with_ones_rows(vt[g * HEAD_DIM:(g + 1) * HEAD_DIM]) for g in range(N_KV)])

    def output(i):
        o_ref[i] = _gate_out(x_ref[i], ot_scr[i].T, z_scr[i * seq:(i + 1) * seq, :], gate, wout_ref[...])
        if seq_done is not None:
            seq_done(i)

    for i in range(n_seq):
        project(i)
    kv = [keys_values(i) for i in range(n_seq)]

    def scores_fn(u):
        i, hd = divmod(u, N_HEADS)
        t = qkvt_scr[i, hd * HEAD_DIM:(hd + 1) * HEAD_DIM, :]
        qn = (_head_rms(t, qw) * (HEAD_DIM ** -0.5 * LOG2E)).astype(BF16)
        return _head_scores(qn, hd // GQA, [kv[i][0]], [None])

    def probs_fn(u, sc):
        return _head_probs(sc, sink_ref[u % N_HEADS] * LOG2E)

    def pv_fn(u, pr):
        i, hd = divmod(u, N_HEADS)
        ot_scr[i, hd * HEAD_DIM:(hd + 1) * HEAD_DIM, :] = _head_pv(
            pr, kv[i][1][hd // GQA], sink_ref[hd] * LOG2E)

    fillers = [lambda: None] + [functools.partial(output, i) for i in range(n_seq - 1)]
    _attend_heads(n_seq * N_HEADS, N_HEADS, scores_fn, probs_fn, pv_fn, fillers)
    output(n_seq - 1)


def _attn_lat_body(sink_ref, x_ref, mod, nw_ref, wqkvt_ref, wz_ref, wout_ref, qw_ref, kw_ref,
                   cos_ref, sin_ref, ck_ref, cv_ref, o_ref,
                   q_scr, z_scr, ot_scr, k_scr, vt_scr, block_done=None):
    seq = x_ref.shape[1]
    gate = mod[:, 2 * D_MODEL:]
    nw = nw_ref[...]
    qw = _head_weight_tile(qw_ref, Q_BLOCK)
    kw = _head_weight_tile(kw_ref, Q_BLOCK)
    n_blocks = seq // Q_BLOCK
    kv_blocks = seq // BLOCK
    k_scr[0:BLOCK, :] = jnp.zeros((BLOCK, KV_W), BF16)
    k_scr[BLOCK + seq:2 * BLOCK + seq, :] = jnp.zeros((BLOCK, KV_W), BF16)
    vt_scr[0] = jnp.zeros((KV_W, BLOCK), BF16)
    vt_scr[kv_blocks + 1] = jnp.zeros((KV_W, BLOCK), BF16)

    def project(c):
        rows = slice(c * Q_BLOCK, (c + 1) * Q_BLOCK)
        h = _mod_norm(x_ref[0, rows, :], nw, mod).astype(BF16)
        z_scr[rows, :] = _dot(h, wz_ref[...])
        qkvt = _dot_nt(wqkvt_ref[...], h)
        cos = cos_ref[:, rows]
        sin = sin_ref[:, rows]
        for hd in range(N_HEADS):
            hr = slice(hd * HEAD_DIM, (hd + 1) * HEAD_DIM)
            t = _rope_t(_head_rms(qkvt[hr], qw), cos, sin)
            t = (t * (HEAD_DIM ** -0.5 * LOG2E)).astype(BF16)
            for j in range(HALVES):
                q_scr[c * HALVES + j, hd // 2, :, (hd % 2) * BLOCK:(hd % 2 + 1) * BLOCK] = (
                    t[:, j * BLOCK:(j + 1) * BLOCK])
        knt = jnp.concatenate(
            [_rope_t(_head_rms(qkvt[BRANCH + g * HEAD_DIM:BRANCH + (g + 1) * HEAD_DIM], kw), cos, sin)
             for g in range(N_KV)], axis=0)
        k_scr[BLOCK + c * Q_BLOCK:BLOCK + (c + 1) * Q_BLOCK, :] = knt.T.astype(BF16)
        vt = qkvt[BRANCH + KV_W:].astype(BF16)
        for j in range(HALVES):
            vt_scr[1 + c * HALVES + j] = vt[:, j * BLOCK:(j + 1) * BLOCK]

    ckb = ck_ref[0].T.astype(BF16)
    cvt = cv_ref[0].astype(BF16)

    win_len = 3 * BLOCK
    n_pairs = N_HEADS // 2
    kj = lax.broadcasted_iota(jnp.int32, (BLOCK, BLOCK), 0)
    qi = lax.broadcasted_iota(jnp.int32, (BLOCK, BLOCK), 1)
    first_head = lax.broadcasted_iota(jnp.int32, (1, 2 * BLOCK), 1) < BLOCK
    assert WINDOW >= BLOCK - 1

    def band_bias(c):
        rel = kj + (c - 1) * BLOCK - qi
        return jnp.where((rel >= -WINDOW) & (rel <= WINDOW), 0.0, NEG_INF)

    band_before, band_after = band_bias(0), band_bias(2)

    def half_operands(hb):
        r0 = pl.multiple_of(hb * BLOCK, BLOCK)
        valid = (jnp.where(hb > 0, band_before, NEG_INF), None,
                 jnp.where(hb < kv_blocks - 1, band_after, NEG_INF))
        kwin = k_scr[pl.ds(r0, win_len), :]
        vall = jnp.concatenate([vt_scr[hb + j] for j in range(win_len // BLOCK)] + [cvt], axis=1)
        valls = [_with_ones_rows(vall[g * HEAD_DIM:(g + 1) * HEAD_DIM]) for g in range(N_KV)]
        return valid, kwin, valls

    def pair_sink2(pair):
        return jnp.where(first_head, sink_ref[2 * pair], sink_ref[2 * pair + 1]) * LOG2E

    def attend(n, carry):
        halves = [half_operands(n * HALVES + j) for j in range(HALVES)]

        def scores_fn(u):
            j, pair = divmod(u, n_pairs)
            valid, kwin, _ = halves[j]
            return _head_scores(q_scr[n * HALVES + j, pair], pair // (GQA // 2), [kwin, ckb], [valid, None])

        def probs_fn(u, sc):
            return _head_probs(sc, pair_sink2(u % n_pairs))

        def pv_fn(u, pr):
            j, pair = divmod(u, n_pairs)
            o = _head_pv(pr, halves[j][2][pair // (GQA // 2)], pair_sink2(pair))
            for i in range(2):
                hd = 2 * pair + i
                ot_scr[n, hd * HEAD_DIM:(hd + 1) * HEAD_DIM, j * BLOCK:(j + 1) * BLOCK] = (
                    o[:, i * BLOCK:(i + 1) * BLOCK])

        _attend_heads(HALVES * n_pairs, GQA, scores_fn, probs_fn, pv_fn)
        return carry

    for c in range(n_blocks):
        project(c)
    lax.fori_loop(0, n_blocks, attend, 0)
    for c in range(n_blocks):
        rows = slice(c * Q_BLOCK, (c + 1) * Q_BLOCK)
        o_ref[0, rows, :] = _gate_out(x_ref[0, rows, :], ot_scr[c].T, z_scr[rows, :], gate, wout_ref[...])
        if block_done is not None:
            block_done(c)


def _rope_tables_t(seq):
    pos = np.arange(seq)
    n_freq = HEAD_DIM // 4
    inv = ROPE_THETA ** (-np.arange(n_freq, dtype=np.float64) / n_freq)
    ang = np.concatenate([(pos // GRID_W)[:, None] * inv, (pos % GRID_W)[:, None] * inv], axis=-1)
    return np.cos(ang).T.astype(np.float32), np.sin(ang).T.astype(np.float32)


def _attn_layer_kernel(sink_ref, xc_ref, xl_ref, mod_ref, nw_ref, wqkvt_ref, wz_ref, wout_ref, qw_ref, kw_ref,
                       cos_ref, sin_ref, ck_ref, cv_ref, oc_hbm, kto_ref, vto_ref, ol_hbm,
                       qkvt_scr, q_scr, z_scr, ot_scr, k_scr, vt_scr, ol_scr, out_sem, oc_scr, oc_sem,
                       *, n_ctx_steps):
    step = pl.program_id(0)
    n_lat = ol_scr.shape[0]
    n_seq = xc_ref.shape[0]
    lat = step - n_ctx_steps

    def result_copy(r, c):
        rows = pl.ds(c * Q_BLOCK, Q_BLOCK)
        return pltpu.make_async_copy(ol_scr.at[r, rows], ol_hbm.at[r, rows], out_sem.at[r, c])

    def ctx_result_copy(k, i):
        return pltpu.make_async_copy(oc_scr.at[k % 2, i], oc_hbm.at[k * n_seq + i], oc_sem.at[k % 2, i])

    @pl.when(step < n_ctx_steps)
    def _():
        @pl.when(step >= 2)
        def _():
            for i in range(n_seq):
                ctx_result_copy(step - 2, i).wait()

        _attn_ctx_body(sink_ref, xc_ref, mod_ref[0:1, :], nw_ref, wqkvt_ref, wz_ref, wout_ref, qw_ref, kw_ref,
                       oc_scr.at[step % 2], kto_ref, vto_ref, qkvt_scr, z_scr, ot_scr,
                       seq_done=lambda i: ctx_result_copy(step, i).start())

    @pl.when(step >= n_ctx_steps)
    def _():
        @pl.when(lat == 0)
        def _():
            for k in (n_ctx_steps - 2, n_ctx_steps - 1):
                for i in range(n_seq):
                    ctx_result_copy(k, i).wait()

        mod = mod_ref[pl.ds(1 + lat, 1), :]
        _attn_lat_body(sink_ref, xl_ref, mod, nw_ref, wqkvt_ref, wz_ref, wout_ref, qw_ref, kw_ref,
                       cos_ref, sin_ref, ck_ref, cv_ref, ol_scr.at[pl.ds(lat, 1)],
                       q_scr, z_scr, ot_scr, k_scr, vt_scr,
                       block_done=lambda c: result_copy(lat, c).start())

        @pl.when(lat == n_lat - 1)
        def _():
            for r in range(n_lat):
                for c in range(out_sem.shape[1]):
                    result_copy(r, c).wait()


def _attn_layer(x_ctx, x_lat, mod, nw, wqkvt, wz, wout, qw, kw, sink, ckt, cvt):
    nb_ctx, seq_ctx, _ = x_ctx.shape
    nb_lat, seq_lat, _ = x_lat.shape
    past = ckt.shape[2]
    n_seq = CTX_SEQS_PER_STEP
    assert nb_ctx % n_seq == 0 and seq_ctx == Q_BLOCK and n_seq <= seq_lat // Q_BLOCK
    n_ctx = nb_ctx // n_seq
    assert n_ctx >= 2
    cos, sin = (jnp.asarray(t) for t in _rope_tables_t(seq_lat))

    def ctx_step(i):
        return jnp.minimum(i, n_ctx - 1)

    def lat_step(i):
        return jnp.maximum(i - n_ctx, 0)

    return pl.pallas_call(
        functools.partial(_attn_layer_kernel, n_ctx_steps=n_ctx),
        grid=(n_ctx + nb_lat,),
        in_specs=[
            pl.BlockSpec(memory_space=pltpu.SMEM),
            pl.BlockSpec((n_seq, seq_ctx, D_MODEL), lambda i: (ctx_step(i), 0, 0)),
            pl.BlockSpec((1, seq_lat, D_MODEL), lambda i: (lat_step(i), 0, 0)),
            MOD_SPEC,
            _const_spec((1, D_MODEL)),
            _const_spec((BRANCH + 2 * KV_W, D_MODEL)),
            _const_spec((D_MODEL, BRANCH)),
            _const_spec((BRANCH, D_MODEL)),
            _const_spec((1, HEAD_DIM)),
            _const_spec((1, HEAD_DIM)),
            _const_spec((HEAD_DIM // 2, seq_lat)),
            _const_spec((HEAD_DIM // 2, seq_lat)),
            pl.BlockSpec((1, KV_W, past), lambda i: (lat_step(i), 0, 0)),
            pl.BlockSpec((1, KV_W, past), lambda i: (lat_step(i), 0, 0)),
        ],
        out_specs=[
            pl.BlockSpec(memory_space=pl.ANY),
            pl.BlockSpec((n_seq, KV_W, seq_ctx), lambda i: (ctx_step(i), 0, 0)),
            pl.BlockSpec((n_seq, KV_W, seq_ctx), lambda i: (ctx_step(i), 0, 0)),
            pl.BlockSpec(memory_space=pl.ANY),
        ],
        out_shape=[
            jax.ShapeDtypeStruct(x_ctx.shape, F32),
            jax.ShapeDtypeStruct((nb_ctx, KV_W, seq_ctx), F32),
            jax.ShapeDtypeStruct((nb_ctx, KV_W, seq_ctx), F32),
            jax.ShapeDtypeStruct(x_lat.shape, F32),
        ],
        scratch_shapes=[
            pltpu.VMEM((n_seq, BRANCH + 2 * KV_W, seq_ctx), F32),
            pltpu.VMEM((seq_lat // BLOCK, N_HEADS // 2, HEAD_DIM, 2 * BLOCK), BF16),
            pltpu.VMEM((seq_lat, BRANCH), F32),
            pltpu.VMEM((seq_lat // Q_BLOCK, BRANCH, Q_BLOCK), F32),
            pltpu.VMEM((seq_lat + 2 * BLOCK, KV_W), BF16),
            pltpu.VMEM((seq_lat // BLOCK + 2, KV_W, BLOCK), BF16),
            pltpu.VMEM(x_lat.shape, F32),
            pltpu.SemaphoreType.DMA((nb_lat, seq_lat // Q_BLOCK)),
            pltpu.VMEM((2, n_seq, seq_ctx, D_MODEL), F32),
            pltpu.SemaphoreType.DMA((2, n_seq)),
        ],
        compiler_params=pltpu.CompilerParams(
            dimension_semantics=("arbitrary",), vmem_limit_bytes=VMEM_LIMIT),
        name="attn_layer",
    )(sink, x_ctx, x_lat, mod, nw, wqkvt, wz, wout, qw, kw, cos, sin, ckt, cvt)


def kernel(x_prompt, x_sample, cache_k_l1, cache_v_l1, c, c_ctx, norm_w_l0, w_mod_l0, b_mod_l0,
           w_in_l0, w_out_l0, norm_w_l1, w_mod_l1, b_mod_l1, w_in_l1, q_norm_w_l1, k_norm_w_l1,
           sink_l1, w_out_l1):
    nb_ctx, seq_ctx, _ = x_prompt.shape
    nb_lat = x_sample.shape[0]
    past = cache_k_l1.shape[1]
    assert 1 + nb_lat <= MOD_ROWS
    nw0 = norm_w_l0.reshape(1, D_MODEL)
    nw1 = norm_w_l1.reshape(1, D_MODEL)
    qw = q_norm_w_l1.reshape(1, HEAD_DIM)
    kw = k_norm_w_l1.reshape(1, HEAD_DIM)

    xp, xs, mod1, wqkvt1, wz1, wout1 = _fourier_layer(
        x_prompt, x_sample, nw0,
        this_layer=(c_ctx, c, w_mod_l0, b_mod_l0, w_in_l0, w_out_l0),
        next_layer=(c_ctx, c, w_mod_l1, b_mod_l1, w_in_l1, w_out_l1))

    def to_feature_major(t):
        return jnp.transpose(t, (0, 2, 3, 1)).reshape(t.shape[0], KV_W, t.shape[1])

    def from_feature_major(t):
        return jnp.transpose(t.reshape(t.shape[0], N_KV, HEAD_DIM, t.shape[2]), (0, 3, 1, 2))

    xp, new_kt, new_vt, xs = _attn_layer(xp, xs, mod1, nw1, wqkvt1, wz1, wout1, qw, kw, sink_l1,
                                         to_feature_major(cache_k_l1), to_feature_major(cache_v_l1))
    return (xp, xs, from_feature_major(new_kt), from_feature_major(new_vt))
```

```python
import functools

import numpy as np
import jax
import jax.numpy as jnp
from jax import lax
from jax.experimental import pallas as pl
from jax.experimental.pallas import tpu as pltpu

D_MODEL = 1024
BRANCH = 1024
N_GROUPS = 4
GROUP_W = BRANCH // N_GROUPS
HALF_W = GROUP_W // 2
HEAD_DIM = 64
N_HEADS = 16
N_KV = 4
GQA = N_HEADS // N_KV
KV_W = N_KV * HEAD_DIM
GRID_W = 64
WINDOW = 128
BLOCK = 128
ROPE_THETA = 10000.0
EPS = 1e-6
NEG_INF = -1e30
LANES = 128
ROW_CHUNK = 256
Q_BLOCK = 256
HALVES = Q_BLOCK // BLOCK
CTX_SEQS_PER_STEP = 2
VMEM_LIMIT = 56 * 1024 * 1024
FOURIER_VMEM_LIMIT = 60 * 1024 * 1024
MOD_ROWS = 8
ONES_ROWS = 16
LOG2E = float(np.log2(np.e))

F32 = jnp.float32
BF16 = jnp.bfloat16


def _dot(a, b):
    return jnp.dot(a, b, preferred_element_type=F32)


def _dot_nt(a, b):
    return lax.dot_general(a, b, (((1,), (1,)), ((), ())), preferred_element_type=F32)


MOD_SPEC = pl.BlockSpec((MOD_ROWS, 3 * D_MODEL), lambda b: (0, 0))


def _mod_norm(x, nw, mod):
    shift = mod[:, :D_MODEL]
    scale = mod[:, D_MODEL:2 * D_MODEL]
    y = x * lax.rsqrt(jnp.mean(x * x, axis=-1, keepdims=True) + EPS)
    return (y * nw) * (1.0 + scale) + shift


def _mod_accumulate(cctx_ref, c_ref, w_ref, b_ref, o_ref, cond_scr, first_step):
    n_lat = c_ref.shape[0]
    cond_scr[...] = jnp.zeros_like(cond_scr)
    cond_scr[0:1, :] = cctx_ref[...]
    cond_scr[1:1 + n_lat, :] = c_ref[...]
    s = jax.nn.silu(cond_scr[...]).astype(BF16)

    @pl.when(pl.program_id(0) == first_step)
    def _():
        o_ref[...] = jnp.broadcast_to(b_ref[...], o_ref.shape)

    o_ref[...] += _dot(s, w_ref[...].astype(BF16))


def _mirror_perm():
    j = np.arange(GROUP_W)
    return np.where(j <= HALF_W, j, GROUP_W + HALF_W - j)


def _perm_matrices():
    perm = _mirror_perm()
    pm = (np.arange(GROUP_W)[:, None] == perm[None, :]).astype(np.float32)
    assert (pm[:HALF_W, HALF_W:] == 0).all() and (pm[HALF_W:, :HALF_W] == 0).all()
    return pm, np.stack([pm[:HALF_W, :HALF_W], pm[HALF_W:, HALF_W:]])


def _fourier_layer_kernel(xc_ref, xl_hbm, nw_ref, m1_ref, csc_ref, ssc_ref, csl_hbm, ssl_hbm,
                          cctx0_ref, c0_ref, wmod0_ref, bmod0_ref, win0_ref, wout0_ref, pm_ref, pmh_ref,
                          cctx_ref, c_ref, wmod1_ref, bmod1_ref, win1_ref, wout1_ref,
                          oc_ref, ol_hbm, mod1_ref, wqkvt1_ref, wz1_ref, wout1b_ref,
                          mod0_scr, win_scr, wout_scr, ta_scr, tb_scr, tr_scr, z_scr, cond_scr,
                          xl_scr, csl_scr, ssl_scr, copy_sem, ol_scr, out_sem,
                          *, n_prep_steps, n_ctx_steps):
    step = pl.program_id(0)
    n_lat = xl_scr.shape[0]
    lat = step - (n_prep_steps + n_ctx_steps)

    def request_copy(r):
        return pltpu.make_async_copy(xl_hbm.at[r], xl_scr.at[r], copy_sem.at[r])

    table_copies = [pltpu.make_async_copy(csl_hbm, csl_scr, copy_sem.at[n_lat]),
                    pltpu.make_async_copy(ssl_hbm, ssl_scr, copy_sem.at[n_lat + 1])]

    def result_copy(r, c):
        rows = pl.ds(c * ROW_CHUNK, ROW_CHUNK)
        return pltpu.make_async_copy(ol_scr.at[r, rows], ol_hbm.at[r, rows], out_sem.at[r, c])

    @pl.when(step == n_prep_steps)
    def _():
        for copy in table_copies + [request_copy(r) for r in range(n_lat)]:
            copy.start()

    @pl.when(step < n_prep_steps)
    def _():
        _mod_accumulate(cctx0_ref, c0_ref, wmod0_ref, bmod0_ref, mod0_scr, cond_scr, 0)
        rows = pl.ds(pl.multiple_of(step * HALF_W, HALF_W), HALF_W)
        w = win0_ref[...].astype(BF16)
        win_scr[rows, :BRANCH] = w[:, :BRANCH]
        pm = pm_ref[...]
        for g in range(N_GROUPS):
            cols = slice(BRANCH + g * GROUP_W, BRANCH + (g + 1) * GROUP_W)
            win_scr[rows, cols] = _dot(w[:, cols], pm).astype(BF16)
        wout_scr[rows, :] = _dot(pmh_ref[step % 2], wout0_ref[...].astype(BF16)).astype(BF16)

    @pl.when(jnp.logical_and(step >= n_prep_steps, step < n_prep_steps + n_ctx_steps))
    def _():
        _mod_accumulate(cctx_ref, c_ref, wmod1_ref, bmod1_ref, mod1_ref, cond_scr, n_prep_steps)
        w = win1_ref[...]
        wqkvt1_ref[...] = w[:, :BRANCH + 2 * KV_W].T.astype(BF16)
        wz1_ref[...] = w[:, BRANCH + 2 * KV_W:].astype(BF16)
        wout1b_ref[...] = wout1_ref[...].astype(BF16)
        _fourier_body(xc_ref, oc_ref, mod0_scr[0:1, :], nw_ref, win_scr, wout_scr, m1_ref, csc_ref, ssc_ref,
                      ta_scr, tb_scr, tr_scr, z_scr)

    @pl.when(step >= n_prep_steps + n_ctx_steps)
    def _():
        @pl.when(lat == 0)
        def _():
            for copy in table_copies:
                copy.wait()

        request_copy(lat).wait()
        mod = mod0_scr[pl.ds(1 + lat, 1), :]
        _fourier_body(xl_scr.at[pl.ds(lat, 1)], ol_scr.at[pl.ds(lat, 1)], mod, nw_ref, win_scr, wout_scr,
                      m1_ref, csl_scr, ssl_scr, ta_scr, tb_scr, tr_scr, z_scr,
                      rows_done=lambda _, rows: result_copy(lat, rows.start // ROW_CHUNK).start())

        @pl.when(lat == n_lat - 1)
        def _():
            for r in range(n_lat):
                for c in range(out_sem.shape[1]):
                    result_copy(r, c).wait()


def _fourier_body(x_ref, o_ref, mod, nw_ref, win_ref, wout_ref, m1_ref, cs_ref, ss_ref,
                  ta_scr, tb_scr, tr_scr, z_scr, rows_done=None):
    gate = mod[:, 2 * D_MODEL:]
    nw = nw_ref[...]
    n_seq, seq, _ = x_ref.shape
    n_chunks = seq // ROW_CHUNK
    lane = lax.broadcasted_iota(jnp.int32, (ROW_CHUNK, HALF_W), 1)
    for i in range(n_seq):
        for c in range(n_chunks):
            rows = slice(c * ROW_CHUNK, (c + 1) * ROW_CHUNK)
            srows = slice(i * seq + c * ROW_CHUNK, i * seq + (c + 1) * ROW_CHUNK)
            h = _mod_norm(x_ref[i, rows, :], nw, mod).astype(BF16)
            uz = _dot(h, win_ref[...])
            z_scr[srows, :] = uz[:, BRANCH:]
            u = uz[:, :BRANCH].astype(BF16)
            tr = jnp.zeros((ROW_CHUNK, HALF_W), F32)
            for g in range(N_GROUPS):
                t = _dot(u[:, g * GROUP_W:(g + 1) * GROUP_W], m1_ref[...])
                half = slice(g * HALF_W, (g + 1) * HALF_W)
                ta_scr[srows, half] = t[:, :HALF_W].astype(BF16)
                tb = t[:, HALF_W:]
                tb_scr[srows, half] = tb.astype(BF16)
                tr = jnp.where(lane == g, tb if g == 0 else pltpu.roll(tb, g, axis=1), tr)
            tr_scr[srows, :] = tr.astype(BF16)
    for i in range(n_seq):
        seq_rows = slice(i * seq, (i + 1) * seq)
        for c in range(n_chunks):
            rows = slice(c * ROW_CHUNK, (c + 1) * ROW_CHUNK)
            srows = slice(i * seq + c * ROW_CHUNK, i * seq + (c + 1) * ROW_CHUNK)
            cs = cs_ref[rows, :]
            p = _dot(cs, ta_scr[seq_rows, :])
            q = _dot(ss_ref[rows, :], tb_scr[seq_rows, :])
            r = _dot(cs, tr_scr[seq_rows, :])
            parts = []
            for g in range(N_GROUPS):
                half = slice(g * HALF_W, (g + 1) * HALF_W)
                pg, qg = p[:, half], q[:, half]
                rg = r if g == 0 else pltpu.roll(r, HALF_W - g, axis=1)
                parts.append(jnp.where(lane == 0, pg, pg - qg))
                parts.append(jnp.where(lane == 0, rg, pg + qg))
            y = jnp.concatenate(parts, axis=1)
            y = (y * jax.nn.silu(z_scr[srows, :])).astype(BF16)
            o_ref[i, rows, :] = x_ref[i, rows, :] + gate * _dot(y, wout_ref[...])
            if rows_done is not None:
                rows_done(i, rows)


def _dft_tables(seq):
    c = np.arange(GROUP_W)[:, None]
    k = np.arange(HALF_W)[None, :]
    cos_lo = np.cos(2.0 * np.pi * ((c * k) % GROUP_W) / GROUP_W)
    sin_lo = np.sin(2.0 * np.pi * ((c * k) % GROUP_W) / GROUP_W)
    sin_lo[:, 0] = np.cos(np.pi * c[:, 0])
    m1 = np.concatenate([cos_lo, sin_lo], axis=1) / np.sqrt(GROUP_W)
    n = np.arange(seq)
    ang = 2.0 * np.pi * ((n[:, None] * n[None, :]) % seq) / seq
    cs = np.cos(ang) / np.sqrt(seq)
    ss = np.sin(ang) / np.sqrt(seq)
    return m1.astype(np.float32), cs.astype(np.float32), ss.astype(np.float32)


def _const_spec(shape):
    return pl.BlockSpec(shape, lambda b: (0,) * len(shape))


def _fourier_layer(x_ctx, x_lat, nw, this_layer, next_layer):
    nb_ctx, seq_ctx, _ = x_ctx.shape
    nb_lat, seq_lat, _ = x_lat.shape
    n_seq = CTX_SEQS_PER_STEP
    assert nb_ctx % n_seq == 0
    n_prep = D_MODEL // HALF_W
    n_ctx = nb_ctx // n_seq
    assert D_MODEL % (n_ctx * LANES) == 0 and n_seq * seq_ctx <= seq_lat
    rows = D_MODEL // n_ctx
    assert rows == HALF_W
    c_ctx, c, w_mod0, b_mod0, w_in0, w_out0 = this_layer
    _, _, w_mod1, b_mod1, w_in1, w_out1 = next_layer
    n_lat = c.shape[0]
    n_qkvz = 2 * BRANCH + 2 * KV_W
    m1, csc, ssc = (jnp.asarray(t).astype(BF16) for t in _dft_tables(seq_ctx))
    _, csl, ssl = (jnp.asarray(t).astype(BF16) for t in _dft_tables(seq_lat))
    pm, pmh = (jnp.asarray(t).astype(BF16) for t in _perm_matrices())
    cc = c_ctx.reshape(1, D_MODEL)

    def prep_step(i):
        return jnp.minimum(i, n_prep - 1)

    def ctx_step(i):
        return jnp.clip(i - n_prep, 0, n_ctx - 1)

    def chunk_specs(step_fn, w_in_cols):
        return [pl.BlockSpec((1, HALF_W), lambda i: (0, step_fn(i))),
                pl.BlockSpec((n_lat, HALF_W), lambda i: (0, step_fn(i))),
                pl.BlockSpec((HALF_W, 3 * D_MODEL), lambda i: (step_fn(i), 0)),
                _const_spec((1, 3 * D_MODEL)),
                pl.BlockSpec((HALF_W, w_in_cols), lambda i: (step_fn(i), 0)),
                pl.BlockSpec((HALF_W, D_MODEL), lambda i: (step_fn(i), 0))]

    in_specs = [
        pl.BlockSpec((n_seq, seq_ctx, D_MODEL), lambda i: (ctx_step(i), 0, 0)),
        pl.BlockSpec(memory_space=pl.ANY),
        _const_spec((1, D_MODEL)),
        _const_spec((GROUP_W, GROUP_W)),
        _const_spec((seq_ctx, seq_ctx)),
        _const_spec((seq_ctx, seq_ctx)),
        pl.BlockSpec(memory_space=pl.ANY),
        pl.BlockSpec(memory_space=pl.ANY),
    ] + chunk_specs(prep_step, 2 * BRANCH) + [
        _const_spec((GROUP_W, GROUP_W)),
        _const_spec((2, HALF_W, HALF_W)),
    ] + chunk_specs(ctx_step, n_qkvz)
    out_specs = [
        pl.BlockSpec((n_seq, seq_ctx, D_MODEL), lambda i: (ctx_step(i), 0, 0)),
        pl.BlockSpec(memory_space=pl.ANY),
        MOD_SPEC,
        pl.BlockSpec((BRANCH + 2 * KV_W, rows), lambda i: (0, ctx_step(i))),
        pl.BlockSpec((rows, BRANCH), lambda i: (ctx_step(i), 0)),
        pl.BlockSpec((rows, D_MODEL), lambda i: (ctx_step(i), 0)),
    ]
    out_shape = [
        jax.ShapeDtypeStruct(x_ctx.shape, F32),
        jax.ShapeDtypeStruct(x_lat.shape, F32),
        jax.ShapeDtypeStruct((MOD_ROWS, 3 * D_MODEL), F32),
        jax.ShapeDtypeStruct((BRANCH + 2 * KV_W, D_MODEL), BF16),
        jax.ShapeDtypeStruct((D_MODEL, BRANCH), BF16),
        jax.ShapeDtypeStruct((BRANCH, D_MODEL), BF16),
    ]
    return pl.pallas_call(
        functools.partial(_fourier_layer_kernel, n_prep_steps=n_prep, n_ctx_steps=n_ctx),
        grid=(n_prep + n_ctx + nb_lat,),
        in_specs=in_specs,
        out_specs=out_specs,
        out_shape=out_shape,
        scratch_shapes=[
            pltpu.VMEM((MOD_ROWS, 3 * D_MODEL), F32),
            pltpu.VMEM((D_MODEL, 2 * BRANCH), BF16),
            pltpu.VMEM((BRANCH, D_MODEL), BF16),
            pltpu.VMEM((seq_lat, N_GROUPS * HALF_W), BF16),
            pltpu.VMEM((seq_lat, N_GROUPS * HALF_W), BF16),
            pltpu.VMEM((seq_lat, HALF_W), BF16),
            pltpu.VMEM((seq_lat, BRANCH), F32),
            pltpu.VMEM((MOD_ROWS, HALF_W), F32),
            pltpu.VMEM(x_lat.shape, F32),
            pltpu.VMEM((seq_lat, seq_lat), BF16),
            pltpu.VMEM((seq_lat, seq_lat), BF16),
            pltpu.SemaphoreType.DMA((nb_lat + 2,)),
            pltpu.VMEM(x_lat.shape, F32),
            pltpu.SemaphoreType.DMA((nb_lat, seq_lat // ROW_CHUNK)),
        ],
        compiler_params=pltpu.CompilerParams(
            dimension_semantics=("arbitrary",), vmem_limit_bytes=FOURIER_VMEM_LIMIT),
        name="fourier_layer",
    )(x_ctx, x_lat, nw, m1, csc, ssc, csl, ssl,
      cc, c, w_mod0, b_mod0.reshape(1, 3 * D_MODEL), w_in0, w_out0, pm, pmh,
      cc, c, w_mod1, b_mod1.reshape(1, 3 * D_MODEL), w_in1, w_out1)


def _head_weight_tile(w_ref, n_tokens):
    row = jnp.broadcast_to(w_ref[...], (HEAD_DIM, HEAD_DIM))
    ii = lax.broadcasted_iota(jnp.int32, (HEAD_DIM, HEAD_DIM), 0)
    jj = lax.broadcasted_iota(jnp.int32, (HEAD_DIM, HEAD_DIM), 1)
    col = jnp.sum(jnp.where(ii == jj, row, 0.0), axis=1, keepdims=True)
    return jnp.broadcast_to(col, (HEAD_DIM, n_tokens))


def _head_rms(t, w):
    return (t * lax.rsqrt(jnp.mean(t * t, axis=0, keepdims=True) + EPS)) * w


def _rope_t(t, cos, sin):
    half = HEAD_DIM // 2
    x1, x2 = t[:half], t[half:]
    return jnp.concatenate([x1 * cos - x2 * sin, x1 * sin + x2 * cos], axis=0)


def _head_scores(qn, g, keys, biases):
    zeros = jnp.zeros_like(qn)
    qz = jnp.concatenate([qn, zeros] if g % 2 == 0 else [zeros, qn], axis=0)
    blk = slice((g // 2) * LANES, (g // 2 + 1) * LANES)
    scores = []
    smax = None
    for k, bias in zip(keys, biases):
        s = _dot(k[:, blk], qz)
        if bias is not None:
            s = jnp.concatenate(
                [s[c * BLOCK:(c + 1) * BLOCK] if b is None else
                 s[c * BLOCK:(c + 1) * BLOCK] + jnp.concatenate([b] * (s.shape[1] // b.shape[1]), axis=1)
                 for c, b in enumerate(bias)], axis=0)
        cmax = jnp.max(s, axis=0, keepdims=True)
        smax = cmax if smax is None else jnp.maximum(smax, cmax)
        scores.append(s)
    return scores, smax


def _with_ones_rows(vt):
    return jnp.concatenate([vt, jnp.ones((ONES_ROWS, vt.shape[1]), vt.dtype)], axis=0)


def _head_probs(scored, sink2):
    scores, smax = scored
    m = jnp.maximum(smax, sink2)
    return jnp.concatenate([jnp.exp2(s - m).astype(BF16) for s in scores], axis=0), m


def _head_pv(probs, values_t, sink2):
    p, m = probs
    acc = _dot(values_t, p)
    den = acc[HEAD_DIM:HEAD_DIM + 1] + jnp.exp2(sink2 - m)
    return acc[:HEAD_DIM] * (1.0 / den)


def _attend_heads(n_units, stage, scores_fn, probs_fn, pv_fn, fillers=()):
    n_stages = n_units // stage
    pending = [scores_fn(u) for u in range(stage)]
    for g in range(n_stages):
        units = range(g * stage, (g + 1) * stage)
        if g < len(fillers):
            fillers[g]()
        nxt = [scores_fn(u) for u in range((g + 1) * stage, (g + 2) * stage)] if g + 1 < n_stages else None
        probs = [probs_fn(u, sc) for u, sc in zip(units, pending)]
        for u, pr in zip(units, probs):
            pv_fn(u, pr)
        pending = nxt


def _gate_out(x, o, z, gate, wout):
    y = (o * jax.nn.silu(z)).astype(BF16)
    return x + gate * _dot(y, wout)


def _attn_ctx_body(sink_ref, x_ref, mod, nw_ref, wqkvt_ref, wz_ref, wout_ref, qw_ref, kw_ref,
                   o_ref, kto_ref, vto_ref, qkvt_scr, z_scr, ot_scr):
    seq = x_ref.shape[1]
    n_seq = x_ref.shape[0]
    gate = mod[:, 2 * D_MODEL:]
    kw = _head_weight_tile(kw_ref, seq)
    qw = _head_weight_tile(qw_ref, seq)

    def project(i):
        h = _mod_norm(x_ref[i], nw_ref[...], mod).astype(BF16)
        qkvt_scr[i] = _dot_nt(wqkvt_ref[...], h)
        z_scr[i * seq:(i + 1) * seq, :] = _dot(h, wz_ref[...])

    def keys_values(i):
        knt = jnp.concatenate(
            [_head_rms(qkvt_scr[i, BRANCH + g * HEAD_DIM:BRANCH + (g + 1) * HEAD_DIM, :], kw)
             for g in range(N_KV)], axis=0)
        kto_ref[i] = knt
        vtf = qkvt_scr[i, BRANCH + KV_W:, :]
        vto_ref[i] = vtf
        vt = vtf.astype(BF16)
        return (knt.T.astype(BF16),
                [_with_ones_rows(vt[g * HEAD_DIM:(g + 1) * HEAD_DIM]) for g in range(N_KV)])

    def output(i):
        o_ref[i] = _gate_out(x_ref[i], ot_scr[i].T, z_scr[i * seq:(i + 1) * seq, :], gate, wout_ref[...])

    for i in range(n_seq):
        project(i)
    kv = [keys_values(i) for i in range(n_seq)]

    def scores_fn(u):
        i, hd = divmod(u, N_HEADS)
        t = qkvt_scr[i, hd * HEAD_DIM:(hd + 1) * HEAD_DIM, :]
        qn = (_head_rms(t, qw) * (HEAD_DIM ** -0.5 * LOG2E)).astype(BF16)
        return _head_scores(qn, hd // GQA, [kv[i][0]], [None])

    def probs_fn(u, sc):
        return _head_probs(sc, sink_ref[u % N_HEADS] * LOG2E)

    def pv_fn(u, pr):
        i, hd = divmod(u, N_HEADS)
        ot_scr[i, hd * HEAD_DIM:(hd + 1) * HEAD_DIM, :] = _head_pv(
            pr, kv[i][1][hd // GQA], sink_ref[hd] * LOG2E)

    fillers = [lambda: None] + [functools.partial(output, i) for i in range(n_seq - 1)]
    _attend_heads(n_seq * N_HEADS, N_HEADS, scores_fn, probs_fn, pv_fn, fillers)
    output(n_seq - 1)


def _attn_lat_body(sink_ref, x_ref, mod, nw_ref, wqkvt_ref, wz_ref, wout_ref, qw_ref, kw_ref,
                   cos_ref, sin_ref, ck_ref, cv_ref, o_ref,
                   q_scr, z_scr, ot_scr, k_scr, vt_scr, block_done=None):
    seq = x_ref.shape[1]
    gate = mod[:, 2 * D_MODEL:]
    nw = nw_ref[...]
    qw = _head_weight_tile(qw_ref, Q_BLOCK)
    kw = _head_weight_tile(kw_ref, Q_BLOCK)
    n_blocks = seq // Q_BLOCK
    kv_blocks = seq // BLOCK
    k_scr[0:BLOCK, :] = jnp.zeros((BLOCK, KV_W), BF16)
    k_scr[BLOCK + seq:2 * BLOCK + seq, :] = jnp.zeros((BLOCK, KV_W), BF16)
    vt_scr[0] = jnp.zeros((KV_W, BLOCK), BF16)
    vt_scr[kv_blocks + 1] = jnp.zeros((KV_W, BLOCK), BF16)

    def project(c):
        rows = slice(c * Q_BLOCK, (c + 1) * Q_BLOCK)
        h = _mod_norm(x_ref[0, rows, :], nw, mod).astype(BF16)
        return h, _dot_nt(wqkvt_ref[...], h)

    def heads(c, qkvt):
        rows = slice(c * Q_BLOCK, (c + 1) * Q_BLOCK)
        cos = cos_ref[:, rows]
        sin = sin_ref[:, rows]
        for hd in range(N_HEADS):
            hr = slice(hd * HEAD_DIM, (hd + 1) * HEAD_DIM)
            t = _rope_t(_head_rms(qkvt[hr], qw), cos, sin)
            t = (t * (HEAD_DIM ** -0.5 * LOG2E)).astype(BF16)
            for j in range(HALVES):
                q_scr[c * HALVES + j, hd // 2, :, (hd % 2) * BLOCK:(hd % 2 + 1) * BLOCK] = (
                    t[:, j * BLOCK:(j + 1) * BLOCK])
        knt = jnp.concatenate(
            [_rope_t(_head_rms(qkvt[BRANCH + g * HEAD_DIM:BRANCH + (g + 1) * HEAD_DIM], kw), cos, sin)
             for g in range(N_KV)], axis=0)
        k_scr[BLOCK + c * Q_BLOCK:BLOCK + (c + 1) * Q_BLOCK, :] = knt.T.astype(BF16)
        vt = qkvt[BRANCH + KV_W:].astype(BF16)
        for j in range(HALVES):
            vt_scr[1 + c * HALVES + j] = vt[:, j * BLOCK:(j + 1) * BLOCK]

    ckb = ck_ref[0].T.astype(BF16)
    cvt = cv_ref[0].astype(BF16)

    win_len = 3 * BLOCK
    n_pairs = N_HEADS // 2
    kj = lax.broadcasted_iota(jnp.int32, (BLOCK, BLOCK), 0)
    qi = lax.broadcasted_iota(jnp.int32, (BLOCK, BLOCK), 1)
    first_head = lax.broadcasted_iota(jnp.int32, (1, 2 * BLOCK), 1) < BLOCK
    assert WINDOW >= BLOCK - 1

    def band_bias(c):
        rel = kj + (c - 1) * BLOCK - qi
        return jnp.where((rel >= -WINDOW) & (rel <= WINDOW), 0.0, NEG_INF)

    band_before, band_after = band_bias(0), band_bias(2)

    def half_operands(hb):
        r0 = pl.multiple_of(hb * BLOCK, BLOCK)
        valid = (jnp.where(hb > 0, band_before, NEG_INF), None,
                 jnp.where(hb < kv_blocks - 1, band_after, NEG_INF))
        kwin = k_scr[pl.ds(r0, win_len), :]
        vall = jnp.concatenate([vt_scr[hb + j] for j in range(win_len // BLOCK)] + [cvt], axis=1)
        valls = [_with_ones_rows(vall[g * HEAD_DIM:(g + 1) * HEAD_DIM]) for g in range(N_KV)]
        return valid, kwin, valls

    def pair_sink2(pair):
        return jnp.where(first_head, sink_ref[2 * pair], sink_ref[2 * pair + 1]) * LOG2E

    def attend(n, carry):
        halves = [half_operands(n * HALVES + j) for j in range(HALVES)]

        def scores_fn(u):
            j, pair = divmod(u, n_pairs)
            valid, kwin, _ = halves[j]
            return _head_scores(q_scr[n * HALVES + j, pair], pair // (GQA // 2), [kwin, ckb], [valid, None])

        def probs_fn(u, sc):
            return _head_probs(sc, pair_sink2(u % n_pairs))

        def pv_fn(u, pr):
            j, pair = divmod(u, n_pairs)
            o = _head_pv(pr, halves[j][2][pair // (GQA // 2)], pair_sink2(pair))
            for i in range(2):
                hd = 2 * pair + i
                ot_scr[n, hd * HEAD_DIM:(hd + 1) * HEAD_DIM, j * BLOCK:(j + 1) * BLOCK] = (
                    o[:, i * BLOCK:(i + 1) * BLOCK])

        _attend_heads(HALVES * n_pairs, GQA, scores_fn, probs_fn, pv_fn)
        return carry

    projected = project(0)
    for c in range(n_blocks):
        h, qkvt = projected
        projected = project(c + 1) if c + 1 < n_blocks else None
        z_scr[c * Q_BLOCK:(c + 1) * Q_BLOCK, :] = _dot(h, wz_ref[...])
        heads(c, qkvt)
    lax.fori_loop(0, n_blocks, attend, 0)
    for c in range(n_blocks):
        rows = slice(c * Q_BLOCK, (c + 1) * Q_BLOCK)
        o_ref[0, rows, :] = _gate_out(x_ref[0, rows, :], ot_scr[c].T, z_scr[rows, :], gate, wout_ref[...])
        if block_done is not None:
            block_done(c)


def _rope_tables_t(seq):
    pos = np.arange(seq)
    n_freq = HEAD_DIM // 4
    inv = ROPE_THETA ** (-np.arange(n_freq, dtype=np.float64) / n_freq)
    ang = np.concatenate([(pos // GRID_W)[:, None] * inv, (pos % GRID_W)[:, None] * inv], axis=-1)
    return np.cos(ang).T.astype(np.float32), np.sin(ang).T.astype(np.float32)


def _attn_layer_kernel(sink_ref, xc_ref, xl_ref, mod_ref, nw_ref, wqkvt_ref, wz_ref, wout_ref, qw_ref, kw_ref,
                       cos_ref, sin_ref, ck_ref, cv_ref, oc_ref, kto_ref, vto_ref, ol_hbm,
                       qkvt_scr, q_scr, z_scr, ot_scr, k_scr, vt_scr, ol_scr, out_sem, *, n_ctx_steps):
    step = pl.program_id(0)
    n_lat = ol_scr.shape[0]
    lat = step - n_ctx_steps

    def result_copy(r, c):
        rows = pl.ds(c * Q_BLOCK, Q_BLOCK)
        return pltpu.make_async_copy(ol_scr.at[r, rows], ol_hbm.at[r, rows], out_sem.at[r, c])

    @pl.when(step < n_ctx_steps)
    def _():
        _attn_ctx_body(sink_ref, xc_ref, mod_ref[0:1, :], nw_ref, wqkvt_ref, wz_ref, wout_ref, qw_ref, kw_ref,
                       oc_ref, kto_ref, vto_ref, qkvt_scr, z_scr, ot_scr)

    @pl.when(step >= n_ctx_steps)
    def _():
        mod = mod_ref[pl.ds(1 + lat, 1), :]
        _attn_lat_body(sink_ref, xl_ref, mod, nw_ref, wqkvt_ref, wz_ref, wout_ref, qw_ref, kw_ref,
                       cos_ref, sin_ref, ck_ref, cv_ref, ol_scr.at[pl.ds(lat, 1)],
                       q_scr, z_scr, ot_scr, k_scr, vt_scr,
                       block_done=lambda c: result_copy(lat, c).start())

        @pl.when(lat == n_lat - 1)
        def _():
            for r in range(n_lat):
                for c in range(out_sem.shape[1]):
                    result_copy(r, c).wait()


def _attn_layer(x_ctx, x_lat, mod, nw, wqkvt, wz, wout, qw, kw, sink, ckt, cvt):
    nb_ctx, seq_ctx, _ = x_ctx.shape
    nb_lat, seq_lat, _ = x_lat.shape
    past = ckt.shape[2]
    n_seq = CTX_SEQS_PER_STEP
    assert nb_ctx % n_seq == 0 and seq_ctx == Q_BLOCK and n_seq <= seq_lat // Q_BLOCK
    n_ctx = nb_ctx // n_seq
    cos, sin = (jnp.asarray(t) for t in _rope_tables_t(seq_lat))

    def ctx_step(i):
        return jnp.minimum(i, n_ctx - 1)

    def lat_step(i):
        return jnp.maximum(i - n_ctx, 0)

    return pl.pallas_call(
        functools.partial(_attn_layer_kernel, n_ctx_steps=n_ctx),
        grid=(n_ctx + nb_lat,),
        in_specs=[
            pl.BlockSpec(memory_space=pltpu.SMEM),
            pl.BlockSpec((n_seq, seq_ctx, D_MODEL), lambda i: (ctx_step(i), 0, 0)),
            pl.BlockSpec((1, seq_lat, D_MODEL), lambda i: (lat_step(i), 0, 0)),
            MOD_SPEC,
            _const_spec((1, D_MODEL)),
            _const_spec((BRANCH + 2 * KV_W, D_MODEL)),
            _const_spec((D_MODEL, BRANCH)),
            _const_spec((BRANCH, D_MODEL)),
            _const_spec((1, HEAD_DIM)),
            _const_spec((1, HEAD_DIM)),
            _const_spec((HEAD_DIM // 2, seq_lat)),
            _const_spec((HEAD_DIM // 2, seq_lat)),
            pl.BlockSpec((1, KV_W, past), lambda i: (lat_step(i), 0, 0)),
            pl.BlockSpec((1, KV_W, past), lambda i: (lat_step(i), 0, 0)),
        ],
        out_specs=[
            pl.BlockSpec((n_seq, seq_ctx, D_MODEL), lambda i: (ctx_step(i), 0, 0)),
            pl.BlockSpec((n_seq, KV_W, seq_ctx), lambda i: (ctx_step(i), 0, 0)),
            pl.BlockSpec((n_seq, KV_W, seq_ctx), lambda i: (ctx_step(i), 0, 0)),
            pl.BlockSpec(memory_space=pl.ANY),
        ],
        out_shape=[
            jax.ShapeDtypeStruct(x_ctx.shape, F32),
            jax.ShapeDtypeStruct((nb_ctx, KV_W, seq_ctx), F32),
            jax.ShapeDtypeStruct((nb_ctx, KV_W, seq_ctx), F32),
            jax.ShapeDtypeStruct(x_lat.shape, F32),
        ],
        scratch_shapes=[
            pltpu.VMEM((n_seq, BRANCH + 2 * KV_W, seq_ctx), F32),
            pltpu.VMEM((seq_lat // BLOCK, N_HEADS // 2, HEAD_DIM, 2 * BLOCK), BF16),
            pltpu.VMEM((seq_lat, BRANCH), F32),
            pltpu.VMEM((seq_lat // Q_BLOCK, BRANCH, Q_BLOCK), F32),
            pltpu.VMEM((seq_lat + 2 * BLOCK, KV_W), BF16),
            pltpu.VMEM((seq_lat // BLOCK + 2, KV_W, BLOCK), BF16),
            pltpu.VMEM(x_lat.shape, F32),
            pltpu.SemaphoreType.DMA((nb_lat, seq_lat // Q_BLOCK)),
        ],
        compiler_params=pltpu.CompilerParams(
            dimension_semantics=("arbitrary",), vmem_limit_bytes=VMEM_LIMIT),
        name="attn_layer",
    )(sink, x_ctx, x_lat, mod, nw, wqkvt, wz, wout, qw, kw, cos, sin, ckt, cvt)


def kernel(x_prompt, x_sample, cache_k_l1, cache_v_l1, c, c_ctx, norm_w_l0, w_mod_l0, b_mod_l0,
           w_in_l0, w_out_l0, norm_w_l1, w_mod_l1, b_mod_l1, w_in_l1, q_norm_w_l1, k_norm_w_l1,
           sink_l1, w_out_l1):
    nb_ctx, seq_ctx, _ = x_prompt.shape
    nb_lat = x_sample.shape[0]
    past = cache_k_l1.shape[1]
    assert 1 + nb_lat <= MOD_ROWS
    nw0 = norm_w_l0.reshape(1, D_MODEL)
    nw1 = norm_w_l1.reshape(1, D_MODEL)
    qw = q_norm_w_l1.reshape(1, HEAD_DIM)
    kw = k_norm_w_l1.reshape(1, HEAD_DIM)

    xp, xs, mod1, wqkvt1, wz1, wout1 = _fourier_layer(
        x_prompt, x_sample, nw0,
        this_layer=(c_ctx, c, w_mod_l0, b_mod_l0, w_in_l0, w_out_l0),
        next_layer=(c_ctx, c, w_mod_l1, b_mod_l1, w_in_l1, w_out_l1))

    def to_feature_major(t):
        return jnp.transpose(t, (0, 2, 3, 1)).reshape(t.shape[0], KV_W, t.shape[1])

    def from_feature_major(t):
        return jnp.transpose(t.reshape(t.shape[0], N_KV, HEAD_DIM, t.shape[2]), (0, 3, 1, 2))

    xp, new_kt, new_vt, xs = _attn_layer(xp, xs, mod1, nw1, wqkvt1, wz1, wout1, qw, kw, sink_l1,
                                         to_feature_major(cache_k_l1), to_feature_major(cache_v_l1))
    return (xp, xs, from_feature_major(new_kt), from_feature_major(new_vt))
```

```python
import functools

import numpy as np
import jax
import jax.numpy as jnp
from jax import lax
from jax.experimental import pallas as pl
from jax.experimental.pallas import tpu as pltpu

D_MODEL = 1024
BRANCH = 1024
N_GROUPS = 4
GROUP_W = BRANCH // N_GROUPS
HALF_W = GROUP_W // 2
HEAD_DIM = 64
N_HEADS = 16
N_KV = 4
GQA = N_HEADS // N_KV
KV_W = N_KV * HEAD_DIM
GRID_W = 64
WINDOW = 128
BLOCK = 128
ROPE_THETA = 10000.0
EPS = 1e-6
NEG_INF = -1e30
LANES = 128
ROW_CHUNK = 256
Q_BLOCK = 256
HALVES = Q_BLOCK // BLOCK
CTX_SEQS_PER_STEP = 2
VMEM_LIMIT = 56 * 1024 * 1024
FOURIER_VMEM_LIMIT = 60 * 1024 * 1024
MOD_ROWS = 8
ONES_ROWS = 16
LOG2E = float(np.log2(np.e))

F32 = jnp.float32
BF16 = jnp.bfloat16


def _dot(a, b):
    return jnp.dot(a, b, preferred_element_type=F32)


def _dot_nt(a, b):
    return lax.dot_general(a, b, (((1,), (1,)), ((), ())), preferred_element_type=F32)


MOD_SPEC = pl.BlockSpec((MOD_ROWS, 3 * D_MODEL), lambda b: (0, 0))


def _mod_norm(x, nw, mod):
    shift = mod[:, :D_MODEL]
    scale = mod[:, D_MODEL:2 * D_MODEL]
    y = x * lax.rsqrt(jnp.mean(x * x, axis=-1, keepdims=True) + EPS)
    return (y * nw) * (1.0 + scale) + shift


def _mod_accumulate(cctx_ref, c_ref, w_ref, b_ref, o_ref, cond_scr, first_step):
    n_lat = c_ref.shape[0]
    cond_scr[...] = jnp.zeros_like(cond_scr)
    cond_scr[0:1, :] = cctx_ref[...]
    cond_scr[1:1 + n_lat, :] = c_ref[...]
    s = jax.nn.silu(cond_scr[...]).astype(BF16)

    @pl.when(pl.program_id(0) == first_step)
    def _():
        o_ref[...] = jnp.broadcast_to(b_ref[...], o_ref.shape)

    o_ref[...] += _dot(s, w_ref[...].astype(BF16))


def _mirror_perm():
    j = np.arange(GROUP_W)
    return np.where(j <= HALF_W, j, GROUP_W + HALF_W - j)


def _perm_matrices():
    perm = _mirror_perm()
    pm = (np.arange(GROUP_W)[:, None] == perm[None, :]).astype(np.float32)
    assert (pm[:HALF_W, HALF_W:] == 0).all() and (pm[HALF_W:, :HALF_W] == 0).all()
    return pm, np.stack([pm[:HALF_W, :HALF_W], pm[HALF_W:, HALF_W:]])


def _fourier_layer_kernel(xc_ref, xl_hbm, nw_ref, m1_ref, csc_ref, ssc_ref, csl_hbm, ssl_hbm,
                          cctx0_ref, c0_ref, wmod0_ref, bmod0_ref, win0_ref, wout0_ref, pm_ref, pmh_ref,
                          cctx_ref, c_ref, wmod1_ref, bmod1_ref, win1_ref, wout1_ref,
                          oc_ref, ol_hbm, mod1_ref, wqkvt1_ref, wz1_ref, wout1b_ref,
                          mod0_scr, win_scr, wout_scr, ta_scr, tb_scr, tr_scr, z_scr, cond_scr,
                          xl_scr, csl_scr, ssl_scr, copy_sem, ol_scr, out_sem,
                          *, n_prep_steps, n_ctx_steps):
    step = pl.program_id(0)
    n_lat = xl_scr.shape[0]
    lat = step - (n_prep_steps + n_ctx_steps)

    def request_copy(r):
        return pltpu.make_async_copy(xl_hbm.at[r], xl_scr.at[r], copy_sem.at[r])

    table_copies = [pltpu.make_async_copy(csl_hbm, csl_scr, copy_sem.at[n_lat]),
                    pltpu.make_async_copy(ssl_hbm, ssl_scr, copy_sem.at[n_lat + 1])]

    def result_copy(r, c):
        rows = pl.ds(c * ROW_CHUNK, ROW_CHUNK)
        return pltpu.make_async_copy(ol_scr.at[r, rows], ol_hbm.at[r, rows], out_sem.at[r, c])

    @pl.when(step == n_prep_steps)
    def _():
        for copy in table_copies + [request_copy(r) for r in range(n_lat)]:
            copy.start()

    @pl.when(step < n_prep_steps)
    def _():
        _mod_accumulate(cctx0_ref, c0_ref, wmod0_ref, bmod0_ref, mod0_scr, cond_scr, 0)
        rows = pl.ds(pl.multiple_of(step * HALF_W, HALF_W), HALF_W)
        w = win0_ref[...].astype(BF16)
        win_scr[rows, :BRANCH] = w[:, :BRANCH]
        pm = pm_ref[...]
        for g in range(N_GROUPS):
            cols = slice(BRANCH + g * GROUP_W, BRANCH + (g + 1) * GROUP_W)
            win_scr[rows, cols] = _dot(w[:, cols], pm).astype(BF16)
        wout_scr[rows, :] = _dot(pmh_ref[step % 2], wout0_ref[...].astype(BF16)).astype(BF16)

    @pl.when(jnp.logical_and(step >= n_prep_steps, step < n_prep_steps + n_ctx_steps))
    def _():
        _mod_accumulate(cctx_ref, c_ref, wmod1_ref, bmod1_ref, mod1_ref, cond_scr, n_prep_steps)
        w = win1_ref[...]
        wqkvt1_ref[...] = w[:, :BRANCH + 2 * KV_W].T.astype(BF16)
        wz1_ref[...] = w[:, BRANCH + 2 * KV_W:].astype(BF16)
        wout1b_ref[...] = wout1_ref[...].astype(BF16)
        _fourier_body(xc_ref, oc_ref, mod0_scr[0:1, :], nw_ref, win_scr, wout_scr, m1_ref, csc_ref, ssc_ref,
                      ta_scr, tb_scr, tr_scr, z_scr)

    @pl.when(step >= n_prep_steps + n_ctx_steps)
    def _():
        @pl.when(lat == 0)
        def _():
            for copy in table_copies:
                copy.wait()

        request_copy(lat).wait()
        mod = mod0_scr[pl.ds(1 + lat, 1), :]
        _fourier_body(xl_scr.at[pl.ds(lat, 1)], ol_scr.at[pl.ds(lat, 1)], mod, nw_ref, win_scr, wout_scr,
                      m1_ref, csl_scr, ssl_scr, ta_scr, tb_scr, tr_scr, z_scr,
                      rows_done=lambda _, rows: result_copy(lat, rows.start // ROW_CHUNK).start())

        @pl.when(lat == n_lat - 1)
        def _():
            for r in range(n_lat):
                for c in range(out_sem.shape[1]):
                    result_copy(r, c).wait()


def _fourier_body(x_ref, o_ref, mod, nw_ref, win_ref, wout_ref, m1_ref, cs_ref, ss_ref,
                  ta_scr, tb_scr, tr_scr, z_scr, rows_done=None):
    gate = mod[:, 2 * D_MODEL:]
    nw = nw_ref[...]
    n_seq, seq, _ = x_ref.shape
    n_chunks = seq // ROW_CHUNK
    lane = lax.broadcasted_iota(jnp.int32, (ROW_CHUNK, HALF_W), 1)
    for i in range(n_seq):
        for c in range(n_chunks):
            rows = slice(c * ROW_CHUNK, (c + 1) * ROW_CHUNK)
            srows = slice(i * seq + c * ROW_CHUNK, i * seq + (c + 1) * ROW_CHUNK)
            h = _mod_norm(x_ref[i, rows, :], nw, mod).astype(BF16)
            uz = _dot(h, win_ref[...])
            z_scr[srows, :] = uz[:, BRANCH:]
            u = uz[:, :BRANCH].astype(BF16)
            tr = jnp.zeros((ROW_CHUNK, HALF_W), F32)
            for g in range(N_GROUPS):
                t = _dot(u[:, g * GROUP_W:(g + 1) * GROUP_W], m1_ref[...])
                half = slice(g * HALF_W, (g + 1) * HALF_W)
                ta_scr[srows, half] = t[:, :HALF_W].astype(BF16)
                tb = t[:, HALF_W:]
                tb_scr[srows, half] = tb.astype(BF16)
                tr = jnp.where(lane == g, tb if g == 0 else pltpu.roll(tb, g, axis=1), tr)
            tr_scr[srows, :] = tr.astype(BF16)
    for i in range(n_seq):
        seq_rows = slice(i * seq, (i + 1) * seq)
        for c in range(n_chunks):
            rows = slice(c * ROW_CHUNK, (c + 1) * ROW_CHUNK)
            srows = slice(i * seq + c * ROW_CHUNK, i * seq + (c + 1) * ROW_CHUNK)
            cs = cs_ref[rows, :]
            p = _dot(cs, ta_scr[seq_rows, :])
            q = _dot(ss_ref[rows, :], tb_scr[seq_rows, :])
            r = _dot(cs, tr_scr[seq_rows, :])
            parts = []
            for g in range(N_GROUPS):
                half = slice(g * HALF_W, (g + 1) * HALF_W)
                pg, qg = p[:, half], q[:, half]
                rg = r if g == 0 else pltpu.roll(r, HALF_W - g, axis=1)
                parts.append(jnp.where(lane == 0, pg, pg - qg))
                parts.append(jnp.where(lane == 0, rg, pg + qg))
            y = jnp.concatenate(parts, axis=1)
            y = (y * jax.nn.silu(z_scr[srows, :])).astype(BF16)
            o_ref[i, rows, :] = x_ref[i, rows, :] + gate * _dot(y, wout_ref[...])
            if rows_done is not None:
                rows_done(i, rows)


def _dft_tables(seq):
    c = np.arange(GROUP_W)[:, None]
    k = np.arange(HALF_W)[None, :]
    cos_lo = np.cos(2.0 * np.pi * ((c * k) % GROUP_W) / GROUP_W)
    sin_lo = np.sin(2.0 * np.pi * ((c * k) % GROUP_W) / GROUP_W)
    sin_lo[:, 0] = np.cos(np.pi * c[:, 0])
    m1 = np.concatenate([cos_lo, sin_lo], axis=1) / np.sqrt(GROUP_W)
    n = np.arange(seq)
    ang = 2.0 * np.pi * ((n[:, None] * n[None, :]) % seq) / seq
    cs = np.cos(ang) / np.sqrt(seq)
    ss = np.sin(ang) / np.sqrt(seq)
    return m1.astype(np.float32), cs.astype(np.float32), ss.astype(np.float32)


def _const_spec(shape):
    return pl.BlockSpec(shape, lambda b: (0,) * len(shape))


def _fourier_layer(x_ctx, x_lat, nw, this_layer, next_layer):
    nb_ctx, seq_ctx, _ = x_ctx.shape
    nb_lat, seq_lat, _ = x_lat.shape
    n_seq = CTX_SEQS_PER_STEP
    assert nb_ctx % n_seq == 0
    n_prep = D_MODEL // HALF_W
    n_ctx = nb_ctx // n_seq
    assert D_MODEL % (n_ctx * LANES) == 0 and n_seq * seq_ctx <= seq_lat
    rows = D_MODEL // n_ctx
    assert rows == HALF_W
    c_ctx, c, w_mod0, b_mod0, w_in0, w_out0 = this_layer
    _, _, w_mod1, b_mod1, w_in1, w_out1 = next_layer
    n_lat = c.shape[0]
    n_qkvz = 2 * BRANCH + 2 * KV_W
    m1, csc, ssc = (jnp.asarray(t).astype(BF16) for t in _dft_tables(seq_ctx))
    _, csl, ssl = (jnp.asarray(t).astype(BF16) for t in _dft_tables(seq_lat))
    pm, pmh = (jnp.asarray(t).astype(BF16) for t in _perm_matrices())
    cc = c_ctx.reshape(1, D_MODEL)

    def prep_step(i):
        return jnp.minimum(i, n_prep - 1)

    def ctx_step(i):
        return jnp.clip(i - n_prep, 0, n_ctx - 1)

    def chunk_specs(step_fn, w_in_cols):
        return [pl.BlockSpec((1, HALF_W), lambda i: (0, step_fn(i))),
                pl.BlockSpec((n_lat, HALF_W), lambda i: (0, step_fn(i))),
                pl.BlockSpec((HALF_W, 3 * D_MODEL), lambda i: (step_fn(i), 0)),
                _const_spec((1, 3 * D_MODEL)),
                pl.BlockSpec((HALF_W, w_in_cols), lambda i: (step_fn(i), 0)),
                pl.BlockSpec((HALF_W, D_MODEL), lambda i: (step_fn(i), 0))]

    in_specs = [
        pl.BlockSpec((n_seq, seq_ctx, D_MODEL), lambda i: (ctx_step(i), 0, 0)),
        pl.BlockSpec(memory_space=pl.ANY),
        _const_spec((1, D_MODEL)),
        _const_spec((GROUP_W, GROUP_W)),
        _const_spec((seq_ctx, seq_ctx)),
        _const_spec((seq_ctx, seq_ctx)),
        pl.BlockSpec(memory_space=pl.ANY),
        pl.BlockSpec(memory_space=pl.ANY),
    ] + chunk_specs(prep_step, 2 * BRANCH) + [
        _const_spec((GROUP_W, GROUP_W)),
        _const_spec((2, HALF_W, HALF_W)),
    ] + chunk_specs(ctx_step, n_qkvz)
    out_specs = [
        pl.BlockSpec((n_seq, seq_ctx, D_MODEL), lambda i: (ctx_step(i), 0, 0)),
        pl.BlockSpec(memory_space=pl.ANY),
        MOD_SPEC,
        pl.BlockSpec((BRANCH + 2 * KV_W, rows), lambda i: (0, ctx_step(i))),
        pl.BlockSpec((rows, BRANCH), lambda i: (ctx_step(i), 0)),
        pl.BlockSpec((rows, D_MODEL), lambda i: (ctx_step(i), 0)),
    ]
    out_shape = [
        jax.ShapeDtypeStruct(x_ctx.shape, F32),
        jax.ShapeDtypeStruct(x_lat.shape, F32),
        jax.ShapeDtypeStruct((MOD_ROWS, 3 * D_MODEL), F32),
        jax.ShapeDtypeStruct((BRANCH + 2 * KV_W, D_MODEL), BF16),
        jax.ShapeDtypeStruct((D_MODEL, BRANCH), BF16),
        jax.ShapeDtypeStruct((BRANCH, D_MODEL), BF16),
    ]
    return pl.pallas_call(
        functools.partial(_fourier_layer_kernel, n_prep_steps=n_prep, n_ctx_steps=n_ctx),
        grid=(n_prep + n_ctx + nb_lat,),
        in_specs=in_specs,
        out_specs=out_specs,
        out_shape=out_shape,
        scratch_shapes=[
            pltpu.VMEM((MOD_ROWS, 3 * D_MODEL), F32),
            pltpu.VMEM((D_MODEL, 2 * BRANCH), BF16),
            pltpu.VMEM((BRANCH, D_MODEL), BF16),
            pltpu.VMEM((seq_lat, N_GROUPS * HALF_W), BF16),
            pltpu.VMEM((seq_lat, N_GROUPS * HALF_W), BF16),
            pltpu.VMEM((seq_lat, HALF_W), BF16),
            pltpu.VMEM((seq_lat, BRANCH), F32),
            pltpu.VMEM((MOD_ROWS, HALF_W), F32),
            pltpu.VMEM(x_lat.shape, F32),
            pltpu.VMEM((seq_lat, seq_lat), BF16),
            pltpu.VMEM((seq_lat, seq_lat), BF16),
            pltpu.SemaphoreType.DMA((nb_lat + 2,)),
            pltpu.VMEM(x_lat.shape, F32),
            pltpu.SemaphoreType.DMA((nb_lat, seq_lat // ROW_CHUNK)),
        ],
        compiler_params=pltpu.CompilerParams(
            dimension_semantics=("arbitrary",), vmem_limit_bytes=FOURIER_VMEM_LIMIT),
        name="fourier_layer",
    )(x_ctx, x_lat, nw, m1, csc, ssc, csl, ssl,
      cc, c, w_mod0, b_mod0.reshape(1, 3 * D_MODEL), w_in0, w_out0, pm, pmh,
      cc, c, w_mod1, b_mod1.reshape(1, 3 * D_MODEL), w_in1, w_out1)


def _head_weight_tile(w_ref, n_tokens):
    row = jnp.broadcast_to(w_ref[...], (HEAD_DIM, HEAD_DIM))
    ii = lax.broadcasted_iota(jnp.int32, (HEAD_DIM, HEAD_DIM), 0)
    jj = lax.broadcasted_iota(jnp.int32, (HEAD_DIM, HEAD_DIM), 1)
    col = jnp.sum(jnp.where(ii == jj, row, 0.0), axis=1, keepdims=True)
    return jnp.broadcast_to(col, (HEAD_DIM, n_tokens))


def _head_rms(t, w):
    return (t * lax.rsqrt(jnp.mean(t * t, axis=0, keepdims=True) + EPS)) * w


def _rope_t(t, cos, sin):
    half = HEAD_DIM // 2
    x1, x2 = t[:half], t[half:]
    return jnp.concatenate([x1 * cos - x2 * sin, x1 * sin + x2 * cos], axis=0)


def _head_scores(qn, g, keys, biases):
    zeros = jnp.zeros_like(qn)
    qz = jnp.concatenate([qn, zeros] if g % 2 == 0 else [zeros, qn], axis=0)
    blk = slice((g // 2) * LANES, (g // 2 + 1) * LANES)
    scores = []
    smax = None
    for k, bias in zip(keys, biases):
        s = _dot(k[:, blk], qz)
        if bias is not None:
            s = jnp.concatenate(
                [s[c * BLOCK:(c + 1) * BLOCK] if b is None else
                 s[c * BLOCK:(c + 1) * BLOCK] + jnp.concatenate([b] * (s.shape[1] // b.shape[1]), axis=1)
                 for c, b in enumerate(bias)], axis=0)
        cmax = jnp.max(s, axis=0, keepdims=True)
        smax = cmax if smax is None else jnp.maximum(smax, cmax)
        scores.append(s)
    return scores, smax


def _with_ones_rows(vt):
    return jnp.concatenate([vt, jnp.ones((ONES_ROWS, vt.shape[1]), vt.dtype)], axis=0)


def _head_probs(scored, sink2):
    scores, smax = scored
    m = jnp.maximum(smax, sink2)
    return jnp.concatenate([jnp.exp2(s - m).astype(BF16) for s in scores], axis=0), m


def _head_pv(probs, values_t, sink2):
    p, m = probs
    acc = _dot(values_t, p)
    den = acc[HEAD_DIM:HEAD_DIM + 1] + jnp.exp2(sink2 - m)
    return acc[:HEAD_DIM] * (1.0 / den)


def _attend_heads(n_units, stage, scores_fn, probs_fn, pv_fn, fillers=()):
    n_stages = n_units // stage
    pending = [scores_fn(u) for u in range(stage)]
    for g in range(n_stages):
        units = range(g * stage, (g + 1) * stage)
        if g < len(fillers):
            fillers[g]()
        nxt = [scores_fn(u) for u in range((g + 1) * stage, (g + 2) * stage)] if g + 1 < n_stages else None
        probs = [probs_fn(u, sc) for u, sc in zip(units, pending)]
        for u, pr in zip(units, probs):
            pv_fn(u, pr)
        pending = nxt


def _gate_out(x, o, z, gate, wout):
    y = (o * jax.nn.silu(z)).astype(BF16)
    return x + gate * _dot(y, wout)


def _attn_ctx_body(sink_ref, x_ref, mod, nw_ref, wqkvt_ref, wz_ref, wout_ref, qw_ref, kw_ref,
                   o_ref, kto_ref, vto_ref, qkvt_scr, z_scr, ot_scr):
    seq = x_ref.shape[1]
    n_seq = x_ref.shape[0]
    gate = mod[:, 2 * D_MODEL:]
    kw = _head_weight_tile(kw_ref, seq)
    qw = _head_weight_tile(qw_ref, seq)

    def project(i):
        h = _mod_norm(x_ref[i], nw_ref[...], mod).astype(BF16)
        qkvt_scr[i] = _dot_nt(wqkvt_ref[...], h)
        z_scr[i * seq:(i + 1) * seq, :] = _dot(h, wz_ref[...])

    def keys_values(i):
        knt = jnp.concatenate(
            [_head_rms(qkvt_scr[i, BRANCH + g * HEAD_DIM:BRANCH + (g + 1) * HEAD_DIM, :], kw)
             for g in range(N_KV)], axis=0)
        kto_ref[i] = knt
        vtf = qkvt_scr[i, BRANCH + KV_W:, :]
        vto_ref[i] = vtf
        vt = vtf.astype(BF16)
        return (knt.T.astype(BF16),
                [_with_ones_rows(vt[g * HEAD_DIM:(g + 1) * HEAD_DIM]) for g in range(N_KV)])

    def output(i):
        o_ref[i] = _gate_out(x_ref[i], ot_scr[i].T, z_scr[i * seq:(i + 1) * seq, :], gate, wout_ref[...])

    for i in range(n_seq):
        project(i)
    kv = [keys_values(i) for i in range(n_seq)]

    def scores_fn(u):
        i, hd = divmod(u, N_HEADS)
        t = qkvt_scr[i, hd * HEAD_DIM:(hd + 1) * HEAD_DIM, :]
        qn = (_head_rms(t, qw) * (HEAD_DIM ** -0.5 * LOG2E)).astype(BF16)
        return _head_scores(qn, hd // GQA, [kv[i][0]], [None])

    def probs_fn(u, sc):
        return _head_probs(sc, sink_ref[u % N_HEADS] * LOG2E)

    def pv_fn(u, pr):
        i, hd = divmod(u, N_HEADS)
        ot_scr[i, hd * HEAD_DIM:(hd + 1) * HEAD_DIM, :] = _head_pv(
            pr, kv[i][1][hd // GQA], sink_ref[hd] * LOG2E)

    fillers = [lambda: None] + [functools.partial(output, i) for i in range(n_seq - 1)]
    _attend_heads(n_seq * N_HEADS, N_HEADS, scores_fn, probs_fn, pv_fn, fillers)
    output(n_seq - 1)


def _attn_lat_body(sink_ref, x_ref, mod, nw_ref, wqkvt_ref, wz_ref, wout_ref, qw_ref, kw_ref,
                   cos_ref, sin_ref, ck_ref, cv_ref, o_ref,
                   q_scr, z_scr, ot_scr, k_scr, vt_scr, block_done=None):
    seq = x_ref.shape[1]
    gate = mod[:, 2 * D_MODEL:]
    nw = nw_ref[...]
    qw = _head_weight_tile(qw_ref, Q_BLOCK)
    kw = _head_weight_tile(kw_ref, Q_BLOCK)
    n_blocks = seq // Q_BLOCK
    kv_blocks = seq // BLOCK
    k_scr[0:BLOCK, :] = jnp.zeros((BLOCK, KV_W), BF16)
    k_scr[BLOCK + seq:2 * BLOCK + seq, :] = jnp.zeros((BLOCK, KV_W), BF16)
    vt_scr[0] = jnp.zeros((KV_W, BLOCK), BF16)
    vt_scr[kv_blocks + 1] = jnp.zeros((KV_W, BLOCK), BF16)

    def project(c):
        rows = slice(c * Q_BLOCK, (c + 1) * Q_BLOCK)
        h = _mod_norm(x_ref[0, rows, :], nw, mod).astype(BF16)
        z_scr[rows, :] = _dot(h, wz_ref[...])
        qkvt = _dot_nt(wqkvt_ref[...], h)
        cos = cos_ref[:, rows]
        sin = sin_ref[:, rows]
        for hd in range(N_HEADS):
            hr = slice(hd * HEAD_DIM, (hd + 1) * HEAD_DIM)
            t = _rope_t(_head_rms(qkvt[hr], qw), cos, sin)
            t = (t * (HEAD_DIM ** -0.5 * LOG2E)).astype(BF16)
            for j in range(HALVES):
                q_scr[c * HALVES + j, hd // 2, :, (hd % 2) * BLOCK:(hd % 2 + 1) * BLOCK] = (
                    t[:, j * BLOCK:(j + 1) * BLOCK])
        knt = jnp.concatenate(
            [_rope_t(_head_rms(qkvt[BRANCH + g * HEAD_DIM:BRANCH + (g + 1) * HEAD_DIM], kw), cos, sin)
             for g in range(N_KV)], axis=0)
        k_scr[BLOCK + c * Q_BLOCK:BLOCK + (c + 1) * Q_BLOCK, :] = knt.T.astype(BF16)
        vt = qkvt[BRANCH + KV_W:].astype(BF16)
        for j in range(HALVES):
            vt_scr[1 + c * HALVES + j] = vt[:, j * BLOCK:(j + 1) * BLOCK]

    ckb = ck_ref[0].T.astype(BF16)
    cvt = cv_ref[0].astype(BF16)

    win_len = 3 * BLOCK
    n_pairs = N_HEADS // 2
    kj = lax.broadcasted_iota(jnp.int32, (BLOCK, BLOCK), 0)
    qi = lax.broadcasted_iota(jnp.int32, (BLOCK, BLOCK), 1)
    first_head = lax.broadcasted_iota(jnp.int32, (1, 2 * BLOCK), 1) < BLOCK
    assert WINDOW >= BLOCK - 1

    def band_bias(c):
        rel = kj + (c - 1) * BLOCK - qi
        return jnp.where((rel >= -WINDOW) & (rel <= WINDOW), 0.0, NEG_INF)

    band_before, band_after = band_bias(0), band_bias(2)

    def half_operands(hb):
        r0 = pl.multiple_of(hb * BLOCK, BLOCK)
        valid = (jnp.where(hb > 0, band_before, NEG_INF), None,
                 jnp.where(hb < kv_blocks - 1, band_after, NEG_INF))
        kwin = k_scr[pl.ds(r0, win_len), :]
        vall = jnp.concatenate([vt_scr[hb + j] for j in range(win_len // BLOCK)] + [cvt], axis=1)
        valls = [_with_ones_rows(vall[g * HEAD_DIM:(g + 1) * HEAD_DIM]) for g in range(N_KV)]
        return valid, kwin, valls

    def pair_sink2(pair):
        return jnp.where(first_head, sink_ref[2 * pair], sink_ref[2 * pair + 1]) * LOG2E

    def attend(n, carry):
        halves = [half_operands(n * HALVES + j) for j in range(HALVES)]

        def scores_fn(u):
            j, pair = divmod(u, n_pairs)
            valid, kwin, _ = halves[j]
            return _head_scores(q_scr[n * HALVES + j, pair], pair // (GQA // 2), [kwin, ckb], [valid, None])

        def probs_fn(u, sc):
            return _head_probs(sc, pair_sink2(u % n_pairs))

        def pv_fn(u, pr):
            j, pair = divmod(u, n_pairs)
            o = _head_pv(pr, halves[j][2][pair // (GQA // 2)], pair_sink2(pair))
            for i in range(2):
                hd = 2 * pair + i
                ot_scr[n, hd * HEAD_DIM:(hd + 1) * HEAD_DIM, j * BLOCK:(j + 1) * BLOCK] = (
                    o[:, i * BLOCK:(i + 1) * BLOCK])

        _attend_heads(HALVES * n_pairs, GQA // 2, scores_fn, probs_fn, pv_fn)
        return carry

    for c in range(n_blocks):
        project(c)
    lax.fori_loop(0, n_blocks, attend, 0)
    for c in range(n_blocks):
        rows = slice(c * Q_BLOCK, (c + 1) * Q_BLOCK)
        o_ref[0, rows, :] = _gate_out(x_ref[0, rows, :], ot_scr[c].T, z_scr[rows, :], gate, wout_ref[...])
        if block_done is not None:
            block_done(c)


def _rope_tables_t(seq):
    pos = np.arange(seq)
    n_freq = HEAD_DIM // 4
    inv = ROPE_THETA ** (-np.arange(n_freq, dtype=np.float64) / n_freq)
    ang = np.concatenate([(pos // GRID_W)[:, None] * inv, (pos % GRID_W)[:, None] * inv], axis=-1)
    return np.cos(ang).T.astype(np.float32), np.sin(ang).T.astype(np.float32)


def _attn_layer_kernel(sink_ref, xc_ref, xl_ref, mod_ref, nw_ref, wqkvt_ref, wz_ref, wout_ref, qw_ref, kw_ref,
                       cos_ref, sin_ref, ck_ref, cv_ref, oc_ref, kto_ref, vto_ref, ol_hbm,
                       qkvt_scr, q_scr, z_scr, ot_scr, k_scr, vt_scr, ol_scr, out_sem, *, n_ctx_steps):
    step = pl.program_id(0)
    n_lat = ol_scr.shape[0]
    lat = step - n_ctx_steps

    def result_copy(r, c):
        rows = pl.ds(c * Q_BLOCK, Q_BLOCK)
        return pltpu.make_async_copy(ol_scr.at[r, rows], ol_hbm.at[r, rows], out_sem.at[r, c])

    @pl.when(step < n_ctx_steps)
    def _():
        _attn_ctx_body(sink_ref, xc_ref, mod_ref[0:1, :], nw_ref, wqkvt_ref, wz_ref, wout_ref, qw_ref, kw_ref,
                       oc_ref, kto_ref, vto_ref, qkvt_scr, z_scr, ot_scr)

    @pl.when(step >= n_ctx_steps)
    def _():
        mod = mod_ref[pl.ds(1 + lat, 1), :]
        _attn_lat_body(sink_ref, xl_ref, mod, nw_ref, wqkvt_ref, wz_ref, wout_ref, qw_ref, kw_ref,
                       cos_ref, sin_ref, ck_ref, cv_ref, ol_scr.at[pl.ds(lat, 1)],
                       q_scr, z_scr, ot_scr, k_scr, vt_scr,
                       block_done=lambda c: result_copy(lat, c).start())

        @pl.when(lat == n_lat - 1)
        def _():
            for r in range(n_lat):
                for c in range(out_sem.shape[1]):
                    result_copy(r, c).wait()


def _attn_layer(x_ctx, x_lat, mod, nw, wqkvt, wz, wout, qw, kw, sink, ckt, cvt):
    nb_ctx, seq_ctx, _ = x_ctx.shape
    nb_lat, seq_lat, _ = x_lat.shape
    past = ckt.shape[2]
    n_seq = CTX_SEQS_PER_STEP
    assert nb_ctx % n_seq == 0 and seq_ctx == Q_BLOCK and n_seq <= seq_lat // Q_BLOCK
    n_ctx = nb_ctx // n_seq
    cos, sin = (jnp.asarray(t) for t in _rope_tables_t(seq_lat))

    def ctx_step(i):
        return jnp.minimum(i, n_ctx - 1)

    def lat_step(i):
        return jnp.maximum(i - n_ctx, 0)

    return pl.pallas_call(
        functools.partial(_attn_layer_kernel, n_ctx_steps=n_ctx),
        grid=(n_ctx + nb_lat,),
        in_specs=[
            pl.BlockSpec(memory_space=pltpu.SMEM),
            pl.BlockSpec((n_seq, seq_ctx, D_MODEL), lambda i: (ctx_step(i), 0, 0)),
            pl.BlockSpec((1, seq_lat, D_MODEL), lambda i: (lat_step(i), 0, 0)),
            MOD_SPEC,
            _const_spec((1, D_MODEL)),
            _const_spec((BRANCH + 2 * KV_W, D_MODEL)),
            _const_spec((D_MODEL, BRANCH)),
            _const_spec((BRANCH, D_MODEL)),
            _const_spec((1, HEAD_DIM)),
            _const_spec((1, HEAD_DIM)),
            _const_spec((HEAD_DIM // 2, seq_lat)),
            _const_spec((HEAD_DIM // 2, seq_lat)),
            pl.BlockSpec((1, KV_W, past), lambda i: (lat_step(i), 0, 0)),
            pl.BlockSpec((1, KV_W, past), lambda i: (lat_step(i), 0, 0)),
        ],
        out_specs=[
            pl.BlockSpec((n_seq, seq_ctx, D_MODEL), lambda i: (ctx_step(i), 0, 0)),
            pl.BlockSpec((n_seq, KV_W, seq_ctx), lambda i: (ctx_step(i), 0, 0)),
            pl.BlockSpec((n_seq, KV_W, seq_ctx), lambda i: (ctx_step(i), 0, 0)),
            pl.BlockSpec(memory_space=pl.ANY),
        ],
        out_shape=[
            jax.ShapeDtypeStruct(x_ctx.shape, F32),
            jax.ShapeDtypeStruct((nb_ctx, KV_W, seq_ctx), F32),
            jax.ShapeDtypeStruct((nb_ctx, KV_W, seq_ctx), F32),
            jax.ShapeDtypeStruct(x_lat.shape, F32),
        ],
        scratch_shapes=[
            pltpu.VMEM((n_seq, BRANCH + 2 * KV_W, seq_ctx), F32),
            pltpu.VMEM((seq_lat // BLOCK, N_HEADS // 2, HEAD_DIM, 2 * BLOCK), BF16),
            pltpu.VMEM((seq_lat, BRANCH), F32),
            pltpu.VMEM((seq_lat // Q_BLOCK, BRANCH, Q_BLOCK), F32),
            pltpu.VMEM((seq_lat + 2 * BLOCK, KV_W), BF16),
            pltpu.VMEM((seq_lat // BLOCK + 2, KV_W, BLOCK), BF16),
            pltpu.VMEM(x_lat.shape, F32),
            pltpu.SemaphoreType.DMA((nb_lat, seq_lat // Q_BLOCK)),
        ],
        compiler_params=pltpu.CompilerParams(
            dimension_semantics=("arbitrary",), vmem_limit_bytes=VMEM_LIMIT),
        name="attn_layer",
    )(sink, x_ctx, x_lat, mod, nw, wqkvt, wz, wout, qw, kw, cos, sin, ckt, cvt)


def kernel(x_prompt, x_sample, cache_k_l1, cache_v_l1, c, c_ctx, norm_w_l0, w_mod_l0, b_mod_l0,
           w_in_l0, w_out_l0, norm_w_l1, w_mod_l1, b_mod_l1, w_in_l1, q_norm_w_l1, k_norm_w_l1,
           sink_l1, w_out_l1):
    nb_ctx, seq_ctx, _ = x_prompt.shape
    nb_lat = x_sample.shape[0]
    past = cache_k_l1.shape[1]
    assert 1 + nb_lat <= MOD_ROWS
    nw0 = norm_w_l0.reshape(1, D_MODEL)
    nw1 = norm_w_l1.reshape(1, D_MODEL)
    qw = q_norm_w_l1.reshape(1, HEAD_DIM)
    kw = k_norm_w_l1.reshape(1, HEAD_DIM)

    xp, xs, mod1, wqkvt1, wz1, wout1 = _fourier_layer(
        x_prompt, x_sample, nw0,
        this_layer=(c_ctx, c, w_mod_l0, b_mod_l0, w_in_l0, w_out_l0),
        next_layer=(c_ctx, c, w_mod_l1, b_mod_l1, w_in_l1, w_out_l1))

    def to_feature_major(t):
        return jnp.transpose(t, (0, 2, 3, 1)).reshape(t.shape[0], KV_W, t.shape[1])

    def from_feature_major(t):
        return jnp.transpose(t.reshape(t.shape[0], N_KV, HEAD_DIM, t.shape[2]), (0, 3, 1, 2))

    xp, new_kt, new_vt, xs = _attn_layer(xp, xs, mod1, nw1, wqkvt1, wz1, wout1, qw, kw, sink_l1,
                                         to_feature_major(cache_k_l1), to_feature_major(cache_v_l1))
    return (xp, xs, from_feature_major(new_kt), from_feature_major(new_vt))
```

```python
import functools

import numpy as np
import jax
import jax.numpy as jnp
from jax import lax
from jax.experimental import pallas as pl
from jax.experimental.pallas import tpu as pltpu

D_MODEL = 1024
BRANCH = 1024
N_GROUPS = 4
GROUP_W = BRANCH // N_GROUPS
HALF_W = GROUP_W // 2
HEAD_DIM = 64
N_HEADS = 16
N_KV = 4
GQA = N_HEADS // N_KV
KV_W = N_KV * HEAD_DIM
GRID_W = 64
WINDOW = 128
BLOCK = 128
ROPE_THETA = 10000.0
EPS = 1e-6
NEG_INF = -1e30
LANES = 128
ROW_CHUNK = 256
Q_BLOCK = 256
HALVES = Q_BLOCK // BLOCK
CTX_SEQS_PER_STEP = 2
VMEM_LIMIT = 56 * 1024 * 1024
FOURIER_VMEM_LIMIT = 60 * 1024 * 1024
MOD_ROWS = 8
ONES_ROWS = 16
LOG2E = float(np.log2(np.e))

F32 = jnp.float32
BF16 = jnp.bfloat16


def _dot(a, b):
    return jnp.dot(a, b, preferred_element_type=F32)


def _dot_nt(a, b):
    return lax.dot_general(a, b, (((1,), (1,)), ((), ())), preferred_element_type=F32)


MOD_SPEC = pl.BlockSpec((MOD_ROWS, 3 * D_MODEL), lambda b: (0, 0))


def _mod_norm(x, nw, mod):
    shift = mod[:, :D_MODEL]
    scale = mod[:, D_MODEL:2 * D_MODEL]
    y = x * lax.rsqrt(jnp.mean(x * x, axis=-1, keepdims=True) + EPS)
    return (y * nw) * (1.0 + scale) + shift


def _mod_accumulate(cctx_ref, c_ref, w_ref, b_ref, o_ref, cond_scr, first_step):
    n_lat = c_ref.shape[0]
    cond_scr[...] = jnp.zeros_like(cond_scr)
    cond_scr[0:1, :] = cctx_ref[...]
    cond_scr[1:1 + n_lat, :] = c_ref[...]
    s = jax.nn.silu(cond_scr[...]).astype(BF16)

    @pl.when(pl.program_id(0) == first_step)
    def _():
        o_ref[...] = jnp.broadcast_to(b_ref[...], o_ref.shape)

    o_ref[...] += _dot(s, w_ref[...].astype(BF16))


def _mirror_perm():
    j = np.arange(GROUP_W)
    return np.where(j <= HALF_W, j, GROUP_W + HALF_W - j)


def _perm_matrices():
    perm = _mirror_perm()
    pm = (np.arange(GROUP_W)[:, None] == perm[None, :]).astype(np.float32)
    assert (pm[:HALF_W, HALF_W:] == 0).all() and (pm[HALF_W:, :HALF_W] == 0).all()
    return pm, np.stack([pm[:HALF_W, :HALF_W], pm[HALF_W:, HALF_W:]])


def _fourier_layer_kernel(xc_ref, xl_hbm, nw_ref, m1_ref, csc_ref, ssc_ref, csl_hbm, ssl_hbm,
                          cctx0_ref, c0_ref, wmod0_ref, bmod0_ref, win0_ref, wout0_ref, pm_ref, pmh_ref,
                          cctx_ref, c_ref, wmod1_ref, bmod1_ref, win1_ref, wout1_ref,
                          oc_ref, ol_hbm, mod1_ref, wqkvt1_ref, wz1_ref, wout1b_ref,
                          mod0_scr, win_scr, wout_scr, ta_scr, tb_scr, tr_scr, z_scr, cond_scr,
                          xl_scr, csl_scr, ssl_scr, copy_sem, ol_scr, out_sem,
                          *, n_prep_steps, n_ctx_steps):
    step = pl.program_id(0)
    n_lat = xl_scr.shape[0]
    lat = step - (n_prep_steps + n_ctx_steps)

    def request_copy(r):
        return pltpu.make_async_copy(xl_hbm.at[r], xl_scr.at[r], copy_sem.at[r])

    table_copies = [pltpu.make_async_copy(csl_hbm, csl_scr, copy_sem.at[n_lat]),
                    pltpu.make_async_copy(ssl_hbm, ssl_scr, copy_sem.at[n_lat + 1])]

    def result_copy(r, c):
        rows = pl.ds(c * ROW_CHUNK, ROW_CHUNK)
        return pltpu.make_async_copy(ol_scr.at[r, rows], ol_hbm.at[r, rows], out_sem.at[r, c])

    @pl.when(step == n_prep_steps)
    def _():
        for copy in table_copies + [request_copy(r) for r in range(n_lat)]:
            copy.start()

    @pl.when(step < n_prep_steps)
    def _():
        _mod_accumulate(cctx0_ref, c0_ref, wmod0_ref, bmod0_ref, mod0_scr, cond_scr, 0)
        rows = pl.ds(pl.multiple_of(step * HALF_W, HALF_W), HALF_W)
        w = win0_ref[...].astype(BF16)
        win_scr[rows, :BRANCH] = w[:, :BRANCH]
        pm = pm_ref[...]
        for g in range(N_GROUPS):
            cols = slice(BRANCH + g * GROUP_W, BRANCH + (g + 1) * GROUP_W)
            win_scr[rows, cols] = _dot(w[:, cols], pm).astype(BF16)
        wout_scr[rows, :] = _dot(pmh_ref[step % 2], wout0_ref[...].astype(BF16)).astype(BF16)

    @pl.when(jnp.logical_and(step >= n_prep_steps, step < n_prep_steps + n_ctx_steps))
    def _():
        _mod_accumulate(cctx_ref, c_ref, wmod1_ref, bmod1_ref, mod1_ref, cond_scr, n_prep_steps)
        w = win1_ref[...]
        wqkvt1_ref[...] = w[:, :BRANCH + 2 * KV_W].T.astype(BF16)
        wz1_ref[...] = w[:, BRANCH + 2 * KV_W:].astype(BF16)
        wout1b_ref[...] = wout1_ref[...].astype(BF16)
        _fourier_body(xc_ref, oc_ref, mod0_scr[0:1, :], nw_ref, win_scr, wout_scr, m1_ref, csc_ref, ssc_ref,
                      ta_scr, tb_scr, tr_scr, z_scr)

    @pl.when(step >= n_prep_steps + n_ctx_steps)
    def _():
        @pl.when(lat == 0)
        def _():
            for copy in table_copies:
                copy.wait()

        request_copy(lat).wait()
        mod = mod0_scr[pl.ds(1 + lat, 1), :]
        _fourier_body(xl_scr.at[pl.ds(lat, 1)], ol_scr.at[pl.ds(lat, 1)], mod, nw_ref, win_scr, wout_scr,
                      m1_ref, csl_scr, ssl_scr, ta_scr, tb_scr, tr_scr, z_scr,
                      rows_done=lambda _, rows: result_copy(lat, rows.start // ROW_CHUNK).start())

        @pl.when(lat == n_lat - 1)
        def _():
            for r in range(n_lat):
                for c in range(out_sem.shape[1]):
                    result_copy(r, c).wait()


def _fourier_body(x_ref, o_ref, mod, nw_ref, win_ref, wout_ref, m1_ref, cs_ref, ss_ref,
                  ta_scr, tb_scr, tr_scr, z_scr, rows_done=None):
    gate = mod[:, 2 * D_MODEL:]
    nw = nw_ref[...]
    n_seq, seq, _ = x_ref.shape
    n_chunks = seq // ROW_CHUNK
    lane = lax.broadcasted_iota(jnp.int32, (ROW_CHUNK, HALF_W), 1)
    for i in range(n_seq):
        for c in range(n_chunks):
            rows = slice(c * ROW_CHUNK, (c + 1) * ROW_CHUNK)
            srows = slice(i * seq + c * ROW_CHUNK, i * seq + (c + 1) * ROW_CHUNK)
            h = _mod_norm(x_ref[i, rows, :], nw, mod).astype(BF16)
            uz = _dot(h, win_ref[...])
            z_scr[srows, :] = uz[:, BRANCH:]
            u = uz[:, :BRANCH].astype(BF16)
            tr = jnp.zeros((ROW_CHUNK, HALF_W), F32)
            for g in range(N_GROUPS):
                t = _dot(u[:, g * GROUP_W:(g + 1) * GROUP_W], m1_ref[...])
                half = slice(g * HALF_W, (g + 1) * HALF_W)
                ta_scr[srows, half] = t[:, :HALF_W].astype(BF16)
                tb = t[:, HALF_W:]
                tb_scr[srows, half] = tb.astype(BF16)
                tr = jnp.where(lane == g, tb if g == 0 else pltpu.roll(tb, g, axis=1), tr)
            tr_scr[srows, :] = tr.astype(BF16)
    for i in range(n_seq):
        seq_rows = slice(i * seq, (i + 1) * seq)
        for c in range(n_chunks):
            rows = slice(c * ROW_CHUNK, (c + 1) * ROW_CHUNK)
            srows = slice(i * seq + c * ROW_CHUNK, i * seq + (c + 1) * ROW_CHUNK)
            cs = cs_ref[rows, :]
            p = _dot(cs, ta_scr[seq_rows, :])
            q = _dot(ss_ref[rows, :], tb_scr[seq_rows, :])
            r = _dot(cs, tr_scr[seq_rows, :])
            parts = []
            for g in range(N_GROUPS):
                half = slice(g * HALF_W, (g + 1) * HALF_W)
                pg, qg = p[:, half], q[:, half]
                rg = r if g == 0 else pltpu.roll(r, HALF_W - g, axis=1)
                parts.append(jnp.where(lane == 0, pg, pg - qg))
                parts.append(jnp.where(lane == 0, rg, pg + qg))
            y = jnp.concatenate(parts, axis=1)
            y = (y * jax.nn.silu(z_scr[srows, :])).astype(BF16)
            o_ref[i, rows, :] = x_ref[i, rows, :] + gate * _dot(y, wout_ref[...])
            if rows_done is not None:
                rows_done(i, rows)


def _dft_tables(seq):
    c = np.arange(GROUP_W)[:, None]
    k = np.arange(HALF_W)[None, :]
    cos_lo = np.cos(2.0 * np.pi * ((c * k) % GROUP_W) / GROUP_W)
    sin_lo = np.sin(2.0 * np.pi * ((c * k) % GROUP_W) / GROUP_W)
    sin_lo[:, 0] = np.cos(np.pi * c[:, 0])
    m1 = np.concatenate([cos_lo, sin_lo], axis=1) / np.sqrt(GROUP_W)
    n = np.arange(seq)
    ang = 2.0 * np.pi * ((n[:, None] * n[None, :]) % seq) / seq
    cs = np.cos(ang) / np.sqrt(seq)
    ss = np.sin(ang) / np.sqrt(seq)
    return m1.astype(np.float32), cs.astype(np.float32), ss.astype(np.float32)


def _const_spec(shape):
    return pl.BlockSpec(shape, lambda b: (0,) * len(shape))


def _fourier_layer(x_ctx, x_lat, nw, this_layer, next_layer):
    nb_ctx, seq_ctx, _ = x_ctx.shape
    nb_lat, seq_lat, _ = x_lat.shape
    n_seq = CTX_SEQS_PER_STEP
    assert nb_ctx % n_seq == 0
    n_prep = D_MODEL // HALF_W
    n_ctx = nb_ctx // n_seq
    assert D_MODEL % (n_ctx * LANES) == 0 and n_seq * seq_ctx <= seq_lat
    rows = D_MODEL // n_ctx
    assert rows == HALF_W
    c_ctx, c, w_mod0, b_mod0, w_in0, w_out0 = this_layer
    _, _, w_mod1, b_mod1, w_in1, w_out1 = next_layer
    n_lat = c.shape[0]
    n_qkvz = 2 * BRANCH + 2 * KV_W
    m1, csc, ssc = (jnp.asarray(t).astype(BF16) for t in _dft_tables(seq_ctx))
    _, csl, ssl = (jnp.asarray(t).astype(BF16) for t in _dft_tables(seq_lat))
    pm, pmh = (jnp.asarray(t).astype(BF16) for t in _perm_matrices())
    cc = c_ctx.reshape(1, D_MODEL)

    def prep_step(i):
        return jnp.minimum(i, n_prep - 1)

    def ctx_step(i):
        return jnp.clip(i - n_prep, 0, n_ctx - 1)

    def chunk_specs(step_fn, w_in_cols):
        return [pl.BlockSpec((1, HALF_W), lambda i: (0, step_fn(i))),
                pl.BlockSpec((n_lat, HALF_W), lambda i: (0, step_fn(i))),
                pl.BlockSpec((HALF_W, 3 * D_MODEL), lambda i: (step_fn(i), 0)),
                _const_spec((1, 3 * D_MODEL)),
                pl.BlockSpec((HALF_W, w_in_cols), lambda i: (step_fn(i), 0)),
                pl.BlockSpec((HALF_W, D_MODEL), lambda i: (step_fn(i), 0))]

    in_specs = [
        pl.BlockSpec((n_seq, seq_ctx, D_MODEL), lambda i: (ctx_step(i), 0, 0)),
        pl.BlockSpec(memory_space=pl.ANY),
        _const_spec((1, D_MODEL)),
        _const_spec((GROUP_W, GROUP_W)),
        _const_spec((seq_ctx, seq_ctx)),
        _const_spec((seq_ctx, seq_ctx)),
        pl.BlockSpec(memory_space=pl.ANY),
        pl.BlockSpec(memory_space=pl.ANY),
    ] + chunk_specs(prep_step, 2 * BRANCH) + [
        _const_spec((GROUP_W, GROUP_W)),
        _const_spec((2, HALF_W, HALF_W)),
    ] + chunk_specs(ctx_step, n_qkvz)
    out_specs = [
        pl.BlockSpec((n_seq, seq_ctx, D_MODEL), lambda i: (ctx_step(i), 0, 0)),
        pl.BlockSpec(memory_space=pl.ANY),
        MOD_SPEC,
        pl.BlockSpec((BRANCH + 2 * KV_W, rows), lambda i: (0, ctx_step(i))),
        pl.BlockSpec((rows, BRANCH), lambda i: (ctx_step(i), 0)),
        pl.BlockSpec((rows, D_MODEL), lambda i: (ctx_step(i), 0)),
    ]
    out_shape = [
        jax.ShapeDtypeStruct(x_ctx.shape, F32),
        jax.ShapeDtypeStruct(x_lat.shape, F32),
        jax.ShapeDtypeStruct((MOD_ROWS, 3 * D_MODEL), F32),
        jax.ShapeDtypeStruct((BRANCH + 2 * KV_W, D_MODEL), BF16),
        jax.ShapeDtypeStruct((D_MODEL, BRANCH), BF16),
        jax.ShapeDtypeStruct((BRANCH, D_MODEL), BF16),
    ]
    return pl.pallas_call(
        functools.partial(_fourier_layer_kernel, n_prep_steps=n_prep, n_ctx_steps=n_ctx),
        grid=(n_prep + n_ctx + nb_lat,),
        in_specs=in_specs,
        out_specs=out_specs,
        out_shape=out_shape,
        scratch_shapes=[
            pltpu.VMEM((MOD_ROWS, 3 * D_MODEL), F32),
            pltpu.VMEM((D_MODEL, 2 * BRANCH), BF16),
            pltpu.VMEM((BRANCH, D_MODEL), BF16),
            pltpu.VMEM((seq_lat, N_GROUPS * HALF_W), BF16),
            pltpu.VMEM((seq_lat, N_GROUPS * HALF_W), BF16),
            pltpu.VMEM((seq_lat, HALF_W), BF16),
            pltpu.VMEM((seq_lat, BRANCH), F32),
            pltpu.VMEM((MOD_ROWS, HALF_W), F32),
            pltpu.VMEM(x_lat.shape, F32),
            pltpu.VMEM((seq_lat, seq_lat), BF16),
            pltpu.VMEM((seq_lat, seq_lat), BF16),
            pltpu.SemaphoreType.DMA((nb_lat + 2,)),
            pltpu.VMEM(x_lat.shape, F32),
            pltpu.SemaphoreType.DMA((nb_lat, seq_lat // ROW_CHUNK)),
        ],
        compiler_params=pltpu.CompilerParams(
            dimension_semantics=("arbitrary",), vmem_limit_bytes=FOURIER_VMEM_LIMIT),
        name="fourier_layer",
    )(x_ctx, x_lat, nw, m1, csc, ssc, csl, ssl,
      cc, c, w_mod0, b_mod0.reshape(1, 3 * D_MODEL), w_in0, w_out0, pm, pmh,
      cc, c, w_mod1, b_mod1.reshape(1, 3 * D_MODEL), w_in1, w_out1)


def _head_weight_tile(w_ref, n_tokens):
    row = jnp.broadcast_to(w_ref[...], (HEAD_DIM, HEAD_DIM))
    ii = lax.broadcasted_iota(jnp.int32, (HEAD_DIM, HEAD_DIM), 0)
    jj = lax.broadcasted_iota(jnp.int32, (HEAD_DIM, HEAD_DIM), 1)
    col = jnp.sum(jnp.where(ii == jj, row, 0.0), axis=1, keepdims=True)
    return jnp.broadcast_to(col, (HEAD_DIM, n_tokens))


def _head_rms(t, w):
    return (t * lax.rsqrt(jnp.mean(t * t, axis=0, keepdims=True) + EPS)) * w


def _rope_t(t, cos, sin):
    half = HEAD_DIM // 2
    x1, x2 = t[:half], t[half:]
    return jnp.concatenate([x1 * cos - x2 * sin, x1 * sin + x2 * cos], axis=0)


def _head_scores(qn, g, keys, biases):
    zeros = jnp.zeros_like(qn)
    qz = jnp.concatenate([qn, zeros] if g % 2 == 0 else [zeros, qn], axis=0)
    blk = slice((g // 2) * LANES, (g // 2 + 1) * LANES)
    scores = []
    smax = None
    for k, bias in zip(keys, biases):
        s = _dot(k[:, blk], qz)
        if bias is not None:
            s = jnp.concatenate(
                [s[c * BLOCK:(c + 1) * BLOCK] if b is None else
                 s[c * BLOCK:(c + 1) * BLOCK] + jnp.concatenate([b] * (s.shape[1] // b.shape[1]), axis=1)
                 for c, b in enumerate(bias)], axis=0)
        cmax = jnp.max(s, axis=0, keepdims=True)
        smax = cmax if smax is None else jnp.maximum(smax, cmax)
        scores.append(s)
    return scores, smax


def _with_ones_rows(vt):
    return jnp.concatenate([vt, jnp.ones((ONES_ROWS, vt.shape[1]), vt.dtype)], axis=0)


def _head_probs(scored, sink2):
    scores, smax = scored
    m = jnp.maximum(smax, sink2)
    return jnp.concatenate([jnp.exp2(s - m).astype(BF16) for s in scores], axis=0), m


def _head_pv(probs, values_t, sink2):
    p, m = probs
    acc = _dot(values_t, p)
    den = acc[HEAD_DIM:HEAD_DIM + 1] + jnp.exp2(sink2 - m)
    return acc[:HEAD_DIM] * (1.0 / den)


def _attend_heads(n_units, stage, scores_fn, probs_fn, pv_fn, fillers=()):
    n_stages = n_units // stage
    pending = [scores_fn(u) for u in range(stage)]
    for g in range(n_stages):
        units = range(g * stage, (g + 1) * stage)
        if g < len(fillers):
            fillers[g]()
        nxt = [scores_fn(u) for u in range((g + 1) * stage, (g + 2) * stage)] if g + 1 < n_stages else None
        probs = [probs_fn(u, sc) for u, sc in zip(units, pending)]
        for u, pr in zip(units, probs):
            pv_fn(u, pr)
        pending = nxt


def _gate_out(x, o, z, gate, wout):
    y = (o * jax.nn.silu(z)).astype(BF16)
    return x + gate * _dot(y, wout)


def _attn_ctx_body(sink_ref, x_ref, mod, nw_ref, wqkvt_ref, wz_ref, wout_ref, qw_ref, kw_ref,
                   o_ref, kto_ref, vto_ref, qkvt_scr, z_scr, ot_scr):
    seq = x_ref.shape[1]
    n_seq = x_ref.shape[0]
    gate = mod[:, 2 * D_MODEL:]
    kw = _head_weight_tile(kw_ref, seq)
    qw = _head_weight_tile(qw_ref, seq)

    def project(i):
        h = _mod_norm(x_ref[i], nw_ref[...], mod).astype(BF16)
        qkvt_scr[i] = _dot_nt(wqkvt_ref[...], h)
        z_scr[i * seq:(i + 1) * seq, :] = _dot(h, wz_ref[...])

    def keys_values(i):
        knt = jnp.concatenate(
            [_head_rms(qkvt_scr[i, BRANCH + g * HEAD_DIM:BRANCH + (g + 1) * HEAD_DIM, :], kw)
             for g in range(N_KV)], axis=0)
        kto_ref[i] = knt
        vtf = qkvt_scr[i, BRANCH + KV_W:, :]
        vto_ref[i] = vtf
        vt = vtf.astype(BF16)
        return (knt.T.astype(BF16),
                [_with_ones_rows(vt[g * HEAD_DIM:(g + 1) * HEAD_DIM]) for g in range(N_KV)])

    def output(i):
        o_ref[i] = _gate_out(x_ref[i], ot_scr[i].T, z_scr[i * seq:(i + 1) * seq, :], gate, wout_ref[...])

    for i in range(n_seq):
        project(i)
    kv = [keys_values(i) for i in range(n_seq)]

    def scores_fn(u):
        i, hd = divmod(u, N_HEADS)
        t = qkvt_scr[i, hd * HEAD_DIM:(hd + 1) * HEAD_DIM, :]
        qn = (_head_rms(t, qw) * (HEAD_DIM ** -0.5 * LOG2E)).astype(BF16)
        return _head_scores(qn, hd // GQA, [kv[i][0]], [None])

    def probs_fn(u, sc):
        return _head_probs(sc, sink_ref[u % N_HEADS] * LOG2E)

    def pv_fn(u, pr):
        i, hd = divmod(u, N_HEADS)
        ot_scr[i, hd * HEAD_DIM:(hd + 1) * HEAD_DIM, :] = _head_pv(
            pr, kv[i][1][hd // GQA], sink_ref[hd] * LOG2E)

    fillers = [lambda: None] + [functools.partial(output, i) for i in range(n_seq - 1)]
    _attend_heads(n_seq * N_HEADS, N_HEADS, scores_fn, probs_fn, pv_fn, fillers)
    output(n_seq - 1)


def _attn_lat_body(sink_ref, x_ref, mod, nw_ref, wqkvt_ref, wz_ref, wout_ref, qw_ref, kw_ref,
                   cos_ref, sin_ref, ck_ref, cv_ref, o_ref,
                   q_scr, z_scr, ot_scr, k_scr, vt_scr, block_done=None):
    seq = x_ref.shape[1]
    gate = mod[:, 2 * D_MODEL:]
    nw = nw_ref[...]
    qw = _head_weight_tile(qw_ref, Q_BLOCK)
    kw = _head_weight_tile(kw_ref, Q_BLOCK)
    n_blocks = seq // Q_BLOCK
    kv_blocks = seq // BLOCK
    k_scr[0:BLOCK, :] = jnp.zeros((BLOCK, KV_W), BF16)
    k_scr[BLOCK + seq:2 * BLOCK + seq, :] = jnp.zeros((BLOCK, KV_W), BF16)
    vt_scr[0] = jnp.zeros((KV_W, BLOCK), BF16)
    vt_scr[kv_blocks + 1] = jnp.zeros((KV_W, BLOCK), BF16)

    def project(c):
        rows = slice(c * Q_BLOCK, (c + 1) * Q_BLOCK)
        h = _mod_norm(x_ref[0, rows, :], nw, mod).astype(BF16)
        z_scr[rows, :] = _dot(h, wz_ref[...])
        qkvt = _dot_nt(wqkvt_ref[...], h)
        cos = cos_ref[:, rows]
        sin = sin_ref[:, rows]
        for hd in range(N_HEADS):
            hr = slice(hd * HEAD_DIM, (hd + 1) * HEAD_DIM)
            t = _rope_t(_head_rms(qkvt[hr], qw), cos, sin)
            t = (t * (HEAD_DIM ** -0.5 * LOG2E)).astype(BF16)
            for j in range(HALVES):
                q_scr[c * HALVES + j, hd // 2, :, (hd % 2) * BLOCK:(hd % 2 + 1) * BLOCK] = (
                    t[:, j * BLOCK:(j + 1) * BLOCK])
        knt = jnp.concatenate(
            [_rope_t(_head_rms(qkvt[BRANCH + g * HEAD_DIM:BRANCH + (g + 1) * HEAD_DIM], kw), cos, sin)
             for g in range(N_KV)], axis=0)
        k_scr[BLOCK + c * Q_BLOCK:BLOCK + (c + 1) * Q_BLOCK, :] = knt.T.astype(BF16)
        vt = qkvt[BRANCH + KV_W:].astype(BF16)
        for j in range(HALVES):
            vt_scr[1 + c * HALVES + j] = vt[:, j * BLOCK:(j + 1) * BLOCK]

    ckb = ck_ref[0].T.astype(BF16)
    cvt = cv_ref[0].astype(BF16)

    win_len = 3 * BLOCK
    n_pairs = N_HEADS // 2
    kj = lax.broadcasted_iota(jnp.int32, (BLOCK, BLOCK), 0)
    qi = lax.broadcasted_iota(jnp.int32, (BLOCK, BLOCK), 1)
    first_head = lax.broadcasted_iota(jnp.int32, (1, 2 * BLOCK), 1) < BLOCK
    assert WINDOW >= BLOCK - 1

    def band_bias(c):
        rel = kj + (c - 1) * BLOCK - qi
        return jnp.where((rel >= -WINDOW) & (rel <= WINDOW), 0.0, NEG_INF)

    band_before, band_after = band_bias(0), band_bias(2)

    def half_operands(hb):
        r0 = pl.multiple_of(hb * BLOCK, BLOCK)
        valid = (jnp.where(hb > 0, band_before, NEG_INF), None,
                 jnp.where(hb < kv_blocks - 1, band_after, NEG_INF))
        kwin = k_scr[pl.ds(r0, win_len), :]
        vall = jnp.concatenate([vt_scr[hb + j] for j in range(win_len // BLOCK)] + [cvt], axis=1)
        valls = [_with_ones_rows(vall[g * HEAD_DIM:(g + 1) * HEAD_DIM]) for g in range(N_KV)]
        return valid, kwin, valls

    def pair_sink2(pair):
        return jnp.where(first_head, sink_ref[2 * pair], sink_ref[2 * pair + 1]) * LOG2E

    def attend(n, carry):
        halves = [half_operands(n * HALVES + j) for j in range(HALVES)]

        def scores_fn(u):
            j, pair = divmod(u, n_pairs)
            valid, kwin, _ = halves[j]
            return _head_scores(q_scr[n * HALVES + j, pair], pair // (GQA // 2), [kwin, ckb], [valid, None])

        def probs_fn(u, sc):
            return _head_probs(sc, pair_sink2(u % n_pairs))

        def pv_fn(u, pr):
            j, pair = divmod(u, n_pairs)
            o = _head_pv(pr, halves[j][2][pair // (GQA // 2)], pair_sink2(pair))
            for i in range(2):
                hd = 2 * pair + i
                ot_scr[n, hd * HEAD_DIM:(hd + 1) * HEAD_DIM, j * BLOCK:(j + 1) * BLOCK] = (
                    o[:, i * BLOCK:(i + 1) * BLOCK])

        _attend_heads(HALVES * n_pairs, GQA // 2, scores_fn, probs_fn, pv_fn)
        return carry

    for c in range(n_blocks):
        project(c)
    def attend_all(n, carry):
        for i in range(n_blocks):
            carry = attend(n_blocks * n + i, carry)
        return carry

    lax.fori_loop(0, 1, attend_all, 0)
    for c in range(n_blocks):
        rows = slice(c * Q_BLOCK, (c + 1) * Q_BLOCK)
        o_ref[0, rows, :] = _gate_out(x_ref[0, rows, :], ot_scr[c].T, z_scr[rows, :], gate, wout_ref[...])
        if block_done is not None:
            block_done(c)


def _rope_tables_t(seq):
    pos = np.arange(seq)
    n_freq = HEAD_DIM // 4
    inv = ROPE_THETA ** (-np.arange(n_freq, dtype=np.float64) / n_freq)
    ang = np.concatenate([(pos // GRID_W)[:, None] * inv, (pos % GRID_W)[:, None] * inv], axis=-1)
    return np.cos(ang).T.astype(np.float32), np.sin(ang).T.astype(np.float32)


def _attn_layer_kernel(sink_ref, xc_ref, xl_ref, mod_ref, nw_ref, wqkvt_ref, wz_ref, wout_ref, qw_ref, kw_ref,
                       cos_ref, sin_ref, ck_ref, cv_ref, oc_ref, kto_ref, vto_ref, ol_hbm,
                       qkvt_scr, q_scr, z_scr, ot_scr, k_scr, vt_scr, ol_scr, out_sem, *, n_ctx_steps):
    step = pl.program_id(0)
    n_lat = ol_scr.shape[0]
    lat = step - n_ctx_steps

    def result_copy(r, c):
        rows = pl.ds(c * Q_BLOCK, Q_BLOCK)
        return pltpu.make_async_copy(ol_scr.at[r, rows], ol_hbm.at[r, rows], out_sem.at[r, c])

    @pl.when(step < n_ctx_steps)
    def _():
        _attn_ctx_body(sink_ref, xc_ref, mod_ref[0:1, :], nw_ref, wqkvt_ref, wz_ref, wout_ref, qw_ref, kw_ref,
                       oc_ref, kto_ref, vto_ref, qkvt_scr, z_scr, ot_scr)

    @pl.when(step >= n_ctx_steps)
    def _():
        mod = mod_ref[pl.ds(1 + lat, 1), :]
        _attn_lat_body(sink_ref, xl_ref, mod, nw_ref, wqkvt_ref, wz_ref, wout_ref, qw_ref, kw_ref,
                       cos_ref, sin_ref, ck_ref, cv_ref, ol_scr.at[pl.ds(lat, 1)],
                       q_scr, z_scr, ot_scr, k_scr, vt_scr,
                       block_done=lambda c: result_copy(lat, c).start())

        @pl.when(lat == n_lat - 1)
        def _():
            for r in range(n_lat):
                for c in range(out_sem.shape[1]):
                    result_copy(r, c).wait()


def _attn_layer(x_ctx, x_lat, mod, nw, wqkvt, wz, wout, qw, kw, sink, ckt, cvt):
    nb_ctx, seq_ctx, _ = x_ctx.shape
    nb_lat, seq_lat, _ = x_lat.shape
    past = ckt.shape[2]
    n_seq = CTX_SEQS_PER_STEP
    assert nb_ctx % n_seq == 0 and seq_ctx == Q_BLOCK and n_seq <= seq_lat // Q_BLOCK
    n_ctx = nb_ctx // n_seq
    cos, sin = (jnp.asarray(t) for t in _rope_tables_t(seq_lat))

    def ctx_step(i):
        return jnp.minimum(i, n_ctx - 1)

    def lat_step(i):
        return jnp.maximum(i - n_ctx, 0)

    return pl.pallas_call(
        functools.partial(_attn_layer_kernel, n_ctx_steps=n_ctx),
        grid=(n_ctx + nb_lat,),
        in_specs=[
            pl.BlockSpec(memory_space=pltpu.SMEM),
            pl.BlockSpec((n_seq, seq_ctx, D_MODEL), lambda i: (ctx_step(i), 0, 0)),
            pl.BlockSpec((1, seq_lat, D_MODEL), lambda i: (lat_step(i), 0, 0)),
            MOD_SPEC,
            _const_spec((1, D_MODEL)),
            _const_spec((BRANCH + 2 * KV_W, D_MODEL)),
            _const_spec((D_MODEL, BRANCH)),
            _const_spec((BRANCH, D_MODEL)),
            _const_spec((1, HEAD_DIM)),
            _const_spec((1, HEAD_DIM)),
            _const_spec((HEAD_DIM // 2, seq_lat)),
            _const_spec((HEAD_DIM // 2, seq_lat)),
            pl.BlockSpec((1, KV_W, past), lambda i: (lat_step(i), 0, 0)),
            pl.BlockSpec((1, KV_W, past), lambda i: (lat_step(i), 0, 0)),
        ],
        out_specs=[
            pl.BlockSpec((n_seq, seq_ctx, D_MODEL), lambda i: (ctx_step(i), 0, 0)),
            pl.BlockSpec((n_seq, KV_W, seq_ctx), lambda i: (ctx_step(i), 0, 0)),
            pl.BlockSpec((n_seq, KV_W, seq_ctx), lambda i: (ctx_step(i), 0, 0)),
            pl.BlockSpec(memory_space=pl.ANY),
        ],
        out_shape=[
            jax.ShapeDtypeStruct(x_ctx.shape, F32),
            jax.ShapeDtypeStruct((nb_ctx, KV_W, seq_ctx), F32),
            jax.ShapeDtypeStruct((nb_ctx, KV_W, seq_ctx), F32),
            jax.ShapeDtypeStruct(x_lat.shape, F32),
        ],
        scratch_shapes=[
            pltpu.VMEM((n_seq, BRANCH + 2 * KV_W, seq_ctx), F32),
            pltpu.VMEM((seq_lat // BLOCK, N_HEADS // 2, HEAD_DIM, 2 * BLOCK), BF16),
            pltpu.VMEM((seq_lat, BRANCH), F32),
            pltpu.VMEM((seq_lat // Q_BLOCK, BRANCH, Q_BLOCK), F32),
            pltpu.VMEM((seq_lat + 2 * BLOCK, KV_W), BF16),
            pltpu.VMEM((seq_lat // BLOCK + 2, KV_W, BLOCK), BF16),
            pltpu.VMEM(x_lat.shape, F32),
            pltpu.SemaphoreType.DMA((nb_lat, seq_lat // Q_BLOCK)),
        ],
        compiler_params=pltpu.CompilerParams(
            dimension_semantics=("arbitrary",), vmem_limit_bytes=VMEM_LIMIT),
        name="attn_layer",
    )(sink, x_ctx, x_lat, mod, nw, wqkvt, wz, wout, qw, kw, cos, sin, ckt, cvt)


def kernel(x_prompt, x_sample, cache_k_l1, cache_v_l1, c, c_ctx, norm_w_l0, w_mod_l0, b_mod_l0,
           w_in_l0, w_out_l0, norm_w_l1, w_mod_l1, b_mod_l1, w_in_l1, q_norm_w_l1, k_norm_w_l1,
           sink_l1, w_out_l1):
    nb_ctx, seq_ctx, _ = x_prompt.shape
    nb_lat = x_sample.shape[0]
    past = cache_k_l1.shape[1]
    assert 1 + nb_lat <= MOD_ROWS
    nw0 = norm_w_l0.reshape(1, D_MODEL)
    nw1 = norm_w_l1.reshape(1, D_MODEL)
    qw = q_norm_w_l1.reshape(1, HEAD_DIM)
    kw = k_norm_w_l1.reshape(1, HEAD_DIM)

    xp, xs, mod1, wqkvt1, wz1, wout1 = _fourier_layer(
        x_prompt, x_sample, nw0,
        this_layer=(c_ctx, c, w_mod_l0, b_mod_l0, w_in_l0, w_out_l0),
        next_layer=(c_ctx, c, w_mod_l1, b_mod_l1, w_in_l1, w_out_l1))

    def to_feature_major(t):
        return jnp.transpose(t, (0, 2, 3, 1)).reshape(t.shape[0], KV_W, t.shape[1])

    def from_feature_major(t):
        return jnp.transpose(t.reshape(t.shape[0], N_KV, HEAD_DIM, t.shape[2]), (0, 3, 1, 2))

    xp, new_kt, new_vt, xs = _attn_layer(xp, xs, mod1, nw1, wqkvt1, wz1, wout1, qw, kw, sink_l1,
                                         to_feature_major(cache_k_l1), to_feature_major(cache_v_l1))
    return (xp, xs, from_feature_major(new_kt), from_feature_major(new_vt))
```

```python
import functools

import numpy as np
import jax
import jax.numpy as jnp
from jax import lax
from jax.experimental import pallas as pl
from jax.experimental.pallas import tpu as pltpu

D_MODEL = 1024
BRANCH = 1024
N_GROUPS = 4
GROUP_W = BRANCH // N_GROUPS
HALF_W = GROUP_W // 2
HEAD_DIM = 64
N_HEADS = 16
N_KV = 4
GQA = N_HEADS // N_KV
KV_W = N_KV * HEAD_DIM
GRID_W = 64
WINDOW = 128
BLOCK = 128
ROPE_THETA = 10000.0
EPS = 1e-6
NEG_INF = -1e30
LANES = 128
ROW_CHUNK = 256
Q_BLOCK = 256
HALVES = Q_BLOCK // BLOCK
CTX_SEQS_PER_STEP = 2
VMEM_LIMIT = 56 * 1024 * 1024
FOURIER_VMEM_LIMIT = 60 * 1024 * 1024
MOD_ROWS = 8
ONES_ROWS = 16
LOG2E = float(np.log2(np.e))

F32 = jnp.float32
BF16 = jnp.bfloat16


def _dot(a, b):
    return jnp.dot(a, b, preferred_element_type=F32)


def _dot_nt(a, b):
    return lax.dot_general(a, b, (((1,), (1,)), ((), ())), preferred_element_type=F32)


MOD_SPEC = pl.BlockSpec((MOD_ROWS, 3 * D_MODEL), lambda b: (0, 0))


def _mod_norm(x, nw, mod):
    shift = mod[:, :D_MODEL]
    scale = mod[:, D_MODEL:2 * D_MODEL]
    y = x * lax.rsqrt(jnp.mean(x * x, axis=-1, keepdims=True) + EPS)
    return (y * nw) * (1.0 + scale) + shift


def _mod_accumulate(cctx_ref, c_ref, w_ref, b_ref, o_ref, cond_scr, first_step):
    n_lat = c_ref.shape[0]
    cond_scr[...] = jnp.zeros_like(cond_scr)
    cond_scr[0:1, :] = cctx_ref[...]
    cond_scr[1:1 + n_lat, :] = c_ref[...]
    s = jax.nn.silu(cond_scr[...]).astype(BF16)

    @pl.when(pl.program_id(0) == first_step)
    def _():
        o_ref[...] = jnp.broadcast_to(b_ref[...], o_ref.shape)

    o_ref[...] += _dot(s, w_ref[...].astype(BF16))


def _mirror_perm():
    j = np.arange(GROUP_W)
    return np.where(j <= HALF_W, j, GROUP_W + HALF_W - j)


def _perm_matrices():
    perm = _mirror_perm()
    pm = (np.arange(GROUP_W)[:, None] == perm[None, :]).astype(np.float32)
    assert (pm[:HALF_W, HALF_W:] == 0).all() and (pm[HALF_W:, :HALF_W] == 0).all()
    return pm, np.stack([pm[:HALF_W, :HALF_W], pm[HALF_W:, HALF_W:]])


def _fourier_layer_kernel(xc_ref, xl_hbm, nw_ref, m1_ref, csc_ref, ssc_ref, csl_hbm, ssl_hbm,
                          cctx0_ref, c0_ref, wmod0_ref, bmod0_ref, win0_ref, wout0_ref, pm_ref, pmh_ref,
                          cctx_ref, c_ref, wmod1_ref, bmod1_ref, win1_ref, wout1_ref,
                          oc_ref, ol_hbm, mod1_ref, wqkvt1_ref, wz1_ref, wout1b_ref,
                          mod0_scr, win_scr, wout_scr, ta_scr, tb_scr, tr_scr, z_scr, cond_scr,
                          xl_scr, csl_scr, ssl_scr, copy_sem, ol_scr, out_sem,
                          *, n_prep_steps, n_ctx_steps):
    step = pl.program_id(0)
    n_lat = xl_scr.shape[0]
    lat = step - (n_prep_steps + n_ctx_steps)

    def request_copy(r):
        return pltpu.make_async_copy(xl_hbm.at[r], xl_scr.at[r], copy_sem.at[r])

    table_copies = [pltpu.make_async_copy(csl_hbm, csl_scr, copy_sem.at[n_lat]),
                    pltpu.make_async_copy(ssl_hbm, ssl_scr, copy_sem.at[n_lat + 1])]

    def result_copy(r, c):
        rows = pl.ds(c * ROW_CHUNK, ROW_CHUNK)
        return pltpu.make_async_copy(ol_scr.at[r, rows], ol_hbm.at[r, rows], out_sem.at[r, c])

    @pl.when(step == n_prep_steps)
    def _():
        for copy in table_copies + [request_copy(r) for r in range(n_lat)]:
            copy.start()

    @pl.when(step < n_prep_steps)
    def _():
        _mod_accumulate(cctx0_ref, c0_ref, wmod0_ref, bmod0_ref, mod0_scr, cond_scr, 0)
        rows = pl.ds(pl.multiple_of(step * HALF_W, HALF_W), HALF_W)
        w = win0_ref[...].astype(BF16)
        win_scr[rows, :BRANCH] = w[:, :BRANCH]
        pm = pm_ref[...]
        for g in range(N_GROUPS):
            cols = slice(BRANCH + g * GROUP_W, BRANCH + (g + 1) * GROUP_W)
            win_scr[rows, cols] = _dot(w[:, cols], pm).astype(BF16)
        wout_scr[rows, :] = _dot(pmh_ref[step % 2], wout0_ref[...].astype(BF16)).astype(BF16)

    @pl.when(jnp.logical_and(step >= n_prep_steps, step < n_prep_steps + n_ctx_steps))
    def _():
        _mod_accumulate(cctx_ref, c_ref, wmod1_ref, bmod1_ref, mod1_ref, cond_scr, n_prep_steps)
        w = win1_ref[...]
        wqkvt1_ref[...] = w[:, :BRANCH + 2 * KV_W].T.astype(BF16)
        wz1_ref[...] = w[:, BRANCH + 2 * KV_W:].astype(BF16)
        wout1b_ref[...] = wout1_ref[...].astype(BF16)
        _fourier_body(xc_ref, oc_ref, mod0_scr[0:1, :], nw_ref, win_scr, wout_scr, m1_ref, csc_ref, ssc_ref,
                      ta_scr, tb_scr, tr_scr, z_scr)

    @pl.when(step >= n_prep_steps + n_ctx_steps)
    def _():
        @pl.when(lat == 0)
        def _():
            for copy in table_copies:
                copy.wait()

        request_copy(lat).wait()
        mod = mod0_scr[pl.ds(1 + lat, 1), :]
        _fourier_body(xl_scr.at[pl.ds(lat, 1)], ol_scr.at[pl.ds(lat, 1)], mod, nw_ref, win_scr, wout_scr,
                      m1_ref, csl_scr, ssl_scr, ta_scr, tb_scr, tr_scr, z_scr,
                      rows_done=lambda _, rows: result_copy(lat, rows.start // ROW_CHUNK).start())

        @pl.when(lat == n_lat - 1)
        def _():
            for r in range(n_lat):
                for c in range(out_sem.shape[1]):
                    result_copy(r, c).wait()


def _fourier_body(x_ref, o_ref, mod, nw_ref, win_ref, wout_ref, m1_ref, cs_ref, ss_ref,
                  ta_scr, tb_scr, tr_scr, z_scr, rows_done=None):
    gate = mod[:, 2 * D_MODEL:]
    nw = nw_ref[...]
    n_seq, seq, _ = x_ref.shape
    n_chunks = seq // ROW_CHUNK
    lane = lax.broadcasted_iota(jnp.int32, (ROW_CHUNK, HALF_W), 1)
    for i in range(n_seq):
        for c in range(n_chunks):
            rows = slice(c * ROW_CHUNK, (c + 1) * ROW_CHUNK)
            srows = slice(i * seq + c * ROW_CHUNK, i * seq + (c + 1) * ROW_CHUNK)
            h = _mod_norm(x_ref[i, rows, :], nw, mod).astype(BF16)
            uz = _dot(h, win_ref[...])
            z_scr[srows, :] = uz[:, BRANCH:]
            u = uz[:, :BRANCH].astype(BF16)
            tr = jnp.zeros((ROW_CHUNK, HALF_W), F32)
            for g in range(N_GROUPS):
                t = _dot(u[:, g * GROUP_W:(g + 1) * GROUP_W], m1_ref[...])
                half = slice(g * HALF_W, (g + 1) * HALF_W)
                ta_scr[srows, half] = t[:, :HALF_W].astype(BF16)
                tb = t[:, HALF_W:]
                tb_scr[srows, half] = tb.astype(BF16)
                tr = jnp.where(lane == g, tb if g == 0 else pltpu.roll(tb, g, axis=1), tr)
            tr_scr[srows, :] = tr.astype(BF16)
    for i in range(n_seq):
        seq_rows = slice(i * seq, (i + 1) * seq)
        for c in range(n_chunks):
            rows = slice(c * ROW_CHUNK, (c + 1) * ROW_CHUNK)
            srows = slice(i * seq + c * ROW_CHUNK, i * seq + (c + 1) * ROW_CHUNK)
            cs = cs_ref[rows, :]
            p = _dot(cs, ta_scr[seq_rows, :])
            q = _dot(ss_ref[rows, :], tb_scr[seq_rows, :])
            r = _dot(cs, tr_scr[seq_rows, :])
            parts = []
            for g in range(N_GROUPS):
                half = slice(g * HALF_W, (g + 1) * HALF_W)
                pg, qg = p[:, half], q[:, half]
                rg = r if g == 0 else pltpu.roll(r, HALF_W - g, axis=1)
                parts.append(jnp.where(lane == 0, pg, pg - qg))
                parts.append(jnp.where(lane == 0, rg, pg + qg))
            y = jnp.concatenate(parts, axis=1)
            y = (y * jax.nn.silu(z_scr[srows, :])).astype(BF16)
            o_ref[i, rows, :] = x_ref[i, rows, :] + gate * _dot(y, wout_ref[...])
            if rows_done is not None:
                rows_done(i, rows)


def _dft_tables(seq):
    c = np.arange(GROUP_W)[:, None]
    k = np.arange(HALF_W)[None, :]
    cos_lo = np.cos(2.0 * np.pi * ((c * k) % GROUP_W) / GROUP_W)
    sin_lo = np.sin(2.0 * np.pi * ((c * k) % GROUP_W) / GROUP_W)
    sin_lo[:, 0] = np.cos(np.pi * c[:, 0])
    m1 = np.concatenate([cos_lo, sin_lo], axis=1) / np.sqrt(GROUP_W)
    n = np.arange(seq)
    ang = 2.0 * np.pi * ((n[:, None] * n[None, :]) % seq) / seq
    cs = np.cos(ang) / np.sqrt(seq)
    ss = np.sin(ang) / np.sqrt(seq)
    return m1.astype(np.float32), cs.astype(np.float32), ss.astype(np.float32)


def _const_spec(shape):
    return pl.BlockSpec(shape, lambda b: (0,) * len(shape))


def _fourier_layer(x_ctx, x_lat, nw, this_layer, next_layer):
    nb_ctx, seq_ctx, _ = x_ctx.shape
    nb_lat, seq_lat, _ = x_lat.shape
    n_seq = CTX_SEQS_PER_STEP
    assert nb_ctx % n_seq == 0
    n_prep = D_MODEL // HALF_W
    n_ctx = nb_ctx // n_seq
    assert D_MODEL % (n_ctx * LANES) == 0 and n_seq * seq_ctx <= seq_lat
    rows = D_MODEL // n_ctx
    assert rows == HALF_W
    c_ctx, c, w_mod0, b_mod0, w_in0, w_out0 = this_layer
    _, _, w_mod1, b_mod1, w_in1, w_out1 = next_layer
    n_lat = c.shape[0]
    n_qkvz = 2 * BRANCH + 2 * KV_W
    m1, csc, ssc = (jnp.asarray(t).astype(BF16) for t in _dft_tables(seq_ctx))
    _, csl, ssl = (jnp.asarray(t).astype(BF16) for t in _dft_tables(seq_lat))
    pm, pmh = (jnp.asarray(t).astype(BF16) for t in _perm_matrices())
    cc = c_ctx.reshape(1, D_MODEL)

    def prep_step(i):
        return jnp.minimum(i, n_prep - 1)

    def ctx_step(i):
        return jnp.clip(i - n_prep, 0, n_ctx - 1)

    def chunk_specs(step_fn, w_in_cols):
        return [pl.BlockSpec((1, HALF_W), lambda i: (0, step_fn(i))),
                pl.BlockSpec((n_lat, HALF_W), lambda i: (0, step_fn(i))),
                pl.BlockSpec((HALF_W, 3 * D_MODEL), lambda i: (step_fn(i), 0)),
                _const_spec((1, 3 * D_MODEL)),
                pl.BlockSpec((HALF_W, w_in_cols), lambda i: (step_fn(i), 0)),
                pl.BlockSpec((HALF_W, D_MODEL), lambda i: (step_fn(i), 0))]

    in_specs = [
        pl.BlockSpec((n_seq, seq_ctx, D_MODEL), lambda i: (ctx_step(i), 0, 0)),
        pl.BlockSpec(memory_space=pl.ANY),
        _const_spec((1, D_MODEL)),
        _const_spec((GROUP_W, GROUP_W)),
        _const_spec((seq_ctx, seq_ctx)),
        _const_spec((seq_ctx, seq_ctx)),
        pl.BlockSpec(memory_space=pl.ANY),
        pl.BlockSpec(memory_space=pl.ANY),
    ] + chunk_specs(prep_step, 2 * BRANCH) + [
        _const_spec((GROUP_W, GROUP_W)),
        _const_spec((2, HALF_W, HALF_W)),
    ] + chunk_specs(ctx_step, n_qkvz)
    out_specs = [
        pl.BlockSpec((n_seq, seq_ctx, D_MODEL), lambda i: (ctx_step(i), 0, 0)),
        pl.BlockSpec(memory_space=pl.ANY),
        MOD_SPEC,
        pl.BlockSpec((BRANCH + 2 * KV_W, rows), lambda i: (0, ctx_step(i))),
        pl.BlockSpec((rows, BRANCH), lambda i: (ctx_step(i), 0)),
        pl.BlockSpec((rows, D_MODEL), lambda i: (ctx_step(i), 0)),
    ]
    out_shape = [
        jax.ShapeDtypeStruct(x_ctx.shape, F32),
        jax.ShapeDtypeStruct(x_lat.shape, F32),
        jax.ShapeDtypeStruct((MOD_ROWS, 3 * D_MODEL), F32),
        jax.ShapeDtypeStruct((BRANCH + 2 * KV_W, D_MODEL), BF16),
        jax.ShapeDtypeStruct((D_MODEL, BRANCH), BF16),
        jax.ShapeDtypeStruct((BRANCH, D_MODEL), BF16),
    ]
    return pl.pallas_call(
        functools.partial(_fourier_layer_kernel, n_prep_steps=n_prep, n_ctx_steps=n_ctx),
        grid=(n_prep + n_ctx + nb_lat,),
        in_specs=in_specs,
        out_specs=out_specs,
        out_shape=out_shape,
        scratch_shapes=[
            pltpu.VMEM((MOD_ROWS, 3 * D_MODEL), F32),
            pltpu.VMEM((D_MODEL, 2 * BRANCH), BF16),
            pltpu.VMEM((BRANCH, D_MODEL), BF16),
            pltpu.VMEM((seq_lat, N_GROUPS * HALF_W), BF16),
            pltpu.VMEM((seq_lat, N_GROUPS * HALF_W), BF16),
            pltpu.VMEM((seq_lat, HALF_W), BF16),
            pltpu.VMEM((seq_lat, BRANCH), F32),
            pltpu.VMEM((MOD_ROWS, HALF_W), F32),
            pltpu.VMEM(x_lat.shape, F32),
            pltpu.VMEM((seq_lat, seq_lat), BF16),
            pltpu.VMEM((seq_lat, seq_lat), BF16),
            pltpu.SemaphoreType.DMA((nb_lat + 2,)),
            pltpu.VMEM(x_lat.shape, F32),
            pltpu.SemaphoreType.DMA((nb_lat, seq_lat // ROW_CHUNK)),
        ],
        compiler_params=pltpu.CompilerParams(
            dimension_semantics=("arbitrary",), vmem_limit_bytes=FOURIER_VMEM_LIMIT),
        name="fourier_layer",
    )(x_ctx, x_lat, nw, m1, csc, ssc, csl, ssl,
      cc, c, w_mod0, b_mod0.reshape(1, 3 * D_MODEL), w_in0, w_out0, pm, pmh,
      cc, c, w_mod1, b_mod1.reshape(1, 3 * D_MODEL), w_in1, w_out1)


def _head_weight_tile(w_ref, n_tokens):
    row = jnp.broadcast_to(w_ref[...], (HEAD_DIM, HEAD_DIM))
    ii = lax.broadcasted_iota(jnp.int32, (HEAD_DIM, HEAD_DIM), 0)
    jj = lax.broadcasted_iota(jnp.int32, (HEAD_DIM, HEAD_DIM), 1)
    col = jnp.sum(jnp.where(ii == jj, row, 0.0), axis=1, keepdims=True)
    return jnp.broadcast_to(col, (HEAD_DIM, n_tokens))


def _head_rms(t, w):
    return (t * lax.rsqrt(jnp.mean(t * t, axis=0, keepdims=True) + EPS)) * w


def _rope_t(t, cos, sin):
    half = HEAD_DIM // 2
    x1, x2 = t[:half], t[half:]
    return jnp.concatenate([x1 * cos - x2 * sin, x1 * sin + x2 * cos], axis=0)


def _head_scores(qn, g, keys, biases):
    zeros = jnp.zeros_like(qn)
    qz = jnp.concatenate([qn, zeros] if g % 2 == 0 else [zeros, qn], axis=0)
    blk = slice((g // 2) * LANES, (g // 2 + 1) * LANES)
    scores = []
    smax = None
    for k, bias in zip(keys, biases):
        s = _dot(k[:, blk], qz)
        if bias is not None:
            s = jnp.concatenate(
                [s[c * BLOCK:(c + 1) * BLOCK] if b is None else
                 s[c * BLOCK:(c + 1) * BLOCK] + jnp.concatenate([b] * (s.shape[1] // b.shape[1]), axis=1)
                 for c, b in enumerate(bias)], axis=0)
        cmax = jnp.max(s, axis=0, keepdims=True)
        smax = cmax if smax is None else jnp.maximum(smax, cmax)
        scores.append(s)
    return scores, smax


def _with_ones_rows(vt):
    return jnp.concatenate([vt, jnp.ones((ONES_ROWS, vt.shape[1]), vt.dtype)], axis=0)


def _head_probs(scored, sink2):
    scores, smax = scored
    m = jnp.maximum(smax, sink2)
    return jnp.concatenate([jnp.exp2(s - m).astype(BF16) for s in scores], axis=0), m


def _head_pv(probs, values_t, sink2):
    p, m = probs
    acc = _dot(values_t, p)
    den = acc[HEAD_DIM:HEAD_DIM + 1] + jnp.exp2(sink2 - m)
    return acc[:HEAD_DIM] * (1.0 / den)


def _attend_heads(n_units, stage, scores_fn, probs_fn, pv_fn, fillers=()):
    n_stages = n_units // stage
    pending = [scores_fn(u) for u in range(stage)]
    for g in range(n_stages):
        units = range(g * stage, (g + 1) * stage)
        if g < len(fillers):
            fillers[g]()
        nxt = [scores_fn(u) for u in range((g + 1) * stage, (g + 2) * stage)] if g + 1 < n_stages else None
        probs = [probs_fn(u, sc) for u, sc in zip(units, pending)]
        for u, pr in zip(units, probs):
            pv_fn(u, pr)
        pending = nxt


def _gate_out(x, o, z, gate, wout):
    y = (o * jax.nn.silu(z)).astype(BF16)
    return x + gate * _dot(y, wout)


def _attn_ctx_body(sink_ref, x_ref, mod, nw_ref, wqkvt_ref, wz_ref, wout_ref, qw_ref, kw_ref,
                   o_ref, kto_ref, vto_ref, qkvt_scr, z_scr, ot_scr):
    seq = x_ref.shape[1]
    n_seq = x_ref.shape[0]
    gate = mod[:, 2 * D_MODEL:]
    kw = _head_weight_tile(kw_ref, seq)
    qw = _head_weight_tile(qw_ref, seq)

    def project(i):
        h = _mod_norm(x_ref[i], nw_ref[...], mod).astype(BF16)
        qkvt_scr[i] = _dot_nt(wqkvt_ref[...], h)
        z_scr[i * seq:(i + 1) * seq, :] = _dot(h, wz_ref[...])

    def keys_values(i):
        knt = jnp.concatenate(
            [_head_rms(qkvt_scr[i, BRANCH + g * HEAD_DIM:BRANCH + (g + 1) * HEAD_DIM, :], kw)
             for g in range(N_KV)], axis=0)
        kto_ref[i] = knt
        vtf = qkvt_scr[i, BRANCH + KV_W:, :]
        vto_ref[i] = vtf
        vt = vtf.astype(BF16)
        return (knt.T.astype(BF16),
                [_with_ones_rows(vt[g * HEAD_DIM:(g + 1) * HEAD_DIM]) for g in range(N_KV)])

    def output(i):
        o_ref[i] = _gate_out(x_ref[i], ot_scr[i].T, z_scr[i * seq:(i + 1) * seq, :], gate, wout_ref[...])

    for i in range(n_seq):
        project(i)
    kv = [keys_values(i) for i in range(n_seq)]

    def scores_fn(u):
        i, hd = divmod(u, N_HEADS)
        t = qkvt_scr[i, hd * HEAD_DIM:(hd + 1) * HEAD_DIM, :]
        qn = (_head_rms(t, qw) * (HEAD_DIM ** -0.5 * LOG2E)).astype(BF16)
        return _head_scores(qn, hd // GQA, [kv[i][0]], [None])

    def probs_fn(u, sc):
        return _head_probs(sc, sink_ref[u % N_HEADS] * LOG2E)

    def pv_fn(u, pr):
        i, hd = divmod(u, N_HEADS)
        ot_scr[i, hd * HEAD_DIM:(hd + 1) * HEAD_DIM, :] = _head_pv(
            pr, kv[i][1][hd // GQA], sink_ref[hd] * LOG2E)

    stage = GQA
    fillers = [lambda: None] * (n_seq * N_HEADS // stage)
    for i in range(n_seq - 1):
        fillers[(i + 1) * (N_HEADS // stage)] = functools.partial(output, i)
    _attend_heads(n_seq * N_HEADS, stage, scores_fn, probs_fn, pv_fn, fillers)
    output(n_seq - 1)


def _attn_lat_body(sink_ref, x_ref, mod, nw_ref, wqkvt_ref, wz_ref, wout_ref, qw_ref, kw_ref,
                   cos_ref, sin_ref, ck_ref, cv_ref, o_ref,
                   q_scr, z_scr, ot_scr, k_scr, vt_scr, block_done=None):
    seq = x_ref.shape[1]
    gate = mod[:, 2 * D_MODEL:]
    nw = nw_ref[...]
    qw = _head_weight_tile(qw_ref, Q_BLOCK)
    kw = _head_weight_tile(kw_ref, Q_BLOCK)
    n_blocks = seq // Q_BLOCK
    kv_blocks = seq // BLOCK
    k_scr[0:BLOCK, :] = jnp.zeros((BLOCK, KV_W), BF16)
    k_scr[BLOCK + seq:2 * BLOCK + seq, :] = jnp.zeros((BLOCK, KV_W), BF16)
    vt_scr[0] = jnp.zeros((KV_W, BLOCK), BF16)
    vt_scr[kv_blocks + 1] = jnp.zeros((KV_W, BLOCK), BF16)

    def project(c):
        rows = slice(c * Q_BLOCK, (c + 1) * Q_BLOCK)
        h = _mod_norm(x_ref[0, rows, :], nw, mod).astype(BF16)
        z_scr[rows, :] = _dot(h, wz_ref[...])
        qkvt = _dot_nt(wqkvt_ref[...], h)
        cos = cos_ref[:, rows]
        sin = sin_ref[:, rows]
        for hd in range(N_HEADS):
            hr = slice(hd * HEAD_DIM, (hd + 1) * HEAD_DIM)
            t = _rope_t(_head_rms(qkvt[hr], qw), cos, sin)
            t = (t * (HEAD_DIM ** -0.5 * LOG2E)).astype(BF16)
            for j in range(HALVES):
                q_scr[c * HALVES + j, hd // 2, :, (hd % 2) * BLOCK:(hd % 2 + 1) * BLOCK] = (
                    t[:, j * BLOCK:(j + 1) * BLOCK])
        knt = jnp.concatenate(
            [_rope_t(_head_rms(qkvt[BRANCH + g * HEAD_DIM:BRANCH + (g + 1) * HEAD_DIM], kw), cos, sin)
             for g in range(N_KV)], axis=0)
        k_scr[BLOCK + c * Q_BLOCK:BLOCK + (c + 1) * Q_BLOCK, :] = knt.T.astype(BF16)
        vt = qkvt[BRANCH + KV_W:].astype(BF16)
        for j in range(HALVES):
            vt_scr[1 + c * HALVES + j] = vt[:, j * BLOCK:(j + 1) * BLOCK]

    ckb = ck_ref[0].T.astype(BF16)
    cvt = cv_ref[0].astype(BF16)

    win_len = 3 * BLOCK
    n_pairs = N_HEADS // 2
    kj = lax.broadcasted_iota(jnp.int32, (BLOCK, BLOCK), 0)
    qi = lax.broadcasted_iota(jnp.int32, (BLOCK, BLOCK), 1)
    first_head = lax.broadcasted_iota(jnp.int32, (1, 2 * BLOCK), 1) < BLOCK
    assert WINDOW >= BLOCK - 1

    def band_bias(c):
        rel = kj + (c - 1) * BLOCK - qi
        return jnp.where((rel >= -WINDOW) & (rel <= WINDOW), 0.0, NEG_INF)

    band_before, band_after = band_bias(0), band_bias(2)

    def half_operands(hb):
        r0 = pl.multiple_of(hb * BLOCK, BLOCK)
        valid = (jnp.where(hb > 0, band_before, NEG_INF), None,
                 jnp.where(hb < kv_blocks - 1, band_after, NEG_INF))
        kwin = k_scr[pl.ds(r0, win_len), :]
        vall = jnp.concatenate([vt_scr[hb + j] for j in range(win_len // BLOCK)] + [cvt], axis=1)
        valls = [_with_ones_rows(vall[g * HEAD_DIM:(g + 1) * HEAD_DIM]) for g in range(N_KV)]
        return valid, kwin, valls

    def pair_sink2(pair):
        return jnp.where(first_head, sink_ref[2 * pair], sink_ref[2 * pair + 1]) * LOG2E

    def attend(n, carry):
        halves = [half_operands(n * HALVES + j) for j in range(HALVES)]

        def scores_fn(u):
            j, pair = divmod(u, n_pairs)
            valid, kwin, _ = halves[j]
            return _head_scores(q_scr[n * HALVES + j, pair], pair // (GQA // 2), [kwin, ckb], [valid, None])

        def probs_fn(u, sc):
            return _head_probs(sc, pair_sink2(u % n_pairs))

        def pv_fn(u, pr):
            j, pair = divmod(u, n_pairs)
            o = _head_pv(pr, halves[j][2][pair // (GQA // 2)], pair_sink2(pair))
            for i in range(2):
                hd = 2 * pair + i
                ot_scr[n, hd * HEAD_DIM:(hd + 1) * HEAD_DIM, j * BLOCK:(j + 1) * BLOCK] = (
                    o[:, i * BLOCK:(i + 1) * BLOCK])

        _attend_heads(HALVES * n_pairs, GQA // 2, scores_fn, probs_fn, pv_fn)
        return carry

    for c in range(n_blocks):
        project(c)
    assert n_blocks % 2 == 0
    lax.fori_loop(0, n_blocks // 2, lambda n, carry: attend(2 * n + 1, attend(2 * n, carry)), 0)
    for c in range(n_blocks):
        rows = slice(c * Q_BLOCK, (c + 1) * Q_BLOCK)
        o_ref[0, rows, :] = _gate_out(x_ref[0, rows, :], ot_scr[c].T, z_scr[rows, :], gate, wout_ref[...])
        if block_done is not None:
            block_done(c)


def _rope_tables_t(seq):
    pos = np.arange(seq)
    n_freq = HEAD_DIM // 4
    inv = ROPE_THETA ** (-np.arange(n_freq, dtype=np.float64) / n_freq)
    ang = np.concatenate([(pos // GRID_W)[:, None] * inv, (pos % GRID_W)[:, None] * inv], axis=-1)
    return np.cos(ang).T.astype(np.float32), np.sin(ang).T.astype(np.float32)


def _attn_layer_kernel(sink_ref, xc_ref, xl_ref, mod_ref, nw_ref, wqkvt_ref, wz_ref, wout_ref, qw_ref, kw_ref,
                       cos_ref, sin_ref, ck_ref, cv_ref, oc_ref, kto_ref, vto_ref, ol_hbm,
                       qkvt_scr, q_scr, z_scr, ot_scr, k_scr, vt_scr, ol_scr, out_sem, *, n_ctx_steps):
    step = pl.program_id(0)
    n_lat = ol_scr.shape[0]
    lat = step - n_ctx_steps

    def result_copy(r, c):
        rows = pl.ds(c * Q_BLOCK, Q_BLOCK)
        return pltpu.make_async_copy(ol_scr.at[r, rows], ol_hbm.at[r, rows], out_sem.at[r, c])

    @pl.when(step < n_ctx_steps)
    def _():
        _attn_ctx_body(sink_ref, xc_ref, mod_ref[0:1, :], nw_ref, wqkvt_ref, wz_ref, wout_ref, qw_ref, kw_ref,
                       oc_ref, kto_ref, vto_ref, qkvt_scr, z_scr, ot_scr)

    @pl.when(step >= n_ctx_steps)
    def _():
        mod = mod_ref[pl.ds(1 + lat, 1), :]
        _attn_lat_body(sink_ref, xl_ref, mod, nw_ref, wqkvt_ref, wz_ref, wout_ref, qw_ref, kw_ref,
                       cos_ref, sin_ref, ck_ref, cv_ref, ol_scr.at[pl.ds(lat, 1)],
                       q_scr, z_scr, ot_scr, k_scr, vt_scr,
                       block_done=lambda c: result_copy(lat, c).start())

        @pl.when(lat == n_lat - 1)
        def _():
            for r in range(n_lat):
                for c in range(out_sem.shape[1]):
                    result_copy(r, c).wait()


def _attn_layer(x_ctx, x_lat, mod, nw, wqkvt, wz, wout, qw, kw, sink, ckt, cvt):
    nb_ctx, seq_ctx, _ = x_ctx.shape
    nb_lat, seq_lat, _ = x_lat.shape
    past = ckt.shape[2]
    n_seq = CTX_SEQS_PER_STEP
    assert nb_ctx % n_seq == 0 and seq_ctx == Q_BLOCK and n_seq <= seq_lat // Q_BLOCK
    n_ctx = nb_ctx // n_seq
    cos, sin = (jnp.asarray(t) for t in _rope_tables_t(seq_lat))

    def ctx_step(i):
        return jnp.minimum(i, n_ctx - 1)

    def lat_step(i):
        return jnp.maximum(i - n_ctx, 0)

    return pl.pallas_call(
        functools.partial(_attn_layer_kernel, n_ctx_steps=n_ctx),
        grid=(n_ctx + nb_lat,),
        in_specs=[
            pl.BlockSpec(memory_space=pltpu.SMEM),
            pl.BlockSpec((n_seq, seq_ctx, D_MODEL), lambda i: (ctx_step(i), 0, 0)),
            pl.BlockSpec((1, seq_lat, D_MODEL), lambda i: (lat_step(i), 0, 0)),
            MOD_SPEC,
            _const_spec((1, D_MODEL)),
            _const_spec((BRANCH + 2 * KV_W, D_MODEL)),
            _const_spec((D_MODEL, BRANCH)),
            _const_spec((BRANCH, D_MODEL)),
            _const_spec((1, HEAD_DIM)),
            _const_spec((1, HEAD_DIM)),
            _const_spec((HEAD_DIM // 2, seq_lat)),
            _const_spec((HEAD_DIM // 2, seq_lat)),
            pl.BlockSpec((1, KV_W, past), lambda i: (lat_step(i), 0, 0)),
            pl.BlockSpec((1, KV_W, past), lambda i: (lat_step(i), 0, 0)),
        ],
        out_specs=[
            pl.BlockSpec((n_seq, seq_ctx, D_MODEL), lambda i: (ctx_step(i), 0, 0)),
            pl.BlockSpec((n_seq, KV_W, seq_ctx), lambda i: (ctx_step(i), 0, 0)),
            pl.BlockSpec((n_seq, KV_W, seq_ctx), lambda i: (ctx_step(i), 0, 0)),
            pl.BlockSpec(memory_space=pl.ANY),
        ],
        out_shape=[
            jax.ShapeDtypeStruct(x_ctx.shape, F32),
            jax.ShapeDtypeStruct((nb_ctx, KV_W, seq_ctx), F32),
            jax.ShapeDtypeStruct((nb_ctx, KV_W, seq_ctx), F32),
            jax.ShapeDtypeStruct(x_lat.shape, F32),
        ],
        scratch_shapes=[
            pltpu.VMEM((n_seq, BRANCH + 2 * KV_W, seq_ctx), F32),
            pltpu.VMEM((seq_lat // BLOCK, N_HEADS // 2, HEAD_DIM, 2 * BLOCK), BF16),
            pltpu.VMEM((seq_lat, BRANCH), F32),
            pltpu.VMEM((seq_lat // Q_BLOCK, BRANCH, Q_BLOCK), F32),
            pltpu.VMEM((seq_lat + 2 * BLOCK, KV_W), BF16),
            pltpu.VMEM((seq_lat // BLOCK + 2, KV_W, BLOCK), BF16),
            pltpu.VMEM(x_lat.shape, F32),
            pltpu.SemaphoreType.DMA((nb_lat, seq_lat // Q_BLOCK)),
        ],
        compiler_params=pltpu.CompilerParams(
            dimension_semantics=("arbitrary",), vmem_limit_bytes=VMEM_LIMIT),
        name="attn_layer",
    )(sink, x_ctx, x_lat, mod, nw, wqkvt, wz, wout, qw, kw, cos, sin, ckt, cvt)


def kernel(x_prompt, x_sample, cache_k_l1, cache_v_l1, c, c_ctx, norm_w_l0, w_mod_l0, b_mod_l0,
           w_in_l0, w_out_l0, norm_w_l1, w_mod_l1, b_mod_l1, w_in_l1, q_norm_w_l1, k_norm_w_l1,
           sink_l1, w_out_l1):
    nb_ctx, seq_ctx, _ = x_prompt.shape
    nb_lat = x_sample.shape[0]
    past = cache_k_l1.shape[1]
    assert 1 + nb_lat <= MOD_ROWS
    nw0 = norm_w_l0.reshape(1, D_MODEL)
    nw1 = norm_w_l1.reshape(1, D_MODEL)
    qw = q_norm_w_l1.reshape(1, HEAD_DIM)
    kw = k_norm_w_l1.reshape(1, HEAD_DIM)

    xp, xs, mod1, wqkvt1, wz1, wout1 = _fourier_layer(
        x_prompt, x_sample, nw0,
        this_layer=(c_ctx, c, w_mod_l0, b_mod_l0, w_in_l0, w_out_l0),
        next_layer=(c_ctx, c, w_mod_l1, b_mod_l1, w_in_l1, w_out_l1))

    def to_feature_major(t):
        return jnp.transpose(t, (0, 2, 3, 1)).reshape(t.shape[0], KV_W, t.shape[1])

    def from_feature_major(t):
        return jnp.transpose(t.reshape(t.shape[0], N_KV, HEAD_DIM, t.shape[2]), (0, 3, 1, 2))

    xp, new_kt, new_vt, xs = _attn_layer(xp, xs, mod1, nw1, wqkvt1, wz1, wout1, qw, kw, sink_l1,
                                         to_feature_major(cache_k_l1), to_feature_major(cache_v_l1))
    return (xp, xs, from_feature_major(new_kt), from_feature_major(new_vt))
```

```python
import functools

import numpy as np
import jax
import jax.numpy as jnp
from jax import lax
from jax.experimental import pallas as pl
from jax.experimental.pallas import tpu as pltpu

D_MODEL = 1024
BRANCH = 1024
N_GROUPS = 4
GROUP_W = BRANCH // N_GROUPS
HALF_W = GROUP_W // 2
HEAD_DIM = 64
N_HEADS = 16
N_KV = 4
GQA = N_HEADS // N_KV
KV_W = N_KV * HEAD_DIM
GRID_W = 64
WINDOW = 128
BLOCK = 128
ROPE_THETA = 10000.0
EPS = 1e-6
NEG_INF = -1e30
LANES = 128
ROW_CHUNK = 256
Q_BLOCK = 256
HALVES = Q_BLOCK // BLOCK
CTX_SEQS_PER_STEP = 2
VMEM_LIMIT = 56 * 1024 * 1024
FOURIER_VMEM_LIMIT = 60 * 1024 * 1024
MOD_ROWS = 8
ONES_ROWS = 16
LOG2E = float(np.log2(np.e))

F32 = jnp.float32
BF16 = jnp.bfloat16


def _dot(a, b):
    return jnp.dot(a, b, preferred_element_type=F32)


def _dot_nt(a, b):
    return lax.dot_general(a, b, (((1,), (1,)), ((), ())), preferred_element_type=F32)


MOD_SPEC = pl.BlockSpec((MOD_ROWS, 3 * D_MODEL), lambda b: (0, 0))


def _mod_norm(x, nw, mod):
    shift = mod[:, :D_MODEL]
    scale = mod[:, D_MODEL:2 * D_MODEL]
    y = x * lax.rsqrt(jnp.mean(x * x, axis=-1, keepdims=True) + EPS)
    return (y * nw) * (1.0 + scale) + shift


def _mod_accumulate(cctx_ref, c_ref, w_ref, b_ref, o_ref, cond_scr, first_step):
    n_lat = c_ref.shape[0]
    cond_scr[...] = jnp.zeros_like(cond_scr)
    cond_scr[0:1, :] = cctx_ref[...]
    cond_scr[1:1 + n_lat, :] = c_ref[...]
    s = jax.nn.silu(cond_scr[...]).astype(BF16)

    @pl.when(pl.program_id(0) == first_step)
    def _():
        o_ref[...] = jnp.broadcast_to(b_ref[...], o_ref.shape)

    o_ref[...] += _dot(s, w_ref[...].astype(BF16))


def _mirror_perm():
    j = np.arange(GROUP_W)
    return np.where(j <= HALF_W, j, GROUP_W + HALF_W - j)


def _perm_matrices():
    perm = _mirror_perm()
    pm = (np.arange(GROUP_W)[:, None] == perm[None, :]).astype(np.float32)
    assert (pm[:HALF_W, HALF_W:] == 0).all() and (pm[HALF_W:, :HALF_W] == 0).all()
    return pm, np.stack([pm[:HALF_W, :HALF_W], pm[HALF_W:, HALF_W:]])


def _fourier_layer_kernel(xc_ref, xl_hbm, nw_ref, m1_ref, csc_ref, ssc_ref, csl_hbm, ssl_hbm,
                          cctx0_ref, c0_ref, wmod0_ref, bmod0_ref, win0_ref, wout0_ref, pm_ref, pmh_ref,
                          cctx_ref, c_ref, wmod1_ref, bmod1_ref, win1_ref, wout1_ref,
                          oc_ref, ol_hbm, mod1_ref, wqkvt1_ref, wz1_ref, wout1b_ref,
                          mod0_scr, win_scr, wout_scr, ta_scr, tb_scr, tr_scr, z_scr,
                          xl_scr, csl_scr, ssl_scr, copy_sem, ol_scr, out_sem, cond_scr,
                          *, n_prep_steps, n_ctx_steps):
    step = pl.program_id(0)
    n_lat = xl_scr.shape[0]
    lat = step - (n_prep_steps + n_ctx_steps)

    def request_copy(r):
        return pltpu.make_async_copy(xl_hbm.at[r], xl_scr.at[r], copy_sem.at[r])

    table_copies = [pltpu.make_async_copy(csl_hbm, csl_scr, copy_sem.at[n_lat]),
                    pltpu.make_async_copy(ssl_hbm, ssl_scr, copy_sem.at[n_lat + 1])]

    def result_copy(r, c):
        rows = pl.ds(c * ROW_CHUNK, ROW_CHUNK)
        return pltpu.make_async_copy(ol_scr.at[r, rows], ol_hbm.at[r, rows], out_sem.at[r, c])

    @pl.when(step == n_prep_steps)
    def _():
        for copy in table_copies + [request_copy(r) for r in range(n_lat)]:
            copy.start()

    @pl.when(step < n_prep_steps)
    def _():
        _mod_accumulate(cctx0_ref, c0_ref, wmod0_ref, bmod0_ref, mod0_scr, cond_scr, 0)
        rows = pl.ds(pl.multiple_of(step * HALF_W, HALF_W), HALF_W)
        w = win0_ref[...].astype(BF16)
        win_scr[rows, :BRANCH] = w[:, :BRANCH]
        pm = pm_ref[...]
        for g in range(N_GROUPS):
            cols = slice(BRANCH + g * GROUP_W, BRANCH + (g + 1) * GROUP_W)
            win_scr[rows, cols] = _dot(w[:, cols], pm).astype(BF16)
        wout_scr[rows, :] = _dot(pmh_ref[step % 2], wout0_ref[...].astype(BF16)).astype(BF16)

    @pl.when(jnp.logical_and(step >= n_prep_steps, step < n_prep_steps + n_ctx_steps))
    def _():
        _mod_accumulate(cctx_ref, c_ref, wmod1_ref, bmod1_ref, mod1_ref, cond_scr, n_prep_steps)
        w = win1_ref[...]
        wqkvt1_ref[...] = w[:, :BRANCH + 2 * KV_W].T.astype(BF16)
        wz1_ref[...] = w[:, BRANCH + 2 * KV_W:].astype(BF16)
        wout1b_ref[...] = wout1_ref[...].astype(BF16)
        _fourier_body(xc_ref, oc_ref, mod0_scr[0:1, :], nw_ref, win_scr, wout_scr, m1_ref, csc_ref, ssc_ref,
                      ta_scr, tb_scr, tr_scr, z_scr)

    @pl.when(step >= n_prep_steps + n_ctx_steps)
    def _():
        @pl.when(lat == 0)
        def _():
            for copy in table_copies:
                copy.wait()

        request_copy(lat).wait()
        mod = mod0_scr[pl.ds(1 + lat, 1), :]
        _fourier_body(xl_scr.at[pl.ds(lat, 1)], ol_scr.at[pl.ds(lat, 1)], mod, nw_ref, win_scr, wout_scr,
                      m1_ref, csl_scr, ssl_scr, ta_scr, tb_scr, tr_scr, z_scr,
                      rows_done=lambda _, rows: result_copy(lat, rows.start // ROW_CHUNK).start())

        @pl.when(lat == n_lat - 1)
        def _():
            for r in range(n_lat):
                for c in range(out_sem.shape[1]):
                    result_copy(r, c).wait()


def _fourier_body(x_ref, o_ref, mod, nw_ref, win_ref, wout_ref, m1_ref, cs_ref, ss_ref,
                  ta_scr, tb_scr, tr_scr, z_scr, rows_done=None):
    gate = mod[:, 2 * D_MODEL:]
    nw = nw_ref[...]
    n_seq, seq, _ = x_ref.shape
    n_chunks = seq // ROW_CHUNK
    lane = lax.broadcasted_iota(jnp.int32, (ROW_CHUNK, HALF_W), 1)
    for i in range(n_seq):
        for c in range(n_chunks):
            rows = slice(c * ROW_CHUNK, (c + 1) * ROW_CHUNK)
            srows = slice(i * seq + c * ROW_CHUNK, i * seq + (c + 1) * ROW_CHUNK)
            h = _mod_norm(x_ref[i, rows, :], nw, mod).astype(BF16)
            uz = _dot(h, win_ref[...])
            z_scr[srows, :] = uz[:, BRANCH:]
            u = uz[:, :BRANCH].astype(BF16)
            tr = jnp.zeros((ROW_CHUNK, HALF_W), F32)
            for g in range(N_GROUPS):
                t = _dot(u[:, g * GROUP_W:(g + 1) * GROUP_W], m1_ref[...])
                half = slice(g * HALF_W, (g + 1) * HALF_W)
                ta_scr[srows, half] = t[:, :HALF_W].astype(BF16)
                tb = t[:, HALF_W:]
                tb_scr[srows, half] = tb.astype(BF16)
                tr = jnp.where(lane == g, tb if g == 0 else pltpu.roll(tb, g, axis=1), tr)
            tr_scr[srows, :] = tr.astype(BF16)
    for i in range(n_seq):
        seq_rows = slice(i * seq, (i + 1) * seq)
        for c in range(n_chunks):
            rows = slice(c * ROW_CHUNK, (c + 1) * ROW_CHUNK)
            srows = slice(i * seq + c * ROW_CHUNK, i * seq + (c + 1) * ROW_CHUNK)
            cs = cs_ref[rows, :]
            p = _dot(cs, ta_scr[seq_rows, :])
            q = _dot(ss_ref[rows, :], tb_scr[seq_rows, :])
            r = _dot(cs, tr_scr[seq_rows, :])
            parts = []
            for g in range(N_GROUPS):
                half = slice(g * HALF_W, (g + 1) * HALF_W)
                pg, qg = p[:, half], q[:, half]
                rg = r if g == 0 else pltpu.roll(r, HALF_W - g, axis=1)
                parts.append(jnp.where(lane == 0, pg, pg - qg))
                parts.append(jnp.where(lane == 0, rg, pg + qg))
            y = jnp.concatenate(parts, axis=1)
            y = (y * jax.nn.silu(z_scr[srows, :])).astype(BF16)
            o_ref[i, rows, :] = x_ref[i, rows, :] + gate * _dot(y, wout_ref[...])
            if rows_done is not None:
                rows_done(i, rows)


def _dft_tables(seq):
    c = np.arange(GROUP_W)[:, None]
    k = np.arange(HALF_W)[None, :]
    cos_lo = np.cos(2.0 * np.pi * ((c * k) % GROUP_W) / GROUP_W)
    sin_lo = np.sin(2.0 * np.pi * ((c * k) % GROUP_W) / GROUP_W)
    sin_lo[:, 0] = np.cos(np.pi * c[:, 0])
    m1 = np.concatenate([cos_lo, sin_lo], axis=1) / np.sqrt(GROUP_W)
    n = np.arange(seq)
    ang = 2.0 * np.pi * ((n[:, None] * n[None, :]) % seq) / seq
    cs = np.cos(ang) / np.sqrt(seq)
    ss = np.sin(ang) / np.sqrt(seq)
    return m1.astype(np.float32), cs.astype(np.float32), ss.astype(np.float32)


def _const_spec(shape):
    return pl.BlockSpec(shape, lambda b: (0,) * len(shape))


def _fourier_layer(x_ctx, x_lat, nw, this_layer, next_layer):
    nb_ctx, seq_ctx, _ = x_ctx.shape
    nb_lat, seq_lat, _ = x_lat.shape
    n_seq = CTX_SEQS_PER_STEP
    assert nb_ctx % n_seq == 0
    n_prep = D_MODEL // HALF_W
    n_ctx = nb_ctx // n_seq
    assert D_MODEL % (n_ctx * LANES) == 0 and n_seq * seq_ctx <= seq_lat
    rows = D_MODEL // n_ctx
    assert rows == HALF_W
    c_ctx, c, w_mod0, b_mod0, w_in0, w_out0 = this_layer
    _, _, w_mod1, b_mod1, w_in1, w_out1 = next_layer
    n_lat = c.shape[0]
    n_qkvz = 2 * BRANCH + 2 * KV_W
    m1, csc, ssc = (jnp.asarray(t).astype(BF16) for t in _dft_tables(seq_ctx))
    _, csl, ssl = (jnp.asarray(t).astype(BF16) for t in _dft_tables(seq_lat))
    pm, pmh = (jnp.asarray(t).astype(BF16) for t in _perm_matrices())
    cc = c_ctx.reshape(1, D_MODEL)

    def prep_step(i):
        return jnp.minimum(i, n_prep - 1)

    def ctx_step(i):
        return jnp.clip(i - n_prep, 0, n_ctx - 1)

    def chunk_specs(step_fn, w_in_cols):
        return [pl.BlockSpec((1, HALF_W), lambda i: (0, step_fn(i))),
                pl.BlockSpec((n_lat, HALF_W), lambda i: (0, step_fn(i))),
                pl.BlockSpec((HALF_W, 3 * D_MODEL), lambda i: (step_fn(i), 0)),
                _const_spec((1, 3 * D_MODEL)),
                pl.BlockSpec((HALF_W, w_in_cols), lambda i: (step_fn(i), 0)),
                pl.BlockSpec((HALF_W, D_MODEL), lambda i: (step_fn(i), 0))]

    in_specs = [
        pl.BlockSpec((n_seq, seq_ctx, D_MODEL), lambda i: (ctx_step(i), 0, 0)),
        pl.BlockSpec(memory_space=pl.ANY),
        _const_spec((1, D_MODEL)),
        _const_spec((GROUP_W, GROUP_W)),
        _const_spec((seq_ctx, seq_ctx)),
        _const_spec((seq_ctx, seq_ctx)),
        pl.BlockSpec(memory_space=pl.ANY),
        pl.BlockSpec(memory_space=pl.ANY),
    ] + chunk_specs(prep_step, 2 * BRANCH) + [
        _const_spec((GROUP_W, GROUP_W)),
        _const_spec((2, HALF_W, HALF_W)),
    ] + chunk_specs(ctx_step, n_qkvz)
    out_specs = [
        pl.BlockSpec((n_seq, seq_ctx, D_MODEL), lambda i: (ctx_step(i), 0, 0)),
        pl.BlockSpec(memory_space=pl.ANY),
        MOD_SPEC,
        pl.BlockSpec((BRANCH + 2 * KV_W, rows), lambda i: (0, ctx_step(i))),
        pl.BlockSpec((rows, BRANCH), lambda i: (ctx_step(i), 0)),
        pl.BlockSpec((rows, D_MODEL), lambda i: (ctx_step(i), 0)),
    ]
    out_shape = [
        jax.ShapeDtypeStruct(x_ctx.shape, F32),
        jax.ShapeDtypeStruct(x_lat.shape, F32),
        jax.ShapeDtypeStruct((MOD_ROWS, 3 * D_MODEL), F32),
        jax.ShapeDtypeStruct((BRANCH + 2 * KV_W, D_MODEL), BF16),
        jax.ShapeDtypeStruct((D_MODEL, BRANCH), BF16),
        jax.ShapeDtypeStruct((BRANCH, D_MODEL), BF16),
    ]
    return pl.pallas_call(
        functools.partial(_fourier_layer_kernel, n_prep_steps=n_prep, n_ctx_steps=n_ctx),
        grid=(n_prep + n_ctx + nb_lat,),
        in_specs=in_specs,
        out_specs=out_specs,
        out_shape=out_shape,
        scratch_shapes=[
            pltpu.VMEM((MOD_ROWS, 3 * D_MODEL), F32),
            pltpu.VMEM((D_MODEL, 2 * BRANCH), BF16),
            pltpu.VMEM((BRANCH, D_MODEL), BF16),
            pltpu.VMEM((seq_lat, N_GROUPS * HALF_W), BF16),
            pltpu.VMEM((seq_lat, N_GROUPS * HALF_W), BF16),
            pltpu.VMEM((seq_lat, HALF_W), BF16),
            pltpu.VMEM((seq_lat, BRANCH), F32),
            pltpu.VMEM(x_lat.shape, F32),
            pltpu.VMEM((seq_lat, seq_lat), BF16),
            pltpu.VMEM((seq_lat, seq_lat), BF16),
            pltpu.SemaphoreType.DMA((nb_lat + 2,)),
            pltpu.VMEM(x_lat.shape, F32),
            pltpu.SemaphoreType.DMA((nb_lat, seq_lat // ROW_CHUNK)),
            pltpu.VMEM((MOD_ROWS, HALF_W), F32),
        ],
        compiler_params=pltpu.CompilerParams(
            dimension_semantics=("arbitrary",), vmem_limit_bytes=FOURIER_VMEM_LIMIT),
        name="fourier_layer",
    )(x_ctx, x_lat, nw, m1, csc, ssc, csl, ssl,
      cc, c, w_mod0, b_mod0.reshape(1, 3 * D_MODEL), w_in0, w_out0, pm, pmh,
      cc, c, w_mod1, b_mod1.reshape(1, 3 * D_MODEL), w_in1, w_out1)


def _head_weight_tile(w_ref, n_tokens):
    row = jnp.broadcast_to(w_ref[...], (HEAD_DIM, HEAD_DIM))
    ii = lax.broadcasted_iota(jnp.int32, (HEAD_DIM, HEAD_DIM), 0)
    jj = lax.broadcasted_iota(jnp.int32, (HEAD_DIM, HEAD_DIM), 1)
    col = jnp.sum(jnp.where(ii == jj, row, 0.0), axis=1, keepdims=True)
    return jnp.broadcast_to(col, (HEAD_DIM, n_tokens))


def _head_rms(t, w):
    return (t * lax.rsqrt(jnp.mean(t * t, axis=0, keepdims=True) + EPS)) * w


def _rope_t(t, cos, sin):
    half = HEAD_DIM // 2
    x1, x2 = t[:half], t[half:]
    return jnp.concatenate([x1 * cos - x2 * sin, x1 * sin + x2 * cos], axis=0)


def _head_scores(qn, g, keys, biases):
    zeros = jnp.zeros_like(qn)
    qz = jnp.concatenate([qn, zeros] if g % 2 == 0 else [zeros, qn], axis=0)
    blk = slice((g // 2) * LANES, (g // 2 + 1) * LANES)
    scores = []
    smax = None
    for k, bias in zip(keys, biases):
        s = _dot(k[:, blk], qz)
        if bias is not None:
            s = jnp.concatenate(
                [s[c * BLOCK:(c + 1) * BLOCK] if b is None else
                 s[c * BLOCK:(c + 1) * BLOCK] + jnp.concatenate([b] * (s.shape[1] // b.shape[1]), axis=1)
                 for c, b in enumerate(bias)], axis=0)
        cmax = jnp.max(s, axis=0, keepdims=True)
        smax = cmax if smax is None else jnp.maximum(smax, cmax)
        scores.append(s)
    return scores, smax


def _with_ones_rows(vt):
    return jnp.concatenate([vt, jnp.ones((ONES_ROWS, vt.shape[1]), vt.dtype)], axis=0)


def _head_probs(scored, sink2):
    scores, smax = scored
    m = jnp.maximum(smax, sink2)
    return jnp.concatenate([jnp.exp2(s - m).astype(BF16) for s in scores], axis=0), m


def _head_pv(probs, values_t, sink2):
    p, m = probs
    acc = _dot(values_t, p)
    den = acc[HEAD_DIM:HEAD_DIM + 1] + jnp.exp2(sink2 - m)
    return acc[:HEAD_DIM] * (1.0 / den)


def _attend_heads(n_units, stage, scores_fn, probs_fn, pv_fn, fillers=()):
    n_stages = n_units // stage
    pending = [scores_fn(u) for u in range(stage)]
    for g in range(n_stages):
        units = range(g * stage, (g + 1) * stage)
        if g < len(fillers):
            fillers[g]()
        nxt = [scores_fn(u) for u in range((g + 1) * stage, (g + 2) * stage)] if g + 1 < n_stages else None
        probs = [probs_fn(u, sc) for u, sc in zip(units, pending)]
        for u, pr in zip(units, probs):
            pv_fn(u, pr)
        pending = nxt


def _gate_out(x, o, z, gate, wout):
    y = (o * jax.nn.silu(z)).astype(BF16)
    return x + gate * _dot(y, wout)


def _attn_ctx_body(sink_ref, x_ref, mod, nw_ref, wqkvt_ref, wz_ref, wout_ref, qw_ref, kw_ref,
                   o_ref, kto_ref, vto_ref, qkvt_scr, z_scr, ot_scr):
    seq = x_ref.shape[1]
    n_seq = x_ref.shape[0]
    gate = mod[:, 2 * D_MODEL:]
    kw = _head_weight_tile(kw_ref, seq)
    qw = _head_weight_tile(qw_ref, seq)

    def project(i):
        h = _mod_norm(x_ref[i], nw_ref[...], mod).astype(BF16)
        qkvt_scr[i] = _dot_nt(wqkvt_ref[...], h)
        z_scr[i * seq:(i + 1) * seq, :] = _dot(h, wz_ref[...])

    def keys_values(i):
        knt = jnp.concatenate(
            [_head_rms(qkvt_scr[i, BRANCH + g * HEAD_DIM:BRANCH + (g + 1) * HEAD_DIM, :], kw)
             for g in range(N_KV)], axis=0)
        kto_ref[i] = knt
        vtf = qkvt_scr[i, BRANCH + KV_W:, :]
        vto_ref[i] = vtf
        vt = vtf.astype(BF16)
        return (knt.T.astype(BF16),
                [_with_ones_rows(vt[g * HEAD_DIM:(g + 1) * HEAD_DIM]) for g in range(N_KV)])

    def output(i):
        o_ref[i] = _gate_out(x_ref[i], ot_scr[i].T, z_scr[i * seq:(i + 1) * seq, :], gate, wout_ref[...])

    for i in range(n_seq):
        project(i)
    kv = [keys_values(i) for i in range(n_seq)]

    def scores_fn(u):
        i, hd = divmod(u, N_HEADS)
        t = qkvt_scr[i, hd * HEAD_DIM:(hd + 1) * HEAD_DIM, :]
        qn = (_head_rms(t, qw) * (HEAD_DIM ** -0.5 * LOG2E)).astype(BF16)
        return _head_scores(qn, hd // GQA, [kv[i][0]], [None])

    def probs_fn(u, sc):
        return _head_probs(sc, sink_ref[u % N_HEADS] * LOG2E)

    def pv_fn(u, pr):
        i, hd = divmod(u, N_HEADS)
        ot_scr[i, hd * HEAD_DIM:(hd + 1) * HEAD_DIM, :] = _head_pv(
            pr, kv[i][1][hd // GQA], sink_ref[hd] * LOG2E)

    fillers = [lambda: None] + [functools.partial(output, i) for i in range(n_seq - 1)]
    _attend_heads(n_seq * N_HEADS, N_HEADS, scores_fn, probs_fn, pv_fn, fillers)
    output(n_seq - 1)


def _attn_lat_body(sink_ref, x_ref, mod, nw_ref, wqkvt_ref, wz_ref, wout_ref, qw_ref, kw_ref,
                   cos_ref, sin_ref, ck_ref, cv_ref, o_ref,
                   q_scr, z_scr, ot_scr, k_scr, vt_scr, block_done=None):
    seq = x_ref.shape[1]
    gate = mod[:, 2 * D_MODEL:]
    nw = nw_ref[...]
    qw = _head_weight_tile(qw_ref, Q_BLOCK)
    kw = _head_weight_tile(kw_ref, Q_BLOCK)
    n_blocks = seq // Q_BLOCK
    kv_blocks = seq // BLOCK
    k_scr[0:BLOCK, :] = jnp.zeros((BLOCK, KV_W), BF16)
    k_scr[BLOCK + seq:2 * BLOCK + seq, :] = jnp.zeros((BLOCK, KV_W), BF16)
    vt_scr[0] = jnp.zeros((KV_W, BLOCK), BF16)
    vt_scr[kv_blocks + 1] = jnp.zeros((KV_W, BLOCK), BF16)

    def project(c):
        rows = slice(c * Q_BLOCK, (c + 1) * Q_BLOCK)
        h = _mod_norm(x_ref[0, rows, :], nw, mod).astype(BF16)
        z_scr[rows, :] = _dot(h, wz_ref[...])
        qkvt = _dot_nt(wqkvt_ref[...], h)
        cos = cos_ref[:, rows]
        sin = sin_ref[:, rows]
        for hd in range(N_HEADS):
            hr = slice(hd * HEAD_DIM, (hd + 1) * HEAD_DIM)
            t = _rope_t(_head_rms(qkvt[hr], qw), cos, sin)
            t = (t * (HEAD_DIM ** -0.5 * LOG2E)).astype(BF16)
            for j in range(HALVES):
                q_scr[c * HALVES + j, hd // 2, :, (hd % 2) * BLOCK:(hd % 2 + 1) * BLOCK] = (
                    t[:, j * BLOCK:(j + 1) * BLOCK])
        knt = jnp.concatenate(
            [_rope_t(_head_rms(qkvt[BRANCH + g * HEAD_DIM:BRANCH + (g + 1) * HEAD_DIM], kw), cos, sin)
             for g in range(N_KV)], axis=0)
        k_scr[BLOCK + c * Q_BLOCK:BLOCK + (c + 1) * Q_BLOCK, :] = knt.T.astype(BF16)
        vt = qkvt[BRANCH + KV_W:].astype(BF16)
        for j in range(HALVES):
            vt_scr[1 + c * HALVES + j] = vt[:, j * BLOCK:(j + 1) * BLOCK]

    ckb = ck_ref[0].T.astype(BF16)
    cvt = cv_ref[0].astype(BF16)

    win_len = 3 * BLOCK
    n_pairs = N_HEADS // 2
    kj = lax.broadcasted_iota(jnp.int32, (BLOCK, BLOCK), 0)
    qi = lax.broadcasted_iota(jnp.int32, (BLOCK, BLOCK), 1)
    first_head = lax.broadcasted_iota(jnp.int32, (1, 2 * BLOCK), 1) < BLOCK
    assert WINDOW >= BLOCK - 1

    def band_bias(c):
        rel = kj + (c - 1) * BLOCK - qi
        return jnp.where((rel >= -WINDOW) & (rel <= WINDOW), 0.0, NEG_INF)

    band_before, band_after = band_bias(0), band_bias(2)

    def half_operands(hb):
        r0 = pl.multiple_of(hb * BLOCK, BLOCK)
        valid = (jnp.where(hb > 0, band_before, NEG_INF), None,
                 jnp.where(hb < kv_blocks - 1, band_after, NEG_INF))
        kwin = k_scr[pl.ds(r0, win_len), :]
        vall = jnp.concatenate([vt_scr[hb + j] for j in range(win_len // BLOCK)] + [cvt], axis=1)
        valls = [_with_ones_rows(vall[g * HEAD_DIM:(g + 1) * HEAD_DIM]) for g in range(N_KV)]
        return valid, kwin, valls

    def pair_sink2(pair):
        return jnp.where(first_head, sink_ref[2 * pair], sink_ref[2 * pair + 1]) * LOG2E

    def attend(n, carry):
        halves = [half_operands(n * HALVES + j) for j in range(HALVES)]

        def scores_fn(u):
            j, pair = divmod(u, n_pairs)
            valid, kwin, _ = halves[j]
            return _head_scores(q_scr[n * HALVES + j, pair], pair // (GQA // 2), [kwin, ckb], [valid, None])

        def probs_fn(u, sc):
            return _head_probs(sc, pair_sink2(u % n_pairs))

        def pv_fn(u, pr):
            j, pair = divmod(u, n_pairs)
            o = _head_pv(pr, halves[j][2][pair // (GQA // 2)], pair_sink2(pair))
            for i in range(2):
                hd = 2 * pair + i
                ot_scr[n, hd * HEAD_DIM:(hd + 1) * HEAD_DIM, j * BLOCK:(j + 1) * BLOCK] = (
                    o[:, i * BLOCK:(i + 1) * BLOCK])

        _attend_heads(HALVES * n_pairs, GQA // 2, scores_fn, probs_fn, pv_fn)
        return carry

    for c in range(n_blocks):
        project(c)
    assert n_blocks % 2 == 0
    lax.fori_loop(0, n_blocks // 2, lambda n, carry: attend(2 * n + 1, attend(2 * n, carry)), 0)
    for c in range(n_blocks):
        rows = slice(c * Q_BLOCK, (c + 1) * Q_BLOCK)
        o_ref[0, rows, :] = _gate_out(x_ref[0, rows, :], ot_scr[c].T, z_scr[rows, :], gate, wout_ref[...])
        if block_done is not None:
            block_done(c)


def _rope_tables_t(seq):
    pos = np.arange(seq)
    n_freq = HEAD_DIM // 4
    inv = ROPE_THETA ** (-np.arange(n_freq, dtype=np.float64) / n_freq)
    ang = np.concatenate([(pos // GRID_W)[:, None] * inv, (pos % GRID_W)[:, None] * inv], axis=-1)
    return np.cos(ang).T.astype(np.float32), np.sin(ang).T.astype(np.float32)


def _attn_layer_kernel(sink_ref, xc_ref, xl_ref, mod_ref, nw_ref, wqkvt_ref, wz_ref, wout_ref, qw_ref, kw_ref,
                       cos_ref, sin_ref, ck_ref, cv_ref, oc_ref, kto_ref, vto_ref, ol_hbm,
                       qkvt_scr, q_scr, z_scr, ot_scr, k_scr, vt_scr, ol_scr, out_sem, *, n_ctx_steps):
    step = pl.program_id(0)
    n_lat = ol_scr.shape[0]
    lat = step - n_ctx_steps

    def result_copy(r, c):
        rows = pl.ds(c * Q_BLOCK, Q_BLOCK)
        return pltpu.make_async_copy(ol_scr.at[r, rows], ol_hbm.at[r, rows], out_sem.at[r, c])

    @pl.when(step < n_ctx_steps)
    def _():
        _attn_ctx_body(sink_ref, xc_ref, mod_ref[0:1, :], nw_ref, wqkvt_ref, wz_ref, wout_ref, qw_ref, kw_ref,
                       oc_ref, kto_ref, vto_ref, qkvt_scr, z_scr, ot_scr)

    @pl.when(step >= n_ctx_steps)
    def _():
        mod = mod_ref[pl.ds(1 + lat, 1), :]
        _attn_lat_body(sink_ref, xl_ref, mod, nw_ref, wqkvt_ref, wz_ref, wout_ref, qw_ref, kw_ref,
                       cos_ref, sin_ref, ck_ref, cv_ref, ol_scr.at[pl.ds(lat, 1)],
                       q_scr, z_scr, ot_scr, k_scr, vt_scr,
                       block_done=lambda c: result_copy(lat, c).start())

        @pl.when(lat == n_lat - 1)
        def _():
            for r in range(n_lat):
                for c in range(out_sem.shape[1]):
                    result_copy(r, c).wait()


def _attn_layer(x_ctx, x_lat, mod, nw, wqkvt, wz, wout, qw, kw, sink, ckt, cvt):
    nb_ctx, seq_ctx, _ = x_ctx.shape
    nb_lat, seq_lat, _ = x_lat.shape
    past = ckt.shape[2]
    n_seq = CTX_SEQS_PER_STEP
    assert nb_ctx % n_seq == 0 and seq_ctx == Q_BLOCK and n_seq <= seq_lat // Q_BLOCK
    n_ctx = nb_ctx // n_seq
    cos, sin = (jnp.asarray(t) for t in _rope_tables_t(seq_lat))

    def ctx_step(i):
        return jnp.minimum(i, n_ctx - 1)

    def lat_step(i):
        return jnp.maximum(i - n_ctx, 0)

    return pl.pallas_call(
        functools.partial(_attn_layer_kernel, n_ctx_steps=n_ctx),
        grid=(n_ctx + nb_lat,),
        in_specs=[
            pl.BlockSpec(memory_space=pltpu.SMEM),
            pl.BlockSpec((n_seq, seq_ctx, D_MODEL), lambda i: (ctx_step(i), 0, 0)),
            pl.BlockSpec((1, seq_lat, D_MODEL), lambda i: (lat_step(i), 0, 0)),
            MOD_SPEC,
            _const_spec((1, D_MODEL)),
            _const_spec((BRANCH + 2 * KV_W, D_MODEL)),
            _const_spec((D_MODEL, BRANCH)),
            _const_spec((BRANCH, D_MODEL)),
            _const_spec((1, HEAD_DIM)),
            _const_spec((1, HEAD_DIM)),
            _const_spec((HEAD_DIM // 2, seq_lat)),
            _const_spec((HEAD_DIM // 2, seq_lat)),
            pl.BlockSpec((1, KV_W, past), lambda i: (lat_step(i), 0, 0)),
            pl.BlockSpec((1, KV_W, past), lambda i: (lat_step(i), 0, 0)),
        ],
        out_specs=[
            pl.BlockSpec((n_seq, seq_ctx, D_MODEL), lambda i: (ctx_step(i), 0, 0)),
            pl.BlockSpec((n_seq, KV_W, seq_ctx), lambda i: (ctx_step(i), 0, 0)),
            pl.BlockSpec((n_seq, KV_W, seq_ctx), lambda i: (ctx_step(i), 0, 0)),
            pl.BlockSpec(memory_space=pl.ANY),
        ],
        out_shape=[
            jax.ShapeDtypeStruct(x_ctx.shape, F32),
            jax.ShapeDtypeStruct((nb_ctx, KV_W, seq_ctx), F32),
            jax.ShapeDtypeStruct((nb_ctx, KV_W, seq_ctx), F32),
            jax.ShapeDtypeStruct(x_lat.shape, F32),
        ],
        scratch_shapes=[
            pltpu.VMEM((n_seq, BRANCH + 2 * KV_W, seq_ctx), F32),
            pltpu.VMEM((seq_lat // BLOCK, N_HEADS // 2, HEAD_DIM, 2 * BLOCK), BF16),
            pltpu.VMEM((seq_lat, BRANCH), F32),
            pltpu.VMEM((seq_lat // Q_BLOCK, BRANCH, Q_BLOCK), F32),
            pltpu.VMEM((seq_lat + 2 * BLOCK, KV_W), BF16),
            pltpu.VMEM((seq_lat // BLOCK + 2, KV_W, BLOCK), BF16),
            pltpu.VMEM(x_lat.shape, F32),
            pltpu.SemaphoreType.DMA((nb_lat, seq_lat // Q_BLOCK)),
        ],
        compiler_params=pltpu.CompilerParams(
            dimension_semantics=("arbitrary",), vmem_limit_bytes=VMEM_LIMIT),
        name="attn_layer",
    )(sink, x_ctx, x_lat, mod, nw, wqkvt, wz, wout, qw, kw, cos, sin, ckt, cvt)


def kernel(x_prompt, x_sample, cache_k_l1, cache_v_l1, c, c_ctx, norm_w_l0, w_mod_l0, b_mod_l0,
           w_in_l0, w_out_l0, norm_w_l1, w_mod_l1, b_mod_l1, w_in_l1, q_norm_w_l1, k_norm_w_l1,
           sink_l1, w_out_l1):
    nb_ctx, seq_ctx, _ = x_prompt.shape
    nb_lat = x_sample.shape[0]
    past = cache_k_l1.shape[1]
    assert 1 + nb_lat <= MOD_ROWS
    nw0 = norm_w_l0.reshape(1, D_MODEL)
    nw1 = norm_w_l1.reshape(1, D_MODEL)
    qw = q_norm_w_l1.reshape(1, HEAD_DIM)
    kw = k_norm_w_l1.reshape(1, HEAD_DIM)

    xp, xs, mod1, wqkvt1, wz1, wout1 = _fourier_layer(
        x_prompt, x_sample, nw0,
        this_layer=(c_ctx, c, w_mod_l0, b_mod_l0, w_in_l0, w_out_l0),
        next_layer=(c_ctx, c, w_mod_l1, b_mod_l1, w_in_l1, w_out_l1))

    def to_feature_major(t):
        return jnp.transpose(t, (0, 2, 3, 1)).reshape(t.shape[0], KV_W, t.shape[1])

    def from_feature_major(t):
        return jnp.transpose(t.reshape(t.shape[0], N_KV, HEAD_DIM, t.shape[2]), (0, 3, 1, 2))

    xp, new_kt, new_vt, xs = _attn_layer(xp, xs, mod1, nw1, wqkvt1, wz1, wout1, qw, kw, sink_l1,
                                         to_feature_major(cache_k_l1), to_feature_major(cache_v_l1))
    return (xp, xs, from_feature_major(new_kt), from_feature_major(new_vt))
```

```python
import functools

import numpy as np
import jax
import jax.numpy as jnp
from jax import lax
from jax.experimental import pallas as pl
from jax.experimental.pallas import tpu as pltpu

D_MODEL = 1024
BRANCH = 1024
N_GROUPS = 4
GROUP_W = BRANCH // N_GROUPS
HALF_W = GROUP_W // 2
HEAD_DIM = 64
N_HEADS = 16
N_KV = 4
GQA = N_HEADS // N_KV
KV_W = N_KV * HEAD_DIM
GRID_W = 64
WINDOW = 128
BLOCK = 128
ROPE_THETA = 10000.0
EPS = 1e-6
NEG_INF = -1e30
LANES = 128
ROW_CHUNK = 256
Q_BLOCK = 256
HALVES = Q_BLOCK // BLOCK
CTX_SEQS_PER_STEP = 2
VMEM_LIMIT = 56 * 1024 * 1024
FOURIER_VMEM_LIMIT = 60 * 1024 * 1024
MOD_ROWS = 8
ONES_ROWS = 16
LOG2E = float(np.log2(np.e))

F32 = jnp.float32
BF16 = jnp.bfloat16


def _dot(a, b):
    return jnp.dot(a, b, preferred_element_type=F32)


def _dot_nt(a, b):
    return lax.dot_general(a, b, (((1,), (1,)), ((), ())), preferred_element_type=F32)


MOD_SPEC = pl.BlockSpec((MOD_ROWS, 3 * D_MODEL), lambda b: (0, 0))


def _mod_norm(x, nw, mod):
    shift = mod[:, :D_MODEL]
    scale = mod[:, D_MODEL:2 * D_MODEL]
    y = x * lax.rsqrt(jnp.mean(x * x, axis=-1, keepdims=True) + EPS)
    return (y * nw) * (1.0 + scale) + shift


def _mod_accumulate(cctx_ref, c_ref, w_ref, b_ref, o_ref, cond_scr, first_step):
    n_lat = c_ref.shape[0]
    cond_scr[...] = jnp.zeros_like(cond_scr)
    cond_scr[0:1, :] = cctx_ref[...]
    cond_scr[1:1 + n_lat, :] = c_ref[...]
    s = jax.nn.silu(cond_scr[...]).astype(BF16)

    @pl.when(pl.program_id(0) == first_step)
    def _():
        o_ref[...] = jnp.broadcast_to(b_ref[...], o_ref.shape)

    o_ref[...] += _dot(s, w_ref[...].astype(BF16))


def _mirror_perm():
    j = np.arange(GROUP_W)
    return np.where(j <= HALF_W, j, GROUP_W + HALF_W - j)


def _perm_matrices():
    perm = _mirror_perm()
    pm = (np.arange(GROUP_W)[:, None] == perm[None, :]).astype(np.float32)
    assert (pm[:HALF_W, HALF_W:] == 0).all() and (pm[HALF_W:, :HALF_W] == 0).all()
    return pm, np.stack([pm[:HALF_W, :HALF_W], pm[HALF_W:, HALF_W:]])


def _fourier_layer_kernel(xc_ref, xl_hbm, nw_ref, m1_ref, csc_ref, ssc_ref, csl_hbm, ssl_hbm,
                          cctx0_ref, c0_ref, wmod0_ref, bmod0_ref, win0_ref, wout0_ref, pm_ref, pmh_ref,
                          cctx_ref, c_ref, wmod1_ref, bmod1_ref, win1_ref, wout1_ref,
                          oc_ref, ol_hbm, mod1_ref, wqkvt1_ref, wz1_ref, wout1b_ref,
                          mod0_scr, win_scr, wout_scr, ta_scr, tb_scr, tr_scr, z_scr,
                          xl_scr, csl_scr, ssl_scr, copy_sem, ol_scr, out_sem, cond_scr,
                          *, n_prep_steps, n_ctx_steps):
    step = pl.program_id(0)
    n_lat = xl_scr.shape[0]
    lat = step - (n_prep_steps + n_ctx_steps)

    def request_copy(r):
        return pltpu.make_async_copy(xl_hbm.at[r], xl_scr.at[r], copy_sem.at[r])

    table_copies = [pltpu.make_async_copy(csl_hbm, csl_scr, copy_sem.at[n_lat]),
                    pltpu.make_async_copy(ssl_hbm, ssl_scr, copy_sem.at[n_lat + 1])]

    def result_copy(r, c):
        rows = pl.ds(c * ROW_CHUNK, ROW_CHUNK)
        return pltpu.make_async_copy(ol_scr.at[r, rows], ol_hbm.at[r, rows], out_sem.at[r, c])

    @pl.when(step == n_prep_steps)
    def _():
        for copy in table_copies + [request_copy(r) for r in range(n_lat)]:
            copy.start()

    @pl.when(step < n_prep_steps)
    def _():
        _mod_accumulate(cctx0_ref, c0_ref, wmod0_ref, bmod0_ref, mod0_scr, cond_scr, 0)
        rows = pl.ds(pl.multiple_of(step * HALF_W, HALF_W), HALF_W)
        w = win0_ref[...].astype(BF16)
        win_scr[rows, :BRANCH] = w[:, :BRANCH]
        pm = pm_ref[...]
        for g in range(N_GROUPS):
            cols = slice(BRANCH + g * GROUP_W, BRANCH + (g + 1) * GROUP_W)
            win_scr[rows, cols] = _dot(w[:, cols], pm).astype(BF16)
        wout_scr[rows, :] = _dot(pmh_ref[step % 2], wout0_ref[...].astype(BF16)).astype(BF16)

    @pl.when(jnp.logical_and(step >= n_prep_steps, step < n_prep_steps + n_ctx_steps))
    def _():
        _mod_accumulate(cctx_ref, c_ref, wmod1_ref, bmod1_ref, mod1_ref, cond_scr, n_prep_steps)
        w = win1_ref[...]
        wqkvt1_ref[...] = w[:, :BRANCH + 2 * KV_W].T.astype(BF16)
        wz1_ref[...] = w[:, BRANCH + 2 * KV_W:].astype(BF16)
        wout1b_ref[...] = wout1_ref[...].astype(BF16)
        _fourier_body(xc_ref, oc_ref, mod0_scr[0:1, :], nw_ref, win_scr, wout_scr, m1_ref, csc_ref, ssc_ref,
                      ta_scr, tb_scr, tr_scr, z_scr)

    @pl.when(step >= n_prep_steps + n_ctx_steps)
    def _():
        @pl.when(lat == 0)
        def _():
            for copy in table_copies:
                copy.wait()

        request_copy(lat).wait()
        mod = mod0_scr[pl.ds(1 + lat, 1), :]
        _fourier_body(xl_scr.at[pl.ds(lat, 1)], ol_scr.at[pl.ds(lat, 1)], mod, nw_ref, win_scr, wout_scr,
                      m1_ref, csl_scr, ssl_scr, ta_scr, tb_scr, tr_scr, z_scr,
                      rows_done=lambda _, rows: result_copy(lat, rows.start // ROW_CHUNK).start())

        @pl.when(lat == n_lat - 1)
        def _():
            for r in range(n_lat):
                for c in range(out_sem.shape[1]):
                    result_copy(r, c).wait()


def _fourier_body(x_ref, o_ref, mod, nw_ref, win_ref, wout_ref, m1_ref, cs_ref, ss_ref,
                  ta_scr, tb_scr, tr_scr, z_scr, rows_done=None):
    gate = mod[:, 2 * D_MODEL:]
    nw = nw_ref[...]
    n_seq, seq, _ = x_ref.shape
    n_chunks = seq // ROW_CHUNK
    lane = lax.broadcasted_iota(jnp.int32, (ROW_CHUNK, HALF_W), 1)
    for i in range(n_seq):
        for c in range(n_chunks):
            rows = slice(c * ROW_CHUNK, (c + 1) * ROW_CHUNK)
            srows = slice(i * seq + c * ROW_CHUNK, i * seq + (c + 1) * ROW_CHUNK)
            h = _mod_norm(x_ref[i, rows, :], nw, mod).astype(BF16)
            uz = _dot(h, win_ref[...])
            z_scr[srows, :] = uz[:, BRANCH:]
            u = uz[:, :BRANCH].astype(BF16)
            tr = jnp.zeros((ROW_CHUNK, HALF_W), F32)
            for g in range(N_GROUPS):
                t = _dot(u[:, g * GROUP_W:(g + 1) * GROUP_W], m1_ref[...])
                half = slice(g * HALF_W, (g + 1) * HALF_W)
                ta_scr[srows, half] = t[:, :HALF_W].astype(BF16)
                tb = t[:, HALF_W:]
                tb_scr[srows, half] = tb.astype(BF16)
                tr = jnp.where(lane == g, tb if g == 0 else pltpu.roll(tb, g, axis=1), tr)
            tr_scr[srows, :] = tr.astype(BF16)
    for i in range(n_seq):
        seq_rows = slice(i * seq, (i + 1) * seq)
        for c in range(n_chunks):
            rows = slice(c * ROW_CHUNK, (c + 1) * ROW_CHUNK)
            srows = slice(i * seq + c * ROW_CHUNK, i * seq + (c + 1) * ROW_CHUNK)
            cs = cs_ref[rows, :]
            p = _dot(cs, ta_scr[seq_rows, :])
            q = _dot(ss_ref[rows, :], tb_scr[seq_rows, :])
            r = _dot(cs, tr_scr[seq_rows, :])
            parts = []
            for g in range(N_GROUPS):
                half = slice(g * HALF_W, (g + 1) * HALF_W)
                pg, qg = p[:, half], q[:, half]
                rg = r if g == 0 else pltpu.roll(r, HALF_W - g, axis=1)
                parts.append(jnp.where(lane == 0, pg, pg - qg))
                parts.append(jnp.where(lane == 0, rg, pg + qg))
            y = jnp.concatenate(parts, axis=1)
            y = (y * jax.nn.silu(z_scr[srows, :])).astype(BF16)
            o_ref[i, rows, :] = x_ref[i, rows, :] + gate * _dot(y, wout_ref[...])
            if rows_done is not None:
                rows_done(i, rows)


def _dft_tables(seq):
    c = np.arange(GROUP_W)[:, None]
    k = np.arange(HALF_W)[None, :]
    cos_lo = np.cos(2.0 * np.pi * ((c * k) % GROUP_W) / GROUP_W)
    sin_lo = np.sin(2.0 * np.pi * ((c * k) % GROUP_W) / GROUP_W)
    sin_lo[:, 0] = np.cos(np.pi * c[:, 0])
    m1 = np.concatenate([cos_lo, sin_lo], axis=1) / np.sqrt(GROUP_W)
    n = np.arange(seq)
    ang = 2.0 * np.pi * ((n[:, None] * n[None, :]) % seq) / seq
    cs = np.cos(ang) / np.sqrt(seq)
    ss = np.sin(ang) / np.sqrt(seq)
    return m1.astype(np.float32), cs.astype(np.float32), ss.astype(np.float32)


def _const_spec(shape):
    return pl.BlockSpec(shape, lambda b: (0,) * len(shape))


def _fourier_layer(x_ctx, x_lat, nw, this_layer, next_layer):
    nb_ctx, seq_ctx, _ = x_ctx.shape
    nb_lat, seq_lat, _ = x_lat.shape
    n_seq = CTX_SEQS_PER_STEP
    assert nb_ctx % n_seq == 0
    n_prep = D_MODEL // HALF_W
    n_ctx = nb_ctx // n_seq
    assert D_MODEL % (n_ctx * LANES) == 0 and n_seq * seq_ctx <= seq_lat
    rows = D_MODEL // n_ctx
    assert rows == HALF_W
    c_ctx, c, w_mod0, b_mod0, w_in0, w_out0 = this_layer
    _, _, w_mod1, b_mod1, w_in1, w_out1 = next_layer
    n_lat = c.shape[0]
    n_qkvz = 2 * BRANCH + 2 * KV_W
    m1, csc, ssc = (jnp.asarray(t).astype(BF16) for t in _dft_tables(seq_ctx))
    _, csl, ssl = (jnp.asarray(t).astype(BF16) for t in _dft_tables(seq_lat))
    pm, pmh = (jnp.asarray(t).astype(BF16) for t in _perm_matrices())
    cc = c_ctx.reshape(1, D_MODEL)

    def prep_step(i):
        return jnp.minimum(i, n_prep - 1)

    def ctx_step(i):
        return jnp.clip(i - n_prep, 0, n_ctx - 1)

    def chunk_specs(step_fn, w_in_cols):
        return [pl.BlockSpec((1, HALF_W), lambda i: (0, step_fn(i))),
                pl.BlockSpec((n_lat, HALF_W), lambda i: (0, step_fn(i))),
                pl.BlockSpec((HALF_W, 3 * D_MODEL), lambda i: (step_fn(i), 0)),
                _const_spec((1, 3 * D_MODEL)),
                pl.BlockSpec((HALF_W, w_in_cols), lambda i: (step_fn(i), 0)),
                pl.BlockSpec((HALF_W, D_MODEL), lambda i: (step_fn(i), 0))]

    in_specs = [
        pl.BlockSpec((n_seq, seq_ctx, D_MODEL), lambda i: (ctx_step(i), 0, 0)),
        pl.BlockSpec(memory_space=pl.ANY),
        _const_spec((1, D_MODEL)),
        _const_spec((GROUP_W, GROUP_W)),
        _const_spec((seq_ctx, seq_ctx)),
        _const_spec((seq_ctx, seq_ctx)),
        pl.BlockSpec(memory_space=pl.ANY),
        pl.BlockSpec(memory_space=pl.ANY),
    ] + chunk_specs(prep_step, 2 * BRANCH) + [
        _const_spec((GROUP_W, GROUP_W)),
        _const_spec((2, HALF_W, HALF_W)),
    ] + chunk_specs(ctx_step, n_qkvz)
    out_specs = [
        pl.BlockSpec((n_seq, seq_ctx, D_MODEL), lambda i: (ctx_step(i), 0, 0)),
        pl.BlockSpec(memory_space=pl.ANY),
        MOD_SPEC,
        pl.BlockSpec((BRANCH + 2 * KV_W, rows), lambda i: (0, ctx_step(i))),
        pl.BlockSpec((rows, BRANCH), lambda i: (ctx_step(i), 0)),
        pl.BlockSpec((rows, D_MODEL), lambda i: (ctx_step(i), 0)),
    ]
    out_shape = [
        jax.ShapeDtypeStruct(x_ctx.shape, F32),
        jax.ShapeDtypeStruct(x_lat.shape, F32),
        jax.ShapeDtypeStruct((MOD_ROWS, 3 * D_MODEL), F32),
        jax.ShapeDtypeStruct((BRANCH + 2 * KV_W, D_MODEL), BF16),
        jax.ShapeDtypeStruct((D_MODEL, BRANCH), BF16),
        jax.ShapeDtypeStruct((BRANCH, D_MODEL), BF16),
    ]
    return pl.pallas_call(
        functools.partial(_fourier_layer_kernel, n_prep_steps=n_prep, n_ctx_steps=n_ctx),
        grid=(n_prep + n_ctx + nb_lat,),
        in_specs=in_specs,
        out_specs=out_specs,
        out_shape=out_shape,
        scratch_shapes=[
            pltpu.VMEM((MOD_ROWS, 3 * D_MODEL), F32),
            pltpu.VMEM((D_MODEL, 2 * BRANCH), BF16),
            pltpu.VMEM((BRANCH, D_MODEL), BF16),
            pltpu.VMEM((seq_lat, N_GROUPS * HALF_W), BF16),
            pltpu.VMEM((seq_lat, N_GROUPS * HALF_W), BF16),
            pltpu.VMEM((seq_lat, HALF_W), BF16),
            pltpu.VMEM((seq_lat, BRANCH), F32),
            pltpu.VMEM(x_lat.shape, F32),
            pltpu.VMEM((seq_lat, seq_lat), BF16),
            pltpu.VMEM((seq_lat, seq_lat), BF16),
            pltpu.SemaphoreType.DMA((nb_lat + 2,)),
            pltpu.VMEM(x_lat.shape, F32),
            pltpu.SemaphoreType.DMA((nb_lat, seq_lat // ROW_CHUNK)),
            pltpu.VMEM((MOD_ROWS, HALF_W), F32),
        ],
        compiler_params=pltpu.CompilerParams(
            dimension_semantics=("arbitrary",), vmem_limit_bytes=FOURIER_VMEM_LIMIT),
        name="fourier_layer",
    )(x_ctx, x_lat, nw, m1, csc, ssc, csl, ssl,
      cc, c, w_mod0, b_mod0.reshape(1, 3 * D_MODEL), w_in0, w_out0, pm, pmh,
      cc, c, w_mod1, b_mod1.reshape(1, 3 * D_MODEL), w_in1, w_out1)


def _head_weight_tile(w_ref, n_tokens):
    row = jnp.broadcast_to(w_ref[...], (HEAD_DIM, HEAD_DIM))
    ii = lax.broadcasted_iota(jnp.int32, (HEAD_DIM, HEAD_DIM), 0)
    jj = lax.broadcasted_iota(jnp.int32, (HEAD_DIM, HEAD_DIM), 1)
    col = jnp.sum(jnp.where(ii == jj, row, 0.0), axis=1, keepdims=True)
    return jnp.broadcast_to(col, (HEAD_DIM, n_tokens))


def _head_rms(t, w):
    return (t * lax.rsqrt(jnp.mean(t * t, axis=0, keepdims=True) + EPS)) * w


def _rope_t(t, cos, sin):
    half = HEAD_DIM // 2
    x1, x2 = t[:half], t[half:]
    return jnp.concatenate([x1 * cos - x2 * sin, x1 * sin + x2 * cos], axis=0)


def _head_scores(qn, g, keys, biases):
    zeros = jnp.zeros_like(qn)
    qz = jnp.concatenate([qn, zeros] if g % 2 == 0 else [zeros, qn], axis=0)
    blk = slice((g // 2) * LANES, (g // 2 + 1) * LANES)
    scores = []
    smax = None
    for k, bias in zip(keys, biases):
        s = _dot(k[:, blk], qz)
        if bias is not None:
            s = jnp.concatenate(
                [s[c * BLOCK:(c + 1) * BLOCK] if b is None else
                 s[c * BLOCK:(c + 1) * BLOCK] + jnp.concatenate([b] * (s.shape[1] // b.shape[1]), axis=1)
                 for c, b in enumerate(bias)], axis=0)
        cmax = jnp.max(s, axis=0, keepdims=True)
        smax = cmax if smax is None else jnp.maximum(smax, cmax)
        scores.append(s)
    return scores, smax


def _with_ones_rows(vt):
    return jnp.concatenate([vt, jnp.ones((ONES_ROWS, vt.shape[1]), vt.dtype)], axis=0)


def _head_probs(scored, sink2):
    scores, smax = scored
    m = jnp.maximum(smax, sink2)
    return jnp.concatenate([jnp.exp2(s - m).astype(BF16) for s in scores], axis=0), m


def _head_pv(probs, values_t, sink2):
    p, m = probs
    acc = _dot(values_t, p)
    den = acc[HEAD_DIM:HEAD_DIM + 1] + jnp.exp2(sink2 - m)
    return acc[:HEAD_DIM] * (1.0 / den)


def _attend_heads(n_units, stage, scores_fn, probs_fn, pv_fn, fillers=()):
    n_stages = n_units // stage
    pending = [scores_fn(u) for u in range(stage)]
    for g in range(n_stages):
        units = range(g * stage, (g + 1) * stage)
        if g < len(fillers):
            fillers[g]()
        nxt = [scores_fn(u) for u in range((g + 1) * stage, (g + 2) * stage)] if g + 1 < n_stages else None
        probs = [probs_fn(u, sc) for u, sc in zip(units, pending)]
        for u, pr in zip(units, probs):
            pv_fn(u, pr)
        pending = nxt


def _gate_out(x, o, z, gate, wout):
    y = (o * jax.nn.silu(z)).astype(BF16)
    return x + gate * _dot(y, wout)


def _attn_ctx_body(sink_ref, x_ref, mod, nw_ref, wqkvt_ref, wz_ref, wout_ref, qw_ref, kw_ref,
                   o_ref, kto_ref, vto_ref, qkvt_scr, z_scr, ot_scr):
    seq = x_ref.shape[1]
    n_seq = x_ref.shape[0]
    gate = mod[:, 2 * D_MODEL:]
    kw = _head_weight_tile(kw_ref, seq)
    qw = _head_weight_tile(qw_ref, seq)

    def project(i):
        h = _mod_norm(x_ref[i], nw_ref[...], mod).astype(BF16)
        qkvt_scr[i] = _dot_nt(wqkvt_ref[...], h)
        z_scr[i * seq:(i + 1) * seq, :] = _dot(h, wz_ref[...])

    def keys_values(i):
        knt = jnp.concatenate(
            [_head_rms(qkvt_scr[i, BRANCH + g * HEAD_DIM:BRANCH + (g + 1) * HEAD_DIM, :], kw)
             for g in range(N_KV)], axis=0)
        kto_ref[i] = knt
        vtf = qkvt_scr[i, BRANCH + KV_W:, :]
        vto_ref[i] = vtf
        vt = vtf.astype(BF16)
        return (knt.T.astype(BF16),
                [_with_ones_rows(vt[g * HEAD_DIM:(g + 1) * HEAD_DIM]) for g in range(N_KV)])

    def output(i):
        o_ref[i] = _gate_out(x_ref[i], ot_scr[i].T, z_scr[i * seq:(i + 1) * seq, :], gate, wout_ref[...])

    for i in range(n_seq):
        project(i)
    kv = [keys_values(i) for i in range(n_seq)]

    def scores_fn(u):
        i, hd = divmod(u, N_HEADS)
        t = qkvt_scr[i, hd * HEAD_DIM:(hd + 1) * HEAD_DIM, :]
        qn = (_head_rms(t, qw) * (HEAD_DIM ** -0.5 * LOG2E)).astype(BF16)
        return _head_scores(qn, hd // GQA, [kv[i][0]], [None])

    def probs_fn(u, sc):
        return _head_probs(sc, sink_ref[u % N_HEADS] * LOG2E)

    def pv_fn(u, pr):
        i, hd = divmod(u, N_HEADS)
        ot_scr[i, hd * HEAD_DIM:(hd + 1) * HEAD_DIM, :] = _head_pv(
            pr, kv[i][1][hd // GQA], sink_ref[hd] * LOG2E)

    fillers = [lambda: None] + [functools.partial(output, i) for i in range(n_seq - 1)]
    _attend_heads(n_seq * N_HEADS, N_HEADS, scores_fn, probs_fn, pv_fn, fillers)
    output(n_seq - 1)


def _attn_lat_body(sink_ref, x_ref, mod, nw_ref, wqkvt_ref, wz_ref, wout_ref, qw_ref, kw_ref,
                   cos_ref, sin_ref, ck_ref, cv_ref, o_ref,
                   q_scr, z_scr, ot_scr, k_scr, vt_scr, block_done=None):
    seq = x_ref.shape[1]
    gate = mod[:, 2 * D_MODEL:]
    nw = nw_ref[...]
    qw = _head_weight_tile(qw_ref, Q_BLOCK)
    kw = _head_weight_tile(kw_ref, Q_BLOCK)
    n_blocks = seq // Q_BLOCK
    kv_blocks = seq // BLOCK
    k_scr[0:BLOCK, :] = jnp.zeros((BLOCK, KV_W), BF16)
    k_scr[BLOCK + seq:2 * BLOCK + seq, :] = jnp.zeros((BLOCK, KV_W), BF16)
    vt_scr[0] = jnp.zeros((KV_W, BLOCK), BF16)
    vt_scr[kv_blocks + 1] = jnp.zeros((KV_W, BLOCK), BF16)

    def project(c):
        rows = slice(c * Q_BLOCK, (c + 1) * Q_BLOCK)
        h = _mod_norm(x_ref[0, rows, :], nw, mod).astype(BF16)
        z_scr[rows, :] = _dot(h, wz_ref[...])
        qkvt = _dot_nt(wqkvt_ref[...], h)
        cos = cos_ref[:, rows]
        sin = sin_ref[:, rows]
        for hd in range(N_HEADS):
            hr = slice(hd * HEAD_DIM, (hd + 1) * HEAD_DIM)
            t = _rope_t(_head_rms(qkvt[hr], qw), cos, sin)
            t = (t * (HEAD_DIM ** -0.5 * LOG2E)).astype(BF16)
            for j in range(HALVES):
                q_scr[c * HALVES + j, hd // 2, :, (hd % 2) * BLOCK:(hd % 2 + 1) * BLOCK] = (
                    t[:, j * BLOCK:(j + 1) * BLOCK])
        knt = jnp.concatenate(
            [_rope_t(_head_rms(qkvt[BRANCH + g * HEAD_DIM:BRANCH + (g + 1) * HEAD_DIM], kw), cos, sin)
             for g in range(N_KV)], axis=0)
        k_scr[BLOCK + c * Q_BLOCK:BLOCK + (c + 1) * Q_BLOCK, :] = knt.T.astype(BF16)
        vt = qkvt[BRANCH + KV_W:].astype(BF16)
        for j in range(HALVES):
            vt_scr[1 + c * HALVES + j] = vt[:, j * BLOCK:(j + 1) * BLOCK]

    ckb = ck_ref[0].T.astype(BF16)
    cvt = cv_ref[0].astype(BF16)

    win_len = 3 * BLOCK
    n_pairs = N_HEADS // 2
    kj = lax.broadcasted_iota(jnp.int32, (BLOCK, BLOCK), 0)
    qi = lax.broadcasted_iota(jnp.int32, (BLOCK, BLOCK), 1)
    first_head = lax.broadcasted_iota(jnp.int32, (1, 2 * BLOCK), 1) < BLOCK
    assert WINDOW >= BLOCK - 1

    def band_bias(c):
        rel = kj + (c - 1) * BLOCK - qi
        return jnp.where((rel >= -WINDOW) & (rel <= WINDOW), 0.0, NEG_INF)

    band_before, band_after = band_bias(0), band_bias(2)

    def half_operands(hb):
        r0 = pl.multiple_of(hb * BLOCK, BLOCK)
        valid = (jnp.where(hb > 0, band_before, NEG_INF), None,
                 jnp.where(hb < kv_blocks - 1, band_after, NEG_INF))
        kwin = k_scr[pl.ds(r0, win_len), :]
        vall = jnp.concatenate([vt_scr[hb + j] for j in range(win_len // BLOCK)] + [cvt], axis=1)
        valls = [_with_ones_rows(vall[g * HEAD_DIM:(g + 1) * HEAD_DIM]) for g in range(N_KV)]
        return valid, kwin, valls

    def pair_sink2(pair):
        return jnp.where(first_head, sink_ref[2 * pair], sink_ref[2 * pair + 1]) * LOG2E

    def attend(n, carry):
        halves = [half_operands(n * HALVES + j) for j in range(HALVES)]

        def scores_fn(u):
            j, pair = divmod(u, n_pairs)
            valid, kwin, _ = halves[j]
            return _head_scores(q_scr[n * HALVES + j, pair], pair // (GQA // 2), [kwin, ckb], [valid, None])

        def probs_fn(u, sc):
            return _head_probs(sc, pair_sink2(u % n_pairs))

        def pv_fn(u, pr):
            j, pair = divmod(u, n_pairs)
            o = _head_pv(pr, halves[j][2][pair // (GQA // 2)], pair_sink2(pair))
            for i in range(2):
                hd = 2 * pair + i
                ot_scr[n, hd * HEAD_DIM:(hd + 1) * HEAD_DIM, j * BLOCK:(j + 1) * BLOCK] = (
                    o[:, i * BLOCK:(i + 1) * BLOCK])

        _attend_heads(HALVES * n_pairs, GQA // 2, scores_fn, probs_fn, pv_fn)
        return carry

    for c in range(n_blocks):
        project(c)
    assert n_blocks % 2 == 0
    lax.fori_loop(0, n_blocks // 2, lambda n, carry: attend(2 * n + 1, attend(2 * n, carry)), 0)
    for c in range(n_blocks):
        rows = slice(c * Q_BLOCK, (c + 1) * Q_BLOCK)
        o_ref[0, rows, :] = _gate_out(x_ref[0, rows, :], ot_scr[c].T, z_scr[rows, :], gate, wout_ref[...])
        if block_done is not None:
            block_done(c)


def _rope_tables_t(seq):
    pos = np.arange(seq)
    n_freq = HEAD_DIM // 4
    inv = ROPE_THETA ** (-np.arange(n_freq, dtype=np.float64) / n_freq)
    ang = np.concatenate([(pos // GRID_W)[:, None] * inv, (pos % GRID_W)[:, None] * inv], axis=-1)
    return np.cos(ang).T.astype(np.float32), np.sin(ang).T.astype(np.float32)


def _attn_layer_kernel(sink_ref, xc_ref, xl_ref, mod_ref, nw_ref, wqkvt_ref, wz_ref, wout_ref, qw_ref, kw_ref,
                       _qw_pad_ref, _kw_pad_ref, cos_ref, sin_ref, ck_ref, cv_ref, oc_ref, kto_ref, vto_ref, ol_hbm,
                       qkvt_scr, q_scr, z_scr, ot_scr, k_scr, vt_scr, ol_scr, out_sem, *, n_ctx_steps):
    step = pl.program_id(0)
    n_lat = ol_scr.shape[0]
    lat = step - n_ctx_steps

    def result_copy(r, c):
        rows = pl.ds(c * Q_BLOCK, Q_BLOCK)
        return pltpu.make_async_copy(ol_scr.at[r, rows], ol_hbm.at[r, rows], out_sem.at[r, c])

    @pl.when(step < n_ctx_steps)
    def _():
        _attn_ctx_body(sink_ref, xc_ref, mod_ref[0:1, :], nw_ref, wqkvt_ref, wz_ref, wout_ref, qw_ref, kw_ref,
                       oc_ref, kto_ref, vto_ref, qkvt_scr, z_scr, ot_scr)

    @pl.when(step >= n_ctx_steps)
    def _():
        mod = mod_ref[pl.ds(1 + lat, 1), :]
        _attn_lat_body(sink_ref, xl_ref, mod, nw_ref, wqkvt_ref, wz_ref, wout_ref, qw_ref, kw_ref,
                       cos_ref, sin_ref, ck_ref, cv_ref, ol_scr.at[pl.ds(lat, 1)],
                       q_scr, z_scr, ot_scr, k_scr, vt_scr,
                       block_done=lambda c: result_copy(lat, c).start())

        @pl.when(lat == n_lat - 1)
        def _():
            for r in range(n_lat):
                for c in range(out_sem.shape[1]):
                    result_copy(r, c).wait()


def _attn_layer(x_ctx, x_lat, mod, nw, wqkvt, wz, wout, qw, kw, sink, ckt, cvt):
    nb_ctx, seq_ctx, _ = x_ctx.shape
    nb_lat, seq_lat, _ = x_lat.shape
    past = ckt.shape[2]
    n_seq = CTX_SEQS_PER_STEP
    assert nb_ctx % n_seq == 0 and seq_ctx == Q_BLOCK and n_seq <= seq_lat // Q_BLOCK
    n_ctx = nb_ctx // n_seq
    cos, sin = (jnp.asarray(t) for t in _rope_tables_t(seq_lat))

    def ctx_step(i):
        return jnp.minimum(i, n_ctx - 1)

    def lat_step(i):
        return jnp.maximum(i - n_ctx, 0)

    return pl.pallas_call(
        functools.partial(_attn_layer_kernel, n_ctx_steps=n_ctx),
        grid=(n_ctx + nb_lat,),
        in_specs=[
            pl.BlockSpec(memory_space=pltpu.SMEM),
            pl.BlockSpec((n_seq, seq_ctx, D_MODEL), lambda i: (ctx_step(i), 0, 0)),
            pl.BlockSpec((1, seq_lat, D_MODEL), lambda i: (lat_step(i), 0, 0)),
            MOD_SPEC,
            _const_spec((1, D_MODEL)),
            _const_spec((BRANCH + 2 * KV_W, D_MODEL)),
            _const_spec((D_MODEL, BRANCH)),
            _const_spec((BRANCH, D_MODEL)),
            _const_spec((1, HEAD_DIM)),
            _const_spec((1, HEAD_DIM)),
            _const_spec((1, HEAD_DIM)),
            _const_spec((1, HEAD_DIM)),
            _const_spec((HEAD_DIM // 2, seq_lat)),
            _const_spec((HEAD_DIM // 2, seq_lat)),
            pl.BlockSpec((1, KV_W, past), lambda i: (lat_step(i), 0, 0)),
            pl.BlockSpec((1, KV_W, past), lambda i: (lat_step(i), 0, 0)),
        ],
        out_specs=[
            pl.BlockSpec((n_seq, seq_ctx, D_MODEL), lambda i: (ctx_step(i), 0, 0)),
            pl.BlockSpec((n_seq, KV_W, seq_ctx), lambda i: (ctx_step(i), 0, 0)),
            pl.BlockSpec((n_seq, KV_W, seq_ctx), lambda i: (ctx_step(i), 0, 0)),
            pl.BlockSpec(memory_space=pl.ANY),
        ],
        out_shape=[
            jax.ShapeDtypeStruct(x_ctx.shape, F32),
            jax.ShapeDtypeStruct((nb_ctx, KV_W, seq_ctx), F32),
            jax.ShapeDtypeStruct((nb_ctx, KV_W, seq_ctx), F32),
            jax.ShapeDtypeStruct(x_lat.shape, F32),
        ],
        scratch_shapes=[
            pltpu.VMEM((n_seq, BRANCH + 2 * KV_W, seq_ctx), F32),
            pltpu.VMEM((seq_lat // BLOCK, N_HEADS // 2, HEAD_DIM, 2 * BLOCK), BF16),
            pltpu.VMEM((seq_lat, BRANCH), F32),
            pltpu.VMEM((seq_lat // Q_BLOCK, BRANCH, Q_BLOCK), F32),
            pltpu.VMEM((seq_lat + 2 * BLOCK, KV_W), BF16),
            pltpu.VMEM((seq_lat // BLOCK + 2, KV_W, BLOCK), BF16),
            pltpu.VMEM(x_lat.shape, F32),
            pltpu.SemaphoreType.DMA((nb_lat, seq_lat // Q_BLOCK)),
        ],
        compiler_params=pltpu.CompilerParams(
            dimension_semantics=("arbitrary",), vmem_limit_bytes=VMEM_LIMIT),
        name="attn_layer",
    )(sink, x_ctx, x_lat, mod, nw, wqkvt, wz, wout, qw, kw, qw, kw, cos, sin, ckt, cvt)


def kernel(x_prompt, x_sample, cache_k_l1, cache_v_l1, c, c_ctx, norm_w_l0, w_mod_l0, b_mod_l0,
           w_in_l0, w_out_l0, norm_w_l1, w_mod_l1, b_mod_l1, w_in_l1, q_norm_w_l1, k_norm_w_l1,
           sink_l1, w_out_l1):
    nb_ctx, seq_ctx, _ = x_prompt.shape
    nb_lat = x_sample.shape[0]
    past = cache_k_l1.shape[1]
    assert 1 + nb_lat <= MOD_ROWS
    nw0 = norm_w_l0.reshape(1, D_MODEL)
    nw1 = norm_w_l1.reshape(1, D_MODEL)
    qw = q_norm_w_l1.reshape(1, HEAD_DIM)
    kw = k_norm_w_l1.reshape(1, HEAD_DIM)

    xp, xs, mod1, wqkvt1, wz1, wout1 = _fourier_layer(
        x_prompt, x_sample, nw0,
        this_layer=(c_ctx, c, w_mod_l0, b_mod_l0, w_in_l0, w_out_l0),
        next_layer=(c_ctx, c, w_mod_l1, b_mod_l1, w_in_l1, w_out_l1))

    def to_feature_major(t):
        return jnp.transpose(t, (0, 2, 3, 1)).reshape(t.shape[0], KV_W, t.shape[1])

    def from_feature_major(t):
        return jnp.transpose(t.reshape(t.shape[0], N_KV, HEAD_DIM, t.shape[2]), (0, 3, 1, 2))

    xp, new_kt, new_vt, xs = _attn_layer(xp, xs, mod1, nw1, wqkvt1, wz1, wout1, qw, kw, sink_l1,
                                         to_feature_major(cache_k_l1), to_feature_major(cache_v_l1))
    return (xp, xs, from_feature_major(new_kt), from_feature_major(new_vt))
```

```python
import functools

import numpy as np
import jax
import jax.numpy as jnp
from jax import lax
from jax.experimental import pallas as pl
from jax.experimental.pallas import tpu as pltpu

D_MODEL = 1024
BRANCH = 1024
N_GROUPS = 4
GROUP_W = BRANCH // N_GROUPS
HALF_W = GROUP_W // 2
HEAD_DIM = 64
N_HEADS = 16
N_KV = 4
GQA = N_HEADS // N_KV
KV_W = N_KV * HEAD_DIM
GRID_W = 64
WINDOW = 128
BLOCK = 128
ROPE_THETA = 10000.0
EPS = 1e-6
NEG_INF = -1e30
LANES = 128
ROW_CHUNK = 256
Q_BLOCK = 256
HALVES = Q_BLOCK // BLOCK
CTX_SEQS_PER_STEP = 2
VMEM_LIMIT = 56 * 1024 * 1024
FOURIER_VMEM_LIMIT = 60 * 1024 * 1024
MOD_ROWS = 8
ONES_ROWS = 16
LOG2E = float(np.log2(np.e))

F32 = jnp.float32
BF16 = jnp.bfloat16


def _dot(a, b):
    return jnp.dot(a, b, preferred_element_type=F32)


def _dot_nt(a, b):
    return lax.dot_general(a, b, (((1,), (1,)), ((), ())), preferred_element_type=F32)


MOD_SPEC = pl.BlockSpec((MOD_ROWS, 3 * D_MODEL), lambda b: (0, 0))


def _mod_norm(x, nw, mod):
    shift = mod[:, :D_MODEL]
    scale = mod[:, D_MODEL:2 * D_MODEL]
    y = x * lax.rsqrt(jnp.mean(x * x, axis=-1, keepdims=True) + EPS)
    return (y * nw) * (1.0 + scale) + shift


def _mod_accumulate(cctx_ref, c_ref, w_ref, b_ref, o_ref, cond_scr, first_step):
    n_lat = c_ref.shape[0]
    cond_scr[...] = jnp.zeros_like(cond_scr)
    cond_scr[0:1, :] = cctx_ref[...]
    cond_scr[1:1 + n_lat, :] = c_ref[...]
    s = jax.nn.silu(cond_scr[...]).astype(BF16)

    @pl.when(pl.program_id(0) == first_step)
    def _():
        o_ref[...] = jnp.broadcast_to(b_ref[...], o_ref.shape)

    o_ref[...] += _dot(s, w_ref[...].astype(BF16))


def _mirror_perm():
    j = np.arange(GROUP_W)
    return np.where(j <= HALF_W, j, GROUP_W + HALF_W - j)


def _perm_matrices():
    perm = _mirror_perm()
    pm = (np.arange(GROUP_W)[:, None] == perm[None, :]).astype(np.float32)
    assert (pm[:HALF_W, HALF_W:] == 0).all() and (pm[HALF_W:, :HALF_W] == 0).all()
    return pm, np.stack([pm[:HALF_W, :HALF_W], pm[HALF_W:, HALF_W:]])


def _fourier_layer_kernel(xc_ref, xl_hbm, nw_ref, m1_ref, csc_ref, ssc_ref, csl_hbm, ssl_hbm,
                          cctx0_ref, c0_ref, wmod0_ref, bmod0_ref, win0_ref, wout0_ref, pm_ref, pmh_ref,
                          cctx_ref, c_ref, wmod1_ref, bmod1_ref, win1_ref, wout1_ref,
                          oc_ref, ol_hbm, mod1_ref, wqkvt1_ref, wz1_ref, wout1b_ref,
                          mod0_scr, win_scr, wout_scr, ta_scr, tb_scr, tr_scr, z_scr,
                          xl_scr, csl_scr, ssl_scr, copy_sem, ol_scr, out_sem, cond_scr,
                          *, n_prep_steps, n_ctx_steps):
    step = pl.program_id(0)
    n_lat = xl_scr.shape[0]
    lat = step - (n_prep_steps + n_ctx_steps)

    def request_copy(r):
        return pltpu.make_async_copy(xl_hbm.at[r], xl_scr.at[r], copy_sem.at[r])

    table_copies = [pltpu.make_async_copy(csl_hbm, csl_scr, copy_sem.at[n_lat]),
                    pltpu.make_async_copy(ssl_hbm, ssl_scr, copy_sem.at[n_lat + 1])]

    def result_copy(r, c):
        rows = pl.ds(c * ROW_CHUNK, ROW_CHUNK)
        return pltpu.make_async_copy(ol_scr.at[r, rows], ol_hbm.at[r, rows], out_sem.at[r, c])

    @pl.when(step == n_prep_steps)
    def _():
        for copy in table_copies + [request_copy(r) for r in range(n_lat)]:
            copy.start(priority=1)

    @pl.when(step < n_prep_steps)
    def _():
        _mod_accumulate(cctx0_ref, c0_ref, wmod0_ref, bmod0_ref, mod0_scr, cond_scr, 0)
        rows = pl.ds(pl.multiple_of(step * HALF_W, HALF_W), HALF_W)
        w = win0_ref[...].astype(BF16)
        win_scr[rows, :BRANCH] = w[:, :BRANCH]
        pm = pm_ref[...]
        for g in range(N_GROUPS):
            cols = slice(BRANCH + g * GROUP_W, BRANCH + (g + 1) * GROUP_W)
            win_scr[rows, cols] = _dot(w[:, cols], pm).astype(BF16)
        wout_scr[rows, :] = _dot(pmh_ref[step % 2], wout0_ref[...].astype(BF16)).astype(BF16)

    @pl.when(jnp.logical_and(step >= n_prep_steps, step < n_prep_steps + n_ctx_steps))
    def _():
        _mod_accumulate(cctx_ref, c_ref, wmod1_ref, bmod1_ref, mod1_ref, cond_scr, n_prep_steps)
        w = win1_ref[...]
        wqkvt1_ref[...] = w[:, :BRANCH + 2 * KV_W].T.astype(BF16)
        wz1_ref[...] = w[:, BRANCH + 2 * KV_W:].astype(BF16)
        wout1b_ref[...] = wout1_ref[...].astype(BF16)
        _fourier_body(xc_ref, oc_ref, mod0_scr[0:1, :], nw_ref, win_scr, wout_scr, m1_ref, csc_ref, ssc_ref,
                      ta_scr, tb_scr, tr_scr, z_scr)

    @pl.when(step >= n_prep_steps + n_ctx_steps)
    def _():
        @pl.when(lat == 0)
        def _():
            for copy in table_copies:
                copy.wait()

        request_copy(lat).wait()
        mod = mod0_scr[pl.ds(1 + lat, 1), :]
        _fourier_body(xl_scr.at[pl.ds(lat, 1)], ol_scr.at[pl.ds(lat, 1)], mod, nw_ref, win_scr, wout_scr,
                      m1_ref, csl_scr, ssl_scr, ta_scr, tb_scr, tr_scr, z_scr,
                      rows_done=lambda _, rows: result_copy(lat, rows.start // ROW_CHUNK).start())

        @pl.when(lat == n_lat - 1)
        def _():
            for r in range(n_lat):
                for c in range(out_sem.shape[1]):
                    result_copy(r, c).wait()


def _fourier_body(x_ref, o_ref, mod, nw_ref, win_ref, wout_ref, m1_ref, cs_ref, ss_ref,
                  ta_scr, tb_scr, tr_scr, z_scr, rows_done=None):
    gate = mod[:, 2 * D_MODEL:]
    nw = nw_ref[...]
    n_seq, seq, _ = x_ref.shape
    n_chunks = seq // ROW_CHUNK
    lane = lax.broadcasted_iota(jnp.int32, (ROW_CHUNK, HALF_W), 1)
    for i in range(n_seq):
        for c in range(n_chunks):
            rows = slice(c * ROW_CHUNK, (c + 1) * ROW_CHUNK)
            srows = slice(i * seq + c * ROW_CHUNK, i * seq + (c + 1) * ROW_CHUNK)
            h = _mod_norm(x_ref[i, rows, :], nw, mod).astype(BF16)
            uz = _dot(h, win_ref[...])
            z_scr[srows, :] = uz[:, BRANCH:]
            u = uz[:, :BRANCH].astype(BF16)
            tr = jnp.zeros((ROW_CHUNK, HALF_W), F32)
            for g in range(N_GROUPS):
                t = _dot(u[:, g * GROUP_W:(g + 1) * GROUP_W], m1_ref[...])
                half = slice(g * HALF_W, (g + 1) * HALF_W)
                ta_scr[srows, half] = t[:, :HALF_W].astype(BF16)
                tb = t[:, HALF_W:]
                tb_scr[srows, half] = tb.astype(BF16)
                tr = jnp.where(lane == g, tb if g == 0 else pltpu.roll(tb, g, axis=1), tr)
            tr_scr[srows, :] = tr.astype(BF16)
    for i in range(n_seq):
        seq_rows = slice(i * seq, (i + 1) * seq)
        for c in range(n_chunks):
            rows = slice(c * ROW_CHUNK, (c + 1) * ROW_CHUNK)
            srows = slice(i * seq + c * ROW_CHUNK, i * seq + (c + 1) * ROW_CHUNK)
            cs = cs_ref[rows, :]
            p = _dot(cs, ta_scr[seq_rows, :])
            q = _dot(ss_ref[rows, :], tb_scr[seq_rows, :])
            r = _dot(cs, tr_scr[seq_rows, :])
            parts = []
            for g in range(N_GROUPS):
                half = slice(g * HALF_W, (g + 1) * HALF_W)
                pg, qg = p[:, half], q[:, half]
                rg = r if g == 0 else pltpu.roll(r, HALF_W - g, axis=1)
                parts.append(jnp.where(lane == 0, pg, pg - qg))
                parts.append(jnp.where(lane == 0, rg, pg + qg))
            y = jnp.concatenate(parts, axis=1)
            y = (y * jax.nn.silu(z_scr[srows, :])).astype(BF16)
            o_ref[i, rows, :] = x_ref[i, rows, :] + gate * _dot(y, wout_ref[...])
            if rows_done is not None:
                rows_done(i, rows)


def _dft_tables(seq):
    c = np.arange(GROUP_W)[:, None]
    k = np.arange(HALF_W)[None, :]
    cos_lo = np.cos(2.0 * np.pi * ((c * k) % GROUP_W) / GROUP_W)
    sin_lo = np.sin(2.0 * np.pi * ((c * k) % GROUP_W) / GROUP_W)
    sin_lo[:, 0] = np.cos(np.pi * c[:, 0])
    m1 = np.concatenate([cos_lo, sin_lo], axis=1) / np.sqrt(GROUP_W)
    n = np.arange(seq)
    ang = 2.0 * np.pi * ((n[:, None] * n[None, :]) % seq) / seq
    cs = np.cos(ang) / np.sqrt(seq)
    ss = np.sin(ang) / np.sqrt(seq)
    return m1.astype(np.float32), cs.astype(np.float32), ss.astype(np.float32)


def _const_spec(shape):
    return pl.BlockSpec(shape, lambda b: (0,) * len(shape))


def _fourier_layer(x_ctx, x_lat, nw, this_layer, next_layer):
    nb_ctx, seq_ctx, _ = x_ctx.shape
    nb_lat, seq_lat, _ = x_lat.shape
    n_seq = CTX_SEQS_PER_STEP
    assert nb_ctx % n_seq == 0
    n_prep = D_MODEL // HALF_W
    n_ctx = nb_ctx // n_seq
    assert D_MODEL % (n_ctx * LANES) == 0 and n_seq * seq_ctx <= seq_lat
    rows = D_MODEL // n_ctx
    assert rows == HALF_W
    c_ctx, c, w_mod0, b_mod0, w_in0, w_out0 = this_layer
    _, _, w_mod1, b_mod1, w_in1, w_out1 = next_layer
    n_lat = c.shape[0]
    n_qkvz = 2 * BRANCH + 2 * KV_W
    m1, csc, ssc = (jnp.asarray(t).astype(BF16) for t in _dft_tables(seq_ctx))
    _, csl, ssl = (jnp.asarray(t).astype(BF16) for t in _dft_tables(seq_lat))
    pm, pmh = (jnp.asarray(t).astype(BF16) for t in _perm_matrices())
    cc = c_ctx.reshape(1, D_MODEL)

    def prep_step(i):
        return jnp.minimum(i, n_prep - 1)

    def ctx_step(i):
        return jnp.clip(i - n_prep, 0, n_ctx - 1)

    def chunk_specs(step_fn, w_in_cols):
        return [pl.BlockSpec((1, HALF_W), lambda i: (0, step_fn(i))),
                pl.BlockSpec((n_lat, HALF_W), lambda i: (0, step_fn(i))),
                pl.BlockSpec((HALF_W, 3 * D_MODEL), lambda i: (step_fn(i), 0)),
                _const_spec((1, 3 * D_MODEL)),
                pl.BlockSpec((HALF_W, w_in_cols), lambda i: (step_fn(i), 0)),
                pl.BlockSpec((HALF_W, D_MODEL), lambda i: (step_fn(i), 0))]

    in_specs = [
        pl.BlockSpec((n_seq, seq_ctx, D_MODEL), lambda i: (ctx_step(i), 0, 0)),
        pl.BlockSpec(memory_space=pl.ANY),
        _const_spec((1, D_MODEL)),
        _const_spec((GROUP_W, GROUP_W)),
        _const_spec((seq_ctx, seq_ctx)),
        _const_spec((seq_ctx, seq_ctx)),
        pl.BlockSpec(memory_space=pl.ANY),
        pl.BlockSpec(memory_space=pl.ANY),
    ] + chunk_specs(prep_step, 2 * BRANCH) + [
        _const_spec((GROUP_W, GROUP_W)),
        _const_spec((2, HALF_W, HALF_W)),
    ] + chunk_specs(ctx_step, n_qkvz)
    out_specs = [
        pl.BlockSpec((n_seq, seq_ctx, D_MODEL), lambda i: (ctx_step(i), 0, 0)),
        pl.BlockSpec(memory_space=pl.ANY),
        MOD_SPEC,
        pl.BlockSpec((BRANCH + 2 * KV_W, rows), lambda i: (0, ctx_step(i))),
        pl.BlockSpec((rows, BRANCH), lambda i: (ctx_step(i), 0)),
        pl.BlockSpec((rows, D_MODEL), lambda i: (ctx_step(i), 0)),
    ]
    out_shape = [
        jax.ShapeDtypeStruct(x_ctx.shape, F32),
        jax.ShapeDtypeStruct(x_lat.shape, F32),
        jax.ShapeDtypeStruct((MOD_ROWS, 3 * D_MODEL), F32),
        jax.ShapeDtypeStruct((BRANCH + 2 * KV_W, D_MODEL), BF16),
        jax.ShapeDtypeStruct((D_MODEL, BRANCH), BF16),
        jax.ShapeDtypeStruct((BRANCH, D_MODEL), BF16),
    ]
    return pl.pallas_call(
        functools.partial(_fourier_layer_kernel, n_prep_steps=n_prep, n_ctx_steps=n_ctx),
        grid=(n_prep + n_ctx + nb_lat,),
        in_specs=in_specs,
        out_specs=out_specs,
        out_shape=out_shape,
        scratch_shapes=[
            pltpu.VMEM((MOD_ROWS, 3 * D_MODEL), F32),
            pltpu.VMEM((D_MODEL, 2 * BRANCH), BF16),
            pltpu.VMEM((BRANCH, D_MODEL), BF16),
            pltpu.VMEM((seq_lat, N_GROUPS * HALF_W), BF16),
            pltpu.VMEM((seq_lat, N_GROUPS * HALF_W), BF16),
            pltpu.VMEM((seq_lat, HALF_W), BF16),
            pltpu.VMEM((seq_lat, BRANCH), F32),
            pltpu.VMEM(x_lat.shape, F32),
            pltpu.VMEM((seq_lat, seq_lat), BF16),
            pltpu.VMEM((seq_lat, seq_lat), BF16),
            pltpu.SemaphoreType.DMA((nb_lat + 2,)),
            pltpu.VMEM(x_lat.shape, F32),
            pltpu.SemaphoreType.DMA((nb_lat, seq_lat // ROW_CHUNK)),
            pltpu.VMEM((MOD_ROWS, HALF_W), F32),
        ],
        compiler_params=pltpu.CompilerParams(
            dimension_semantics=("arbitrary",), vmem_limit_bytes=FOURIER_VMEM_LIMIT),
        name="fourier_layer",
    )(x_ctx, x_lat, nw, m1, csc, ssc, csl, ssl,
      cc, c, w_mod0, b_mod0.reshape(1, 3 * D_MODEL), w_in0, w_out0, pm, pmh,
      cc, c, w_mod1, b_mod1.reshape(1, 3 * D_MODEL), w_in1, w_out1)


def _head_weight_tile(w_ref, n_tokens):
    row = jnp.broadcast_to(w_ref[...], (HEAD_DIM, HEAD_DIM))
    ii = lax.broadcasted_iota(jnp.int32, (HEAD_DIM, HEAD_DIM), 0)
    jj = lax.broadcasted_iota(jnp.int32, (HEAD_DIM, HEAD_DIM), 1)
    col = jnp.sum(jnp.where(ii == jj, row, 0.0), axis=1, keepdims=True)
    return jnp.broadcast_to(col, (HEAD_DIM, n_tokens))


def _head_rms(t, w):
    return (t * lax.rsqrt(jnp.mean(t * t, axis=0, keepdims=True) + EPS)) * w


def _rope_t(t, cos, sin):
    half = HEAD_DIM // 2
    x1, x2 = t[:half], t[half:]
    return jnp.concatenate([x1 * cos - x2 * sin, x1 * sin + x2 * cos], axis=0)


def _head_scores(qn, g, keys, biases):
    zeros = jnp.zeros_like(qn)
    qz = jnp.concatenate([qn, zeros] if g % 2 == 0 else [zeros, qn], axis=0)
    blk = slice((g // 2) * LANES, (g // 2 + 1) * LANES)
    scores = []
    smax = None
    for k, bias in zip(keys, biases):
        s = _dot(k[:, blk], qz)
        if bias is not None:
            s = jnp.concatenate(
                [s[c * BLOCK:(c + 1) * BLOCK] if b is None else
                 s[c * BLOCK:(c + 1) * BLOCK] + jnp.concatenate([b] * (s.shape[1] // b.shape[1]), axis=1)
                 for c, b in enumerate(bias)], axis=0)
        cmax = jnp.max(s, axis=0, keepdims=True)
        smax = cmax if smax is None else jnp.maximum(smax, cmax)
        scores.append(s)
    return scores, smax


def _with_ones_rows(vt):
    return jnp.concatenate([vt, jnp.ones((ONES_ROWS, vt.shape[1]), vt.dtype)], axis=0)


def _head_probs(scored, sink2):
    scores, smax = scored
    m = jnp.maximum(smax, sink2)
    return jnp.concatenate([jnp.exp2(s - m).astype(BF16) for s in scores], axis=0), m


def _head_pv(probs, values_t, sink2):
    p, m = probs
    acc = _dot(values_t, p)
    den = acc[HEAD_DIM:HEAD_DIM + 1] + jnp.exp2(sink2 - m)
    return acc[:HEAD_DIM] * (1.0 / den)


def _attend_heads(n_units, stage, scores_fn, probs_fn, pv_fn, fillers=()):
    n_stages = n_units // stage
    pending = [scores_fn(u) for u in range(stage)]
    for g in range(n_stages):
        units = range(g * stage, (g + 1) * stage)
        if g < len(fillers):
            fillers[g]()
        nxt = [scores_fn(u) for u in range((g + 1) * stage, (g + 2) * stage)] if g + 1 < n_stages else None
        probs = [probs_fn(u, sc) for u, sc in zip(units, pending)]
        for u, pr in zip(units, probs):
            pv_fn(u, pr)
        pending = nxt


def _gate_out(x, o, z, gate, wout):
    y = (o * jax.nn.silu(z)).astype(BF16)
    return x + gate * _dot(y, wout)


def _attn_ctx_body(sink_ref, x_ref, mod, nw_ref, wqkvt_ref, wz_ref, wout_ref, qw_ref, kw_ref,
                   o_ref, kto_ref, vto_ref, qkvt_scr, z_scr, ot_scr):
    seq = x_ref.shape[1]
    n_seq = x_ref.shape[0]
    gate = mod[:, 2 * D_MODEL:]
    kw = _head_weight_tile(kw_ref, seq)
    qw = _head_weight_tile(qw_ref, seq)

    def project(i):
        h = _mod_norm(x_ref[i], nw_ref[...], mod).astype(BF16)
        qkvt_scr[i] = _dot_nt(wqkvt_ref[...], h)
        z_scr[i * seq:(i + 1) * seq, :] = _dot(h, wz_ref[...])

    def keys_values(i):
        knt = jnp.concatenate(
            [_head_rms(qkvt_scr[i, BRANCH + g * HEAD_DIM:BRANCH + (g + 1) * HEAD_DIM, :], kw)
             for g in range(N_KV)], axis=0)
        kto_ref[i] = knt
        vtf = qkvt_scr[i, BRANCH + KV_W:, :]
        vto_ref[i] = vtf
        vt = vtf.astype(BF16)
        return (knt.T.astype(BF16),
                [_with_ones_rows(vt[g * HEAD_DIM:(g + 1) * HEAD_DIM]) for g in range(N_KV)])

    def output(i):
        o_ref[i] = _gate_out(x_ref[i], ot_scr[i].T, z_scr[i * seq:(i + 1) * seq, :], gate, wout_ref[...])

    for i in range(n_seq):
        project(i)
    kv = [keys_values(i) for i in range(n_seq)]

    def scores_fn(u):
        i, hd = divmod(u, N_HEADS)
        t = qkvt_scr[i, hd * HEAD_DIM:(hd + 1) * HEAD_DIM, :]
        qn = (_head_rms(t, qw) * (HEAD_DIM ** -0.5 * LOG2E)).astype(BF16)
        return _head_scores(qn, hd // GQA, [kv[i][0]], [None])

    def probs_fn(u, sc):
        return _head_probs(sc, sink_ref[u % N_HEADS] * LOG2E)

    def pv_fn(u, pr):
        i, hd = divmod(u, N_HEADS)
        ot_scr[i, hd * HEAD_DIM:(hd + 1) * HEAD_DIM, :] = _head_pv(
            pr, kv[i][1][hd // GQA], sink_ref[hd] * LOG2E)

    fillers = [lambda: None] + [functools.partial(output, i) for i in range(n_seq - 1)]
    _attend_heads(n_seq * N_HEADS, N_HEADS, scores_fn, probs_fn, pv_fn, fillers)
    output(n_seq - 1)


def _attn_lat_body(sink_ref, x_ref, mod, nw_ref, wqkvt_ref, wz_ref, wout_ref, qw_ref, kw_ref,
                   cos_ref, sin_ref, ck_ref, cv_ref, o_ref,
                   q_scr, z_scr, ot_scr, k_scr, vt_scr, block_done=None):
    seq = x_ref.shape[1]
    gate = mod[:, 2 * D_MODEL:]
    nw = nw_ref[...]
    qw = _head_weight_tile(qw_ref, Q_BLOCK)
    kw = _head_weight_tile(kw_ref, Q_BLOCK)
    n_blocks = seq // Q_BLOCK
    kv_blocks = seq // BLOCK
    k_scr[0:BLOCK, :] = jnp.zeros((BLOCK, KV_W), BF16)
    k_scr[BLOCK + seq:2 * BLOCK + seq, :] = jnp.zeros((BLOCK, KV_W), BF16)
    vt_scr[0] = jnp.zeros((KV_W, BLOCK), BF16)
    vt_scr[kv_blocks + 1] = jnp.zeros((KV_W, BLOCK), BF16)

    def project(c):
        rows = slice(c * Q_BLOCK, (c + 1) * Q_BLOCK)
        h = _mod_norm(x_ref[0, rows, :], nw, mod).astype(BF16)
        z_scr[rows, :] = _dot(h, wz_ref[...])
        qkvt = _dot_nt(wqkvt_ref[...], h)
        cos = cos_ref[:, rows]
        sin = sin_ref[:, rows]
        for hd in range(N_HEADS):
            hr = slice(hd * HEAD_DIM, (hd + 1) * HEAD_DIM)
            t = _rope_t(_head_rms(qkvt[hr], qw), cos, sin)
            t = (t * (HEAD_DIM ** -0.5 * LOG2E)).astype(BF16)
            for j in range(HALVES):
                q_scr[c * HALVES + j, hd // 2, :, (hd % 2) * BLOCK:(hd % 2 + 1) * BLOCK] = (
                    t[:, j * BLOCK:(j + 1) * BLOCK])
        knt = jnp.concatenate(
            [_rope_t(_head_rms(qkvt[BRANCH + g * HEAD_DIM:BRANCH + (g + 1) * HEAD_DIM], kw), cos, sin)
             for g in range(N_KV)], axis=0)
        k_scr[BLOCK + c * Q_BLOCK:BLOCK + (c + 1) * Q_BLOCK, :] = knt.T.astype(BF16)
        vt = qkvt[BRANCH + KV_W:].astype(BF16)
        for j in range(HALVES):
            vt_scr[1 + c * HALVES + j] = vt[:, j * BLOCK:(j + 1) * BLOCK]

    ckb = ck_ref[0].T.astype(BF16)
    cvt = cv_ref[0].astype(BF16)

    win_len = 3 * BLOCK
    n_pairs = N_HEADS // 2
    kj = lax.broadcasted_iota(jnp.int32, (BLOCK, BLOCK), 0)
    qi = lax.broadcasted_iota(jnp.int32, (BLOCK, BLOCK), 1)
    first_head = lax.broadcasted_iota(jnp.int32, (1, 2 * BLOCK), 1) < BLOCK
    assert WINDOW >= BLOCK - 1

    def band_bias(c):
        rel = kj + (c - 1) * BLOCK - qi
        return jnp.where((rel >= -WINDOW) & (rel <= WINDOW), 0.0, NEG_INF)

    band_before, band_after = band_bias(0), band_bias(2)

    def half_operands(hb):
        r0 = pl.multiple_of(hb * BLOCK, BLOCK)
        valid = (jnp.where(hb > 0, band_before, NEG_INF), None,
                 jnp.where(hb < kv_blocks - 1, band_after, NEG_INF))
        kwin = k_scr[pl.ds(r0, win_len), :]
        vall = jnp.concatenate([vt_scr[hb + j] for j in range(win_len // BLOCK)] + [cvt], axis=1)
        valls = [_with_ones_rows(vall[g * HEAD_DIM:(g + 1) * HEAD_DIM]) for g in range(N_KV)]
        return valid, kwin, valls

    def pair_sink2(pair):
        return jnp.where(first_head, sink_ref[2 * pair], sink_ref[2 * pair + 1]) * LOG2E

    def attend(n, carry):
        halves = [half_operands(n * HALVES + j) for j in range(HALVES)]

        def scores_fn(u):
            j, pair = divmod(u, n_pairs)
            valid, kwin, _ = halves[j]
            return _head_scores(q_scr[n * HALVES + j, pair], pair // (GQA // 2), [kwin, ckb], [valid, None])

        def probs_fn(u, sc):
            return _head_probs(sc, pair_sink2(u % n_pairs))

        def pv_fn(u, pr):
            j, pair = divmod(u, n_pairs)
            o = _head_pv(pr, halves[j][2][pair // (GQA // 2)], pair_sink2(pair))
            for i in range(2):
                hd = 2 * pair + i
                ot_scr[n, hd * HEAD_DIM:(hd + 1) * HEAD_DIM, j * BLOCK:(j + 1) * BLOCK] = (
                    o[:, i * BLOCK:(i + 1) * BLOCK])

        _attend_heads(HALVES * n_pairs, GQA // 2, scores_fn, probs_fn, pv_fn)
        return carry

    for c in range(n_blocks):
        project(c)
    assert n_blocks % 2 == 0
    lax.fori_loop(0, n_blocks // 2, lambda n, carry: attend(2 * n + 1, attend(2 * n, carry)), 0)
    for c in range(n_blocks):
        rows = slice(c * Q_BLOCK, (c + 1) * Q_BLOCK)
        o_ref[0, rows, :] = _gate_out(x_ref[0, rows, :], ot_scr[c].T, z_scr[rows, :], gate, wout_ref[...])
        if block_done is not None:
            block_done(c)


def _rope_tables_t(seq):
    pos = np.arange(seq)
    n_freq = HEAD_DIM // 4
    inv = ROPE_THETA ** (-np.arange(n_freq, dtype=np.float64) / n_freq)
    ang = np.concatenate([(pos // GRID_W)[:, None] * inv, (pos % GRID_W)[:, None] * inv], axis=-1)
    return np.cos(ang).T.astype(np.float32), np.sin(ang).T.astype(np.float32)


def _attn_layer_kernel(sink_ref, xc_ref, xl_ref, mod_ref, nw_ref, wqkvt_ref, wz_ref, wout_ref, qw_ref, kw_ref,
                       cos_ref, sin_ref, ck_ref, cv_ref, oc_ref, kto_ref, vto_ref, ol_hbm,
                       qkvt_scr, q_scr, z_scr, ot_scr, k_scr, vt_scr, ol_scr, out_sem, *, n_ctx_steps):
    step = pl.program_id(0)
    n_lat = ol_scr.shape[0]
    lat = step - n_ctx_steps

    def result_copy(r, c):
        rows = pl.ds(c * Q_BLOCK, Q_BLOCK)
        return pltpu.make_async_copy(ol_scr.at[r, rows], ol_hbm.at[r, rows], out_sem.at[r, c])

    @pl.when(step < n_ctx_steps)
    def _():
        _attn_ctx_body(sink_ref, xc_ref, mod_ref[0:1, :], nw_ref, wqkvt_ref, wz_ref, wout_ref, qw_ref, kw_ref,
                       oc_ref, kto_ref, vto_ref, qkvt_scr, z_scr, ot_scr)

    @pl.when(step >= n_ctx_steps)
    def _():
        mod = mod_ref[pl.ds(1 + lat, 1), :]
        _attn_lat_body(sink_ref, xl_ref, mod, nw_ref, wqkvt_ref, wz_ref, wout_ref, qw_ref, kw_ref,
                       cos_ref, sin_ref, ck_ref, cv_ref, ol_scr.at[pl.ds(lat, 1)],
                       q_scr, z_scr, ot_scr, k_scr, vt_scr,
                       block_done=lambda c: result_copy(lat, c).start())

        @pl.when(lat == n_lat - 1)
        def _():
            for r in range(n_lat):
                for c in range(out_sem.shape[1]):
                    result_copy(r, c).wait()


def _attn_layer(x_ctx, x_lat, mod, nw, wqkvt, wz, wout, qw, kw, sink, ckt, cvt):
    nb_ctx, seq_ctx, _ = x_ctx.shape
    nb_lat, seq_lat, _ = x_lat.shape
    past = ckt.shape[2]
    n_seq = CTX_SEQS_PER_STEP
    assert nb_ctx % n_seq == 0 and seq_ctx == Q_BLOCK and n_seq <= seq_lat // Q_BLOCK
    n_ctx = nb_ctx // n_seq
    cos, sin = (jnp.asarray(t) for t in _rope_tables_t(seq_lat))

    def ctx_step(i):
        return jnp.minimum(i, n_ctx - 1)

    def lat_step(i):
        return jnp.maximum(i - n_ctx, 0)

    return pl.pallas_call(
        functools.partial(_attn_layer_kernel, n_ctx_steps=n_ctx),
        grid=(n_ctx + nb_lat,),
        in_specs=[
            pl.BlockSpec(memory_space=pltpu.SMEM),
            pl.BlockSpec((n_seq, seq_ctx, D_MODEL), lambda i: (ctx_step(i), 0, 0)),
            pl.BlockSpec((1, seq_lat, D_MODEL), lambda i: (lat_step(i), 0, 0)),
            MOD_SPEC,
            _const_spec((1, D_MODEL)),
            _const_spec((BRANCH + 2 * KV_W, D_MODEL)),
            _const_spec((D_MODEL, BRANCH)),
            _const_spec((BRANCH, D_MODEL)),
            _const_spec((1, HEAD_DIM)),
            _const_spec((1, HEAD_DIM)),
            _const_spec((HEAD_DIM // 2, seq_lat)),
            _const_spec((HEAD_DIM // 2, seq_lat)),
            pl.BlockSpec((1, KV_W, past), lambda i: (lat_step(i), 0, 0)),
            pl.BlockSpec((1, KV_W, past), lambda i: (lat_step(i), 0, 0)),
        ],
        out_specs=[
            pl.BlockSpec((n_seq, seq_ctx, D_MODEL), lambda i: (ctx_step(i), 0, 0)),
            pl.BlockSpec((n_seq, KV_W, seq_ctx), lambda i: (ctx_step(i), 0, 0)),
            pl.BlockSpec((n_seq, KV_W, seq_ctx), lambda i: (ctx_step(i), 0, 0)),
            pl.BlockSpec(memory_space=pl.ANY),
        ],
        out_shape=[
            jax.ShapeDtypeStruct(x_ctx.shape, F32),
            jax.ShapeDtypeStruct((nb_ctx, KV_W, seq_ctx), F32),
            jax.ShapeDtypeStruct((nb_ctx, KV_W, seq_ctx), F32),
            jax.ShapeDtypeStruct(x_lat.shape, F32),
        ],
        scratch_shapes=[
            pltpu.VMEM((n_seq, BRANCH + 2 * KV_W, seq_ctx), F32),
            pltpu.VMEM((seq_lat // BLOCK, N_HEADS // 2, HEAD_DIM, 2 * BLOCK), BF16),
            pltpu.VMEM((seq_lat, BRANCH), F32),
            pltpu.VMEM((seq_lat // Q_BLOCK, BRANCH, Q_BLOCK), F32),
            pltpu.VMEM((seq_lat + 2 * BLOCK, KV_W), BF16),
            pltpu.VMEM((seq_lat // BLOCK + 2, KV_W, BLOCK), BF16),
            pltpu.VMEM(x_lat.shape, F32),
            pltpu.SemaphoreType.DMA((nb_lat, seq_lat // Q_BLOCK)),
        ],
        compiler_params=pltpu.CompilerParams(
            dimension_semantics=("arbitrary",), vmem_limit_bytes=VMEM_LIMIT),
        name="attn_layer",
    )(sink, x_ctx, x_lat, mod, nw, wqkvt, wz, wout, qw, kw, cos, sin, ckt, cvt)


def kernel(x_prompt, x_sample, cache_k_l1, cache_v_l1, c, c_ctx, norm_w_l0, w_mod_l0, b_mod_l0,
           w_in_l0, w_out_l0, norm_w_l1, w_mod_l1, b_mod_l1, w_in_l1, q_norm_w_l1, k_norm_w_l1,
           sink_l1, w_out_l1):
    nb_ctx, seq_ctx, _ = x_prompt.shape
    nb_lat = x_sample.shape[0]
    past = cache_k_l1.shape[1]
    assert 1 + nb_lat <= MOD_ROWS
    nw0 = norm_w_l0.reshape(1, D_MODEL)
    nw1 = norm_w_l1.reshape(1, D_MODEL)
    qw = q_norm_w_l1.reshape(1, HEAD_DIM)
    kw = k_norm_w_l1.reshape(1, HEAD_DIM)

    xp, xs, mod1, wqkvt1, wz1, wout1 = _fourier_layer(
        x_prompt, x_sample, nw0,
        this_layer=(c_ctx, c, w_mod_l0, b_mod_l0, w_in_l0, w_out_l0),
        next_layer=(c_ctx, c, w_mod_l1, b_mod_l1, w_in_l1, w_out_l1))

    def to_feature_major(t):
        return jnp.transpose(t, (0, 2, 3, 1)).reshape(t.shape[0], KV_W, t.shape[1])

    def from_feature_major(t):
        return jnp.transpose(t.reshape(t.shape[0], N_KV, HEAD_DIM, t.shape[2]), (0, 3, 1, 2))

    xp, new_kt, new_vt, xs = _attn_layer(xp, xs, mod1, nw1, wqkvt1, wz1, wout1, qw, kw, sink_l1,
                                         to_feature_major(cache_k_l1), to_feature_major(cache_v_l1))
    return (xp, xs, from_feature_major(new_kt), from_feature_major(new_vt))
```

```python
import functools

import numpy as np
import jax
import jax.numpy as jnp
from jax import lax
from jax.experimental import pallas as pl
from jax.experimental.pallas import tpu as pltpu

D_MODEL = 1024
BRANCH = 1024
N_GROUPS = 4
GROUP_W = BRANCH // N_GROUPS
HALF_W = GROUP_W // 2
HEAD_DIM = 64
N_HEADS = 16
N_KV = 4
GQA = N_HEADS // N_KV
KV_W = N_KV * HEAD_DIM
GRID_W = 64
WINDOW = 128
BLOCK = 128
ROPE_THETA = 10000.0
EPS = 1e-6
NEG_INF = -1e30
LANES = 128
ROW_CHUNK = 256
Q_BLOCK = 256
HALVES = Q_BLOCK // BLOCK
CTX_SEQS_PER_STEP = 2
VMEM_LIMIT = 56 * 1024 * 1024
FOURIER_VMEM_LIMIT = 60 * 1024 * 1024
MOD_ROWS = 8
ONES_ROWS = 16
LOG2E = float(np.log2(np.e))

F32 = jnp.float32
BF16 = jnp.bfloat16


def _dot(a, b):
    return jnp.dot(a, b, preferred_element_type=F32)


def _dot_nt(a, b):
    return lax.dot_general(a, b, (((1,), (1,)), ((), ())), preferred_element_type=F32)


MOD_SPEC = pl.BlockSpec((MOD_ROWS, 3 * D_MODEL), lambda b: (0, 0))


def _mod_norm(x, nw, mod):
    shift = mod[:, :D_MODEL]
    scale = mod[:, D_MODEL:2 * D_MODEL]
    y = x * lax.rsqrt(jnp.mean(x * x, axis=-1, keepdims=True) + EPS)
    return (y * nw) * (1.0 + scale) + shift


def _mod_accumulate(cctx_ref, c_ref, w_ref, b_ref, o_ref, cond_scr, first_step):
    n_lat = c_ref.shape[0]
    cond_scr[...] = jnp.zeros_like(cond_scr)
    cond_scr[0:1, :] = cctx_ref[...]
    cond_scr[1:1 + n_lat, :] = c_ref[...]
    s = jax.nn.silu(cond_scr[...]).astype(BF16)

    @pl.when(pl.program_id(0) == first_step)
    def _():
        o_ref[...] = jnp.broadcast_to(b_ref[...], o_ref.shape)

    o_ref[...] += _dot(s, w_ref[...].astype(BF16))


def _mirror_perm():
    j = np.arange(GROUP_W)
    return np.where(j <= HALF_W, j, GROUP_W + HALF_W - j)


def _perm_matrices():
    perm = _mirror_perm()
    pm = (np.arange(GROUP_W)[:, None] == perm[None, :]).astype(np.float32)
    assert (pm[:HALF_W, HALF_W:] == 0).all() and (pm[HALF_W:, :HALF_W] == 0).all()
    return pm, np.stack([pm[:HALF_W, :HALF_W], pm[HALF_W:, HALF_W:]])


def _fourier_layer_kernel(xc_ref, xl_hbm, nw_ref, m1_ref, csc_ref, ssc_ref, csl_hbm, ssl_hbm,
                          cctx0_ref, c0_ref, wmod0_ref, bmod0_ref, win0_ref, wout0_ref, pm_ref, pmh_ref,
                          cctx_ref, c_ref, wmod1_ref, bmod1_ref, win1_ref, wout1_ref,
                          oc_ref, ol_hbm, mod1_ref, wqkvt1_ref, wz1_ref, wout1b_ref,
                          mod0_scr, win_scr, wout_scr, ta_scr, tb_scr, tr_scr, z_scr,
                          xl_scr, csl_scr, ssl_scr, copy_sem, ol_scr, out_sem, cond_scr,
                          *, n_prep_steps, n_ctx_steps):
    step = pl.program_id(0)
    n_lat = xl_scr.shape[0]
    lat = step - (n_prep_steps + n_ctx_steps)

    def request_copy(r):
        return pltpu.make_async_copy(xl_hbm.at[r], xl_scr.at[r], copy_sem.at[r])

    table_copies = [pltpu.make_async_copy(csl_hbm, csl_scr, copy_sem.at[n_lat]),
                    pltpu.make_async_copy(ssl_hbm, ssl_scr, copy_sem.at[n_lat + 1])]

    def result_copy(r, c):
        rows = pl.ds(c * ROW_CHUNK, ROW_CHUNK)
        return pltpu.make_async_copy(ol_scr.at[r, rows], ol_hbm.at[r, rows], out_sem.at[r, c])

    @pl.when(step == n_prep_steps)
    def _():
        for copy in table_copies + [request_copy(r) for r in range(n_lat)]:
            copy.start(priority=1)

    @pl.when(step < n_prep_steps)
    def _():
        _mod_accumulate(cctx0_ref, c0_ref, wmod0_ref, bmod0_ref, mod0_scr, cond_scr, 0)
        rows = pl.ds(pl.multiple_of(step * HALF_W, HALF_W), HALF_W)
        w = win0_ref[...].astype(BF16)
        win_scr[rows, :BRANCH] = w[:, :BRANCH]
        pm = pm_ref[...]
        for g in range(N_GROUPS):
            cols = slice(BRANCH + g * GROUP_W, BRANCH + (g + 1) * GROUP_W)
            win_scr[rows, cols] = _dot(w[:, cols], pm).astype(BF16)
        wout_scr[rows, :] = _dot(pmh_ref[step % 2], wout0_ref[...].astype(BF16)).astype(BF16)

    @pl.when(jnp.logical_and(step >= n_prep_steps, step < n_prep_steps + n_ctx_steps))
    def _():
        _mod_accumulate(cctx_ref, c_ref, wmod1_ref, bmod1_ref, mod1_ref, cond_scr, n_prep_steps)
        w = win1_ref[...]
        wqkvt1_ref[...] = w[:, :BRANCH + 2 * KV_W].T.astype(BF16)
        wz1_ref[...] = w[:, BRANCH + 2 * KV_W:].astype(BF16)
        wout1b_ref[...] = wout1_ref[...].astype(BF16)
        _fourier_body(xc_ref, oc_ref, mod0_scr[0:1, :], nw_ref, win_scr, wout_scr, m1_ref, csc_ref, ssc_ref,
                      ta_scr, tb_scr, tr_scr, z_scr)

    @pl.when(step >= n_prep_steps + n_ctx_steps)
    def _():
        @pl.when(lat == 0)
        def _():
            for copy in table_copies:
                copy.wait()

        request_copy(lat).wait()
        mod = mod0_scr[pl.ds(1 + lat, 1), :]
        _fourier_body(xl_scr.at[pl.ds(lat, 1)], ol_scr.at[pl.ds(lat, 1)], mod, nw_ref, win_scr, wout_scr,
                      m1_ref, csl_scr, ssl_scr, ta_scr, tb_scr, tr_scr, z_scr,
                      rows_done=lambda _, rows: result_copy(lat, rows.start // ROW_CHUNK).start())

        @pl.when(lat == n_lat - 1)
        def _():
            for r in range(n_lat):
                for c in range(out_sem.shape[1]):
                    result_copy(r, c).wait()


def _fourier_body(x_ref, o_ref, mod, nw_ref, win_ref, wout_ref, m1_ref, cs_ref, ss_ref,
                  ta_scr, tb_scr, tr_scr, z_scr, rows_done=None):
    gate = mod[:, 2 * D_MODEL:]
    nw = nw_ref[...]
    n_seq, seq, _ = x_ref.shape
    n_chunks = seq // ROW_CHUNK
    lane = lax.broadcasted_iota(jnp.int32, (ROW_CHUNK, HALF_W), 1)
    for i in range(n_seq):
        for c in range(n_chunks):
            rows = slice(c * ROW_CHUNK, (c + 1) * ROW_CHUNK)
            srows = slice(i * seq + c * ROW_CHUNK, i * seq + (c + 1) * ROW_CHUNK)
            h = _mod_norm(x_ref[i, rows, :], nw, mod).astype(BF16)
            uz = _dot(h, win_ref[...])
            z_scr[srows, :] = uz[:, BRANCH:]
            u = uz[:, :BRANCH].astype(BF16)
            tr = jnp.zeros((ROW_CHUNK, HALF_W), F32)
            for g in range(N_GROUPS):
                t = _dot(u[:, g * GROUP_W:(g + 1) * GROUP_W], m1_ref[...])
                half = slice(g * HALF_W, (g + 1) * HALF_W)
                ta_scr[srows, half] = t[:, :HALF_W].astype(BF16)
                tb = t[:, HALF_W:]
                tb_scr[srows, half] = tb.astype(BF16)
                tr = jnp.where(lane == g, tb if g == 0 else pltpu.roll(tb, g, axis=1), tr)
            tr_scr[srows, :] = tr.astype(BF16)
    for i in range(n_seq):
        seq_rows = slice(i * seq, (i + 1) * seq)
        for c in range(n_chunks):
            rows = slice(c * ROW_CHUNK, (c + 1) * ROW_CHUNK)
            srows = slice(i * seq + c * ROW_CHUNK, i * seq + (c + 1) * ROW_CHUNK)
            cs = cs_ref[rows, :]
            p = _dot(cs, ta_scr[seq_rows, :])
            q = _dot(ss_ref[rows, :], tb_scr[seq_rows, :])
            r = _dot(cs, tr_scr[seq_rows, :])
            parts = []
            for g in range(N_GROUPS):
                half = slice(g * HALF_W, (g + 1) * HALF_W)
                pg, qg = p[:, half], q[:, half]
                rg = r if g == 0 else pltpu.roll(r, HALF_W - g, axis=1)
                parts.append(jnp.where(lane == 0, pg, pg - qg))
                parts.append(jnp.where(lane == 0, rg, pg + qg))
            y = jnp.concatenate(parts, axis=1)
            y = (y * jax.nn.silu(z_scr[srows, :])).astype(BF16)
            o_ref[i, rows, :] = x_ref[i, rows, :] + gate * _dot(y, wout_ref[...])
            if rows_done is not None:
                rows_done(i, rows)


def _dft_tables(seq):
    c = np.arange(GROUP_W)[:, None]
    k = np.arange(HALF_W)[None, :]
    cos_lo = np.cos(2.0 * np.pi * ((c * k) % GROUP_W) / GROUP_W)
    sin_lo = np.sin(2.0 * np.pi * ((c * k) % GROUP_W) / GROUP_W)
    sin_lo[:, 0] = np.cos(np.pi * c[:, 0])
    m1 = np.concatenate([cos_lo, sin_lo], axis=1) / np.sqrt(GROUP_W)
    n = np.arange(seq)
    ang = 2.0 * np.pi * ((n[:, None] * n[None, :]) % seq) / seq
    cs = np.cos(ang) / np.sqrt(seq)
    ss = np.sin(ang) / np.sqrt(seq)
    return m1.astype(np.float32), cs.astype(np.float32), ss.astype(np.float32)


def _const_spec(shape):
    return pl.BlockSpec(shape, lambda b: (0,) * len(shape))


def _fourier_layer(x_ctx, x_lat, nw, this_layer, next_layer):
    nb_ctx, seq_ctx, _ = x_ctx.shape
    nb_lat, seq_lat, _ = x_lat.shape
    n_seq = CTX_SEQS_PER_STEP
    assert nb_ctx % n_seq == 0
    n_prep = D_MODEL // HALF_W
    n_ctx = nb_ctx // n_seq
    assert D_MODEL % (n_ctx * LANES) == 0 and n_seq * seq_ctx <= seq_lat
    rows = D_MODEL // n_ctx
    assert rows == HALF_W
    c_ctx, c, w_mod0, b_mod0, w_in0, w_out0 = this_layer
    _, _, w_mod1, b_mod1, w_in1, w_out1 = next_layer
    n_lat = c.shape[0]
    n_qkvz = 2 * BRANCH + 2 * KV_W
    m1, csc, ssc = (jnp.asarray(t).astype(BF16) for t in _dft_tables(seq_ctx))
    _, csl, ssl = (jnp.asarray(t).astype(BF16) for t in _dft_tables(seq_lat))
    pm, pmh = (jnp.asarray(t).astype(BF16) for t in _perm_matrices())
    cc = c_ctx.reshape(1, D_MODEL)

    def prep_step(i):
        return jnp.minimum(i, n_prep - 1)

    def ctx_step(i):
        return jnp.clip(i - n_prep, 0, n_ctx - 1)

    def chunk_specs(step_fn, w_in_cols):
        return [pl.BlockSpec((1, HALF_W), lambda i: (0, step_fn(i))),
                pl.BlockSpec((n_lat, HALF_W), lambda i: (0, step_fn(i))),
                pl.BlockSpec((HALF_W, 3 * D_MODEL), lambda i: (step_fn(i), 0)),
                _const_spec((1, 3 * D_MODEL)),
                pl.BlockSpec((HALF_W, w_in_cols), lambda i: (step_fn(i), 0)),
                pl.BlockSpec((HALF_W, D_MODEL), lambda i: (step_fn(i), 0))]

    in_specs = [
        pl.BlockSpec((n_seq, seq_ctx, D_MODEL), lambda i: (ctx_step(i), 0, 0)),
        pl.BlockSpec(memory_space=pl.ANY),
        _const_spec((1, D_MODEL)),
        _const_spec((GROUP_W, GROUP_W)),
        _const_spec((seq_ctx, seq_ctx)),
        _const_spec((seq_ctx, seq_ctx)),
        pl.BlockSpec(memory_space=pl.ANY),
        pl.BlockSpec(memory_space=pl.ANY),
    ] + chunk_specs(prep_step, 2 * BRANCH) + [
        _const_spec((GROUP_W, GROUP_W)),
        _const_spec((2, HALF_W, HALF_W)),
    ] + chunk_specs(ctx_step, n_qkvz)
    out_specs = [
        pl.BlockSpec((n_seq, seq_ctx, D_MODEL), lambda i: (ctx_step(i), 0, 0)),
        pl.BlockSpec(memory_space=pl.ANY),
        MOD_SPEC,
        pl.BlockSpec((BRANCH + 2 * KV_W, rows), lambda i: (0, ctx_step(i))),
        pl.BlockSpec((rows, BRANCH), lambda i: (ctx_step(i), 0)),
        pl.BlockSpec((rows, D_MODEL), lambda i: (ctx_step(i), 0)),
    ]
    out_shape = [
        jax.ShapeDtypeStruct(x_ctx.shape, F32),
        jax.ShapeDtypeStruct(x_lat.shape, F32),
        jax.ShapeDtypeStruct((MOD_ROWS, 3 * D_MODEL), F32),
        jax.ShapeDtypeStruct((BRANCH + 2 * KV_W, D_MODEL), BF16),
        jax.ShapeDtypeStruct((D_MODEL, BRANCH), BF16),
        jax.ShapeDtypeStruct((BRANCH, D_MODEL), BF16),
    ]
    return pl.pallas_call(
        functools.partial(_fourier_layer_kernel, n_prep_steps=n_prep, n_ctx_steps=n_ctx),
        grid=(n_prep + n_ctx + nb_lat,),
        in_specs=in_specs,
        out_specs=out_specs,
        out_shape=out_shape,
        scratch_shapes=[
            pltpu.VMEM((MOD_ROWS, 3 * D_MODEL), F32),
            pltpu.VMEM((D_MODEL, 2 * BRANCH), BF16),
            pltpu.VMEM((BRANCH, D_MODEL), BF16),
            pltpu.VMEM((seq_lat, N_GROUPS * HALF_W), BF16),
            pltpu.VMEM((seq_lat, N_GROUPS * HALF_W), BF16),
            pltpu.VMEM((seq_lat, HALF_W), BF16),
            pltpu.VMEM((seq_lat, BRANCH), F32),
            pltpu.VMEM(x_lat.shape, F32),
            pltpu.VMEM((seq_lat, seq_lat), BF16),
            pltpu.VMEM((seq_lat, seq_lat), BF16),
            pltpu.SemaphoreType.DMA((nb_lat + 2,)),
            pltpu.VMEM(x_lat.shape, F32),
            pltpu.SemaphoreType.DMA((nb_lat, seq_lat // ROW_CHUNK)),
            pltpu.VMEM((MOD_ROWS, HALF_W), F32),
        ],
        compiler_params=pltpu.CompilerParams(
            dimension_semantics=("arbitrary",), vmem_limit_bytes=FOURIER_VMEM_LIMIT),
        name="fourier_layer",
    )(x_ctx, x_lat, nw, m1, csc, ssc, csl, ssl,
      cc, c, w_mod0, b_mod0.reshape(1, 3 * D_MODEL), w_in0, w_out0, pm, pmh,
      cc, c, w_mod1, b_mod1.reshape(1, 3 * D_MODEL), w_in1, w_out1)


def _head_weight_tile(w_ref, n_tokens):
    row = jnp.broadcast_to(w_ref[...], (HEAD_DIM, HEAD_DIM))
    ii = lax.broadcasted_iota(jnp.int32, (HEAD_DIM, HEAD_DIM), 0)
    jj = lax.broadcasted_iota(jnp.int32, (HEAD_DIM, HEAD_DIM), 1)
    col = jnp.sum(jnp.where(ii == jj, row, 0.0), axis=1, keepdims=True)
    return jnp.broadcast_to(col, (HEAD_DIM, n_tokens))


def _head_rms(t, w):
    return (t * lax.rsqrt(jnp.mean(t * t, axis=0, keepdims=True) + EPS)) * w


def _rope_t(t, cos, sin):
    half = HEAD_DIM // 2
    x1, x2 = t[:half], t[half:]
    return jnp.concatenate([x1 * cos - x2 * sin, x1 * sin + x2 * cos], axis=0)


def _head_scores(qn, g, keys, biases):
    zeros = jnp.zeros_like(qn)
    qz = jnp.concatenate([qn, zeros] if g % 2 == 0 else [zeros, qn], axis=0)
    blk = slice((g // 2) * LANES, (g // 2 + 1) * LANES)
    scores = []
    smax = None
    for k, bias in zip(keys, biases):
        s = _dot(k[:, blk], qz)
        if bias is not None:
            s = jnp.concatenate(
                [s[c * BLOCK:(c + 1) * BLOCK] if b is None else
                 s[c * BLOCK:(c + 1) * BLOCK] + jnp.concatenate([b] * (s.shape[1] // b.shape[1]), axis=1)
                 for c, b in enumerate(bias)], axis=0)
        cmax = jnp.max(s, axis=0, keepdims=True)
        smax = cmax if smax is None else jnp.maximum(smax, cmax)
        scores.append(s)
    return scores, smax


def _with_ones_rows(vt):
    return jnp.concatenate([vt, jnp.ones((ONES_ROWS, vt.shape[1]), vt.dtype)], axis=0)


def _head_probs(scored, sink2):
    scores, smax = scored
    m = jnp.maximum(smax, sink2)
    return jnp.concatenate([jnp.exp2(s - m).astype(BF16) for s in scores], axis=0), m


def _head_pv(probs, values_t, sink2):
    p, m = probs
    acc = _dot(values_t, p)
    den = acc[HEAD_DIM:HEAD_DIM + 1] + jnp.exp2(sink2 - m)
    return acc[:HEAD_DIM] * (1.0 / den)


def _attend_heads(n_units, stage, scores_fn, probs_fn, pv_fn, fillers=()):
    n_stages = n_units // stage
    pending = [scores_fn(u) for u in range(stage)]
    for g in range(n_stages):
        units = range(g * stage, (g + 1) * stage)
        if g < len(fillers):
            fillers[g]()
        nxt = [scores_fn(u) for u in range((g + 1) * stage, (g + 2) * stage)] if g + 1 < n_stages else None
        probs = [probs_fn(u, sc) for u, sc in zip(units, pending)]
        for u, pr in zip(units, probs):
            pv_fn(u, pr)
        pending = nxt


def _gate_out(x, o, z, gate, wout):
    y = (o * jax.nn.silu(z)).astype(BF16)
    return x + gate * _dot(y, wout)


def _attn_ctx_body(sink_ref, x_ref, mod, nw_ref, wqkvt_ref, wz_ref, wout_ref, qw_ref, kw_ref,
                   o_ref, kto_ref, vto_ref, qkvt_scr, z_scr, ot_scr):
    seq = x_ref.shape[1]
    n_seq = x_ref.shape[0]
    gate = mod[:, 2 * D_MODEL:]
    kw = _head_weight_tile(kw_ref, seq)
    qw = _head_weight_tile(qw_ref, seq)

    def project(i):
        h = _mod_norm(x_ref[i], nw_ref[...], mod).astype(BF16)
        qkvt_scr[i] = _dot_nt(wqkvt_ref[...], h)
        z_scr[i * seq:(i + 1) * seq, :] = _dot(h, wz_ref[...])

    def keys_values(i):
        knt = jnp.concatenate(
            [_head_rms(qkvt_scr[i, BRANCH + g * HEAD_DIM:BRANCH + (g + 1) * HEAD_DIM, :], kw)
             for g in range(N_KV)], axis=0)
        kto_ref[i] = knt
        vtf = qkvt_scr[i, BRANCH + KV_W:, :]
        vto_ref[i] = vtf
        vt = vtf.astype(BF16)
        return (knt.T.astype(BF16),
                [_with_ones_rows(vt[g * HEAD_DIM:(g + 1) * HEAD_DIM]) for g in range(N_KV)])

    def output(i):
        o_ref[i] = _gate_out(x_ref[i], ot_scr[i].T, z_scr[i * seq:(i + 1) * seq, :], gate, wout_ref[...])

    for i in range(n_seq):
        project(i)
    kv = [keys_values(i) for i in range(n_seq)]

    def scores_fn(u):
        i, hd = divmod(u, N_HEADS)
        t = qkvt_scr[i, hd * HEAD_DIM:(hd + 1) * HEAD_DIM, :]
        qn = (_head_rms(t, qw) * (HEAD_DIM ** -0.5 * LOG2E)).astype(BF16)
        return _head_scores(qn, hd // GQA, [kv[i][0]], [None])

    def probs_fn(u, sc):
        return _head_probs(sc, sink_ref[u % N_HEADS] * LOG2E)

    def pv_fn(u, pr):
        i, hd = divmod(u, N_HEADS)
        ot_scr[i, hd * HEAD_DIM:(hd + 1) * HEAD_DIM, :] = _head_pv(
            pr, kv[i][1][hd // GQA], sink_ref[hd] * LOG2E)

    fillers = [lambda: None] + [functools.partial(output, i) for i in range(n_seq - 1)]
    _attend_heads(n_seq * N_HEADS, N_HEADS, scores_fn, probs_fn, pv_fn, fillers)
    output(n_seq - 1)


def _attn_lat_body(sink_ref, x_ref, mod, nw_ref, wqkvt_ref, wz_ref, wout_ref, qw_ref, kw_ref,
                   cos_ref, sin_ref, ck_ref, cv_ref, o_ref,
                   q_scr, z_scr, ot_scr, k_scr, vt_scr, block_done=None):
    seq = x_ref.shape[1]
    gate = mod[:, 2 * D_MODEL:]
    nw = nw_ref[...]
    qw = _head_weight_tile(qw_ref, Q_BLOCK)
    kw = _head_weight_tile(kw_ref, Q_BLOCK)
    n_blocks = seq // Q_BLOCK
    kv_blocks = seq // BLOCK
    k_scr[0:BLOCK, :] = jnp.zeros((BLOCK, KV_W), BF16)
    k_scr[BLOCK + seq:2 * BLOCK + seq, :] = jnp.zeros((BLOCK, KV_W), BF16)
    vt_scr[0] = jnp.zeros((KV_W, BLOCK), BF16)
    vt_scr[kv_blocks + 1] = jnp.zeros((KV_W, BLOCK), BF16)

    def project(c):
        rows = slice(c * Q_BLOCK, (c + 1) * Q_BLOCK)
        h = _mod_norm(x_ref[0, rows, :], nw, mod).astype(BF16)
        z_scr[rows, :] = _dot(h, wz_ref[...])
        qkvt = _dot_nt(wqkvt_ref[...], h)
        cos = cos_ref[:, rows]
        sin = sin_ref[:, rows]
        for hd in range(N_HEADS):
            hr = slice(hd * HEAD_DIM, (hd + 1) * HEAD_DIM)
            t = _rope_t(_head_rms(qkvt[hr], qw), cos, sin)
            t = (t * (HEAD_DIM ** -0.5 * LOG2E)).astype(BF16)
            for j in range(HALVES):
                q_scr[c * HALVES + j, hd // 2, :, (hd % 2) * BLOCK:(hd % 2 + 1) * BLOCK] = (
                    t[:, j * BLOCK:(j + 1) * BLOCK])
        knt = jnp.concatenate(
            [_rope_t(_head_rms(qkvt[BRANCH + g * HEAD_DIM:BRANCH + (g + 1) * HEAD_DIM], kw), cos, sin)
             for g in range(N_KV)], axis=0)
        k_scr[BLOCK + c * Q_BLOCK:BLOCK + (c + 1) * Q_BLOCK, :] = knt.T.astype(BF16)
        vt = qkvt[BRANCH + KV_W:].astype(BF16)
        for j in range(HALVES):
            vt_scr[1 + c * HALVES + j] = vt[:, j * BLOCK:(j + 1) * BLOCK]

    ckb = ck_ref[0].T.astype(BF16)
    cvt = cv_ref[0].astype(BF16)

    win_len = 3 * BLOCK
    n_pairs = N_HEADS // 2
    kj = lax.broadcasted_iota(jnp.int32, (BLOCK, BLOCK), 0)
    qi = lax.broadcasted_iota(jnp.int32, (BLOCK, BLOCK), 1)
    first_head = lax.broadcasted_iota(jnp.int32, (1, 2 * BLOCK), 1) < BLOCK
    assert WINDOW >= BLOCK - 1

    def band_bias(c):
        rel = kj + (c - 1) * BLOCK - qi
        return jnp.where((rel >= -WINDOW) & (rel <= WINDOW), 0.0, NEG_INF)

    band_before, band_after = band_bias(0), band_bias(2)

    def half_operands(hb):
        r0 = pl.multiple_of(hb * BLOCK, BLOCK)
        valid = (jnp.where(hb > 0, band_before, NEG_INF), None,
                 jnp.where(hb < kv_blocks - 1, band_after, NEG_INF))
        kwin = k_scr[pl.ds(r0, win_len), :]
        vall = jnp.concatenate([vt_scr[hb + j] for j in range(win_len // BLOCK)] + [cvt], axis=1)
        valls = [_with_ones_rows(vall[g * HEAD_DIM:(g + 1) * HEAD_DIM]) for g in range(N_KV)]
        return valid, kwin, valls

    def pair_sink2(pair):
        return jnp.where(first_head, sink_ref[2 * pair], sink_ref[2 * pair + 1]) * LOG2E

    def attend(n, carry):
        halves = [half_operands(n * HALVES + j) for j in range(HALVES)]

        def scores_fn(u):
            j, pair = divmod(u, n_pairs)
            valid, kwin, _ = halves[j]
            return _head_scores(q_scr[n * HALVES + j, pair], pair // (GQA // 2), [kwin, ckb], [valid, None])

        def probs_fn(u, sc):
            return _head_probs(sc, pair_sink2(u % n_pairs))

        def pv_fn(u, pr):
            j, pair = divmod(u, n_pairs)
            o = _head_pv(pr, halves[j][2][pair // (GQA // 2)], pair_sink2(pair))
            for i in range(2):
                hd = 2 * pair + i
                ot_scr[n, hd * HEAD_DIM:(hd + 1) * HEAD_DIM, j * BLOCK:(j + 1) * BLOCK] = (
                    o[:, i * BLOCK:(i + 1) * BLOCK])

        _attend_heads(HALVES * n_pairs, GQA // 2, scores_fn, probs_fn, pv_fn)
        return carry

    for c in range(n_blocks):
        project(c)
    assert n_blocks % 2 == 0
    lax.fori_loop(0, n_blocks // 2, lambda n, carry: attend(2 * n + 1, attend(2 * n, carry)), 0)
    for c in range(n_blocks):
        rows = slice(c * Q_BLOCK, (c + 1) * Q_BLOCK)
        o_ref[0, rows, :] = _gate_out(x_ref[0, rows, :], ot_scr[c].T, z_scr[rows, :], gate, wout_ref[...])
        if block_done is not None:
            block_done(c)


def _rope_tables_t(seq):
    pos = np.arange(seq)
    n_freq = HEAD_DIM // 4
    inv = ROPE_THETA ** (-np.arange(n_freq, dtype=np.float64) / n_freq)
    ang = np.concatenate([(pos // GRID_W)[:, None] * inv, (pos % GRID_W)[:, None] * inv], axis=-1)
    return np.cos(ang).T.astype(np.float32), np.sin(ang).T.astype(np.float32)


def _attn_layer_kernel(sink_ref, xc_ref, xl_hbm, mod_ref, nw_ref, wqkvt_ref, wz_ref, wout_ref, qw_ref, kw_ref,
                       cos_ref, sin_ref, ck_ref, cv_ref, oc_ref, kto_ref, vto_ref, ol_hbm,
                       qkvt_scr, q_scr, z_scr, ot_scr, k_scr, vt_scr, ol_scr, out_sem, xl_scr, copy_sem,
                       *, n_ctx_steps):
    step = pl.program_id(0)
    n_lat = ol_scr.shape[0]
    lat = step - n_ctx_steps

    def request_copy(r):
        return pltpu.make_async_copy(xl_hbm.at[r], xl_scr.at[r], copy_sem.at[r])

    @pl.when(step == 0)
    def _():
        for r in range(n_lat):
            request_copy(r).start(priority=1)

    def result_copy(r, c):
        rows = pl.ds(c * Q_BLOCK, Q_BLOCK)
        return pltpu.make_async_copy(ol_scr.at[r, rows], ol_hbm.at[r, rows], out_sem.at[r, c])

    @pl.when(step < n_ctx_steps)
    def _():
        _attn_ctx_body(sink_ref, xc_ref, mod_ref[0:1, :], nw_ref, wqkvt_ref, wz_ref, wout_ref, qw_ref, kw_ref,
                       oc_ref, kto_ref, vto_ref, qkvt_scr, z_scr, ot_scr)

    @pl.when(step >= n_ctx_steps)
    def _():
        request_copy(lat).wait()
        mod = mod_ref[pl.ds(1 + lat, 1), :]
        _attn_lat_body(sink_ref, xl_scr.at[pl.ds(lat, 1)], mod, nw_ref, wqkvt_ref, wz_ref, wout_ref, qw_ref, kw_ref,
                       cos_ref, sin_ref, ck_ref, cv_ref, ol_scr.at[pl.ds(lat, 1)],
                       q_scr, z_scr, ot_scr, k_scr, vt_scr,
                       block_done=lambda c: result_copy(lat, c).start())

        @pl.when(lat == n_lat - 1)
        def _():
            for r in range(n_lat):
                for c in range(out_sem.shape[1]):
                    result_copy(r, c).wait()


def _attn_layer(x_ctx, x_lat, mod, nw, wqkvt, wz, wout, qw, kw, sink, ckt, cvt):
    nb_ctx, seq_ctx, _ = x_ctx.shape
    nb_lat, seq_lat, _ = x_lat.shape
    past = ckt.shape[2]
    n_seq = CTX_SEQS_PER_STEP
    assert nb_ctx % n_seq == 0 and seq_ctx == Q_BLOCK and n_seq <= seq_lat // Q_BLOCK
    n_ctx = nb_ctx // n_seq
    cos, sin = (jnp.asarray(t) for t in _rope_tables_t(seq_lat))

    def ctx_step(i):
        return jnp.minimum(i, n_ctx - 1)

    def lat_step(i):
        return jnp.maximum(i - n_ctx, 0)

    return pl.pallas_call(
        functools.partial(_attn_layer_kernel, n_ctx_steps=n_ctx),
        grid=(n_ctx + nb_lat,),
        in_specs=[
            pl.BlockSpec(memory_space=pltpu.SMEM),
            pl.BlockSpec((n_seq, seq_ctx, D_MODEL), lambda i: (ctx_step(i), 0, 0)),
            pl.BlockSpec(memory_space=pl.ANY),
            MOD_SPEC,
            _const_spec((1, D_MODEL)),
            _const_spec((BRANCH + 2 * KV_W, D_MODEL)),
            _const_spec((D_MODEL, BRANCH)),
            _const_spec((BRANCH, D_MODEL)),
            _const_spec((1, HEAD_DIM)),
            _const_spec((1, HEAD_DIM)),
            _const_spec((HEAD_DIM // 2, seq_lat)),
            _const_spec((HEAD_DIM // 2, seq_lat)),
            pl.BlockSpec((1, KV_W, past), lambda i: (lat_step(i), 0, 0)),
            pl.BlockSpec((1, KV_W, past), lambda i: (lat_step(i), 0, 0)),
        ],
        out_specs=[
            pl.BlockSpec((n_seq, seq_ctx, D_MODEL), lambda i: (ctx_step(i), 0, 0)),
            pl.BlockSpec((n_seq, KV_W, seq_ctx), lambda i: (ctx_step(i), 0, 0)),
            pl.BlockSpec((n_seq, KV_W, seq_ctx), lambda i: (ctx_step(i), 0, 0)),
            pl.BlockSpec(memory_space=pl.ANY),
        ],
        out_shape=[
            jax.ShapeDtypeStruct(x_ctx.shape, F32),
            jax.ShapeDtypeStruct((nb_ctx, KV_W, seq_ctx), F32),
            jax.ShapeDtypeStruct((nb_ctx, KV_W, seq_ctx), F32),
            jax.ShapeDtypeStruct(x_lat.shape, F32),
        ],
        scratch_shapes=[
            pltpu.VMEM((n_seq, BRANCH + 2 * KV_W, seq_ctx), F32),
            pltpu.VMEM((seq_lat // BLOCK, N_HEADS // 2, HEAD_DIM, 2 * BLOCK), BF16),
            pltpu.VMEM((seq_lat, BRANCH), F32),
            pltpu.VMEM((seq_lat // Q_BLOCK, BRANCH, Q_BLOCK), F32),
            pltpu.VMEM((seq_lat + 2 * BLOCK, KV_W), BF16),
            pltpu.VMEM((seq_lat // BLOCK + 2, KV_W, BLOCK), BF16),
            pltpu.VMEM(x_lat.shape, F32),
            pltpu.SemaphoreType.DMA((nb_lat, seq_lat // Q_BLOCK)),
            pltpu.VMEM(x_lat.shape, F32),
            pltpu.SemaphoreType.DMA((nb_lat,)),
        ],
        compiler_params=pltpu.CompilerParams(
            dimension_semantics=("arbitrary",), vmem_limit_bytes=VMEM_LIMIT),
        name="attn_layer",
    )(sink, x_ctx, x_lat, mod, nw, wqkvt, wz, wout, qw, kw, cos, sin, ckt, cvt)


def kernel(x_prompt, x_sample, cache_k_l1, cache_v_l1, c, c_ctx, norm_w_l0, w_mod_l0, b_mod_l0,
           w_in_l0, w_out_l0, norm_w_l1, w_mod_l1, b_mod_l1, w_in_l1, q_norm_w_l1, k_norm_w_l1,
           sink_l1, w_out_l1):
    nb_ctx, seq_ctx, _ = x_prompt.shape
    nb_lat = x_sample.shape[0]
    past = cache_k_l1.shape[1]
    assert 1 + nb_lat <= MOD_ROWS
    nw0 = norm_w_l0.reshape(1, D_MODEL)
    nw1 = norm_w_l1.reshape(1, D_MODEL)
    qw = q_norm_w_l1.reshape(1, HEAD_DIM)
    kw = k_norm_w_l1.reshape(1, HEAD_DIM)

    xp, xs, mod1, wqkvt1, wz1, wout1 = _fourier_layer(
        x_prompt, x_sample, nw0,
        this_layer=(c_ctx, c, w_mod_l0, b_mod_l0, w_in_l0, w_out_l0),
        next_layer=(c_ctx, c, w_mod_l1, b_mod_l1, w_in_l1, w_out_l1))

    def to_feature_major(t):
        return jnp.transpose(t, (0, 2, 3, 1)).reshape(t.shape[0], KV_W, t.shape[1])

    def from_feature_major(t):
        return jnp.transpose(t.reshape(t.shape[0], N_KV, HEAD_DIM, t.shape[2]), (0, 3, 1, 2))

    xp, new_kt, new_vt, xs = _attn_layer(xp, xs, mod1, nw1, wqkvt1, wz1, wout1, qw, kw, sink_l1,
                                         to_feature_major(cache_k_l1), to_feature_major(cache_v_l1))
    return (xp, xs, from_feature_major(new_kt), from_feature_major(new_vt))
```

```python
import functools

import numpy as np
import jax
import jax.numpy as jnp
from jax import lax
from jax.experimental import pallas as pl
from jax.experimental.pallas import tpu as pltpu

D_MODEL = 1024
BRANCH = 1024
N_GROUPS = 4
GROUP_W = BRANCH // N_GROUPS
HALF_W = GROUP_W // 2
HEAD_DIM = 64
N_HEADS = 16
N_KV = 4
GQA = N_HEADS // N_KV
KV_W = N_KV * HEAD_DIM
GRID_W = 64
WINDOW = 128
BLOCK = 128
ROPE_THETA = 10000.0
EPS = 1e-6
NEG_INF = -1e30
LANES = 128
ROW_CHUNK = 256
Q_BLOCK = 256
HALVES = Q_BLOCK // BLOCK
CTX_SEQS_PER_STEP = 2
VMEM_LIMIT = 56 * 1024 * 1024
FOURIER_VMEM_LIMIT = 60 * 1024 * 1024
MOD_ROWS = 8
ONES_ROWS = 16
LOG2E = float(np.log2(np.e))

F32 = jnp.float32
BF16 = jnp.bfloat16


def _dot(a, b):
    return jnp.dot(a, b, preferred_element_type=F32)


def _dot_nt(a, b):
    return lax.dot_general(a, b, (((1,), (1,)), ((), ())), preferred_element_type=F32)


MOD_SPEC = pl.BlockSpec((MOD_ROWS, 3 * D_MODEL), lambda b: (0, 0))


def _mod_norm(x, nw, mod):
    shift = mod[:, :D_MODEL]
    scale = mod[:, D_MODEL:2 * D_MODEL]
    y = x * lax.rsqrt(jnp.mean(x * x, axis=-1, keepdims=True) + EPS)
    return (y * nw) * (1.0 + scale) + shift


def _mod_accumulate(cctx_ref, c_ref, w_ref, b_ref, o_ref, cond_scr, first_step):
    n_lat = c_ref.shape[0]
    cond_scr[...] = jnp.zeros_like(cond_scr)
    cond_scr[0:1, :] = cctx_ref[...]
    cond_scr[1:1 + n_lat, :] = c_ref[...]
    s = jax.nn.silu(cond_scr[...]).astype(BF16)

    @pl.when(pl.program_id(0) == first_step)
    def _():
        o_ref[...] = jnp.broadcast_to(b_ref[...], o_ref.shape)

    o_ref[...] += _dot(s, w_ref[...].astype(BF16))


def _mirror_perm():
    j = np.arange(GROUP_W)
    return np.where(j <= HALF_W, j, GROUP_W + HALF_W - j)


def _perm_matrices():
    perm = _mirror_perm()
    pm = (np.arange(GROUP_W)[:, None] == perm[None, :]).astype(np.float32)
    assert (pm[:HALF_W, HALF_W:] == 0).all() and (pm[HALF_W:, :HALF_W] == 0).all()
    return pm, np.stack([pm[:HALF_W, :HALF_W], pm[HALF_W:, HALF_W:]])


def _fourier_layer_kernel(xc_ref, xl_hbm, nw_ref, m1_ref, csc_ref, ssc_ref, csl_hbm, ssl_hbm,
                          cctx0_ref, c0_ref, wmod0_ref, bmod0_ref, win0_ref, wout0_ref, pm_ref, pmh_ref,
                          cctx_ref, c_ref, wmod1_ref, bmod1_ref, win1_ref, wout1_ref,
                          oc_ref, ol_hbm, mod1_ref, wqkvt1_ref, wz1_ref, wout1b_ref,
                          mod0_scr, win_scr, wout_scr, ta_scr, tb_scr, tr_scr, z_scr,
                          xl_scr, csl_scr, ssl_scr, copy_sem, ol_scr, out_sem, cond_scr,
                          *, n_prep_steps, n_ctx_steps):
    step = pl.program_id(0)
    n_lat = xl_scr.shape[0]
    lat = step - (n_prep_steps + n_ctx_steps)

    def request_copy(r):
        return pltpu.make_async_copy(xl_hbm.at[r], xl_scr.at[r], copy_sem.at[r])

    table_copies = [pltpu.make_async_copy(csl_hbm, csl_scr, copy_sem.at[n_lat]),
                    pltpu.make_async_copy(ssl_hbm, ssl_scr, copy_sem.at[n_lat + 1])]

    def result_copy(r, c):
        rows = pl.ds(c * ROW_CHUNK, ROW_CHUNK)
        return pltpu.make_async_copy(ol_scr.at[r, rows], ol_hbm.at[r, rows], out_sem.at[r, c])

    @pl.when(step == n_prep_steps)
    def _():
        for copy in table_copies + [request_copy(r) for r in range(n_lat)]:
            copy.start(priority=1)

    @pl.when(step < n_prep_steps)
    def _():
        _mod_accumulate(cctx0_ref, c0_ref, wmod0_ref, bmod0_ref, mod0_scr, cond_scr, 0)
        rows = pl.ds(pl.multiple_of(step * HALF_W, HALF_W), HALF_W)
        w = win0_ref[...].astype(BF16)
        win_scr[rows, :BRANCH] = w[:, :BRANCH]
        pm = pm_ref[...]
        for g in range(N_GROUPS):
            cols = slice(BRANCH + g * GROUP_W, BRANCH + (g + 1) * GROUP_W)
            win_scr[rows, cols] = _dot(w[:, cols], pm).astype(BF16)
        wout_scr[rows, :] = _dot(pmh_ref[step % 2], wout0_ref[...].astype(BF16)).astype(BF16)

    @pl.when(jnp.logical_and(step >= n_prep_steps, step < n_prep_steps + n_ctx_steps))
    def _():
        _mod_accumulate(cctx_ref, c_ref, wmod1_ref, bmod1_ref, mod1_ref, cond_scr, n_prep_steps)
        w = win1_ref[...]
        wqkvt1_ref[...] = w[:, :BRANCH + 2 * KV_W].T.astype(BF16)
        wz1_ref[...] = w[:, BRANCH + 2 * KV_W:].astype(BF16)
        wout1b_ref[...] = wout1_ref[...].astype(BF16)
        _fourier_body(xc_ref, oc_ref, mod0_scr[0:1, :], nw_ref, win_scr, wout_scr, m1_ref, csc_ref, ssc_ref,
                      ta_scr, tb_scr, tr_scr, z_scr)

    @pl.when(step >= n_prep_steps + n_ctx_steps)
    def _():
        @pl.when(lat == 0)
        def _():
            for copy in table_copies:
                copy.wait()

        request_copy(lat).wait()
        mod = mod0_scr[pl.ds(1 + lat, 1), :]
        _fourier_body(xl_scr.at[pl.ds(lat, 1)], ol_scr.at[pl.ds(lat, 1)], mod, nw_ref, win_scr, wout_scr,
                      m1_ref, csl_scr, ssl_scr, ta_scr, tb_scr, tr_scr, z_scr,
                      rows_done=lambda _, rows: result_copy(lat, rows.start // ROW_CHUNK).start(priority=1))

        @pl.when(lat == n_lat - 1)
        def _():
            for r in range(n_lat):
                for c in range(out_sem.shape[1]):
                    result_copy(r, c).wait()


def _fourier_body(x_ref, o_ref, mod, nw_ref, win_ref, wout_ref, m1_ref, cs_ref, ss_ref,
                  ta_scr, tb_scr, tr_scr, z_scr, rows_done=None):
    gate = mod[:, 2 * D_MODEL:]
    nw = nw_ref[...]
    n_seq, seq, _ = x_ref.shape
    n_chunks = seq // ROW_CHUNK
    lane = lax.broadcasted_iota(jnp.int32, (ROW_CHUNK, HALF_W), 1)
    for i in range(n_seq):
        for c in range(n_chunks):
            rows = slice(c * ROW_CHUNK, (c + 1) * ROW_CHUNK)
            srows = slice(i * seq + c * ROW_CHUNK, i * seq + (c + 1) * ROW_CHUNK)
            h = _mod_norm(x_ref[i, rows, :], nw, mod).astype(BF16)
            uz = _dot(h, win_ref[...])
            z_scr[srows, :] = uz[:, BRANCH:]
            u = uz[:, :BRANCH].astype(BF16)
            tr = jnp.zeros((ROW_CHUNK, HALF_W), F32)
            for g in range(N_GROUPS):
                t = _dot(u[:, g * GROUP_W:(g + 1) * GROUP_W], m1_ref[...])
                half = slice(g * HALF_W, (g + 1) * HALF_W)
                ta_scr[srows, half] = t[:, :HALF_W].astype(BF16)
                tb = t[:, HALF_W:]
                tb_scr[srows, half] = tb.astype(BF16)
                tr = jnp.where(lane == g, tb if g == 0 else pltpu.roll(tb, g, axis=1), tr)
            tr_scr[srows, :] = tr.astype(BF16)
    for i in range(n_seq):
        seq_rows = slice(i * seq, (i + 1) * seq)
        for c in range(n_chunks):
            rows = slice(c * ROW_CHUNK, (c + 1) * ROW_CHUNK)
            srows = slice(i * seq + c * ROW_CHUNK, i * seq + (c + 1) * ROW_CHUNK)
            cs = cs_ref[rows, :]
            p = _dot(cs, ta_scr[seq_rows, :])
            q = _dot(ss_ref[rows, :], tb_scr[seq_rows, :])
            r = _dot(cs, tr_scr[seq_rows, :])
            parts = []
            for g in range(N_GROUPS):
                half = slice(g * HALF_W, (g + 1) * HALF_W)
                pg, qg = p[:, half], q[:, half]
                rg = r if g == 0 else pltpu.roll(r, HALF_W - g, axis=1)
                parts.append(jnp.where(lane == 0, pg, pg - qg))
                parts.append(jnp.where(lane == 0, rg, pg + qg))
            y = jnp.concatenate(parts, axis=1)
            y = (y * jax.nn.silu(z_scr[srows, :])).astype(BF16)
            o_ref[i, rows, :] = x_ref[i, rows, :] + gate * _dot(y, wout_ref[...])
            if rows_done is not None:
                rows_done(i, rows)


def _dft_tables(seq):
    c = np.arange(GROUP_W)[:, None]
    k = np.arange(HALF_W)[None, :]
    cos_lo = np.cos(2.0 * np.pi * ((c * k) % GROUP_W) / GROUP_W)
    sin_lo = np.sin(2.0 * np.pi * ((c * k) % GROUP_W) / GROUP_W)
    sin_lo[:, 0] = np.cos(np.pi * c[:, 0])
    m1 = np.concatenate([cos_lo, sin_lo], axis=1) / np.sqrt(GROUP_W)
    n = np.arange(seq)
    ang = 2.0 * np.pi * ((n[:, None] * n[None, :]) % seq) / seq
    cs = np.cos(ang) / np.sqrt(seq)
    ss = np.sin(ang) / np.sqrt(seq)
    return m1.astype(np.float32), cs.astype(np.float32), ss.astype(np.float32)


def _const_spec(shape):
    return pl.BlockSpec(shape, lambda b: (0,) * len(shape))


def _fourier_layer(x_ctx, x_lat, nw, this_layer, next_layer):
    nb_ctx, seq_ctx, _ = x_ctx.shape
    nb_lat, seq_lat, _ = x_lat.shape
    n_seq = CTX_SEQS_PER_STEP
    assert nb_ctx % n_seq == 0
    n_prep = D_MODEL // HALF_W
    n_ctx = nb_ctx // n_seq
    assert D_MODEL % (n_ctx * LANES) == 0 and n_seq * seq_ctx <= seq_lat
    rows = D_MODEL // n_ctx
    assert rows == HALF_W
    c_ctx, c, w_mod0, b_mod0, w_in0, w_out0 = this_layer
    _, _, w_mod1, b_mod1, w_in1, w_out1 = next_layer
    n_lat = c.shape[0]
    n_qkvz = 2 * BRANCH + 2 * KV_W
    m1, csc, ssc = (jnp.asarray(t).astype(BF16) for t in _dft_tables(seq_ctx))
    _, csl, ssl = (jnp.asarray(t).astype(BF16) for t in _dft_tables(seq_lat))
    pm, pmh = (jnp.asarray(t).astype(BF16) for t in _perm_matrices())
    cc = c_ctx.reshape(1, D_MODEL)

    def prep_step(i):
        return jnp.minimum(i, n_prep - 1)

    def ctx_step(i):
        return jnp.clip(i - n_prep, 0, n_ctx - 1)

    def chunk_specs(step_fn, w_in_cols):
        return [pl.BlockSpec((1, HALF_W), lambda i: (0, step_fn(i))),
                pl.BlockSpec((n_lat, HALF_W), lambda i: (0, step_fn(i))),
                pl.BlockSpec((HALF_W, 3 * D_MODEL), lambda i: (step_fn(i), 0)),
                _const_spec((1, 3 * D_MODEL)),
                pl.BlockSpec((HALF_W, w_in_cols), lambda i: (step_fn(i), 0)),
                pl.BlockSpec((HALF_W, D_MODEL), lambda i: (step_fn(i), 0))]

    in_specs = [
        pl.BlockSpec((n_seq, seq_ctx, D_MODEL), lambda i: (ctx_step(i), 0, 0)),
        pl.BlockSpec(memory_space=pl.ANY),
        _const_spec((1, D_MODEL)),
        _const_spec((GROUP_W, GROUP_W)),
        _const_spec((seq_ctx, seq_ctx)),
        _const_spec((seq_ctx, seq_ctx)),
        pl.BlockSpec(memory_space=pl.ANY),
        pl.BlockSpec(memory_space=pl.ANY),
    ] + chunk_specs(prep_step, 2 * BRANCH) + [
        _const_spec((GROUP_W, GROUP_W)),
        _const_spec((2, HALF_W, HALF_W)),
    ] + chunk_specs(ctx_step, n_qkvz)
    out_specs = [
        pl.BlockSpec((n_seq, seq_ctx, D_MODEL), lambda i: (ctx_step(i), 0, 0)),
        pl.BlockSpec(memory_space=pl.ANY),
        MOD_SPEC,
        pl.BlockSpec((BRANCH + 2 * KV_W, rows), lambda i: (0, ctx_step(i))),
        pl.BlockSpec((rows, BRANCH), lambda i: (ctx_step(i), 0)),
        pl.BlockSpec((rows, D_MODEL), lambda i: (ctx_step(i), 0)),
    ]
    out_shape = [
        jax.ShapeDtypeStruct(x_ctx.shape, F32),
        jax.ShapeDtypeStruct(x_lat.shape, F32),
        jax.ShapeDtypeStruct((MOD_ROWS, 3 * D_MODEL), F32),
        jax.ShapeDtypeStruct((BRANCH + 2 * KV_W, D_MODEL), BF16),
        jax.ShapeDtypeStruct((D_MODEL, BRANCH), BF16),
        jax.ShapeDtypeStruct((BRANCH, D_MODEL), BF16),
    ]
    return pl.pallas_call(
        functools.partial(_fourier_layer_kernel, n_prep_steps=n_prep, n_ctx_steps=n_ctx),
        grid=(n_prep + n_ctx + nb_lat,),
        in_specs=in_specs,
        out_specs=out_specs,
        out_shape=out_shape,
        scratch_shapes=[
            pltpu.VMEM((MOD_ROWS, 3 * D_MODEL), F32),
            pltpu.VMEM((D_MODEL, 2 * BRANCH), BF16),
            pltpu.VMEM((BRANCH, D_MODEL), BF16),
            pltpu.VMEM((seq_lat, N_GROUPS * HALF_W), BF16),
            pltpu.VMEM((seq_lat, N_GROUPS * HALF_W), BF16),
            pltpu.VMEM((seq_lat, HALF_W), BF16),
            pltpu.VMEM((seq_lat, BRANCH), F32),
            pltpu.VMEM(x_lat.shape, F32),
            pltpu.VMEM((seq_lat, seq_lat), BF16),
            pltpu.VMEM((seq_lat, seq_lat), BF16),
            pltpu.SemaphoreType.DMA((nb_lat + 2,)),
            pltpu.VMEM(x_lat.shape, F32),
            pltpu.SemaphoreType.DMA((nb_lat, seq_lat // ROW_CHUNK)),
            pltpu.VMEM((MOD_ROWS, HALF_W), F32),
        ],
        compiler_params=pltpu.CompilerParams(
            dimension_semantics=("arbitrary",), vmem_limit_bytes=FOURIER_VMEM_LIMIT),
        name="fourier_layer",
    )(x_ctx, x_lat, nw, m1, csc, ssc, csl, ssl,
      cc, c, w_mod0, b_mod0.reshape(1, 3 * D_MODEL), w_in0, w_out0, pm, pmh,
      cc, c, w_mod1, b_mod1.reshape(1, 3 * D_MODEL), w_in1, w_out1)


def _head_weight_tile(w_ref, n_tokens):
    row = jnp.broadcast_to(w_ref[...], (HEAD_DIM, HEAD_DIM))
    ii = lax.broadcasted_iota(jnp.int32, (HEAD_DIM, HEAD_DIM), 0)
    jj = lax.broadcasted_iota(jnp.int32, (HEAD_DIM, HEAD_DIM), 1)
    col = jnp.sum(jnp.where(ii == jj, row, 0.0), axis=1, keepdims=True)
    return jnp.broadcast_to(col, (HEAD_DIM, n_tokens))


def _head_rms(t, w):
    return (t * lax.rsqrt(jnp.mean(t * t, axis=0, keepdims=True) + EPS)) * w


def _rope_t(t, cos, sin):
    half = HEAD_DIM // 2
    x1, x2 = t[:half], t[half:]
    return jnp.concatenate([x1 * cos - x2 * sin, x1 * sin + x2 * cos], axis=0)


def _head_scores(qn, g, keys, biases):
    zeros = jnp.zeros_like(qn)
    qz = jnp.concatenate([qn, zeros] if g % 2 == 0 else [zeros, qn], axis=0)
    blk = slice((g // 2) * LANES, (g // 2 + 1) * LANES)
    scores = []
    smax = None
    for k, bias in zip(keys, biases):
        s = _dot(k[:, blk], qz)
        if bias is not None:
            s = jnp.concatenate(
                [s[c * BLOCK:(c + 1) * BLOCK] if b is None else
                 s[c * BLOCK:(c + 1) * BLOCK] + jnp.concatenate([b] * (s.shape[1] // b.shape[1]), axis=1)
                 for c, b in enumerate(bias)], axis=0)
        cmax = jnp.max(s, axis=0, keepdims=True)
        smax = cmax if smax is None else jnp.maximum(smax, cmax)
        scores.append(s)
    return scores, smax


def _with_ones_rows(vt):
    return jnp.concatenate([vt, jnp.ones((ONES_ROWS, vt.shape[1]), vt.dtype)], axis=0)


def _head_probs(scored, sink2):
    scores, smax = scored
    m = jnp.maximum(smax, sink2)
    return jnp.concatenate([jnp.exp2(s - m).astype(BF16) for s in scores], axis=0), m


def _head_pv(probs, values_t, sink2):
    p, m = probs
    acc = _dot(values_t, p)
    den = acc[HEAD_DIM:HEAD_DIM + 1] + jnp.exp2(sink2 - m)
    return acc[:HEAD_DIM] * (1.0 / den)


def _attend_heads(n_units, stage, scores_fn, probs_fn, pv_fn, fillers=()):
    n_stages = n_units // stage
    pending = [scores_fn(u) for u in range(stage)]
    for g in range(n_stages):
        units = range(g * stage, (g + 1) * stage)
        if g < len(fillers):
            fillers[g]()
        nxt = [scores_fn(u) for u in range((g + 1) * stage, (g + 2) * stage)] if g + 1 < n_stages else None
        probs = [probs_fn(u, sc) for u, sc in zip(units, pending)]
        for u, pr in zip(units, probs):
            pv_fn(u, pr)
        pending = nxt


def _gate_out(x, o, z, gate, wout):
    y = (o * jax.nn.silu(z)).astype(BF16)
    return x + gate * _dot(y, wout)


def _attn_ctx_body(sink_ref, x_ref, mod, nw_ref, wqkvt_ref, wz_ref, wout_ref, qw_ref, kw_ref,
                   o_ref, kto_ref, vto_ref, qkvt_scr, z_scr, ot_scr):
    seq = x_ref.shape[1]
    n_seq = x_ref.shape[0]
    gate = mod[:, 2 * D_MODEL:]
    kw = _head_weight_tile(kw_ref, seq)
    qw = _head_weight_tile(qw_ref, seq)

    def project(i):
        h = _mod_norm(x_ref[i], nw_ref[...], mod).astype(BF16)
        qkvt_scr[i] = _dot_nt(wqkvt_ref[...], h)
        z_scr[i * seq:(i + 1) * seq, :] = _dot(h, wz_ref[...])

    def keys_values(i):
        knt = jnp.concatenate(
            [_head_rms(qkvt_scr[i, BRANCH + g * HEAD_DIM:BRANCH + (g + 1) * HEAD_DIM, :], kw)
             for g in range(N_KV)], axis=0)
        kto_ref[i] = knt
        vtf = qkvt_scr[i, BRANCH + KV_W:, :]
        vto_ref[i] = vtf
        vt = vtf.astype(BF16)
        return (knt.T.astype(BF16),
                [_with_ones_rows(vt[g * HEAD_DIM:(g + 1) * HEAD_DIM]) for g in range(N_KV)])

    def output(i):
        o_ref[i] = _gate_out(x_ref[i], ot_scr[i].T, z_scr[i * seq:(i + 1) * seq, :], gate, wout_ref[...])

    for i in range(n_seq):
        project(i)
    kv = [keys_values(i) for i in range(n_seq)]

    def scores_fn(u):
        i, hd = divmod(u, N_HEADS)
        t = qkvt_scr[i, hd * HEAD_DIM:(hd + 1) * HEAD_DIM, :]
        qn = (_head_rms(t, qw) * (HEAD_DIM ** -0.5 * LOG2E)).astype(BF16)
        return _head_scores(qn, hd // GQA, [kv[i][0]], [None])

    def probs_fn(u, sc):
        return _head_probs(sc, sink_ref[u % N_HEADS] * LOG2E)

    def pv_fn(u, pr):
        i, hd = divmod(u, N_HEADS)
        ot_scr[i, hd * HEAD_DIM:(hd + 1) * HEAD_DIM, :] = _head_pv(
            pr, kv[i][1][hd // GQA], sink_ref[hd] * LOG2E)

    fillers = [lambda: None] + [functools.partial(output, i) for i in range(n_seq - 1)]
    _attend_heads(n_seq * N_HEADS, N_HEADS, scores_fn, probs_fn, pv_fn, fillers)
    output(n_seq - 1)


def _attn_lat_body(sink_ref, x_ref, mod, nw_ref, wqkvt_ref, wz_ref, wout_ref, qw_ref, kw_ref,
                   cos_ref, sin_ref, ck_ref, cv_ref, o_ref,
                   q_scr, z_scr, ot_scr, k_scr, vt_scr, block_done=None):
    seq = x_ref.shape[1]
    gate = mod[:, 2 * D_MODEL:]
    nw = nw_ref[...]
    qw = _head_weight_tile(qw_ref, Q_BLOCK)
    kw = _head_weight_tile(kw_ref, Q_BLOCK)
    n_blocks = seq // Q_BLOCK
    kv_blocks = seq // BLOCK
    k_scr[0:BLOCK, :] = jnp.zeros((BLOCK, KV_W), BF16)
    k_scr[BLOCK + seq:2 * BLOCK + seq, :] = jnp.zeros((BLOCK, KV_W), BF16)
    vt_scr[0] = jnp.zeros((KV_W, BLOCK), BF16)
    vt_scr[kv_blocks + 1] = jnp.zeros((KV_W, BLOCK), BF16)

    def project(c):
        rows = slice(c * Q_BLOCK, (c + 1) * Q_BLOCK)
        h = _mod_norm(x_ref[0, rows, :], nw, mod).astype(BF16)
        z_scr[rows, :] = _dot(h, wz_ref[...])
        qkvt = _dot_nt(wqkvt_ref[...], h)
        cos = cos_ref[:, rows]
        sin = sin_ref[:, rows]
        for hd in range(N_HEADS):
            hr = slice(hd * HEAD_DIM, (hd + 1) * HEAD_DIM)
            t = _rope_t(_head_rms(qkvt[hr], qw), cos, sin)
            t = (t * (HEAD_DIM ** -0.5 * LOG2E)).astype(BF16)
            for j in range(HALVES):
                q_scr[c * HALVES + j, hd // 2, :, (hd % 2) * BLOCK:(hd % 2 + 1) * BLOCK] = (
                    t[:, j * BLOCK:(j + 1) * BLOCK])
        knt = jnp.concatenate(
            [_rope_t(_head_rms(qkvt[BRANCH + g * HEAD_DIM:BRANCH + (g + 1) * HEAD_DIM], kw), cos, sin)
             for g in range(N_KV)], axis=0)
        k_scr[BLOCK + c * Q_BLOCK:BLOCK + (c + 1) * Q_BLOCK, :] = knt.T.astype(BF16)
        vt = qkvt[BRANCH + KV_W:].astype(BF16)
        for j in range(HALVES):
            vt_scr[1 + c * HALVES + j] = vt[:, j * BLOCK:(j + 1) * BLOCK]

    ckb = ck_ref[0].T.astype(BF16)
    cvt = cv_ref[0].astype(BF16)

    win_len = 3 * BLOCK
    n_pairs = N_HEADS // 2
    kj = lax.broadcasted_iota(jnp.int32, (BLOCK, BLOCK), 0)
    qi = lax.broadcasted_iota(jnp.int32, (BLOCK, BLOCK), 1)
    first_head = lax.broadcasted_iota(jnp.int32, (1, 2 * BLOCK), 1) < BLOCK
    assert WINDOW >= BLOCK - 1

    def band_bias(c):
        rel = kj + (c - 1) * BLOCK - qi
        return jnp.where((rel >= -WINDOW) & (rel <= WINDOW), 0.0, NEG_INF)

    band_before, band_after = band_bias(0), band_bias(2)

    def half_operands(hb):
        r0 = pl.multiple_of(hb * BLOCK, BLOCK)
        valid = (jnp.where(hb > 0, band_before, NEG_INF), None,
                 jnp.where(hb < kv_blocks - 1, band_after, NEG_INF))
        kwin = k_scr[pl.ds(r0, win_len), :]
        vall = jnp.concatenate([vt_scr[hb + j] for j in range(win_len // BLOCK)] + [cvt], axis=1)
        valls = [_with_ones_rows(vall[g * HEAD_DIM:(g + 1) * HEAD_DIM]) for g in range(N_KV)]
        return valid, kwin, valls

    def pair_sink2(pair):
        return jnp.where(first_head, sink_ref[2 * pair], sink_ref[2 * pair + 1]) * LOG2E

    def attend(n, carry):
        halves = [half_operands(n * HALVES + j) for j in range(HALVES)]

        def scores_fn(u):
            j, pair = divmod(u, n_pairs)
            valid, kwin, _ = halves[j]
            return _head_scores(q_scr[n * HALVES + j, pair], pair // (GQA // 2), [kwin, ckb], [valid, None])

        def probs_fn(u, sc):
            return _head_probs(sc, pair_sink2(u % n_pairs))

        def pv_fn(u, pr):
            j, pair = divmod(u, n_pairs)
            o = _head_pv(pr, halves[j][2][pair // (GQA // 2)], pair_sink2(pair))
            for i in range(2):
                hd = 2 * pair + i
                ot_scr[n, hd * HEAD_DIM:(hd + 1) * HEAD_DIM, j * BLOCK:(j + 1) * BLOCK] = (
                    o[:, i * BLOCK:(i + 1) * BLOCK])

        _attend_heads(HALVES * n_pairs, GQA // 2, scores_fn, probs_fn, pv_fn)
        return carry

    for c in range(n_blocks):
        project(c)
    assert n_blocks % 2 == 0
    lax.fori_loop(0, n_blocks // 2, lambda n, carry: attend(2 * n + 1, attend(2 * n, carry)), 0)
    for c in range(n_blocks):
        rows = slice(c * Q_BLOCK, (c + 1) * Q_BLOCK)
        o_ref[0, rows, :] = _gate_out(x_ref[0, rows, :], ot_scr[c].T, z_scr[rows, :], gate, wout_ref[...])
        if block_done is not None:
            block_done(c)


def _rope_tables_t(seq):
    pos = np.arange(seq)
    n_freq = HEAD_DIM // 4
    inv = ROPE_THETA ** (-np.arange(n_freq, dtype=np.float64) / n_freq)
    ang = np.concatenate([(pos // GRID_W)[:, None] * inv, (pos % GRID_W)[:, None] * inv], axis=-1)
    return np.cos(ang).T.astype(np.float32), np.sin(ang).T.astype(np.float32)


def _attn_layer_kernel(sink_ref, xc_ref, xl_ref, mod_ref, nw_ref, wqkvt_ref, wz_ref, wout_ref, qw_ref, kw_ref,
                       cos_ref, sin_ref, ck_ref, cv_ref, oc_ref, kto_ref, vto_ref, ol_hbm,
                       qkvt_scr, q_scr, z_scr, ot_scr, k_scr, vt_scr, ol_scr, out_sem, *, n_ctx_steps):
    step = pl.program_id(0)
    n_lat = ol_scr.shape[0]
    lat = step - n_ctx_steps

    def result_copy(r, c):
        rows = pl.ds(c * Q_BLOCK, Q_BLOCK)
        return pltpu.make_async_copy(ol_scr.at[r, rows], ol_hbm.at[r, rows], out_sem.at[r, c])

    @pl.when(step < n_ctx_steps)
    def _():
        _attn_ctx_body(sink_ref, xc_ref, mod_ref[0:1, :], nw_ref, wqkvt_ref, wz_ref, wout_ref, qw_ref, kw_ref,
                       oc_ref, kto_ref, vto_ref, qkvt_scr, z_scr, ot_scr)

    @pl.when(step >= n_ctx_steps)
    def _():
        mod = mod_ref[pl.ds(1 + lat, 1), :]
        _attn_lat_body(sink_ref, xl_ref, mod, nw_ref, wqkvt_ref, wz_ref, wout_ref, qw_ref, kw_ref,
                       cos_ref, sin_ref, ck_ref, cv_ref, ol_scr.at[pl.ds(lat, 1)],
                       q_scr, z_scr, ot_scr, k_scr, vt_scr,
                       block_done=lambda c: result_copy(lat, c).start(priority=1))

        @pl.when(lat == n_lat - 1)
        def _():
            for r in range(n_lat):
                for c in range(out_sem.shape[1]):
                    result_copy(r, c).wait()


def _attn_layer(x_ctx, x_lat, mod, nw, wqkvt, wz, wout, qw, kw, sink, ckt, cvt):
    nb_ctx, seq_ctx, _ = x_ctx.shape
    nb_lat, seq_lat, _ = x_lat.shape
    past = ckt.shape[2]
    n_seq = CTX_SEQS_PER_STEP
    assert nb_ctx % n_seq == 0 and seq_ctx == Q_BLOCK and n_seq <= seq_lat // Q_BLOCK
    n_ctx = nb_ctx // n_seq
    cos, sin = (jnp.asarray(t) for t in _rope_tables_t(seq_lat))

    def ctx_step(i):
        return jnp.minimum(i, n_ctx - 1)

    def lat_step(i):
        return jnp.maximum(i - n_ctx, 0)

    return pl.pallas_call(
        functools.partial(_attn_layer_kernel, n_ctx_steps=n_ctx),
        grid=(n_ctx + nb_lat,),
        in_specs=[
            pl.BlockSpec(memory_space=pltpu.SMEM),
            pl.BlockSpec((n_seq, seq_ctx, D_MODEL), lambda i: (ctx_step(i), 0, 0)),
            pl.BlockSpec((1, seq_lat, D_MODEL), lambda i: (lat_step(i), 0, 0)),
            MOD_SPEC,
            _const_spec((1, D_MODEL)),
            _const_spec((BRANCH + 2 * KV_W, D_MODEL)),
            _const_spec((D_MODEL, BRANCH)),
            _const_spec((BRANCH, D_MODEL)),
            _const_spec((1, HEAD_DIM)),
            _const_spec((1, HEAD_DIM)),
            _const_spec((HEAD_DIM // 2, seq_lat)),
            _const_spec((HEAD_DIM // 2, seq_lat)),
            pl.BlockSpec((1, KV_W, past), lambda i: (lat_step(i), 0, 0)),
            pl.BlockSpec((1, KV_W, past), lambda i: (lat_step(i), 0, 0)),
        ],
        out_specs=[
            pl.BlockSpec((n_seq, seq_ctx, D_MODEL), lambda i: (ctx_step(i), 0, 0)),
            pl.BlockSpec((n_seq, KV_W, seq_ctx), lambda i: (ctx_step(i), 0, 0)),
            pl.BlockSpec((n_seq, KV_W, seq_ctx), lambda i: (ctx_step(i), 0, 0)),
            pl.BlockSpec(memory_space=pl.ANY),
        ],
        out_shape=[
            jax.ShapeDtypeStruct(x_ctx.shape, F32),
            jax.ShapeDtypeStruct((nb_ctx, KV_W, seq_ctx), F32),
            jax.ShapeDtypeStruct((nb_ctx, KV_W, seq_ctx), F32),
            jax.ShapeDtypeStruct(x_lat.shape, F32),
        ],
        scratch_shapes=[
            pltpu.VMEM((n_seq, BRANCH + 2 * KV_W, seq_ctx), F32),
            pltpu.VMEM((seq_lat // BLOCK, N_HEADS // 2, HEAD_DIM, 2 * BLOCK), BF16),
            pltpu.VMEM((seq_lat, BRANCH), F32),
            pltpu.VMEM((seq_lat // Q_BLOCK, BRANCH, Q_BLOCK), F32),
            pltpu.VMEM((seq_lat + 2 * BLOCK, KV_W), BF16),
            pltpu.VMEM((seq_lat // BLOCK + 2, KV_W, BLOCK), BF16),
            pltpu.VMEM(x_lat.shape, F32),
            pltpu.SemaphoreType.DMA((nb_lat, seq_lat // Q_BLOCK)),
        ],
        compiler_params=pltpu.CompilerParams(
            dimension_semantics=("arbitrary",), vmem_limit_bytes=VMEM_LIMIT),
        name="attn_layer",
    )(sink, x_ctx, x_lat, mod, nw, wqkvt, wz, wout, qw, kw, cos, sin, ckt, cvt)


def kernel(x_prompt, x_sample, cache_k_l1, cache_v_l1, c, c_ctx, norm_w_l0, w_mod_l0, b_mod_l0,
           w_in_l0, w_out_l0, norm_w_l1, w_mod_l1, b_mod_l1, w_in_l1, q_norm_w_l1, k_norm_w_l1,
           sink_l1, w_out_l1):
    nb_ctx, seq_ctx, _ = x_prompt.shape
    nb_lat = x_sample.shape[0]
    past = cache_k_l1.shape[1]
    assert 1 + nb_lat <= MOD_ROWS
    nw0 = norm_w_l0.reshape(1, D_MODEL)
    nw1 = norm_w_l1.reshape(1, D_MODEL)
    qw = q_norm_w_l1.reshape(1, HEAD_DIM)
    kw = k_norm_w_l1.reshape(1, HEAD_DIM)

    xp, xs, mod1, wqkvt1, wz1, wout1 = _fourier_layer(
        x_prompt, x_sample, nw0,
        this_layer=(c_ctx, c, w_mod_l0, b_mod_l0, w_in_l0, w_out_l0),
        next_layer=(c_ctx, c, w_mod_l1, b_mod_l1, w_in_l1, w_out_l1))

    def to_feature_major(t):
        return jnp.transpose(t, (0, 2, 3, 1)).reshape(t.shape[0], KV_W, t.shape[1])

    def from_feature_major(t):
        return jnp.transpose(t.reshape(t.shape[0], N_KV, HEAD_DIM, t.shape[2]), (0, 3, 1, 2))

    xp, new_kt, new_vt, xs = _attn_layer(xp, xs, mod1, nw1, wqkvt1, wz1, wout1, qw, kw, sink_l1,
                                         to_feature_major(cache_k_l1), to_feature_major(cache_v_l1))
    return (xp, xs, from_feature_major(new_kt), from_feature_major(new_vt))
```
